```python
import jax, jax.numpy as jnp
from jax import lax
import numpy as np

D_MODEL = 1024
BATCH = 16
SEQ = 256
DEPTH = 2
DEC_BATCH = 4
DEC_SEQ = 1024
PAST_LEN = 256

GRID_W = 64
HEAD_DIM = 64
POOL_GROUPS = 4
POOL_CH = 64
POOL_WIDTH = POOL_GROUPS * POOL_CH
POOL_WINDOWS = (2, 4, 8, 16)
H_B = 6
KV_B = 2
H_C = 6
WIN_R = 8
WIN_C = 16
Q_BLOCK = 128
ROPE_THETA = 10000.0
MIX_WIDTH = POOL_WIDTH + (H_B + H_C) * HEAD_DIM
PROJ_SIZES = (POOL_WIDTH, H_B * HEAD_DIM, KV_B * HEAD_DIM, KV_B * HEAD_DIM,
              H_C * HEAD_DIM, H_C * HEAD_DIM, H_C * HEAD_DIM)
PROJ_WIDTH = int(sum(PROJ_SIZES))
PROJ_SPLITS = [int(s) for s in np.cumsum(PROJ_SIZES)[:-1]]
N_GROUPS = 4
EXPERTS_PER_GROUP = 8
N_EXPERTS = N_GROUPS * EXPERTS_PER_GROUP
TOP_K = 2
D_EXPERT = 256
ALPHA = (2 * DEPTH) ** 0.25
BETA = (8 * DEPTH) ** -0.25
LN_EPS = 1e-6
RMS_EPS = 1e-6
NEG = -1e30

kernel_name = 'hybrid_pool_gqa_natten_hmoe_step'


def layer_norm(x, g, b):
    xf = x.astype(jnp.float32)
    mu = jnp.mean(xf, axis=-1, keepdims=True)
    var = jnp.mean(jnp.square(xf - mu), axis=-1, keepdims=True)
    return ((xf - mu) * lax.rsqrt(var + LN_EPS) * g + b).astype(x.dtype)


def rms_norm(x, w):
    xf = x.astype(jnp.float32)
    return (xf * lax.rsqrt(jnp.mean(jnp.square(xf), axis=-1, keepdims=True) + RMS_EPS) * w).astype(x.dtype)


def adaln(cvec, w, b):
    m = (jax.nn.silu(cvec) @ w + b)[..., None, :]
    return jnp.split(m, 6, axis=-1)


def rope_2d(x):
    L = x.shape[1]
    t = jnp.arange(L)
    row, col = t // GRID_W, t % GRID_W
    half = HEAD_DIM // 2
    nf = half // 2
    inv = ROPE_THETA ** (-jnp.arange(nf, dtype=jnp.float32) / nf)

    def rot(xp, pos):
        ang = pos.astype(jnp.float32)[:, None] * inv[None, :]
        cos, sin = jnp.cos(ang)[None, :, None, :], jnp.sin(ang)[None, :, None, :]
        x1, x2 = xp[..., :nf], xp[..., nf:]
        return jnp.concatenate([x1 * cos - x2 * sin, x1 * sin + x2 * cos], axis=-1)

    xf = x.astype(jnp.float32)
    return jnp.concatenate([rot(xf[..., :half], row), rot(xf[..., half:], col)], axis=-1).astype(x.dtype)


def split_proj(h, w_in):
    B, L, _ = h.shape
    a, qb, kb, vb, qc, kc, vc = jnp.split(h @ w_in, PROJ_SPLITS, axis=-1)
    r = lambda t, n: t.reshape(B, L, n, HEAD_DIM)
    return a, r(qb, H_B), r(kb, KV_B), r(vb, KV_B), r(qc, H_C), r(kc, H_C), r(vc, H_C)


def pool_mixer(a, pool_w, pool_scale):
    B, L, _ = a.shape
    ag = a.reshape(B, L, POOL_GROUPS, POOL_CH).astype(jnp.float32)
    cs = jnp.concatenate([jnp.zeros((B, 1, POOL_GROUPS, POOL_CH), jnp.float32),
                          jnp.cumsum(ag, axis=1)], axis=1)
    t = jnp.arange(L)
    outs = []
    for g, win in enumerate(POOL_WINDOWS):
        lo = jnp.clip(t - win // 2, 0, L)
        hi = jnp.clip(t + win // 2, 0, L)
        csg = cs[:, :, g]
        cnt = (hi - lo).astype(jnp.float32)[None, :, None]
        outs.append((csg[:, hi] - csg[:, lo]) / cnt - ag[:, :, g])
    pooled = jnp.stack(outs, axis=2).astype(a.dtype)
    y = jnp.einsum('blgc,gcd->blgd', pooled, pool_w)
    return y.reshape(B, L, POOL_WIDTH) * pool_scale


def blocked_attention(q, k, v):
    B, Lq, H, D = q.shape
    KVH = k.shape[2]
    G = H // KVH
    nb = Lq // Q_BLOCK
    scale = D ** -0.5
    qb = q.reshape(B, nb, Q_BLOCK, KVH, G, D).transpose(1, 0, 2, 3, 4, 5)

    def one(qblk):
        s = jnp.einsum('bqhgd,bkhd->bhgqk', qblk, k).astype(jnp.float32) * scale
        p = jax.nn.softmax(s, axis=-1).astype(v.dtype)
        return jnp.einsum('bhgqk,bkhd->bqhgd', p, v)

    o = lax.map(one, qb)
    return o.transpose(1, 0, 2, 3, 4, 5).reshape(B, Lq, H * D)


def neighborhood_attention(q, k, v, k_ctx, v_ctx, rpb):
    B, L, H, D = q.shape
    rows = L // GRID_W
    wr = min(WIN_R, rows)
    scale = D ** -0.5
    qg = q.reshape(B, rows, GRID_W, H, D)
    kg = k.reshape(B, rows, GRID_W, H, D)
    vg = v.reshape(B, rows, GRID_W, H, D)
    r = jnp.arange(rows)
    rs = jnp.clip(r - wr // 2, 0, rows - wr)
    ridx = rs[:, None] + jnp.arange(wr)[None, :]
    kr = kg[:, ridx]
    vr = vg[:, ridx]
    cidx = jnp.arange(GRID_W)
    cs = jnp.clip(cidx - WIN_C // 2, 0, GRID_W - WIN_C)
    col_ok = (cidx[None, :] >= cs[:, None]) & (cidx[None, :] < cs[:, None] + WIN_C)
    dr = ridx - r[:, None]
    dc = jnp.clip(cidx[None, :] - cidx[:, None], -(WIN_C - 1), WIN_C - 1)
    bias = rpb[:, dr[:, None, :, None] + (WIN_R - 1), dc[None, :, None, :] + (WIN_C - 1)]
    s_loc = jnp.einsum('brqhd,brjwhd->bhrqjw', qg, kr).astype(jnp.float32) * scale + bias[None].astype(jnp.float32)
    s_loc = jnp.where(col_ok[:, None, :], s_loc, NEG)
    s_ctx = jnp.einsum('brqhd,bkhd->bhrqk', qg, k_ctx).astype(jnp.float32) * scale
    P = k_ctx.shape[1]
    s = jnp.concatenate([s_loc.reshape(B, H, rows, GRID_W, wr * GRID_W), s_ctx], axis=-1)
    p = jax.nn.softmax(s, axis=-1).astype(v.dtype)
    p_loc = p[..., :wr * GRID_W].reshape(B, H, rows, GRID_W, wr, GRID_W)
    p_ctx = p[..., wr * GRID_W:]
    o = jnp.einsum('bhrqjw,brjwhd->brqhd', p_loc, vr) + jnp.einsum('bhrqk,bkhd->brqhd', p_ctx, v_ctx)
    return o.reshape(B, L, H * D)


def hier_moe(h, p):
    B, L, D = h.shape
    t = h.reshape(-1, D)
    lg = (t @ p['router_g'] + p['router_g_b']).astype(jnp.float32)
    pg = jax.nn.softmax(lg, axis=-1)
    _, gsel = lax.top_k(lg, 1)
    pg_sel = jnp.take_along_axis(pg, gsel, axis=1)
    le = (t @ p['router_e'] + p['router_e_b']).astype(jnp.float32).reshape(-1, N_GROUPS, EXPERTS_PER_GROUP)
    le_sel = jnp.take_along_axis(le, gsel[:, :, None], axis=1)[:, 0]
    ev, ei = lax.top_k(le_sel, TOP_K)
    wts = jax.nn.softmax(ev, axis=-1) * pg_sel
    eidx = gsel * EXPERTS_PER_GROUP + ei
    gate = jnp.sum(jax.nn.one_hot(eidx, N_EXPERTS, dtype=jnp.float32) * wts[..., None], axis=1).astype(h.dtype)
    hg = jnp.einsum('td,edf->tef', t, p['w_gate'])
    hu = jnp.einsum('td,edf->tef', t, p['w_up'])
    act = jax.nn.silu(hg) * hu * gate[:, :, None]
    return jnp.einsum('tef,efd->td', act, p['w_down']).reshape(B, L, D)


def residual_tail(x, ya, yb, yc, g1, sh2, sc2, g2, p):
    o = jnp.concatenate([ya, yb, yc], axis=-1) @ p['w_out']
    x = layer_norm(ALPHA * x + g1 * o, p['ln1_g'], p['ln1_b'])
    h = x * (1 + sc2) + sh2
    return layer_norm(ALPHA * x + g2 * hier_moe(h, p), p['ln2_g'], p['ln2_b'])


def context_layer(x, c_ctx, p):
    sh1, sc1, g1, sh2, sc2, g2 = adaln(c_ctx, p['w_ada'], p['b_ada'])
    h = x * (1 + sc1) + sh1
    a, qb, kb, vb, qc, kc, vc = split_proj(h, p['w_in'])
    ya = pool_mixer(a, p['pool_w'], p['pool_scale'])
    qb = rms_norm(qb, p['q_norm'])
    kb = rms_norm(kb, p['k_norm'])
    yb = blocked_attention(qb, kb, vb)
    yc = blocked_attention(qc, kc, vc)
    x = residual_tail(x, ya, yb, yc, g1, sh2, sc2, g2, p)
    return x, kb, vb, kc, vc


def latent_layer(x, c, ck_b, cv_b, ck_c, cv_c, p):
    sh1, sc1, g1, sh2, sc2, g2 = adaln(c, p['w_ada'], p['b_ada'])
    h = x * (1 + sc1) + sh1
    a, qb, kb, vb, qc, kc, vc = split_proj(h, p['w_in'])
    ya = pool_mixer(a, p['pool_w'], p['pool_scale'])
    qb = rope_2d(rms_norm(qb, p['q_norm']))
    kb = rope_2d(rms_norm(kb, p['k_norm']))
    yb = blocked_attention(qb, jnp.concatenate([ck_b, kb], axis=1), jnp.concatenate([cv_b, vb], axis=1))
    yc = neighborhood_attention(qc, kc, vc, ck_c, cv_c, p['rpb'])
    return residual_tail(x, ya, yb, yc, g1, sh2, sc2, g2, p)


def setup_inputs(seed: int = 0) -> dict:
    key = jax.random.key(seed)
    ks = jax.random.split(key, 28)
    f32 = jnp.float32
    nrm = lambda k, shape, s: jax.random.normal(k, shape, f32) * s
    D = D_MODEL
    return {
        'x_prompt': nrm(ks[0], (BATCH, SEQ, D), 1.0),
        'x_sample': nrm(ks[1], (DEC_BATCH, DEC_SEQ, D), 1.0),
        'cache_b_k': nrm(ks[2], (DEC_BATCH, DEPTH, PAST_LEN, KV_B, HEAD_DIM), 1.0),
        'cache_b_v': nrm(ks[3], (DEC_BATCH, DEPTH, PAST_LEN, KV_B, HEAD_DIM), 1.0),
        'cache_c_k': nrm(ks[4], (DEC_BATCH, DEPTH, PAST_LEN, H_C, HEAD_DIM), 1.0),
        'cache_c_v': nrm(ks[5], (DEC_BATCH, DEPTH, PAST_LEN, H_C, HEAD_DIM), 1.0),
        'c': nrm(ks[6], (DEC_BATCH, D), 1.0),
        'c_ctx': nrm(ks[7], (D,), 1.0),
        'w_ada': nrm(ks[8], (DEPTH, D, 6 * D), 0.5 * D ** -0.5),
        'b_ada': nrm(ks[9], (DEPTH, 6 * D), 0.02),
        'w_in': nrm(ks[10], (DEPTH, D, PROJ_WIDTH), D ** -0.5),
        'w_out': nrm(ks[11], (DEPTH, MIX_WIDTH, D), BETA * MIX_WIDTH ** -0.5),
        'pool_w': nrm(ks[12], (DEPTH, POOL_GROUPS, POOL_CH, POOL_CH), POOL_CH ** -0.5),
        'pool_scale': 1.0 + nrm(ks[13], (DEPTH, POOL_WIDTH), 0.02),
        'q_norm': 1.0 + nrm(ks[14], (DEPTH, HEAD_DIM), 0.02),
        'k_norm': 1.0 + nrm(ks[15], (DEPTH, HEAD_DIM), 0.02),
        'rpb': nrm(ks[16], (DEPTH, H_C, 2 * WIN_R - 1, 2 * WIN_C - 1), 0.1),
        'ln1_g': 1.0 + nrm(ks[17], (DEPTH, D), 0.02),
        'ln1_b': nrm(ks[18], (DEPTH, D), 0.02),
        'ln2_g': 1.0 + nrm(ks[19], (DEPTH, D), 0.02),
        'ln2_b': nrm(ks[20], (DEPTH, D), 0.02),
        'router_g': nrm(ks[21], (DEPTH, D, N_GROUPS), D ** -0.5),
        'router_g_b': nrm(ks[22], (DEPTH, N_GROUPS), 0.01),
        'router_e': nrm(ks[23], (DEPTH, D, N_EXPERTS), D ** -0.5),
        'router_e_b': nrm(ks[24], (DEPTH, N_EXPERTS), 0.01),
        'w_gate': nrm(ks[25], (DEPTH, N_EXPERTS, D, D_EXPERT), D ** -0.5),
        'w_up': nrm(ks[26], (DEPTH, N_EXPERTS, D, D_EXPERT), D ** -0.5),
        'w_down': nrm(ks[27], (DEPTH, N_EXPERTS, D_EXPERT, D), BETA * D_EXPERT ** -0.5),
    }


def reference(x_prompt, x_sample, cache_b_k, cache_b_v, cache_c_k, cache_c_v, c, c_ctx,
              w_ada, b_ada, w_in, w_out, pool_w, pool_scale, q_norm, k_norm, rpb,
              ln1_g, ln1_b, ln2_g, ln2_b, router_g, router_g_b, router_e, router_e_b,
              w_gate, w_up, w_down):
    xp = x_prompt
    xs = x_sample
    kb_list, vb_list, kc_list, vc_list = [], [], [], []
    for l in range(DEPTH):
        p = {'w_ada': w_ada[l], 'b_ada': b_ada[l], 'w_in': w_in[l], 'w_out': w_out[l],
             'pool_w': pool_w[l], 'pool_scale': pool_scale[l], 'q_norm': q_norm[l], 'k_norm': k_norm[l],
             'rpb': rpb[l], 'ln1_g': ln1_g[l], 'ln1_b': ln1_b[l], 'ln2_g': ln2_g[l], 'ln2_b': ln2_b[l],
             'router_g': router_g[l], 'router_g_b': router_g_b[l], 'router_e': router_e[l],
             'router_e_b': router_e_b[l], 'w_gate': w_gate[l], 'w_up': w_up[l], 'w_down': w_down[l]}
        xp, kb, vb, kc, vc = context_layer(xp, c_ctx, p)
        kb_list.append(kb)
        vb_list.append(vb)
        kc_list.append(kc)
        vc_list.append(vc)
        xs = latent_layer(xs, c, cache_b_k[:, l], cache_b_v[:, l], cache_c_k[:, l], cache_c_v[:, l], p)
    new_b_k = jnp.stack(kb_list, axis=1)
    new_b_v = jnp.stack(vb_list, axis=1)
    new_c_k = jnp.stack(kc_list, axis=1)
    new_c_v = jnp.stack(vc_list, axis=1)
    return (xp, xs, new_b_k, new_b_v, new_c_k, new_c_v)
```

```python
import functools

import numpy as np
import jax
import jax.numpy as jnp
from jax import lax
from jax.experimental import pallas as pl
from jax.experimental.pallas import tpu as pltpu

F32 = jnp.float32
BF16 = jnp.bfloat16

D_MODEL = 1024
BATCH = 16
SEQ = 256
DEPTH = 2
DEC_BATCH = 4
DEC_SEQ = 1024
PAST_LEN = 256
GRID_W = 64
GRID_ROWS = DEC_SEQ // GRID_W
HEAD_DIM = 64
POOL_WIDTH = 256
POOL_WINDOWS = (2, 4, 8, 16)
H_B = 6
KV_B = 2
H_C = 6
WIN_R = 8
WIN_C = 16
ROPE_THETA = 10000.0
QB_W = H_B * HEAD_DIM
KB_W = KV_B * HEAD_DIM
C_W = H_C * HEAD_DIM
PROJ_WIDTH = 2048
N_GROUPS = 4
EXPERTS_PER_GROUP = 8
N_EXPERTS = 32
D_EXPERT = 256
ALPHA = (2 * DEPTH) ** 0.25
LN_EPS = 1e-6
RMS_EPS = 1e-6
NEG = -1e30
ATTN_SCALE = HEAD_DIM ** -0.5

T_CTX = BATCH * SEQ
T_LAT = DEC_BATCH * DEC_SEQ
T_ALL = T_CTX + T_LAT

LANES = 128
ROUTER_LANES = 128
EXPERT_LANE0 = N_GROUPS
MOD_ROWS = 8
CTX_MOD_ROW = DEC_BATCH

TT = 512
TP = 1024
TM = 1024
HALF = DEC_SEQ // 2
NAT_KEYS = 12 * GRID_W
VMEM_LIMIT = 56 * 1024 * 1024


def _dot(a, b):
    return jnp.dot(a, b, preferred_element_type=F32)


def _dot_nt(a, b):
    return lax.dot_general(a, b, (((1,), (1,)), ((), ())), preferred_element_type=F32)


def _split_bf16(x):
    hi = x.astype(BF16)
    lo = (x - hi.astype(F32)).astype(BF16)
    return hi, lo


def _layer_norm(y, g, b):
    mu = jnp.mean(y, axis=-1, keepdims=True)
    var = jnp.mean(jnp.square(y - mu), axis=-1, keepdims=True)
    return (y - mu) * lax.rsqrt(var + LN_EPS) * g + b


def _adaln_kernel(c_ref, w_ref, b_ref, o_ref):
    c = c_ref[...]
    s = (c * jax.nn.sigmoid(c)).astype(BF16)
    o_ref[0] = _dot(s, w_ref[0].astype(BF16)) + b_ref[0]


def _adaln(c8, w_ada, b_ada):
    tn = 1536
    n = w_ada.shape[-1]
    return pl.pallas_call(
        _adaln_kernel,
        grid=(DEPTH, n // tn),
        in_specs=[
            pl.BlockSpec((MOD_ROWS, D_MODEL), lambda l, j: (0, 0)),
            pl.BlockSpec((1, D_MODEL, tn), lambda l, j: (l, 0, j)),
            pl.BlockSpec((1, 1, tn), lambda l, j: (l, 0, j)),
        ],
        out_specs=pl.BlockSpec((1, MOD_ROWS, tn), lambda l, j: (l, 0, j)),
        out_shape=jax.ShapeDtypeStruct((DEPTH, MOD_ROWS, n), F32),
        compiler_params=pltpu.CompilerParams(
            dimension_semantics=("arbitrary", "arbitrary"), vmem_limit_bytes=VMEM_LIMIT),
        name="adaln",
    )(c8, w_ada, b_ada.reshape(DEPTH, 1, n))


def _mod_row_of_tile(i, tile):
    n_ctx = T_CTX // tile
    per_req = DEC_SEQ // tile
    return jnp.where(i < n_ctx, CTX_MOD_ROW, (i - n_ctx) // per_req)


def _mod_spec(chunk, tile):
    return pl.BlockSpec((1, 1, D_MODEL), lambda i: (_mod_row_of_tile(i, tile) * 6 + chunk, 0, 0))


def _rms_norm_heads(x, ones_bd, w):
    hi, lo = _split_bf16(x * x)
    ssq = _dot(hi, ones_bd) + _dot(lo, ones_bd)
    return x * lax.rsqrt(ssq * (1.0 / HEAD_DIM) + RMS_EPS) * w


def _rope_cols(x, cos, sin, first16):
    cols = []
    for j in range(x.shape[1] // LANES):
        xc = x[:, j * LANES:(j + 1) * LANES]
        partner = jnp.where(first16, pltpu.roll(xc, LANES - 16, axis=1), pltpu.roll(xc, 16, axis=1))
        cols.append(xc * cos + partner * sin)
    return jnp.concatenate(cols, axis=1) if len(cols) > 1 else cols[0]


def _proj_kernel(x_ref, sh_ref, sc_ref, w_ref, ones_ref, qn_ref, kn_ref, cos_ref, sin_ref,
                 a_ref, qb_ref, kb_ref, vb_ref, qc_ref, kc_ref, vc_ref, wbf_ref):
    @pl.when(pl.program_id(0) == 0)
    def _():
        wbf_ref[...] = w_ref[...].astype(BF16)

    h = x_ref[...] * (1.0 + sc_ref[0]) + sh_ref[0]
    p = _dot(h.astype(BF16), wbf_ref[...])
    o = 0
    a_ref[...] = p[:, o:o + POOL_WIDTH]; o += POOL_WIDTH
    qb = p[:, o:o + QB_W]; o += QB_W
    kb = p[:, o:o + KB_W]; o += KB_W
    vb_ref[...] = p[:, o:o + KB_W]; o += KB_W
    qc_ref[...] = (p[:, o:o + C_W] * ATTN_SCALE).astype(BF16); o += C_W
    kc_ref[...] = p[:, o:o + C_W]; o += C_W
    vc_ref[...] = p[:, o:o + C_W]

    ones_bd = ones_ref[...]
    cos = cos_ref[...]
    sin = sin_ref[...]
    lane = lax.broadcasted_iota(jnp.int32, (1, LANES), 1)
    first16 = (lane & 31) < 16
    qb = _rope_cols(_rms_norm_heads(qb, ones_bd, qn_ref[...]), cos, sin, first16)
    kb = _rope_cols(_rms_norm_heads(kb, ones_bd[:KB_W, :KB_W], kn_ref[...]), cos, sin, first16)
    qb_ref[...] = (qb * ATTN_SCALE).astype(BF16)
    kb_ref[...] = kb


def _proj(x, mod, w_in, ones_bd, qn, kn, cos_t, sin_t):
    n_tiles = T_ALL // TT
    n_ctx = T_CTX // TT
    per_req = DEC_SEQ // TT

    def rope_idx(i):
        return (jnp.where(i < n_ctx, per_req, (i - n_ctx) % per_req), 0)

    row = lambda w: pl.BlockSpec((TT, w), lambda i: (i, 0))
    const = lambda s: pl.BlockSpec(s, lambda i: (0,) * len(s))
    widths = (POOL_WIDTH, QB_W, KB_W, KB_W, C_W, C_W, C_W)
    dtypes = (F32, BF16, F32, F32, BF16, F32, F32)
    return pl.pallas_call(
        _proj_kernel,
        grid=(n_tiles,),
        in_specs=[row(D_MODEL), _mod_spec(0, TT), _mod_spec(1, TT), const((D_MODEL, PROJ_WIDTH)),
                  const((QB_W, QB_W)), const((1, QB_W)), const((1, KB_W)),
                  pl.BlockSpec((TT, LANES), rope_idx), pl.BlockSpec((TT, LANES), rope_idx)],
        out_specs=[row(w) for w in widths],
        out_shape=[jax.ShapeDtypeStruct((T_ALL, w), d) for w, d in zip(widths, dtypes)],
        scratch_shapes=[pltpu.VMEM((D_MODEL, PROJ_WIDTH), BF16)],
        compiler_params=pltpu.CompilerParams(dimension_semantics=("arbitrary",), vmem_limit_bytes=VMEM_LIMIT),
        name="proj",
    )(x, mod, mod, w_in, ones_bd, qn, kn, cos_t, sin_t)


def _pool_kernel(a_ref, w_ref, scale_ref, o_ref):
    i = pl.program_id(0)
    a = a_ref[...]
    row = lax.broadcasted_iota(jnp.int32, a.shape, 0)
    lane = lax.broadcasted_iota(jnp.int32, a.shape, 1)
    seq_m1 = jnp.where(i < T_CTX // TP, SEQ - 1, DEC_SEQ - 1)
    t = row & seq_m1

    def shifted(k):
        v = pltpu.roll(a, (TP - k) % TP, axis=0) if k else a
        ok = (t + k >= 0) & (t + k <= seq_m1)
        return jnp.where(ok, v, 0.0)

    s = {k: shifted(k) for k in range(-8, 8)}
    c2 = s[-1] + s[0]
    c4 = c2 + s[-2] + s[1]
    c8 = c4 + s[-4] + s[-3] + s[2] + s[3]
    c16 = c8 + s[-8] + s[-7] + s[-6] + s[-5] + s[4] + s[5] + s[6] + s[7]
    g = lane >> 6
    csum = jnp.where(g == 0, c2, jnp.where(g == 1, c4, jnp.where(g == 2, c8, c16)))
    half = jnp.where(g == 0, 1, jnp.where(g == 1, 2, jnp.where(g == 2, 4, 8)))
    cnt = jnp.minimum(t + half, seq_m1 + 1) - jnp.maximum(t - half, 0)
    pooled = csum / cnt.astype(F32) - a
    y = _dot(pooled.astype(BF16), w_ref[...].astype(BF16)) * scale_ref[...]
    o_ref[...] = y.astype(BF16)


def _pool(a, w_bd, scale):
    return pl.pallas_call(
        _pool_kernel,
        grid=(T_ALL // TP,),
        in_specs=[pl.BlockSpec((TP, POOL_WIDTH), lambda i: (i, 0)),
                  pl.BlockSpec((POOL_WIDTH, POOL_WIDTH), lambda i: (0, 0)),
                  pl.BlockSpec((1, POOL_WIDTH), lambda i: (0, 0))],
        out_specs=pl.BlockSpec((TP, POOL_WIDTH), lambda i: (i, 0)),
        out_shape=jax.ShapeDtypeStruct((T_ALL, POOL_WIDTH), BF16),
        compiler_params=pltpu.CompilerParams(dimension_semantics=("arbitrary",), vmem_limit_bytes=VMEM_LIMIT),
        name="pool",
    )(a, w_bd, scale)


def _softmax_pv(scores, values):
    m = scores[0].max(axis=-1, keepdims=True)
    for s in scores[1:]:
        m = jnp.maximum(m, s.max(axis=-1, keepdims=True))
    ps = [jnp.exp(s - m) for s in scores]
    l = ps[0].sum(axis=-1, keepdims=True)
    for p in ps[1:]:
        l = l + p.sum(axis=-1, keepdims=True)
    r = 1.0 / l
    o = _dot((ps[0] * r).astype(BF16), values[0])
    for p, v in zip(ps[1:], values[1:]):
        o = o + _dot((p * r).astype(BF16), v)
    return o


def _lane_halves():
    lane = lax.broadcasted_iota(jnp.int32, (1, LANES), 1)
    return lane < HEAD_DIM, lane >= HEAD_DIM


def _keep(x, mask):
    return jnp.where(mask, x, 0.0).astype(BF16)


def _gqa_variants(x):
    lo, hi = _lane_halves()
    xs = pltpu.roll(x, HEAD_DIM, axis=1)
    nat_lo, nat_hi = _keep(x, lo), _keep(x, hi)
    sw_lo, sw_hi = _keep(xs, lo), _keep(xs, hi)
    return ((nat_lo, sw_hi), (nat_lo, nat_hi), (sw_lo, nat_hi))


def _mha_variants(x):
    lo, hi = _lane_halves()
    out = []
    for j in range(x.shape[1] // LANES):
        xc = x[:, j * LANES:(j + 1) * LANES]
        out.append((_keep(xc, lo), _keep(xc, hi)))
    return tuple(out)


def _attend_cols(q, k_vars, v_vars, extra_k=None, extra_v=None, bias=None):
    cols = []
    for j in range(q.shape[1] // LANES):
        qc = q[:, j * LANES:(j + 1) * LANES]
        o = None
        for hh in range(2):
            s = _dot_nt(qc, k_vars[j][hh])
            if bias is not None:
                s = s + bias[j][hh]
            scores, values = [s], [v_vars[j][hh]]
            if extra_k is not None:
                scores.append(_dot_nt(qc, extra_k[j][hh]))
                values.append(extra_v[j][hh])
            oh = _softmax_pv(scores, values)
            o = oh if o is None else o + oh
        cols.append(o)
    return jnp.concatenate(cols, axis=1)


def _attn_ctx_kernel(qb_ref, kb_ref, vb_ref, qc_ref, kc_ref, vc_ref, yb_ref, yc_ref):
    yb = _attend_cols(qb_ref[...], _gqa_variants(kb_ref[...]), _gqa_variants(vb_ref[...]))
    yb_ref[...] = yb.astype(BF16)
    yc = _attend_cols(qc_ref[...], _mha_variants(kc_ref[...]), _mha_variants(vc_ref[...]))
    yc_ref[...] = yc.astype(BF16)


def _attn_ctx(qb, kb, vb, qc, kc, vc):
    row = lambda w: pl.BlockSpec((SEQ, w), lambda i: (i, 0))
    return pl.pallas_call(
        _attn_ctx_kernel,
        grid=(BATCH,),
        in_specs=[row(QB_W), row(KB_W), row(KB_W), row(C_W), row(C_W), row(C_W)],
        out_specs=[row(QB_W), row(C_W)],
        out_shape=[jax.ShapeDtypeStruct((T_CTX, QB_W), BF16), jax.ShapeDtypeStruct((T_CTX, C_W), BF16)],
        compiler_params=pltpu.CompilerParams(dimension_semantics=("arbitrary",), vmem_limit_bytes=VMEM_LIMIT),
        name="attn_ctx",
    )(qb, kb, vb, qc, kc, vc)


def _attn_latb_kernel(q_ref, k_ref, v_ref, ck_ref, cv_ref, y_ref):
    y = _attend_cols(q_ref[...], _gqa_variants(ck_ref[0, 0]), _gqa_variants(cv_ref[0, 0]),
                     extra_k=_gqa_variants(k_ref[...]), extra_v=_gqa_variants(v_ref[...]))
    y_ref[...] = y.astype(BF16)


def _attn_latb(layer, qb, kb, vb, cache_k, cache_v):
    ctx_h = T_CTX // HALF
    ctx_r = T_CTX // DEC_SEQ
    cache = pl.BlockSpec((1, 1, PAST_LEN, KB_W), lambda b, s: (b, layer, 0, 0))
    own = pl.BlockSpec((DEC_SEQ, KB_W), lambda b, s: (ctx_r + b, 0))
    return pl.pallas_call(
        _attn_latb_kernel,
        grid=(DEC_BATCH, DEC_SEQ // HALF),
        in_specs=[pl.BlockSpec((HALF, QB_W), lambda b, s: (ctx_h + 2 * b + s, 0)), own, own, cache, cache],
        out_specs=pl.BlockSpec((HALF, QB_W), lambda b, s: (2 * b + s, 0)),
        out_shape=jax.ShapeDtypeStruct((T_LAT, QB_W), BF16),
        compiler_params=pltpu.CompilerParams(
            dimension_semantics=("arbitrary", "arbitrary"), vmem_limit_bytes=VMEM_LIMIT),
        name="attn_latb",
    )(qb, kb, vb, cache_k, cache_v)


def _natten_kernel(q_ref, k_ref, v_ref, ck_ref, cv_ref, tab_ref, y_ref):
    s = pl.program_id(1)
    start = pl.multiple_of(s * (4 * GRID_W), 4 * GRID_W)
    k = _mha_variants(k_ref[pl.ds(start, NAT_KEYS), :])
    v = _mha_variants(v_ref[pl.ds(start, NAT_KEYS), :])
    ck = _mha_variants(ck_ref[0, 0])
    cv = _mha_variants(cv_ref[0, 0])
    bias = ((tab_ref[0, 0, 0], tab_ref[0, 1, 0]),)
    y = _attend_cols(q_ref[...], k, v, extra_k=ck, extra_v=cv, bias=bias)
    y_ref[...] = y.astype(BF16)


def _natten(layer, qc, kc, vc, cache_k, cache_v, table):
    ctx_h = T_CTX // HALF
    ctx_r = T_CTX // DEC_SEQ
    cache = pl.BlockSpec((1, 1, PAST_LEN, LANES), lambda j, s, b: (b, layer, 0, j))
    own = pl.BlockSpec((DEC_SEQ, LANES), lambda j, s, b: (ctx_r + b, j))
    return pl.pallas_call(
        _natten_kernel,
        grid=(C_W // LANES, DEC_SEQ // HALF, DEC_BATCH),
        in_specs=[pl.BlockSpec((HALF, LANES), lambda j, s, b: (ctx_h + 2 * b + s, j)), own, own, cache, cache,
                  pl.BlockSpec((1, 2, 1, HALF, NAT_KEYS), lambda j, s, b: (j, 0, s, 0, 0))],
        out_specs=pl.BlockSpec((HALF, LANES), lambda j, s, b: (2 * b + s, j)),
        out_shape=jax.ShapeDtypeStruct((T_LAT, C_W), BF16),
        compiler_params=pltpu.CompilerParams(
            dimension_semantics=("arbitrary", "arbitrary", "arbitrary"), vmem_limit_bytes=VMEM_LIMIT),
        name="natten",
    )(qc, kc, vc, cache_k, cache_v, table)


def _natten_table(rpb):
    i = np.arange(HALF) // GRID_W
    qcol = np.arange(HALF) % GRID_W
    j = np.arange(NAT_KEYS) // GRID_W
    kcol = np.arange(NAT_KEYS) % GRID_W
    tabs = []
    for s in range(2):
        r = 8 * s + i
        rk = 4 * s + j
        rs = np.clip(r - WIN_R // 2, 0, GRID_ROWS - WIN_R)
        row_ok = (rk[None, :] >= rs[:, None]) & (rk[None, :] < rs[:, None] + WIN_R)
        cs = np.clip(qcol - WIN_C // 2, 0, GRID_W - WIN_C)
        col_ok = (kcol[None, :] >= cs[:, None]) & (kcol[None, :] < cs[:, None] + WIN_C)
        dr = np.clip(rk[None, :] - r[:, None], -(WIN_R - 1), WIN_R - 1) + (WIN_R - 1)
        dc = np.clip(kcol[None, :] - qcol[:, None], -(WIN_C - 1), WIN_C - 1) + (WIN_C - 1)
        tabs.append(jnp.where(jnp.asarray(row_ok & col_ok)[None], rpb[:, dr, dc], NEG))
    tab = jnp.stack(tabs, axis=1)
    return tab.reshape(H_C // 2, 2, 2, HALF, NAT_KEYS)


def _route(lg):
    lane = lax.broadcasted_iota(jnp.int32, lg.shape, 1).astype(F32)
    low = jnp.float32(-3.0e38)
    far = jnp.float32(ROUTER_LANES)
    is_g = lane < N_GROUPS
    gmax = jnp.where(is_g, lg, low).max(axis=-1, keepdims=True)
    gsel = jnp.where(is_g & (lg == gmax), lane, far).min(axis=-1, keepdims=True)
    pg_sel = 1.0 / jnp.where(is_g, jnp.exp(lg - gmax), 0.0).sum(axis=-1, keepdims=True)
    e0 = EXPERT_LANE0 + EXPERTS_PER_GROUP * gsel
    in_g = (lane >= e0) & (lane < e0 + EXPERTS_PER_GROUP)
    m1 = jnp.where(in_g, lg, low).max(axis=-1, keepdims=True)
    i1 = jnp.where(in_g & (lg == m1), lane, far).min(axis=-1, keepdims=True)
    rest = in_g & (lane != i1)
    m2 = jnp.where(rest, lg, low).max(axis=-1, keepdims=True)
    i2 = jnp.where(rest & (lg == m2), lane, far).min(axis=-1, keepdims=True)
    t = jnp.exp(m2 - m1)
    ssum = 1.0 + t
    w1 = (1.0 / ssum) * pg_sel
    w2 = (t / ssum) * pg_sel
    return jnp.where(lane == i1, w1, jnp.where(lane == i2, w2, 0.0))


def _tail_kernel(x_ref, ya_ref, yb_ref, yc_ref, w_ref, g1_ref, sh2_ref, sc2_ref, lg_ref, lb_ref, wr_ref, br_ref,
                 x1_ref, h2_ref, gate_ref, wbf_ref):
    @pl.when(pl.program_id(0) == 0)
    def _():
        wbf_ref[...] = w_ref[...].astype(BF16)

    o = (_dot(ya_ref[...], wbf_ref[0:POOL_WIDTH, :])
         + _dot(yb_ref[...], wbf_ref[POOL_WIDTH:POOL_WIDTH + QB_W, :])
         + _dot(yc_ref[...], wbf_ref[POOL_WIDTH + QB_W:, :]))
    x1 = _layer_norm(ALPHA * x_ref[...] + g1_ref[0] * o, lg_ref[...], lb_ref[...])
    x1_ref[...] = x1
    h2 = x1 * (1.0 + sc2_ref[0]) + sh2_ref[0]
    hh, hl = _split_bf16(h2)
    h2_ref[...] = hh
    wh, wl = _split_bf16(wr_ref[...])
    lg = _dot(hh, wh) + _dot(hl, wh) + _dot(hh, wl) + br_ref[...]
    gate_ref[...] = _route(lg)


def _tail(x, ya, yb, yc, w_out, mod, ln_g, ln_b, wr, br):
    row = lambda w: pl.BlockSpec((TT, w), lambda i: (i, 0))
    const = lambda s: pl.BlockSpec(s, lambda i: (0,) * len(s))
    return pl.pallas_call(
        _tail_kernel,
        grid=(T_ALL // TT,),
        in_specs=[row(D_MODEL), row(POOL_WIDTH), row(QB_W), row(C_W), const((D_MODEL, D_MODEL)),
                  _mod_spec(2, TT), _mod_spec(3, TT), _mod_spec(4, TT),
                  const((1, D_MODEL)), const((1, D_MODEL)), const((D_MODEL, ROUTER_LANES)), const((1, ROUTER_LANES))],
        out_specs=[row(D_MODEL), row(D_MODEL), row(ROUTER_LANES)],
        out_shape=[jax.ShapeDtypeStruct((T_ALL, D_MODEL), F32), jax.ShapeDtypeStruct((T_ALL, D_MODEL), BF16),
                   jax.ShapeDtypeStruct((T_ALL, ROUTER_LANES), F32)],
        scratch_shapes=[pltpu.VMEM((D_MODEL, D_MODEL), BF16)],
        compiler_params=pltpu.CompilerParams(dimension_semantics=("arbitrary",), vmem_limit_bytes=VMEM_LIMIT),
        name="tail",
    )(x, ya, yb, yc, w_out, mod, mod, mod, ln_g, ln_b, wr, br)


def _moe_kernel(h_ref, gate_ref, x1_ref, wg_ref, wu_ref, wd_ref, g2_ref, lg_ref, lb_ref, o_ref, acc_ref):
    e = pl.program_id(1)

    @pl.when(e == 0)
    def _():
        acc_ref[...] = jnp.zeros_like(acc_ref)

    h = h_ref[...]
    hg = _dot(h, wg_ref[0].astype(BF16))
    hu = _dot(h, wu_ref[0].astype(BF16))
    gate = gate_ref[...]
    lane = lax.broadcasted_iota(jnp.int32, gate.shape, 1)
    gcol = jnp.where(lane == EXPERT_LANE0 + e, gate, 0.0).sum(axis=-1, keepdims=True)
    act = hg * jax.nn.sigmoid(hg) * hu * gcol
    acc_ref[...] += _dot(act.astype(BF16), wd_ref[0].astype(BF16))

    @pl.when(e == N_EXPERTS - 1)
    def _():
        y = ALPHA * x1_ref[...] + g2_ref[0] * acc_ref[...]
        o_ref[...] = _layer_norm(y, lg_ref[...], lb_ref[...])


def _moe(h2, gate, x1, w_gate, w_up, w_down, mod, ln_g, ln_b):
    row = lambda w: pl.BlockSpec((TM, w), lambda i, e: (i, 0))
    const = lambda s: pl.BlockSpec(s, lambda i, e: (0,) * len(s))
    g2 = pl.BlockSpec((1, 1, D_MODEL), lambda i, e: (_mod_row_of_tile(i, TM) * 6 + 5, 0, 0))
    return pl.pallas_call(
        _moe_kernel,
        grid=(T_ALL // TM, N_EXPERTS),
        in_specs=[row(D_MODEL), row(ROUTER_LANES), row(D_MODEL),
                  pl.BlockSpec((1, D_MODEL, D_EXPERT), lambda i, e: (e, 0, 0)),
                  pl.BlockSpec((1, D_MODEL, D_EXPERT), lambda i, e: (e, 0, 0)),
                  pl.BlockSpec((1, D_EXPERT, D_MODEL), lambda i, e: (e, 0, 0)),
                  g2, const((1, D_MODEL)), const((1, D_MODEL))],
        out_specs=row(D_MODEL),
        out_shape=jax.ShapeDtypeStruct((T_ALL, D_MODEL), F32),
        scratch_shapes=[pltpu.VMEM((TM, D_MODEL), F32)],
        compiler_params=pltpu.CompilerParams(
            dimension_semantics=("arbitrary", "arbitrary"), vmem_limit_bytes=VMEM_LIMIT),
        name="moe",
    )(h2, gate, x1, w_gate, w_up, w_down, mod, ln_g, ln_b)


def _rope_tables():
    t = np.arange(DEC_SEQ)
    pos = np.stack([t // GRID_W, t % GRID_W], axis=1).astype(np.float32)
    nf = HEAD_DIM // 4
    inv = jnp.asarray(ROPE_THETA, F32) ** (-jnp.arange(nf, dtype=F32) / nf)
    d = np.arange(LANES) % HEAD_DIM
    which = d // (HEAD_DIM // 2)
    ang = jnp.asarray(pos)[:, which] * inv[d % nf][None, :]
    sign = np.where((d % 32) < 16, -1.0, 1.0).astype(np.float32)
    cos = jnp.concatenate([jnp.cos(ang), jnp.ones((TT, LANES), F32)], axis=0)
    sin = jnp.concatenate([jnp.sin(ang) * sign[None, :], jnp.zeros((TT, LANES), F32)], axis=0)
    return cos, sin


def _block_ones():
    h = np.arange(QB_W) // HEAD_DIM
    return jnp.asarray((h[:, None] == h[None, :]).astype(np.float32), dtype=BF16)


def _pool_block_diag(pool_w):
    out = jnp.zeros((POOL_WIDTH, POOL_WIDTH), F32)
    for g in range(4):
        out = out.at[64 * g:64 * g + 64, 64 * g:64 * g + 64].set(pool_w[g])
    return out


def kernel(x_prompt, x_sample, cache_b_k, cache_b_v, cache_c_k, cache_c_v, c, c_ctx, w_ada, b_ada, w_in, w_out,
           pool_w, pool_scale, q_norm, k_norm, rpb, ln1_g, ln1_b, ln2_g, ln2_b, router_g, router_g_b, router_e,
           router_e_b, w_gate, w_up, w_down):
    x = jnp.concatenate([x_prompt.reshape(T_CTX, D_MODEL), x_sample.reshape(T_LAT, D_MODEL)], axis=0)
    c8 = jnp.concatenate([c, c_ctx[None], jnp.zeros((MOD_ROWS - DEC_BATCH - 1, D_MODEL), F32)], axis=0)
    mod_all = _adaln(c8, w_ada, b_ada)

    cos_t, sin_t = _rope_tables()
    ones_bd = _block_ones()
    cbk = cache_b_k.reshape(DEC_BATCH, DEPTH, PAST_LEN, KB_W)
    cbv = cache_b_v.reshape(DEC_BATCH, DEPTH, PAST_LEN, KB_W)
    cck = cache_c_k.reshape(DEC_BATCH, DEPTH, PAST_LEN, C_W)
    ccv = cache_c_v.reshape(DEC_BATCH, DEPTH, PAST_LEN, C_W)
    pad = jnp.zeros((D_MODEL, ROUTER_LANES - N_GROUPS - N_EXPERTS), F32)

    new_bk, new_bv, new_ck, new_cv = [], [], [], []
    for l in range(DEPTH):
        mod = mod_all[l].reshape(MOD_ROWS * 6, 1, D_MODEL)
        qn = jnp.tile(q_norm[l], H_B)[None]
        kn = jnp.tile(k_norm[l], KV_B)[None]
        a, qb, kb, vb, qc, kc, vc = _proj(x, mod, w_in[l], ones_bd, qn, kn, cos_t, sin_t)
        ya = _pool(a, _pool_block_diag(pool_w[l]), pool_scale[l][None])
        yb_ctx, yc_ctx = _attn_ctx(qb, kb, vb, qc, kc, vc)
        yb_lat = _attn_latb(l, qb, kb, vb, cbk, cbv)
        yc_lat = _natten(l, qc, kc, vc, cck, ccv, _natten_table(rpb[l]))
        yb = jnp.concatenate([yb_ctx, yb_lat], axis=0)
        yc = jnp.concatenate([yc_ctx, yc_lat], axis=0)
        wr = jnp.concatenate([router_g[l], router_e[l], pad], axis=1)
        br = jnp.concatenate([router_g_b[l], router_e_b[l], pad[0]], axis=0)[None]
        x1, h2, gate = _tail(x, ya, yb, yc, w_out[l], mod, ln1_g[l][None], ln1_b[l][None], wr, br)
        x = _moe(h2, gate, x1, w_gate[l], w_up[l], w_down[l], mod, ln2_g[l][None], ln2_b[l][None])
        new_bk.append(kb[:T_CTX].reshape(BATCH, SEQ, KV_B, HEAD_DIM))
        new_bv.append(vb[:T_CTX].reshape(BATCH, SEQ, KV_B, HEAD_DIM))
        new_ck.append(kc[:T_CTX].reshape(BATCH, SEQ, H_C, HEAD_DIM))
        new_cv.append(vc[:T_CTX].reshape(BATCH, SEQ, H_C, HEAD_DIM))

    y_prompt = x[:T_CTX].reshape(BATCH, SEQ, D_MODEL)
    y_sample = x[T_CTX:].reshape(DEC_BATCH, DEC_SEQ, D_MODEL)
    return (y_prompt, y_sample, jnp.stack(new_bk, axis=1), jnp.stack(new_bv, axis=1),
            jnp.stack(new_ck, axis=1), jnp.stack(new_cv, axis=1))
```

```python
import functools

import numpy as np
import jax
import jax.numpy as jnp
from jax import lax
from jax.experimental import pallas as pl
from jax.experimental.pallas import tpu as pltpu

F32 = jnp.float32
BF16 = jnp.bfloat16

D_MODEL = 1024
BATCH = 16
SEQ = 256
DEPTH = 2
DEC_BATCH = 4
DEC_SEQ = 1024
PAST_LEN = 256
GRID_W = 64
GRID_ROWS = DEC_SEQ // GRID_W
HEAD_DIM = 64
POOL_WIDTH = 256
POOL_WINDOWS = (2, 4, 8, 16)
H_B = 6
KV_B = 2
H_C = 6
WIN_R = 8
WIN_C = 16
ROPE_THETA = 10000.0
QB_W = H_B * HEAD_DIM
KB_W = KV_B * HEAD_DIM
C_W = H_C * HEAD_DIM
PROJ_WIDTH = 2048
N_GROUPS = 4
EXPERTS_PER_GROUP = 8
N_EXPERTS = 32
D_EXPERT = 256
ALPHA = (2 * DEPTH) ** 0.25
LN_EPS = 1e-6
RMS_EPS = 1e-6
NEG = -1e30
ATTN_SCALE = HEAD_DIM ** -0.5

T_CTX = BATCH * SEQ
T_LAT = DEC_BATCH * DEC_SEQ
T_ALL = T_CTX + T_LAT

LANES = 128
ROUTER_LANES = 128
EXPERT_LANE0 = N_GROUPS
MOD_ROWS = 8
CTX_MOD_ROW = DEC_BATCH

TT = 512
TP = 1024
TM = 1024
HALF = DEC_SEQ // 2
NAT_KEYS = 12 * GRID_W
VMEM_LIMIT = 56 * 1024 * 1024


def _dot(a, b):
    return jnp.dot(a, b, preferred_element_type=F32)


def _dot_nt(a, b):
    return lax.dot_general(a, b, (((1,), (1,)), ((), ())), preferred_element_type=F32)


def _split_bf16(x):
    hi = x.astype(BF16)
    lo = (x - hi.astype(F32)).astype(BF16)
    return hi, lo


def _layer_norm(y, g, b):
    mu = jnp.mean(y, axis=-1, keepdims=True)
    var = jnp.mean(jnp.square(y - mu), axis=-1, keepdims=True)
    return (y - mu) * lax.rsqrt(var + LN_EPS) * g + b


def _adaln_kernel(c_ref, w_ref, b_ref, o_ref):
    c = c_ref[...]
    s = (c * jax.nn.sigmoid(c)).astype(BF16)
    o_ref[0] = _dot(s, w_ref[0].astype(BF16)) + b_ref[0]


def _adaln(c8, w_ada, b_ada):
    tn = 1536
    n = w_ada.shape[-1]
    return pl.pallas_call(
        _adaln_kernel,
        grid=(DEPTH, n // tn),
        in_specs=[
            pl.BlockSpec((MOD_ROWS, D_MODEL), lambda l, j: (0, 0)),
            pl.BlockSpec((1, D_MODEL, tn), lambda l, j: (l, 0, j)),
            pl.BlockSpec((1, 1, tn), lambda l, j: (l, 0, j)),
        ],
        out_specs=pl.BlockSpec((1, MOD_ROWS, tn), lambda l, j: (l, 0, j)),
        out_shape=jax.ShapeDtypeStruct((DEPTH, MOD_ROWS, n), F32),
        compiler_params=pltpu.CompilerParams(
            dimension_semantics=("arbitrary", "arbitrary"), vmem_limit_bytes=VMEM_LIMIT),
        name="adaln",
    )(c8, w_ada, b_ada.reshape(DEPTH, 1, n))


def _mod_row_of_tile(i, tile):
    n_ctx = T_CTX // tile
    per_req = DEC_SEQ // tile
    return jnp.where(i < n_ctx, CTX_MOD_ROW, (i - n_ctx) // per_req)


def _mod_spec(chunk, tile):
    return pl.BlockSpec((1, 1, D_MODEL), lambda i: (_mod_row_of_tile(i, tile) * 6 + chunk, 0, 0))


def _rms_norm_heads(x, ones_bd, w):
    hi, lo = _split_bf16(x * x)
    ssq = _dot(hi, ones_bd) + _dot(lo, ones_bd)
    return x * lax.rsqrt(ssq * (1.0 / HEAD_DIM) + RMS_EPS) * w


def _rope_cols(x, cos, sin, first16):
    cols = []
    for j in range(x.shape[1] // LANES):
        xc = x[:, j * LANES:(j + 1) * LANES]
        partner = jnp.where(first16, pltpu.roll(xc, LANES - 16, axis=1), pltpu.roll(xc, 16, axis=1))
        cols.append(xc * cos + partner * sin)
    return jnp.concatenate(cols, axis=1) if len(cols) > 1 else cols[0]


def _proj_kernel(x_ref, sh_ref, sc_ref, w_ref, ones_ref, qn_ref, kn_ref, cos_ref, sin_ref,
                 a_ref, qb_ref, kb_ref, vb_ref, qc_ref, kc_ref, vc_ref, wbf_ref):
    @pl.when(pl.program_id(0) == 0)
    def _():
        wbf_ref[...] = w_ref[...].astype(BF16)

    h = x_ref[...] * (1.0 + sc_ref[0]) + sh_ref[0]
    p = _dot(h.astype(BF16), wbf_ref[...])
    o = 0
    a_ref[...] = p[:, o:o + POOL_WIDTH]; o += POOL_WIDTH
    qb = p[:, o:o + QB_W]; o += QB_W
    kb = p[:, o:o + KB_W]; o += KB_W
    vb_ref[...] = p[:, o:o + KB_W]; o += KB_W
    qc_ref[...] = (p[:, o:o + C_W] * ATTN_SCALE).astype(BF16); o += C_W
    kc_ref[...] = p[:, o:o + C_W]; o += C_W
    vc_ref[...] = p[:, o:o + C_W]

    ones_bd = ones_ref[...]
    cos = cos_ref[...]
    sin = sin_ref[...]
    lane = lax.broadcasted_iota(jnp.int32, (1, LANES), 1)
    first16 = (lane & 31) < 16
    qb = _rope_cols(_rms_norm_heads(qb, ones_bd, qn_ref[...]), cos, sin, first16)
    kb = _rope_cols(_rms_norm_heads(kb, ones_bd[:KB_W, :KB_W], kn_ref[...]), cos, sin, first16)
    qb_ref[...] = (qb * ATTN_SCALE).astype(BF16)
    kb_ref[...] = kb


def _proj(x, mod, w_in, ones_bd, qn, kn, cos_t, sin_t):
    n_tiles = T_ALL // TT
    n_ctx = T_CTX // TT
    per_req = DEC_SEQ // TT

    def rope_idx(i):
        return (jnp.where(i < n_ctx, per_req, (i - n_ctx) % per_req), 0)

    row = lambda w: pl.BlockSpec((TT, w), lambda i: (i, 0))
    const = lambda s: pl.BlockSpec(s, lambda i: (0,) * len(s))
    widths = (POOL_WIDTH, QB_W, KB_W, KB_W, C_W, C_W, C_W)
    dtypes = (F32, BF16, F32, F32, BF16, F32, F32)
    return pl.pallas_call(
        _proj_kernel,
        grid=(n_tiles,),
        in_specs=[row(D_MODEL), _mod_spec(0, TT), _mod_spec(1, TT), const((D_MODEL, PROJ_WIDTH)),
                  const((QB_W, QB_W)), const((1, QB_W)), const((1, KB_W)),
                  pl.BlockSpec((TT, LANES), rope_idx), pl.BlockSpec((TT, LANES), rope_idx)],
        out_specs=[row(w) for w in widths],
        out_shape=[jax.ShapeDtypeStruct((T_ALL, w), d) for w, d in zip(widths, dtypes)],
        scratch_shapes=[pltpu.VMEM((D_MODEL, PROJ_WIDTH), BF16)],
        compiler_params=pltpu.CompilerParams(dimension_semantics=("arbitrary",), vmem_limit_bytes=VMEM_LIMIT),
        name="proj",
    )(x, mod, mod, w_in, ones_bd, qn, kn, cos_t, sin_t)


def _pool_kernel(a_ref, w_ref, scale_ref, o_ref):
    i = pl.program_id(0)
    a = a_ref[...]
    row = lax.broadcasted_iota(jnp.int32, a.shape, 0)
    lane = lax.broadcasted_iota(jnp.int32, a.shape, 1)
    seq_m1 = jnp.where(i < T_CTX // TP, SEQ - 1, DEC_SEQ - 1)
    t = row & seq_m1

    def shifted(k):
        v = pltpu.roll(a, (TP - k) % TP, axis=0) if k else a
        ok = (t + k >= 0) & (t + k <= seq_m1)
        return jnp.where(ok, v, 0.0)

    s = {k: shifted(k) for k in range(-8, 8)}
    c2 = s[-1] + s[0]
    c4 = c2 + s[-2] + s[1]
    c8 = c4 + s[-4] + s[-3] + s[2] + s[3]
    c16 = c8 + s[-8] + s[-7] + s[-6] + s[-5] + s[4] + s[5] + s[6] + s[7]
    g = lane >> 6
    csum = jnp.where(g == 0, c2, jnp.where(g == 1, c4, jnp.where(g == 2, c8, c16)))
    half = jnp.where(g == 0, 1, jnp.where(g == 1, 2, jnp.where(g == 2, 4, 8)))
    cnt = jnp.minimum(t + half, seq_m1 + 1) - jnp.maximum(t - half, 0)
    pooled = csum / cnt.astype(F32) - a
    y = _dot(pooled.astype(BF16), w_ref[...].astype(BF16)) * scale_ref[...]
    o_ref[...] = y.astype(BF16)


def _pool(a, w_bd, scale):
    return pl.pallas_call(
        _pool_kernel,
        grid=(T_ALL // TP,),
        in_specs=[pl.BlockSpec((TP, POOL_WIDTH), lambda i: (i, 0)),
                  pl.BlockSpec((POOL_WIDTH, POOL_WIDTH), lambda i: (0, 0)),
                  pl.BlockSpec((1, POOL_WIDTH), lambda i: (0, 0))],
        out_specs=pl.BlockSpec((TP, POOL_WIDTH), lambda i: (i, 0)),
        out_shape=jax.ShapeDtypeStruct((T_ALL, POOL_WIDTH), BF16),
        compiler_params=pltpu.CompilerParams(dimension_semantics=("arbitrary",), vmem_limit_bytes=VMEM_LIMIT),
        name="pool",
    )(a, w_bd, scale)


def _softmax_pv(scores, values):
    m = scores[0].max(axis=-1, keepdims=True)
    for s in scores[1:]:
        m = jnp.maximum(m, s.max(axis=-1, keepdims=True))
    ps = [jnp.exp(s - m) for s in scores]
    l = ps[0].sum(axis=-1, keepdims=True)
    for p in ps[1:]:
        l = l + p.sum(axis=-1, keepdims=True)
    r = 1.0 / l
    o = _dot((ps[0] * r).astype(BF16), values[0])
    for p, v in zip(ps[1:], values[1:]):
        o = o + _dot((p * r).astype(BF16), v)
    return o


def _lane_halves():
    lane = lax.broadcasted_iota(jnp.int32, (1, LANES), 1)
    return lane < HEAD_DIM, lane >= HEAD_DIM


def _keep(x, mask):
    return jnp.where(mask, x, 0.0).astype(BF16)


def _gqa_variants(x):
    lo, hi = _lane_halves()
    xs = pltpu.roll(x, HEAD_DIM, axis=1)
    nat_lo, nat_hi = _keep(x, lo), _keep(x, hi)
    sw_lo, sw_hi = _keep(xs, lo), _keep(xs, hi)
    return ((nat_lo, sw_hi), (nat_lo, nat_hi), (sw_lo, nat_hi))


def _mha_variants(x):
    lo, hi = _lane_halves()
    out = []
    for j in range(x.shape[1] // LANES):
        xc = x[:, j * LANES:(j + 1) * LANES]
        out.append((_keep(xc, lo), _keep(xc, hi)))
    return tuple(out)


def _attend_cols(q, k_vars, v_vars, extra_k=None, extra_v=None, bias=None):
    cols = []
    for j in range(q.shape[1] // LANES):
        qc = q[:, j * LANES:(j + 1) * LANES]
        o = None
        for hh in range(2):
            s = _dot_nt(qc, k_vars[j][hh])
            if bias is not None:
                s = s + bias[j][hh]
            scores, values = [s], [v_vars[j][hh]]
            if extra_k is not None:
                scores.append(_dot_nt(qc, extra_k[j][hh]))
                values.append(extra_v[j][hh])
            oh = _softmax_pv(scores, values)
            o = oh if o is None else o + oh
        cols.append(o)
    return jnp.concatenate(cols, axis=1)


def _attn_ctx_kernel(qb_ref, kb_ref, vb_ref, qc_ref, kc_ref, vc_ref, yb_ref, yc_ref):
    yb = _attend_cols(qb_ref[...], _gqa_variants(kb_ref[...]), _gqa_variants(vb_ref[...]))
    yb_ref[...] = yb.astype(BF16)
    yc = _attend_cols(qc_ref[...], _mha_variants(kc_ref[...]), _mha_variants(vc_ref[...]))
    yc_ref[...] = yc.astype(BF16)


def _attn_ctx(qb, kb, vb, qc, kc, vc):
    row = lambda w: pl.BlockSpec((SEQ, w), lambda i: (i, 0))
    return pl.pallas_call(
        _attn_ctx_kernel,
        grid=(BATCH,),
        in_specs=[row(QB_W), row(KB_W), row(KB_W), row(C_W), row(C_W), row(C_W)],
        out_specs=[row(QB_W), row(C_W)],
        out_shape=[jax.ShapeDtypeStruct((T_CTX, QB_W), BF16), jax.ShapeDtypeStruct((T_CTX, C_W), BF16)],
        compiler_params=pltpu.CompilerParams(dimension_semantics=("arbitrary",), vmem_limit_bytes=VMEM_LIMIT),
        name="attn_ctx",
    )(qb, kb, vb, qc, kc, vc)


def _attn_latb_kernel(q_ref, k_ref, v_ref, ck_ref, cv_ref, y_ref):
    y = _attend_cols(q_ref[...], _gqa_variants(ck_ref[0, 0]), _gqa_variants(cv_ref[0, 0]),
                     extra_k=_gqa_variants(k_ref[...]), extra_v=_gqa_variants(v_ref[...]))
    y_ref[...] = y.astype(BF16)


def _attn_latb(layer, qb, kb, vb, cache_k, cache_v):
    ctx_h = T_CTX // HALF
    ctx_r = T_CTX // DEC_SEQ
    cache = pl.BlockSpec((1, 1, PAST_LEN, KB_W), lambda b, s: (b, layer, 0, 0))
    own = pl.BlockSpec((DEC_SEQ, KB_W), lambda b, s: (ctx_r + b, 0))
    return pl.pallas_call(
        _attn_latb_kernel,
        grid=(DEC_BATCH, DEC_SEQ // HALF),
        in_specs=[pl.BlockSpec((HALF, QB_W), lambda b, s: (ctx_h + 2 * b + s, 0)), own, own, cache, cache],
        out_specs=pl.BlockSpec((HALF, QB_W), lambda b, s: (2 * b + s, 0)),
        out_shape=jax.ShapeDtypeStruct((T_LAT, QB_W), BF16),
        compiler_params=pltpu.CompilerParams(
            dimension_semantics=("arbitrary", "arbitrary"), vmem_limit_bytes=VMEM_LIMIT),
        name="attn_latb",
    )(qb, kb, vb, cache_k, cache_v)


def _natten_kernel(q_ref, k_ref, v_ref, ck_ref, cv_ref, tab_ref, y_ref):
    s = pl.program_id(1)
    start = pl.multiple_of(s * (4 * GRID_W), 4 * GRID_W)
    k = _mha_variants(k_ref[pl.ds(start, NAT_KEYS), :])
    v = _mha_variants(v_ref[pl.ds(start, NAT_KEYS), :])
    ck = _mha_variants(ck_ref[0, 0])
    cv = _mha_variants(cv_ref[0, 0])
    bias = ((tab_ref[0, 0, 0], tab_ref[0, 1, 0]),)
    y = _attend_cols(q_ref[...], k, v, extra_k=ck, extra_v=cv, bias=bias)
    y_ref[...] = y.astype(BF16)


def _natten(layer, qc, kc, vc, cache_k, cache_v, table):
    ctx_h = T_CTX // HALF
    ctx_r = T_CTX // DEC_SEQ
    cache = pl.BlockSpec((1, 1, PAST_LEN, LANES), lambda j, s, b: (b, layer, 0, j))
    own = pl.BlockSpec((DEC_SEQ, LANES), lambda j, s, b: (ctx_r + b, j))
    return pl.pallas_call(
        _natten_kernel,
        grid=(C_W // LANES, DEC_SEQ // HALF, DEC_BATCH),
        in_specs=[pl.BlockSpec((HALF, LANES), lambda j, s, b: (ctx_h + 2 * b + s, j)), own, own, cache, cache,
                  pl.BlockSpec((1, 2, 1, HALF, NAT_KEYS), lambda j, s, b: (j, 0, s, 0, 0))],
        out_specs=pl.BlockSpec((HALF, LANES), lambda j, s, b: (2 * b + s, j)),
        out_shape=jax.ShapeDtypeStruct((T_LAT, C_W), BF16),
        compiler_params=pltpu.CompilerParams(
            dimension_semantics=("arbitrary", "arbitrary", "arbitrary"), vmem_limit_bytes=VMEM_LIMIT),
        name="natten",
    )(qc, kc, vc, cache_k, cache_v, table)


def _natten_table(rpb):
    qcol = np.arange(GRID_W)[:, None]
    kcol = np.arange(GRID_W)[None, :]
    cs = np.clip(qcol - WIN_C // 2, 0, GRID_W - WIN_C)
    col_ok = (kcol >= cs) & (kcol < cs + WIN_C)
    dc = np.clip(kcol - qcol, -(WIN_C - 1), WIN_C - 1) + (WIN_C - 1)
    blocks = jnp.full((H_C, 2 * WIN_R - 1, GRID_W, GRID_W), NEG, F32)
    for m in range(2 * WIN_C - 1):
        sel = jnp.asarray((dc == m) & col_ok)[None, None]
        blocks = jnp.where(sel, rpb[:, :, m][:, :, None, None], blocks)
    masked = jnp.full((H_C, GRID_W, GRID_W), NEG, F32)
    halves = []
    for s in range(2):
        rows = []
        for i in range(HALF // GRID_W):
            r = (HALF // GRID_W) * s + i
            rs = min(max(r - WIN_R // 2, 0), GRID_ROWS - WIN_R)
            row = []
            for j in range(NAT_KEYS // GRID_W):
                rk = 4 * s + j
                row.append(blocks[:, rk - r + WIN_R - 1] if rs <= rk < rs + WIN_R else masked)
            rows.append(jnp.concatenate(row, axis=2))
        halves.append(jnp.concatenate(rows, axis=1))
    tab = jnp.stack(halves, axis=1)
    return tab.reshape(H_C // 2, 2, 2, HALF, NAT_KEYS)


def _route(lg):
    lane = lax.broadcasted_iota(jnp.int32, lg.shape, 1).astype(F32)
    low = jnp.float32(-3.0e38)
    far = jnp.float32(ROUTER_LANES)
    is_g = lane < N_GROUPS
    gmax = jnp.where(is_g, lg, low).max(axis=-1, keepdims=True)
    gsel = jnp.where(is_g & (lg == gmax), lane, far).min(axis=-1, keepdims=True)
    pg_sel = 1.0 / jnp.where(is_g, jnp.exp(lg - gmax), 0.0).sum(axis=-1, keepdims=True)
    e0 = EXPERT_LANE0 + EXPERTS_PER_GROUP * gsel
    in_g = (lane >= e0) & (lane < e0 + EXPERTS_PER_GROUP)
    m1 = jnp.where(in_g, lg, low).max(axis=-1, keepdims=True)
    i1 = jnp.where(in_g & (lg == m1), lane, far).min(axis=-1, keepdims=True)
    rest = in_g & (lane != i1)
    m2 = jnp.where(rest, lg, low).max(axis=-1, keepdims=True)
    i2 = jnp.where(rest & (lg == m2), lane, far).min(axis=-1, keepdims=True)
    t = jnp.exp(m2 - m1)
    ssum = 1.0 + t
    w1 = (1.0 / ssum) * pg_sel
    w2 = (t / ssum) * pg_sel
    return jnp.where(lane == i1, w1, jnp.where(lane == i2, w2, 0.0))


def _tail_kernel(x_ref, ya_ref, yb_ref, yc_ref, w_ref, g1_ref, sh2_ref, sc2_ref, lg_ref, lb_ref, wr_ref, br_ref,
                 x1_ref, h2_ref, gate_ref, wbf_ref):
    @pl.when(pl.program_id(0) == 0)
    def _():
        wbf_ref[...] = w_ref[...].astype(BF16)

    o = (_dot(ya_ref[...], wbf_ref[0:POOL_WIDTH, :])
         + _dot(yb_ref[...], wbf_ref[POOL_WIDTH:POOL_WIDTH + QB_W, :])
         + _dot(yc_ref[...], wbf_ref[POOL_WIDTH + QB_W:, :]))
    x1 = _layer_norm(ALPHA * x_ref[...] + g1_ref[0] * o, lg_ref[...], lb_ref[...])
    x1_ref[...] = x1
    h2 = x1 * (1.0 + sc2_ref[0]) + sh2_ref[0]
    hh, hl = _split_bf16(h2)
    h2_ref[...] = hh
    wh, wl = _split_bf16(wr_ref[...])
    lg = _dot(hh, wh) + _dot(hl, wh) + _dot(hh, wl) + br_ref[...]
    gate_ref[...] = _route(lg)


def _tail(x, ya, yb, yc, w_out, mod, ln_g, ln_b, wr, br):
    row = lambda w: pl.BlockSpec((TT, w), lambda i: (i, 0))
    const = lambda s: pl.BlockSpec(s, lambda i: (0,) * len(s))
    return pl.pallas_call(
        _tail_kernel,
        grid=(T_ALL // TT,),
        in_specs=[row(D_MODEL), row(POOL_WIDTH), row(QB_W), row(C_W), const((D_MODEL, D_MODEL)),
                  _mod_spec(2, TT), _mod_spec(3, TT), _mod_spec(4, TT),
                  const((1, D_MODEL)), const((1, D_MODEL)), const((D_MODEL, ROUTER_LANES)), const((1, ROUTER_LANES))],
        out_specs=[row(D_MODEL), row(D_MODEL), row(ROUTER_LANES)],
        out_shape=[jax.ShapeDtypeStruct((T_ALL, D_MODEL), F32), jax.ShapeDtypeStruct((T_ALL, D_MODEL), BF16),
                   jax.ShapeDtypeStruct((T_ALL, ROUTER_LANES), F32)],
        scratch_shapes=[pltpu.VMEM((D_MODEL, D_MODEL), BF16)],
        compiler_params=pltpu.CompilerParams(dimension_semantics=("arbitrary",), vmem_limit_bytes=VMEM_LIMIT),
        name="tail",
    )(x, ya, yb, yc, w_out, mod, mod, mod, ln_g, ln_b, wr, br)


def _moe_kernel(h_ref, gate_ref, x1_ref, wg_ref, wu_ref, wd_ref, g2_ref, lg_ref, lb_ref, o_ref, acc_ref):
    e = pl.program_id(1)

    @pl.when(e == 0)
    def _():
        acc_ref[...] = jnp.zeros_like(acc_ref)

    h = h_ref[...]
    hg = _dot(h, wg_ref[0].astype(BF16))
    hu = _dot(h, wu_ref[0].astype(BF16))
    gate = gate_ref[...]
    lane = lax.broadcasted_iota(jnp.int32, gate.shape, 1)
    gcol = jnp.where(lane == EXPERT_LANE0 + e, gate, 0.0).sum(axis=-1, keepdims=True)
    act = hg * jax.nn.sigmoid(hg) * hu * gcol
    acc_ref[...] += _dot(act.astype(BF16), wd_ref[0].astype(BF16))

    @pl.when(e == N_EXPERTS - 1)
    def _():
        y = ALPHA * x1_ref[...] + g2_ref[0] * acc_ref[...]
        o_ref[...] = _layer_norm(y, lg_ref[...], lb_ref[...])


def _moe(h2, gate, x1, w_gate, w_up, w_down, mod, ln_g, ln_b):
    row = lambda w: pl.BlockSpec((TM, w), lambda i, e: (i, 0))
    const = lambda s: pl.BlockSpec(s, lambda i, e: (0,) * len(s))
    g2 = pl.BlockSpec((1, 1, D_MODEL), lambda i, e: (_mod_row_of_tile(i, TM) * 6 + 5, 0, 0))
    return pl.pallas_call(
        _moe_kernel,
        grid=(T_ALL // TM, N_EXPERTS),
        in_specs=[row(D_MODEL), row(ROUTER_LANES), row(D_MODEL),
                  pl.BlockSpec((1, D_MODEL, D_EXPERT), lambda i, e: (e, 0, 0)),
                  pl.BlockSpec((1, D_MODEL, D_EXPERT), lambda i, e: (e, 0, 0)),
                  pl.BlockSpec((1, D_EXPERT, D_MODEL), lambda i, e: (e, 0, 0)),
                  g2, const((1, D_MODEL)), const((1, D_MODEL))],
        out_specs=row(D_MODEL),
        out_shape=jax.ShapeDtypeStruct((T_ALL, D_MODEL), F32),
        scratch_shapes=[pltpu.VMEM((TM, D_MODEL), F32)],
        compiler_params=pltpu.CompilerParams(
            dimension_semantics=("arbitrary", "arbitrary"), vmem_limit_bytes=VMEM_LIMIT),
        name="moe",
    )(h2, gate, x1, w_gate, w_up, w_down, mod, ln_g, ln_b)


def _rope_tables():
    t = np.arange(DEC_SEQ)
    pos = np.stack([t // GRID_W, t % GRID_W], axis=1).astype(np.float32)
    nf = HEAD_DIM // 4
    inv = jnp.asarray(ROPE_THETA, F32) ** (-jnp.arange(nf, dtype=F32) / nf)
    d = np.arange(LANES) % HEAD_DIM
    which = d // (HEAD_DIM // 2)
    ang = jnp.asarray(pos)[:, which] * inv[d % nf][None, :]
    sign = np.where((d % 32) < 16, -1.0, 1.0).astype(np.float32)
    cos = jnp.concatenate([jnp.cos(ang), jnp.ones((TT, LANES), F32)], axis=0)
    sin = jnp.concatenate([jnp.sin(ang) * sign[None, :], jnp.zeros((TT, LANES), F32)], axis=0)
    return cos, sin


def _block_ones():
    h = np.arange(QB_W) // HEAD_DIM
    return jnp.asarray((h[:, None] == h[None, :]).astype(np.float32), dtype=BF16)


def _pool_block_diag(pool_w):
    out = jnp.zeros((POOL_WIDTH, POOL_WIDTH), F32)
    for g in range(4):
        out = out.at[64 * g:64 * g + 64, 64 * g:64 * g + 64].set(pool_w[g])
    return out


def kernel(x_prompt, x_sample, cache_b_k, cache_b_v, cache_c_k, cache_c_v, c, c_ctx, w_ada, b_ada, w_in, w_out,
           pool_w, pool_scale, q_norm, k_norm, rpb, ln1_g, ln1_b, ln2_g, ln2_b, router_g, router_g_b, router_e,
           router_e_b, w_gate, w_up, w_down):
    x = jnp.concatenate([x_prompt.reshape(T_CTX, D_MODEL), x_sample.reshape(T_LAT, D_MODEL)], axis=0)
    c8 = jnp.concatenate([c, c_ctx[None], jnp.zeros((MOD_ROWS - DEC_BATCH - 1, D_MODEL), F32)], axis=0)
    mod_all = _adaln(c8, w_ada, b_ada)

    cos_t, sin_t = _rope_tables()
    ones_bd = _block_ones()
    cbk = cache_b_k.reshape(DEC_BATCH, DEPTH, PAST_LEN, KB_W)
    cbv = cache_b_v.reshape(DEC_BATCH, DEPTH, PAST_LEN, KB_W)
    cck = cache_c_k.reshape(DEC_BATCH, DEPTH, PAST_LEN, C_W)
    ccv = cache_c_v.reshape(DEC_BATCH, DEPTH, PAST_LEN, C_W)
    pad = jnp.zeros((D_MODEL, ROUTER_LANES - N_GROUPS - N_EXPERTS), F32)

    new_bk, new_bv, new_ck, new_cv = [], [], [], []
    for l in range(DEPTH):
        mod = mod_all[l].reshape(MOD_ROWS * 6, 1, D_MODEL)
        qn = jnp.tile(q_norm[l], H_B)[None]
        kn = jnp.tile(k_norm[l], KV_B)[None]
        a, qb, kb, vb, qc, kc, vc = _proj(x, mod, w_in[l], ones_bd, qn, kn, cos_t, sin_t)
        ya = _pool(a, _pool_block_diag(pool_w[l]), pool_scale[l][None])
        yb_ctx, yc_ctx = _attn_ctx(qb, kb, vb, qc, kc, vc)
        yb_lat = _attn_latb(l, qb, kb, vb, cbk, cbv)
        yc_lat = _natten(l, qc, kc, vc, cck, ccv, _natten_table(rpb[l]))
        yb = jnp.concatenate([yb_ctx, yb_lat], axis=0)
        yc = jnp.concatenate([yc_ctx, yc_lat], axis=0)
        wr = jnp.concatenate([router_g[l], router_e[l], pad], axis=1)
        br = jnp.concatenate([router_g_b[l], router_e_b[l], pad[0]], axis=0)[None]
        x1, h2, gate = _tail(x, ya, yb, yc, w_out[l], mod, ln1_g[l][None], ln1_b[l][None], wr, br)
        x = _moe(h2, gate, x1, w_gate[l], w_up[l], w_down[l], mod, ln2_g[l][None], ln2_b[l][None])
        new_bk.append(kb[:T_CTX].reshape(BATCH, SEQ, KV_B, HEAD_DIM))
        new_bv.append(vb[:T_CTX].reshape(BATCH, SEQ, KV_B, HEAD_DIM))
        new_ck.append(kc[:T_CTX].reshape(BATCH, SEQ, H_C, HEAD_DIM))
        new_cv.append(vc[:T_CTX].reshape(BATCH, SEQ, H_C, HEAD_DIM))

    y_prompt = x[:T_CTX].reshape(BATCH, SEQ, D_MODEL)
    y_sample = x[T_CTX:].reshape(DEC_BATCH, DEC_SEQ, D_MODEL)
    return (y_prompt, y_sample, jnp.stack(new_bk, axis=1), jnp.stack(new_bv, axis=1),
            jnp.stack(new_ck, axis=1), jnp.stack(new_cv, axis=1))
```

```python
import functools

import numpy as np
import jax
import jax.numpy as jnp
from jax import lax
from jax.experimental import pallas as pl
from jax.experimental.pallas import tpu as pltpu

F32 = jnp.float32
BF16 = jnp.bfloat16

D_MODEL = 1024
BATCH = 16
SEQ = 256
DEPTH = 2
DEC_BATCH = 4
DEC_SEQ = 1024
PAST_LEN = 256
GRID_W = 64
GRID_ROWS = DEC_SEQ // GRID_W
HEAD_DIM = 64
POOL_WIDTH = 256
POOL_WINDOWS = (2, 4, 8, 16)
H_B = 6
KV_B = 2
H_C = 6
WIN_R = 8
WIN_C = 16
ROPE_THETA = 10000.0
QB_W = H_B * HEAD_DIM
KB_W = KV_B * HEAD_DIM
C_W = H_C * HEAD_DIM
PROJ_WIDTH = 2048
N_GROUPS = 4
EXPERTS_PER_GROUP = 8
N_EXPERTS = 32
D_EXPERT = 256
ALPHA = (2 * DEPTH) ** 0.25
LN_EPS = 1e-6
RMS_EPS = 1e-6
NEG = -1e30
ATTN_SCALE = HEAD_DIM ** -0.5

T_CTX = BATCH * SEQ
T_LAT = DEC_BATCH * DEC_SEQ
T_ALL = T_CTX + T_LAT

LANES = 128
ROUTER_LANES = 128
EXPERT_LANE0 = N_GROUPS
MOD_ROWS = 8
CTX_MOD_ROW = DEC_BATCH

TT = 512
TP = 1024
N_STREAMS = 2
T_STREAM = T_ALL // N_STREAMS
TMX = 128
P_MAX = 2 * T_STREAM + N_EXPERTS * TMX
NT_MAX = P_MAX // TMX
GROUP_ROWS = 4
PACK_W = D_MODEL // 2
XSTRIDE = TMX + 8
HALF = DEC_SEQ // 2
NAT_KEYS = 12 * GRID_W
VMEM_LIMIT = 56 * 1024 * 1024


def _dot(a, b):
    return jnp.dot(a, b, preferred_element_type=F32)


def _dot_nt(a, b):
    return lax.dot_general(a, b, (((1,), (1,)), ((), ())), preferred_element_type=F32)


def _split_bf16(x):
    hi = x.astype(BF16)
    lo = (x - hi.astype(F32)).astype(BF16)
    return hi, lo


def _layer_norm(y, g, b):
    mu = jnp.mean(y, axis=-1, keepdims=True)
    var = jnp.mean(jnp.square(y - mu), axis=-1, keepdims=True)
    return (y - mu) * lax.rsqrt(var + LN_EPS) * g + b


def _adaln_kernel(c_ref, w_ref, b_ref, o_ref):
    c = c_ref[...]
    s = (c * jax.nn.sigmoid(c)).astype(BF16)
    o_ref[0] = _dot(s, w_ref[0].astype(BF16)) + b_ref[0]


def _adaln(c8, w_ada, b_ada):
    tn = 1536
    n = w_ada.shape[-1]
    return pl.pallas_call(
        _adaln_kernel,
        grid=(DEPTH, n // tn),
        in_specs=[
            pl.BlockSpec((MOD_ROWS, D_MODEL), lambda l, j: (0, 0)),
            pl.BlockSpec((1, D_MODEL, tn), lambda l, j: (l, 0, j)),
            pl.BlockSpec((1, 1, tn), lambda l, j: (l, 0, j)),
        ],
        out_specs=pl.BlockSpec((1, MOD_ROWS, tn), lambda l, j: (l, 0, j)),
        out_shape=jax.ShapeDtypeStruct((DEPTH, MOD_ROWS, n), F32),
        compiler_params=pltpu.CompilerParams(
            dimension_semantics=("arbitrary", "arbitrary"), vmem_limit_bytes=VMEM_LIMIT),
        name="adaln",
    )(c8, w_ada, b_ada.reshape(DEPTH, 1, n))


def _mod_row_of_tile(i, tile):
    n_ctx = T_CTX // tile
    per_req = DEC_SEQ // tile
    return jnp.where(i < n_ctx, CTX_MOD_ROW, (i - n_ctx) // per_req)


def _mod_spec(chunk, tile):
    return pl.BlockSpec((1, 1, D_MODEL), lambda i: (_mod_row_of_tile(i, tile) * 6 + chunk, 0, 0))


def _rms_norm_heads(x, ones_bd, w):
    hi, lo = _split_bf16(x * x)
    ssq = _dot(hi, ones_bd) + _dot(lo, ones_bd)
    return x * lax.rsqrt(ssq * (1.0 / HEAD_DIM) + RMS_EPS) * w


def _rope_cols(x, cos, sin, first16):
    cols = []
    for j in range(x.shape[1] // LANES):
        xc = x[:, j * LANES:(j + 1) * LANES]
        partner = jnp.where(first16, pltpu.roll(xc, LANES - 16, axis=1), pltpu.roll(xc, 16, axis=1))
        cols.append(xc * cos + partner * sin)
    return jnp.concatenate(cols, axis=1) if len(cols) > 1 else cols[0]


def _proj_kernel(x_ref, sh_ref, sc_ref, w_ref, ones_ref, qn_ref, kn_ref, cos_ref, sin_ref,
                 a_ref, qb_ref, kb_ref, vb_ref, qc_ref, kc_ref, vc_ref, wbf_ref):
    @pl.when(pl.program_id(0) == 0)
    def _():
        wbf_ref[...] = w_ref[...].astype(BF16)

    h = x_ref[...] * (1.0 + sc_ref[0]) + sh_ref[0]
    p = _dot(h.astype(BF16), wbf_ref[...])
    o = 0
    a_ref[...] = p[:, o:o + POOL_WIDTH]; o += POOL_WIDTH
    qb = p[:, o:o + QB_W]; o += QB_W
    kb = p[:, o:o + KB_W]; o += KB_W
    vb_ref[...] = p[:, o:o + KB_W]; o += KB_W
    qc_ref[...] = (p[:, o:o + C_W] * ATTN_SCALE).astype(BF16); o += C_W
    kc_ref[...] = p[:, o:o + C_W]; o += C_W
    vc_ref[...] = p[:, o:o + C_W]

    ones_bd = ones_ref[...]
    cos = cos_ref[...]
    sin = sin_ref[...]
    lane = lax.broadcasted_iota(jnp.int32, (1, LANES), 1)
    first16 = (lane & 31) < 16
    qb = _rope_cols(_rms_norm_heads(qb, ones_bd, qn_ref[...]), cos, sin, first16)
    kb = _rope_cols(_rms_norm_heads(kb, ones_bd[:KB_W, :KB_W], kn_ref[...]), cos, sin, first16)
    qb_ref[...] = (qb * ATTN_SCALE).astype(BF16)
    kb_ref[...] = kb


def _proj(x, mod, w_in, ones_bd, qn, kn, cos_t, sin_t):
    n_tiles = T_ALL // TT
    n_ctx = T_CTX // TT
    per_req = DEC_SEQ // TT

    def rope_idx(i):
        return (jnp.where(i < n_ctx, per_req, (i - n_ctx) % per_req), 0)

    row = lambda w: pl.BlockSpec((TT, w), lambda i: (i, 0))
    const = lambda s: pl.BlockSpec(s, lambda i: (0,) * len(s))
    widths = (POOL_WIDTH, QB_W, KB_W, KB_W, C_W, C_W, C_W)
    dtypes = (F32, BF16, F32, F32, BF16, F32, F32)
    return pl.pallas_call(
        _proj_kernel,
        grid=(n_tiles,),
        in_specs=[row(D_MODEL), _mod_spec(0, TT), _mod_spec(1, TT), const((D_MODEL, PROJ_WIDTH)),
                  const((QB_W, QB_W)), const((1, QB_W)), const((1, KB_W)),
                  pl.BlockSpec((TT, LANES), rope_idx), pl.BlockSpec((TT, LANES), rope_idx)],
        out_specs=[row(w) for w in widths],
        out_shape=[jax.ShapeDtypeStruct((T_ALL, w), d) for w, d in zip(widths, dtypes)],
        scratch_shapes=[pltpu.VMEM((D_MODEL, PROJ_WIDTH), BF16)],
        compiler_params=pltpu.CompilerParams(dimension_semantics=("arbitrary",), vmem_limit_bytes=VMEM_LIMIT),
        name="proj",
    )(x, mod, mod, w_in, ones_bd, qn, kn, cos_t, sin_t)


def _pool_kernel(a_ref, w_ref, scale_ref, o_ref):
    i = pl.program_id(0)
    a = a_ref[...]
    row = lax.broadcasted_iota(jnp.int32, a.shape, 0)
    lane = lax.broadcasted_iota(jnp.int32, a.shape, 1)
    seq_m1 = jnp.where(i < T_CTX // TP, SEQ - 1, DEC_SEQ - 1)
    t = row & seq_m1

    def shifted(k):
        v = pltpu.roll(a, (TP - k) % TP, axis=0) if k else a
        ok = (t + k >= 0) & (t + k <= seq_m1)
        return jnp.where(ok, v, 0.0)

    s = {k: shifted(k) for k in range(-8, 8)}
    c2 = s[-1] + s[0]
    c4 = c2 + s[-2] + s[1]
    c8 = c4 + s[-4] + s[-3] + s[2] + s[3]
    c16 = c8 + s[-8] + s[-7] + s[-6] + s[-5] + s[4] + s[5] + s[6] + s[7]
    g = lane >> 6
    csum = jnp.where(g == 0, c2, jnp.where(g == 1, c4, jnp.where(g == 2, c8, c16)))
    half = jnp.where(g == 0, 1, jnp.where(g == 1, 2, jnp.where(g == 2, 4, 8)))
    cnt = jnp.minimum(t + half, seq_m1 + 1) - jnp.maximum(t - half, 0)
    pooled = csum / cnt.astype(F32) - a
    y = _dot(pooled.astype(BF16), w_ref[...].astype(BF16)) * scale_ref[...]
    o_ref[...] = y.astype(BF16)


def _pool(a, w_bd, scale):
    return pl.pallas_call(
        _pool_kernel,
        grid=(T_ALL // TP,),
        in_specs=[pl.BlockSpec((TP, POOL_WIDTH), lambda i: (i, 0)),
                  pl.BlockSpec((POOL_WIDTH, POOL_WIDTH), lambda i: (0, 0)),
                  pl.BlockSpec((1, POOL_WIDTH), lambda i: (0, 0))],
        out_specs=pl.BlockSpec((TP, POOL_WIDTH), lambda i: (i, 0)),
        out_shape=jax.ShapeDtypeStruct((T_ALL, POOL_WIDTH), BF16),
        compiler_params=pltpu.CompilerParams(dimension_semantics=("arbitrary",), vmem_limit_bytes=VMEM_LIMIT),
        name="pool",
    )(a, w_bd, scale)


def _softmax_pv(scores, values):
    m = scores[0].max(axis=-1, keepdims=True)
    for s in scores[1:]:
        m = jnp.maximum(m, s.max(axis=-1, keepdims=True))
    ps = [jnp.exp(s - m) for s in scores]
    l = ps[0].sum(axis=-1, keepdims=True)
    for p in ps[1:]:
        l = l + p.sum(axis=-1, keepdims=True)
    r = 1.0 / l
    o = _dot((ps[0] * r).astype(BF16), values[0])
    for p, v in zip(ps[1:], values[1:]):
        o = o + _dot((p * r).astype(BF16), v)
    return o


def _lane_halves():
    lane = lax.broadcasted_iota(jnp.int32, (1, LANES), 1)
    return lane < HEAD_DIM, lane >= HEAD_DIM


def _keep(x, mask):
    return jnp.where(mask, x, 0.0).astype(BF16)


def _gqa_variants(x):
    lo, hi = _lane_halves()
    xs = pltpu.roll(x, HEAD_DIM, axis=1)
    nat_lo, nat_hi = _keep(x, lo), _keep(x, hi)
    sw_lo, sw_hi = _keep(xs, lo), _keep(xs, hi)
    return ((nat_lo, sw_hi), (nat_lo, nat_hi), (sw_lo, nat_hi))


def _mha_variants(x):
    lo, hi = _lane_halves()
    out = []
    for j in range(x.shape[1] // LANES):
        xc = x[:, j * LANES:(j + 1) * LANES]
        out.append((_keep(xc, lo), _keep(xc, hi)))
    return tuple(out)


def _attend_cols(q, k_vars, v_vars, extra_k=None, extra_v=None, bias=None):
    cols = []
    for j in range(q.shape[1] // LANES):
        qc = q[:, j * LANES:(j + 1) * LANES]
        o = None
        for hh in range(2):
            s = _dot_nt(qc, k_vars[j][hh])
            if bias is not None:
                s = s + bias[j][hh]
            scores, values = [s], [v_vars[j][hh]]
            if extra_k is not None:
                scores.append(_dot_nt(qc, extra_k[j][hh]))
                values.append(extra_v[j][hh])
            oh = _softmax_pv(scores, values)
            o = oh if o is None else o + oh
        cols.append(o)
    return jnp.concatenate(cols, axis=1)


def _attn_ctx_kernel(qb_ref, kb_ref, vb_ref, qc_ref, kc_ref, vc_ref, yb_ref, yc_ref):
    yb = _attend_cols(qb_ref[...], _gqa_variants(kb_ref[...]), _gqa_variants(vb_ref[...]))
    yb_ref[...] = yb.astype(BF16)
    yc = _attend_cols(qc_ref[...], _mha_variants(kc_ref[...]), _mha_variants(vc_ref[...]))
    yc_ref[...] = yc.astype(BF16)


def _attn_ctx(qb, kb, vb, qc, kc, vc):
    row = lambda w: pl.BlockSpec((SEQ, w), lambda i: (i, 0))
    return pl.pallas_call(
        _attn_ctx_kernel,
        grid=(BATCH,),
        in_specs=[row(QB_W), row(KB_W), row(KB_W), row(C_W), row(C_W), row(C_W)],
        out_specs=[row(QB_W), row(C_W)],
        out_shape=[jax.ShapeDtypeStruct((T_CTX, QB_W), BF16), jax.ShapeDtypeStruct((T_CTX, C_W), BF16)],
        compiler_params=pltpu.CompilerParams(dimension_semantics=("arbitrary",), vmem_limit_bytes=VMEM_LIMIT),
        name="attn_ctx",
    )(qb, kb, vb, qc, kc, vc)


def _attn_latb_kernel(q_ref, k_ref, v_ref, ck_ref, cv_ref, y_ref):
    y = _attend_cols(q_ref[...], _gqa_variants(ck_ref[0, 0]), _gqa_variants(cv_ref[0, 0]),
                     extra_k=_gqa_variants(k_ref[...]), extra_v=_gqa_variants(v_ref[...]))
    y_ref[...] = y.astype(BF16)


def _attn_latb(layer, qb, kb, vb, cache_k, cache_v):
    ctx_h = T_CTX // HALF
    ctx_r = T_CTX // DEC_SEQ
    cache = pl.BlockSpec((1, 1, PAST_LEN, KB_W), lambda b, s: (b, layer, 0, 0))
    own = pl.BlockSpec((DEC_SEQ, KB_W), lambda b, s: (ctx_r + b, 0))
    return pl.pallas_call(
        _attn_latb_kernel,
        grid=(DEC_BATCH, DEC_SEQ // HALF),
        in_specs=[pl.BlockSpec((HALF, QB_W), lambda b, s: (ctx_h + 2 * b + s, 0)), own, own, cache, cache],
        out_specs=pl.BlockSpec((HALF, QB_W), lambda b, s: (2 * b + s, 0)),
        out_shape=jax.ShapeDtypeStruct((T_LAT, QB_W), BF16),
        compiler_params=pltpu.CompilerParams(
            dimension_semantics=("arbitrary", "arbitrary"), vmem_limit_bytes=VMEM_LIMIT),
        name="attn_latb",
    )(qb, kb, vb, cache_k, cache_v)


def _natten_kernel(q_ref, k_ref, v_ref, ck_ref, cv_ref, tab_ref, y_ref):
    s = pl.program_id(1)
    start = pl.multiple_of(s * (4 * GRID_W), 4 * GRID_W)
    k = _mha_variants(k_ref[pl.ds(start, NAT_KEYS), :])
    v = _mha_variants(v_ref[pl.ds(start, NAT_KEYS), :])
    ck = _mha_variants(ck_ref[0, 0])
    cv = _mha_variants(cv_ref[0, 0])
    bias = ((tab_ref[0, 0, 0], tab_ref[0, 1, 0]),)
    y = _attend_cols(q_ref[...], k, v, extra_k=ck, extra_v=cv, bias=bias)
    y_ref[...] = y.astype(BF16)


def _natten(layer, qc, kc, vc, cache_k, cache_v, table):
    ctx_h = T_CTX // HALF
    ctx_r = T_CTX // DEC_SEQ
    cache = pl.BlockSpec((1, 1, PAST_LEN, LANES), lambda j, s, b: (b, layer, 0, j))
    own = pl.BlockSpec((DEC_SEQ, LANES), lambda j, s, b: (ctx_r + b, j))
    return pl.pallas_call(
        _natten_kernel,
        grid=(C_W // LANES, DEC_SEQ // HALF, DEC_BATCH),
        in_specs=[pl.BlockSpec((HALF, LANES), lambda j, s, b: (ctx_h + 2 * b + s, j)), own, own, cache, cache,
                  pl.BlockSpec((1, 2, 1, HALF, NAT_KEYS), lambda j, s, b: (j, 0, s, 0, 0))],
        out_specs=pl.BlockSpec((HALF, LANES), lambda j, s, b: (2 * b + s, j)),
        out_shape=jax.ShapeDtypeStruct((T_LAT, C_W), BF16),
        compiler_params=pltpu.CompilerParams(
            dimension_semantics=("arbitrary", "arbitrary", "arbitrary"), vmem_limit_bytes=VMEM_LIMIT),
        name="natten",
    )(qc, kc, vc, cache_k, cache_v, table)


def _natten_table(rpb):
    qcol = np.arange(GRID_W)[:, None]
    kcol = np.arange(GRID_W)[None, :]
    cs = np.clip(qcol - WIN_C // 2, 0, GRID_W - WIN_C)
    col_ok = (kcol >= cs) & (kcol < cs + WIN_C)
    dc = np.clip(kcol - qcol, -(WIN_C - 1), WIN_C - 1) + (WIN_C - 1)
    blocks = jnp.full((H_C, 2 * WIN_R - 1, GRID_W, GRID_W), NEG, F32)
    for m in range(2 * WIN_C - 1):
        sel = jnp.asarray((dc == m) & col_ok)[None, None]
        blocks = jnp.where(sel, rpb[:, :, m][:, :, None, None], blocks)
    masked = jnp.full((H_C, GRID_W, GRID_W), NEG, F32)
    halves = []
    for s in range(2):
        rows = []
        for i in range(HALF // GRID_W):
            r = (HALF // GRID_W) * s + i
            rs = min(max(r - WIN_R // 2, 0), GRID_ROWS - WIN_R)
            row = []
            for j in range(NAT_KEYS // GRID_W):
                rk = 4 * s + j
                row.append(blocks[:, rk - r + WIN_R - 1] if rs <= rk < rs + WIN_R else masked)
            rows.append(jnp.concatenate(row, axis=2))
        halves.append(jnp.concatenate(rows, axis=1))
    tab = jnp.stack(halves, axis=1)
    return tab.reshape(H_C // 2, 2, 2, HALF, NAT_KEYS)


def _route(lg):
    lane = lax.broadcasted_iota(jnp.int32, lg.shape, 1).astype(F32)
    low = jnp.float32(-3.0e38)
    far = jnp.float32(ROUTER_LANES)
    is_g = lane < N_GROUPS
    gmax = jnp.where(is_g, lg, low).max(axis=-1, keepdims=True)
    gsel = jnp.where(is_g & (lg == gmax), lane, far).min(axis=-1, keepdims=True)
    pg_sel = 1.0 / jnp.where(is_g, jnp.exp(lg - gmax), 0.0).sum(axis=-1, keepdims=True)
    e0 = EXPERT_LANE0 + EXPERTS_PER_GROUP * gsel
    in_g = (lane >= e0) & (lane < e0 + EXPERTS_PER_GROUP)
    m1 = jnp.where(in_g, lg, low).max(axis=-1, keepdims=True)
    i1 = jnp.where(in_g & (lg == m1), lane, far).min(axis=-1, keepdims=True)
    rest = in_g & (lane != i1)
    m2 = jnp.where(rest, lg, low).max(axis=-1, keepdims=True)
    i2 = jnp.where(rest & (lg == m2), lane, far).min(axis=-1, keepdims=True)
    t = jnp.exp(m2 - m1)
    ssum = 1.0 + t
    w1 = (1.0 / ssum) * pg_sel
    w2 = (t / ssum) * pg_sel
    return i1 - EXPERT_LANE0, i2 - EXPERT_LANE0, w1, w2


def _pack_bf16_pairs(x):
    u = pltpu.bitcast(x.astype(BF16).astype(F32), jnp.uint32)
    return (u[:, :PACK_W] >> 16) | u[:, PACK_W:]


def _unpack_bf16_pairs(words):
    lo = [pltpu.bitcast(w << 16, F32) for w in words]
    hi = [pltpu.bitcast(w & jnp.uint32(0xFFFF0000), F32) for w in words]
    return jnp.concatenate(lo + hi, axis=1)


def _rows_to_groups(words, stage_ref, out_ref, n_rows):
    stride = n_rows + 8
    for c in range(GROUP_ROWS):
        stage_ref[c * stride:c * stride + n_rows, :] = words[:, c * LANES:(c + 1) * LANES]
    for j in range(n_rows):
        out_ref[GROUP_ROWS * j:GROUP_ROWS * (j + 1), :] = stage_ref[pl.ds(j, GROUP_ROWS, stride=stride), :]


def _tail_kernel(x_ref, ya_ref, yb_ref, yc_ref, w_ref, g1_ref, sh2_ref, sc2_ref, lg_ref, lb_ref, wr_ref, br_ref,
                 x1_ref, h2p_ref, wts_ref, ids_ref, wbf_ref, stage_ref):
    @pl.when(pl.program_id(0) == 0)
    def _():
        wbf_ref[...] = w_ref[...].astype(BF16)

    o = (_dot(ya_ref[...], wbf_ref[0:POOL_WIDTH, :])
         + _dot(yb_ref[...], wbf_ref[POOL_WIDTH:POOL_WIDTH + QB_W, :])
         + _dot(yc_ref[...], wbf_ref[POOL_WIDTH + QB_W:, :]))
    x1 = _layer_norm(ALPHA * x_ref[...] + g1_ref[0] * o, lg_ref[...], lb_ref[...])
    x1_ref[...] = x1
    h2 = x1 * (1.0 + sc2_ref[0]) + sh2_ref[0]
    hh, hl = _split_bf16(h2)
    wh, wl = _split_bf16(wr_ref[...])
    lg = _dot(hh, wh) + _dot(hl, wh) + _dot(hh, wl) + br_ref[...]
    i1, i2, w1, w2 = _route(lg)
    lane = lax.broadcasted_iota(jnp.int32, lg.shape, 1)
    wts_ref[...] = jnp.where(lane == 0, w1, jnp.where(lane == 1, w2, 0.0))
    ids = jnp.where(lane == 0, i1, jnp.where(lane == 1, i2, 0.0))
    ids_ref[0] = ids.T[0:8, :].astype(jnp.int32)
    _rows_to_groups(_pack_bf16_pairs(h2), stage_ref, h2p_ref, TT)


def _tail(x, ya, yb, yc, w_out, mod, ln_g, ln_b, wr, br):
    row = lambda w: pl.BlockSpec((TT, w), lambda i: (i, 0))
    const = lambda s: pl.BlockSpec(s, lambda i: (0,) * len(s))
    return pl.pallas_call(
        _tail_kernel,
        grid=(T_ALL // TT,),
        in_specs=[row(D_MODEL), row(POOL_WIDTH), row(QB_W), row(C_W), const((D_MODEL, D_MODEL)),
                  _mod_spec(2, TT), _mod_spec(3, TT), _mod_spec(4, TT),
                  const((1, D_MODEL)), const((1, D_MODEL)), const((D_MODEL, ROUTER_LANES)), const((1, ROUTER_LANES))],
        out_specs=[row(D_MODEL), pl.BlockSpec((TT * GROUP_ROWS, LANES), lambda i: (i, 0)), row(LANES),
                   pl.BlockSpec((1, 8, TT), lambda i: (i, 0, 0))],
        out_shape=[jax.ShapeDtypeStruct((T_ALL, D_MODEL), F32),
                   jax.ShapeDtypeStruct((T_ALL * GROUP_ROWS, LANES), jnp.uint32),
                   jax.ShapeDtypeStruct((T_ALL, LANES), F32),
                   jax.ShapeDtypeStruct((T_ALL // TT, 8, TT), jnp.int32)],
        scratch_shapes=[pltpu.VMEM((D_MODEL, D_MODEL), BF16),
                        pltpu.VMEM((GROUP_ROWS * (TT + 8), LANES), jnp.uint32)],
        compiler_params=pltpu.CompilerParams(dimension_semantics=("arbitrary",), vmem_limit_bytes=VMEM_LIMIT),
        name="tail",
    )(x, ya, yb, yc, w_out, mod, mod, mod, ln_g, ln_b, wr, br)


def _plan_kernel(eid_ref, pos_ref, tile_ref):
    e = eid_ref[0]
    r = lax.broadcasted_iota(jnp.int32, (LANES, LANES), 0)
    c = lax.broadcasted_iota(jnp.int32, (LANES, LANES), 1)
    upper = (r <= c).astype(BF16)
    rows = e.shape[0]
    lower = (lax.broadcasted_iota(jnp.int32, (rows, rows), 1)
             < lax.broadcasted_iota(jnp.int32, (rows, rows), 0)).astype(BF16)
    tile_start = (lax.broadcasted_iota(jnp.int32, (1, LANES), 1) * TMX).astype(F32)
    pos = jnp.zeros(e.shape, F32)
    base = jnp.zeros((1, LANES), F32)
    n_before = jnp.zeros((1, LANES), F32)
    for ex in range(N_EXPERTS):
        m = e == ex
        incl = _dot(m.astype(BF16), upper)
        row_tot = jnp.broadcast_to(incl[:, LANES - 1:LANES], incl.shape)
        row_off = _dot(lower, row_tot.astype(BF16))
        cnt = jnp.sum(row_tot, axis=0, keepdims=True)
        pos = jnp.where(m, base + row_off + incl - 1.0, pos)
        base = base + jnp.ceil(cnt * (1.0 / TMX)) * TMX
        n_before = n_before + (tile_start >= base).astype(F32)
    pos_ref[0] = pos.astype(jnp.int32)
    sub = lax.broadcasted_iota(jnp.int32, (8, LANES), 0)
    tile_expert = jnp.broadcast_to(jnp.minimum(n_before, N_EXPERTS - 1.0), (8, LANES))
    n_tiles = jnp.broadcast_to(base * (1.0 / TMX), (8, LANES))
    tile_ref[0] = jnp.where(sub == 0, tile_expert, jnp.where(sub == 1, n_tiles, 0.0)).astype(jnp.int32)


def _plan(eid):
    rows = 2 * T_STREAM // LANES
    return pl.pallas_call(
        _plan_kernel,
        grid=(N_STREAMS,),
        in_specs=[pl.BlockSpec((1, rows, LANES), lambda s: (s, 0, 0))],
        out_specs=[pl.BlockSpec((1, rows, LANES), lambda s: (s, 0, 0)), pl.BlockSpec((1, 8, LANES), lambda s: (s, 0, 0))],
        out_shape=[jax.ShapeDtypeStruct((N_STREAMS, rows, LANES), jnp.int32),
                   jax.ShapeDtypeStruct((N_STREAMS, 8, LANES), jnp.int32)],
        compiler_params=pltpu.CompilerParams(dimension_semantics=("arbitrary",), vmem_limit_bytes=VMEM_LIMIT),
        name="plan",
    )(eid)


def _moe_kernel(pos_ref, texp_ref, ntile_ref, h2p_hbm, wg_ref, wu_ref, wd_ref, y_hbm,
                h2v, yv, inv, xg, yg, wgb, wub, wdb, sem):
    s = pl.program_id(0)
    t = pl.program_id(1)
    n_pairs = 2 * T_STREAM
    trash = n_pairs

    def h2_copy():
        return pltpu.make_async_copy(
            h2p_hbm.at[pl.ds(pl.multiple_of(s * (T_STREAM * GROUP_ROWS), 8), T_STREAM * GROUP_ROWS)], h2v, sem.at[0])

    def y_copy():
        return pltpu.make_async_copy(yv.at[pl.ds(0, n_pairs * GROUP_ROWS)], y_hbm.at[s], sem.at[1])

    @pl.when(t == 0)
    def _():
        h2_copy().start()

        def fill(i, carry):
            for u in range(8):
                inv[i * 8 + u] = trash
            return carry
        lax.fori_loop(0, P_MAX // 8, fill, 0)

        def scatter(i, carry):
            for u in range(8):
                p = i * 8 + u
                inv[pos_ref[s * n_pairs + p]] = p
            return carry
        lax.fori_loop(0, n_pairs // 8, scatter, 0)
        h2_copy().wait()

    ex = texp_ref[s * LANES + t]
    prev = texp_ref[s * LANES + jnp.maximum(t - 1, 0)]
    active = t < ntile_ref[s]

    @pl.when(active & ((t == 0) | (ex != prev)))
    def _():
        wgb[...] = wg_ref[0].astype(BF16)
        wub[...] = wu_ref[0].astype(BF16)
        wdb[...] = wd_ref[0].astype(BF16)

    @pl.when(active)
    def _():
        base = t * TMX
        for j in range(TMX):
            tok = inv[base + j] & (T_STREAM - 1)
            src = pl.multiple_of(tok * GROUP_ROWS, GROUP_ROWS)
            xg[pl.ds(j, GROUP_ROWS, stride=XSTRIDE), :] = h2v[pl.ds(src, GROUP_ROWS), :]
        x = _unpack_bf16_pairs([xg[c * XSTRIDE:c * XSTRIDE + TMX, :] for c in range(GROUP_ROWS)]).astype(BF16)
        hg = _dot(x, wgb[...])
        hu = _dot(x, wub[...])
        act = hg * jax.nn.sigmoid(hg) * hu
        y = _dot(act.astype(BF16), wdb[...])
        words = _pack_bf16_pairs(y)
        for c in range(GROUP_ROWS):
            yg[c * XSTRIDE:c * XSTRIDE + TMX, :] = words[:, c * LANES:(c + 1) * LANES]
        for j in range(TMX):
            dst = pl.multiple_of(inv[base + j] * GROUP_ROWS, GROUP_ROWS)
            yv[pl.ds(dst, GROUP_ROWS), :] = yg[pl.ds(j, GROUP_ROWS, stride=XSTRIDE), :]

    @pl.when(t == NT_MAX - 1)
    def _():
        y_copy().start()
        y_copy().wait()


def _moe(pos, texp, ntile, h2p, w_gate, w_up, w_down):
    wspec = lambda shape: pl.BlockSpec(shape, lambda s, t, pos, texp, ntile: (texp[s * LANES + t], 0, 0))
    n_pairs = 2 * T_STREAM
    return pl.pallas_call(
        _moe_kernel,
        grid_spec=pltpu.PrefetchScalarGridSpec(
            num_scalar_prefetch=3,
            grid=(N_STREAMS, NT_MAX),
            in_specs=[pl.BlockSpec(memory_space=pl.ANY),
                      wspec((1, D_MODEL, D_EXPERT)), wspec((1, D_MODEL, D_EXPERT)), wspec((1, D_EXPERT, D_MODEL))],
            out_specs=pl.BlockSpec(memory_space=pl.ANY),
            scratch_shapes=[pltpu.VMEM((T_STREAM * GROUP_ROWS, LANES), jnp.uint32),
                            pltpu.VMEM(((n_pairs + 2) * GROUP_ROWS, LANES), jnp.uint32),
                            pltpu.SMEM((P_MAX,), jnp.int32),
                            pltpu.VMEM((GROUP_ROWS * XSTRIDE, LANES), jnp.uint32),
                            pltpu.VMEM((GROUP_ROWS * XSTRIDE, LANES), jnp.uint32),
                            pltpu.VMEM((D_MODEL, D_EXPERT), BF16), pltpu.VMEM((D_MODEL, D_EXPERT), BF16),
                            pltpu.VMEM((D_EXPERT, D_MODEL), BF16),
                            pltpu.SemaphoreType.DMA((2,))]),
        out_shape=jax.ShapeDtypeStruct((N_STREAMS, n_pairs * GROUP_ROWS, LANES), jnp.uint32),
        compiler_params=pltpu.CompilerParams(
            dimension_semantics=("arbitrary", "arbitrary"), vmem_limit_bytes=VMEM_LIMIT),
        name="moe",
    )(pos, texp, ntile, h2p, w_gate, w_up, w_down)


def _groups_to_rows(src_ref, stage_ref, n_rows):
    stride = n_rows + 8
    for j in range(n_rows):
        stage_ref[pl.ds(j, GROUP_ROWS, stride=stride), :] = src_ref[0, GROUP_ROWS * j:GROUP_ROWS * (j + 1), :]
    return _unpack_bf16_pairs([stage_ref[c * stride:c * stride + n_rows, :] for c in range(GROUP_ROWS)])


def _final_kernel(x1_ref, y0_ref, y1_ref, wts_ref, g2_ref, lg_ref, lb_ref, o_ref, st0_ref, st1_ref):
    y0 = _groups_to_rows(y0_ref, st0_ref, TT)
    y1 = _groups_to_rows(y1_ref, st1_ref, TT)
    wts = wts_ref[...]
    moe = wts[:, 0:1] * y0 + wts[:, 1:2] * y1
    o_ref[...] = _layer_norm(ALPHA * x1_ref[...] + g2_ref[0] * moe, lg_ref[...], lb_ref[...])


def _final(x1, y, wts, mod, ln_g, ln_b):
    row = lambda w: pl.BlockSpec((TT, w), lambda i: (i, 0))
    const = lambda s: pl.BlockSpec(s, lambda i: (0,) * len(s))
    per_stream = T_STREAM // TT
    slot = lambda k: pl.BlockSpec((1, TT * GROUP_ROWS, LANES),
                                  lambda i: (i // per_stream, k * per_stream + i % per_stream, 0))
    return pl.pallas_call(
        _final_kernel,
        grid=(T_ALL // TT,),
        in_specs=[row(D_MODEL), slot(0), slot(1), row(LANES), _mod_spec(5, TT), const((1, D_MODEL)), const((1, D_MODEL))],
        out_specs=row(D_MODEL),
        out_shape=jax.ShapeDtypeStruct((T_ALL, D_MODEL), F32),
        scratch_shapes=[pltpu.VMEM((GROUP_ROWS * (TT + 8), LANES), jnp.uint32),
                        pltpu.VMEM((GROUP_ROWS * (TT + 8), LANES), jnp.uint32)],
        compiler_params=pltpu.CompilerParams(dimension_semantics=("arbitrary",), vmem_limit_bytes=VMEM_LIMIT),
        name="final",
    )(x1, y, y, wts, mod, ln_g, ln_b)


def _rope_tables():
    t = np.arange(DEC_SEQ)
    pos = np.stack([t // GRID_W, t % GRID_W], axis=1).astype(np.float32)
    nf = HEAD_DIM // 4
    inv = jnp.asarray(ROPE_THETA, F32) ** (-jnp.arange(nf, dtype=F32) / nf)
    d = np.arange(LANES) % HEAD_DIM
    which = d // (HEAD_DIM // 2)
    ang = jnp.asarray(pos)[:, which] * inv[d % nf][None, :]
    sign = np.where((d % 32) < 16, -1.0, 1.0).astype(np.float32)
    cos = jnp.concatenate([jnp.cos(ang), jnp.ones((TT, LANES), F32)], axis=0)
    sin = jnp.concatenate([jnp.sin(ang) * sign[None, :], jnp.zeros((TT, LANES), F32)], axis=0)
    return cos, sin


def _block_ones():
    h = np.arange(QB_W) // HEAD_DIM
    return jnp.asarray((h[:, None] == h[None, :]).astype(np.float32), dtype=BF16)


def _pool_block_diag(pool_w):
    out = jnp.zeros((POOL_WIDTH, POOL_WIDTH), F32)
    for g in range(4):
        out = out.at[64 * g:64 * g + 64, 64 * g:64 * g + 64].set(pool_w[g])
    return out


def kernel(x_prompt, x_sample, cache_b_k, cache_b_v, cache_c_k, cache_c_v, c, c_ctx, w_ada, b_ada, w_in, w_out,
           pool_w, pool_scale, q_norm, k_norm, rpb, ln1_g, ln1_b, ln2_g, ln2_b, router_g, router_g_b, router_e,
           router_e_b, w_gate, w_up, w_down):
    x = jnp.concatenate([x_prompt.reshape(T_CTX, D_MODEL), x_sample.reshape(T_LAT, D_MODEL)], axis=0)
    c8 = jnp.concatenate([c, c_ctx[None], jnp.zeros((MOD_ROWS - DEC_BATCH - 1, D_MODEL), F32)], axis=0)
    mod_all = _adaln(c8, w_ada, b_ada)

    cos_t, sin_t = _rope_tables()
    ones_bd = _block_ones()
    cbk = cache_b_k.reshape(DEC_BATCH, DEPTH, PAST_LEN, KB_W)
    cbv = cache_b_v.reshape(DEC_BATCH, DEPTH, PAST_LEN, KB_W)
    cck = cache_c_k.reshape(DEC_BATCH, DEPTH, PAST_LEN, C_W)
    ccv = cache_c_v.reshape(DEC_BATCH, DEPTH, PAST_LEN, C_W)
    pad = jnp.zeros((D_MODEL, ROUTER_LANES - N_GROUPS - N_EXPERTS), F32)

    new_bk, new_bv, new_ck, new_cv = [], [], [], []
    for l in range(DEPTH):
        mod = mod_all[l].reshape(MOD_ROWS * 6, 1, D_MODEL)
        qn = jnp.tile(q_norm[l], H_B)[None]
        kn = jnp.tile(k_norm[l], KV_B)[None]
        a, qb, kb, vb, qc, kc, vc = _proj(x, mod, w_in[l], ones_bd, qn, kn, cos_t, sin_t)
        ya = _pool(a, _pool_block_diag(pool_w[l]), pool_scale[l][None])
        yb_ctx, yc_ctx = _attn_ctx(qb, kb, vb, qc, kc, vc)
        yb_lat = _attn_latb(l, qb, kb, vb, cbk, cbv)
        yc_lat = _natten(l, qc, kc, vc, cck, ccv, _natten_table(rpb[l]))
        yb = jnp.concatenate([yb_ctx, yb_lat], axis=0)
        yc = jnp.concatenate([yc_ctx, yc_lat], axis=0)
        wr = jnp.concatenate([router_g[l], router_e[l], pad], axis=1)
        br = jnp.concatenate([router_g_b[l], router_e_b[l], pad[0]], axis=0)[None]
        x1, h2p, wts, ids = _tail(x, ya, yb, yc, w_out[l], mod, ln1_g[l][None], ln1_b[l][None], wr, br)
        eid = ids.reshape(N_STREAMS, T_STREAM // TT, 8, TT)[:, :, :2, :].transpose(0, 2, 1, 3)
        pos, tiles = _plan(eid.reshape(N_STREAMS, 2 * T_STREAM // LANES, LANES))
        y = _moe(pos.reshape(-1), tiles[:, 0, :].reshape(-1), tiles[:, 1, 0], h2p, w_gate[l], w_up[l], w_down[l])
        x = _final(x1, y, wts, mod, ln2_g[l][None], ln2_b[l][None])
        new_bk.append(kb[:T_CTX].reshape(BATCH, SEQ, KV_B, HEAD_DIM))
        new_bv.append(vb[:T_CTX].reshape(BATCH, SEQ, KV_B, HEAD_DIM))
        new_ck.append(kc[:T_CTX].reshape(BATCH, SEQ, H_C, HEAD_DIM))
        new_cv.append(vc[:T_CTX].reshape(BATCH, SEQ, H_C, HEAD_DIM))

    y_prompt = x[:T_CTX].reshape(BATCH, SEQ, D_MODEL)
    y_sample = x[T_CTX:].reshape(DEC_BATCH, DEC_SEQ, D_MODEL)
    return (y_prompt, y_sample, jnp.stack(new_bk, axis=1), jnp.stack(new_bv, axis=1),
            jnp.stack(new_ck, axis=1), jnp.stack(new_cv, axis=1))
```

```python
import functools

import numpy as np
import jax
import jax.numpy as jnp
from jax import lax
from jax.experimental import pallas as pl
from jax.experimental.pallas import tpu as pltpu

F32 = jnp.float32
BF16 = jnp.bfloat16

D_MODEL = 1024
BATCH = 16
SEQ = 256
DEPTH = 2
DEC_BATCH = 4
DEC_SEQ = 1024
PAST_LEN = 256
GRID_W = 64
GRID_ROWS = DEC_SEQ // GRID_W
HEAD_DIM = 64
POOL_WIDTH = 256
POOL_WINDOWS = (2, 4, 8, 16)
H_B = 6
KV_B = 2
H_C = 6
WIN_R = 8
WIN_C = 16
ROPE_THETA = 10000.0
QB_W = H_B * HEAD_DIM
KB_W = KV_B * HEAD_DIM
C_W = H_C * HEAD_DIM
PROJ_WIDTH = 2048
N_GROUPS = 4
EXPERTS_PER_GROUP = 8
N_EXPERTS = 32
D_EXPERT = 256
ALPHA = (2 * DEPTH) ** 0.25
LN_EPS = 1e-6
RMS_EPS = 1e-6
NEG = -1e30
ATTN_SCALE = HEAD_DIM ** -0.5

T_CTX = BATCH * SEQ
T_LAT = DEC_BATCH * DEC_SEQ
T_ALL = T_CTX + T_LAT

LANES = 128
ROUTER_LANES = 128
EXPERT_LANE0 = N_GROUPS
MOD_ROWS = 8
CTX_MOD_ROW = DEC_BATCH

TT = 512
TP = 1024
N_STREAMS = 2
T_STREAM = T_ALL // N_STREAMS
TMX = 128
P_MAX = 2 * T_STREAM + N_EXPERTS * TMX
NT_MAX = P_MAX // TMX
GROUP_ROWS = D_MODEL // LANES
XSTRIDE = TMX + 8
RMW_BATCH = 8
HALF = DEC_SEQ // 2
NAT_KEYS = 12 * GRID_W
VMEM_LIMIT = 56 * 1024 * 1024


def _dot(a, b):
    return jnp.dot(a, b, preferred_element_type=F32)


def _dot_nt(a, b):
    return lax.dot_general(a, b, (((1,), (1,)), ((), ())), preferred_element_type=F32)


def _split_bf16(x):
    hi = x.astype(BF16)
    lo = (x - hi.astype(F32)).astype(BF16)
    return hi, lo


def _layer_norm(y, g, b):
    mu = jnp.mean(y, axis=-1, keepdims=True)
    var = jnp.mean(jnp.square(y - mu), axis=-1, keepdims=True)
    return (y - mu) * lax.rsqrt(var + LN_EPS) * g + b


def _adaln_kernel(c_ref, w_ref, b_ref, o_ref):
    c = c_ref[...]
    s = (c * jax.nn.sigmoid(c)).astype(BF16)
    o_ref[0] = _dot(s, w_ref[0].astype(BF16)) + b_ref[0]


def _adaln(c8, w_ada, b_ada):
    tn = 1536
    n = w_ada.shape[-1]
    return pl.pallas_call(
        _adaln_kernel,
        grid=(DEPTH, n // tn),
        in_specs=[
            pl.BlockSpec((MOD_ROWS, D_MODEL), lambda l, j: (0, 0)),
            pl.BlockSpec((1, D_MODEL, tn), lambda l, j: (l, 0, j)),
            pl.BlockSpec((1, 1, tn), lambda l, j: (l, 0, j)),
        ],
        out_specs=pl.BlockSpec((1, MOD_ROWS, tn), lambda l, j: (l, 0, j)),
        out_shape=jax.ShapeDtypeStruct((DEPTH, MOD_ROWS, n), F32),
        compiler_params=pltpu.CompilerParams(
            dimension_semantics=("arbitrary", "arbitrary"), vmem_limit_bytes=VMEM_LIMIT),
        name="adaln",
    )(c8, w_ada, b_ada.reshape(DEPTH, 1, n))


def _mod_row_of_tile(i, tile):
    n_ctx = T_CTX // tile
    per_req = DEC_SEQ // tile
    return jnp.where(i < n_ctx, CTX_MOD_ROW, (i - n_ctx) // per_req)


def _mod_spec(chunk, tile):
    return pl.BlockSpec((1, 1, D_MODEL), lambda i: (_mod_row_of_tile(i, tile) * 6 + chunk, 0, 0))


def _rms_norm_heads(x, ones_bd, w):
    hi, lo = _split_bf16(x * x)
    ssq = _dot(hi, ones_bd) + _dot(lo, ones_bd)
    return x * lax.rsqrt(ssq * (1.0 / HEAD_DIM) + RMS_EPS) * w


def _rope_cols(x, cos, sin, first16):
    cols = []
    for j in range(x.shape[1] // LANES):
        xc = x[:, j * LANES:(j + 1) * LANES]
        partner = jnp.where(first16, pltpu.roll(xc, LANES - 16, axis=1), pltpu.roll(xc, 16, axis=1))
        cols.append(xc * cos + partner * sin)
    return jnp.concatenate(cols, axis=1) if len(cols) > 1 else cols[0]


def _proj_kernel(x_ref, sh_ref, sc_ref, w_ref, ones_ref, qn_ref, kn_ref, cos_ref, sin_ref,
                 a_ref, qb_ref, kb_ref, vb_ref, qc_ref, kc_ref, vc_ref, wbf_ref):
    @pl.when(pl.program_id(0) == 0)
    def _():
        wbf_ref[...] = w_ref[...].astype(BF16)

    h = x_ref[...] * (1.0 + sc_ref[0]) + sh_ref[0]
    p = _dot(h.astype(BF16), wbf_ref[...])
    o = 0
    a_ref[...] = p[:, o:o + POOL_WIDTH]; o += POOL_WIDTH
    qb = p[:, o:o + QB_W]; o += QB_W
    kb = p[:, o:o + KB_W]; o += KB_W
    vb_ref[...] = p[:, o:o + KB_W]; o += KB_W
    qc_ref[...] = (p[:, o:o + C_W] * ATTN_SCALE).astype(BF16); o += C_W
    kc_ref[...] = p[:, o:o + C_W]; o += C_W
    vc_ref[...] = p[:, o:o + C_W]

    ones_bd = ones_ref[...]
    cos = cos_ref[...]
    sin = sin_ref[...]
    lane = lax.broadcasted_iota(jnp.int32, (1, LANES), 1)
    first16 = (lane & 31) < 16
    qb = _rope_cols(_rms_norm_heads(qb, ones_bd, qn_ref[...]), cos, sin, first16)
    kb = _rope_cols(_rms_norm_heads(kb, ones_bd[:KB_W, :KB_W], kn_ref[...]), cos, sin, first16)
    qb_ref[...] = (qb * ATTN_SCALE).astype(BF16)
    kb_ref[...] = kb


def _proj(x, mod, w_in, ones_bd, qn, kn, cos_t, sin_t):
    n_tiles = T_ALL // TT
    n_ctx = T_CTX // TT
    per_req = DEC_SEQ // TT

    def rope_idx(i):
        return (jnp.where(i < n_ctx, per_req, (i - n_ctx) % per_req), 0)

    row = lambda w: pl.BlockSpec((TT, w), lambda i: (i, 0))
    const = lambda s: pl.BlockSpec(s, lambda i: (0,) * len(s))
    widths = (POOL_WIDTH, QB_W, KB_W, KB_W, C_W, C_W, C_W)
    dtypes = (F32, BF16, F32, F32, BF16, F32, F32)
    return pl.pallas_call(
        _proj_kernel,
        grid=(n_tiles,),
        in_specs=[row(D_MODEL), _mod_spec(0, TT), _mod_spec(1, TT), const((D_MODEL, PROJ_WIDTH)),
                  const((QB_W, QB_W)), const((1, QB_W)), const((1, KB_W)),
                  pl.BlockSpec((TT, LANES), rope_idx), pl.BlockSpec((TT, LANES), rope_idx)],
        out_specs=[row(w) for w in widths],
        out_shape=[jax.ShapeDtypeStruct((T_ALL, w), d) for w, d in zip(widths, dtypes)],
        scratch_shapes=[pltpu.VMEM((D_MODEL, PROJ_WIDTH), BF16)],
        compiler_params=pltpu.CompilerParams(dimension_semantics=("arbitrary",), vmem_limit_bytes=VMEM_LIMIT),
        name="proj",
    )(x, mod, mod, w_in, ones_bd, qn, kn, cos_t, sin_t)


def _pool_kernel(a_ref, w_ref, scale_ref, o_ref):
    i = pl.program_id(0)
    a = a_ref[...]
    row = lax.broadcasted_iota(jnp.int32, a.shape, 0)
    lane = lax.broadcasted_iota(jnp.int32, a.shape, 1)
    seq_m1 = jnp.where(i < T_CTX // TP, SEQ - 1, DEC_SEQ - 1)
    t = row & seq_m1

    def shifted(k):
        v = pltpu.roll(a, (TP - k) % TP, axis=0) if k else a
        ok = (t + k >= 0) & (t + k <= seq_m1)
        return jnp.where(ok, v, 0.0)

    s = {k: shifted(k) for k in range(-8, 8)}
    c2 = s[-1] + s[0]
    c4 = c2 + s[-2] + s[1]
    c8 = c4 + s[-4] + s[-3] + s[2] + s[3]
    c16 = c8 + s[-8] + s[-7] + s[-6] + s[-5] + s[4] + s[5] + s[6] + s[7]
    g = lane >> 6
    csum = jnp.where(g == 0, c2, jnp.where(g == 1, c4, jnp.where(g == 2, c8, c16)))
    half = jnp.where(g == 0, 1, jnp.where(g == 1, 2, jnp.where(g == 2, 4, 8)))
    cnt = jnp.minimum(t + half, seq_m1 + 1) - jnp.maximum(t - half, 0)
    pooled = csum / cnt.astype(F32) - a
    y = _dot(pooled.astype(BF16), w_ref[...].astype(BF16)) * scale_ref[...]
    o_ref[...] = y.astype(BF16)


def _pool(a, w_bd, scale):
    return pl.pallas_call(
        _pool_kernel,
        grid=(T_ALL // TP,),
        in_specs=[pl.BlockSpec((TP, POOL_WIDTH), lambda i: (i, 0)),
                  pl.BlockSpec((POOL_WIDTH, POOL_WIDTH), lambda i: (0, 0)),
                  pl.BlockSpec((1, POOL_WIDTH), lambda i: (0, 0))],
        out_specs=pl.BlockSpec((TP, POOL_WIDTH), lambda i: (i, 0)),
        out_shape=jax.ShapeDtypeStruct((T_ALL, POOL_WIDTH), BF16),
        compiler_params=pltpu.CompilerParams(dimension_semantics=("arbitrary",), vmem_limit_bytes=VMEM_LIMIT),
        name="pool",
    )(a, w_bd, scale)


def _softmax_pv(scores, values):
    m = scores[0].max(axis=-1, keepdims=True)
    for s in scores[1:]:
        m = jnp.maximum(m, s.max(axis=-1, keepdims=True))
    ps = [jnp.exp(s - m) for s in scores]
    l = ps[0].sum(axis=-1, keepdims=True)
    for p in ps[1:]:
        l = l + p.sum(axis=-1, keepdims=True)
    r = 1.0 / l
    o = _dot((ps[0] * r).astype(BF16), values[0])
    for p, v in zip(ps[1:], values[1:]):
        o = o + _dot((p * r).astype(BF16), v)
    return o


def _lane_halves():
    lane = lax.broadcasted_iota(jnp.int32, (1, LANES), 1)
    return lane < HEAD_DIM, lane >= HEAD_DIM


def _keep(x, mask):
    return jnp.where(mask, x, 0.0).astype(BF16)


def _gqa_variants(x):
    lo, hi = _lane_halves()
    xs = pltpu.roll(x, HEAD_DIM, axis=1)
    nat_lo, nat_hi = _keep(x, lo), _keep(x, hi)
    sw_lo, sw_hi = _keep(xs, lo), _keep(xs, hi)
    return ((nat_lo, sw_hi), (nat_lo, nat_hi), (sw_lo, nat_hi))


def _mha_variants(x):
    lo, hi = _lane_halves()
    out = []
    for j in range(x.shape[1] // LANES):
        xc = x[:, j * LANES:(j + 1) * LANES]
        out.append((_keep(xc, lo), _keep(xc, hi)))
    return tuple(out)


def _attend_cols(q, k_vars, v_vars, extra_k=None, extra_v=None, bias=None):
    cols = []
    for j in range(q.shape[1] // LANES):
        qc = q[:, j * LANES:(j + 1) * LANES]
        o = None
        for hh in range(2):
            s = _dot_nt(qc, k_vars[j][hh])
            if bias is not None:
                s = s + bias[j][hh]
            scores, values = [s], [v_vars[j][hh]]
            if extra_k is not None:
                scores.append(_dot_nt(qc, extra_k[j][hh]))
                values.append(extra_v[j][hh])
            oh = _softmax_pv(scores, values)
            o = oh if o is None else o + oh
        cols.append(o)
    return jnp.concatenate(cols, axis=1)


def _attn_ctx_kernel(qb_ref, kb_ref, vb_ref, qc_ref, kc_ref, vc_ref, yb_ref, yc_ref):
    yb = _attend_cols(qb_ref[...], _gqa_variants(kb_ref[...]), _gqa_variants(vb_ref[...]))
    yb_ref[...] = yb.astype(BF16)
    yc = _attend_cols(qc_ref[...], _mha_variants(kc_ref[...]), _mha_variants(vc_ref[...]))
    yc_ref[...] = yc.astype(BF16)


def _attn_ctx(qb, kb, vb, qc, kc, vc):
    row = lambda w: pl.BlockSpec((SEQ, w), lambda i: (i, 0))
    return pl.pallas_call(
        _attn_ctx_kernel,
        grid=(BATCH,),
        in_specs=[row(QB_W), row(KB_W), row(KB_W), row(C_W), row(C_W), row(C_W)],
        out_specs=[row(QB_W), row(C_W)],
        out_shape=[jax.ShapeDtypeStruct((T_CTX, QB_W), BF16), jax.ShapeDtypeStruct((T_CTX, C_W), BF16)],
        compiler_params=pltpu.CompilerParams(dimension_semantics=("arbitrary",), vmem_limit_bytes=VMEM_LIMIT),
        name="attn_ctx",
    )(qb, kb, vb, qc, kc, vc)


def _attn_latb_kernel(q_ref, k_ref, v_ref, ck_ref, cv_ref, y_ref):
    y = _attend_cols(q_ref[...], _gqa_variants(ck_ref[0, 0]), _gqa_variants(cv_ref[0, 0]),
                     extra_k=_gqa_variants(k_ref[...]), extra_v=_gqa_variants(v_ref[...]))
    y_ref[...] = y.astype(BF16)


def _attn_latb(layer, qb, kb, vb, cache_k, cache_v):
    ctx_h = T_CTX // HALF
    ctx_r = T_CTX // DEC_SEQ
    cache = pl.BlockSpec((1, 1, PAST_LEN, KB_W), lambda b, s: (b, layer, 0, 0))
    own = pl.BlockSpec((DEC_SEQ, KB_W), lambda b, s: (ctx_r + b, 0))
    return pl.pallas_call(
        _attn_latb_kernel,
        grid=(DEC_BATCH, DEC_SEQ // HALF),
        in_specs=[pl.BlockSpec((HALF, QB_W), lambda b, s: (ctx_h + 2 * b + s, 0)), own, own, cache, cache],
        out_specs=pl.BlockSpec((HALF, QB_W), lambda b, s: (2 * b + s, 0)),
        out_shape=jax.ShapeDtypeStruct((T_LAT, QB_W), BF16),
        compiler_params=pltpu.CompilerParams(
            dimension_semantics=("arbitrary", "arbitrary"), vmem_limit_bytes=VMEM_LIMIT),
        name="attn_latb",
    )(qb, kb, vb, cache_k, cache_v)


def _natten_kernel(q_ref, k_ref, v_ref, ck_ref, cv_ref, tab_ref, y_ref):
    s = pl.program_id(1)
    start = pl.multiple_of(s * (4 * GRID_W), 4 * GRID_W)
    k = _mha_variants(k_ref[pl.ds(start, NAT_KEYS), :])
    v = _mha_variants(v_ref[pl.ds(start, NAT_KEYS), :])
    ck = _mha_variants(ck_ref[0, 0])
    cv = _mha_variants(cv_ref[0, 0])
    bias = ((tab_ref[0, 0, 0], tab_ref[0, 1, 0]),)
    y = _attend_cols(q_ref[...], k, v, extra_k=ck, extra_v=cv, bias=bias)
    y_ref[...] = y.astype(BF16)


def _natten(layer, qc, kc, vc, cache_k, cache_v, table):
    ctx_h = T_CTX // HALF
    ctx_r = T_CTX // DEC_SEQ
    cache = pl.BlockSpec((1, 1, PAST_LEN, LANES), lambda j, s, b: (b, layer, 0, j))
    own = pl.BlockSpec((DEC_SEQ, LANES), lambda j, s, b: (ctx_r + b, j))
    return pl.pallas_call(
        _natten_kernel,
        grid=(C_W // LANES, DEC_SEQ // HALF, DEC_BATCH),
        in_specs=[pl.BlockSpec((HALF, LANES), lambda j, s, b: (ctx_h + 2 * b + s, j)), own, own, cache, cache,
                  pl.BlockSpec((1, 2, 1, HALF, NAT_KEYS), lambda j, s, b: (j, 0, s, 0, 0))],
        out_specs=pl.BlockSpec((HALF, LANES), lambda j, s, b: (2 * b + s, j)),
        out_shape=jax.ShapeDtypeStruct((T_LAT, C_W), BF16),
        compiler_params=pltpu.CompilerParams(
            dimension_semantics=("arbitrary", "arbitrary", "arbitrary"), vmem_limit_bytes=VMEM_LIMIT),
        name="natten",
    )(qc, kc, vc, cache_k, cache_v, table)


def _natten_table(rpb):
    qcol = np.arange(GRID_W)[:, None]
    kcol = np.arange(GRID_W)[None, :]
    cs = np.clip(qcol - WIN_C // 2, 0, GRID_W - WIN_C)
    col_ok = (kcol >= cs) & (kcol < cs + WIN_C)
    dc = np.clip(kcol - qcol, -(WIN_C - 1), WIN_C - 1) + (WIN_C - 1)
    blocks = jnp.full((H_C, 2 * WIN_R - 1, GRID_W, GRID_W), NEG, F32)
    for m in range(2 * WIN_C - 1):
        sel = jnp.asarray((dc == m) & col_ok)[None, None]
        blocks = jnp.where(sel, rpb[:, :, m][:, :, None, None], blocks)
    masked = jnp.full((H_C, GRID_W, GRID_W), NEG, F32)
    halves = []
    for s in range(2):
        rows = []
        for i in range(HALF // GRID_W):
            r = (HALF // GRID_W) * s + i
            rs = min(max(r - WIN_R // 2, 0), GRID_ROWS - WIN_R)
            row = []
            for j in range(NAT_KEYS // GRID_W):
                rk = 4 * s + j
                row.append(blocks[:, rk - r + WIN_R - 1] if rs <= rk < rs + WIN_R else masked)
            rows.append(jnp.concatenate(row, axis=2))
        halves.append(jnp.concatenate(rows, axis=1))
    tab = jnp.stack(halves, axis=1)
    return tab.reshape(H_C // 2, 2, 2, HALF, NAT_KEYS)


def _route(lg):
    lane = lax.broadcasted_iota(jnp.int32, lg.shape, 1).astype(F32)
    low = jnp.float32(-3.0e38)
    far = jnp.float32(ROUTER_LANES)
    is_g = lane < N_GROUPS
    gmax = jnp.where(is_g, lg, low).max(axis=-1, keepdims=True)
    gsel = jnp.where(is_g & (lg == gmax), lane, far).min(axis=-1, keepdims=True)
    pg_sel = 1.0 / jnp.where(is_g, jnp.exp(lg - gmax), 0.0).sum(axis=-1, keepdims=True)
    e0 = EXPERT_LANE0 + EXPERTS_PER_GROUP * gsel
    in_g = (lane >= e0) & (lane < e0 + EXPERTS_PER_GROUP)
    m1 = jnp.where(in_g, lg, low).max(axis=-1, keepdims=True)
    i1 = jnp.where(in_g & (lg == m1), lane, far).min(axis=-1, keepdims=True)
    rest = in_g & (lane != i1)
    m2 = jnp.where(rest, lg, low).max(axis=-1, keepdims=True)
    i2 = jnp.where(rest & (lg == m2), lane, far).min(axis=-1, keepdims=True)
    t = jnp.exp(m2 - m1)
    ssum = 1.0 + t
    w1 = (1.0 / ssum) * pg_sel
    w2 = (t / ssum) * pg_sel
    return i1 - EXPERT_LANE0, i2 - EXPERT_LANE0, w1, w2


def _rows_to_groups(x, stage_ref, out_ref, n_rows):
    stride = n_rows + 8
    for c in range(GROUP_ROWS):
        stage_ref[c * stride:c * stride + n_rows, :] = x[:, c * LANES:(c + 1) * LANES]
    for j in range(n_rows):
        out_ref[GROUP_ROWS * j:GROUP_ROWS * (j + 1), :] = stage_ref[pl.ds(j, GROUP_ROWS, stride=stride), :]


def _tail_kernel(x_ref, ya_ref, yb_ref, yc_ref, w_ref, g1_ref, sh2_ref, sc2_ref, lg_ref, lb_ref, wr_ref, br_ref,
                 x1_ref, h2g_ref, ids_ref, wts_ref, wbf_ref, stage_ref):
    @pl.when(pl.program_id(0) == 0)
    def _():
        wbf_ref[...] = w_ref[...].astype(BF16)

    o = (_dot(ya_ref[...], wbf_ref[0:POOL_WIDTH, :])
         + _dot(yb_ref[...], wbf_ref[POOL_WIDTH:POOL_WIDTH + QB_W, :])
         + _dot(yc_ref[...], wbf_ref[POOL_WIDTH + QB_W:, :]))
    x1 = _layer_norm(ALPHA * x_ref[...] + g1_ref[0] * o, lg_ref[...], lb_ref[...])
    x1_ref[...] = x1
    h2 = x1 * (1.0 + sc2_ref[0]) + sh2_ref[0]
    hh, hl = _split_bf16(h2)
    wh, wl = _split_bf16(wr_ref[...])
    lg = _dot(hh, wh) + _dot(hl, wh) + _dot(hh, wl) + br_ref[...]
    i1, i2, w1, w2 = _route(lg)
    lane = lax.broadcasted_iota(jnp.int32, lg.shape, 1)
    ids = jnp.where(lane == 0, i1, jnp.where(lane == 1, i2, 0.0))
    ids_ref[0] = ids.T[0:8, :].astype(jnp.int32)
    wts = jnp.where(lane == 0, w1, jnp.where(lane == 1, w2, 0.0))
    wts_ref[0] = wts.T[0:8, :]
    _rows_to_groups(h2, stage_ref, h2g_ref, TT)


def _tail(x, ya, yb, yc, w_out, mod, ln_g, ln_b, wr, br):
    row = lambda w: pl.BlockSpec((TT, w), lambda i: (i, 0))
    const = lambda s: pl.BlockSpec(s, lambda i: (0,) * len(s))
    return pl.pallas_call(
        _tail_kernel,
        grid=(T_ALL // TT,),
        in_specs=[row(D_MODEL), row(POOL_WIDTH), row(QB_W), row(C_W), const((D_MODEL, D_MODEL)),
                  _mod_spec(2, TT), _mod_spec(3, TT), _mod_spec(4, TT),
                  const((1, D_MODEL)), const((1, D_MODEL)), const((D_MODEL, ROUTER_LANES)), const((1, ROUTER_LANES))],
        out_specs=[row(D_MODEL), pl.BlockSpec((TT * GROUP_ROWS, LANES), lambda i: (i, 0)),
                   pl.BlockSpec((1, 8, TT), lambda i: (i, 0, 0)), pl.BlockSpec((1, 8, TT), lambda i: (i, 0, 0))],
        out_shape=[jax.ShapeDtypeStruct((T_ALL, D_MODEL), F32),
                   jax.ShapeDtypeStruct((T_ALL * GROUP_ROWS, LANES), F32),
                   jax.ShapeDtypeStruct((T_ALL // TT, 8, TT), jnp.int32),
                   jax.ShapeDtypeStruct((T_ALL // TT, 8, TT), F32)],
        scratch_shapes=[pltpu.VMEM((D_MODEL, D_MODEL), BF16),
                        pltpu.VMEM((GROUP_ROWS * (TT + 8), LANES), F32)],
        compiler_params=pltpu.CompilerParams(dimension_semantics=("arbitrary",), vmem_limit_bytes=VMEM_LIMIT),
        name="tail",
    )(x, ya, yb, yc, w_out, mod, mod, mod, ln_g, ln_b, wr, br)


def _plan_kernel(eid_ref, pos_ref, tile_ref):
    e = eid_ref[0]
    r = lax.broadcasted_iota(jnp.int32, (LANES, LANES), 0)
    c = lax.broadcasted_iota(jnp.int32, (LANES, LANES), 1)
    upper = (r <= c).astype(BF16)
    rows = e.shape[0]
    lower = (lax.broadcasted_iota(jnp.int32, (rows, rows), 1)
             < lax.broadcasted_iota(jnp.int32, (rows, rows), 0)).astype(BF16)
    tile_start = (lax.broadcasted_iota(jnp.int32, (1, LANES), 1) * TMX).astype(F32)
    pos = jnp.zeros(e.shape, F32)
    base = jnp.zeros((1, LANES), F32)
    n_before = jnp.zeros((1, LANES), F32)
    for ex in range(N_EXPERTS):
        m = e == ex
        incl = _dot(m.astype(BF16), upper)
        row_tot = jnp.broadcast_to(incl[:, LANES - 1:LANES], incl.shape)
        row_off = _dot(lower, row_tot.astype(BF16))
        cnt = jnp.sum(row_tot, axis=0, keepdims=True)
        pos = jnp.where(m, base + row_off + incl - 1.0, pos)
        base = base + jnp.ceil(cnt * (1.0 / TMX)) * TMX
        n_before = n_before + (tile_start >= base).astype(F32)
    pos_ref[0] = pos.astype(jnp.int32)
    sub = lax.broadcasted_iota(jnp.int32, (8, LANES), 0)
    tile_expert = jnp.broadcast_to(jnp.minimum(n_before, N_EXPERTS - 1.0), (8, LANES))
    n_tiles = jnp.broadcast_to(base * (1.0 / TMX), (8, LANES))
    tile_ref[0] = jnp.where(sub == 0, tile_expert, jnp.where(sub == 1, n_tiles, 0.0)).astype(jnp.int32)


def _plan(eid):
    rows = 2 * T_STREAM // LANES
    return pl.pallas_call(
        _plan_kernel,
        grid=(N_STREAMS,),
        in_specs=[pl.BlockSpec((1, rows, LANES), lambda s: (s, 0, 0))],
        out_specs=[pl.BlockSpec((1, rows, LANES), lambda s: (s, 0, 0)), pl.BlockSpec((1, 8, LANES), lambda s: (s, 0, 0))],
        out_shape=[jax.ShapeDtypeStruct((N_STREAMS, rows, LANES), jnp.int32),
                   jax.ShapeDtypeStruct((N_STREAMS, 8, LANES), jnp.int32)],
        compiler_params=pltpu.CompilerParams(dimension_semantics=("arbitrary",), vmem_limit_bytes=VMEM_LIMIT),
        name="plan",
    )(eid)


def _moe_kernel(pos_ref, wpair_ref, texp_ref, ntile_ref, h2g_hbm, wg_ref, wu_ref, wd_ref, y_hbm,
                h2v, yv, inv, winv, xg, yg, wgb, wub, wdb, sem):
    s = pl.program_id(0)
    t = pl.program_id(1)
    n_pairs = 2 * T_STREAM
    spare = T_STREAM

    def h2_copy():
        return pltpu.make_async_copy(
            h2g_hbm.at[pl.ds(pl.multiple_of(s * (T_STREAM * GROUP_ROWS), 8), T_STREAM * GROUP_ROWS)], h2v, sem.at[0])

    def y_copy():
        return pltpu.make_async_copy(yv.at[pl.ds(0, T_STREAM * GROUP_ROWS)], y_hbm.at[s], sem.at[1])

    @pl.when(t == 0)
    def _():
        h2_copy().start()
        yv[...] = jnp.zeros_like(yv)

        def fill(i, carry):
            for u in range(8):
                inv[i * 8 + u] = n_pairs
                winv[i * 8 + u] = 0.0
            return carry
        lax.fori_loop(0, P_MAX // 8, fill, 0)

        def scatter(i, carry):
            for u in range(8):
                p = i * 8 + u
                r = pos_ref[s * n_pairs + p]
                inv[r] = p
                winv[r] = wpair_ref[s * n_pairs + p]
            return carry
        lax.fori_loop(0, n_pairs // 8, scatter, 0)
        h2_copy().wait()

    ex = texp_ref[s * LANES + t]
    prev = texp_ref[s * LANES + jnp.maximum(t - 1, 0)]
    active = t < ntile_ref[s]

    @pl.when(active & ((t == 0) | (ex != prev)))
    def _():
        wgb[...] = wg_ref[0].astype(BF16)
        wub[...] = wu_ref[0].astype(BF16)
        wdb[...] = wd_ref[0].astype(BF16)

    @pl.when(active)
    def _():
        base = t * TMX
        for j in range(TMX):
            tok = inv[base + j] & (T_STREAM - 1)
            src = pl.multiple_of(tok * GROUP_ROWS, GROUP_ROWS)
            xg[pl.ds(j, GROUP_ROWS, stride=XSTRIDE), :] = h2v[pl.ds(src, GROUP_ROWS), :]
        x = jnp.concatenate([xg[c * XSTRIDE:c * XSTRIDE + TMX, :] for c in range(GROUP_ROWS)], axis=1).astype(BF16)
        hg = _dot(x, wgb[...])
        hu = _dot(x, wub[...])
        act = hg * jax.nn.sigmoid(hg) * hu
        y = _dot(act.astype(BF16), wdb[...])
        for c in range(GROUP_ROWS):
            yg[c * XSTRIDE:c * XSTRIDE + TMX, :] = y[:, c * LANES:(c + 1) * LANES]
        for j0 in range(0, TMX, RMW_BATCH):
            updates = []
            for j in range(j0, j0 + RMW_BATCH):
                p = inv[base + j]
                tok = jnp.where(p < n_pairs, p & (T_STREAM - 1), spare)
                dst = pl.multiple_of(tok * GROUP_ROWS, GROUP_ROWS)
                row = yg[pl.ds(j, GROUP_ROWS, stride=XSTRIDE), :]
                updates.append((dst, yv[pl.ds(dst, GROUP_ROWS), :] + winv[base + j] * row))
            for dst, val in updates:
                yv[pl.ds(dst, GROUP_ROWS), :] = val

    @pl.when(t == NT_MAX - 1)
    def _():
        y_copy().start()
        y_copy().wait()


def _moe(pos, wpair, texp, ntile, h2g, w_gate, w_up, w_down):
    wspec = lambda shape: pl.BlockSpec(shape, lambda s, t, pos, wpair, texp, ntile: (texp[s * LANES + t], 0, 0))
    return pl.pallas_call(
        _moe_kernel,
        grid_spec=pltpu.PrefetchScalarGridSpec(
            num_scalar_prefetch=4,
            grid=(N_STREAMS, NT_MAX),
            in_specs=[pl.BlockSpec(memory_space=pl.ANY),
                      wspec((1, D_MODEL, D_EXPERT)), wspec((1, D_MODEL, D_EXPERT)), wspec((1, D_EXPERT, D_MODEL))],
            out_specs=pl.BlockSpec(memory_space=pl.ANY),
            scratch_shapes=[pltpu.VMEM((T_STREAM * GROUP_ROWS, LANES), F32),
                            pltpu.VMEM(((T_STREAM + 1) * GROUP_ROWS, LANES), F32),
                            pltpu.SMEM((P_MAX,), jnp.int32),
                            pltpu.SMEM((P_MAX,), F32),
                            pltpu.VMEM((GROUP_ROWS * XSTRIDE, LANES), F32),
                            pltpu.VMEM((GROUP_ROWS * XSTRIDE, LANES), F32),
                            pltpu.VMEM((D_MODEL, D_EXPERT), BF16), pltpu.VMEM((D_MODEL, D_EXPERT), BF16),
                            pltpu.VMEM((D_EXPERT, D_MODEL), BF16),
                            pltpu.SemaphoreType.DMA((2,))]),
        out_shape=jax.ShapeDtypeStruct((N_STREAMS, T_STREAM * GROUP_ROWS, LANES), F32),
        compiler_params=pltpu.CompilerParams(
            dimension_semantics=("arbitrary", "arbitrary"), vmem_limit_bytes=VMEM_LIMIT),
        name="moe",
    )(pos, wpair, texp, ntile, h2g, w_gate, w_up, w_down)


def _groups_to_rows(src_ref, stage_ref, n_rows):
    stride = n_rows + 8
    for j in range(n_rows):
        stage_ref[pl.ds(j, GROUP_ROWS, stride=stride), :] = src_ref[GROUP_ROWS * j:GROUP_ROWS * (j + 1), :]
    return jnp.concatenate([stage_ref[c * stride:c * stride + n_rows, :] for c in range(GROUP_ROWS)], axis=1)


def _final_kernel(x1_ref, y_ref, g2_ref, lg_ref, lb_ref, o_ref, stage_ref):
    moe = _groups_to_rows(y_ref, stage_ref, TT)
    o_ref[...] = _layer_norm(ALPHA * x1_ref[...] + g2_ref[0] * moe, lg_ref[...], lb_ref[...])


def _final(x1, y, mod, ln_g, ln_b):
    row = lambda w: pl.BlockSpec((TT, w), lambda i: (i, 0))
    const = lambda s: pl.BlockSpec(s, lambda i: (0,) * len(s))
    return pl.pallas_call(
        _final_kernel,
        grid=(T_ALL // TT,),
        in_specs=[row(D_MODEL), pl.BlockSpec((TT * GROUP_ROWS, LANES), lambda i: (i, 0)),
                  _mod_spec(5, TT), const((1, D_MODEL)), const((1, D_MODEL))],
        out_specs=row(D_MODEL),
        out_shape=jax.ShapeDtypeStruct((T_ALL, D_MODEL), F32),
        scratch_shapes=[pltpu.VMEM((GROUP_ROWS * (TT + 8), LANES), F32)],
        compiler_params=pltpu.CompilerParams(dimension_semantics=("arbitrary",), vmem_limit_bytes=VMEM_LIMIT),
        name="final",
    )(x1, y, mod, ln_g, ln_b)


def _rope_tables():
    t = np.arange(DEC_SEQ)
    pos = np.stack([t // GRID_W, t % GRID_W], axis=1).astype(np.float32)
    nf = HEAD_DIM // 4
    inv = jnp.asarray(ROPE_THETA, F32) ** (-jnp.arange(nf, dtype=F32) / nf)
    d = np.arange(LANES) % HEAD_DIM
    which = d // (HEAD_DIM // 2)
    ang = jnp.asarray(pos)[:, which] * inv[d % nf][None, :]
    sign = np.where((d % 32) < 16, -1.0, 1.0).astype(np.float32)
    cos = jnp.concatenate([jnp.cos(ang), jnp.ones((TT, LANES), F32)], axis=0)
    sin = jnp.concatenate([jnp.sin(ang) * sign[None, :], jnp.zeros((TT, LANES), F32)], axis=0)
    return cos, sin


def _block_ones():
    h = np.arange(QB_W) // HEAD_DIM
    return jnp.asarray((h[:, None] == h[None, :]).astype(np.float32), dtype=BF16)


def _pool_block_diag(pool_w):
    out = jnp.zeros((POOL_WIDTH, POOL_WIDTH), F32)
    for g in range(4):
        out = out.at[64 * g:64 * g + 64, 64 * g:64 * g + 64].set(pool_w[g])
    return out


def kernel(x_prompt, x_sample, cache_b_k, cache_b_v, cache_c_k, cache_c_v, c, c_ctx, w_ada, b_ada, w_in, w_out,
           pool_w, pool_scale, q_norm, k_norm, rpb, ln1_g, ln1_b, ln2_g, ln2_b, router_g, router_g_b, router_e,
           router_e_b, w_gate, w_up, w_down):
    x = jnp.concatenate([x_prompt.reshape(T_CTX, D_MODEL), x_sample.reshape(T_LAT, D_MODEL)], axis=0)
    c8 = jnp.concatenate([c, c_ctx[None], jnp.zeros((MOD_ROWS - DEC_BATCH - 1, D_MODEL), F32)], axis=0)
    mod_all = _adaln(c8, w_ada, b_ada)

    cos_t, sin_t = _rope_tables()
    ones_bd = _block_ones()
    cbk = cache_b_k.reshape(DEC_BATCH, DEPTH, PAST_LEN, KB_W)
    cbv = cache_b_v.reshape(DEC_BATCH, DEPTH, PAST_LEN, KB_W)
    cck = cache_c_k.reshape(DEC_BATCH, DEPTH, PAST_LEN, C_W)
    ccv = cache_c_v.reshape(DEC_BATCH, DEPTH, PAST_LEN, C_W)
    pad = jnp.zeros((D_MODEL, ROUTER_LANES - N_GROUPS - N_EXPERTS), F32)

    new_bk, new_bv, new_ck, new_cv = [], [], [], []
    for l in range(DEPTH):
        mod = mod_all[l].reshape(MOD_ROWS * 6, 1, D_MODEL)
        qn = jnp.tile(q_norm[l], H_B)[None]
        kn = jnp.tile(k_norm[l], KV_B)[None]
        a, qb, kb, vb, qc, kc, vc = _proj(x, mod, w_in[l], ones_bd, qn, kn, cos_t, sin_t)
        ya = _pool(a, _pool_block_diag(pool_w[l]), pool_scale[l][None])
        yb_ctx, yc_ctx = _attn_ctx(qb, kb, vb, qc, kc, vc)
        yb_lat = _attn_latb(l, qb, kb, vb, cbk, cbv)
        yc_lat = _natten(l, qc, kc, vc, cck, ccv, _natten_table(rpb[l]))
        yb = jnp.concatenate([yb_ctx, yb_lat], axis=0)
        yc = jnp.concatenate([yc_ctx, yc_lat], axis=0)
        wr = jnp.concatenate([router_g[l], router_e[l], pad], axis=1)
        br = jnp.concatenate([router_g_b[l], router_e_b[l], pad[0]], axis=0)[None]
        x1, h2g, ids, wts = _tail(x, ya, yb, yc, w_out[l], mod, ln1_g[l][None], ln1_b[l][None], wr, br)
        pairs = lambda a: a.reshape(N_STREAMS, T_STREAM // TT, 8, TT)[:, :, :2, :].transpose(0, 2, 1, 3)
        pos, tiles = _plan(pairs(ids).reshape(N_STREAMS, 2 * T_STREAM // LANES, LANES))
        y = _moe(pos.reshape(-1), pairs(wts).reshape(-1), tiles[:, 0, :].reshape(-1), tiles[:, 1, 0],
                 h2g, w_gate[l], w_up[l], w_down[l])
        x = _final(x1, y.reshape(T_ALL * GROUP_ROWS, LANES), mod, ln2_g[l][None], ln2_b[l][None])
        new_bk.append(kb[:T_CTX].reshape(BATCH, SEQ, KV_B, HEAD_DIM))
        new_bv.append(vb[:T_CTX].reshape(BATCH, SEQ, KV_B, HEAD_DIM))
        new_ck.append(kc[:T_CTX].reshape(BATCH, SEQ, H_C, HEAD_DIM))
        new_cv.append(vc[:T_CTX].reshape(BATCH, SEQ, H_C, HEAD_DIM))

    y_prompt = x[:T_CTX].reshape(BATCH, SEQ, D_MODEL)
    y_sample = x[T_CTX:].reshape(DEC_BATCH, DEC_SEQ, D_MODEL)
    return (y_prompt, y_sample, jnp.stack(new_bk, axis=1), jnp.stack(new_bv, axis=1),
            jnp.stack(new_ck, axis=1), jnp.stack(new_cv, axis=1))
```

```python
import functools

import numpy as np
import jax
import jax.numpy as jnp
from jax import lax
from jax.experimental import pallas as pl
from jax.experimental.pallas import tpu as pltpu

F32 = jnp.float32
BF16 = jnp.bfloat16

D_MODEL = 1024
BATCH = 16
SEQ = 256
DEPTH = 2
DEC_BATCH = 4
DEC_SEQ = 1024
PAST_LEN = 256
GRID_W = 64
GRID_ROWS = DEC_SEQ // GRID_W
HEAD_DIM = 64
POOL_WIDTH = 256
POOL_WINDOWS = (2, 4, 8, 16)
H_B = 6
KV_B = 2
H_C = 6
WIN_R = 8
WIN_C = 16
ROPE_THETA = 10000.0
QB_W = H_B * HEAD_DIM
KB_W = KV_B * HEAD_DIM
C_W = H_C * HEAD_DIM
PROJ_WIDTH = 2048
N_GROUPS = 4
EXPERTS_PER_GROUP = 8
N_EXPERTS = 32
D_EXPERT = 256
ALPHA = (2 * DEPTH) ** 0.25
LN_EPS = 1e-6
RMS_EPS = 1e-6
NEG = -1e30
ATTN_SCALE = HEAD_DIM ** -0.5

T_CTX = BATCH * SEQ
T_LAT = DEC_BATCH * DEC_SEQ
T_ALL = T_CTX + T_LAT

LANES = 128
ROUTER_LANES = 128
EXPERT_LANE0 = N_GROUPS
MOD_ROWS = 8
CTX_MOD_ROW = DEC_BATCH

TT = 512
TP = 1024
N_STREAMS = 2
T_STREAM = T_ALL // N_STREAMS
TMX = 128
P_MAX = 2 * T_STREAM + N_EXPERTS * TMX
NT_MAX = P_MAX // TMX
GROUP_ROWS = D_MODEL // LANES
XSTRIDE = TMX + 8
RMW_BATCH = 8
PLAN_TILE_EXPERT, PLAN_N_TILES, PLAN_VALID_END, PLAN_PAD_END = 0, 1, 2, 3
HALF = DEC_SEQ // 2
NAT_KEYS = 12 * GRID_W
VMEM_LIMIT = 56 * 1024 * 1024


def _dot(a, b):
    return jnp.dot(a, b, preferred_element_type=F32)


def _dot_nt(a, b):
    return lax.dot_general(a, b, (((1,), (1,)), ((), ())), preferred_element_type=F32)


def _split_bf16(x):
    hi = x.astype(BF16)
    lo = (x - hi.astype(F32)).astype(BF16)
    return hi, lo


def _layer_norm(y, g, b):
    mu = jnp.mean(y, axis=-1, keepdims=True)
    var = jnp.mean(jnp.square(y - mu), axis=-1, keepdims=True)
    return (y - mu) * lax.rsqrt(var + LN_EPS) * g + b


def _adaln_kernel(c_ref, w_ref, b_ref, o_ref):
    c = c_ref[...]
    s = (c * jax.nn.sigmoid(c)).astype(BF16)
    o_ref[0] = _dot(s, w_ref[0].astype(BF16)) + b_ref[0]


def _adaln(c8, w_ada, b_ada):
    tn = 1536
    n = w_ada.shape[-1]
    return pl.pallas_call(
        _adaln_kernel,
        grid=(DEPTH, n // tn),
        in_specs=[
            pl.BlockSpec((MOD_ROWS, D_MODEL), lambda l, j: (0, 0)),
            pl.BlockSpec((1, D_MODEL, tn), lambda l, j: (l, 0, j)),
            pl.BlockSpec((1, 1, tn), lambda l, j: (l, 0, j)),
        ],
        out_specs=pl.BlockSpec((1, MOD_ROWS, tn), lambda l, j: (l, 0, j)),
        out_shape=jax.ShapeDtypeStruct((DEPTH, MOD_ROWS, n), F32),
        compiler_params=pltpu.CompilerParams(
            dimension_semantics=("arbitrary", "arbitrary"), vmem_limit_bytes=VMEM_LIMIT),
        name="adaln",
    )(c8, w_ada, b_ada.reshape(DEPTH, 1, n))


def _mod_row_of_tile(i, tile):
    n_ctx = T_CTX // tile
    per_req = DEC_SEQ // tile
    return jnp.where(i < n_ctx, CTX_MOD_ROW, (i - n_ctx) // per_req)


def _mod_spec(chunk, tile):
    return pl.BlockSpec((1, 1, D_MODEL), lambda i: (_mod_row_of_tile(i, tile) * 6 + chunk, 0, 0))


def _rms_norm_heads(x, ones_bd, w):
    hi, lo = _split_bf16(x * x)
    ssq = _dot(hi, ones_bd) + _dot(lo, ones_bd)
    return x * lax.rsqrt(ssq * (1.0 / HEAD_DIM) + RMS_EPS) * w


def _rope_cols(x, cos, sin, first16):
    cols = []
    for j in range(x.shape[1] // LANES):
        xc = x[:, j * LANES:(j + 1) * LANES]
        partner = jnp.where(first16, pltpu.roll(xc, LANES - 16, axis=1), pltpu.roll(xc, 16, axis=1))
        cols.append(xc * cos + partner * sin)
    return jnp.concatenate(cols, axis=1) if len(cols) > 1 else cols[0]


def _proj_kernel(x_ref, sh_ref, sc_ref, w_ref, ones_ref, qn_ref, kn_ref, cos_ref, sin_ref,
                 a_ref, qb_ref, kb_ref, vb_ref, qc_ref, kc_ref, vc_ref, wbf_ref):
    @pl.when(pl.program_id(0) == 0)
    def _():
        wbf_ref[...] = w_ref[...].astype(BF16)

    h = x_ref[...] * (1.0 + sc_ref[0]) + sh_ref[0]
    p = _dot(h.astype(BF16), wbf_ref[...])
    o = 0
    a_ref[...] = p[:, o:o + POOL_WIDTH]; o += POOL_WIDTH
    qb = p[:, o:o + QB_W]; o += QB_W
    kb = p[:, o:o + KB_W]; o += KB_W
    vb_ref[...] = p[:, o:o + KB_W]; o += KB_W
    qc_ref[...] = (p[:, o:o + C_W] * ATTN_SCALE).astype(BF16); o += C_W
    kc_ref[...] = p[:, o:o + C_W]; o += C_W
    vc_ref[...] = p[:, o:o + C_W]

    ones_bd = ones_ref[...]
    cos = cos_ref[...]
    sin = sin_ref[...]
    lane = lax.broadcasted_iota(jnp.int32, (1, LANES), 1)
    first16 = (lane & 31) < 16
    qb = _rope_cols(_rms_norm_heads(qb, ones_bd, qn_ref[...]), cos, sin, first16)
    kb = _rope_cols(_rms_norm_heads(kb, ones_bd[:KB_W, :KB_W], kn_ref[...]), cos, sin, first16)
    qb_ref[...] = (qb * ATTN_SCALE).astype(BF16)
    kb_ref[...] = kb


def _proj(x, mod, w_in, ones_bd, qn, kn, cos_t, sin_t):
    n_tiles = T_ALL // TT
    n_ctx = T_CTX // TT
    per_req = DEC_SEQ // TT

    def rope_idx(i):
        return (jnp.where(i < n_ctx, per_req, (i - n_ctx) % per_req), 0)

    row = lambda w: pl.BlockSpec((TT, w), lambda i: (i, 0))
    const = lambda s: pl.BlockSpec(s, lambda i: (0,) * len(s))
    widths = (POOL_WIDTH, QB_W, KB_W, KB_W, C_W, C_W, C_W)
    dtypes = (F32, BF16, F32, F32, BF16, F32, F32)
    return pl.pallas_call(
        _proj_kernel,
        grid=(n_tiles,),
        in_specs=[row(D_MODEL), _mod_spec(0, TT), _mod_spec(1, TT), const((D_MODEL, PROJ_WIDTH)),
                  const((QB_W, QB_W)), const((1, QB_W)), const((1, KB_W)),
                  pl.BlockSpec((TT, LANES), rope_idx), pl.BlockSpec((TT, LANES), rope_idx)],
        out_specs=[row(w) for w in widths],
        out_shape=[jax.ShapeDtypeStruct((T_ALL, w), d) for w, d in zip(widths, dtypes)],
        scratch_shapes=[pltpu.VMEM((D_MODEL, PROJ_WIDTH), BF16)],
        compiler_params=pltpu.CompilerParams(dimension_semantics=("arbitrary",), vmem_limit_bytes=VMEM_LIMIT),
        name="proj",
    )(x, mod, mod, w_in, ones_bd, qn, kn, cos_t, sin_t)


def _pool_kernel(a_ref, w_ref, scale_ref, o_ref):
    i = pl.program_id(0)
    a = a_ref[...]
    row = lax.broadcasted_iota(jnp.int32, a.shape, 0)
    lane = lax.broadcasted_iota(jnp.int32, a.shape, 1)
    seq_m1 = jnp.where(i < T_CTX // TP, SEQ - 1, DEC_SEQ - 1)
    t = row & seq_m1

    def shifted(k):
        v = pltpu.roll(a, (TP - k) % TP, axis=0) if k else a
        ok = (t + k >= 0) & (t + k <= seq_m1)
        return jnp.where(ok, v, 0.0)

    s = {k: shifted(k) for k in range(-8, 8)}
    c2 = s[-1] + s[0]
    c4 = c2 + s[-2] + s[1]
    c8 = c4 + s[-4] + s[-3] + s[2] + s[3]
    c16 = c8 + s[-8] + s[-7] + s[-6] + s[-5] + s[4] + s[5] + s[6] + s[7]
    g = lane >> 6
    csum = jnp.where(g == 0, c2, jnp.where(g == 1, c4, jnp.where(g == 2, c8, c16)))
    half = jnp.where(g == 0, 1, jnp.where(g == 1, 2, jnp.where(g == 2, 4, 8)))
    cnt = jnp.minimum(t + half, seq_m1 + 1) - jnp.maximum(t - half, 0)
    pooled = csum / cnt.astype(F32) - a
    y = _dot(pooled.astype(BF16), w_ref[...].astype(BF16)) * scale_ref[...]
    o_ref[...] = y.astype(BF16)


def _pool(a, w_bd, scale):
    return pl.pallas_call(
        _pool_kernel,
        grid=(T_ALL // TP,),
        in_specs=[pl.BlockSpec((TP, POOL_WIDTH), lambda i: (i, 0)),
                  pl.BlockSpec((POOL_WIDTH, POOL_WIDTH), lambda i: (0, 0)),
                  pl.BlockSpec((1, POOL_WIDTH), lambda i: (0, 0))],
        out_specs=pl.BlockSpec((TP, POOL_WIDTH), lambda i: (i, 0)),
        out_shape=jax.ShapeDtypeStruct((T_ALL, POOL_WIDTH), BF16),
        compiler_params=pltpu.CompilerParams(dimension_semantics=("arbitrary",), vmem_limit_bytes=VMEM_LIMIT),
        name="pool",
    )(a, w_bd, scale)


def _softmax_pv(scores, values):
    m = scores[0].max(axis=-1, keepdims=True)
    for s in scores[1:]:
        m = jnp.maximum(m, s.max(axis=-1, keepdims=True))
    ps = [jnp.exp(s - m) for s in scores]
    l = ps[0].sum(axis=-1, keepdims=True)
    for p in ps[1:]:
        l = l + p.sum(axis=-1, keepdims=True)
    r = 1.0 / l
    o = _dot((ps[0] * r).astype(BF16), values[0])
    for p, v in zip(ps[1:], values[1:]):
        o = o + _dot((p * r).astype(BF16), v)
    return o


def _lane_halves():
    lane = lax.broadcasted_iota(jnp.int32, (1, LANES), 1)
    return lane < HEAD_DIM, lane >= HEAD_DIM


def _keep(x, mask):
    return jnp.where(mask, x, 0.0).astype(BF16)


def _gqa_variants(x):
    lo, hi = _lane_halves()
    xs = pltpu.roll(x, HEAD_DIM, axis=1)
    nat_lo, nat_hi = _keep(x, lo), _keep(x, hi)
    sw_lo, sw_hi = _keep(xs, lo), _keep(xs, hi)
    return ((nat_lo, sw_hi), (nat_lo, nat_hi), (sw_lo, nat_hi))


def _mha_variants(x):
    lo, hi = _lane_halves()
    out = []
    for j in range(x.shape[1] // LANES):
        xc = x[:, j * LANES:(j + 1) * LANES]
        out.append((_keep(xc, lo), _keep(xc, hi)))
    return tuple(out)


def _attend_cols(q, k_vars, v_vars, extra_k=None, extra_v=None, bias=None):
    cols = []
    for j in range(q.shape[1] // LANES):
        qc = q[:, j * LANES:(j + 1) * LANES]
        o = None
        for hh in range(2):
            s = _dot_nt(qc, k_vars[j][hh])
            if bias is not None:
                s = s + bias[j][hh]
            scores, values = [s], [v_vars[j][hh]]
            if extra_k is not None:
                scores.append(_dot_nt(qc, extra_k[j][hh]))
                values.append(extra_v[j][hh])
            oh = _softmax_pv(scores, values)
            o = oh if o is None else o + oh
        cols.append(o)
    return jnp.concatenate(cols, axis=1)


def _attn_ctx_kernel(qb_ref, kb_ref, vb_ref, qc_ref, kc_ref, vc_ref, yb_ref, yc_ref):
    yb = _attend_cols(qb_ref[...], _gqa_variants(kb_ref[...]), _gqa_variants(vb_ref[...]))
    yb_ref[...] = yb.astype(BF16)
    yc = _attend_cols(qc_ref[...], _mha_variants(kc_ref[...]), _mha_variants(vc_ref[...]))
    yc_ref[...] = yc.astype(BF16)


def _attn_ctx(qb, kb, vb, qc, kc, vc):
    row = lambda w: pl.BlockSpec((SEQ, w), lambda i: (i, 0))
    return pl.pallas_call(
        _attn_ctx_kernel,
        grid=(BATCH,),
        in_specs=[row(QB_W), row(KB_W), row(KB_W), row(C_W), row(C_W), row(C_W)],
        out_specs=[row(QB_W), row(C_W)],
        out_shape=[jax.ShapeDtypeStruct((T_CTX, QB_W), BF16), jax.ShapeDtypeStruct((T_CTX, C_W), BF16)],
        compiler_params=pltpu.CompilerParams(dimension_semantics=("arbitrary",), vmem_limit_bytes=VMEM_LIMIT),
        name="attn_ctx",
    )(qb, kb, vb, qc, kc, vc)


def _attn_latb_kernel(q_ref, k_ref, v_ref, ck_ref, cv_ref, y_ref):
    y = _attend_cols(q_ref[...], _gqa_variants(ck_ref[0, 0]), _gqa_variants(cv_ref[0, 0]),
                     extra_k=_gqa_variants(k_ref[...]), extra_v=_gqa_variants(v_ref[...]))
    y_ref[...] = y.astype(BF16)


def _attn_latb(layer, qb, kb, vb, cache_k, cache_v):
    ctx_h = T_CTX // HALF
    ctx_r = T_CTX // DEC_SEQ
    cache = pl.BlockSpec((1, 1, PAST_LEN, KB_W), lambda b, s: (b, layer, 0, 0))
    own = pl.BlockSpec((DEC_SEQ, KB_W), lambda b, s: (ctx_r + b, 0))
    return pl.pallas_call(
        _attn_latb_kernel,
        grid=(DEC_BATCH, DEC_SEQ // HALF),
        in_specs=[pl.BlockSpec((HALF, QB_W), lambda b, s: (ctx_h + 2 * b + s, 0)), own, own, cache, cache],
        out_specs=pl.BlockSpec((HALF, QB_W), lambda b, s: (2 * b + s, 0)),
        out_shape=jax.ShapeDtypeStruct((T_LAT, QB_W), BF16),
        compiler_params=pltpu.CompilerParams(
            dimension_semantics=("arbitrary", "arbitrary"), vmem_limit_bytes=VMEM_LIMIT),
        name="attn_latb",
    )(qb, kb, vb, cache_k, cache_v)


def _natten_kernel(q_ref, k_ref, v_ref, ck_ref, cv_ref, tab_ref, y_ref):
    s = pl.program_id(1)
    start = pl.multiple_of(s * (4 * GRID_W), 4 * GRID_W)
    k = _mha_variants(k_ref[pl.ds(start, NAT_KEYS), :])
    v = _mha_variants(v_ref[pl.ds(start, NAT_KEYS), :])
    ck = _mha_variants(ck_ref[0, 0])
    cv = _mha_variants(cv_ref[0, 0])
    bias = ((tab_ref[0, 0, 0], tab_ref[0, 1, 0]),)
    y = _attend_cols(q_ref[...], k, v, extra_k=ck, extra_v=cv, bias=bias)
    y_ref[...] = y.astype(BF16)


def _natten(layer, qc, kc, vc, cache_k, cache_v, table):
    ctx_h = T_CTX // HALF
    ctx_r = T_CTX // DEC_SEQ
    cache = pl.BlockSpec((1, 1, PAST_LEN, LANES), lambda j, s, b: (b, layer, 0, j))
    own = pl.BlockSpec((DEC_SEQ, LANES), lambda j, s, b: (ctx_r + b, j))
    return pl.pallas_call(
        _natten_kernel,
        grid=(C_W // LANES, DEC_SEQ // HALF, DEC_BATCH),
        in_specs=[pl.BlockSpec((HALF, LANES), lambda j, s, b: (ctx_h + 2 * b + s, j)), own, own, cache, cache,
                  pl.BlockSpec((1, 2, 1, HALF, NAT_KEYS), lambda j, s, b: (j, 0, s, 0, 0))],
        out_specs=pl.BlockSpec((HALF, LANES), lambda j, s, b: (2 * b + s, j)),
        out_shape=jax.ShapeDtypeStruct((T_LAT, C_W), BF16),
        compiler_params=pltpu.CompilerParams(
            dimension_semantics=("arbitrary", "arbitrary", "arbitrary"), vmem_limit_bytes=VMEM_LIMIT),
        name="natten",
    )(qc, kc, vc, cache_k, cache_v, table)


def _natten_table(rpb):
    n_qr, n_kr = HALF // GRID_W, NAT_KEYS // GRID_W
    qcol = np.arange(GRID_W)[:, None]
    kcol = np.arange(GRID_W)[None, :]
    cs = np.clip(qcol - WIN_C // 2, 0, GRID_W - WIN_C)
    col_ok = (kcol >= cs) & (kcol < cs + WIN_C)
    dc = np.clip(kcol - qcol, -(WIN_C - 1), WIN_C - 1) + (WIN_C - 1)
    sel_c = (dc[:, :, None] == np.arange(2 * WIN_C - 1)).astype(np.float32)
    s = np.arange(2)[:, None, None]
    r = n_qr * s + np.arange(n_qr)[None, :, None]
    rk = 4 * s + np.arange(n_kr)[None, None, :]
    rs = np.clip(r - WIN_R // 2, 0, GRID_ROWS - WIN_R)
    row_ok = (rk >= rs) & (rk < rs + WIN_R)
    dr = np.clip(rk - r, -(WIN_R - 1), WIN_R - 1) + (WIN_R - 1)
    sel_r = (dr[..., None] == np.arange(2 * WIN_R - 1)).astype(np.float32)
    tab = jnp.einsum("sijd,hdm,qkm->hsiqjk", jnp.asarray(sel_r), rpb, jnp.asarray(sel_c),
                     precision=lax.Precision.HIGHEST)
    ok = row_ok[None, :, :, None, :, None] & col_ok[None, None, None, :, None, :]
    tab = jnp.where(jnp.asarray(ok), tab, NEG)
    return tab.reshape(H_C // 2, 2, 2, HALF, NAT_KEYS)


def _route(lg):
    lane = lax.broadcasted_iota(jnp.int32, lg.shape, 1).astype(F32)
    low = jnp.float32(-3.0e38)
    far = jnp.float32(ROUTER_LANES)
    is_g = lane < N_GROUPS
    gmax = jnp.where(is_g, lg, low).max(axis=-1, keepdims=True)
    gsel = jnp.where(is_g & (lg == gmax), lane, far).min(axis=-1, keepdims=True)
    pg_sel = 1.0 / jnp.where(is_g, jnp.exp(lg - gmax), 0.0).sum(axis=-1, keepdims=True)
    e0 = EXPERT_LANE0 + EXPERTS_PER_GROUP * gsel
    in_g = (lane >= e0) & (lane < e0 + EXPERTS_PER_GROUP)
    m1 = jnp.where(in_g, lg, low).max(axis=-1, keepdims=True)
    i1 = jnp.where(in_g & (lg == m1), lane, far).min(axis=-1, keepdims=True)
    rest = in_g & (lane != i1)
    m2 = jnp.where(rest, lg, low).max(axis=-1, keepdims=True)
    i2 = jnp.where(rest & (lg == m2), lane, far).min(axis=-1, keepdims=True)
    t = jnp.exp(m2 - m1)
    ssum = 1.0 + t
    w1 = (1.0 / ssum) * pg_sel
    w2 = (t / ssum) * pg_sel
    return i1 - EXPERT_LANE0, i2 - EXPERT_LANE0, w1, w2


def _rows_to_groups(x, stage_ref, out_ref, n_rows):
    stride = n_rows + 8
    for c in range(GROUP_ROWS):
        stage_ref[c * stride:c * stride + n_rows, :] = x[:, c * LANES:(c + 1) * LANES]
    for j in range(n_rows):
        out_ref[GROUP_ROWS * j:GROUP_ROWS * (j + 1), :] = stage_ref[pl.ds(j, GROUP_ROWS, stride=stride), :]


def _tail_kernel(x_ref, ya_ref, yb_ref, yc_ref, w_ref, g1_ref, sh2_ref, sc2_ref, lg_ref, lb_ref, wr_ref, br_ref,
                 x1_ref, h2g_ref, ids_ref, wts_ref, wbf_ref, stage_ref):
    @pl.when(pl.program_id(0) == 0)
    def _():
        wbf_ref[...] = w_ref[...].astype(BF16)

    o = (_dot(ya_ref[...], wbf_ref[0:POOL_WIDTH, :])
         + _dot(yb_ref[...], wbf_ref[POOL_WIDTH:POOL_WIDTH + QB_W, :])
         + _dot(yc_ref[...], wbf_ref[POOL_WIDTH + QB_W:, :]))
    x1 = _layer_norm(ALPHA * x_ref[...] + g1_ref[0] * o, lg_ref[...], lb_ref[...])
    x1_ref[...] = x1
    h2 = x1 * (1.0 + sc2_ref[0]) + sh2_ref[0]
    hh, hl = _split_bf16(h2)
    wh, wl = _split_bf16(wr_ref[...])
    lg = _dot(hh, wh) + _dot(hl, wh) + _dot(hh, wl) + br_ref[...]
    i1, i2, w1, w2 = _route(lg)
    lane = lax.broadcasted_iota(jnp.int32, lg.shape, 1)
    ids = jnp.where(lane == 0, i1, jnp.where(lane == 1, i2, 0.0))
    ids_ref[0] = ids.T[0:8, :].astype(jnp.int32)
    wts = jnp.where(lane == 0, w1, jnp.where(lane == 1, w2, 0.0))
    wts_ref[0] = wts.T[0:8, :]
    _rows_to_groups(h2, stage_ref, h2g_ref, TT)


def _tail(x, ya, yb, yc, w_out, mod, ln_g, ln_b, wr, br):
    row = lambda w: pl.BlockSpec((TT, w), lambda i: (i, 0))
    const = lambda s: pl.BlockSpec(s, lambda i: (0,) * len(s))
    return pl.pallas_call(
        _tail_kernel,
        grid=(T_ALL // TT,),
        in_specs=[row(D_MODEL), row(POOL_WIDTH), row(QB_W), row(C_W), const((D_MODEL, D_MODEL)),
                  _mod_spec(2, TT), _mod_spec(3, TT), _mod_spec(4, TT),
                  const((1, D_MODEL)), const((1, D_MODEL)), const((D_MODEL, ROUTER_LANES)), const((1, ROUTER_LANES))],
        out_specs=[row(D_MODEL), pl.BlockSpec((TT * GROUP_ROWS, LANES), lambda i: (i, 0)),
                   pl.BlockSpec((1, 8, TT), lambda i: (i, 0, 0)), pl.BlockSpec((1, 8, TT), lambda i: (i, 0, 0))],
        out_shape=[jax.ShapeDtypeStruct((T_ALL, D_MODEL), F32),
                   jax.ShapeDtypeStruct((T_ALL * GROUP_ROWS, LANES), F32),
                   jax.ShapeDtypeStruct((T_ALL // TT, 8, TT), jnp.int32),
                   jax.ShapeDtypeStruct((T_ALL // TT, 8, TT), F32)],
        scratch_shapes=[pltpu.VMEM((D_MODEL, D_MODEL), BF16),
                        pltpu.VMEM((GROUP_ROWS * (TT + 8), LANES), F32)],
        compiler_params=pltpu.CompilerParams(dimension_semantics=("arbitrary",), vmem_limit_bytes=VMEM_LIMIT),
        name="tail",
    )(x, ya, yb, yc, w_out, mod, mod, mod, ln_g, ln_b, wr, br)


def _plan_kernel(eid_ref, pos_ref, tile_ref):
    e = eid_ref[0]
    r = lax.broadcasted_iota(jnp.int32, (LANES, LANES), 0)
    c = lax.broadcasted_iota(jnp.int32, (LANES, LANES), 1)
    upper = (r <= c).astype(BF16)
    rows = e.shape[0]
    lower = (lax.broadcasted_iota(jnp.int32, (rows, rows), 1)
             < lax.broadcasted_iota(jnp.int32, (rows, rows), 0)).astype(BF16)
    tile_start = (lax.broadcasted_iota(jnp.int32, (1, LANES), 1) * TMX).astype(F32)
    lane = lax.broadcasted_iota(jnp.int32, (1, LANES), 1)
    pos = jnp.zeros(e.shape, F32)
    base = jnp.zeros((1, LANES), F32)
    n_before = jnp.zeros((1, LANES), F32)
    valid_end = jnp.zeros((1, LANES), F32)
    pad_end = jnp.zeros((1, LANES), F32)
    for ex in range(N_EXPERTS):
        m = e == ex
        incl = _dot(m.astype(BF16), upper)
        row_tot = jnp.broadcast_to(incl[:, LANES - 1:LANES], incl.shape)
        row_off = _dot(lower, row_tot.astype(BF16))
        cnt = jnp.sum(row_tot, axis=0, keepdims=True)
        pos = jnp.where(m, base + row_off + incl - 1.0, pos)
        valid_end = jnp.where(lane == ex, base + cnt, valid_end)
        base = base + jnp.ceil(cnt * (1.0 / TMX)) * TMX
        pad_end = jnp.where(lane == ex, base, pad_end)
        n_before = n_before + (tile_start >= base).astype(F32)
    pos_ref[0] = pos.astype(jnp.int32)
    sub = lax.broadcasted_iota(jnp.int32, (8, LANES), 0)
    rows8 = lambda v: jnp.broadcast_to(v, (8, LANES))
    table = jnp.where(sub == PLAN_TILE_EXPERT, rows8(jnp.minimum(n_before, N_EXPERTS - 1.0)),
                      jnp.where(sub == PLAN_N_TILES, rows8(base * (1.0 / TMX)),
                                jnp.where(sub == PLAN_VALID_END, rows8(valid_end),
                                          jnp.where(sub == PLAN_PAD_END, rows8(pad_end), 0.0))))
    tile_ref[0] = table.astype(jnp.int32)


def _plan(eid):
    rows = 2 * T_STREAM // LANES
    return pl.pallas_call(
        _plan_kernel,
        grid=(N_STREAMS,),
        in_specs=[pl.BlockSpec((1, rows, LANES), lambda s: (s, 0, 0))],
        out_specs=[pl.BlockSpec((1, rows, LANES), lambda s: (s, 0, 0)), pl.BlockSpec((1, 8, LANES), lambda s: (s, 0, 0))],
        out_shape=[jax.ShapeDtypeStruct((N_STREAMS, rows, LANES), jnp.int32),
                   jax.ShapeDtypeStruct((N_STREAMS, 8, LANES), jnp.int32)],
        compiler_params=pltpu.CompilerParams(dimension_semantics=("arbitrary",), vmem_limit_bytes=VMEM_LIMIT),
        name="plan",
    )(eid)


def _moe_kernel(pos_ref, wpair_ref, plan_ref, h2g_hbm, wg_ref, wu_ref, wd_ref, y_hbm,
                h2v, yv, off, wrow, xg, yg, wgb, wub, wdb, sem):
    s = pl.program_id(0)
    t = pl.program_id(1)
    n_pairs = 2 * T_STREAM
    spare_off = T_STREAM * GROUP_ROWS
    plan = lambda row, idx: plan_ref[(s * 8 + row) * LANES + idx]

    def h2_copy():
        return pltpu.make_async_copy(
            h2g_hbm.at[pl.ds(pl.multiple_of(s * (T_STREAM * GROUP_ROWS), 8), T_STREAM * GROUP_ROWS)],
            h2v.at[pl.ds(0, T_STREAM * GROUP_ROWS)], sem.at[0])

    def y_copy():
        return pltpu.make_async_copy(yv.at[pl.ds(0, T_STREAM * GROUP_ROWS)], y_hbm.at[s], sem.at[1])

    @pl.when(t == 0)
    def _():
        h2_copy().start()
        yv[...] = jnp.zeros_like(yv)
        h2v[spare_off:spare_off + GROUP_ROWS, :] = jnp.zeros((GROUP_ROWS, LANES), F32)

        def pad_expert(e, carry):
            def pad_row(r, c):
                off[r] = spare_off
                wrow[r] = 0.0
                return c
            return lax.fori_loop(plan(PLAN_VALID_END, e), plan(PLAN_PAD_END, e), pad_row, carry)
        lax.fori_loop(0, N_EXPERTS, pad_expert, 0)

        def place(i, carry):
            for u in range(8):
                p = i * 8 + u
                r = pos_ref[s * n_pairs + p]
                off[r] = (p & (T_STREAM - 1)) * GROUP_ROWS
                wrow[r] = wpair_ref[s * n_pairs + p]
            return carry
        lax.fori_loop(0, n_pairs // 8, place, 0)
        h2_copy().wait()

    ex = plan(PLAN_TILE_EXPERT, t)
    prev = plan(PLAN_TILE_EXPERT, jnp.maximum(t - 1, 0))
    active = t < plan(PLAN_N_TILES, 0)

    @pl.when(active & ((t == 0) | (ex != prev)))
    def _():
        wgb[...] = wg_ref[0].astype(BF16)
        wub[...] = wu_ref[0].astype(BF16)
        wdb[...] = wd_ref[0].astype(BF16)

    @pl.when(active)
    def _():
        base = t * TMX
        for j in range(TMX):
            src = pl.multiple_of(off[base + j], GROUP_ROWS)
            xg[pl.ds(j, GROUP_ROWS, stride=XSTRIDE), :] = h2v[pl.ds(src, GROUP_ROWS), :]
        x = jnp.concatenate([xg[c * XSTRIDE:c * XSTRIDE + TMX, :] for c in range(GROUP_ROWS)], axis=1).astype(BF16)
        hg = _dot(x, wgb[...])
        hu = _dot(x, wub[...])
        act = hg * jax.nn.sigmoid(hg) * hu
        y = _dot(act.astype(BF16), wdb[...])
        for c in range(GROUP_ROWS):
            yg[c * XSTRIDE:c * XSTRIDE + TMX, :] = y[:, c * LANES:(c + 1) * LANES]
        for j0 in range(0, TMX, RMW_BATCH):
            updates = []
            for j in range(j0, j0 + RMW_BATCH):
                dst = pl.multiple_of(off[base + j], GROUP_ROWS)
                row = yg[pl.ds(j, GROUP_ROWS, stride=XSTRIDE), :]
                updates.append((dst, yv[pl.ds(dst, GROUP_ROWS), :] + wrow[base + j] * row))
            for dst, val in updates:
                yv[pl.ds(dst, GROUP_ROWS), :] = val

    @pl.when(t == NT_MAX - 1)
    def _():
        y_copy().start()
        y_copy().wait()


def _moe(pos, wpair, plan, h2g, w_gate, w_up, w_down):
    wspec = lambda shape: pl.BlockSpec(
        shape, lambda s, t, pos, wpair, plan: (plan[(s * 8 + PLAN_TILE_EXPERT) * LANES + t], 0, 0))
    return pl.pallas_call(
        _moe_kernel,
        grid_spec=pltpu.PrefetchScalarGridSpec(
            num_scalar_prefetch=3,
            grid=(N_STREAMS, NT_MAX),
            in_specs=[pl.BlockSpec(memory_space=pl.ANY),
                      wspec((1, D_MODEL, D_EXPERT)), wspec((1, D_MODEL, D_EXPERT)), wspec((1, D_EXPERT, D_MODEL))],
            out_specs=pl.BlockSpec(memory_space=pl.ANY),
            scratch_shapes=[pltpu.VMEM(((T_STREAM + 1) * GROUP_ROWS, LANES), F32),
                            pltpu.VMEM(((T_STREAM + 1) * GROUP_ROWS, LANES), F32),
                            pltpu.SMEM((P_MAX,), jnp.int32),
                            pltpu.SMEM((P_MAX,), F32),
                            pltpu.VMEM((GROUP_ROWS * XSTRIDE, LANES), F32),
                            pltpu.VMEM((GROUP_ROWS * XSTRIDE, LANES), F32),
                            pltpu.VMEM((D_MODEL, D_EXPERT), BF16), pltpu.VMEM((D_MODEL, D_EXPERT), BF16),
                            pltpu.VMEM((D_EXPERT, D_MODEL), BF16),
                            pltpu.SemaphoreType.DMA((2,))]),
        out_shape=jax.ShapeDtypeStruct((N_STREAMS, T_STREAM * GROUP_ROWS, LANES), F32),
        compiler_params=pltpu.CompilerParams(
            dimension_semantics=("arbitrary", "arbitrary"), vmem_limit_bytes=VMEM_LIMIT),
        name="moe",
    )(pos, wpair, plan, h2g, w_gate, w_up, w_down)


def _groups_to_rows(src_ref, stage_ref, n_rows):
    stride = n_rows + 8
    for j in range(n_rows):
        stage_ref[pl.ds(j, GROUP_ROWS, stride=stride), :] = src_ref[GROUP_ROWS * j:GROUP_ROWS * (j + 1), :]
    return jnp.concatenate([stage_ref[c * stride:c * stride + n_rows, :] for c in range(GROUP_ROWS)], axis=1)


def _final_kernel(x1_ref, y_ref, g2_ref, lg_ref, lb_ref, o_ref, stage_ref):
    moe = _groups_to_rows(y_ref, stage_ref, TT)
    o_ref[...] = _layer_norm(ALPHA * x1_ref[...] + g2_ref[0] * moe, lg_ref[...], lb_ref[...])


def _final(x1, y, mod, ln_g, ln_b):
    row = lambda w: pl.BlockSpec((TT, w), lambda i: (i, 0))
    const = lambda s: pl.BlockSpec(s, lambda i: (0,) * len(s))
    return pl.pallas_call(
        _final_kernel,
        grid=(T_ALL // TT,),
        in_specs=[row(D_MODEL), pl.BlockSpec((TT * GROUP_ROWS, LANES), lambda i: (i, 0)),
                  _mod_spec(5, TT), const((1, D_MODEL)), const((1, D_MODEL))],
        out_specs=row(D_MODEL),
        out_shape=jax.ShapeDtypeStruct((T_ALL, D_MODEL), F32),
        scratch_shapes=[pltpu.VMEM((GROUP_ROWS * (TT + 8), LANES), F32)],
        compiler_params=pltpu.CompilerParams(dimension_semantics=("arbitrary",), vmem_limit_bytes=VMEM_LIMIT),
        name="final",
    )(x1, y, mod, ln_g, ln_b)


def _rope_tables():
    t = np.arange(DEC_SEQ)
    pos = np.stack([t // GRID_W, t % GRID_W], axis=1).astype(np.float32)
    nf = HEAD_DIM // 4
    inv = jnp.asarray(ROPE_THETA, F32) ** (-jnp.arange(nf, dtype=F32) / nf)
    d = np.arange(LANES) % HEAD_DIM
    which = d // (HEAD_DIM // 2)
    ang = jnp.asarray(pos)[:, which] * inv[d % nf][None, :]
    sign = np.where((d % 32) < 16, -1.0, 1.0).astype(np.float32)
    cos = jnp.concatenate([jnp.cos(ang), jnp.ones((TT, LANES), F32)], axis=0)
    sin = jnp.concatenate([jnp.sin(ang) * sign[None, :], jnp.zeros((TT, LANES), F32)], axis=0)
    return cos, sin


def _block_ones():
    h = np.arange(QB_W) // HEAD_DIM
    return jnp.asarray((h[:, None] == h[None, :]).astype(np.float32), dtype=BF16)


def _pool_block_diag(pool_w):
    out = jnp.zeros((POOL_WIDTH, POOL_WIDTH), F32)
    for g in range(4):
        out = out.at[64 * g:64 * g + 64, 64 * g:64 * g + 64].set(pool_w[g])
    return out


def kernel(x_prompt, x_sample, cache_b_k, cache_b_v, cache_c_k, cache_c_v, c, c_ctx, w_ada, b_ada, w_in, w_out,
           pool_w, pool_scale, q_norm, k_norm, rpb, ln1_g, ln1_b, ln2_g, ln2_b, router_g, router_g_b, router_e,
           router_e_b, w_gate, w_up, w_down):
    x = jnp.concatenate([x_prompt.reshape(T_CTX, D_MODEL), x_sample.reshape(T_LAT, D_MODEL)], axis=0)
    c8 = jnp.concatenate([c, c_ctx[None], jnp.zeros((MOD_ROWS - DEC_BATCH - 1, D_MODEL), F32)], axis=0)
    mod_all = _adaln(c8, w_ada, b_ada)

    cos_t, sin_t = _rope_tables()
    ones_bd = _block_ones()
    cbk = cache_b_k.reshape(DEC_BATCH, DEPTH, PAST_LEN, KB_W)
    cbv = cache_b_v.reshape(DEC_BATCH, DEPTH, PAST_LEN, KB_W)
    cck = cache_c_k.reshape(DEC_BATCH, DEPTH, PAST_LEN, C_W)
    ccv = cache_c_v.reshape(DEC_BATCH, DEPTH, PAST_LEN, C_W)
    pad = jnp.zeros((D_MODEL, ROUTER_LANES - N_GROUPS - N_EXPERTS), F32)

    new_bk, new_bv, new_ck, new_cv = [], [], [], []
    for l in range(DEPTH):
        mod = mod_all[l].reshape(MOD_ROWS * 6, 1, D_MODEL)
        qn = jnp.tile(q_norm[l], H_B)[None]
        kn = jnp.tile(k_norm[l], KV_B)[None]
        a, qb, kb, vb, qc, kc, vc = _proj(x, mod, w_in[l], ones_bd, qn, kn, cos_t, sin_t)
        ya = _pool(a, _pool_block_diag(pool_w[l]), pool_scale[l][None])
        yb_ctx, yc_ctx = _attn_ctx(qb, kb, vb, qc, kc, vc)
        yb_lat = _attn_latb(l, qb, kb, vb, cbk, cbv)
        yc_lat = _natten(l, qc, kc, vc, cck, ccv, _natten_table(rpb[l]))
        yb = jnp.concatenate([yb_ctx, yb_lat], axis=0)
        yc = jnp.concatenate([yc_ctx, yc_lat], axis=0)
        wr = jnp.concatenate([router_g[l], router_e[l], pad], axis=1)
        br = jnp.concatenate([router_g_b[l], router_e_b[l], pad[0]], axis=0)[None]
        x1, h2g, ids, wts = _tail(x, ya, yb, yc, w_out[l], mod, ln1_g[l][None], ln1_b[l][None], wr, br)
        pairs = lambda a: a.reshape(N_STREAMS, T_STREAM // TT, 8, TT)[:, :, :2, :].transpose(0, 2, 1, 3)
        pos, tiles = _plan(pairs(ids).reshape(N_STREAMS, 2 * T_STREAM // LANES, LANES))
        y = _moe(pos.reshape(-1), pairs(wts).reshape(-1), tiles.reshape(-1), h2g, w_gate[l], w_up[l], w_down[l])
        x = _final(x1, y.reshape(T_ALL * GROUP_ROWS, LANES), mod, ln2_g[l][None], ln2_b[l][None])
        new_bk.append(kb[:T_CTX].reshape(BATCH, SEQ, KV_B, HEAD_DIM))
        new_bv.append(vb[:T_CTX].reshape(BATCH, SEQ, KV_B, HEAD_DIM))
        new_ck.append(kc[:T_CTX].reshape(BATCH, SEQ, H_C, HEAD_DIM))
        new_cv.append(vc[:T_CTX].reshape(BATCH, SEQ, H_C, HEAD_DIM))

    y_prompt = x[:T_CTX].reshape(BATCH, SEQ, D_MODEL)
    y_sample = x[T_CTX:].reshape(DEC_BATCH, DEC_SEQ, D_MODEL)
    return (y_prompt, y_sample, jnp.stack(new_bk, axis=1), jnp.stack(new_bv, axis=1),
            jnp.stack(new_ck, axis=1), jnp.stack(new_cv, axis=1))
```

```python
import functools

import numpy as np
import jax
import jax.numpy as jnp
from jax import lax
from jax.experimental import pallas as pl
from jax.experimental.pallas import tpu as pltpu

F32 = jnp.float32
BF16 = jnp.bfloat16

D_MODEL = 1024
BATCH = 16
SEQ = 256
DEPTH = 2
DEC_BATCH = 4
DEC_SEQ = 1024
PAST_LEN = 256
GRID_W = 64
GRID_ROWS = DEC_SEQ // GRID_W
HEAD_DIM = 64
POOL_WIDTH = 256
POOL_WINDOWS = (2, 4, 8, 16)
H_B = 6
KV_B = 2
H_C = 6
WIN_R = 8
WIN_C = 16
ROPE_THETA = 10000.0
QB_W = H_B * HEAD_DIM
KB_W = KV_B * HEAD_DIM
C_W = H_C * HEAD_DIM
PROJ_WIDTH = 2048
N_GROUPS = 4
EXPERTS_PER_GROUP = 8
N_EXPERTS = 32
D_EXPERT = 256
ALPHA = (2 * DEPTH) ** 0.25
LN_EPS = 1e-6
RMS_EPS = 1e-6
NEG = -1e30
ATTN_SCALE = HEAD_DIM ** -0.5

T_CTX = BATCH * SEQ
T_LAT = DEC_BATCH * DEC_SEQ
T_ALL = T_CTX + T_LAT

LANES = 128
ROUTER_LANES = 128
EXPERT_LANE0 = N_GROUPS
MOD_ROWS = 8
CTX_MOD_ROW = DEC_BATCH

TT = 512
TP = 1024
N_STREAMS = 2
T_STREAM = T_ALL // N_STREAMS
TMX = 128
P_MAX = 2 * T_STREAM + N_EXPERTS * TMX
NT_MAX = P_MAX // TMX
GROUP_ROWS = D_MODEL // LANES
XSTRIDE = TMX + 8
RMW_BATCH = 16
PLAN_TILE_EXPERT, PLAN_N_TILES, PLAN_VALID_END, PLAN_PAD_END = 0, 1, 2, 3
HALF = DEC_SEQ // 2
NAT_KEYS = 12 * GRID_W
VMEM_LIMIT = 56 * 1024 * 1024


def _dot(a, b):
    return jnp.dot(a, b, preferred_element_type=F32)


def _dot_nt(a, b):
    return lax.dot_general(a, b, (((1,), (1,)), ((), ())), preferred_element_type=F32)


def _split_bf16(x):
    hi = x.astype(BF16)
    lo = (x - hi.astype(F32)).astype(BF16)
    return hi, lo


def _layer_norm(y, g, b):
    mu = jnp.mean(y, axis=-1, keepdims=True)
    var = jnp.mean(jnp.square(y - mu), axis=-1, keepdims=True)
    return (y - mu) * lax.rsqrt(var + LN_EPS) * g + b


def _adaln_kernel(c_ref, w_ref, b_ref, o_ref):
    c = c_ref[...]
    s = (c * jax.nn.sigmoid(c)).astype(BF16)
    o_ref[0] = _dot(s, w_ref[0].astype(BF16)) + b_ref[0]


def _adaln(c8, w_ada, b_ada):
    tn = 1536
    n = w_ada.shape[-1]
    return pl.pallas_call(
        _adaln_kernel,
        grid=(DEPTH, n // tn),
        in_specs=[
            pl.BlockSpec((MOD_ROWS, D_MODEL), lambda l, j: (0, 0)),
            pl.BlockSpec((1, D_MODEL, tn), lambda l, j: (l, 0, j)),
            pl.BlockSpec((1, 1, tn), lambda l, j: (l, 0, j)),
        ],
        out_specs=pl.BlockSpec((1, MOD_ROWS, tn), lambda l, j: (l, 0, j)),
        out_shape=jax.ShapeDtypeStruct((DEPTH, MOD_ROWS, n), F32),
        compiler_params=pltpu.CompilerParams(
            dimension_semantics=("arbitrary", "arbitrary"), vmem_limit_bytes=VMEM_LIMIT),
        name="adaln",
    )(c8, w_ada, b_ada.reshape(DEPTH, 1, n))


def _mod_row_of_tile(i, tile):
    n_ctx = T_CTX // tile
    per_req = DEC_SEQ // tile
    return jnp.where(i < n_ctx, CTX_MOD_ROW, (i - n_ctx) // per_req)


def _mod_spec(chunk, tile):
    return pl.BlockSpec((1, 1, D_MODEL), lambda i: (_mod_row_of_tile(i, tile) * 6 + chunk, 0, 0))


def _rms_norm_heads(x, ones_bd, w):
    hi, lo = _split_bf16(x * x)
    ssq = _dot(hi, ones_bd) + _dot(lo, ones_bd)
    return x * lax.rsqrt(ssq * (1.0 / HEAD_DIM) + RMS_EPS) * w


def _rope_cols(x, cos, sin, first16):
    cols = []
    for j in range(x.shape[1] // LANES):
        xc = x[:, j * LANES:(j + 1) * LANES]
        partner = jnp.where(first16, pltpu.roll(xc, LANES - 16, axis=1), pltpu.roll(xc, 16, axis=1))
        cols.append(xc * cos + partner * sin)
    return jnp.concatenate(cols, axis=1) if len(cols) > 1 else cols[0]


def _proj_kernel(x_ref, sh_ref, sc_ref, w_ref, ones_ref, qn_ref, kn_ref, cos_ref, sin_ref,
                 a_ref, qb_ref, kb_ref, vb_ref, qc_ref, kc_ref, vc_ref, wbf_ref):
    @pl.when(pl.program_id(0) == 0)
    def _():
        wbf_ref[...] = w_ref[...].astype(BF16)

    h = x_ref[...] * (1.0 + sc_ref[0]) + sh_ref[0]
    p = _dot(h.astype(BF16), wbf_ref[...])
    o = 0
    a_ref[...] = p[:, o:o + POOL_WIDTH]; o += POOL_WIDTH
    qb = p[:, o:o + QB_W]; o += QB_W
    kb = p[:, o:o + KB_W]; o += KB_W
    vb_ref[...] = p[:, o:o + KB_W]; o += KB_W
    qc_ref[...] = (p[:, o:o + C_W] * ATTN_SCALE).astype(BF16); o += C_W
    kc_ref[...] = p[:, o:o + C_W]; o += C_W
    vc_ref[...] = p[:, o:o + C_W]

    ones_bd = ones_ref[...]
    cos = cos_ref[...]
    sin = sin_ref[...]
    lane = lax.broadcasted_iota(jnp.int32, (1, LANES), 1)
    first16 = (lane & 31) < 16
    qb = _rope_cols(_rms_norm_heads(qb, ones_bd, qn_ref[...]), cos, sin, first16)
    kb = _rope_cols(_rms_norm_heads(kb, ones_bd[:KB_W, :KB_W], kn_ref[...]), cos, sin, first16)
    qb_ref[...] = (qb * ATTN_SCALE).astype(BF16)
    kb_ref[...] = kb


def _proj(x, mod, w_in, ones_bd, qn, kn, cos_t, sin_t):
    n_tiles = T_ALL // TT
    n_ctx = T_CTX // TT
    per_req = DEC_SEQ // TT

    def rope_idx(i):
        return (jnp.where(i < n_ctx, per_req, (i - n_ctx) % per_req), 0)

    row = lambda w: pl.BlockSpec((TT, w), lambda i: (i, 0))
    const = lambda s: pl.BlockSpec(s, lambda i: (0,) * len(s))
    widths = (POOL_WIDTH, QB_W, KB_W, KB_W, C_W, C_W, C_W)
    dtypes = (F32, BF16, F32, F32, BF16, F32, F32)
    return pl.pallas_call(
        _proj_kernel,
        grid=(n_tiles,),
        in_specs=[row(D_MODEL), _mod_spec(0, TT), _mod_spec(1, TT), const((D_MODEL, PROJ_WIDTH)),
                  const((QB_W, QB_W)), const((1, QB_W)), const((1, KB_W)),
                  pl.BlockSpec((TT, LANES), rope_idx), pl.BlockSpec((TT, LANES), rope_idx)],
        out_specs=[row(w) for w in widths],
        out_shape=[jax.ShapeDtypeStruct((T_ALL, w), d) for w, d in zip(widths, dtypes)],
        scratch_shapes=[pltpu.VMEM((D_MODEL, PROJ_WIDTH), BF16)],
        compiler_params=pltpu.CompilerParams(dimension_semantics=("arbitrary",), vmem_limit_bytes=VMEM_LIMIT),
        name="proj",
    )(x, mod, mod, w_in, ones_bd, qn, kn, cos_t, sin_t)


def _pool_kernel(a_ref, w_ref, scale_ref, o_ref):
    i = pl.program_id(0)
    a = a_ref[...]
    row = lax.broadcasted_iota(jnp.int32, a.shape, 0)
    lane = lax.broadcasted_iota(jnp.int32, a.shape, 1)
    seq_m1 = jnp.where(i < T_CTX // TP, SEQ - 1, DEC_SEQ - 1)
    t = row & seq_m1

    def shifted(k):
        v = pltpu.roll(a, (TP - k) % TP, axis=0) if k else a
        ok = (t + k >= 0) & (t + k <= seq_m1)
        return jnp.where(ok, v, 0.0)

    s = {k: shifted(k) for k in range(-8, 8)}
    c2 = s[-1] + s[0]
    c4 = c2 + s[-2] + s[1]
    c8 = c4 + s[-4] + s[-3] + s[2] + s[3]
    c16 = c8 + s[-8] + s[-7] + s[-6] + s[-5] + s[4] + s[5] + s[6] + s[7]
    g = lane >> 6
    csum = jnp.where(g == 0, c2, jnp.where(g == 1, c4, jnp.where(g == 2, c8, c16)))
    half = jnp.where(g == 0, 1, jnp.where(g == 1, 2, jnp.where(g == 2, 4, 8)))
    cnt = jnp.minimum(t + half, seq_m1 + 1) - jnp.maximum(t - half, 0)
    pooled = csum / cnt.astype(F32) - a
    y = _dot(pooled.astype(BF16), w_ref[...].astype(BF16)) * scale_ref[...]
    o_ref[...] = y.astype(BF16)


def _pool(a, w_bd, scale):
    return pl.pallas_call(
        _pool_kernel,
        grid=(T_ALL // TP,),
        in_specs=[pl.BlockSpec((TP, POOL_WIDTH), lambda i: (i, 0)),
                  pl.BlockSpec((POOL_WIDTH, POOL_WIDTH), lambda i: (0, 0)),
                  pl.BlockSpec((1, POOL_WIDTH), lambda i: (0, 0))],
        out_specs=pl.BlockSpec((TP, POOL_WIDTH), lambda i: (i, 0)),
        out_shape=jax.ShapeDtypeStruct((T_ALL, POOL_WIDTH), BF16),
        compiler_params=pltpu.CompilerParams(dimension_semantics=("arbitrary",), vmem_limit_bytes=VMEM_LIMIT),
        name="pool",
    )(a, w_bd, scale)


def _softmax_pv(scores, values):
    m = scores[0].max(axis=-1, keepdims=True)
    for s in scores[1:]:
        m = jnp.maximum(m, s.max(axis=-1, keepdims=True))
    ps = [jnp.exp(s - m) for s in scores]
    l = ps[0].sum(axis=-1, keepdims=True)
    for p in ps[1:]:
        l = l + p.sum(axis=-1, keepdims=True)
    r = 1.0 / l
    o = _dot((ps[0] * r).astype(BF16), values[0])
    for p, v in zip(ps[1:], values[1:]):
        o = o + _dot((p * r).astype(BF16), v)
    return o


def _lane_halves():
    lane = lax.broadcasted_iota(jnp.int32, (1, LANES), 1)
    return lane < HEAD_DIM, lane >= HEAD_DIM


def _keep(x, mask):
    return jnp.where(mask, x, 0.0).astype(BF16)


def _gqa_variants(x):
    lo, hi = _lane_halves()
    xs = pltpu.roll(x, HEAD_DIM, axis=1)
    nat_lo, nat_hi = _keep(x, lo), _keep(x, hi)
    sw_lo, sw_hi = _keep(xs, lo), _keep(xs, hi)
    return ((nat_lo, sw_hi), (nat_lo, nat_hi), (sw_lo, nat_hi))


def _mha_variants(x):
    lo, hi = _lane_halves()
    out = []
    for j in range(x.shape[1] // LANES):
        xc = x[:, j * LANES:(j + 1) * LANES]
        out.append((_keep(xc, lo), _keep(xc, hi)))
    return tuple(out)


def _attend_cols(q, k_vars, v_vars, extra_k=None, extra_v=None, bias=None):
    cols = []
    for j in range(q.shape[1] // LANES):
        qc = q[:, j * LANES:(j + 1) * LANES]
        o = None
        for hh in range(2):
            s = _dot_nt(qc, k_vars[j][hh])
            if bias is not None:
                s = s + bias[j][hh]
            scores, values = [s], [v_vars[j][hh]]
            if extra_k is not None:
                scores.append(_dot_nt(qc, extra_k[j][hh]))
                values.append(extra_v[j][hh])
            oh = _softmax_pv(scores, values)
            o = oh if o is None else o + oh
        cols.append(o)
    return jnp.concatenate(cols, axis=1)


def _attn_ctx_kernel(qb_ref, kb_ref, vb_ref, qc_ref, kc_ref, vc_ref, yb_ref, yc_ref):
    yb = _attend_cols(qb_ref[...], _gqa_variants(kb_ref[...]), _gqa_variants(vb_ref[...]))
    yb_ref[...] = yb.astype(BF16)
    yc = _attend_cols(qc_ref[...], _mha_variants(kc_ref[...]), _mha_variants(vc_ref[...]))
    yc_ref[...] = yc.astype(BF16)


def _attn_ctx(qb, kb, vb, qc, kc, vc):
    row = lambda w: pl.BlockSpec((SEQ, w), lambda i: (i, 0))
    return pl.pallas_call(
        _attn_ctx_kernel,
        grid=(BATCH,),
        in_specs=[row(QB_W), row(KB_W), row(KB_W), row(C_W), row(C_W), row(C_W)],
        out_specs=[row(QB_W), row(C_W)],
        out_shape=[jax.ShapeDtypeStruct((T_CTX, QB_W), BF16), jax.ShapeDtypeStruct((T_CTX, C_W), BF16)],
        compiler_params=pltpu.CompilerParams(dimension_semantics=("arbitrary",), vmem_limit_bytes=VMEM_LIMIT),
        name="attn_ctx",
    )(qb, kb, vb, qc, kc, vc)


def _attn_latb_kernel(q_ref, k_ref, v_ref, ck_ref, cv_ref, y_ref):
    y = _attend_cols(q_ref[...], _gqa_variants(ck_ref[0, 0]), _gqa_variants(cv_ref[0, 0]),
                     extra_k=_gqa_variants(k_ref[...]), extra_v=_gqa_variants(v_ref[...]))
    y_ref[...] = y.astype(BF16)


def _attn_latb(layer, qb, kb, vb, cache_k, cache_v):
    ctx_h = T_CTX // HALF
    ctx_r = T_CTX // DEC_SEQ
    cache = pl.BlockSpec((1, 1, PAST_LEN, KB_W), lambda b, s: (b, layer, 0, 0))
    own = pl.BlockSpec((DEC_SEQ, KB_W), lambda b, s: (ctx_r + b, 0))
    return pl.pallas_call(
        _attn_latb_kernel,
        grid=(DEC_BATCH, DEC_SEQ // HALF),
        in_specs=[pl.BlockSpec((HALF, QB_W), lambda b, s: (ctx_h + 2 * b + s, 0)), own, own, cache, cache],
        out_specs=pl.BlockSpec((HALF, QB_W), lambda b, s: (2 * b + s, 0)),
        out_shape=jax.ShapeDtypeStruct((T_LAT, QB_W), BF16),
        compiler_params=pltpu.CompilerParams(
            dimension_semantics=("arbitrary", "arbitrary"), vmem_limit_bytes=VMEM_LIMIT),
        name="attn_latb",
    )(qb, kb, vb, cache_k, cache_v)


def _natten_window(s, i):
    r = (HALF // GRID_W) * s + i
    rs = min(max(r - WIN_R // 2, 0), GRID_ROWS - WIN_R)
    return r, rs


def _natten_kernel(q_ref, k_ref, v_ref, ck_ref, cv_ref, blk_ref, y_ref, bias_ref):
    s = pl.program_id(1)

    for sv in range(2):
        @pl.when((pl.program_id(2) == 0) & (s == sv))
        def _():
            lo_half, _ = _lane_halves()
            masked = jnp.full((GRID_W, LANES), NEG, F32)
            for hh in range(2):
                for i in range(HALF // GRID_W):
                    r, rs = _natten_window(sv, i)
                    for jp in range(NAT_KEYS // LANES):
                        pair = []
                        for j in (2 * jp, 2 * jp + 1):
                            rk = 4 * sv + j
                            pair.append(blk_ref[0, hh, rk - r + WIN_R - 1] if rs <= rk < rs + WIN_R else masked)
                        bias_ref[hh, i * GRID_W:(i + 1) * GRID_W, jp * LANES:(jp + 1) * LANES] = (
                            jnp.where(lo_half, pair[0], pair[1]))

    start = pl.multiple_of(s * (4 * GRID_W), 4 * GRID_W)
    k = _mha_variants(k_ref[pl.ds(start, NAT_KEYS), :])
    v = _mha_variants(v_ref[pl.ds(start, NAT_KEYS), :])
    ck = _mha_variants(ck_ref[0, 0])
    cv = _mha_variants(cv_ref[0, 0])
    bias = ((bias_ref[0], bias_ref[1]),)
    y = _attend_cols(q_ref[...], k, v, extra_k=ck, extra_v=cv, bias=bias)
    y_ref[...] = y.astype(BF16)


def _natten(layer, qc, kc, vc, cache_k, cache_v, blocks):
    ctx_h = T_CTX // HALF
    ctx_r = T_CTX // DEC_SEQ
    cache = pl.BlockSpec((1, 1, PAST_LEN, LANES), lambda j, s, b: (b, layer, 0, j))
    own = pl.BlockSpec((DEC_SEQ, LANES), lambda j, s, b: (ctx_r + b, j))
    return pl.pallas_call(
        _natten_kernel,
        grid=(C_W // LANES, DEC_SEQ // HALF, DEC_BATCH),
        in_specs=[pl.BlockSpec((HALF, LANES), lambda j, s, b: (ctx_h + 2 * b + s, j)), own, own, cache, cache,
                  pl.BlockSpec((1, 2, 2 * WIN_R - 1, GRID_W, LANES), lambda j, s, b: (j, 0, 0, 0, 0))],
        out_specs=pl.BlockSpec((HALF, LANES), lambda j, s, b: (2 * b + s, j)),
        out_shape=jax.ShapeDtypeStruct((T_LAT, C_W), BF16),
        scratch_shapes=[pltpu.VMEM((2, HALF, NAT_KEYS), F32)],
        compiler_params=pltpu.CompilerParams(
            dimension_semantics=("arbitrary", "arbitrary", "arbitrary"), vmem_limit_bytes=VMEM_LIMIT),
        name="natten",
    )(qc, kc, vc, cache_k, cache_v, blocks)


def _natten_blocks(rpb):
    qcol = np.arange(GRID_W)[:, None]
    kcol = np.arange(GRID_W)[None, :]
    cs = np.clip(qcol - WIN_C // 2, 0, GRID_W - WIN_C)
    col_ok = (kcol >= cs) & (kcol < cs + WIN_C)
    dc = np.clip(kcol - qcol, -(WIN_C - 1), WIN_C - 1) + (WIN_C - 1)
    sel_c = (dc[:, :, None] == np.arange(2 * WIN_C - 1)).astype(np.float32)
    blk = jnp.einsum("hdm,qkm->hdqk", rpb, jnp.asarray(sel_c), precision=lax.Precision.HIGHEST)
    blk = jnp.where(jnp.asarray(col_ok), blk, NEG)
    blk = jnp.concatenate([blk, blk], axis=-1)
    return blk.reshape(H_C // 2, 2, 2 * WIN_R - 1, GRID_W, LANES)


def _route(lg):
    lane = lax.broadcasted_iota(jnp.int32, lg.shape, 1).astype(F32)
    low = jnp.float32(-3.0e38)
    far = jnp.float32(ROUTER_LANES)
    is_g = lane < N_GROUPS
    gmax = jnp.where(is_g, lg, low).max(axis=-1, keepdims=True)
    gsel = jnp.where(is_g & (lg == gmax), lane, far).min(axis=-1, keepdims=True)
    pg_sel = 1.0 / jnp.where(is_g, jnp.exp(lg - gmax), 0.0).sum(axis=-1, keepdims=True)
    e0 = EXPERT_LANE0 + EXPERTS_PER_GROUP * gsel
    in_g = (lane >= e0) & (lane < e0 + EXPERTS_PER_GROUP)
    m1 = jnp.where(in_g, lg, low).max(axis=-1, keepdims=True)
    i1 = jnp.where(in_g & (lg == m1), lane, far).min(axis=-1, keepdims=True)
    rest = in_g & (lane != i1)
    m2 = jnp.where(rest, lg, low).max(axis=-1, keepdims=True)
    i2 = jnp.where(rest & (lg == m2), lane, far).min(axis=-1, keepdims=True)
    t = jnp.exp(m2 - m1)
    ssum = 1.0 + t
    w1 = (1.0 / ssum) * pg_sel
    w2 = (t / ssum) * pg_sel
    return i1 - EXPERT_LANE0, i2 - EXPERT_LANE0, w1, w2


def _rows_to_groups(x, stage_ref, out_ref, n_rows):
    stride = n_rows + 8
    for c in range(GROUP_ROWS):
        stage_ref[c * stride:c * stride + n_rows, :] = x[:, c * LANES:(c + 1) * LANES]
    for j in range(n_rows):
        out_ref[GROUP_ROWS * j:GROUP_ROWS * (j + 1), :] = stage_ref[pl.ds(j, GROUP_ROWS, stride=stride), :]


def _tail_kernel(x_ref, ya_ref, yb_ref, yc_ref, w_ref, g1_ref, sh2_ref, sc2_ref, lg_ref, lb_ref, wr_ref, br_ref,
                 x1_ref, h2g_ref, ids_ref, wts_ref, wbf_ref, stage_ref):
    @pl.when(pl.program_id(0) == 0)
    def _():
        wbf_ref[...] = w_ref[...].astype(BF16)

    o = (_dot(ya_ref[...], wbf_ref[0:POOL_WIDTH, :])
         + _dot(yb_ref[...], wbf_ref[POOL_WIDTH:POOL_WIDTH + QB_W, :])
         + _dot(yc_ref[...], wbf_ref[POOL_WIDTH + QB_W:, :]))
    x1 = _layer_norm(ALPHA * x_ref[...] + g1_ref[0] * o, lg_ref[...], lb_ref[...])
    x1_ref[...] = x1
    h2 = x1 * (1.0 + sc2_ref[0]) + sh2_ref[0]
    hh, hl = _split_bf16(h2)
    wh, wl = _split_bf16(wr_ref[...])
    lg = _dot(hh, wh) + _dot(hl, wh) + _dot(hh, wl) + br_ref[...]
    i1, i2, w1, w2 = _route(lg)
    lane = lax.broadcasted_iota(jnp.int32, lg.shape, 1)
    ids = jnp.where(lane == 0, i1, jnp.where(lane == 1, i2, 0.0))
    ids_ref[0] = ids.T[0:8, :].astype(jnp.int32)
    wts = jnp.where(lane == 0, w1, jnp.where(lane == 1, w2, 0.0))
    wts_ref[0] = wts.T[0:8, :]
    _rows_to_groups(h2, stage_ref, h2g_ref, TT)


def _tail(x, ya, yb, yc, w_out, mod, ln_g, ln_b, wr, br):
    row = lambda w: pl.BlockSpec((TT, w), lambda i: (i, 0))
    const = lambda s: pl.BlockSpec(s, lambda i: (0,) * len(s))
    return pl.pallas_call(
        _tail_kernel,
        grid=(T_ALL // TT,),
        in_specs=[row(D_MODEL), row(POOL_WIDTH), row(QB_W), row(C_W), const((D_MODEL, D_MODEL)),
                  _mod_spec(2, TT), _mod_spec(3, TT), _mod_spec(4, TT),
                  const((1, D_MODEL)), const((1, D_MODEL)), const((D_MODEL, ROUTER_LANES)), const((1, ROUTER_LANES))],
        out_specs=[row(D_MODEL), pl.BlockSpec((TT * GROUP_ROWS, LANES), lambda i: (i, 0)),
                   pl.BlockSpec((1, 8, TT), lambda i: (i, 0, 0)), pl.BlockSpec((1, 8, TT), lambda i: (i, 0, 0))],
        out_shape=[jax.ShapeDtypeStruct((T_ALL, D_MODEL), F32),
                   jax.ShapeDtypeStruct((T_ALL * GROUP_ROWS, LANES), F32),
                   jax.ShapeDtypeStruct((T_ALL // TT, 8, TT), jnp.int32),
                   jax.ShapeDtypeStruct((T_ALL // TT, 8, TT), F32)],
        scratch_shapes=[pltpu.VMEM((D_MODEL, D_MODEL), BF16),
                        pltpu.VMEM((GROUP_ROWS * (TT + 8), LANES), F32)],
        compiler_params=pltpu.CompilerParams(dimension_semantics=("arbitrary",), vmem_limit_bytes=VMEM_LIMIT),
        name="tail",
    )(x, ya, yb, yc, w_out, mod, mod, mod, ln_g, ln_b, wr, br)


def _plan_kernel(eid_ref, pos_ref, tile_ref):
    e = eid_ref[0]
    r = lax.broadcasted_iota(jnp.int32, (LANES, LANES), 0)
    c = lax.broadcasted_iota(jnp.int32, (LANES, LANES), 1)
    upper = (r <= c).astype(BF16)
    rows = e.shape[0]
    lower = (lax.broadcasted_iota(jnp.int32, (rows, rows), 1)
             < lax.broadcasted_iota(jnp.int32, (rows, rows), 0)).astype(BF16)
    tile_start = (lax.broadcasted_iota(jnp.int32, (1, LANES), 1) * TMX).astype(F32)
    lane = lax.broadcasted_iota(jnp.int32, (1, LANES), 1)
    pos = jnp.zeros(e.shape, F32)
    base = jnp.zeros((1, LANES), F32)
    n_before = jnp.zeros((1, LANES), F32)
    valid_end = jnp.zeros((1, LANES), F32)
    pad_end = jnp.zeros((1, LANES), F32)
    for ex in range(N_EXPERTS):
        m = e == ex
        incl = _dot(m.astype(BF16), upper)
        row_tot = jnp.broadcast_to(incl[:, LANES - 1:LANES], incl.shape)
        row_off = _dot(lower, row_tot.astype(BF16))
        cnt = jnp.sum(row_tot, axis=0, keepdims=True)
        pos = jnp.where(m, base + row_off + incl - 1.0, pos)
        valid_end = jnp.where(lane == ex, base + cnt, valid_end)
        base = base + jnp.ceil(cnt * (1.0 / TMX)) * TMX
        pad_end = jnp.where(lane == ex, base, pad_end)
        n_before = n_before + (tile_start >= base).astype(F32)
    pos_ref[0] = pos.astype(jnp.int32)
    sub = lax.broadcasted_iota(jnp.int32, (8, LANES), 0)
    rows8 = lambda v: jnp.broadcast_to(v, (8, LANES))
    table = jnp.where(sub == PLAN_TILE_EXPERT, rows8(jnp.minimum(n_before, N_EXPERTS - 1.0)),
                      jnp.where(sub == PLAN_N_TILES, rows8(base * (1.0 / TMX)),
                                jnp.where(sub == PLAN_VALID_END, rows8(valid_end),
                                          jnp.where(sub == PLAN_PAD_END, rows8(pad_end), 0.0))))
    tile_ref[0] = table.astype(jnp.int32)


def _plan(eid):
    rows = 2 * T_STREAM // LANES
    return pl.pallas_call(
        _plan_kernel,
        grid=(N_STREAMS,),
        in_specs=[pl.BlockSpec((1, rows, LANES), lambda s: (s, 0, 0))],
        out_specs=[pl.BlockSpec((1, rows, LANES), lambda s: (s, 0, 0)), pl.BlockSpec((1, 8, LANES), lambda s: (s, 0, 0))],
        out_shape=[jax.ShapeDtypeStruct((N_STREAMS, rows, LANES), jnp.int32),
                   jax.ShapeDtypeStruct((N_STREAMS, 8, LANES), jnp.int32)],
        compiler_params=pltpu.CompilerParams(dimension_semantics=("arbitrary",), vmem_limit_bytes=VMEM_LIMIT),
        name="plan",
    )(eid)


def _moe_kernel(pos_ref, wpair_ref, plan_ref, h2g_hbm, wg_ref, wu_ref, wd_ref, y_hbm,
                h2v, yv, off, wcol, xg0, xg1, yg0, yg1, wgb, wub, wdb, sem):
    s = pl.program_id(0)
    t = pl.program_id(1)
    n_pairs = 2 * T_STREAM
    spare_off = T_STREAM * GROUP_ROWS
    plan = lambda row, idx: plan_ref[(s * 8 + row) * LANES + idx]
    n_tiles = plan(PLAN_N_TILES, 0)
    last = n_tiles - 1

    xgs = (xg0, xg1)
    ygs = (yg0, yg1)

    def gather(tile, dst):
        base = tile * TMX
        for j in range(TMX):
            src = pl.multiple_of(off[base + j], GROUP_ROWS)
            dst[pl.ds(j, GROUP_ROWS, stride=XSTRIDE), :] = h2v[pl.ds(src, GROUP_ROWS), :]

    def h2_copy():
        return pltpu.make_async_copy(
            h2g_hbm.at[pl.ds(pl.multiple_of(s * (T_STREAM * GROUP_ROWS), 8), T_STREAM * GROUP_ROWS)],
            h2v.at[pl.ds(0, T_STREAM * GROUP_ROWS)], sem.at[0])

    def y_copy():
        return pltpu.make_async_copy(yv.at[pl.ds(0, T_STREAM * GROUP_ROWS)], y_hbm.at[s], sem.at[1])

    @pl.when(t == 0)
    def _():
        h2_copy().start()
        yv[...] = jnp.zeros_like(yv)
        h2v[spare_off:spare_off + GROUP_ROWS, :] = jnp.zeros((GROUP_ROWS, LANES), F32)

        def pad_expert(e, carry):
            def pad_row(r, c):
                off[r] = spare_off
                wcol[pl.ds(r, 1), :] = jnp.zeros((1, LANES), F32)
                return c
            return lax.fori_loop(plan(PLAN_VALID_END, e), plan(PLAN_PAD_END, e), pad_row, carry)
        lax.fori_loop(0, N_EXPERTS, pad_expert, 0)

        def place(i, carry):
            for u in range(8):
                tok = i * 8 + u
                for slot in range(2):
                    p = s * n_pairs + slot * T_STREAM + tok
                    r = pos_ref[p]
                    off[r] = tok * GROUP_ROWS
                    wcol[pl.ds(r, 1), :] = jnp.full((1, LANES), wpair_ref[p], F32)
            return carry
        lax.fori_loop(0, T_STREAM // 8, place, 0)
        yg1[...] = jnp.zeros_like(yg1)
        h2_copy().wait()
        gather(0, xg0)

    ex = plan(PLAN_TILE_EXPERT, t)
    prev = plan(PLAN_TILE_EXPERT, jnp.maximum(t - 1, 0))
    active = t <= n_tiles

    @pl.when(active & ((t == 0) | (ex != prev)))
    def _():
        wgb[...] = wg_ref[0].astype(BF16)
        wub[...] = wu_ref[0].astype(BF16)
        wdb[...] = wd_ref[0].astype(BF16)

    def step(cur, nxt, out, done):
        gather(jnp.minimum(t + 1, last), nxt)

        base = jnp.minimum(t, last) * TMX
        x = jnp.concatenate([cur[c * XSTRIDE:c * XSTRIDE + TMX, :] for c in range(GROUP_ROWS)], axis=1).astype(BF16)
        hg = _dot(x, wgb[...])
        hu = _dot(x, wub[...])
        gate = wcol[pl.ds(pl.multiple_of(base, TMX), TMX), :]
        act = hg * jax.nn.sigmoid(hg) * hu * jnp.concatenate([gate] * (D_EXPERT // LANES), axis=1)
        y = _dot(act.astype(BF16), wdb[...])
        for c in range(GROUP_ROWS):
            out[c * XSTRIDE:c * XSTRIDE + TMX, :] = y[:, c * LANES:(c + 1) * LANES]

        base = jnp.maximum(t - 1, 0) * TMX
        for j0 in range(0, TMX, RMW_BATCH):
            updates = []
            for j in range(j0, j0 + RMW_BATCH):
                dst = pl.multiple_of(off[base + j], GROUP_ROWS)
                updates.append((dst, yv[pl.ds(dst, GROUP_ROWS), :] + done[pl.ds(j, GROUP_ROWS, stride=XSTRIDE), :]))
            for dst, val in updates:
                yv[pl.ds(dst, GROUP_ROWS), :] = val

    for par in range(2):
        @pl.when(active & ((t & 1) == par))
        def _():
            step(xgs[par], xgs[1 - par], ygs[par], ygs[1 - par])

    @pl.when(t == NT_MAX)
    def _():
        y_copy().start()
        y_copy().wait()


def _moe(pos, wpair, plan, h2g, w_gate, w_up, w_down):
    wspec = lambda shape: pl.BlockSpec(
        shape, lambda s, t, pos, wpair, plan: (plan[(s * 8 + PLAN_TILE_EXPERT) * LANES + t], 0, 0))
    return pl.pallas_call(
        _moe_kernel,
        grid_spec=pltpu.PrefetchScalarGridSpec(
            num_scalar_prefetch=3,
            grid=(N_STREAMS, NT_MAX + 1),
            in_specs=[pl.BlockSpec(memory_space=pl.ANY),
                      wspec((1, D_MODEL, D_EXPERT)), wspec((1, D_MODEL, D_EXPERT)), wspec((1, D_EXPERT, D_MODEL))],
            out_specs=pl.BlockSpec(memory_space=pl.ANY),
            scratch_shapes=[pltpu.VMEM(((T_STREAM + 1) * GROUP_ROWS, LANES), F32),
                            pltpu.VMEM(((T_STREAM + 1) * GROUP_ROWS, LANES), F32),
                            pltpu.SMEM((P_MAX,), jnp.int32),
                            pltpu.VMEM((P_MAX, LANES), F32),
                            pltpu.VMEM((GROUP_ROWS * XSTRIDE, LANES), F32),
                            pltpu.VMEM((GROUP_ROWS * XSTRIDE, LANES), F32),
                            pltpu.VMEM((GROUP_ROWS * XSTRIDE, LANES), F32),
                            pltpu.VMEM((GROUP_ROWS * XSTRIDE, LANES), F32),
                            pltpu.VMEM((D_MODEL, D_EXPERT), BF16), pltpu.VMEM((D_MODEL, D_EXPERT), BF16),
                            pltpu.VMEM((D_EXPERT, D_MODEL), BF16),
                            pltpu.SemaphoreType.DMA((2,))]),
        out_shape=jax.ShapeDtypeStruct((N_STREAMS, T_STREAM * GROUP_ROWS, LANES), F32),
        compiler_params=pltpu.CompilerParams(
            dimension_semantics=("arbitrary", "arbitrary"), vmem_limit_bytes=VMEM_LIMIT),
        name="moe",
    )(pos, wpair, plan, h2g, w_gate, w_up, w_down)


def _groups_to_rows(src_ref, stage_ref, n_rows):
    stride = n_rows + 8
    for j in range(n_rows):
        stage_ref[pl.ds(j, GROUP_ROWS, stride=stride), :] = src_ref[GROUP_ROWS * j:GROUP_ROWS * (j + 1), :]
    return jnp.concatenate([stage_ref[c * stride:c * stride + n_rows, :] for c in range(GROUP_ROWS)], axis=1)


def _final_kernel(x1_ref, y_ref, g2_ref, lg_ref, lb_ref, o_ref, stage_ref):
    moe = _groups_to_rows(y_ref, stage_ref, TT)
    o_ref[...] = _layer_norm(ALPHA * x1_ref[...] + g2_ref[0] * moe, lg_ref[...], lb_ref[...])


def _final(x1, y, mod, ln_g, ln_b):
    row = lambda w: pl.BlockSpec((TT, w), lambda i: (i, 0))
    const = lambda s: pl.BlockSpec(s, lambda i: (0,) * len(s))
    return pl.pallas_call(
        _final_kernel,
        grid=(T_ALL // TT,),
        in_specs=[row(D_MODEL), pl.BlockSpec((TT * GROUP_ROWS, LANES), lambda i: (i, 0)),
                  _mod_spec(5, TT), const((1, D_MODEL)), const((1, D_MODEL))],
        out_specs=row(D_MODEL),
        out_shape=jax.ShapeDtypeStruct((T_ALL, D_MODEL), F32),
        scratch_shapes=[pltpu.VMEM((GROUP_ROWS * (TT + 8), LANES), F32)],
        compiler_params=pltpu.CompilerParams(dimension_semantics=("arbitrary",), vmem_limit_bytes=VMEM_LIMIT),
        name="final",
    )(x1, y, mod, ln_g, ln_b)


def _rope_tables():
    t = np.arange(DEC_SEQ)
    pos = np.stack([t // GRID_W, t % GRID_W], axis=1).astype(np.float32)
    nf = HEAD_DIM // 4
    inv = jnp.asarray(ROPE_THETA, F32) ** (-jnp.arange(nf, dtype=F32) / nf)
    d = np.arange(LANES) % HEAD_DIM
    which = d // (HEAD_DIM // 2)
    ang = jnp.asarray(pos)[:, which] * inv[d % nf][None, :]
    sign = np.where((d % 32) < 16, -1.0, 1.0).astype(np.float32)
    cos = jnp.concatenate([jnp.cos(ang), jnp.ones((TT, LANES), F32)], axis=0)
    sin = jnp.concatenate([jnp.sin(ang) * sign[None, :], jnp.zeros((TT, LANES), F32)], axis=0)
    return cos, sin


def _block_ones():
    h = np.arange(QB_W) // HEAD_DIM
    return jnp.asarray((h[:, None] == h[None, :]).astype(np.float32), dtype=BF16)


def _pool_block_diag(pool_w):
    out = jnp.zeros((POOL_WIDTH, POOL_WIDTH), F32)
    for g in range(4):
        out = out.at[64 * g:64 * g + 64, 64 * g:64 * g + 64].set(pool_w[g])
    return out


def kernel(x_prompt, x_sample, cache_b_k, cache_b_v, cache_c_k, cache_c_v, c, c_ctx, w_ada, b_ada, w_in, w_out,
           pool_w, pool_scale, q_norm, k_norm, rpb, ln1_g, ln1_b, ln2_g, ln2_b, router_g, router_g_b, router_e,
           router_e_b, w_gate, w_up, w_down):
    x = jnp.concatenate([x_prompt.reshape(T_CTX, D_MODEL), x_sample.reshape(T_LAT, D_MODEL)], axis=0)
    c8 = jnp.concatenate([c, c_ctx[None], jnp.zeros((MOD_ROWS - DEC_BATCH - 1, D_MODEL), F32)], axis=0)
    mod_all = _adaln(c8, w_ada, b_ada)

    cos_t, sin_t = _rope_tables()
    ones_bd = _block_ones()
    cbk = cache_b_k.reshape(DEC_BATCH, DEPTH, PAST_LEN, KB_W)
    cbv = cache_b_v.reshape(DEC_BATCH, DEPTH, PAST_LEN, KB_W)
    cck = cache_c_k.reshape(DEC_BATCH, DEPTH, PAST_LEN, C_W)
    ccv = cache_c_v.reshape(DEC_BATCH, DEPTH, PAST_LEN, C_W)
    pad = jnp.zeros((D_MODEL, ROUTER_LANES - N_GROUPS - N_EXPERTS), F32)

    new_bk, new_bv, new_ck, new_cv = [], [], [], []
    for l in range(DEPTH):
        mod = mod_all[l].reshape(MOD_ROWS * 6, 1, D_MODEL)
        qn = jnp.tile(q_norm[l], H_B)[None]
        kn = jnp.tile(k_norm[l], KV_B)[None]
        a, qb, kb, vb, qc, kc, vc = _proj(x, mod, w_in[l], ones_bd, qn, kn, cos_t, sin_t)
        ya = _pool(a, _pool_block_diag(pool_w[l]), pool_scale[l][None])
        yb_ctx, yc_ctx = _attn_ctx(qb, kb, vb, qc, kc, vc)
        yb_lat = _attn_latb(l, qb, kb, vb, cbk, cbv)
        yc_lat = _natten(l, qc, kc, vc, cck, ccv, _natten_blocks(rpb[l]))
        yb = jnp.concatenate([yb_ctx, yb_lat], axis=0)
        yc = jnp.concatenate([yc_ctx, yc_lat], axis=0)
        wr = jnp.concatenate([router_g[l], router_e[l], pad], axis=1)
        br = jnp.concatenate([router_g_b[l], router_e_b[l], pad[0]], axis=0)[None]
        x1, h2g, ids, wts = _tail(x, ya, yb, yc, w_out[l], mod, ln1_g[l][None], ln1_b[l][None], wr, br)
        pairs = lambda a: a.reshape(N_STREAMS, T_STREAM // TT, 8, TT)[:, :, :2, :].transpose(0, 2, 1, 3)
        pos, tiles = _plan(pairs(ids).reshape(N_STREAMS, 2 * T_STREAM // LANES, LANES))
        y = _moe(pos.reshape(-1), pairs(wts).reshape(-1), tiles.reshape(-1), h2g, w_gate[l], w_up[l], w_down[l])
        x = _final(x1, y.reshape(T_ALL * GROUP_ROWS, LANES), mod, ln2_g[l][None], ln2_b[l][None])
        new_bk.append(kb[:T_CTX].reshape(BATCH, SEQ, KV_B, HEAD_DIM))
        new_bv.append(vb[:T_CTX].reshape(BATCH, SEQ, KV_B, HEAD_DIM))
        new_ck.append(kc[:T_CTX].reshape(BATCH, SEQ, H_C, HEAD_DIM))
        new_cv.append(vc[:T_CTX].reshape(BATCH, SEQ, H_C, HEAD_DIM))

    y_prompt = x[:T_CTX].reshape(BATCH, SEQ, D_MODEL)
    y_sample = x[T_CTX:].reshape(DEC_BATCH, DEC_SEQ, D_MODEL)
    return (y_prompt, y_sample, jnp.stack(new_bk, axis=1), jnp.stack(new_bv, axis=1),
            jnp.stack(new_ck, axis=1), jnp.stack(new_cv, axis=1))
```

```python
import functools

import numpy as np
import jax
import jax.numpy as jnp
from jax import lax
from jax.experimental import pallas as pl
from jax.experimental.pallas import tpu as pltpu

F32 = jnp.float32
BF16 = jnp.bfloat16

D_MODEL = 1024
BATCH = 16
SEQ = 256
DEPTH = 2
DEC_BATCH = 4
DEC_SEQ = 1024
PAST_LEN = 256
GRID_W = 64
GRID_ROWS = DEC_SEQ // GRID_W
HEAD_DIM = 64
POOL_WIDTH = 256
POOL_WINDOWS = (2, 4, 8, 16)
H_B = 6
KV_B = 2
H_C = 6
WIN_R = 8
WIN_C = 16
ROPE_THETA = 10000.0
QB_W = H_B * HEAD_DIM
KB_W = KV_B * HEAD_DIM
C_W = H_C * HEAD_DIM
PROJ_WIDTH = 2048
N_GROUPS = 4
EXPERTS_PER_GROUP = 8
N_EXPERTS = 32
D_EXPERT = 256
ALPHA = (2 * DEPTH) ** 0.25
LN_EPS = 1e-6
RMS_EPS = 1e-6
NEG = -1e30
ATTN_SCALE = HEAD_DIM ** -0.5

T_CTX = BATCH * SEQ
T_LAT = DEC_BATCH * DEC_SEQ
T_ALL = T_CTX + T_LAT

LANES = 128
ROUTER_LANES = 128
EXPERT_LANE0 = N_GROUPS
MOD_ROWS = 8
CTX_MOD_ROW = DEC_BATCH

TT = 512
TP = 1024
N_STREAMS = 2
T_STREAM = T_ALL // N_STREAMS
TMX = 128
P_MAX = 2 * T_STREAM + N_EXPERTS * TMX
NT_MAX = P_MAX // TMX
GROUP_ROWS = D_MODEL // LANES
XSTRIDE = TMX + 8
RMW_BATCH = 16
PLAN_TILE_EXPERT, PLAN_N_TILES, PLAN_VALID_END, PLAN_PAD_END = 0, 1, 2, 3
HALF = DEC_SEQ // 2
NAT_KEYS = 12 * GRID_W
VMEM_LIMIT = 56 * 1024 * 1024


def _dot(a, b):
    return jnp.dot(a, b, preferred_element_type=F32)


def _dot_nt(a, b):
    return lax.dot_general(a, b, (((1,), (1,)), ((), ())), preferred_element_type=F32)


def _split_bf16(x):
    hi = x.astype(BF16)
    lo = (x - hi.astype(F32)).astype(BF16)
    return hi, lo


def _layer_norm(y, g, b):
    mu = jnp.mean(y, axis=-1, keepdims=True)
    var = jnp.mean(jnp.square(y - mu), axis=-1, keepdims=True)
    return (y - mu) * lax.rsqrt(var + LN_EPS) * g + b


def _adaln_kernel(c_ref, w_ref, b_ref, o_ref):
    c = c_ref[...]
    s = (c * jax.nn.sigmoid(c)).astype(BF16)
    o_ref[0] = _dot(s, w_ref[0].astype(BF16)) + b_ref[0]


def _adaln(c8, w_ada, b_ada):
    tn = 1536
    n = w_ada.shape[-1]
    return pl.pallas_call(
        _adaln_kernel,
        grid=(DEPTH, n // tn),
        in_specs=[
            pl.BlockSpec((MOD_ROWS, D_MODEL), lambda l, j: (0, 0)),
            pl.BlockSpec((1, D_MODEL, tn), lambda l, j: (l, 0, j)),
            pl.BlockSpec((1, 1, tn), lambda l, j: (l, 0, j)),
        ],
        out_specs=pl.BlockSpec((1, MOD_ROWS, tn), lambda l, j: (l, 0, j)),
        out_shape=jax.ShapeDtypeStruct((DEPTH, MOD_ROWS, n), F32),
        compiler_params=pltpu.CompilerParams(
            dimension_semantics=("arbitrary", "arbitrary"), vmem_limit_bytes=VMEM_LIMIT),
        name="adaln",
    )(c8, w_ada, b_ada.reshape(DEPTH, 1, n))


def _mod_row_of_tile(i, tile):
    n_ctx = T_CTX // tile
    per_req = DEC_SEQ // tile
    return jnp.where(i < n_ctx, CTX_MOD_ROW, (i - n_ctx) // per_req)


def _mod_spec(chunk, tile):
    return pl.BlockSpec((1, 1, D_MODEL), lambda i: (_mod_row_of_tile(i, tile) * 6 + chunk, 0, 0))


def _rms_norm_heads(x, ones_bd, w):
    hi, lo = _split_bf16(x * x)
    ssq = _dot(hi, ones_bd) + _dot(lo, ones_bd)
    return x * lax.rsqrt(ssq * (1.0 / HEAD_DIM) + RMS_EPS) * w


def _rope_cols(x, cos, sin, first16):
    cols = []
    for j in range(x.shape[1] // LANES):
        xc = x[:, j * LANES:(j + 1) * LANES]
        partner = jnp.where(first16, pltpu.roll(xc, LANES - 16, axis=1), pltpu.roll(xc, 16, axis=1))
        cols.append(xc * cos + partner * sin)
    return jnp.concatenate(cols, axis=1) if len(cols) > 1 else cols[0]


def _proj_kernel(x_ref, sh_ref, sc_ref, w_ref, ones_ref, qn_ref, kn_ref, cos_ref, sin_ref,
                 a_ref, qb_ref, kb_ref, vb_ref, qc_ref, kc_ref, vc_ref, wbf_ref):
    @pl.when(pl.program_id(0) == 0)
    def _():
        wbf_ref[...] = w_ref[...].astype(BF16)

    h = x_ref[...] * (1.0 + sc_ref[0]) + sh_ref[0]
    p = _dot(h.astype(BF16), wbf_ref[...])
    o = 0
    a_ref[...] = p[:, o:o + POOL_WIDTH]; o += POOL_WIDTH
    qb = p[:, o:o + QB_W]; o += QB_W
    kb = p[:, o:o + KB_W]; o += KB_W
    vb_ref[...] = p[:, o:o + KB_W]; o += KB_W
    qc_ref[...] = (p[:, o:o + C_W] * ATTN_SCALE).astype(BF16); o += C_W
    kc_ref[...] = p[:, o:o + C_W]; o += C_W
    vc_ref[...] = p[:, o:o + C_W]

    ones_bd = ones_ref[...]
    cos = cos_ref[...]
    sin = sin_ref[...]
    lane = lax.broadcasted_iota(jnp.int32, (1, LANES), 1)
    first16 = (lane & 31) < 16
    qb = _rope_cols(_rms_norm_heads(qb, ones_bd, qn_ref[...]), cos, sin, first16)
    kb = _rope_cols(_rms_norm_heads(kb, ones_bd[:KB_W, :KB_W], kn_ref[...]), cos, sin, first16)
    qb_ref[...] = (qb * ATTN_SCALE).astype(BF16)
    kb_ref[...] = kb


def _proj(x, mod, w_in, ones_bd, qn, kn, cos_t, sin_t):
    n_tiles = T_ALL // TT
    n_ctx = T_CTX // TT
    per_req = DEC_SEQ // TT

    def rope_idx(i):
        return (jnp.where(i < n_ctx, per_req, (i - n_ctx) % per_req), 0)

    row = lambda w: pl.BlockSpec((TT, w), lambda i: (i, 0))
    const = lambda s: pl.BlockSpec(s, lambda i: (0,) * len(s))
    widths = (POOL_WIDTH, QB_W, KB_W, KB_W, C_W, C_W, C_W)
    dtypes = (F32, BF16, F32, F32, BF16, F32, F32)
    return pl.pallas_call(
        _proj_kernel,
        grid=(n_tiles,),
        in_specs=[row(D_MODEL), _mod_spec(0, TT), _mod_spec(1, TT), const((D_MODEL, PROJ_WIDTH)),
                  const((QB_W, QB_W)), const((1, QB_W)), const((1, KB_W)),
                  pl.BlockSpec((TT, LANES), rope_idx), pl.BlockSpec((TT, LANES), rope_idx)],
        out_specs=[row(w) for w in widths],
        out_shape=[jax.ShapeDtypeStruct((T_ALL, w), d) for w, d in zip(widths, dtypes)],
        scratch_shapes=[pltpu.VMEM((D_MODEL, PROJ_WIDTH), BF16)],
        compiler_params=pltpu.CompilerParams(dimension_semantics=("arbitrary",), vmem_limit_bytes=VMEM_LIMIT),
        name="proj",
    )(x, mod, mod, w_in, ones_bd, qn, kn, cos_t, sin_t)


def _pool_kernel(a_ref, w_ref, scale_ref, o_ref):
    i = pl.program_id(0)
    a = a_ref[...]
    row = lax.broadcasted_iota(jnp.int32, a.shape, 0)
    lane = lax.broadcasted_iota(jnp.int32, a.shape, 1)
    seq_m1 = jnp.where(i < T_CTX // TP, SEQ - 1, DEC_SEQ - 1)
    t = row & seq_m1

    def shifted(k):
        v = pltpu.roll(a, (TP - k) % TP, axis=0) if k else a
        ok = (t + k >= 0) & (t + k <= seq_m1)
        return jnp.where(ok, v, 0.0)

    s = {k: shifted(k) for k in range(-8, 8)}
    c2 = s[-1] + s[0]
    c4 = c2 + s[-2] + s[1]
    c8 = c4 + s[-4] + s[-3] + s[2] + s[3]
    c16 = c8 + s[-8] + s[-7] + s[-6] + s[-5] + s[4] + s[5] + s[6] + s[7]
    g = lane >> 6
    csum = jnp.where(g == 0, c2, jnp.where(g == 1, c4, jnp.where(g == 2, c8, c16)))
    half = jnp.where(g == 0, 1, jnp.where(g == 1, 2, jnp.where(g == 2, 4, 8)))
    cnt = jnp.minimum(t + half, seq_m1 + 1) - jnp.maximum(t - half, 0)
    pooled = csum / cnt.astype(F32) - a
    y = _dot(pooled.astype(BF16), w_ref[...].astype(BF16)) * scale_ref[...]
    o_ref[...] = y.astype(BF16)


def _pool(a, w_bd, scale):
    return pl.pallas_call(
        _pool_kernel,
        grid=(T_ALL // TP,),
        in_specs=[pl.BlockSpec((TP, POOL_WIDTH), lambda i: (i, 0)),
                  pl.BlockSpec((POOL_WIDTH, POOL_WIDTH), lambda i: (0, 0)),
                  pl.BlockSpec((1, POOL_WIDTH), lambda i: (0, 0))],
        out_specs=pl.BlockSpec((TP, POOL_WIDTH), lambda i: (i, 0)),
        out_shape=jax.ShapeDtypeStruct((T_ALL, POOL_WIDTH), BF16),
        compiler_params=pltpu.CompilerParams(dimension_semantics=("arbitrary",), vmem_limit_bytes=VMEM_LIMIT),
        name="pool",
    )(a, w_bd, scale)


def _softmax_pv(scores, values):
    m = scores[0].max(axis=-1, keepdims=True)
    for s in scores[1:]:
        m = jnp.maximum(m, s.max(axis=-1, keepdims=True))
    ps = [jnp.exp(s - m) for s in scores]
    l = ps[0].sum(axis=-1, keepdims=True)
    for p in ps[1:]:
        l = l + p.sum(axis=-1, keepdims=True)
    r = 1.0 / l
    o = _dot((ps[0] * r).astype(BF16), values[0])
    for p, v in zip(ps[1:], values[1:]):
        o = o + _dot((p * r).astype(BF16), v)
    return o


def _lane_halves():
    lane = lax.broadcasted_iota(jnp.int32, (1, LANES), 1)
    return lane < HEAD_DIM, lane >= HEAD_DIM


def _keep(x, mask):
    return jnp.where(mask, x, 0.0).astype(BF16)


def _gqa_variants(x):
    lo, hi = _lane_halves()
    xs = pltpu.roll(x, HEAD_DIM, axis=1)
    nat_lo, nat_hi = _keep(x, lo), _keep(x, hi)
    sw_lo, sw_hi = _keep(xs, lo), _keep(xs, hi)
    return ((nat_lo, sw_hi), (nat_lo, nat_hi), (sw_lo, nat_hi))


def _mha_variants(x):
    lo, hi = _lane_halves()
    out = []
    for j in range(x.shape[1] // LANES):
        xc = x[:, j * LANES:(j + 1) * LANES]
        out.append((_keep(xc, lo), _keep(xc, hi)))
    return tuple(out)


def _attend_cols(q, k_vars, v_vars, extra_k=None, extra_v=None, bias=None):
    cols = []
    for j in range(q.shape[1] // LANES):
        qc = q[:, j * LANES:(j + 1) * LANES]
        o = None
        for hh in range(2):
            s = _dot_nt(qc, k_vars[j][hh])
            if bias is not None:
                s = s + bias[j][hh]
            scores, values = [s], [v_vars[j][hh]]
            if extra_k is not None:
                scores.append(_dot_nt(qc, extra_k[j][hh]))
                values.append(extra_v[j][hh])
            oh = _softmax_pv(scores, values)
            o = oh if o is None else o + oh
        cols.append(o)
    return jnp.concatenate(cols, axis=1)


def _attn_ctx_kernel(qb_ref, kb_ref, vb_ref, qc_ref, kc_ref, vc_ref, yb_ref, yc_ref):
    yb = _attend_cols(qb_ref[...], _gqa_variants(kb_ref[...]), _gqa_variants(vb_ref[...]))
    yb_ref[...] = yb.astype(BF16)
    yc = _attend_cols(qc_ref[...], _mha_variants(kc_ref[...]), _mha_variants(vc_ref[...]))
    yc_ref[...] = yc.astype(BF16)


def _attn_ctx(qb, kb, vb, qc, kc, vc):
    row = lambda w: pl.BlockSpec((SEQ, w), lambda i: (i, 0))
    return pl.pallas_call(
        _attn_ctx_kernel,
        grid=(BATCH,),
        in_specs=[row(QB_W), row(KB_W), row(KB_W), row(C_W), row(C_W), row(C_W)],
        out_specs=[row(QB_W), row(C_W)],
        out_shape=[jax.ShapeDtypeStruct((T_CTX, QB_W), BF16), jax.ShapeDtypeStruct((T_CTX, C_W), BF16)],
        compiler_params=pltpu.CompilerParams(dimension_semantics=("arbitrary",), vmem_limit_bytes=VMEM_LIMIT),
        name="attn_ctx",
    )(qb, kb, vb, qc, kc, vc)


def _attn_latb_kernel(q_ref, k_ref, v_ref, ck_ref, cv_ref, y_ref):
    y = _attend_cols(q_ref[...], _gqa_variants(ck_ref[0, 0]), _gqa_variants(cv_ref[0, 0]),
                     extra_k=_gqa_variants(k_ref[...]), extra_v=_gqa_variants(v_ref[...]))
    y_ref[...] = y.astype(BF16)


def _attn_latb(layer, qb, kb, vb, cache_k, cache_v):
    ctx_h = T_CTX // HALF
    ctx_r = T_CTX // DEC_SEQ
    cache = pl.BlockSpec((1, 1, PAST_LEN, KB_W), lambda b, s: (b, layer, 0, 0))
    own = pl.BlockSpec((DEC_SEQ, KB_W), lambda b, s: (ctx_r + b, 0))
    return pl.pallas_call(
        _attn_latb_kernel,
        grid=(DEC_BATCH, DEC_SEQ // HALF),
        in_specs=[pl.BlockSpec((HALF, QB_W), lambda b, s: (ctx_h + 2 * b + s, 0)), own, own, cache, cache],
        out_specs=pl.BlockSpec((HALF, QB_W), lambda b, s: (2 * b + s, 0)),
        out_shape=jax.ShapeDtypeStruct((T_LAT, QB_W), BF16),
        compiler_params=pltpu.CompilerParams(
            dimension_semantics=("arbitrary", "arbitrary"), vmem_limit_bytes=VMEM_LIMIT),
        name="attn_latb",
    )(qb, kb, vb, cache_k, cache_v)


def _natten_window(s, i):
    r = (HALF // GRID_W) * s + i
    rs = min(max(r - WIN_R // 2, 0), GRID_ROWS - WIN_R)
    return r, rs


def _natten_kernel(q_ref, k_ref, v_ref, ck_ref, cv_ref, blk_ref, y_ref, bias_ref):
    s = pl.program_id(1)

    for sv in range(2):
        @pl.when((pl.program_id(2) == 0) & (s == sv))
        def _():
            lo_half, _ = _lane_halves()
            masked = jnp.full((GRID_W, LANES), NEG, F32)
            for hh in range(2):
                for i in range(HALF // GRID_W):
                    r, rs = _natten_window(sv, i)
                    for jp in range(NAT_KEYS // LANES):
                        pair = []
                        for j in (2 * jp, 2 * jp + 1):
                            rk = 4 * sv + j
                            pair.append(blk_ref[0, hh, rk - r + WIN_R - 1] if rs <= rk < rs + WIN_R else masked)
                        bias_ref[hh, i * GRID_W:(i + 1) * GRID_W, jp * LANES:(jp + 1) * LANES] = (
                            jnp.where(lo_half, pair[0], pair[1]))

    start = pl.multiple_of(s * (4 * GRID_W), 4 * GRID_W)
    k = _mha_variants(k_ref[pl.ds(start, NAT_KEYS), :])
    v = _mha_variants(v_ref[pl.ds(start, NAT_KEYS), :])
    ck = _mha_variants(ck_ref[0, 0])
    cv = _mha_variants(cv_ref[0, 0])
    bias = ((bias_ref[0], bias_ref[1]),)
    y = _attend_cols(q_ref[...], k, v, extra_k=ck, extra_v=cv, bias=bias)
    y_ref[...] = y.astype(BF16)


def _natten(layer, qc, kc, vc, cache_k, cache_v, blocks):
    ctx_h = T_CTX // HALF
    ctx_r = T_CTX // DEC_SEQ
    cache = pl.BlockSpec((1, 1, PAST_LEN, LANES), lambda j, s, b: (b, layer, 0, j))
    own = pl.BlockSpec((DEC_SEQ, LANES), lambda j, s, b: (ctx_r + b, j))
    return pl.pallas_call(
        _natten_kernel,
        grid=(C_W // LANES, DEC_SEQ // HALF, DEC_BATCH),
        in_specs=[pl.BlockSpec((HALF, LANES), lambda j, s, b: (ctx_h + 2 * b + s, j)), own, own, cache, cache,
                  pl.BlockSpec((1, 2, 2 * WIN_R - 1, GRID_W, LANES), lambda j, s, b: (j, 0, 0, 0, 0))],
        out_specs=pl.BlockSpec((HALF, LANES), lambda j, s, b: (2 * b + s, j)),
        out_shape=jax.ShapeDtypeStruct((T_LAT, C_W), BF16),
        scratch_shapes=[pltpu.VMEM((2, HALF, NAT_KEYS), F32)],
        compiler_params=pltpu.CompilerParams(
            dimension_semantics=("arbitrary", "arbitrary", "arbitrary"), vmem_limit_bytes=VMEM_LIMIT),
        name="natten",
    )(qc, kc, vc, cache_k, cache_v, blocks)


def _natten_blocks(rpb):
    qcol = np.arange(GRID_W)[:, None]
    kcol = np.arange(GRID_W)[None, :]
    cs = np.clip(qcol - WIN_C // 2, 0, GRID_W - WIN_C)
    col_ok = (kcol >= cs) & (kcol < cs + WIN_C)
    dc = np.clip(kcol - qcol, -(WIN_C - 1), WIN_C - 1) + (WIN_C - 1)
    sel_c = (dc[:, :, None] == np.arange(2 * WIN_C - 1)).astype(np.float32)
    blk = jnp.einsum("hdm,qkm->hdqk", rpb, jnp.asarray(sel_c), precision=lax.Precision.HIGHEST)
    blk = jnp.where(jnp.asarray(col_ok), blk, NEG)
    blk = jnp.concatenate([blk, blk], axis=-1)
    return blk.reshape(H_C // 2, 2, 2 * WIN_R - 1, GRID_W, LANES)


def _route(lg):
    lane = lax.broadcasted_iota(jnp.int32, lg.shape, 1).astype(F32)
    low = jnp.float32(-3.0e38)
    far = jnp.float32(ROUTER_LANES)
    is_g = lane < N_GROUPS
    gmax = jnp.where(is_g, lg, low).max(axis=-1, keepdims=True)
    gsel = jnp.where(is_g & (lg == gmax), lane, far).min(axis=-1, keepdims=True)
    pg_sel = 1.0 / jnp.where(is_g, jnp.exp(lg - gmax), 0.0).sum(axis=-1, keepdims=True)
    e0 = EXPERT_LANE0 + EXPERTS_PER_GROUP * gsel
    in_g = (lane >= e0) & (lane < e0 + EXPERTS_PER_GROUP)
    m1 = jnp.where(in_g, lg, low).max(axis=-1, keepdims=True)
    i1 = jnp.where(in_g & (lg == m1), lane, far).min(axis=-1, keepdims=True)
    rest = in_g & (lane != i1)
    m2 = jnp.where(rest, lg, low).max(axis=-1, keepdims=True)
    i2 = jnp.where(rest & (lg == m2), lane, far).min(axis=-1, keepdims=True)
    t = jnp.exp(m2 - m1)
    ssum = 1.0 + t
    w1 = (1.0 / ssum) * pg_sel
    w2 = (t / ssum) * pg_sel
    return i1 - EXPERT_LANE0, i2 - EXPERT_LANE0, w1, w2


def _rows_to_groups(x, stage_ref, out_ref, n_rows):
    stride = n_rows + 8
    for c in range(GROUP_ROWS):
        stage_ref[c * stride:c * stride + n_rows, :] = x[:, c * LANES:(c + 1) * LANES]
    for j in range(n_rows):
        out_ref[GROUP_ROWS * j:GROUP_ROWS * (j + 1), :] = stage_ref[pl.ds(j, GROUP_ROWS, stride=stride), :]


def _tail_kernel(x_ref, ya_ref, yb_ref, yc_ref, w_ref, g1_ref, sh2_ref, sc2_ref, lg_ref, lb_ref, wr_ref, br_ref,
                 x1_ref, h2g_ref, ids_ref, wts_ref, wbf_ref, stage_ref):
    @pl.when(pl.program_id(0) == 0)
    def _():
        wbf_ref[...] = w_ref[...].astype(BF16)

    o = (_dot(ya_ref[...], wbf_ref[0:POOL_WIDTH, :])
         + _dot(yb_ref[...], wbf_ref[POOL_WIDTH:POOL_WIDTH + QB_W, :])
         + _dot(yc_ref[...], wbf_ref[POOL_WIDTH + QB_W:, :]))
    x1 = _layer_norm(ALPHA * x_ref[...] + g1_ref[0] * o, lg_ref[...], lb_ref[...])
    x1_ref[...] = x1
    h2 = x1 * (1.0 + sc2_ref[0]) + sh2_ref[0]
    hh, hl = _split_bf16(h2)
    wh, wl = _split_bf16(wr_ref[...])
    lg = _dot(hh, wh) + _dot(hl, wh) + _dot(hh, wl) + br_ref[...]
    i1, i2, w1, w2 = _route(lg)
    lane = lax.broadcasted_iota(jnp.int32, lg.shape, 1)
    ids = jnp.where(lane == 0, i1, jnp.where(lane == 1, i2, 0.0))
    ids_ref[0] = ids.T[0:8, :].astype(jnp.int32)
    wts = jnp.where(lane == 0, w1, jnp.where(lane == 1, w2, 0.0))
    wts_ref[0] = wts.T[0:8, :]
    _rows_to_groups(h2, stage_ref, h2g_ref, TT)


def _tail(x, ya, yb, yc, w_out, mod, ln_g, ln_b, wr, br):
    row = lambda w: pl.BlockSpec((TT, w), lambda i: (i, 0))
    const = lambda s: pl.BlockSpec(s, lambda i: (0,) * len(s))
    return pl.pallas_call(
        _tail_kernel,
        grid=(T_ALL // TT,),
        in_specs=[row(D_MODEL), row(POOL_WIDTH), row(QB_W), row(C_W), const((D_MODEL, D_MODEL)),
                  _mod_spec(2, TT), _mod_spec(3, TT), _mod_spec(4, TT),
                  const((1, D_MODEL)), const((1, D_MODEL)), const((D_MODEL, ROUTER_LANES)), const((1, ROUTER_LANES))],
        out_specs=[row(D_MODEL), pl.BlockSpec((TT * GROUP_ROWS, LANES), lambda i: (i, 0)),
                   pl.BlockSpec((1, 8, TT), lambda i: (i, 0, 0)), pl.BlockSpec((1, 8, TT), lambda i: (i, 0, 0))],
        out_shape=[jax.ShapeDtypeStruct((T_ALL, D_MODEL), F32),
                   jax.ShapeDtypeStruct((T_ALL * GROUP_ROWS, LANES), F32),
                   jax.ShapeDtypeStruct((T_ALL // TT, 8, TT), jnp.int32),
                   jax.ShapeDtypeStruct((T_ALL // TT, 8, TT), F32)],
        scratch_shapes=[pltpu.VMEM((D_MODEL, D_MODEL), BF16),
                        pltpu.VMEM((GROUP_ROWS * (TT + 8), LANES), F32)],
        compiler_params=pltpu.CompilerParams(dimension_semantics=("arbitrary",), vmem_limit_bytes=VMEM_LIMIT),
        name="tail",
    )(x, ya, yb, yc, w_out, mod, mod, mod, ln_g, ln_b, wr, br)


def _plan_kernel(eid_ref, pos_ref, tile_ref):
    e = eid_ref[0]
    r = lax.broadcasted_iota(jnp.int32, (LANES, LANES), 0)
    c = lax.broadcasted_iota(jnp.int32, (LANES, LANES), 1)
    upper = (r <= c).astype(BF16)
    rows = e.shape[0]
    lower = (lax.broadcasted_iota(jnp.int32, (rows, rows), 1)
             < lax.broadcasted_iota(jnp.int32, (rows, rows), 0)).astype(BF16)
    tile_start = (lax.broadcasted_iota(jnp.int32, (1, LANES), 1) * TMX).astype(F32)
    lane = lax.broadcasted_iota(jnp.int32, (1, LANES), 1)
    pos = jnp.zeros(e.shape, F32)
    base = jnp.zeros((1, LANES), F32)
    n_before = jnp.zeros((1, LANES), F32)
    valid_end = jnp.zeros((1, LANES), F32)
    pad_end = jnp.zeros((1, LANES), F32)
    for ex in range(N_EXPERTS):
        m = e == ex
        incl = _dot(m.astype(BF16), upper)
        row_tot = jnp.broadcast_to(incl[:, LANES - 1:LANES], incl.shape)
        row_off = _dot(lower, row_tot.astype(BF16))
        cnt = jnp.sum(row_tot, axis=0, keepdims=True)
        pos = jnp.where(m, base + row_off + incl - 1.0, pos)
        valid_end = jnp.where(lane == ex, base + cnt, valid_end)
        base = base + jnp.ceil(cnt * (1.0 / TMX)) * TMX
        pad_end = jnp.where(lane == ex, base, pad_end)
        n_before = n_before + (tile_start >= base).astype(F32)
    pos_ref[0] = pos.astype(jnp.int32)
    sub = lax.broadcasted_iota(jnp.int32, (8, LANES), 0)
    rows8 = lambda v: jnp.broadcast_to(v, (8, LANES))
    table = jnp.where(sub == PLAN_TILE_EXPERT, rows8(jnp.minimum(n_before, N_EXPERTS - 1.0)),
                      jnp.where(sub == PLAN_N_TILES, rows8(base * (1.0 / TMX)),
                                jnp.where(sub == PLAN_VALID_END, rows8(valid_end),
                                          jnp.where(sub == PLAN_PAD_END, rows8(pad_end), 0.0))))
    tile_ref[0] = table.astype(jnp.int32)


def _plan(eid):
    rows = 2 * T_STREAM // LANES
    return pl.pallas_call(
        _plan_kernel,
        grid=(N_STREAMS,),
        in_specs=[pl.BlockSpec((1, rows, LANES), lambda s: (s, 0, 0))],
        out_specs=[pl.BlockSpec((1, rows, LANES), lambda s: (s, 0, 0)), pl.BlockSpec((1, 8, LANES), lambda s: (s, 0, 0))],
        out_shape=[jax.ShapeDtypeStruct((N_STREAMS, rows, LANES), jnp.int32),
                   jax.ShapeDtypeStruct((N_STREAMS, 8, LANES), jnp.int32)],
        compiler_params=pltpu.CompilerParams(dimension_semantics=("arbitrary",), vmem_limit_bytes=VMEM_LIMIT),
        name="plan",
    )(eid)


def _moe_kernel(pos_ref, wpair_ref, plan_ref, h2g_hbm, wg_ref, wu_ref, wd_ref, y_hbm,
                h2v, yv, off, wcol, xg0, xg1, yg0, yg1, wgb, wub, wdb, sem):
    s = pl.program_id(0)
    e = pl.program_id(1)
    n_pairs = 2 * T_STREAM
    spare_off = T_STREAM * GROUP_ROWS
    plan = lambda row, idx: plan_ref[(s * 8 + row) * LANES + idx]
    n_tiles = plan(PLAN_N_TILES, 0)
    last = n_tiles - 1

    xgs = (xg0, xg1)
    ygs = (yg0, yg1)

    def gather(tile, dst):
        base = tile * TMX
        for j in range(TMX):
            src = pl.multiple_of(off[base + j], GROUP_ROWS)
            dst[pl.ds(j, GROUP_ROWS, stride=XSTRIDE), :] = h2v[pl.ds(src, GROUP_ROWS), :]

    def h2_copy():
        return pltpu.make_async_copy(
            h2g_hbm.at[pl.ds(pl.multiple_of(s * (T_STREAM * GROUP_ROWS), 8), T_STREAM * GROUP_ROWS)],
            h2v.at[pl.ds(0, T_STREAM * GROUP_ROWS)], sem.at[0])

    def y_copy():
        return pltpu.make_async_copy(yv.at[pl.ds(0, T_STREAM * GROUP_ROWS)], y_hbm.at[s], sem.at[1])

    @pl.when(e == 0)
    def _():
        h2_copy().start()
        yv[...] = jnp.zeros_like(yv)
        h2v[spare_off:spare_off + GROUP_ROWS, :] = jnp.zeros((GROUP_ROWS, LANES), F32)

        def pad_expert(ex, carry):
            def pad_row(r, c):
                off[r] = spare_off
                wcol[pl.ds(r, 1), :] = jnp.zeros((1, LANES), F32)
                return c
            return lax.fori_loop(plan(PLAN_VALID_END, ex), plan(PLAN_PAD_END, ex), pad_row, carry)
        lax.fori_loop(0, N_EXPERTS, pad_expert, 0)

        def place(i, carry):
            for u in range(8):
                tok = i * 8 + u
                for slot in range(2):
                    p = s * n_pairs + slot * T_STREAM + tok
                    r = pos_ref[p]
                    off[r] = tok * GROUP_ROWS
                    wcol[pl.ds(r, 1), :] = jnp.full((1, LANES), wpair_ref[p], F32)
            return carry
        lax.fori_loop(0, T_STREAM // 8, place, 0)
        yg1[...] = jnp.zeros_like(yg1)
        h2_copy().wait()
        gather(0, xg0)

    t_lo = jnp.where(e == 0, 0, plan(PLAN_PAD_END, jnp.maximum(e - 1, 0))) // TMX
    t_hi = plan(PLAN_PAD_END, e) // TMX

    @pl.when(t_hi > t_lo)
    def _():
        wgb[...] = wg_ref[0].astype(BF16)
        wub[...] = wu_ref[0].astype(BF16)
        wdb[...] = wd_ref[0].astype(BF16)

    def add_tile(tile, done):
        base = tile * TMX
        for j0 in range(0, TMX, RMW_BATCH):
            updates = []
            for j in range(j0, j0 + RMW_BATCH):
                dst = pl.multiple_of(off[base + j], GROUP_ROWS)
                updates.append((dst, yv[pl.ds(dst, GROUP_ROWS), :] + done[pl.ds(j, GROUP_ROWS, stride=XSTRIDE), :]))
            for dst, val in updates:
                yv[pl.ds(dst, GROUP_ROWS), :] = val

    def step(t, cur, nxt, out, done):
        gather(jnp.minimum(t + 1, last), nxt)
        x = jnp.concatenate([cur[c * XSTRIDE:c * XSTRIDE + TMX, :] for c in range(GROUP_ROWS)], axis=1).astype(BF16)
        hg = _dot(x, wgb[...])
        hu = _dot(x, wub[...])
        gate = wcol[pl.ds(pl.multiple_of(t * TMX, TMX), TMX), :]
        act = hg * jax.nn.sigmoid(hg) * hu * jnp.concatenate([gate] * (D_EXPERT // LANES), axis=1)
        y = _dot(act.astype(BF16), wdb[...])
        for c in range(GROUP_ROWS):
            out[c * XSTRIDE:c * XSTRIDE + TMX, :] = y[:, c * LANES:(c + 1) * LANES]
        add_tile(jnp.maximum(t - 1, 0), done)

    def tile_body(t, carry):
        for par in range(2):
            @pl.when((t & 1) == par)
            def _():
                step(t, xgs[par], xgs[1 - par], ygs[par], ygs[1 - par])
        return carry
    lax.fori_loop(t_lo, t_hi, tile_body, 0)

    @pl.when(e == N_EXPERTS - 1)
    def _():
        for par in range(2):
            @pl.when((last & 1) == par)
            def _():
                add_tile(last, ygs[par])
        y_copy().start()
        y_copy().wait()


def _moe(pos, wpair, plan, h2g, w_gate, w_up, w_down):
    wspec = lambda shape: pl.BlockSpec(shape, lambda s, e, pos, wpair, plan: (e, 0, 0))
    return pl.pallas_call(
        _moe_kernel,
        grid_spec=pltpu.PrefetchScalarGridSpec(
            num_scalar_prefetch=3,
            grid=(N_STREAMS, N_EXPERTS),
            in_specs=[pl.BlockSpec(memory_space=pl.ANY),
                      wspec((1, D_MODEL, D_EXPERT)), wspec((1, D_MODEL, D_EXPERT)), wspec((1, D_EXPERT, D_MODEL))],
            out_specs=pl.BlockSpec(memory_space=pl.ANY),
            scratch_shapes=[pltpu.VMEM(((T_STREAM + 1) * GROUP_ROWS, LANES), F32),
                            pltpu.VMEM(((T_STREAM + 1) * GROUP_ROWS, LANES), F32),
                            pltpu.SMEM((P_MAX,), jnp.int32),
                            pltpu.VMEM((P_MAX, LANES), F32),
                            pltpu.VMEM((GROUP_ROWS * XSTRIDE, LANES), F32),
                            pltpu.VMEM((GROUP_ROWS * XSTRIDE, LANES), F32),
                            pltpu.VMEM((GROUP_ROWS * XSTRIDE, LANES), F32),
                            pltpu.VMEM((GROUP_ROWS * XSTRIDE, LANES), F32),
                            pltpu.VMEM((D_MODEL, D_EXPERT), BF16), pltpu.VMEM((D_MODEL, D_EXPERT), BF16),
                            pltpu.VMEM((D_EXPERT, D_MODEL), BF16),
                            pltpu.SemaphoreType.DMA((2,))]),
        out_shape=jax.ShapeDtypeStruct((N_STREAMS, T_STREAM * GROUP_ROWS, LANES), F32),
        compiler_params=pltpu.CompilerParams(
            dimension_semantics=("arbitrary", "arbitrary"), vmem_limit_bytes=VMEM_LIMIT),
        name="moe",
    )(pos, wpair, plan, h2g, w_gate, w_up, w_down)


def _groups_to_rows(src_ref, stage_ref, n_rows):
    stride = n_rows + 8
    for j in range(n_rows):
        stage_ref[pl.ds(j, GROUP_ROWS, stride=stride), :] = src_ref[GROUP_ROWS * j:GROUP_ROWS * (j + 1), :]
    return jnp.concatenate([stage_ref[c * stride:c * stride + n_rows, :] for c in range(GROUP_ROWS)], axis=1)


def _final_kernel(x1_ref, y_ref, g2_ref, lg_ref, lb_ref, o_ref, stage_ref):
    moe = _groups_to_rows(y_ref, stage_ref, TT)
    o_ref[...] = _layer_norm(ALPHA * x1_ref[...] + g2_ref[0] * moe, lg_ref[...], lb_ref[...])


def _final(x1, y, mod, ln_g, ln_b):
    row = lambda w: pl.BlockSpec((TT, w), lambda i: (i, 0))
    const = lambda s: pl.BlockSpec(s, lambda i: (0,) * len(s))
    return pl.pallas_call(
        _final_kernel,
        grid=(T_ALL // TT,),
        in_specs=[row(D_MODEL), pl.BlockSpec((TT * GROUP_ROWS, LANES), lambda i: (i, 0)),
                  _mod_spec(5, TT), const((1, D_MODEL)), const((1, D_MODEL))],
        out_specs=row(D_MODEL),
        out_shape=jax.ShapeDtypeStruct((T_ALL, D_MODEL), F32),
        scratch_shapes=[pltpu.VMEM((GROUP_ROWS * (TT + 8), LANES), F32)],
        compiler_params=pltpu.CompilerParams(dimension_semantics=("arbitrary",), vmem_limit_bytes=VMEM_LIMIT),
        name="final",
    )(x1, y, mod, ln_g, ln_b)


def _rope_tables():
    t = np.arange(DEC_SEQ)
    pos = np.stack([t // GRID_W, t % GRID_W], axis=1).astype(np.float32)
    nf = HEAD_DIM // 4
    inv = jnp.asarray(ROPE_THETA, F32) ** (-jnp.arange(nf, dtype=F32) / nf)
    d = np.arange(LANES) % HEAD_DIM
    which = d // (HEAD_DIM // 2)
    ang = jnp.asarray(pos)[:, which] * inv[d % nf][None, :]
    sign = np.where((d % 32) < 16, -1.0, 1.0).astype(np.float32)
    cos = jnp.concatenate([jnp.cos(ang), jnp.ones((TT, LANES), F32)], axis=0)
    sin = jnp.concatenate([jnp.sin(ang) * sign[None, :], jnp.zeros((TT, LANES), F32)], axis=0)
    return cos, sin


def _block_ones():
    h = np.arange(QB_W) // HEAD_DIM
    return jnp.asarray((h[:, None] == h[None, :]).astype(np.float32), dtype=BF16)


def _pool_block_diag(pool_w):
    out = jnp.zeros((POOL_WIDTH, POOL_WIDTH), F32)
    for g in range(4):
        out = out.at[64 * g:64 * g + 64, 64 * g:64 * g + 64].set(pool_w[g])
    return out


def kernel(x_prompt, x_sample, cache_b_k, cache_b_v, cache_c_k, cache_c_v, c, c_ctx, w_ada, b_ada, w_in, w_out,
           pool_w, pool_scale, q_norm, k_norm, rpb, ln1_g, ln1_b, ln2_g, ln2_b, router_g, router_g_b, router_e,
           router_e_b, w_gate, w_up, w_down):
    x = jnp.concatenate([x_prompt.reshape(T_CTX, D_MODEL), x_sample.reshape(T_LAT, D_MODEL)], axis=0)
    c8 = jnp.concatenate([c, c_ctx[None], jnp.zeros((MOD_ROWS - DEC_BATCH - 1, D_MODEL), F32)], axis=0)
    mod_all = _adaln(c8, w_ada, b_ada)

    cos_t, sin_t = _rope_tables()
    ones_bd = _block_ones()
    cbk = cache_b_k.reshape(DEC_BATCH, DEPTH, PAST_LEN, KB_W)
    cbv = cache_b_v.reshape(DEC_BATCH, DEPTH, PAST_LEN, KB_W)
    cck = cache_c_k.reshape(DEC_BATCH, DEPTH, PAST_LEN, C_W)
    ccv = cache_c_v.reshape(DEC_BATCH, DEPTH, PAST_LEN, C_W)
    pad = jnp.zeros((D_MODEL, ROUTER_LANES - N_GROUPS - N_EXPERTS), F32)

    new_bk, new_bv, new_ck, new_cv = [], [], [], []
    for l in range(DEPTH):
        mod = mod_all[l].reshape(MOD_ROWS * 6, 1, D_MODEL)
        qn = jnp.tile(q_norm[l], H_B)[None]
        kn = jnp.tile(k_norm[l], KV_B)[None]
        a, qb, kb, vb, qc, kc, vc = _proj(x, mod, w_in[l], ones_bd, qn, kn, cos_t, sin_t)
        ya = _pool(a, _pool_block_diag(pool_w[l]), pool_scale[l][None])
        yb_ctx, yc_ctx = _attn_ctx(qb, kb, vb, qc, kc, vc)
        yb_lat = _attn_latb(l, qb, kb, vb, cbk, cbv)
        yc_lat = _natten(l, qc, kc, vc, cck, ccv, _natten_blocks(rpb[l]))
        yb = jnp.concatenate([yb_ctx, yb_lat], axis=0)
        yc = jnp.concatenate([yc_ctx, yc_lat], axis=0)
        wr = jnp.concatenate([router_g[l], router_e[l], pad], axis=1)
        br = jnp.concatenate([router_g_b[l], router_e_b[l], pad[0]], axis=0)[None]
        x1, h2g, ids, wts = _tail(x, ya, yb, yc, w_out[l], mod, ln1_g[l][None], ln1_b[l][None], wr, br)
        pairs = lambda a: a.reshape(N_STREAMS, T_STREAM // TT, 8, TT)[:, :, :2, :].transpose(0, 2, 1, 3)
        pos, tiles = _plan(pairs(ids).reshape(N_STREAMS, 2 * T_STREAM // LANES, LANES))
        y = _moe(pos.reshape(-1), pairs(wts).reshape(-1), tiles.reshape(-1), h2g, w_gate[l], w_up[l], w_down[l])
        x = _final(x1, y.reshape(T_ALL * GROUP_ROWS, LANES), mod, ln2_g[l][None], ln2_b[l][None])
        new_bk.append(kb[:T_CTX].reshape(BATCH, SEQ, KV_B, HEAD_DIM))
        new_bv.append(vb[:T_CTX].reshape(BATCH, SEQ, KV_B, HEAD_DIM))
        new_ck.append(kc[:T_CTX].reshape(BATCH, SEQ, H_C, HEAD_DIM))
        new_cv.append(vc[:T_CTX].reshape(BATCH, SEQ, H_C, HEAD_DIM))

    y_prompt = x[:T_CTX].reshape(BATCH, SEQ, D_MODEL)
    y_sample = x[T_CTX:].reshape(DEC_BATCH, DEC_SEQ, D_MODEL)
    return (y_prompt, y_sample, jnp.stack(new_bk, axis=1), jnp.stack(new_bv, axis=1),
            jnp.stack(new_ck, axis=1), jnp.stack(new_cv, axis=1))
```

```python
import functools

import numpy as np
import jax
import jax.numpy as jnp
from jax import lax
from jax.experimental import pallas as pl
from jax.experimental.pallas import tpu as pltpu

F32 = jnp.float32
BF16 = jnp.bfloat16

D_MODEL = 1024
BATCH = 16
SEQ = 256
DEPTH = 2
DEC_BATCH = 4
DEC_SEQ = 1024
PAST_LEN = 256
GRID_W = 64
GRID_ROWS = DEC_SEQ // GRID_W
HEAD_DIM = 64
POOL_WIDTH = 256
POOL_WINDOWS = (2, 4, 8, 16)
H_B = 6
KV_B = 2
H_C = 6
WIN_R = 8
WIN_C = 16
ROPE_THETA = 10000.0
QB_W = H_B * HEAD_DIM
KB_W = KV_B * HEAD_DIM
C_W = H_C * HEAD_DIM
PROJ_WIDTH = 2048
N_GROUPS = 4
EXPERTS_PER_GROUP = 8
N_EXPERTS = 32
D_EXPERT = 256
ALPHA = (2 * DEPTH) ** 0.25
LN_EPS = 1e-6
RMS_EPS = 1e-6
NEG = -1e30
ATTN_SCALE = HEAD_DIM ** -0.5

T_CTX = BATCH * SEQ
T_LAT = DEC_BATCH * DEC_SEQ
T_ALL = T_CTX + T_LAT

LANES = 128
ROUTER_LANES = 128
EXPERT_LANE0 = N_GROUPS
MOD_ROWS = 8
CTX_MOD_ROW = DEC_BATCH

TT = 512
TP = 1024
N_STREAMS = 2
T_STREAM = T_ALL // N_STREAMS
TMX = 128
P_MAX = 2 * T_STREAM + N_EXPERTS * TMX
NT_MAX = P_MAX // TMX
GROUP_ROWS = D_MODEL // LANES
XSTRIDE = TMX + 8
RMW_BATCH = 16
PLAN_TILE_EXPERT, PLAN_N_TILES, PLAN_VALID_END, PLAN_PAD_END = 0, 1, 2, 3
HALF = DEC_SEQ // 2
NAT_KEYS = 12 * GRID_W
VMEM_LIMIT = 56 * 1024 * 1024


def _dot(a, b):
    return jnp.dot(a, b, preferred_element_type=F32)


def _dot_nt(a, b):
    return lax.dot_general(a, b, (((1,), (1,)), ((), ())), preferred_element_type=F32)


def _split_bf16(x):
    hi = x.astype(BF16)
    lo = (x - hi.astype(F32)).astype(BF16)
    return hi, lo


def _layer_norm(y, g, b):
    mu = jnp.mean(y, axis=-1, keepdims=True)
    var = jnp.mean(jnp.square(y - mu), axis=-1, keepdims=True)
    return (y - mu) * lax.rsqrt(var + LN_EPS) * g + b


def _adaln_kernel(c_ref, w_ref, b_ref, o_ref):
    c = c_ref[...]
    s = (c * jax.nn.sigmoid(c)).astype(BF16)
    o_ref[0] = _dot(s, w_ref[0].astype(BF16)) + b_ref[0]


def _adaln(c8, w_ada, b_ada):
    tn = 1536
    n = w_ada.shape[-1]
    return pl.pallas_call(
        _adaln_kernel,
        grid=(DEPTH, n // tn),
        in_specs=[
            pl.BlockSpec((MOD_ROWS, D_MODEL), lambda l, j: (0, 0)),
            pl.BlockSpec((1, D_MODEL, tn), lambda l, j: (l, 0, j)),
            pl.BlockSpec((1, 1, tn), lambda l, j: (l, 0, j)),
        ],
        out_specs=pl.BlockSpec((1, MOD_ROWS, tn), lambda l, j: (l, 0, j)),
        out_shape=jax.ShapeDtypeStruct((DEPTH, MOD_ROWS, n), F32),
        compiler_params=pltpu.CompilerParams(
            dimension_semantics=("arbitrary", "arbitrary"), vmem_limit_bytes=VMEM_LIMIT),
        name="adaln",
    )(c8, w_ada, b_ada.reshape(DEPTH, 1, n))


def _mod_row_of_tile(i, tile):
    n_ctx = T_CTX // tile
    per_req = DEC_SEQ // tile
    return jnp.where(i < n_ctx, CTX_MOD_ROW, (i - n_ctx) // per_req)


def _mod_spec(layer, chunk, tile):
    return pl.BlockSpec(
        (1, 1, D_MODEL), lambda i: ((layer * MOD_ROWS + _mod_row_of_tile(i, tile)) * 6 + chunk, 0, 0))


def _layer_spec(layer, shape):
    return pl.BlockSpec((1,) + shape, lambda *_: (layer,) + (0,) * len(shape))


N_CTX_TILES = T_CTX // TT


def _split_row_specs(width):
    return (pl.BlockSpec((TT, width), lambda i: (jnp.minimum(i, N_CTX_TILES - 1), 0)),
            pl.BlockSpec((TT, width), lambda i: (jnp.maximum(i - N_CTX_TILES, 0), 0)))


def _split_rows(ctx_ref, lat_ref):
    return jnp.where(pl.program_id(0) < N_CTX_TILES, ctx_ref[...], lat_ref[...])


def _rms_norm_heads(x, ones_bd, w):
    hi, lo = _split_bf16(x * x)
    ssq = _dot(hi, ones_bd) + _dot(lo, ones_bd)
    return x * lax.rsqrt(ssq * (1.0 / HEAD_DIM) + RMS_EPS) * w


def _rope_cols(x, cos, sin, first16):
    cols = []
    for j in range(x.shape[1] // LANES):
        xc = x[:, j * LANES:(j + 1) * LANES]
        partner = jnp.where(first16, pltpu.roll(xc, LANES - 16, axis=1), pltpu.roll(xc, 16, axis=1))
        cols.append(xc * cos + partner * sin)
    return jnp.concatenate(cols, axis=1) if len(cols) > 1 else cols[0]


def _proj_kernel(xc_ref, xl_ref, sh_ref, sc_ref, w_ref, ones_ref, qn_ref, kn_ref, cos_ref, sin_ref,
                 a_ref, qb_ref, kb_ref, vb_ref, qc_ref, kc_ref, vc_ref, wbf_ref):
    @pl.when(pl.program_id(0) == 0)
    def _():
        wbf_ref[...] = w_ref[0].astype(BF16)

    h = _split_rows(xc_ref, xl_ref) * (1.0 + sc_ref[0]) + sh_ref[0]
    p = _dot(h.astype(BF16), wbf_ref[...])
    o = 0
    a_ref[...] = p[:, o:o + POOL_WIDTH]; o += POOL_WIDTH
    qb = p[:, o:o + QB_W]; o += QB_W
    kb = p[:, o:o + KB_W]; o += KB_W
    vb_ref[...] = p[:, o:o + KB_W]; o += KB_W
    qc_ref[...] = (p[:, o:o + C_W] * ATTN_SCALE).astype(BF16); o += C_W
    kc_ref[...] = p[:, o:o + C_W]; o += C_W
    vc_ref[...] = p[:, o:o + C_W]

    ones_bd = ones_ref[...]
    cos = cos_ref[...]
    sin = sin_ref[...]
    lane = lax.broadcasted_iota(jnp.int32, (1, LANES), 1)
    first16 = (lane & 31) < 16
    qb = _rope_cols(_rms_norm_heads(qb, ones_bd, qn_ref[...]), cos, sin, first16)
    kb = _rope_cols(_rms_norm_heads(kb, ones_bd[:KB_W, :KB_W], kn_ref[...]), cos, sin, first16)
    qb_ref[...] = (qb * ATTN_SCALE).astype(BF16)
    kb_ref[...] = kb


def _proj(layer, x_ctx, x_lat, mod, w_in, ones_bd, qn, kn, cos_t, sin_t):
    n_tiles = T_ALL // TT
    n_ctx = N_CTX_TILES
    per_req = DEC_SEQ // TT

    def rope_idx(i):
        return (jnp.where(i < n_ctx, per_req, (i - n_ctx) % per_req), 0)

    row = lambda w: pl.BlockSpec((TT, w), lambda i: (i, 0))
    const = lambda s: pl.BlockSpec(s, lambda i: (0,) * len(s))
    widths = (POOL_WIDTH, QB_W, KB_W, KB_W, C_W, C_W, C_W)
    dtypes = (F32, BF16, F32, F32, BF16, F32, F32)
    return pl.pallas_call(
        _proj_kernel,
        grid=(n_tiles,),
        in_specs=[*_split_row_specs(D_MODEL), _mod_spec(layer, 0, TT), _mod_spec(layer, 1, TT),
                  _layer_spec(layer, (D_MODEL, PROJ_WIDTH)),
                  const((QB_W, QB_W)), const((1, QB_W)), const((1, KB_W)),
                  pl.BlockSpec((TT, LANES), rope_idx), pl.BlockSpec((TT, LANES), rope_idx)],
        out_specs=[row(w) for w in widths],
        out_shape=[jax.ShapeDtypeStruct((T_ALL, w), d) for w, d in zip(widths, dtypes)],
        scratch_shapes=[pltpu.VMEM((D_MODEL, PROJ_WIDTH), BF16)],
        compiler_params=pltpu.CompilerParams(dimension_semantics=("arbitrary",), vmem_limit_bytes=VMEM_LIMIT),
        name="proj",
    )(x_ctx, x_lat, mod, mod, w_in, ones_bd, qn, kn, cos_t, sin_t)


def _pool_kernel(a_ref, w_ref, scale_ref, o_ref):
    i = pl.program_id(0)
    a = a_ref[...]
    row = lax.broadcasted_iota(jnp.int32, a.shape, 0)
    lane = lax.broadcasted_iota(jnp.int32, a.shape, 1)
    seq_m1 = jnp.where(i < T_CTX // TP, SEQ - 1, DEC_SEQ - 1)
    t = row & seq_m1

    def shifted(k):
        v = pltpu.roll(a, (TP - k) % TP, axis=0) if k else a
        ok = (t + k >= 0) & (t + k <= seq_m1)
        return jnp.where(ok, v, 0.0)

    s = {k: shifted(k) for k in range(-8, 8)}
    c2 = s[-1] + s[0]
    c4 = c2 + s[-2] + s[1]
    c8 = c4 + s[-4] + s[-3] + s[2] + s[3]
    c16 = c8 + s[-8] + s[-7] + s[-6] + s[-5] + s[4] + s[5] + s[6] + s[7]
    g = lane >> 6
    csum = jnp.where(g == 0, c2, jnp.where(g == 1, c4, jnp.where(g == 2, c8, c16)))
    half = jnp.where(g == 0, 1, jnp.where(g == 1, 2, jnp.where(g == 2, 4, 8)))
    cnt = jnp.minimum(t + half, seq_m1 + 1) - jnp.maximum(t - half, 0)
    pooled = csum / cnt.astype(F32) - a
    y = _dot(pooled.astype(BF16), w_ref[...].astype(BF16)) * scale_ref[...]
    o_ref[...] = y.astype(BF16)


def _pool(a, w_bd, scale):
    return pl.pallas_call(
        _pool_kernel,
        grid=(T_ALL // TP,),
        in_specs=[pl.BlockSpec((TP, POOL_WIDTH), lambda i: (i, 0)),
                  pl.BlockSpec((POOL_WIDTH, POOL_WIDTH), lambda i: (0, 0)),
                  pl.BlockSpec((1, POOL_WIDTH), lambda i: (0, 0))],
        out_specs=pl.BlockSpec((TP, POOL_WIDTH), lambda i: (i, 0)),
        out_shape=jax.ShapeDtypeStruct((T_ALL, POOL_WIDTH), BF16),
        compiler_params=pltpu.CompilerParams(dimension_semantics=("arbitrary",), vmem_limit_bytes=VMEM_LIMIT),
        name="pool",
    )(a, w_bd, scale)


def _softmax_pv(scores, values):
    m = scores[0].max(axis=-1, keepdims=True)
    for s in scores[1:]:
        m = jnp.maximum(m, s.max(axis=-1, keepdims=True))
    ps = [jnp.exp(s - m) for s in scores]
    l = ps[0].sum(axis=-1, keepdims=True)
    for p in ps[1:]:
        l = l + p.sum(axis=-1, keepdims=True)
    r = 1.0 / l
    o = _dot((ps[0] * r).astype(BF16), values[0])
    for p, v in zip(ps[1:], values[1:]):
        o = o + _dot((p * r).astype(BF16), v)
    return o


def _lane_halves():
    lane = lax.broadcasted_iota(jnp.int32, (1, LANES), 1)
    return lane < HEAD_DIM, lane >= HEAD_DIM


def _keep(x, mask):
    return jnp.where(mask, x, 0.0).astype(BF16)


def _gqa_variants(x):
    lo, hi = _lane_halves()
    xs = pltpu.roll(x, HEAD_DIM, axis=1)
    nat_lo, nat_hi = _keep(x, lo), _keep(x, hi)
    sw_lo, sw_hi = _keep(xs, lo), _keep(xs, hi)
    return ((nat_lo, sw_hi), (nat_lo, nat_hi), (sw_lo, nat_hi))


def _mha_variants(x):
    lo, hi = _lane_halves()
    out = []
    for j in range(x.shape[1] // LANES):
        xc = x[:, j * LANES:(j + 1) * LANES]
        out.append((_keep(xc, lo), _keep(xc, hi)))
    return tuple(out)


def _attend_cols(q, k_vars, v_vars, extra_k=None, extra_v=None, bias=None):
    cols = []
    for j in range(q.shape[1] // LANES):
        qc = q[:, j * LANES:(j + 1) * LANES]
        o = None
        for hh in range(2):
            s = _dot_nt(qc, k_vars[j][hh])
            if bias is not None:
                s = s + bias[j][hh]
            scores, values = [s], [v_vars[j][hh]]
            if extra_k is not None:
                scores.append(_dot_nt(qc, extra_k[j][hh]))
                values.append(extra_v[j][hh])
            oh = _softmax_pv(scores, values)
            o = oh if o is None else o + oh
        cols.append(o)
    return jnp.concatenate(cols, axis=1)


def _attn_ctx_kernel(qb_ref, kb_ref, vb_ref, qc_ref, kc_ref, vc_ref, yb_ref, yc_ref):
    yb = _attend_cols(qb_ref[...], _gqa_variants(kb_ref[...]), _gqa_variants(vb_ref[...]))
    yb_ref[...] = yb.astype(BF16)
    yc = _attend_cols(qc_ref[...], _mha_variants(kc_ref[...]), _mha_variants(vc_ref[...]))
    yc_ref[...] = yc.astype(BF16)


def _attn_ctx(qb, kb, vb, qc, kc, vc):
    row = lambda w: pl.BlockSpec((SEQ, w), lambda i: (i, 0))
    return pl.pallas_call(
        _attn_ctx_kernel,
        grid=(BATCH,),
        in_specs=[row(QB_W), row(KB_W), row(KB_W), row(C_W), row(C_W), row(C_W)],
        out_specs=[row(QB_W), row(C_W)],
        out_shape=[jax.ShapeDtypeStruct((T_CTX, QB_W), BF16), jax.ShapeDtypeStruct((T_CTX, C_W), BF16)],
        compiler_params=pltpu.CompilerParams(dimension_semantics=("arbitrary",), vmem_limit_bytes=VMEM_LIMIT),
        name="attn_ctx",
    )(qb, kb, vb, qc, kc, vc)


def _attn_latb_kernel(q_ref, k_ref, v_ref, ck_ref, cv_ref, y_ref):
    y = _attend_cols(q_ref[...], _gqa_variants(ck_ref[0, 0]), _gqa_variants(cv_ref[0, 0]),
                     extra_k=_gqa_variants(k_ref[...]), extra_v=_gqa_variants(v_ref[...]))
    y_ref[...] = y.astype(BF16)


def _attn_latb(layer, qb, kb, vb, cache_k, cache_v):
    ctx_h = T_CTX // HALF
    ctx_r = T_CTX // DEC_SEQ
    cache = pl.BlockSpec((1, 1, PAST_LEN, KB_W), lambda b, s: (b, layer, 0, 0))
    own = pl.BlockSpec((DEC_SEQ, KB_W), lambda b, s: (ctx_r + b, 0))
    return pl.pallas_call(
        _attn_latb_kernel,
        grid=(DEC_BATCH, DEC_SEQ // HALF),
        in_specs=[pl.BlockSpec((HALF, QB_W), lambda b, s: (ctx_h + 2 * b + s, 0)), own, own, cache, cache],
        out_specs=pl.BlockSpec((HALF, QB_W), lambda b, s: (2 * b + s, 0)),
        out_shape=jax.ShapeDtypeStruct((T_LAT, QB_W), BF16),
        compiler_params=pltpu.CompilerParams(
            dimension_semantics=("arbitrary", "arbitrary"), vmem_limit_bytes=VMEM_LIMIT),
        name="attn_latb",
    )(qb, kb, vb, cache_k, cache_v)


def _natten_window(s, i):
    r = (HALF // GRID_W) * s + i
    rs = min(max(r - WIN_R // 2, 0), GRID_ROWS - WIN_R)
    return r, rs


def _natten_kernel(q_ref, k_ref, v_ref, ck_ref, cv_ref, blk_ref, y_ref, bias_ref):
    s = pl.program_id(1)

    for sv in range(2):
        @pl.when((pl.program_id(2) == 0) & (s == sv))
        def _():
            lo_half, _ = _lane_halves()
            masked = jnp.full((GRID_W, LANES), NEG, F32)
            for hh in range(2):
                for i in range(HALF // GRID_W):
                    r, rs = _natten_window(sv, i)
                    for jp in range(NAT_KEYS // LANES):
                        pair = []
                        for j in (2 * jp, 2 * jp + 1):
                            rk = 4 * sv + j
                            pair.append(blk_ref[0, hh, rk - r + WIN_R - 1] if rs <= rk < rs + WIN_R else masked)
                        bias_ref[hh, i * GRID_W:(i + 1) * GRID_W, jp * LANES:(jp + 1) * LANES] = (
                            jnp.where(lo_half, pair[0], pair[1]))

    start = pl.multiple_of(s * (4 * GRID_W), 4 * GRID_W)
    k = _mha_variants(k_ref[pl.ds(start, NAT_KEYS), :])
    v = _mha_variants(v_ref[pl.ds(start, NAT_KEYS), :])
    ck = _mha_variants(ck_ref[0, 0])
    cv = _mha_variants(cv_ref[0, 0])
    bias = ((bias_ref[0], bias_ref[1]),)
    y = _attend_cols(q_ref[...], k, v, extra_k=ck, extra_v=cv, bias=bias)
    y_ref[...] = y.astype(BF16)


def _natten(layer, qc, kc, vc, cache_k, cache_v, blocks):
    ctx_h = T_CTX // HALF
    ctx_r = T_CTX // DEC_SEQ
    cache = pl.BlockSpec((1, 1, PAST_LEN, LANES), lambda j, s, b: (b, layer, 0, j))
    own = pl.BlockSpec((DEC_SEQ, LANES), lambda j, s, b: (ctx_r + b, j))
    return pl.pallas_call(
        _natten_kernel,
        grid=(C_W // LANES, DEC_SEQ // HALF, DEC_BATCH),
        in_specs=[pl.BlockSpec((HALF, LANES), lambda j, s, b: (ctx_h + 2 * b + s, j)), own, own, cache, cache,
                  pl.BlockSpec((1, 2, 2 * WIN_R - 1, GRID_W, LANES), lambda j, s, b: (j, 0, 0, 0, 0))],
        out_specs=pl.BlockSpec((HALF, LANES), lambda j, s, b: (2 * b + s, j)),
        out_shape=jax.ShapeDtypeStruct((T_LAT, C_W), BF16),
        scratch_shapes=[pltpu.VMEM((2, HALF, NAT_KEYS), F32)],
        compiler_params=pltpu.CompilerParams(
            dimension_semantics=("arbitrary", "arbitrary", "arbitrary"), vmem_limit_bytes=VMEM_LIMIT),
        name="natten",
    )(qc, kc, vc, cache_k, cache_v, blocks)


def _natten_blocks(rpb):
    qcol = np.arange(GRID_W)[:, None]
    kcol = np.arange(GRID_W)[None, :]
    cs = np.clip(qcol - WIN_C // 2, 0, GRID_W - WIN_C)
    col_ok = (kcol >= cs) & (kcol < cs + WIN_C)
    dc = np.clip(kcol - qcol, -(WIN_C - 1), WIN_C - 1) + (WIN_C - 1)
    sel_c = (dc[:, :, None] == np.arange(2 * WIN_C - 1)).astype(np.float32)
    blk = jnp.einsum("hdm,qkm->hdqk", rpb, jnp.asarray(sel_c), precision=lax.Precision.HIGHEST)
    blk = jnp.where(jnp.asarray(col_ok), blk, NEG)
    blk = jnp.concatenate([blk, blk], axis=-1)
    return blk.reshape(H_C // 2, 2, 2 * WIN_R - 1, GRID_W, LANES)


def _route(lg):
    lane = lax.broadcasted_iota(jnp.int32, lg.shape, 1).astype(F32)
    low = jnp.float32(-3.0e38)
    far = jnp.float32(ROUTER_LANES)
    is_g = lane < N_GROUPS
    gmax = jnp.where(is_g, lg, low).max(axis=-1, keepdims=True)
    gsel = jnp.where(is_g & (lg == gmax), lane, far).min(axis=-1, keepdims=True)
    pg_sel = 1.0 / jnp.where(is_g, jnp.exp(lg - gmax), 0.0).sum(axis=-1, keepdims=True)
    e0 = EXPERT_LANE0 + EXPERTS_PER_GROUP * gsel
    in_g = (lane >= e0) & (lane < e0 + EXPERTS_PER_GROUP)
    m1 = jnp.where(in_g, lg, low).max(axis=-1, keepdims=True)
    i1 = jnp.where(in_g & (lg == m1), lane, far).min(axis=-1, keepdims=True)
    rest = in_g & (lane != i1)
    m2 = jnp.where(rest, lg, low).max(axis=-1, keepdims=True)
    i2 = jnp.where(rest & (lg == m2), lane, far).min(axis=-1, keepdims=True)
    t = jnp.exp(m2 - m1)
    ssum = 1.0 + t
    w1 = (1.0 / ssum) * pg_sel
    w2 = (t / ssum) * pg_sel
    return i1 - EXPERT_LANE0, i2 - EXPERT_LANE0, w1, w2


def _rows_to_groups(x, stage_ref, out_ref, n_rows):
    stride = n_rows + 8
    for c in range(GROUP_ROWS):
        stage_ref[c * stride:c * stride + n_rows, :] = x[:, c * LANES:(c + 1) * LANES]
    for j in range(n_rows):
        out_ref[GROUP_ROWS * j:GROUP_ROWS * (j + 1), :] = stage_ref[pl.ds(j, GROUP_ROWS, stride=stride), :]


def _tail_kernel(xc_ref, xl_ref, ya_ref, ybc_ref, ybl_ref, ycc_ref, ycl_ref, w_ref, g1_ref, sh2_ref, sc2_ref,
                 lg_ref, lb_ref, wr_ref, br_ref, x1_ref, h2g_ref, ids_ref, wts_ref, wbf_ref, stage_ref):
    @pl.when(pl.program_id(0) == 0)
    def _():
        wbf_ref[...] = w_ref[0].astype(BF16)

    o = (_dot(ya_ref[...], wbf_ref[0:POOL_WIDTH, :])
         + _dot(_split_rows(ybc_ref, ybl_ref), wbf_ref[POOL_WIDTH:POOL_WIDTH + QB_W, :])
         + _dot(_split_rows(ycc_ref, ycl_ref), wbf_ref[POOL_WIDTH + QB_W:, :]))
    x1 = _layer_norm(ALPHA * _split_rows(xc_ref, xl_ref) + g1_ref[0] * o, lg_ref[0], lb_ref[0])
    x1_ref[...] = x1
    h2 = x1 * (1.0 + sc2_ref[0]) + sh2_ref[0]
    hh, hl = _split_bf16(h2)
    wh, wl = _split_bf16(wr_ref[0])
    lg = _dot(hh, wh) + _dot(hl, wh) + _dot(hh, wl) + br_ref[0]
    i1, i2, w1, w2 = _route(lg)
    lane = lax.broadcasted_iota(jnp.int32, lg.shape, 1)
    ids = jnp.where(lane == 0, i1, jnp.where(lane == 1, i2, 0.0))
    ids_ref[0] = ids.T[0:8, :].astype(jnp.int32)
    wts = jnp.where(lane == 0, w1, jnp.where(lane == 1, w2, 0.0))
    wts_ref[0] = wts.T[0:8, :]
    _rows_to_groups(h2, stage_ref, h2g_ref, TT)


def _tail(layer, x_ctx, x_lat, ya, yb_ctx, yb_lat, yc_ctx, yc_lat, w_out, mod, ln_g, ln_b, wr, br):
    row = lambda w: pl.BlockSpec((TT, w), lambda i: (i, 0))
    return pl.pallas_call(
        _tail_kernel,
        grid=(T_ALL // TT,),
        in_specs=[*_split_row_specs(D_MODEL), row(POOL_WIDTH), *_split_row_specs(QB_W), *_split_row_specs(C_W),
                  _layer_spec(layer, (D_MODEL, D_MODEL)),
                  _mod_spec(layer, 2, TT), _mod_spec(layer, 3, TT), _mod_spec(layer, 4, TT),
                  _layer_spec(layer, (1, D_MODEL)), _layer_spec(layer, (1, D_MODEL)),
                  _layer_spec(layer, (D_MODEL, ROUTER_LANES)), _layer_spec(layer, (1, ROUTER_LANES))],
        out_specs=[row(D_MODEL), pl.BlockSpec((TT * GROUP_ROWS, LANES), lambda i: (i, 0)),
                   pl.BlockSpec((1, 8, TT), lambda i: (i, 0, 0)), pl.BlockSpec((1, 8, TT), lambda i: (i, 0, 0))],
        out_shape=[jax.ShapeDtypeStruct((T_ALL, D_MODEL), F32),
                   jax.ShapeDtypeStruct((T_ALL * GROUP_ROWS, LANES), F32),
                   jax.ShapeDtypeStruct((T_ALL // TT, 8, TT), jnp.int32),
                   jax.ShapeDtypeStruct((T_ALL // TT, 8, TT), F32)],
        scratch_shapes=[pltpu.VMEM((D_MODEL, D_MODEL), BF16),
                        pltpu.VMEM((GROUP_ROWS * (TT + 8), LANES), F32)],
        compiler_params=pltpu.CompilerParams(dimension_semantics=("arbitrary",), vmem_limit_bytes=VMEM_LIMIT),
        name="tail",
    )(x_ctx, x_lat, ya, yb_ctx, yb_lat, yc_ctx, yc_lat, w_out, mod, mod, mod, ln_g, ln_b, wr, br)


def _plan_kernel(eid_ref, pos_ref, tile_ref):
    e = eid_ref[0]
    r = lax.broadcasted_iota(jnp.int32, (LANES, LANES), 0)
    c = lax.broadcasted_iota(jnp.int32, (LANES, LANES), 1)
    upper = (r <= c).astype(BF16)
    rows = e.shape[0]
    lower = (lax.broadcasted_iota(jnp.int32, (rows, rows), 1)
             < lax.broadcasted_iota(jnp.int32, (rows, rows), 0)).astype(BF16)
    tile_start = (lax.broadcasted_iota(jnp.int32, (1, LANES), 1) * TMX).astype(F32)
    lane = lax.broadcasted_iota(jnp.int32, (1, LANES), 1)
    pos = jnp.zeros(e.shape, F32)
    base = jnp.zeros((1, LANES), F32)
    n_before = jnp.zeros((1, LANES), F32)
    valid_end = jnp.zeros((1, LANES), F32)
    pad_end = jnp.zeros((1, LANES), F32)
    for ex in range(N_EXPERTS):
        m = e == ex
        incl = _dot(m.astype(BF16), upper)
        row_tot = jnp.broadcast_to(incl[:, LANES - 1:LANES], incl.shape)
        row_off = _dot(lower, row_tot.astype(BF16))
        cnt = jnp.sum(row_tot, axis=0, keepdims=True)
        pos = jnp.where(m, base + row_off + incl - 1.0, pos)
        valid_end = jnp.where(lane == ex, base + cnt, valid_end)
        base = base + jnp.ceil(cnt * (1.0 / TMX)) * TMX
        pad_end = jnp.where(lane == ex, base, pad_end)
        n_before = n_before + (tile_start >= base).astype(F32)
    pos_ref[0] = pos.astype(jnp.int32)
    sub = lax.broadcasted_iota(jnp.int32, (8, LANES), 0)
    rows8 = lambda v: jnp.broadcast_to(v, (8, LANES))
    table = jnp.where(sub == PLAN_TILE_EXPERT, rows8(jnp.minimum(n_before, N_EXPERTS - 1.0)),
                      jnp.where(sub == PLAN_N_TILES, rows8(base * (1.0 / TMX)),
                                jnp.where(sub == PLAN_VALID_END, rows8(valid_end),
                                          jnp.where(sub == PLAN_PAD_END, rows8(pad_end), 0.0))))
    tile_ref[0] = table.astype(jnp.int32)


def _plan(eid):
    rows = 2 * T_STREAM // LANES
    return pl.pallas_call(
        _plan_kernel,
        grid=(N_STREAMS,),
        in_specs=[pl.BlockSpec((1, rows, LANES), lambda s: (s, 0, 0))],
        out_specs=[pl.BlockSpec((1, rows, LANES), lambda s: (s, 0, 0)), pl.BlockSpec((1, 8, LANES), lambda s: (s, 0, 0))],
        out_shape=[jax.ShapeDtypeStruct((N_STREAMS, rows, LANES), jnp.int32),
                   jax.ShapeDtypeStruct((N_STREAMS, 8, LANES), jnp.int32)],
        compiler_params=pltpu.CompilerParams(dimension_semantics=("arbitrary",), vmem_limit_bytes=VMEM_LIMIT),
        name="plan",
    )(eid)


def _moe_kernel(pos_ref, wpair_ref, plan_ref, h2g_hbm, wg_ref, wu_ref, wd_ref, y_hbm,
                h2v, yv, off, wcol, xg0, xg1, yg0, yg1, wgb, wub, wdb, sem):
    s = pl.program_id(0)
    e = pl.program_id(1)
    n_pairs = 2 * T_STREAM
    spare_off = T_STREAM * GROUP_ROWS
    plan = lambda row, idx: plan_ref[(s * 8 + row) * LANES + idx]
    n_tiles = plan(PLAN_N_TILES, 0)
    last = n_tiles - 1

    xgs = (xg0, xg1)
    ygs = (yg0, yg1)

    def gather(tile, dst):
        base = tile * TMX
        for j in range(TMX):
            src = pl.multiple_of(off[base + j], GROUP_ROWS)
            dst[pl.ds(j, GROUP_ROWS, stride=XSTRIDE), :] = h2v[pl.ds(src, GROUP_ROWS), :]

    def h2_copy():
        return pltpu.make_async_copy(
            h2g_hbm.at[pl.ds(pl.multiple_of(s * (T_STREAM * GROUP_ROWS), 8), T_STREAM * GROUP_ROWS)],
            h2v.at[pl.ds(0, T_STREAM * GROUP_ROWS)], sem.at[0])

    def y_copy():
        return pltpu.make_async_copy(yv.at[pl.ds(0, T_STREAM * GROUP_ROWS)], y_hbm.at[s], sem.at[1])

    @pl.when(e == 0)
    def _():
        h2_copy().start()
        yv[...] = jnp.zeros_like(yv)
        h2v[spare_off:spare_off + GROUP_ROWS, :] = jnp.zeros((GROUP_ROWS, LANES), F32)

        def pad_expert(ex, carry):
            def pad_row(r, c):
                off[r] = spare_off
                wcol[pl.ds(r, 1), :] = jnp.zeros((1, LANES), F32)
                return c
            return lax.fori_loop(plan(PLAN_VALID_END, ex), plan(PLAN_PAD_END, ex), pad_row, carry)
        lax.fori_loop(0, N_EXPERTS, pad_expert, 0)

        def place(i, carry):
            for u in range(8):
                tok = i * 8 + u
                for slot in range(2):
                    p = s * n_pairs + slot * T_STREAM + tok
                    r = pos_ref[p]
                    off[r] = tok * GROUP_ROWS
                    wcol[pl.ds(r, 1), :] = jnp.full((1, LANES), wpair_ref[p], F32)
            return carry
        lax.fori_loop(0, T_STREAM // 8, place, 0)
        yg1[...] = jnp.zeros_like(yg1)
        h2_copy().wait()
        gather(0, xg0)

    t_lo = jnp.where(e == 0, 0, plan(PLAN_PAD_END, jnp.maximum(e - 1, 0))) // TMX
    t_hi = plan(PLAN_PAD_END, e) // TMX

    @pl.when(t_hi > t_lo)
    def _():
        wgb[...] = wg_ref[0, 0].astype(BF16)
        wub[...] = wu_ref[0, 0].astype(BF16)
        wdb[...] = wd_ref[0, 0].astype(BF16)

    def add_tile(tile, done):
        base = tile * TMX
        for j0 in range(0, TMX, RMW_BATCH):
            updates = []
            for j in range(j0, j0 + RMW_BATCH):
                dst = pl.multiple_of(off[base + j], GROUP_ROWS)
                updates.append((dst, yv[pl.ds(dst, GROUP_ROWS), :] + done[pl.ds(j, GROUP_ROWS, stride=XSTRIDE), :]))
            for dst, val in updates:
                yv[pl.ds(dst, GROUP_ROWS), :] = val

    def step(t, cur, nxt, out, done):
        gather(jnp.minimum(t + 1, last), nxt)
        x = jnp.concatenate([cur[c * XSTRIDE:c * XSTRIDE + TMX, :] for c in range(GROUP_ROWS)], axis=1).astype(BF16)
        hg = _dot(x, wgb[...])
        hu = _dot(x, wub[...])
        gate = wcol[pl.ds(pl.multiple_of(t * TMX, TMX), TMX), :]
        act = hg * jax.nn.sigmoid(hg) * hu * jnp.concatenate([gate] * (D_EXPERT // LANES), axis=1)
        y = _dot(act.astype(BF16), wdb[...])
        for c in range(GROUP_ROWS):
            out[c * XSTRIDE:c * XSTRIDE + TMX, :] = y[:, c * LANES:(c + 1) * LANES]
        add_tile(jnp.maximum(t - 1, 0), done)

    def tile_body(t, carry):
        for par in range(2):
            @pl.when((t & 1) == par)
            def _():
                step(t, xgs[par], xgs[1 - par], ygs[par], ygs[1 - par])
        return carry
    lax.fori_loop(t_lo, t_hi, tile_body, 0)

    @pl.when(e == N_EXPERTS - 1)
    def _():
        for par in range(2):
            @pl.when((last & 1) == par)
            def _():
                add_tile(last, ygs[par])
        y_copy().start()
        y_copy().wait()


def _moe(layer, pos, wpair, plan, h2g, w_gate, w_up, w_down):
    wspec = lambda shape: pl.BlockSpec((1,) + shape, lambda s, e, pos, wpair, plan: (layer, e, 0, 0))
    return pl.pallas_call(
        _moe_kernel,
        grid_spec=pltpu.PrefetchScalarGridSpec(
            num_scalar_prefetch=3,
            grid=(N_STREAMS, N_EXPERTS),
            in_specs=[pl.BlockSpec(memory_space=pl.ANY),
                      wspec((1, D_MODEL, D_EXPERT)), wspec((1, D_MODEL, D_EXPERT)), wspec((1, D_EXPERT, D_MODEL))],
            out_specs=pl.BlockSpec(memory_space=pl.ANY),
            scratch_shapes=[pltpu.VMEM(((T_STREAM + 1) * GROUP_ROWS, LANES), F32),
                            pltpu.VMEM(((T_STREAM + 1) * GROUP_ROWS, LANES), F32),
                            pltpu.SMEM((P_MAX,), jnp.int32),
                            pltpu.VMEM((P_MAX, LANES), F32),
                            pltpu.VMEM((GROUP_ROWS * XSTRIDE, LANES), F32),
                            pltpu.VMEM((GROUP_ROWS * XSTRIDE, LANES), F32),
                            pltpu.VMEM((GROUP_ROWS * XSTRIDE, LANES), F32),
                            pltpu.VMEM((GROUP_ROWS * XSTRIDE, LANES), F32),
                            pltpu.VMEM((D_MODEL, D_EXPERT), BF16), pltpu.VMEM((D_MODEL, D_EXPERT), BF16),
                            pltpu.VMEM((D_EXPERT, D_MODEL), BF16),
                            pltpu.SemaphoreType.DMA((2,))]),
        out_shape=jax.ShapeDtypeStruct((N_STREAMS, T_STREAM * GROUP_ROWS, LANES), F32),
        compiler_params=pltpu.CompilerParams(
            dimension_semantics=("arbitrary", "arbitrary"), vmem_limit_bytes=VMEM_LIMIT),
        name="moe",
    )(pos, wpair, plan, h2g, w_gate, w_up, w_down)


def _groups_to_rows(src_ref, stage_ref, n_rows):
    stride = n_rows + 8
    for j in range(n_rows):
        stage_ref[pl.ds(j, GROUP_ROWS, stride=stride), :] = src_ref[GROUP_ROWS * j:GROUP_ROWS * (j + 1), :]
    return jnp.concatenate([stage_ref[c * stride:c * stride + n_rows, :] for c in range(GROUP_ROWS)], axis=1)


def _final_kernel(x1_ref, y_ref, g2_ref, lg_ref, lb_ref, o_ref, stage_ref):
    moe = _groups_to_rows(y_ref, stage_ref, TT)
    o_ref[...] = _layer_norm(ALPHA * x1_ref[...] + g2_ref[0] * moe, lg_ref[0], lb_ref[0])


def _final(layer, stream, x1, y, mod, ln_g, ln_b):
    first = stream * (T_STREAM // TT)
    g2 = pl.BlockSpec(
        (1, 1, D_MODEL), lambda i: ((layer * MOD_ROWS + _mod_row_of_tile(first + i, TT)) * 6 + 5, 0, 0))
    return pl.pallas_call(
        _final_kernel,
        grid=(T_STREAM // TT,),
        in_specs=[pl.BlockSpec((TT, D_MODEL), lambda i: (first + i, 0)),
                  pl.BlockSpec((TT * GROUP_ROWS, LANES), lambda i: (first + i, 0)),
                  g2, _layer_spec(layer, (1, D_MODEL)), _layer_spec(layer, (1, D_MODEL))],
        out_specs=pl.BlockSpec((TT, D_MODEL), lambda i: (i, 0)),
        out_shape=jax.ShapeDtypeStruct((T_STREAM, D_MODEL), F32),
        scratch_shapes=[pltpu.VMEM((GROUP_ROWS * (TT + 8), LANES), F32)],
        compiler_params=pltpu.CompilerParams(dimension_semantics=("arbitrary",), vmem_limit_bytes=VMEM_LIMIT),
        name="final",
    )(x1, y, mod, ln_g, ln_b)


def _rope_tables():
    t = np.arange(DEC_SEQ)
    pos = np.stack([t // GRID_W, t % GRID_W], axis=1).astype(np.float32)
    nf = HEAD_DIM // 4
    inv = jnp.asarray(ROPE_THETA, F32) ** (-jnp.arange(nf, dtype=F32) / nf)
    d = np.arange(LANES) % HEAD_DIM
    which = d // (HEAD_DIM // 2)
    ang = jnp.asarray(pos)[:, which] * inv[d % nf][None, :]
    sign = np.where((d % 32) < 16, -1.0, 1.0).astype(np.float32)
    cos = jnp.concatenate([jnp.cos(ang), jnp.ones((TT, LANES), F32)], axis=0)
    sin = jnp.concatenate([jnp.sin(ang) * sign[None, :], jnp.zeros((TT, LANES), F32)], axis=0)
    return cos, sin


def _block_ones():
    h = np.arange(QB_W) // HEAD_DIM
    return jnp.asarray((h[:, None] == h[None, :]).astype(np.float32), dtype=BF16)


def _pool_block_diag(pool_w):
    out = jnp.zeros((POOL_WIDTH, POOL_WIDTH), F32)
    for g in range(4):
        out = out.at[64 * g:64 * g + 64, 64 * g:64 * g + 64].set(pool_w[g])
    return out


def kernel(x_prompt, x_sample, cache_b_k, cache_b_v, cache_c_k, cache_c_v, c, c_ctx, w_ada, b_ada, w_in, w_out,
           pool_w, pool_scale, q_norm, k_norm, rpb, ln1_g, ln1_b, ln2_g, ln2_b, router_g, router_g_b, router_e,
           router_e_b, w_gate, w_up, w_down):
    x_ctx = x_prompt.reshape(T_CTX, D_MODEL)
    x_lat = x_sample.reshape(T_LAT, D_MODEL)
    c8 = jnp.concatenate([c, c_ctx[None], jnp.zeros((MOD_ROWS - DEC_BATCH - 1, D_MODEL), F32)], axis=0)
    mod = _adaln(c8, w_ada, b_ada).reshape(DEPTH * MOD_ROWS * 6, 1, D_MODEL)

    cos_t, sin_t = _rope_tables()
    ones_bd = _block_ones()
    cbk = cache_b_k.reshape(DEC_BATCH, DEPTH, PAST_LEN, KB_W)
    cbv = cache_b_v.reshape(DEC_BATCH, DEPTH, PAST_LEN, KB_W)
    cck = cache_c_k.reshape(DEC_BATCH, DEPTH, PAST_LEN, C_W)
    ccv = cache_c_v.reshape(DEC_BATCH, DEPTH, PAST_LEN, C_W)
    pad = jnp.zeros((DEPTH, D_MODEL, ROUTER_LANES - N_GROUPS - N_EXPERTS), F32)
    wr = jnp.concatenate([router_g, router_e, pad], axis=2)
    br = jnp.concatenate([router_g_b, router_e_b, pad[:, 0]], axis=1)[:, None]
    per_layer = lambda p: p[:, None]

    new_bk, new_bv, new_ck, new_cv = [], [], [], []
    for l in range(DEPTH):
        qn = jnp.tile(q_norm[l], H_B)[None]
        kn = jnp.tile(k_norm[l], KV_B)[None]
        a, qb, kb, vb, qc, kc, vc = _proj(l, x_ctx, x_lat, mod, w_in, ones_bd, qn, kn, cos_t, sin_t)
        ya = _pool(a, _pool_block_diag(pool_w[l]), pool_scale[l][None])
        yb_ctx, yc_ctx = _attn_ctx(qb, kb, vb, qc, kc, vc)
        yb_lat = _attn_latb(l, qb, kb, vb, cbk, cbv)
        yc_lat = _natten(l, qc, kc, vc, cck, ccv, _natten_blocks(rpb[l]))
        x1, h2g, ids, wts = _tail(l, x_ctx, x_lat, ya, yb_ctx, yb_lat, yc_ctx, yc_lat, w_out, mod,
                                  per_layer(ln1_g), per_layer(ln1_b), wr, br)
        pairs = lambda a: a.reshape(N_STREAMS, T_STREAM // TT, 8, TT)[:, :, :2, :].transpose(0, 2, 1, 3)
        pos, tiles = _plan(pairs(ids).reshape(N_STREAMS, 2 * T_STREAM // LANES, LANES))
        y = _moe(l, pos.reshape(-1), pairs(wts).reshape(-1), tiles.reshape(-1), h2g, w_gate, w_up, w_down)
        y = y.reshape(T_ALL * GROUP_ROWS, LANES)
        x_ctx = _final(l, 0, x1, y, mod, per_layer(ln2_g), per_layer(ln2_b))
        x_lat = _final(l, 1, x1, y, mod, per_layer(ln2_g), per_layer(ln2_b))
        new_bk.append(kb[:T_CTX].reshape(BATCH, SEQ, KV_B, HEAD_DIM))
        new_bv.append(vb[:T_CTX].reshape(BATCH, SEQ, KV_B, HEAD_DIM))
        new_ck.append(kc[:T_CTX].reshape(BATCH, SEQ, H_C, HEAD_DIM))
        new_cv.append(vc[:T_CTX].reshape(BATCH, SEQ, H_C, HEAD_DIM))

    y_prompt = x_ctx.reshape(BATCH, SEQ, D_MODEL)
    y_sample = x_lat.reshape(DEC_BATCH, DEC_SEQ, D_MODEL)
    return (y_prompt, y_sample, jnp.stack(new_bk, axis=1), jnp.stack(new_bv, axis=1),
            jnp.stack(new_ck, axis=1), jnp.stack(new_cv, axis=1))
```

```python
import functools

import numpy as np
import jax
import jax.numpy as jnp
from jax import lax
from jax.experimental import pallas as pl
from jax.experimental.pallas import tpu as pltpu

F32 = jnp.float32
BF16 = jnp.bfloat16

D_MODEL = 1024
BATCH = 16
SEQ = 256
DEPTH = 2
DEC_BATCH = 4
DEC_SEQ = 1024
PAST_LEN = 256
GRID_W = 64
GRID_ROWS = DEC_SEQ // GRID_W
HEAD_DIM = 64
POOL_WIDTH = 256
POOL_WINDOWS = (2, 4, 8, 16)
H_B = 6
KV_B = 2
H_C = 6
WIN_R = 8
WIN_C = 16
ROPE_THETA = 10000.0
QB_W = H_B * HEAD_DIM
KB_W = KV_B * HEAD_DIM
C_W = H_C * HEAD_DIM
PROJ_WIDTH = 2048
N_GROUPS = 4
EXPERTS_PER_GROUP = 8
N_EXPERTS = 32
D_EXPERT = 256
ALPHA = (2 * DEPTH) ** 0.25
LN_EPS = 1e-6
RMS_EPS = 1e-6
NEG = -1e30
ATTN_SCALE = HEAD_DIM ** -0.5

T_CTX = BATCH * SEQ
T_LAT = DEC_BATCH * DEC_SEQ
T_ALL = T_CTX + T_LAT

LANES = 128
ROUTER_LANES = 128
EXPERT_LANE0 = N_GROUPS
MOD_ROWS = 8
CTX_MOD_ROW = DEC_BATCH

TT = 512
TP = 1024
N_STREAMS = 2
T_STREAM = T_ALL // N_STREAMS
TMX = 256
PLACE_UNROLL = 8
P_MAX = 2 * T_STREAM + N_EXPERTS * TMX
NT_MAX = P_MAX // TMX
GROUP_ROWS = D_MODEL // LANES
XSTRIDE = TMX + 8
RMW_BATCH = 16
PLAN_TILE_EXPERT, PLAN_N_TILES, PLAN_VALID_END, PLAN_PAD_END = 0, 1, 2, 3
HALF = DEC_SEQ // 2
NAT_KEYS = 12 * GRID_W
VMEM_LIMIT = 56 * 1024 * 1024


def _dot(a, b):
    return jnp.dot(a, b, preferred_element_type=F32)


def _dot_nt(a, b):
    return lax.dot_general(a, b, (((1,), (1,)), ((), ())), preferred_element_type=F32)


def _split_bf16(x):
    hi = x.astype(BF16)
    lo = (x - hi.astype(F32)).astype(BF16)
    return hi, lo


def _layer_norm(y, g, b):
    mu = jnp.mean(y, axis=-1, keepdims=True)
    var = jnp.mean(jnp.square(y - mu), axis=-1, keepdims=True)
    return (y - mu) * lax.rsqrt(var + LN_EPS) * g + b


def _adaln_kernel(c_ref, w_ref, b_ref, o_ref):
    c = c_ref[...]
    s = (c * jax.nn.sigmoid(c)).astype(BF16)
    o_ref[0] = _dot(s, w_ref[0].astype(BF16)) + b_ref[0]


def _adaln(c8, w_ada, b_ada):
    tn = 1536
    n = w_ada.shape[-1]
    return pl.pallas_call(
        _adaln_kernel,
        grid=(DEPTH, n // tn),
        in_specs=[
            pl.BlockSpec((MOD_ROWS, D_MODEL), lambda l, j: (0, 0)),
            pl.BlockSpec((1, D_MODEL, tn), lambda l, j: (l, 0, j)),
            pl.BlockSpec((1, 1, tn), lambda l, j: (l, 0, j)),
        ],
        out_specs=pl.BlockSpec((1, MOD_ROWS, tn), lambda l, j: (l, 0, j)),
        out_shape=jax.ShapeDtypeStruct((DEPTH, MOD_ROWS, n), F32),
        compiler_params=pltpu.CompilerParams(
            dimension_semantics=("arbitrary", "arbitrary"), vmem_limit_bytes=VMEM_LIMIT),
        name="adaln",
    )(c8, w_ada, b_ada.reshape(DEPTH, 1, n))


def _mod_row_of_tile(i, tile):
    n_ctx = T_CTX // tile
    per_req = DEC_SEQ // tile
    return jnp.where(i < n_ctx, CTX_MOD_ROW, (i - n_ctx) // per_req)


def _mod_spec(layer, chunk, tile):
    return pl.BlockSpec(
        (1, 1, D_MODEL), lambda i: ((layer * MOD_ROWS + _mod_row_of_tile(i, tile)) * 6 + chunk, 0, 0))


def _layer_spec(layer, shape):
    return pl.BlockSpec((1,) + shape, lambda *_: (layer,) + (0,) * len(shape))


N_CTX_TILES = T_CTX // TT


def _split_row_specs(width):
    return (pl.BlockSpec((TT, width), lambda i: (jnp.minimum(i, N_CTX_TILES - 1), 0)),
            pl.BlockSpec((TT, width), lambda i: (jnp.maximum(i - N_CTX_TILES, 0), 0)))


def _split_rows(ctx_ref, lat_ref):
    return jnp.where(pl.program_id(0) < N_CTX_TILES, ctx_ref[...], lat_ref[...])


def _rms_norm_heads(x, ones_bd, w):
    hi, lo = _split_bf16(x * x)
    ssq = _dot(hi, ones_bd) + _dot(lo, ones_bd)
    return x * lax.rsqrt(ssq * (1.0 / HEAD_DIM) + RMS_EPS) * w


def _rope_cols(x, cos, sin, first16):
    cols = []
    for j in range(x.shape[1] // LANES):
        xc = x[:, j * LANES:(j + 1) * LANES]
        partner = jnp.where(first16, pltpu.roll(xc, LANES - 16, axis=1), pltpu.roll(xc, 16, axis=1))
        cols.append(xc * cos + partner * sin)
    return jnp.concatenate(cols, axis=1) if len(cols) > 1 else cols[0]


def _proj_kernel(xc_ref, xl_ref, sh_ref, sc_ref, w_ref, ones_ref, qn_ref, kn_ref, cos_ref, sin_ref,
                 a_ref, qb_ref, kb_ref, vb_ref, qc_ref, kc_ref, vc_ref, wbf_ref):
    @pl.when(pl.program_id(0) == 0)
    def _():
        wbf_ref[...] = w_ref[0].astype(BF16)

    h = _split_rows(xc_ref, xl_ref) * (1.0 + sc_ref[0]) + sh_ref[0]
    p = _dot(h.astype(BF16), wbf_ref[...])
    o = 0
    a_ref[...] = p[:, o:o + POOL_WIDTH]; o += POOL_WIDTH
    qb = p[:, o:o + QB_W]; o += QB_W
    kb = p[:, o:o + KB_W]; o += KB_W
    vb_ref[...] = p[:, o:o + KB_W]; o += KB_W
    qc_ref[...] = (p[:, o:o + C_W] * ATTN_SCALE).astype(BF16); o += C_W
    kc_ref[...] = p[:, o:o + C_W]; o += C_W
    vc_ref[...] = p[:, o:o + C_W]

    ones_bd = ones_ref[...]
    cos = cos_ref[...]
    sin = sin_ref[...]
    lane = lax.broadcasted_iota(jnp.int32, (1, LANES), 1)
    first16 = (lane & 31) < 16
    qb = _rope_cols(_rms_norm_heads(qb, ones_bd, qn_ref[...]), cos, sin, first16)
    kb = _rope_cols(_rms_norm_heads(kb, ones_bd[:KB_W, :KB_W], kn_ref[...]), cos, sin, first16)
    qb_ref[...] = (qb * ATTN_SCALE).astype(BF16)
    kb_ref[...] = kb


def _proj(layer, x_ctx, x_lat, mod, w_in, ones_bd, qn, kn, cos_t, sin_t):
    n_tiles = T_ALL // TT
    n_ctx = N_CTX_TILES
    per_req = DEC_SEQ // TT

    def rope_idx(i):
        return (jnp.where(i < n_ctx, per_req, (i - n_ctx) % per_req), 0)

    row = lambda w: pl.BlockSpec((TT, w), lambda i: (i, 0))
    const = lambda s: pl.BlockSpec(s, lambda i: (0,) * len(s))
    widths = (POOL_WIDTH, QB_W, KB_W, KB_W, C_W, C_W, C_W)
    dtypes = (F32, BF16, F32, F32, BF16, F32, F32)
    return pl.pallas_call(
        _proj_kernel,
        grid=(n_tiles,),
        in_specs=[*_split_row_specs(D_MODEL), _mod_spec(layer, 0, TT), _mod_spec(layer, 1, TT),
                  _layer_spec(layer, (D_MODEL, PROJ_WIDTH)),
                  const((QB_W, QB_W)), const((1, QB_W)), const((1, KB_W)),
                  pl.BlockSpec((TT, LANES), rope_idx), pl.BlockSpec((TT, LANES), rope_idx)],
        out_specs=[row(w) for w in widths],
        out_shape=[jax.ShapeDtypeStruct((T_ALL, w), d) for w, d in zip(widths, dtypes)],
        scratch_shapes=[pltpu.VMEM((D_MODEL, PROJ_WIDTH), BF16)],
        compiler_params=pltpu.CompilerParams(dimension_semantics=("arbitrary",), vmem_limit_bytes=VMEM_LIMIT),
        name="proj",
    )(x_ctx, x_lat, mod, mod, w_in, ones_bd, qn, kn, cos_t, sin_t)


def _pool_kernel(a_ref, w_ref, scale_ref, o_ref):
    i = pl.program_id(0)
    a = a_ref[...]
    row = lax.broadcasted_iota(jnp.int32, a.shape, 0)
    lane = lax.broadcasted_iota(jnp.int32, a.shape, 1)
    seq_m1 = jnp.where(i < T_CTX // TP, SEQ - 1, DEC_SEQ - 1)
    t = row & seq_m1

    def shifted(k):
        v = pltpu.roll(a, (TP - k) % TP, axis=0) if k else a
        ok = (t + k >= 0) & (t + k <= seq_m1)
        return jnp.where(ok, v, 0.0)

    s = {k: shifted(k) for k in range(-8, 8)}
    c2 = s[-1] + s[0]
    c4 = c2 + s[-2] + s[1]
    c8 = c4 + s[-4] + s[-3] + s[2] + s[3]
    c16 = c8 + s[-8] + s[-7] + s[-6] + s[-5] + s[4] + s[5] + s[6] + s[7]
    g = lane >> 6
    csum = jnp.where(g == 0, c2, jnp.where(g == 1, c4, jnp.where(g == 2, c8, c16)))
    half = jnp.where(g == 0, 1, jnp.where(g == 1, 2, jnp.where(g == 2, 4, 8)))
    cnt = jnp.minimum(t + half, seq_m1 + 1) - jnp.maximum(t - half, 0)
    pooled = csum / cnt.astype(F32) - a
    y = _dot(pooled.astype(BF16), w_ref[...].astype(BF16)) * scale_ref[...]
    o_ref[...] = y.astype(BF16)


def _pool(a, w_bd, scale):
    return pl.pallas_call(
        _pool_kernel,
        grid=(T_ALL // TP,),
        in_specs=[pl.BlockSpec((TP, POOL_WIDTH), lambda i: (i, 0)),
                  pl.BlockSpec((POOL_WIDTH, POOL_WIDTH), lambda i: (0, 0)),
                  pl.BlockSpec((1, POOL_WIDTH), lambda i: (0, 0))],
        out_specs=pl.BlockSpec((TP, POOL_WIDTH), lambda i: (i, 0)),
        out_shape=jax.ShapeDtypeStruct((T_ALL, POOL_WIDTH), BF16),
        compiler_params=pltpu.CompilerParams(dimension_semantics=("arbitrary",), vmem_limit_bytes=VMEM_LIMIT),
        name="pool",
    )(a, w_bd, scale)


def _softmax_pv(scores, values):
    m = scores[0].max(axis=-1, keepdims=True)
    for s in scores[1:]:
        m = jnp.maximum(m, s.max(axis=-1, keepdims=True))
    ps = [jnp.exp(s - m) for s in scores]
    l = ps[0].sum(axis=-1, keepdims=True)
    for p in ps[1:]:
        l = l + p.sum(axis=-1, keepdims=True)
    r = 1.0 / l
    o = _dot((ps[0] * r).astype(BF16), values[0])
    for p, v in zip(ps[1:], values[1:]):
        o = o + _dot((p * r).astype(BF16), v)
    return o


def _lane_halves():
    lane = lax.broadcasted_iota(jnp.int32, (1, LANES), 1)
    return lane < HEAD_DIM, lane >= HEAD_DIM


def _keep(x, mask):
    return jnp.where(mask, x, 0.0).astype(BF16)


def _gqa_variants(x):
    lo, hi = _lane_halves()
    xs = pltpu.roll(x, HEAD_DIM, axis=1)
    nat_lo, nat_hi = _keep(x, lo), _keep(x, hi)
    sw_lo, sw_hi = _keep(xs, lo), _keep(xs, hi)
    return ((nat_lo, sw_hi), (nat_lo, nat_hi), (sw_lo, nat_hi))


def _mha_variants(x):
    lo, hi = _lane_halves()
    out = []
    for j in range(x.shape[1] // LANES):
        xc = x[:, j * LANES:(j + 1) * LANES]
        out.append((_keep(xc, lo), _keep(xc, hi)))
    return tuple(out)


def _attend_cols(q, k_vars, v_vars, extra_k=None, extra_v=None, bias=None):
    cols = []
    for j in range(q.shape[1] // LANES):
        qc = q[:, j * LANES:(j + 1) * LANES]
        o = None
        for hh in range(2):
            s = _dot_nt(qc, k_vars[j][hh])
            if bias is not None:
                s = s + bias[j][hh]
            scores, values = [s], [v_vars[j][hh]]
            if extra_k is not None:
                scores.append(_dot_nt(qc, extra_k[j][hh]))
                values.append(extra_v[j][hh])
            oh = _softmax_pv(scores, values)
            o = oh if o is None else o + oh
        cols.append(o)
    return jnp.concatenate(cols, axis=1)


def _attn_ctx_kernel(qb_ref, kb_ref, vb_ref, qc_ref, kc_ref, vc_ref, yb_ref, yc_ref):
    yb = _attend_cols(qb_ref[...], _gqa_variants(kb_ref[...]), _gqa_variants(vb_ref[...]))
    yb_ref[...] = yb.astype(BF16)
    yc = _attend_cols(qc_ref[...], _mha_variants(kc_ref[...]), _mha_variants(vc_ref[...]))
    yc_ref[...] = yc.astype(BF16)


def _attn_ctx(qb, kb, vb, qc, kc, vc):
    row = lambda w: pl.BlockSpec((SEQ, w), lambda i: (i, 0))
    return pl.pallas_call(
        _attn_ctx_kernel,
        grid=(BATCH,),
        in_specs=[row(QB_W), row(KB_W), row(KB_W), row(C_W), row(C_W), row(C_W)],
        out_specs=[row(QB_W), row(C_W)],
        out_shape=[jax.ShapeDtypeStruct((T_CTX, QB_W), BF16), jax.ShapeDtypeStruct((T_CTX, C_W), BF16)],
        compiler_params=pltpu.CompilerParams(dimension_semantics=("arbitrary",), vmem_limit_bytes=VMEM_LIMIT),
        name="attn_ctx",
    )(qb, kb, vb, qc, kc, vc)


def _attn_latb_kernel(q_ref, k_ref, v_ref, ck_ref, cv_ref, y_ref):
    y = _attend_cols(q_ref[...], _gqa_variants(ck_ref[0, 0]), _gqa_variants(cv_ref[0, 0]),
                     extra_k=_gqa_variants(k_ref[...]), extra_v=_gqa_variants(v_ref[...]))
    y_ref[...] = y.astype(BF16)


def _attn_latb(layer, qb, kb, vb, cache_k, cache_v):
    ctx_h = T_CTX // HALF
    ctx_r = T_CTX // DEC_SEQ
    cache = pl.BlockSpec((1, 1, PAST_LEN, KB_W), lambda b, s: (b, layer, 0, 0))
    own = pl.BlockSpec((DEC_SEQ, KB_W), lambda b, s: (ctx_r + b, 0))
    return pl.pallas_call(
        _attn_latb_kernel,
        grid=(DEC_BATCH, DEC_SEQ // HALF),
        in_specs=[pl.BlockSpec((HALF, QB_W), lambda b, s: (ctx_h + 2 * b + s, 0)), own, own, cache, cache],
        out_specs=pl.BlockSpec((HALF, QB_W), lambda b, s: (2 * b + s, 0)),
        out_shape=jax.ShapeDtypeStruct((T_LAT, QB_W), BF16),
        compiler_params=pltpu.CompilerParams(
            dimension_semantics=("arbitrary", "arbitrary"), vmem_limit_bytes=VMEM_LIMIT),
        name="attn_latb",
    )(qb, kb, vb, cache_k, cache_v)


def _natten_window(s, i):
    r = (HALF // GRID_W) * s + i
    rs = min(max(r - WIN_R // 2, 0), GRID_ROWS - WIN_R)
    return r, rs


def _natten_kernel(q_ref, k_ref, v_ref, ck_ref, cv_ref, blk_ref, y_ref, bias_ref):
    s = pl.program_id(1)

    for sv in range(2):
        @pl.when((pl.program_id(2) == 0) & (s == sv))
        def _():
            lo_half, _ = _lane_halves()
            masked = jnp.full((GRID_W, LANES), NEG, F32)
            for hh in range(2):
                for i in range(HALF // GRID_W):
                    r, rs = _natten_window(sv, i)
                    for jp in range(NAT_KEYS // LANES):
                        pair = []
                        for j in (2 * jp, 2 * jp + 1):
                            rk = 4 * sv + j
                            pair.append(blk_ref[0, hh, rk - r + WIN_R - 1] if rs <= rk < rs + WIN_R else masked)
                        bias_ref[hh, i * GRID_W:(i + 1) * GRID_W, jp * LANES:(jp + 1) * LANES] = (
                            jnp.where(lo_half, pair[0], pair[1]))

    start = pl.multiple_of(s * (4 * GRID_W), 4 * GRID_W)
    k = _mha_variants(k_ref[pl.ds(start, NAT_KEYS), :])
    v = _mha_variants(v_ref[pl.ds(start, NAT_KEYS), :])
    ck = _mha_variants(ck_ref[0, 0])
    cv = _mha_variants(cv_ref[0, 0])
    bias = ((bias_ref[0], bias_ref[1]),)
    y = _attend_cols(q_ref[...], k, v, extra_k=ck, extra_v=cv, bias=bias)
    y_ref[...] = y.astype(BF16)


def _natten(layer, qc, kc, vc, cache_k, cache_v, blocks):
    ctx_h = T_CTX // HALF
    ctx_r = T_CTX // DEC_SEQ
    cache = pl.BlockSpec((1, 1, PAST_LEN, LANES), lambda j, s, b: (b, layer, 0, j))
    own = pl.BlockSpec((DEC_SEQ, LANES), lambda j, s, b: (ctx_r + b, j))
    return pl.pallas_call(
        _natten_kernel,
        grid=(C_W // LANES, DEC_SEQ // HALF, DEC_BATCH),
        in_specs=[pl.BlockSpec((HALF, LANES), lambda j, s, b: (ctx_h + 2 * b + s, j)), own, own, cache, cache,
                  pl.BlockSpec((1, 2, 2 * WIN_R - 1, GRID_W, LANES), lambda j, s, b: (j, 0, 0, 0, 0))],
        out_specs=pl.BlockSpec((HALF, LANES), lambda j, s, b: (2 * b + s, j)),
        out_shape=jax.ShapeDtypeStruct((T_LAT, C_W), BF16),
        scratch_shapes=[pltpu.VMEM((2, HALF, NAT_KEYS), F32)],
        compiler_params=pltpu.CompilerParams(
            dimension_semantics=("arbitrary", "arbitrary", "arbitrary"), vmem_limit_bytes=VMEM_LIMIT),
        name="natten",
    )(qc, kc, vc, cache_k, cache_v, blocks)


def _natten_blocks(rpb):
    qcol = np.arange(GRID_W)[:, None]
    kcol = np.arange(GRID_W)[None, :]
    cs = np.clip(qcol - WIN_C // 2, 0, GRID_W - WIN_C)
    col_ok = (kcol >= cs) & (kcol < cs + WIN_C)
    dc = np.clip(kcol - qcol, -(WIN_C - 1), WIN_C - 1) + (WIN_C - 1)
    sel_c = (dc[:, :, None] == np.arange(2 * WIN_C - 1)).astype(np.float32)
    blk = jnp.einsum("hdm,qkm->hdqk", rpb, jnp.asarray(sel_c), precision=lax.Precision.HIGHEST)
    blk = jnp.where(jnp.asarray(col_ok), blk, NEG)
    blk = jnp.concatenate([blk, blk], axis=-1)
    return blk.reshape(H_C // 2, 2, 2 * WIN_R - 1, GRID_W, LANES)


def _route(lg):
    lane = lax.broadcasted_iota(jnp.int32, lg.shape, 1).astype(F32)
    low = jnp.float32(-3.0e38)
    far = jnp.float32(ROUTER_LANES)
    is_g = lane < N_GROUPS
    gmax = jnp.where(is_g, lg, low).max(axis=-1, keepdims=True)
    gsel = jnp.where(is_g & (lg == gmax), lane, far).min(axis=-1, keepdims=True)
    pg_sel = 1.0 / jnp.where(is_g, jnp.exp(lg - gmax), 0.0).sum(axis=-1, keepdims=True)
    e0 = EXPERT_LANE0 + EXPERTS_PER_GROUP * gsel
    in_g = (lane >= e0) & (lane < e0 + EXPERTS_PER_GROUP)
    m1 = jnp.where(in_g, lg, low).max(axis=-1, keepdims=True)
    i1 = jnp.where(in_g & (lg == m1), lane, far).min(axis=-1, keepdims=True)
    rest = in_g & (lane != i1)
    m2 = jnp.where(rest, lg, low).max(axis=-1, keepdims=True)
    i2 = jnp.where(rest & (lg == m2), lane, far).min(axis=-1, keepdims=True)
    t = jnp.exp(m2 - m1)
    ssum = 1.0 + t
    w1 = (1.0 / ssum) * pg_sel
    w2 = (t / ssum) * pg_sel
    return i1 - EXPERT_LANE0, i2 - EXPERT_LANE0, w1, w2


def _rows_to_groups(x, stage_ref, out_ref, n_rows):
    stride = n_rows + 8
    for c in range(GROUP_ROWS):
        stage_ref[c * stride:c * stride + n_rows, :] = x[:, c * LANES:(c + 1) * LANES]
    for j in range(n_rows):
        out_ref[GROUP_ROWS * j:GROUP_ROWS * (j + 1), :] = stage_ref[pl.ds(j, GROUP_ROWS, stride=stride), :]


def _tail_kernel(xc_ref, xl_ref, ya_ref, ybc_ref, ybl_ref, ycc_ref, ycl_ref, w_ref, g1_ref, sh2_ref, sc2_ref,
                 lg_ref, lb_ref, wr_ref, br_ref, x1_ref, h2g_ref, ids_ref, wts_ref, wbf_ref, stage_ref):
    @pl.when(pl.program_id(0) == 0)
    def _():
        wbf_ref[...] = w_ref[0].astype(BF16)

    o = (_dot(ya_ref[...], wbf_ref[0:POOL_WIDTH, :])
         + _dot(_split_rows(ybc_ref, ybl_ref), wbf_ref[POOL_WIDTH:POOL_WIDTH + QB_W, :])
         + _dot(_split_rows(ycc_ref, ycl_ref), wbf_ref[POOL_WIDTH + QB_W:, :]))
    x1 = _layer_norm(ALPHA * _split_rows(xc_ref, xl_ref) + g1_ref[0] * o, lg_ref[0], lb_ref[0])
    x1_ref[...] = x1
    h2 = x1 * (1.0 + sc2_ref[0]) + sh2_ref[0]
    hh, hl = _split_bf16(h2)
    wh, wl = _split_bf16(wr_ref[0])
    lg = _dot(hh, wh) + _dot(hl, wh) + _dot(hh, wl) + br_ref[0]
    i1, i2, w1, w2 = _route(lg)
    lane = lax.broadcasted_iota(jnp.int32, lg.shape, 1)
    ids = jnp.where(lane == 0, i1, jnp.where(lane == 1, i2, 0.0))
    ids_ref[0] = ids.T[0:8, :].astype(jnp.int32)
    wts = jnp.where(lane == 0, w1, jnp.where(lane == 1, w2, 0.0))
    wts_ref[0] = wts.T[0:8, :]
    _rows_to_groups(h2, stage_ref, h2g_ref, TT)


def _tail(layer, x_ctx, x_lat, ya, yb_ctx, yb_lat, yc_ctx, yc_lat, w_out, mod, ln_g, ln_b, wr, br):
    row = lambda w: pl.BlockSpec((TT, w), lambda i: (i, 0))
    return pl.pallas_call(
        _tail_kernel,
        grid=(T_ALL // TT,),
        in_specs=[*_split_row_specs(D_MODEL), row(POOL_WIDTH), *_split_row_specs(QB_W), *_split_row_specs(C_W),
                  _layer_spec(layer, (D_MODEL, D_MODEL)),
                  _mod_spec(layer, 2, TT), _mod_spec(layer, 3, TT), _mod_spec(layer, 4, TT),
                  _layer_spec(layer, (1, D_MODEL)), _layer_spec(layer, (1, D_MODEL)),
                  _layer_spec(layer, (D_MODEL, ROUTER_LANES)), _layer_spec(layer, (1, ROUTER_LANES))],
        out_specs=[row(D_MODEL), pl.BlockSpec((TT * GROUP_ROWS, LANES), lambda i: (i, 0)),
                   pl.BlockSpec((1, 8, TT), lambda i: (i, 0, 0)), pl.BlockSpec((1, 8, TT), lambda i: (i, 0, 0))],
        out_shape=[jax.ShapeDtypeStruct((T_ALL, D_MODEL), F32),
                   jax.ShapeDtypeStruct((T_ALL * GROUP_ROWS, LANES), F32),
                   jax.ShapeDtypeStruct((T_ALL // TT, 8, TT), jnp.int32),
                   jax.ShapeDtypeStruct((T_ALL // TT, 8, TT), F32)],
        scratch_shapes=[pltpu.VMEM((D_MODEL, D_MODEL), BF16),
                        pltpu.VMEM((GROUP_ROWS * (TT + 8), LANES), F32)],
        compiler_params=pltpu.CompilerParams(dimension_semantics=("arbitrary",), vmem_limit_bytes=VMEM_LIMIT),
        name="tail",
    )(x_ctx, x_lat, ya, yb_ctx, yb_lat, yc_ctx, yc_lat, w_out, mod, mod, mod, ln_g, ln_b, wr, br)


def _plan_kernel(eid_ref, pos_ref, tile_ref):
    e = eid_ref[0]
    r = lax.broadcasted_iota(jnp.int32, (LANES, LANES), 0)
    c = lax.broadcasted_iota(jnp.int32, (LANES, LANES), 1)
    upper = (r <= c).astype(BF16)
    rows = e.shape[0]
    lower = (lax.broadcasted_iota(jnp.int32, (rows, rows), 1)
             < lax.broadcasted_iota(jnp.int32, (rows, rows), 0)).astype(BF16)
    tile_start = (lax.broadcasted_iota(jnp.int32, (1, LANES), 1) * TMX).astype(F32)
    lane = lax.broadcasted_iota(jnp.int32, (1, LANES), 1)
    pos = jnp.zeros(e.shape, F32)
    base = jnp.zeros((1, LANES), F32)
    n_before = jnp.zeros((1, LANES), F32)
    valid_end = jnp.zeros((1, LANES), F32)
    pad_end = jnp.zeros((1, LANES), F32)
    for ex in range(N_EXPERTS):
        m = e == ex
        incl = _dot(m.astype(BF16), upper)
        row_tot = jnp.broadcast_to(incl[:, LANES - 1:LANES], incl.shape)
        row_off = _dot(lower, row_tot.astype(BF16))
        cnt = jnp.sum(row_tot, axis=0, keepdims=True)
        pos = jnp.where(m, base + row_off + incl - 1.0, pos)
        valid_end = jnp.where(lane == ex, base + cnt, valid_end)
        base = base + jnp.ceil(cnt * (1.0 / TMX)) * TMX
        pad_end = jnp.where(lane == ex, base, pad_end)
        n_before = n_before + (tile_start >= base).astype(F32)
    pos_ref[0] = pos.astype(jnp.int32)
    sub = lax.broadcasted_iota(jnp.int32, (8, LANES), 0)
    rows8 = lambda v: jnp.broadcast_to(v, (8, LANES))
    table = jnp.where(sub == PLAN_TILE_EXPERT, rows8(jnp.minimum(n_before, N_EXPERTS - 1.0)),
                      jnp.where(sub == PLAN_N_TILES, rows8(base * (1.0 / TMX)),
                                jnp.where(sub == PLAN_VALID_END, rows8(valid_end),
                                          jnp.where(sub == PLAN_PAD_END, rows8(pad_end), 0.0))))
    tile_ref[0] = table.astype(jnp.int32)


def _plan(eid):
    rows = 2 * T_STREAM // LANES
    return pl.pallas_call(
        _plan_kernel,
        grid=(N_STREAMS,),
        in_specs=[pl.BlockSpec((1, rows, LANES), lambda s: (s, 0, 0))],
        out_specs=[pl.BlockSpec((1, rows, LANES), lambda s: (s, 0, 0)), pl.BlockSpec((1, 8, LANES), lambda s: (s, 0, 0))],
        out_shape=[jax.ShapeDtypeStruct((N_STREAMS, rows, LANES), jnp.int32),
                   jax.ShapeDtypeStruct((N_STREAMS, 8, LANES), jnp.int32)],
        compiler_params=pltpu.CompilerParams(dimension_semantics=("arbitrary",), vmem_limit_bytes=VMEM_LIMIT),
        name="plan",
    )(eid)


def _moe_kernel(pos_ref, wpair_ref, plan_ref, h2g_hbm, wg_ref, wu_ref, wd_ref, y_hbm,
                h2v, yv, off, wcol, xg0, xg1, yg0, yg1, wgb, wub, wdb, sem):
    s = pl.program_id(0)
    e = pl.program_id(1)
    n_pairs = 2 * T_STREAM
    spare_off = T_STREAM * GROUP_ROWS
    plan = lambda row, idx: plan_ref[(s * 8 + row) * LANES + idx]
    n_tiles = plan(PLAN_N_TILES, 0)
    last = n_tiles - 1

    xgs = (xg0, xg1)
    ygs = (yg0, yg1)

    def gather(tile, dst):
        base = tile * TMX
        for j in range(TMX):
            src = pl.multiple_of(off[base + j], GROUP_ROWS)
            dst[pl.ds(j, GROUP_ROWS, stride=XSTRIDE), :] = h2v[pl.ds(src, GROUP_ROWS), :]

    def h2_copy():
        return pltpu.make_async_copy(
            h2g_hbm.at[pl.ds(pl.multiple_of(s * (T_STREAM * GROUP_ROWS), 8), T_STREAM * GROUP_ROWS)],
            h2v.at[pl.ds(0, T_STREAM * GROUP_ROWS)], sem.at[0])

    def y_copy():
        return pltpu.make_async_copy(yv.at[pl.ds(0, T_STREAM * GROUP_ROWS)], y_hbm.at[s], sem.at[1])

    @pl.when(e == 0)
    def _():
        h2_copy().start()
        yv[...] = jnp.zeros_like(yv)
        h2v[spare_off:spare_off + GROUP_ROWS, :] = jnp.zeros((GROUP_ROWS, LANES), F32)

        def pad_expert(ex, carry):
            def pad_row(r, c):
                off[r] = spare_off
                wcol[pl.ds(r, 1), :] = jnp.zeros((1, LANES), F32)
                return c
            return lax.fori_loop(plan(PLAN_VALID_END, ex), plan(PLAN_PAD_END, ex), pad_row, carry)
        lax.fori_loop(0, N_EXPERTS, pad_expert, 0)

        def place(i, carry):
            for u in range(PLACE_UNROLL):
                tok = i * PLACE_UNROLL + u
                for slot in range(2):
                    p = s * n_pairs + slot * T_STREAM + tok
                    r = pos_ref[p]
                    off[r] = tok * GROUP_ROWS
                    wcol[pl.ds(r, 1), :] = jnp.full((1, LANES), wpair_ref[p], F32)
            return carry
        lax.fori_loop(0, T_STREAM // PLACE_UNROLL, place, 0)
        yg1[...] = jnp.zeros_like(yg1)
        h2_copy().wait()
        gather(0, xg0)

    t_lo = jnp.where(e == 0, 0, plan(PLAN_PAD_END, jnp.maximum(e - 1, 0))) // TMX
    t_hi = plan(PLAN_PAD_END, e) // TMX

    @pl.when(t_hi > t_lo)
    def _():
        wgb[...] = wg_ref[0, 0].astype(BF16)
        wub[...] = wu_ref[0, 0].astype(BF16)
        wdb[...] = wd_ref[0, 0].astype(BF16)

    def add_tile(tile, done):
        base = tile * TMX
        for j0 in range(0, TMX, RMW_BATCH):
            updates = []
            for j in range(j0, j0 + RMW_BATCH):
                dst = pl.multiple_of(off[base + j], GROUP_ROWS)
                updates.append((dst, yv[pl.ds(dst, GROUP_ROWS), :] + done[pl.ds(j, GROUP_ROWS, stride=XSTRIDE), :]))
            for dst, val in updates:
                yv[pl.ds(dst, GROUP_ROWS), :] = val

    def step(t, cur, nxt, out, done):
        gather(jnp.minimum(t + 1, last), nxt)
        x = jnp.concatenate([cur[c * XSTRIDE:c * XSTRIDE + TMX, :] for c in range(GROUP_ROWS)], axis=1).astype(BF16)
        hg = _dot(x, wgb[...])
        hu = _dot(x, wub[...])
        gate = wcol[pl.ds(pl.multiple_of(t * TMX, TMX), TMX), :]
        act = hg * jax.nn.sigmoid(hg) * hu * jnp.concatenate([gate] * (D_EXPERT // LANES), axis=1)
        y = _dot(act.astype(BF16), wdb[...])
        for c in range(GROUP_ROWS):
            out[c * XSTRIDE:c * XSTRIDE + TMX, :] = y[:, c * LANES:(c + 1) * LANES]
        add_tile(jnp.maximum(t - 1, 0), done)

    def tile_body(t, carry):
        for par in range(2):
            @pl.when((t & 1) == par)
            def _():
                step(t, xgs[par], xgs[1 - par], ygs[par], ygs[1 - par])
        return carry
    lax.fori_loop(t_lo, t_hi, tile_body, 0)

    @pl.when(e == N_EXPERTS - 1)
    def _():
        for par in range(2):
            @pl.when((last & 1) == par)
            def _():
                add_tile(last, ygs[par])
        y_copy().start()
        y_copy().wait()


def _moe(layer, pos, wpair, plan, h2g, w_gate, w_up, w_down):
    wspec = lambda shape: pl.BlockSpec((1,) + shape, lambda s, e, pos, wpair, plan: (layer, e, 0, 0))
    return pl.pallas_call(
        _moe_kernel,
        grid_spec=pltpu.PrefetchScalarGridSpec(
            num_scalar_prefetch=3,
            grid=(N_STREAMS, N_EXPERTS),
            in_specs=[pl.BlockSpec(memory_space=pl.ANY),
                      wspec((1, D_MODEL, D_EXPERT)), wspec((1, D_MODEL, D_EXPERT)), wspec((1, D_EXPERT, D_MODEL))],
            out_specs=pl.BlockSpec(memory_space=pl.ANY),
            scratch_shapes=[pltpu.VMEM(((T_STREAM + 1) * GROUP_ROWS, LANES), F32),
                            pltpu.VMEM(((T_STREAM + 1) * GROUP_ROWS, LANES), F32),
                            pltpu.SMEM((P_MAX,), jnp.int32),
                            pltpu.VMEM((P_MAX, LANES), F32),
                            pltpu.VMEM((GROUP_ROWS * XSTRIDE, LANES), F32),
                            pltpu.VMEM((GROUP_ROWS * XSTRIDE, LANES), F32),
                            pltpu.VMEM((GROUP_ROWS * XSTRIDE, LANES), F32),
                            pltpu.VMEM((GROUP_ROWS * XSTRIDE, LANES), F32),
                            pltpu.VMEM((D_MODEL, D_EXPERT), BF16), pltpu.VMEM((D_MODEL, D_EXPERT), BF16),
                            pltpu.VMEM((D_EXPERT, D_MODEL), BF16),
                            pltpu.SemaphoreType.DMA((2,))]),
        out_shape=jax.ShapeDtypeStruct((N_STREAMS, T_STREAM * GROUP_ROWS, LANES), F32),
        compiler_params=pltpu.CompilerParams(
            dimension_semantics=("arbitrary", "arbitrary"), vmem_limit_bytes=VMEM_LIMIT),
        name="moe",
    )(pos, wpair, plan, h2g, w_gate, w_up, w_down)


def _groups_to_rows(src_ref, stage_ref, n_rows):
    stride = n_rows + 8
    for j in range(n_rows):
        stage_ref[pl.ds(j, GROUP_ROWS, stride=stride), :] = src_ref[GROUP_ROWS * j:GROUP_ROWS * (j + 1), :]
    return jnp.concatenate([stage_ref[c * stride:c * stride + n_rows, :] for c in range(GROUP_ROWS)], axis=1)


def _final_kernel(x1_ref, y_ref, g2_ref, lg_ref, lb_ref, o_ref, stage_ref):
    moe = _groups_to_rows(y_ref, stage_ref, TT)
    o_ref[...] = _layer_norm(ALPHA * x1_ref[...] + g2_ref[0] * moe, lg_ref[0], lb_ref[0])


def _final(layer, stream, x1, y, mod, ln_g, ln_b):
    first = stream * (T_STREAM // TT)
    g2 = pl.BlockSpec(
        (1, 1, D_MODEL), lambda i: ((layer * MOD_ROWS + _mod_row_of_tile(first + i, TT)) * 6 + 5, 0, 0))
    return pl.pallas_call(
        _final_kernel,
        grid=(T_STREAM // TT,),
        in_specs=[pl.BlockSpec((TT, D_MODEL), lambda i: (first + i, 0)),
                  pl.BlockSpec((TT * GROUP_ROWS, LANES), lambda i: (first + i, 0)),
                  g2, _layer_spec(layer, (1, D_MODEL)), _layer_spec(layer, (1, D_MODEL))],
        out_specs=pl.BlockSpec((TT, D_MODEL), lambda i: (i, 0)),
        out_shape=jax.ShapeDtypeStruct((T_STREAM, D_MODEL), F32),
        scratch_shapes=[pltpu.VMEM((GROUP_ROWS * (TT + 8), LANES), F32)],
        compiler_params=pltpu.CompilerParams(dimension_semantics=("arbitrary",), vmem_limit_bytes=VMEM_LIMIT),
        name="final",
    )(x1, y, mod, ln_g, ln_b)


def _rope_tables():
    t = np.arange(DEC_SEQ)
    pos = np.stack([t // GRID_W, t % GRID_W], axis=1).astype(np.float32)
    nf = HEAD_DIM // 4
    inv = jnp.asarray(ROPE_THETA, F32) ** (-jnp.arange(nf, dtype=F32) / nf)
    d = np.arange(LANES) % HEAD_DIM
    which = d // (HEAD_DIM // 2)
    ang = jnp.asarray(pos)[:, which] * inv[d % nf][None, :]
    sign = np.where((d % 32) < 16, -1.0, 1.0).astype(np.float32)
    cos = jnp.concatenate([jnp.cos(ang), jnp.ones((TT, LANES), F32)], axis=0)
    sin = jnp.concatenate([jnp.sin(ang) * sign[None, :], jnp.zeros((TT, LANES), F32)], axis=0)
    return cos, sin


def _block_ones():
    h = np.arange(QB_W) // HEAD_DIM
    return jnp.asarray((h[:, None] == h[None, :]).astype(np.float32), dtype=BF16)


def _pool_block_diag(pool_w):
    out = jnp.zeros((POOL_WIDTH, POOL_WIDTH), F32)
    for g in range(4):
        out = out.at[64 * g:64 * g + 64, 64 * g:64 * g + 64].set(pool_w[g])
    return out


def kernel(x_prompt, x_sample, cache_b_k, cache_b_v, cache_c_k, cache_c_v, c, c_ctx, w_ada, b_ada, w_in, w_out,
           pool_w, pool_scale, q_norm, k_norm, rpb, ln1_g, ln1_b, ln2_g, ln2_b, router_g, router_g_b, router_e,
           router_e_b, w_gate, w_up, w_down):
    x_ctx = x_prompt.reshape(T_CTX, D_MODEL)
    x_lat = x_sample.reshape(T_LAT, D_MODEL)
    c8 = jnp.concatenate([c, c_ctx[None], jnp.zeros((MOD_ROWS - DEC_BATCH - 1, D_MODEL), F32)], axis=0)
    mod = _adaln(c8, w_ada, b_ada).reshape(DEPTH * MOD_ROWS * 6, 1, D_MODEL)

    cos_t, sin_t = _rope_tables()
    ones_bd = _block_ones()
    cbk = cache_b_k.reshape(DEC_BATCH, DEPTH, PAST_LEN, KB_W)
    cbv = cache_b_v.reshape(DEC_BATCH, DEPTH, PAST_LEN, KB_W)
    cck = cache_c_k.reshape(DEC_BATCH, DEPTH, PAST_LEN, C_W)
    ccv = cache_c_v.reshape(DEC_BATCH, DEPTH, PAST_LEN, C_W)
    pad = jnp.zeros((DEPTH, D_MODEL, ROUTER_LANES - N_GROUPS - N_EXPERTS), F32)
    wr = jnp.concatenate([router_g, router_e, pad], axis=2)
    br = jnp.concatenate([router_g_b, router_e_b, pad[:, 0]], axis=1)[:, None]
    per_layer = lambda p: p[:, None]

    new_bk, new_bv, new_ck, new_cv = [], [], [], []
    for l in range(DEPTH):
        qn = jnp.tile(q_norm[l], H_B)[None]
        kn = jnp.tile(k_norm[l], KV_B)[None]
        a, qb, kb, vb, qc, kc, vc = _proj(l, x_ctx, x_lat, mod, w_in, ones_bd, qn, kn, cos_t, sin_t)
        ya = _pool(a, _pool_block_diag(pool_w[l]), pool_scale[l][None])
        yb_ctx, yc_ctx = _attn_ctx(qb, kb, vb, qc, kc, vc)
        yb_lat = _attn_latb(l, qb, kb, vb, cbk, cbv)
        yc_lat = _natten(l, qc, kc, vc, cck, ccv, _natten_blocks(rpb[l]))
        x1, h2g, ids, wts = _tail(l, x_ctx, x_lat, ya, yb_ctx, yb_lat, yc_ctx, yc_lat, w_out, mod,
                                  per_layer(ln1_g), per_layer(ln1_b), wr, br)
        pairs = lambda a: a.reshape(N_STREAMS, T_STREAM // TT, 8, TT)[:, :, :2, :].transpose(0, 2, 1, 3)
        pos, tiles = _plan(pairs(ids).reshape(N_STREAMS, 2 * T_STREAM // LANES, LANES))
        y = _moe(l, pos.reshape(-1), pairs(wts).reshape(-1), tiles.reshape(-1), h2g, w_gate, w_up, w_down)
        y = y.reshape(T_ALL * GROUP_ROWS, LANES)
        x_ctx = _final(l, 0, x1, y, mod, per_layer(ln2_g), per_layer(ln2_b))
        x_lat = _final(l, 1, x1, y, mod, per_layer(ln2_g), per_layer(ln2_b))
        new_bk.append(kb[:T_CTX].reshape(BATCH, SEQ, KV_B, HEAD_DIM))
        new_bv.append(vb[:T_CTX].reshape(BATCH, SEQ, KV_B, HEAD_DIM))
        new_ck.append(kc[:T_CTX].reshape(BATCH, SEQ, H_C, HEAD_DIM))
        new_cv.append(vc[:T_CTX].reshape(BATCH, SEQ, H_C, HEAD_DIM))

    y_prompt = x_ctx.reshape(BATCH, SEQ, D_MODEL)
    y_sample = x_lat.reshape(DEC_BATCH, DEC_SEQ, D_MODEL)
    return (y_prompt, y_sample, jnp.stack(new_bk, axis=1), jnp.stack(new_bv, axis=1),
            jnp.stack(new_ck, axis=1), jnp.stack(new_cv, axis=1))
```

```python
import functools

import numpy as np
import jax
import jax.numpy as jnp
from jax import lax
from jax.experimental import pallas as pl
from jax.experimental.pallas import tpu as pltpu

F32 = jnp.float32
BF16 = jnp.bfloat16

D_MODEL = 1024
BATCH = 16
SEQ = 256
DEPTH = 2
DEC_BATCH = 4
DEC_SEQ = 1024
PAST_LEN = 256
GRID_W = 64
GRID_ROWS = DEC_SEQ // GRID_W
HEAD_DIM = 64
POOL_WIDTH = 256
POOL_WINDOWS = (2, 4, 8, 16)
H_B = 6
KV_B = 2
H_C = 6
WIN_R = 8
WIN_C = 16
ROPE_THETA = 10000.0
QB_W = H_B * HEAD_DIM
KB_W = KV_B * HEAD_DIM
C_W = H_C * HEAD_DIM
PROJ_WIDTH = 2048
N_GROUPS = 4
EXPERTS_PER_GROUP = 8
N_EXPERTS = 32
D_EXPERT = 256
ALPHA = (2 * DEPTH) ** 0.25
LN_EPS = 1e-6
RMS_EPS = 1e-6
NEG = -1e30
ATTN_SCALE = HEAD_DIM ** -0.5

T_CTX = BATCH * SEQ
T_LAT = DEC_BATCH * DEC_SEQ
T_ALL = T_CTX + T_LAT

LANES = 128
ROUTER_LANES = 128
EXPERT_LANE0 = N_GROUPS
MOD_ROWS = 8
CTX_MOD_ROW = DEC_BATCH

TT = 512
TP = 1024
POOL_GAP = 16
N_STREAMS = 2
T_STREAM = T_ALL // N_STREAMS
TMX = 128
PLACE_UNROLL = 8
P_MAX = 2 * T_STREAM + N_EXPERTS * TMX
NT_MAX = P_MAX // TMX
GROUP_ROWS = D_MODEL // LANES
XSTRIDE = TMX + 8
RMW_BATCH = 16
PLAN_TILE_EXPERT, PLAN_N_TILES, PLAN_VALID_END, PLAN_PAD_END = 0, 1, 2, 3
HALF = DEC_SEQ // 2
NAT_KEYS = 12 * GRID_W
VMEM_LIMIT = 56 * 1024 * 1024


def _dot(a, b):
    return jnp.dot(a, b, preferred_element_type=F32)


def _dot_nt(a, b):
    return lax.dot_general(a, b, (((1,), (1,)), ((), ())), preferred_element_type=F32)


def _split_bf16(x):
    hi = x.astype(BF16)
    lo = (x - hi.astype(F32)).astype(BF16)
    return hi, lo


def _layer_norm(y, g, b):
    mu = jnp.mean(y, axis=-1, keepdims=True)
    var = jnp.mean(jnp.square(y - mu), axis=-1, keepdims=True)
    return (y - mu) * lax.rsqrt(var + LN_EPS) * g + b


def _adaln_kernel(c_ref, w_ref, b_ref, o_ref):
    c = c_ref[...]
    s = (c * jax.nn.sigmoid(c)).astype(BF16)
    o_ref[0] = _dot(s, w_ref[0].astype(BF16)) + b_ref[0]


def _adaln(c8, w_ada, b_ada):
    tn = 1536
    n = w_ada.shape[-1]
    return pl.pallas_call(
        _adaln_kernel,
        grid=(DEPTH, n // tn),
        in_specs=[
            pl.BlockSpec((MOD_ROWS, D_MODEL), lambda l, j: (0, 0)),
            pl.BlockSpec((1, D_MODEL, tn), lambda l, j: (l, 0, j)),
            pl.BlockSpec((1, 1, tn), lambda l, j: (l, 0, j)),
        ],
        out_specs=pl.BlockSpec((1, MOD_ROWS, tn), lambda l, j: (l, 0, j)),
        out_shape=jax.ShapeDtypeStruct((DEPTH, MOD_ROWS, n), F32),
        compiler_params=pltpu.CompilerParams(
            dimension_semantics=("arbitrary", "arbitrary"), vmem_limit_bytes=VMEM_LIMIT),
        name="adaln",
    )(c8, w_ada, b_ada.reshape(DEPTH, 1, n))


def _mod_row_of_tile(i, tile):
    n_ctx = T_CTX // tile
    per_req = DEC_SEQ // tile
    return jnp.where(i < n_ctx, CTX_MOD_ROW, (i - n_ctx) // per_req)


def _mod_spec(layer, chunk, tile):
    return pl.BlockSpec(
        (1, 1, D_MODEL), lambda i: ((layer * MOD_ROWS + _mod_row_of_tile(i, tile)) * 6 + chunk, 0, 0))


def _layer_spec(layer, shape):
    return pl.BlockSpec((1,) + shape, lambda *_: (layer,) + (0,) * len(shape))


N_CTX_TILES = T_CTX // TT


def _split_row_specs(width):
    return (pl.BlockSpec((TT, width), lambda i: (jnp.minimum(i, N_CTX_TILES - 1), 0)),
            pl.BlockSpec((TT, width), lambda i: (jnp.maximum(i - N_CTX_TILES, 0), 0)))


def _split_rows(ctx_ref, lat_ref):
    return jnp.where(pl.program_id(0) < N_CTX_TILES, ctx_ref[...], lat_ref[...])


def _rms_norm_heads(x, ones_bd, w):
    hi, lo = _split_bf16(x * x)
    ssq = _dot(hi, ones_bd) + _dot(lo, ones_bd)
    return x * lax.rsqrt(ssq * (1.0 / HEAD_DIM) + RMS_EPS) * w


def _rope_cols(x, cos, sin, first16):
    cols = []
    for j in range(x.shape[1] // LANES):
        xc = x[:, j * LANES:(j + 1) * LANES]
        partner = jnp.where(first16, pltpu.roll(xc, LANES - 16, axis=1), pltpu.roll(xc, 16, axis=1))
        cols.append(xc * cos + partner * sin)
    return jnp.concatenate(cols, axis=1) if len(cols) > 1 else cols[0]


def _proj_kernel(xc_ref, xl_ref, sh_ref, sc_ref, w_ref, ones_ref, qn_ref, kn_ref, cos_ref, sin_ref,
                 a_ref, qb_ref, kb_ref, vb_ref, qc_ref, kc_ref, vc_ref, wbf_ref):
    @pl.when(pl.program_id(0) == 0)
    def _():
        wbf_ref[...] = w_ref[0].astype(BF16)

    h = _split_rows(xc_ref, xl_ref) * (1.0 + sc_ref[0]) + sh_ref[0]
    p = _dot(h.astype(BF16), wbf_ref[...])
    o = 0
    a_ref[...] = p[:, o:o + POOL_WIDTH]; o += POOL_WIDTH
    qb = p[:, o:o + QB_W]; o += QB_W
    kb = p[:, o:o + KB_W]; o += KB_W
    vb_ref[...] = p[:, o:o + KB_W]; o += KB_W
    qc_ref[...] = (p[:, o:o + C_W] * ATTN_SCALE).astype(BF16); o += C_W
    kc_ref[...] = p[:, o:o + C_W]; o += C_W
    vc_ref[...] = p[:, o:o + C_W]

    ones_bd = ones_ref[...]
    cos = cos_ref[...]
    sin = sin_ref[...]
    lane = lax.broadcasted_iota(jnp.int32, (1, LANES), 1)
    first16 = (lane & 31) < 16
    qb = _rope_cols(_rms_norm_heads(qb, ones_bd, qn_ref[...]), cos, sin, first16)
    kb = _rope_cols(_rms_norm_heads(kb, ones_bd[:KB_W, :KB_W], kn_ref[...]), cos, sin, first16)
    qb_ref[...] = (qb * ATTN_SCALE).astype(BF16)
    kb_ref[...] = kb


def _proj(layer, x_ctx, x_lat, mod, w_in, ones_bd, qn, kn, cos_t, sin_t):
    n_tiles = T_ALL // TT
    n_ctx = N_CTX_TILES
    per_req = DEC_SEQ // TT

    def rope_idx(i):
        return (jnp.where(i < n_ctx, per_req, (i - n_ctx) % per_req), 0)

    row = lambda w: pl.BlockSpec((TT, w), lambda i: (i, 0))
    const = lambda s: pl.BlockSpec(s, lambda i: (0,) * len(s))
    widths = (POOL_WIDTH, QB_W, KB_W, KB_W, C_W, C_W, C_W)
    dtypes = (F32, BF16, F32, F32, BF16, F32, F32)
    return pl.pallas_call(
        _proj_kernel,
        grid=(n_tiles,),
        in_specs=[*_split_row_specs(D_MODEL), _mod_spec(layer, 0, TT), _mod_spec(layer, 1, TT),
                  _layer_spec(layer, (D_MODEL, PROJ_WIDTH)),
                  const((QB_W, QB_W)), const((1, QB_W)), const((1, KB_W)),
                  pl.BlockSpec((TT, LANES), rope_idx), pl.BlockSpec((TT, LANES), rope_idx)],
        out_specs=[row(w) for w in widths],
        out_shape=[jax.ShapeDtypeStruct((T_ALL, w), d) for w, d in zip(widths, dtypes)],
        scratch_shapes=[pltpu.VMEM((D_MODEL, PROJ_WIDTH), BF16)],
        compiler_params=pltpu.CompilerParams(dimension_semantics=("arbitrary",), vmem_limit_bytes=VMEM_LIMIT),
        name="proj",
    )(x_ctx, x_lat, mod, mod, w_in, ones_bd, qn, kn, cos_t, sin_t)


def _pool_windows(p):
    n = p.shape[0]
    sh = lambda x, k: pltpu.roll(x, (n - k) % n, axis=0)
    w2 = p + sh(p, -1)
    w4 = sh(w2, -1) + sh(w2, 1)
    w8 = sh(w4, -2) + sh(w4, 2)
    w16 = sh(w8, -4) + sh(w8, 4)
    g = lax.broadcasted_iota(jnp.int32, p.shape, 1) >> 6
    return jnp.where(g == 0, w2, jnp.where(g == 1, w4, jnp.where(g == 2, w8, w16)))


def _pool_kernel(a_ref, w_ref, scale_ref, o_ref, pad_ref):
    a = a_ref[...]

    def mix(csum, seq_len):
        row = lax.broadcasted_iota(jnp.int32, a.shape, 0)
        g = lax.broadcasted_iota(jnp.int32, a.shape, 1) >> 6
        t = row & (seq_len - 1)
        half = jnp.where(g == 0, 1, jnp.where(g == 1, 2, jnp.where(g == 2, 4, 8)))
        cnt = jnp.minimum(t + half, seq_len) - jnp.maximum(t - half, 0)
        pooled = csum / cnt.astype(F32) - a
        y = _dot(pooled.astype(BF16), w_ref[...].astype(BF16)) * scale_ref[...]
        o_ref[...] = y.astype(BF16)

    def padded_sums(seq_len):
        pitch = seq_len + POOL_GAP
        n_req = TP // seq_len
        n = n_req * pitch
        pad_ref[...] = jnp.zeros_like(pad_ref)
        for r in range(n_req):
            pad_ref[r * pitch + POOL_GAP // 2:r * pitch + POOL_GAP // 2 + seq_len, :] = a[r * seq_len:(r + 1) * seq_len]
        sums = _pool_windows(pad_ref[0:n, :])
        return jnp.concatenate(
            [sums[r * pitch + POOL_GAP // 2:r * pitch + POOL_GAP // 2 + seq_len] for r in range(n_req)], axis=0)

    is_ctx = pl.program_id(0) < T_CTX // TP

    @pl.when(is_ctx)
    def _():
        mix(padded_sums(SEQ), SEQ)

    @pl.when(jnp.logical_not(is_ctx))
    def _():
        mix(padded_sums(DEC_SEQ), DEC_SEQ)


def _pool(a, w_bd, scale):
    return pl.pallas_call(
        _pool_kernel,
        grid=(T_ALL // TP,),
        in_specs=[pl.BlockSpec((TP, POOL_WIDTH), lambda i: (i, 0)),
                  pl.BlockSpec((POOL_WIDTH, POOL_WIDTH), lambda i: (0, 0)),
                  pl.BlockSpec((1, POOL_WIDTH), lambda i: (0, 0))],
        out_specs=pl.BlockSpec((TP, POOL_WIDTH), lambda i: (i, 0)),
        out_shape=jax.ShapeDtypeStruct((T_ALL, POOL_WIDTH), BF16),
        scratch_shapes=[pltpu.VMEM(((TP // SEQ) * (SEQ + POOL_GAP), POOL_WIDTH), F32)],
        compiler_params=pltpu.CompilerParams(dimension_semantics=("arbitrary",), vmem_limit_bytes=VMEM_LIMIT),
        name="pool",
    )(a, w_bd, scale)


def _softmax_pv(scores, values):
    m = scores[0].max(axis=-1, keepdims=True)
    for s in scores[1:]:
        m = jnp.maximum(m, s.max(axis=-1, keepdims=True))
    ps = [jnp.exp(s - m) for s in scores]
    l = ps[0].sum(axis=-1, keepdims=True)
    for p in ps[1:]:
        l = l + p.sum(axis=-1, keepdims=True)
    r = 1.0 / l
    o = _dot((ps[0] * r).astype(BF16), values[0])
    for p, v in zip(ps[1:], values[1:]):
        o = o + _dot((p * r).astype(BF16), v)
    return o


def _lane_halves():
    lane = lax.broadcasted_iota(jnp.int32, (1, LANES), 1)
    return lane < HEAD_DIM, lane >= HEAD_DIM


def _keep(x, mask):
    return jnp.where(mask, x, 0.0).astype(BF16)


def _gqa_variants(x):
    lo, hi = _lane_halves()
    xs = pltpu.roll(x, HEAD_DIM, axis=1)
    nat_lo, nat_hi = _keep(x, lo), _keep(x, hi)
    sw_lo, sw_hi = _keep(xs, lo), _keep(xs, hi)
    return ((nat_lo, sw_hi), (nat_lo, nat_hi), (sw_lo, nat_hi))


def _mha_variants(x):
    lo, hi = _lane_halves()
    out = []
    for j in range(x.shape[1] // LANES):
        xc = x[:, j * LANES:(j + 1) * LANES]
        out.append((_keep(xc, lo), _keep(xc, hi)))
    return tuple(out)


def _attend_cols(q, k_vars, v_vars, extra_k=None, extra_v=None, bias=None):
    cols = []
    for j in range(q.shape[1] // LANES):
        qc = q[:, j * LANES:(j + 1) * LANES]
        o = None
        for hh in range(2):
            s = _dot_nt(qc, k_vars[j][hh])
            if bias is not None:
                s = s + bias[j][hh]
            scores, values = [s], [v_vars[j][hh]]
            if extra_k is not None:
                scores.append(_dot_nt(qc, extra_k[j][hh]))
                values.append(extra_v[j][hh])
            oh = _softmax_pv(scores, values)
            o = oh if o is None else o + oh
        cols.append(o)
    return jnp.concatenate(cols, axis=1)


def _attn_ctx_kernel(qb_ref, kb_ref, vb_ref, qc_ref, kc_ref, vc_ref, *refs):
    yb_ref, yc_ref, kbt_ref, vbt_ref, kct_ref, vct_ref = refs[-6:]
    kb, vb, kc, vc = kb_ref[...], vb_ref[...], kc_ref[...], vc_ref[...]
    yb = _attend_cols(qb_ref[...], _gqa_variants(kb), _gqa_variants(vb))
    yb_ref[...] = yb.astype(BF16)
    yc = _attend_cols(qc_ref[...], _mha_variants(kc), _mha_variants(vc))
    yc_ref[...] = yc.astype(BF16)
    kbt_ref[0, 0] = kb.T
    vbt_ref[0, 0] = vb.T
    kct_ref[0, 0] = kc.T
    vct_ref[0, 0] = vc.T


CACHE_WIDTHS = (KB_W, KB_W, C_W, C_W)


def _attn_ctx(layer, qb, kb, vb, qc, kc, vc, caches):
    row = lambda w: pl.BlockSpec((SEQ, w), lambda i: (i, 0))
    cache = lambda w: pl.BlockSpec((1, 1, w, SEQ), lambda i: (i, layer, 0, 0))
    return pl.pallas_call(
        _attn_ctx_kernel,
        grid=(BATCH,),
        in_specs=[row(QB_W), row(KB_W), row(KB_W), row(C_W), row(C_W), row(C_W)]
                 + [pl.BlockSpec(memory_space=pl.ANY)] * len(caches),
        out_specs=[row(QB_W), row(C_W)] + [cache(w) for w in CACHE_WIDTHS],
        out_shape=[jax.ShapeDtypeStruct((T_CTX, QB_W), BF16), jax.ShapeDtypeStruct((T_CTX, C_W), BF16)]
                  + [jax.ShapeDtypeStruct((BATCH, DEPTH, w, SEQ), F32) for w in CACHE_WIDTHS],
        input_output_aliases={6 + k: 2 + k for k in range(len(caches))},
        compiler_params=pltpu.CompilerParams(dimension_semantics=("arbitrary",), vmem_limit_bytes=VMEM_LIMIT),
        name="attn_ctx",
    )(qb, kb, vb, qc, kc, vc, *caches)


def _attn_latb_kernel(q_ref, k_ref, v_ref, ck_ref, cv_ref, y_ref):
    y = _attend_cols(q_ref[...], _gqa_variants(ck_ref[0, 0]), _gqa_variants(cv_ref[0, 0]),
                     extra_k=_gqa_variants(k_ref[...]), extra_v=_gqa_variants(v_ref[...]))
    y_ref[...] = y.astype(BF16)


def _attn_latb(layer, qb, kb, vb, cache_k, cache_v):
    ctx_h = T_CTX // HALF
    ctx_r = T_CTX // DEC_SEQ
    cache = pl.BlockSpec((1, 1, PAST_LEN, KB_W), lambda b, s: (b, layer, 0, 0))
    own = pl.BlockSpec((DEC_SEQ, KB_W), lambda b, s: (ctx_r + b, 0))
    return pl.pallas_call(
        _attn_latb_kernel,
        grid=(DEC_BATCH, DEC_SEQ // HALF),
        in_specs=[pl.BlockSpec((HALF, QB_W), lambda b, s: (ctx_h + 2 * b + s, 0)), own, own, cache, cache],
        out_specs=pl.BlockSpec((HALF, QB_W), lambda b, s: (2 * b + s, 0)),
        out_shape=jax.ShapeDtypeStruct((T_LAT, QB_W), BF16),
        compiler_params=pltpu.CompilerParams(
            dimension_semantics=("arbitrary", "arbitrary"), vmem_limit_bytes=VMEM_LIMIT),
        name="attn_latb",
    )(qb, kb, vb, cache_k, cache_v)


def _natten_window(s, i):
    r = (HALF // GRID_W) * s + i
    rs = min(max(r - WIN_R // 2, 0), GRID_ROWS - WIN_R)
    return r, rs


def _natten_kernel(q_ref, k_ref, v_ref, ck_ref, cv_ref, blk_ref, y_ref, bias_ref):
    s = pl.program_id(1)

    for sv in range(2):
        @pl.when((pl.program_id(2) == 0) & (s == sv))
        def _():
            lo_half, _ = _lane_halves()
            masked = jnp.full((GRID_W, LANES), NEG, F32)
            for hh in range(2):
                for i in range(HALF // GRID_W):
                    r, rs = _natten_window(sv, i)
                    for jp in range(NAT_KEYS // LANES):
                        pair = []
                        for j in (2 * jp, 2 * jp + 1):
                            rk = 4 * sv + j
                            pair.append(blk_ref[0, hh, rk - r + WIN_R - 1] if rs <= rk < rs + WIN_R else masked)
                        bias_ref[hh, i * GRID_W:(i + 1) * GRID_W, jp * LANES:(jp + 1) * LANES] = (
                            jnp.where(lo_half, pair[0], pair[1]))

    start = pl.multiple_of(s * (4 * GRID_W), 4 * GRID_W)
    k = _mha_variants(k_ref[pl.ds(start, NAT_KEYS), :])
    v = _mha_variants(v_ref[pl.ds(start, NAT_KEYS), :])
    ck = _mha_variants(ck_ref[0, 0])
    cv = _mha_variants(cv_ref[0, 0])
    bias = ((bias_ref[0], bias_ref[1]),)
    y = _attend_cols(q_ref[...], k, v, extra_k=ck, extra_v=cv, bias=bias)
    y_ref[...] = y.astype(BF16)


def _natten(layer, qc, kc, vc, cache_k, cache_v, blocks):
    ctx_h = T_CTX // HALF
    ctx_r = T_CTX // DEC_SEQ
    cache = pl.BlockSpec((1, 1, PAST_LEN, LANES), lambda j, s, b: (b, layer, 0, j))
    own = pl.BlockSpec((DEC_SEQ, LANES), lambda j, s, b: (ctx_r + b, j))
    return pl.pallas_call(
        _natten_kernel,
        grid=(C_W // LANES, DEC_SEQ // HALF, DEC_BATCH),
        in_specs=[pl.BlockSpec((HALF, LANES), lambda j, s, b: (ctx_h + 2 * b + s, j)), own, own, cache, cache,
                  pl.BlockSpec((1, 2, 2 * WIN_R - 1, GRID_W, LANES), lambda j, s, b: (j, 0, 0, 0, 0))],
        out_specs=pl.BlockSpec((HALF, LANES), lambda j, s, b: (2 * b + s, j)),
        out_shape=jax.ShapeDtypeStruct((T_LAT, C_W), BF16),
        scratch_shapes=[pltpu.VMEM((2, HALF, NAT_KEYS), F32)],
        compiler_params=pltpu.CompilerParams(
            dimension_semantics=("arbitrary", "arbitrary", "arbitrary"), vmem_limit_bytes=VMEM_LIMIT),
        name="natten",
    )(qc, kc, vc, cache_k, cache_v, blocks)


def _natten_blocks(rpb):
    qcol = np.arange(GRID_W)[:, None]
    kcol = np.arange(GRID_W)[None, :]
    cs = np.clip(qcol - WIN_C // 2, 0, GRID_W - WIN_C)
    col_ok = (kcol >= cs) & (kcol < cs + WIN_C)
    dc = np.clip(kcol - qcol, -(WIN_C - 1), WIN_C - 1) + (WIN_C - 1)
    sel_c = (dc[:, :, None] == np.arange(2 * WIN_C - 1)).astype(np.float32)
    blk = jnp.einsum("hdm,qkm->hdqk", rpb, jnp.asarray(sel_c), precision=lax.Precision.HIGHEST)
    blk = jnp.where(jnp.asarray(col_ok), blk, NEG)
    blk = jnp.concatenate([blk, blk], axis=-1)
    return blk.reshape(H_C // 2, 2, 2 * WIN_R - 1, GRID_W, LANES)


def _route(lg):
    lane = lax.broadcasted_iota(jnp.int32, lg.shape, 1).astype(F32)
    low = jnp.float32(-3.0e38)
    far = jnp.float32(ROUTER_LANES)
    is_g = lane < N_GROUPS
    gmax = jnp.where(is_g, lg, low).max(axis=-1, keepdims=True)
    gsel = jnp.where(is_g & (lg == gmax), lane, far).min(axis=-1, keepdims=True)
    pg_sel = 1.0 / jnp.where(is_g, jnp.exp(lg - gmax), 0.0).sum(axis=-1, keepdims=True)
    e0 = EXPERT_LANE0 + EXPERTS_PER_GROUP * gsel
    in_g = (lane >= e0) & (lane < e0 + EXPERTS_PER_GROUP)
    m1 = jnp.where(in_g, lg, low).max(axis=-1, keepdims=True)
    i1 = jnp.where(in_g & (lg == m1), lane, far).min(axis=-1, keepdims=True)
    rest = in_g & (lane != i1)
    m2 = jnp.where(rest, lg, low).max(axis=-1, keepdims=True)
    i2 = jnp.where(rest & (lg == m2), lane, far).min(axis=-1, keepdims=True)
    t = jnp.exp(m2 - m1)
    ssum = 1.0 + t
    w1 = (1.0 / ssum) * pg_sel
    w2 = (t / ssum) * pg_sel
    return i1 - EXPERT_LANE0, i2 - EXPERT_LANE0, w1, w2


def _rows_to_groups(x, stage_ref, out_ref, n_rows):
    stride = n_rows + 8
    for c in range(GROUP_ROWS):
        stage_ref[c * stride:c * stride + n_rows, :] = x[:, c * LANES:(c + 1) * LANES]
    for j in range(n_rows):
        out_ref[GROUP_ROWS * j:GROUP_ROWS * (j + 1), :] = stage_ref[pl.ds(j, GROUP_ROWS, stride=stride), :]


def _tail_kernel(xc_ref, xl_ref, ya_ref, ybc_ref, ybl_ref, ycc_ref, ycl_ref, w_ref, g1_ref, sh2_ref, sc2_ref,
                 lg_ref, lb_ref, wr_ref, br_ref, x1_ref, h2g_ref, ids_ref, wts_ref, wbf_ref, stage_ref):
    @pl.when(pl.program_id(0) == 0)
    def _():
        wbf_ref[...] = w_ref[0].astype(BF16)

    o = (_dot(ya_ref[...], wbf_ref[0:POOL_WIDTH, :])
         + _dot(_split_rows(ybc_ref, ybl_ref), wbf_ref[POOL_WIDTH:POOL_WIDTH + QB_W, :])
         + _dot(_split_rows(ycc_ref, ycl_ref), wbf_ref[POOL_WIDTH + QB_W:, :]))
    x1 = _layer_norm(ALPHA * _split_rows(xc_ref, xl_ref) + g1_ref[0] * o, lg_ref[0], lb_ref[0])
    x1_ref[...] = x1
    h2 = x1 * (1.0 + sc2_ref[0]) + sh2_ref[0]
    hh, hl = _split_bf16(h2)
    wh, wl = _split_bf16(wr_ref[0])
    lg = _dot(hh, wh) + _dot(hl, wh) + _dot(hh, wl) + br_ref[0]
    i1, i2, w1, w2 = _route(lg)
    lane = lax.broadcasted_iota(jnp.int32, lg.shape, 1)
    ids = jnp.where(lane == 0, i1, jnp.where(lane == 1, i2, 0.0))
    ids_ref[0] = ids.T[0:8, :].astype(jnp.int32)
    wts = jnp.where(lane == 0, w1, jnp.where(lane == 1, w2, 0.0))
    wts_ref[0] = wts.T[0:8, :]
    _rows_to_groups(h2, stage_ref, h2g_ref, TT)


def _tail(layer, x_ctx, x_lat, ya, yb_ctx, yb_lat, yc_ctx, yc_lat, w_out, mod, ln_g, ln_b, wr, br):
    row = lambda w: pl.BlockSpec((TT, w), lambda i: (i, 0))
    return pl.pallas_call(
        _tail_kernel,
        grid=(T_ALL // TT,),
        in_specs=[*_split_row_specs(D_MODEL), row(POOL_WIDTH), *_split_row_specs(QB_W), *_split_row_specs(C_W),
                  _layer_spec(layer, (D_MODEL, D_MODEL)),
                  _mod_spec(layer, 2, TT), _mod_spec(layer, 3, TT), _mod_spec(layer, 4, TT),
                  _layer_spec(layer, (1, D_MODEL)), _layer_spec(layer, (1, D_MODEL)),
                  _layer_spec(layer, (D_MODEL, ROUTER_LANES)), _layer_spec(layer, (1, ROUTER_LANES))],
        out_specs=[row(D_MODEL), pl.BlockSpec((TT * GROUP_ROWS, LANES), lambda i: (i, 0)),
                   pl.BlockSpec((1, 8, TT), lambda i: (i, 0, 0)), pl.BlockSpec((1, 8, TT), lambda i: (i, 0, 0))],
        out_shape=[jax.ShapeDtypeStruct((T_ALL, D_MODEL), F32),
                   jax.ShapeDtypeStruct((T_ALL * GROUP_ROWS, LANES), F32),
                   jax.ShapeDtypeStruct((T_ALL // TT, 8, TT), jnp.int32),
                   jax.ShapeDtypeStruct((T_ALL // TT, 8, TT), F32)],
        scratch_shapes=[pltpu.VMEM((D_MODEL, D_MODEL), BF16),
                        pltpu.VMEM((GROUP_ROWS * (TT + 8), LANES), F32)],
        compiler_params=pltpu.CompilerParams(dimension_semantics=("arbitrary",), vmem_limit_bytes=VMEM_LIMIT),
        name="tail",
    )(x_ctx, x_lat, ya, yb_ctx, yb_lat, yc_ctx, yc_lat, w_out, mod, mod, mod, ln_g, ln_b, wr, br)


def _plan_kernel(eid_ref, pos_ref, tile_ref):
    e = eid_ref[0]
    r = lax.broadcasted_iota(jnp.int32, (LANES, LANES), 0)
    c = lax.broadcasted_iota(jnp.int32, (LANES, LANES), 1)
    upper = (r <= c).astype(BF16)
    rows = e.shape[0]
    lower = (lax.broadcasted_iota(jnp.int32, (rows, rows), 1)
             < lax.broadcasted_iota(jnp.int32, (rows, rows), 0)).astype(BF16)
    tile_start = (lax.broadcasted_iota(jnp.int32, (1, LANES), 1) * TMX).astype(F32)
    lane = lax.broadcasted_iota(jnp.int32, (1, LANES), 1)
    pos = jnp.zeros(e.shape, F32)
    base = jnp.zeros((1, LANES), F32)
    n_before = jnp.zeros((1, LANES), F32)
    valid_end = jnp.zeros((1, LANES), F32)
    pad_end = jnp.zeros((1, LANES), F32)
    for ex in range(N_EXPERTS):
        m = e == ex
        incl = _dot(m.astype(BF16), upper)
        row_tot = jnp.broadcast_to(incl[:, LANES - 1:LANES], incl.shape)
        row_off = _dot(lower, row_tot.astype(BF16))
        cnt = jnp.sum(row_tot, axis=0, keepdims=True)
        pos = jnp.where(m, base + row_off + incl - 1.0, pos)
        valid_end = jnp.where(lane == ex, base + cnt, valid_end)
        base = base + jnp.ceil(cnt * (1.0 / TMX)) * TMX
        pad_end = jnp.where(lane == ex, base, pad_end)
        n_before = n_before + (tile_start >= base).astype(F32)
    pos_ref[0] = pos.astype(jnp.int32)
    sub = lax.broadcasted_iota(jnp.int32, (8, LANES), 0)
    rows8 = lambda v: jnp.broadcast_to(v, (8, LANES))
    table = jnp.where(sub == PLAN_TILE_EXPERT, rows8(jnp.minimum(n_before, N_EXPERTS - 1.0)),
                      jnp.where(sub == PLAN_N_TILES, rows8(base * (1.0 / TMX)),
                                jnp.where(sub == PLAN_VALID_END, rows8(valid_end),
                                          jnp.where(sub == PLAN_PAD_END, rows8(pad_end), 0.0))))
    tile_ref[0] = table.astype(jnp.int32)


def _plan(eid):
    rows = 2 * T_STREAM // LANES
    return pl.pallas_call(
        _plan_kernel,
        grid=(N_STREAMS,),
        in_specs=[pl.BlockSpec((1, rows, LANES), lambda s: (s, 0, 0))],
        out_specs=[pl.BlockSpec((1, rows, LANES), lambda s: (s, 0, 0)), pl.BlockSpec((1, 8, LANES), lambda s: (s, 0, 0))],
        out_shape=[jax.ShapeDtypeStruct((N_STREAMS, rows, LANES), jnp.int32),
                   jax.ShapeDtypeStruct((N_STREAMS, 8, LANES), jnp.int32)],
        compiler_params=pltpu.CompilerParams(dimension_semantics=("arbitrary",), vmem_limit_bytes=VMEM_LIMIT),
        name="plan",
    )(eid)


def _moe_kernel(pos_ref, wpair_ref, plan_ref, h2g_hbm, wg_ref, wu_ref, wd_ref, y_hbm,
                h2v, yv, off, wcol, xg0, xg1, yg0, yg1, wgb, wub, wdb, sem):
    s = pl.program_id(0)
    e = pl.program_id(1)
    n_pairs = 2 * T_STREAM
    spare_off = T_STREAM * GROUP_ROWS
    plan = lambda row, idx: plan_ref[(s * 8 + row) * LANES + idx]
    n_tiles = plan(PLAN_N_TILES, 0)
    last = n_tiles - 1

    xgs = (xg0, xg1)
    ygs = (yg0, yg1)

    def gather(tile, dst):
        base = tile * TMX
        for j in range(TMX):
            src = pl.multiple_of(off[base + j], GROUP_ROWS)
            dst[pl.ds(j, GROUP_ROWS, stride=XSTRIDE), :] = h2v[pl.ds(src, GROUP_ROWS), :]

    def h2_copy():
        return pltpu.make_async_copy(
            h2g_hbm.at[pl.ds(pl.multiple_of(s * (T_STREAM * GROUP_ROWS), 8), T_STREAM * GROUP_ROWS)],
            h2v.at[pl.ds(0, T_STREAM * GROUP_ROWS)], sem.at[0])

    def y_copy():
        return pltpu.make_async_copy(yv.at[pl.ds(0, T_STREAM * GROUP_ROWS)], y_hbm.at[s], sem.at[1])

    @pl.when(e == 0)
    def _():
        h2_copy().start()
        yv[...] = jnp.zeros_like(yv)
        h2v[spare_off:spare_off + GROUP_ROWS, :] = jnp.zeros((GROUP_ROWS, LANES), F32)

        def pad_expert(ex, carry):
            def pad_row(r, c):
                off[r] = spare_off
                wcol[pl.ds(r, 1), :] = jnp.zeros((1, LANES), F32)
                return c
            return lax.fori_loop(plan(PLAN_VALID_END, ex), plan(PLAN_PAD_END, ex), pad_row, carry)
        lax.fori_loop(0, N_EXPERTS, pad_expert, 0)

        def place(i, carry):
            for u in range(PLACE_UNROLL):
                tok = i * PLACE_UNROLL + u
                for slot in range(2):
                    p = s * n_pairs + slot * T_STREAM + tok
                    r = pos_ref[p]
                    off[r] = tok * GROUP_ROWS
                    wcol[pl.ds(r, 1), :] = jnp.full((1, LANES), wpair_ref[p], F32)
            return carry
        lax.fori_loop(0, T_STREAM // PLACE_UNROLL, place, 0)
        yg1[...] = jnp.zeros_like(yg1)
        h2_copy().wait()
        gather(0, xg0)

    t_lo = jnp.where(e == 0, 0, plan(PLAN_PAD_END, jnp.maximum(e - 1, 0))) // TMX
    t_hi = plan(PLAN_PAD_END, e) // TMX

    @pl.when(t_hi > t_lo)
    def _():
        wgb[...] = wg_ref[0, 0].astype(BF16)
        wub[...] = wu_ref[0, 0].astype(BF16)
        wdb[...] = wd_ref[0, 0].astype(BF16)

    def add_tile(tile, done):
        base = tile * TMX
        for j0 in range(0, TMX, RMW_BATCH):
            updates = []
            for j in range(j0, j0 + RMW_BATCH):
                dst = pl.multiple_of(off[base + j], GROUP_ROWS)
                updates.append((dst, yv[pl.ds(dst, GROUP_ROWS), :] + done[pl.ds(j, GROUP_ROWS, stride=XSTRIDE), :]))
            for dst, val in updates:
                yv[pl.ds(dst, GROUP_ROWS), :] = val

    def step(t, cur, nxt, out, done):
        gather(jnp.minimum(t + 1, last), nxt)
        x = jnp.concatenate([cur[c * XSTRIDE:c * XSTRIDE + TMX, :] for c in range(GROUP_ROWS)], axis=1).astype(BF16)
        hg = _dot(x, wgb[...])
        hu = _dot(x, wub[...])
        gate = wcol[pl.ds(pl.multiple_of(t * TMX, TMX), TMX), :]
        act = hg * jax.nn.sigmoid(hg) * hu * jnp.concatenate([gate] * (D_EXPERT // LANES), axis=1)
        y = _dot(act.astype(BF16), wdb[...])
        for c in range(GROUP_ROWS):
            out[c * XSTRIDE:c * XSTRIDE + TMX, :] = y[:, c * LANES:(c + 1) * LANES]
        add_tile(jnp.maximum(t - 1, 0), done)

    def tile_body(t, carry):
        for par in range(2):
            @pl.when((t & 1) == par)
            def _():
                step(t, xgs[par], xgs[1 - par], ygs[par], ygs[1 - par])
        return carry
    lax.fori_loop(t_lo, t_hi, tile_body, 0)

    @pl.when(e == N_EXPERTS - 1)
    def _():
        for par in range(2):
            @pl.when((last & 1) == par)
            def _():
                add_tile(last, ygs[par])
        y_copy().start()
        y_copy().wait()


def _moe(layer, pos, wpair, plan, h2g, w_gate, w_up, w_down):
    wspec = lambda shape: pl.BlockSpec((1,) + shape, lambda s, e, pos, wpair, plan: (layer, e, 0, 0))
    return pl.pallas_call(
        _moe_kernel,
        grid_spec=pltpu.PrefetchScalarGridSpec(
            num_scalar_prefetch=3,
            grid=(N_STREAMS, N_EXPERTS),
            in_specs=[pl.BlockSpec(memory_space=pl.ANY),
                      wspec((1, D_MODEL, D_EXPERT)), wspec((1, D_MODEL, D_EXPERT)), wspec((1, D_EXPERT, D_MODEL))],
            out_specs=pl.BlockSpec(memory_space=pl.ANY),
            scratch_shapes=[pltpu.VMEM(((T_STREAM + 1) * GROUP_ROWS, LANES), F32),
                            pltpu.VMEM(((T_STREAM + 1) * GROUP_ROWS, LANES), F32),
                            pltpu.SMEM((P_MAX,), jnp.int32),
                            pltpu.VMEM((P_MAX, LANES), F32),
                            pltpu.VMEM((GROUP_ROWS * XSTRIDE, LANES), F32),
                            pltpu.VMEM((GROUP_ROWS * XSTRIDE, LANES), F32),
                            pltpu.VMEM((GROUP_ROWS * XSTRIDE, LANES), F32),
                            pltpu.VMEM((GROUP_ROWS * XSTRIDE, LANES), F32),
                            pltpu.VMEM((D_MODEL, D_EXPERT), BF16), pltpu.VMEM((D_MODEL, D_EXPERT), BF16),
                            pltpu.VMEM((D_EXPERT, D_MODEL), BF16),
                            pltpu.SemaphoreType.DMA((2,))]),
        out_shape=jax.ShapeDtypeStruct((N_STREAMS, T_STREAM * GROUP_ROWS, LANES), F32),
        compiler_params=pltpu.CompilerParams(
            dimension_semantics=("arbitrary", "arbitrary"), vmem_limit_bytes=VMEM_LIMIT),
        name="moe",
    )(pos, wpair, plan, h2g, w_gate, w_up, w_down)


def _groups_to_rows(src_ref, stage_ref, n_rows):
    stride = n_rows + 8
    for j in range(n_rows):
        stage_ref[pl.ds(j, GROUP_ROWS, stride=stride), :] = src_ref[GROUP_ROWS * j:GROUP_ROWS * (j + 1), :]
    return jnp.concatenate([stage_ref[c * stride:c * stride + n_rows, :] for c in range(GROUP_ROWS)], axis=1)


def _final_kernel(x1_ref, y_ref, g2_ref, lg_ref, lb_ref, o_ref, stage_ref):
    moe = _groups_to_rows(y_ref, stage_ref, TT)
    o_ref[...] = _layer_norm(ALPHA * x1_ref[...] + g2_ref[0] * moe, lg_ref[0], lb_ref[0])


def _final(layer, stream, x1, y, mod, ln_g, ln_b):
    first = stream * (T_STREAM // TT)
    g2 = pl.BlockSpec(
        (1, 1, D_MODEL), lambda i: ((layer * MOD_ROWS + _mod_row_of_tile(first + i, TT)) * 6 + 5, 0, 0))
    return pl.pallas_call(
        _final_kernel,
        grid=(T_STREAM // TT,),
        in_specs=[pl.BlockSpec((TT, D_MODEL), lambda i: (first + i, 0)),
                  pl.BlockSpec((TT * GROUP_ROWS, LANES), lambda i: (first + i, 0)),
                  g2, _layer_spec(layer, (1, D_MODEL)), _layer_spec(layer, (1, D_MODEL))],
        out_specs=pl.BlockSpec((TT, D_MODEL), lambda i: (i, 0)),
        out_shape=jax.ShapeDtypeStruct((T_STREAM, D_MODEL), F32),
        scratch_shapes=[pltpu.VMEM((GROUP_ROWS * (TT + 8), LANES), F32)],
        compiler_params=pltpu.CompilerParams(dimension_semantics=("arbitrary",), vmem_limit_bytes=VMEM_LIMIT),
        name="final",
    )(x1, y, mod, ln_g, ln_b)


def _rope_tables():
    t = np.arange(DEC_SEQ)
    pos = np.stack([t // GRID_W, t % GRID_W], axis=1).astype(np.float32)
    nf = HEAD_DIM // 4
    inv = jnp.asarray(ROPE_THETA, F32) ** (-jnp.arange(nf, dtype=F32) / nf)
    d = np.arange(LANES) % HEAD_DIM
    which = d // (HEAD_DIM // 2)
    ang = jnp.asarray(pos)[:, which] * inv[d % nf][None, :]
    sign = np.where((d % 32) < 16, -1.0, 1.0).astype(np.float32)
    cos = jnp.concatenate([jnp.cos(ang), jnp.ones((TT, LANES), F32)], axis=0)
    sin = jnp.concatenate([jnp.sin(ang) * sign[None, :], jnp.zeros((TT, LANES), F32)], axis=0)
    return cos, sin


def _block_ones():
    h = np.arange(QB_W) // HEAD_DIM
    return jnp.asarray((h[:, None] == h[None, :]).astype(np.float32), dtype=BF16)


def _pool_block_diag(pool_w):
    out = jnp.zeros((POOL_WIDTH, POOL_WIDTH), F32)
    for g in range(4):
        out = out.at[64 * g:64 * g + 64, 64 * g:64 * g + 64].set(pool_w[g])
    return out


def kernel(x_prompt, x_sample, cache_b_k, cache_b_v, cache_c_k, cache_c_v, c, c_ctx, w_ada, b_ada, w_in, w_out,
           pool_w, pool_scale, q_norm, k_norm, rpb, ln1_g, ln1_b, ln2_g, ln2_b, router_g, router_g_b, router_e,
           router_e_b, w_gate, w_up, w_down):
    x_ctx = x_prompt.reshape(T_CTX, D_MODEL)
    x_lat = x_sample.reshape(T_LAT, D_MODEL)
    c8 = jnp.concatenate([c, c_ctx[None], jnp.zeros((MOD_ROWS - DEC_BATCH - 1, D_MODEL), F32)], axis=0)
    mod = _adaln(c8, w_ada, b_ada).reshape(DEPTH * MOD_ROWS * 6, 1, D_MODEL)

    cos_t, sin_t = _rope_tables()
    ones_bd = _block_ones()
    cbk = cache_b_k.reshape(DEC_BATCH, DEPTH, PAST_LEN, KB_W)
    cbv = cache_b_v.reshape(DEC_BATCH, DEPTH, PAST_LEN, KB_W)
    cck = cache_c_k.reshape(DEC_BATCH, DEPTH, PAST_LEN, C_W)
    ccv = cache_c_v.reshape(DEC_BATCH, DEPTH, PAST_LEN, C_W)
    pad = jnp.zeros((DEPTH, D_MODEL, ROUTER_LANES - N_GROUPS - N_EXPERTS), F32)
    wr = jnp.concatenate([router_g, router_e, pad], axis=2)
    br = jnp.concatenate([router_g_b, router_e_b, pad[:, 0]], axis=1)[:, None]
    per_layer = lambda p: p[:, None]

    caches = [jnp.zeros((BATCH, DEPTH, w, SEQ), F32) for w in CACHE_WIDTHS]
    for l in range(DEPTH):
        qn = jnp.tile(q_norm[l], H_B)[None]
        kn = jnp.tile(k_norm[l], KV_B)[None]
        a, qb, kb, vb, qc, kc, vc = _proj(l, x_ctx, x_lat, mod, w_in, ones_bd, qn, kn, cos_t, sin_t)
        ya = _pool(a, _pool_block_diag(pool_w[l]), pool_scale[l][None])
        yb_ctx, yc_ctx, *caches = _attn_ctx(l, qb, kb, vb, qc, kc, vc, caches)
        yb_lat = _attn_latb(l, qb, kb, vb, cbk, cbv)
        yc_lat = _natten(l, qc, kc, vc, cck, ccv, _natten_blocks(rpb[l]))
        x1, h2g, ids, wts = _tail(l, x_ctx, x_lat, ya, yb_ctx, yb_lat, yc_ctx, yc_lat, w_out, mod,
                                  per_layer(ln1_g), per_layer(ln1_b), wr, br)
        pairs = lambda a: a.reshape(N_STREAMS, T_STREAM // TT, 8, TT)[:, :, :2, :].transpose(0, 2, 1, 3)
        pos, tiles = _plan(pairs(ids).reshape(N_STREAMS, 2 * T_STREAM // LANES, LANES))
        y = _moe(l, pos.reshape(-1), pairs(wts).reshape(-1), tiles.reshape(-1), h2g, w_gate, w_up, w_down)
        y = y.reshape(T_ALL * GROUP_ROWS, LANES)
        x_ctx = _final(l, 0, x1, y, mod, per_layer(ln2_g), per_layer(ln2_b))
        x_lat = _final(l, 1, x1, y, mod, per_layer(ln2_g), per_layer(ln2_b))

    y_prompt = x_ctx.reshape(BATCH, SEQ, D_MODEL)
    y_sample = x_lat.reshape(DEC_BATCH, DEC_SEQ, D_MODEL)
    heads_last = lambda t, h: t.reshape(BATCH, DEPTH, h, HEAD_DIM, SEQ).transpose(0, 1, 4, 2, 3)
    new_bk, new_bv, new_ck, new_cv = caches
    return (y_prompt, y_sample, heads_last(new_bk, KV_B), heads_last(new_bv, KV_B),
            heads_last(new_ck, H_C), heads_last(new_cv, H_C))
```

```python
import functools

import numpy as np
import jax
import jax.numpy as jnp
from jax import lax
from jax.experimental import pallas as pl
from jax.experimental.pallas import tpu as pltpu

F32 = jnp.float32
BF16 = jnp.bfloat16

D_MODEL = 1024
BATCH = 16
SEQ = 256
DEPTH = 2
DEC_BATCH = 4
DEC_SEQ = 1024
PAST_LEN = 256
GRID_W = 64
GRID_ROWS = DEC_SEQ // GRID_W
HEAD_DIM = 64
POOL_WIDTH = 256
POOL_WINDOWS = (2, 4, 8, 16)
H_B = 6
KV_B = 2
H_C = 6
WIN_R = 8
WIN_C = 16
ROPE_THETA = 10000.0
QB_W = H_B * HEAD_DIM
KB_W = KV_B * HEAD_DIM
C_W = H_C * HEAD_DIM
PROJ_WIDTH = 2048
N_GROUPS = 4
EXPERTS_PER_GROUP = 8
N_EXPERTS = 32
D_EXPERT = 256
ALPHA = (2 * DEPTH) ** 0.25
LN_EPS = 1e-6
RMS_EPS = 1e-6
NEG = -1e30
ATTN_SCALE = HEAD_DIM ** -0.5

T_CTX = BATCH * SEQ
T_LAT = DEC_BATCH * DEC_SEQ
T_ALL = T_CTX + T_LAT

LANES = 128
ROUTER_LANES = 128
EXPERT_LANE0 = N_GROUPS
MOD_ROWS = 8
CTX_MOD_ROW = DEC_BATCH

TT = 512
TP = 1024
POOL_GAP = 16
N_STREAMS = 2
T_STREAM = T_ALL // N_STREAMS
TMX = 128
PLACE_UNROLL = 8
SLOT_PITCH = 2 * T_STREAM
P_MAX = 2 * T_STREAM + N_EXPERTS * TMX
NT_MAX = P_MAX // TMX
GROUP_ROWS = D_MODEL // LANES
XSTRIDE = TMX + 8
RMW_BATCH = 16
PLAN_TILE_EXPERT, PLAN_N_TILES, PLAN_VALID_END, PLAN_PAD_END = 0, 1, 2, 3
HALF = DEC_SEQ // 2
NAT_KEYS = 12 * GRID_W
VMEM_LIMIT = 56 * 1024 * 1024


def _dot(a, b):
    return jnp.dot(a, b, preferred_element_type=F32)


def _dot_nt(a, b):
    return lax.dot_general(a, b, (((1,), (1,)), ((), ())), preferred_element_type=F32)


def _split_bf16(x):
    hi = x.astype(BF16)
    lo = (x - hi.astype(F32)).astype(BF16)
    return hi, lo


def _layer_norm(y, g, b):
    mu = jnp.mean(y, axis=-1, keepdims=True)
    var = jnp.mean(jnp.square(y - mu), axis=-1, keepdims=True)
    return (y - mu) * lax.rsqrt(var + LN_EPS) * g + b


def _adaln_kernel(c_ref, w_ref, b_ref, o_ref):
    c = c_ref[...]
    s = (c * jax.nn.sigmoid(c)).astype(BF16)
    o_ref[0] = _dot(s, w_ref[0].astype(BF16)) + b_ref[0]


def _adaln(c8, w_ada, b_ada):
    tn = 1536
    n = w_ada.shape[-1]
    return pl.pallas_call(
        _adaln_kernel,
        grid=(DEPTH, n // tn),
        in_specs=[
            pl.BlockSpec((MOD_ROWS, D_MODEL), lambda l, j: (0, 0)),
            pl.BlockSpec((1, D_MODEL, tn), lambda l, j: (l, 0, j)),
            pl.BlockSpec((1, 1, tn), lambda l, j: (l, 0, j)),
        ],
        out_specs=pl.BlockSpec((1, MOD_ROWS, tn), lambda l, j: (l, 0, j)),
        out_shape=jax.ShapeDtypeStruct((DEPTH, MOD_ROWS, n), F32),
        compiler_params=pltpu.CompilerParams(
            dimension_semantics=("arbitrary", "arbitrary"), vmem_limit_bytes=VMEM_LIMIT),
        name="adaln",
    )(c8, w_ada, b_ada.reshape(DEPTH, 1, n))


def _mod_row_of_tile(i, tile):
    n_ctx = T_CTX // tile
    per_req = DEC_SEQ // tile
    return jnp.where(i < n_ctx, CTX_MOD_ROW, (i - n_ctx) // per_req)


def _mod_spec(layer, chunk, tile):
    return pl.BlockSpec(
        (1, 1, D_MODEL), lambda i: ((layer * MOD_ROWS + _mod_row_of_tile(i, tile)) * 6 + chunk, 0, 0))


def _layer_spec(layer, shape):
    return pl.BlockSpec((1,) + shape, lambda *_: (layer,) + (0,) * len(shape))


N_CTX_TILES = T_CTX // TT


def _split_row_specs(width):
    return (pl.BlockSpec((TT, width), lambda i: (jnp.minimum(i, N_CTX_TILES - 1), 0)),
            pl.BlockSpec((TT, width), lambda i: (jnp.maximum(i - N_CTX_TILES, 0), 0)))


def _split_rows(ctx_ref, lat_ref):
    return jnp.where(pl.program_id(0) < N_CTX_TILES, ctx_ref[...], lat_ref[...])


def _rms_norm_heads(x, ones_bd, w):
    hi, lo = _split_bf16(x * x)
    ssq = _dot(hi, ones_bd) + _dot(lo, ones_bd)
    return x * lax.rsqrt(ssq * (1.0 / HEAD_DIM) + RMS_EPS) * w


def _rope_cols(x, cos, sin, first16):
    cols = []
    for j in range(x.shape[1] // LANES):
        xc = x[:, j * LANES:(j + 1) * LANES]
        partner = jnp.where(first16, pltpu.roll(xc, LANES - 16, axis=1), pltpu.roll(xc, 16, axis=1))
        cols.append(xc * cos + partner * sin)
    return jnp.concatenate(cols, axis=1) if len(cols) > 1 else cols[0]


def _proj_kernel(xc_ref, xl_ref, sh_ref, sc_ref, w_ref, ones_ref, qn_ref, kn_ref, cos_ref, sin_ref,
                 a_ref, qb_ref, kb_ref, vb_ref, qc_ref, kc_ref, vc_ref, wbf_ref):
    @pl.when(pl.program_id(0) == 0)
    def _():
        wbf_ref[...] = w_ref[0].astype(BF16)

    h = _split_rows(xc_ref, xl_ref) * (1.0 + sc_ref[0]) + sh_ref[0]
    p = _dot(h.astype(BF16), wbf_ref[...])
    o = 0
    a_ref[...] = p[:, o:o + POOL_WIDTH]; o += POOL_WIDTH
    qb = p[:, o:o + QB_W]; o += QB_W
    kb = p[:, o:o + KB_W]; o += KB_W
    vb_ref[...] = p[:, o:o + KB_W]; o += KB_W
    qc_ref[...] = (p[:, o:o + C_W] * ATTN_SCALE).astype(BF16); o += C_W
    kc_ref[...] = p[:, o:o + C_W]; o += C_W
    vc_ref[...] = p[:, o:o + C_W]

    ones_bd = ones_ref[...]
    cos = cos_ref[...]
    sin = sin_ref[...]
    lane = lax.broadcasted_iota(jnp.int32, (1, LANES), 1)
    first16 = (lane & 31) < 16
    qb = _rope_cols(_rms_norm_heads(qb, ones_bd, qn_ref[...]), cos, sin, first16)
    kb = _rope_cols(_rms_norm_heads(kb, ones_bd[:KB_W, :KB_W], kn_ref[...]), cos, sin, first16)
    qb_ref[...] = (qb * ATTN_SCALE).astype(BF16)
    kb_ref[...] = kb


def _proj(layer, x_ctx, x_lat, mod, w_in, ones_bd, qn, kn, cos_t, sin_t):
    n_tiles = T_ALL // TT
    n_ctx = N_CTX_TILES
    per_req = DEC_SEQ // TT

    def rope_idx(i):
        return (jnp.where(i < n_ctx, per_req, (i - n_ctx) % per_req), 0)

    row = lambda w: pl.BlockSpec((TT, w), lambda i: (i, 0))
    const = lambda s: pl.BlockSpec(s, lambda i: (0,) * len(s))
    widths = (POOL_WIDTH, QB_W, KB_W, KB_W, C_W, C_W, C_W)
    dtypes = (F32, BF16, F32, F32, BF16, F32, F32)
    return pl.pallas_call(
        _proj_kernel,
        grid=(n_tiles,),
        in_specs=[*_split_row_specs(D_MODEL), _mod_spec(layer, 0, TT), _mod_spec(layer, 1, TT),
                  _layer_spec(layer, (D_MODEL, PROJ_WIDTH)),
                  const((QB_W, QB_W)), const((1, QB_W)), const((1, KB_W)),
                  pl.BlockSpec((TT, LANES), rope_idx), pl.BlockSpec((TT, LANES), rope_idx)],
        out_specs=[row(w) for w in widths],
        out_shape=[jax.ShapeDtypeStruct((T_ALL, w), d) for w, d in zip(widths, dtypes)],
        scratch_shapes=[pltpu.VMEM((D_MODEL, PROJ_WIDTH), BF16)],
        compiler_params=pltpu.CompilerParams(dimension_semantics=("arbitrary",), vmem_limit_bytes=VMEM_LIMIT),
        name="proj",
    )(x_ctx, x_lat, mod, mod, w_in, ones_bd, qn, kn, cos_t, sin_t)


def _pool_windows(p):
    n = p.shape[0]
    sh = lambda x, k: pltpu.roll(x, (n - k) % n, axis=0)
    w2 = p + sh(p, -1)
    w4 = sh(w2, -1) + sh(w2, 1)
    w8 = sh(w4, -2) + sh(w4, 2)
    w16 = sh(w8, -4) + sh(w8, 4)
    g = lax.broadcasted_iota(jnp.int32, p.shape, 1) >> 6
    return jnp.where(g == 0, w2, jnp.where(g == 1, w4, jnp.where(g == 2, w8, w16)))


def _pool_kernel(a_ref, w_ref, scale_ref, o_ref, pad_ref):
    a = a_ref[...]

    def mix(csum, seq_len):
        row = lax.broadcasted_iota(jnp.int32, a.shape, 0)
        g = lax.broadcasted_iota(jnp.int32, a.shape, 1) >> 6
        t = row & (seq_len - 1)
        half = jnp.where(g == 0, 1, jnp.where(g == 1, 2, jnp.where(g == 2, 4, 8)))
        cnt = jnp.minimum(t + half, seq_len) - jnp.maximum(t - half, 0)
        pooled = csum / cnt.astype(F32) - a
        y = _dot(pooled.astype(BF16), w_ref[...].astype(BF16)) * scale_ref[...]
        o_ref[...] = y.astype(BF16)

    def padded_sums(seq_len):
        pitch = seq_len + POOL_GAP
        n_req = TP // seq_len
        n = n_req * pitch
        pad_ref[...] = jnp.zeros_like(pad_ref)
        for r in range(n_req):
            pad_ref[r * pitch + POOL_GAP // 2:r * pitch + POOL_GAP // 2 + seq_len, :] = a[r * seq_len:(r + 1) * seq_len]
        sums = _pool_windows(pad_ref[0:n, :])
        return jnp.concatenate(
            [sums[r * pitch + POOL_GAP // 2:r * pitch + POOL_GAP // 2 + seq_len] for r in range(n_req)], axis=0)

    is_ctx = pl.program_id(0) < T_CTX // TP

    @pl.when(is_ctx)
    def _():
        mix(padded_sums(SEQ), SEQ)

    @pl.when(jnp.logical_not(is_ctx))
    def _():
        mix(padded_sums(DEC_SEQ), DEC_SEQ)


def _pool(a, w_bd, scale):
    return pl.pallas_call(
        _pool_kernel,
        grid=(T_ALL // TP,),
        in_specs=[pl.BlockSpec((TP, POOL_WIDTH), lambda i: (i, 0)),
                  pl.BlockSpec((POOL_WIDTH, POOL_WIDTH), lambda i: (0, 0)),
                  pl.BlockSpec((1, POOL_WIDTH), lambda i: (0, 0))],
        out_specs=pl.BlockSpec((TP, POOL_WIDTH), lambda i: (i, 0)),
        out_shape=jax.ShapeDtypeStruct((T_ALL, POOL_WIDTH), BF16),
        scratch_shapes=[pltpu.VMEM(((TP // SEQ) * (SEQ + POOL_GAP), POOL_WIDTH), F32)],
        compiler_params=pltpu.CompilerParams(dimension_semantics=("arbitrary",), vmem_limit_bytes=VMEM_LIMIT),
        name="pool",
    )(a, w_bd, scale)


def _softmax_pv(scores, values):
    m = scores[0].max(axis=-1, keepdims=True)
    for s in scores[1:]:
        m = jnp.maximum(m, s.max(axis=-1, keepdims=True))
    ps = [jnp.exp(s - m) for s in scores]
    l = ps[0].sum(axis=-1, keepdims=True)
    for p in ps[1:]:
        l = l + p.sum(axis=-1, keepdims=True)
    r = 1.0 / l
    o = _dot((ps[0] * r).astype(BF16), values[0])
    for p, v in zip(ps[1:], values[1:]):
        o = o + _dot((p * r).astype(BF16), v)
    return o


def _lane_halves():
    lane = lax.broadcasted_iota(jnp.int32, (1, LANES), 1)
    return lane < HEAD_DIM, lane >= HEAD_DIM


def _keep(x, mask):
    return jnp.where(mask, x, 0.0).astype(BF16)


def _gqa_variants(x):
    lo, hi = _lane_halves()
    xs = pltpu.roll(x, HEAD_DIM, axis=1)
    nat_lo, nat_hi = _keep(x, lo), _keep(x, hi)
    sw_lo, sw_hi = _keep(xs, lo), _keep(xs, hi)
    return ((nat_lo, sw_hi), (nat_lo, nat_hi), (sw_lo, nat_hi))


def _mha_variants(x):
    lo, hi = _lane_halves()
    out = []
    for j in range(x.shape[1] // LANES):
        xc = x[:, j * LANES:(j + 1) * LANES]
        out.append((_keep(xc, lo), _keep(xc, hi)))
    return tuple(out)


def _attend_cols(q, k_vars, v_vars, extra_k=None, extra_v=None, bias=None):
    cols = []
    for j in range(q.shape[1] // LANES):
        qc = q[:, j * LANES:(j + 1) * LANES]
        o = None
        for hh in range(2):
            s = _dot_nt(qc, k_vars[j][hh])
            if bias is not None:
                s = s + bias[j][hh]
            scores, values = [s], [v_vars[j][hh]]
            if extra_k is not None:
                scores.append(_dot_nt(qc, extra_k[j][hh]))
                values.append(extra_v[j][hh])
            oh = _softmax_pv(scores, values)
            o = oh if o is None else o + oh
        cols.append(o)
    return jnp.concatenate(cols, axis=1)


def _attn_ctx_kernel(qb_ref, kb_ref, vb_ref, qc_ref, kc_ref, vc_ref, *refs):
    yb_ref, yc_ref, kbt_ref, vbt_ref, kct_ref, vct_ref = refs[-6:]
    kb, vb, kc, vc = kb_ref[...], vb_ref[...], kc_ref[...], vc_ref[...]
    yb = _attend_cols(qb_ref[...], _gqa_variants(kb), _gqa_variants(vb))
    yb_ref[...] = yb.astype(BF16)
    yc = _attend_cols(qc_ref[...], _mha_variants(kc), _mha_variants(vc))
    yc_ref[...] = yc.astype(BF16)
    kbt_ref[0, 0] = kb.T
    vbt_ref[0, 0] = vb.T
    kct_ref[0, 0] = kc.T
    vct_ref[0, 0] = vc.T


CACHE_WIDTHS = (KB_W, KB_W, C_W, C_W)


def _attn_ctx(layer, qb, kb, vb, qc, kc, vc, caches):
    row = lambda w: pl.BlockSpec((SEQ, w), lambda i: (i, 0))
    cache = lambda w: pl.BlockSpec((1, 1, w, SEQ), lambda i: (i, layer, 0, 0))
    return pl.pallas_call(
        _attn_ctx_kernel,
        grid=(BATCH,),
        in_specs=[row(QB_W), row(KB_W), row(KB_W), row(C_W), row(C_W), row(C_W)]
                 + [pl.BlockSpec(memory_space=pl.ANY)] * len(caches),
        out_specs=[row(QB_W), row(C_W)] + [cache(w) for w in CACHE_WIDTHS],
        out_shape=[jax.ShapeDtypeStruct((T_CTX, QB_W), BF16), jax.ShapeDtypeStruct((T_CTX, C_W), BF16)]
                  + [jax.ShapeDtypeStruct((BATCH, DEPTH, w, SEQ), F32) for w in CACHE_WIDTHS],
        input_output_aliases={6 + k: 2 + k for k in range(len(caches))},
        compiler_params=pltpu.CompilerParams(dimension_semantics=("arbitrary",), vmem_limit_bytes=VMEM_LIMIT),
        name="attn_ctx",
    )(qb, kb, vb, qc, kc, vc, *caches)


def _attn_latb_kernel(q_ref, k_ref, v_ref, ck_ref, cv_ref, y_ref):
    y = _attend_cols(q_ref[...], _gqa_variants(ck_ref[0, 0]), _gqa_variants(cv_ref[0, 0]),
                     extra_k=_gqa_variants(k_ref[...]), extra_v=_gqa_variants(v_ref[...]))
    y_ref[...] = y.astype(BF16)


def _attn_latb(layer, qb, kb, vb, cache_k, cache_v):
    ctx_h = T_CTX // HALF
    ctx_r = T_CTX // DEC_SEQ
    cache = pl.BlockSpec((1, 1, PAST_LEN, KB_W), lambda b, s: (b, layer, 0, 0))
    own = pl.BlockSpec((DEC_SEQ, KB_W), lambda b, s: (ctx_r + b, 0))
    return pl.pallas_call(
        _attn_latb_kernel,
        grid=(DEC_BATCH, DEC_SEQ // HALF),
        in_specs=[pl.BlockSpec((HALF, QB_W), lambda b, s: (ctx_h + 2 * b + s, 0)), own, own, cache, cache],
        out_specs=pl.BlockSpec((HALF, QB_W), lambda b, s: (2 * b + s, 0)),
        out_shape=jax.ShapeDtypeStruct((T_LAT, QB_W), BF16),
        compiler_params=pltpu.CompilerParams(
            dimension_semantics=("arbitrary", "arbitrary"), vmem_limit_bytes=VMEM_LIMIT),
        name="attn_latb",
    )(qb, kb, vb, cache_k, cache_v)


def _natten_window(s, i):
    r = (HALF // GRID_W) * s + i
    rs = min(max(r - WIN_R // 2, 0), GRID_ROWS - WIN_R)
    return r, rs


def _natten_kernel(q_ref, k_ref, v_ref, ck_ref, cv_ref, blk_ref, y_ref, bias_ref):
    s = pl.program_id(1)

    for sv in range(2):
        @pl.when((pl.program_id(2) == 0) & (s == sv))
        def _():
            lo_half, _ = _lane_halves()
            masked = jnp.full((GRID_W, LANES), NEG, F32)
            for hh in range(2):
                for i in range(HALF // GRID_W):
                    r, rs = _natten_window(sv, i)
                    for jp in range(NAT_KEYS // LANES):
                        pair = []
                        for j in (2 * jp, 2 * jp + 1):
                            rk = 4 * sv + j
                            pair.append(blk_ref[0, hh, rk - r + WIN_R - 1] if rs <= rk < rs + WIN_R else masked)
                        bias_ref[hh, i * GRID_W:(i + 1) * GRID_W, jp * LANES:(jp + 1) * LANES] = (
                            jnp.where(lo_half, pair[0], pair[1]))

    start = pl.multiple_of(s * (4 * GRID_W), 4 * GRID_W)
    k = _mha_variants(k_ref[pl.ds(start, NAT_KEYS), :])
    v = _mha_variants(v_ref[pl.ds(start, NAT_KEYS), :])
    ck = _mha_variants(ck_ref[0, 0])
    cv = _mha_variants(cv_ref[0, 0])
    bias = ((bias_ref[0], bias_ref[1]),)
    y = _attend_cols(q_ref[...], k, v, extra_k=ck, extra_v=cv, bias=bias)
    y_ref[...] = y.astype(BF16)


def _natten(layer, qc, kc, vc, cache_k, cache_v, blocks):
    ctx_h = T_CTX // HALF
    ctx_r = T_CTX // DEC_SEQ
    cache = pl.BlockSpec((1, 1, PAST_LEN, LANES), lambda j, s, b: (b, layer, 0, j))
    own = pl.BlockSpec((DEC_SEQ, LANES), lambda j, s, b: (ctx_r + b, j))
    return pl.pallas_call(
        _natten_kernel,
        grid=(C_W // LANES, DEC_SEQ // HALF, DEC_BATCH),
        in_specs=[pl.BlockSpec((HALF, LANES), lambda j, s, b: (ctx_h + 2 * b + s, j)), own, own, cache, cache,
                  pl.BlockSpec((1, 2, 2 * WIN_R - 1, GRID_W, LANES), lambda j, s, b: (j, 0, 0, 0, 0))],
        out_specs=pl.BlockSpec((HALF, LANES), lambda j, s, b: (2 * b + s, j)),
        out_shape=jax.ShapeDtypeStruct((T_LAT, C_W), BF16),
        scratch_shapes=[pltpu.VMEM((2, HALF, NAT_KEYS), F32)],
        compiler_params=pltpu.CompilerParams(
            dimension_semantics=("arbitrary", "arbitrary", "arbitrary"), vmem_limit_bytes=VMEM_LIMIT),
        name="natten",
    )(qc, kc, vc, cache_k, cache_v, blocks)


def _natten_blocks(rpb):
    qcol = np.arange(GRID_W)[:, None]
    kcol = np.arange(GRID_W)[None, :]
    cs = np.clip(qcol - WIN_C // 2, 0, GRID_W - WIN_C)
    col_ok = (kcol >= cs) & (kcol < cs + WIN_C)
    dc = np.clip(kcol - qcol, -(WIN_C - 1), WIN_C - 1) + (WIN_C - 1)
    sel_c = (dc[:, :, None] == np.arange(2 * WIN_C - 1)).astype(np.float32)
    blk = jnp.einsum("hdm,qkm->hdqk", rpb, jnp.asarray(sel_c), precision=lax.Precision.HIGHEST)
    blk = jnp.where(jnp.asarray(col_ok), blk, NEG)
    blk = jnp.concatenate([blk, blk], axis=-1)
    return blk.reshape(H_C // 2, 2, 2 * WIN_R - 1, GRID_W, LANES)


def _route(lg):
    lane = lax.broadcasted_iota(jnp.int32, lg.shape, 1).astype(F32)
    low = jnp.float32(-3.0e38)
    far = jnp.float32(ROUTER_LANES)
    is_g = lane < N_GROUPS
    gmax = jnp.where(is_g, lg, low).max(axis=-1, keepdims=True)
    gsel = jnp.where(is_g & (lg == gmax), lane, far).min(axis=-1, keepdims=True)
    pg_sel = 1.0 / jnp.where(is_g, jnp.exp(lg - gmax), 0.0).sum(axis=-1, keepdims=True)
    e0 = EXPERT_LANE0 + EXPERTS_PER_GROUP * gsel
    in_g = (lane >= e0) & (lane < e0 + EXPERTS_PER_GROUP)
    m1 = jnp.where(in_g, lg, low).max(axis=-1, keepdims=True)
    i1 = jnp.where(in_g & (lg == m1), lane, far).min(axis=-1, keepdims=True)
    rest = in_g & (lane != i1)
    m2 = jnp.where(rest, lg, low).max(axis=-1, keepdims=True)
    i2 = jnp.where(rest & (lg == m2), lane, far).min(axis=-1, keepdims=True)
    t = jnp.exp(m2 - m1)
    ssum = 1.0 + t
    w1 = (1.0 / ssum) * pg_sel
    w2 = (t / ssum) * pg_sel
    return i1 - EXPERT_LANE0, i2 - EXPERT_LANE0, w1, w2


def _rows_to_groups(x, stage_ref, out_ref, n_rows):
    stride = n_rows + 8
    for c in range(GROUP_ROWS):
        stage_ref[c * stride:c * stride + n_rows, :] = x[:, c * LANES:(c + 1) * LANES]
    for j in range(n_rows):
        out_ref[GROUP_ROWS * j:GROUP_ROWS * (j + 1), :] = stage_ref[pl.ds(j, GROUP_ROWS, stride=stride), :]


def _tail_kernel(xc_ref, xl_ref, ya_ref, ybc_ref, ybl_ref, ycc_ref, ycl_ref, w_ref, g1_ref, sh2_ref, sc2_ref,
                 lg_ref, lb_ref, wr_ref, br_ref, x1_ref, h2g_ref, ids_ref, wts_ref, wbf_ref, stage_ref):
    @pl.when(pl.program_id(0) == 0)
    def _():
        wbf_ref[...] = w_ref[0].astype(BF16)

    o = (_dot(ya_ref[...], wbf_ref[0:POOL_WIDTH, :])
         + _dot(_split_rows(ybc_ref, ybl_ref), wbf_ref[POOL_WIDTH:POOL_WIDTH + QB_W, :])
         + _dot(_split_rows(ycc_ref, ycl_ref), wbf_ref[POOL_WIDTH + QB_W:, :]))
    x1 = _layer_norm(ALPHA * _split_rows(xc_ref, xl_ref) + g1_ref[0] * o, lg_ref[0], lb_ref[0])
    x1_ref[...] = x1
    h2 = x1 * (1.0 + sc2_ref[0]) + sh2_ref[0]
    hh, hl = _split_bf16(h2)
    wh, wl = _split_bf16(wr_ref[0])
    lg = _dot(hh, wh) + _dot(hl, wh) + _dot(hh, wl) + br_ref[0]
    i1, i2, w1, w2 = _route(lg)
    lane = lax.broadcasted_iota(jnp.int32, lg.shape, 1)
    ids = jnp.where(lane == 0, i1, jnp.where(lane == 1, i2, 0.0))
    ids_ref[0] = ids.T[0:8, :].astype(jnp.int32)
    wts = jnp.where(lane == 0, w1, jnp.where(lane == 1, w2, 0.0))
    wts_ref[0] = wts.T[0:8, :]
    _rows_to_groups(h2, stage_ref, h2g_ref, TT)


def _tail(layer, x_ctx, x_lat, ya, yb_ctx, yb_lat, yc_ctx, yc_lat, w_out, mod, ln_g, ln_b, wr, br):
    row = lambda w: pl.BlockSpec((TT, w), lambda i: (i, 0))
    return pl.pallas_call(
        _tail_kernel,
        grid=(T_ALL // TT,),
        in_specs=[*_split_row_specs(D_MODEL), row(POOL_WIDTH), *_split_row_specs(QB_W), *_split_row_specs(C_W),
                  _layer_spec(layer, (D_MODEL, D_MODEL)),
                  _mod_spec(layer, 2, TT), _mod_spec(layer, 3, TT), _mod_spec(layer, 4, TT),
                  _layer_spec(layer, (1, D_MODEL)), _layer_spec(layer, (1, D_MODEL)),
                  _layer_spec(layer, (D_MODEL, ROUTER_LANES)), _layer_spec(layer, (1, ROUTER_LANES))],
        out_specs=[row(D_MODEL), pl.BlockSpec((TT * GROUP_ROWS, LANES), lambda i: (i, 0)),
                   pl.BlockSpec((1, 8, TT), lambda i: (i, 0, 0)), pl.BlockSpec((1, 8, TT), lambda i: (i, 0, 0))],
        out_shape=[jax.ShapeDtypeStruct((T_ALL, D_MODEL), F32),
                   jax.ShapeDtypeStruct((T_ALL * GROUP_ROWS, LANES), F32),
                   jax.ShapeDtypeStruct((T_ALL // TT, 8, TT), jnp.int32),
                   jax.ShapeDtypeStruct((T_ALL // TT, 8, TT), F32)],
        scratch_shapes=[pltpu.VMEM((D_MODEL, D_MODEL), BF16),
                        pltpu.VMEM((GROUP_ROWS * (TT + 8), LANES), F32)],
        compiler_params=pltpu.CompilerParams(dimension_semantics=("arbitrary",), vmem_limit_bytes=VMEM_LIMIT),
        name="tail",
    )(x_ctx, x_lat, ya, yb_ctx, yb_lat, yc_ctx, yc_lat, w_out, mod, mod, mod, ln_g, ln_b, wr, br)


def _plan_kernel(eid_ref, pos_ref, tile_ref):
    e = eid_ref[0]
    r = lax.broadcasted_iota(jnp.int32, (LANES, LANES), 0)
    c = lax.broadcasted_iota(jnp.int32, (LANES, LANES), 1)
    upper = (r <= c).astype(BF16)
    rows = e.shape[0]
    lower = (lax.broadcasted_iota(jnp.int32, (rows, rows), 1)
             < lax.broadcasted_iota(jnp.int32, (rows, rows), 0)).astype(BF16)
    tile_start = (lax.broadcasted_iota(jnp.int32, (1, LANES), 1) * TMX).astype(F32)
    lane = lax.broadcasted_iota(jnp.int32, (1, LANES), 1)
    pos = jnp.zeros(e.shape, F32)
    base = jnp.zeros((1, LANES), F32)
    n_before = jnp.zeros((1, LANES), F32)
    valid_end = jnp.zeros((1, LANES), F32)
    pad_end = jnp.zeros((1, LANES), F32)
    for ex in range(N_EXPERTS):
        m = e == ex
        incl = _dot(m.astype(BF16), upper)
        row_tot = jnp.broadcast_to(incl[:, LANES - 1:LANES], incl.shape)
        row_off = _dot(lower, row_tot.astype(BF16))
        cnt = jnp.sum(row_tot, axis=0, keepdims=True)
        pos = jnp.where(m, base + row_off + incl - 1.0, pos)
        valid_end = jnp.where(lane == ex, base + cnt, valid_end)
        base = base + jnp.ceil(cnt * (1.0 / TMX)) * TMX
        pad_end = jnp.where(lane == ex, base, pad_end)
        n_before = n_before + (tile_start >= base).astype(F32)
    pos_ref[0] = pos.astype(jnp.int32)
    sub = lax.broadcasted_iota(jnp.int32, (8, LANES), 0)
    rows8 = lambda v: jnp.broadcast_to(v, (8, LANES))
    table = jnp.where(sub == PLAN_TILE_EXPERT, rows8(jnp.minimum(n_before, N_EXPERTS - 1.0)),
                      jnp.where(sub == PLAN_N_TILES, rows8(base * (1.0 / TMX)),
                                jnp.where(sub == PLAN_VALID_END, rows8(valid_end),
                                          jnp.where(sub == PLAN_PAD_END, rows8(pad_end), 0.0))))
    tile_ref[0] = table.astype(jnp.int32)


def _plan(eid):
    rows = 2 * T_STREAM // LANES
    return pl.pallas_call(
        _plan_kernel,
        grid=(N_STREAMS,),
        in_specs=[pl.BlockSpec((1, rows, LANES), lambda s: (s, 0, 0))],
        out_specs=[pl.BlockSpec((1, rows, LANES), lambda s: (s, 0, 0)), pl.BlockSpec((1, 8, LANES), lambda s: (s, 0, 0))],
        out_shape=[jax.ShapeDtypeStruct((N_STREAMS, rows, LANES), jnp.int32),
                   jax.ShapeDtypeStruct((N_STREAMS, 8, LANES), jnp.int32)],
        compiler_params=pltpu.CompilerParams(dimension_semantics=("arbitrary",), vmem_limit_bytes=VMEM_LIMIT),
        name="plan",
    )(eid)


def _moe_kernel(pos_ref, wpair_ref, plan_ref, h2g_hbm, wg_ref, wu_ref, wd_ref, y_hbm,
                h2v, yv, off, xg0, xg1, yg0, yg1, wgb, wub, wdb, sem):
    s = pl.program_id(0)
    e = pl.program_id(1)
    n_pairs = 2 * T_STREAM
    spare_off = T_STREAM * GROUP_ROWS
    plan = lambda row, idx: plan_ref[(s * 8 + row) * LANES + idx]
    n_tiles = plan(PLAN_N_TILES, 0)
    last = n_tiles - 1

    xgs = (xg0, xg1)
    ygs = (yg0, yg1)

    def row_offset(pair):
        return pl.multiple_of((pair & (SLOT_PITCH - 1)) << 3, GROUP_ROWS)

    def gather(tile, dst):
        base = tile * TMX
        for j in range(TMX):
            dst[pl.ds(j, GROUP_ROWS, stride=XSTRIDE), :] = h2v[pl.ds(row_offset(off[base + j]), GROUP_ROWS), :]

    def h2_copy():
        return pltpu.make_async_copy(
            h2g_hbm.at[pl.ds(pl.multiple_of(s * (T_STREAM * GROUP_ROWS), 8), T_STREAM * GROUP_ROWS)],
            h2v.at[pl.ds(0, T_STREAM * GROUP_ROWS)], sem.at[0])

    def y_copy():
        return pltpu.make_async_copy(yv.at[pl.ds(0, T_STREAM * GROUP_ROWS)], y_hbm.at[s], sem.at[1])

    @pl.when(e == 0)
    def _():
        h2_copy().start()
        yv[...] = jnp.zeros_like(yv)
        h2v[spare_off:spare_off + GROUP_ROWS, :] = jnp.zeros((GROUP_ROWS, LANES), F32)

        def pad_expert(ex, carry):
            def pad_row(r, c):
                off[r] = T_STREAM
                return c
            return lax.fori_loop(plan(PLAN_VALID_END, ex), plan(PLAN_PAD_END, ex), pad_row, carry)
        lax.fori_loop(0, N_EXPERTS, pad_expert, 0)

        def place(i, carry):
            for u in range(PLACE_UNROLL):
                tok = i * PLACE_UNROLL + u
                for slot in range(2):
                    off[pos_ref[s * n_pairs + slot * T_STREAM + tok]] = slot * SLOT_PITCH + tok
            return carry
        lax.fori_loop(0, T_STREAM // PLACE_UNROLL, place, 0)
        yg1[...] = jnp.zeros_like(yg1)
        h2_copy().wait()
        gather(0, xg0)

    t_lo = jnp.where(e == 0, 0, plan(PLAN_PAD_END, jnp.maximum(e - 1, 0))) // TMX
    t_hi = plan(PLAN_PAD_END, e) // TMX

    @pl.when(t_hi > t_lo)
    def _():
        wgb[...] = wg_ref[0, 0].astype(BF16)
        wub[...] = wu_ref[0, 0].astype(BF16)
        wdb[...] = wd_ref[0, 0].astype(BF16)

    def add_tile(tile, done):
        base = tile * TMX
        for j0 in range(0, TMX, RMW_BATCH):
            updates = []
            for j in range(j0, j0 + RMW_BATCH):
                pair = off[base + j]
                dst = row_offset(pair)
                gate = wpair_ref[s * (2 * SLOT_PITCH) + pair]
                row = done[pl.ds(j, GROUP_ROWS, stride=XSTRIDE), :]
                updates.append((dst, yv[pl.ds(dst, GROUP_ROWS), :] + gate * row))
            for dst, val in updates:
                yv[pl.ds(dst, GROUP_ROWS), :] = val

    def step(t, cur, nxt, out, done):
        gather(jnp.minimum(t + 1, last), nxt)
        x = jnp.concatenate([cur[c * XSTRIDE:c * XSTRIDE + TMX, :] for c in range(GROUP_ROWS)], axis=1).astype(BF16)
        hg = _dot(x, wgb[...])
        hu = _dot(x, wub[...])
        act = hg * jax.nn.sigmoid(hg) * hu
        y = _dot(act.astype(BF16), wdb[...])
        for c in range(GROUP_ROWS):
            out[c * XSTRIDE:c * XSTRIDE + TMX, :] = y[:, c * LANES:(c + 1) * LANES]
        add_tile(jnp.maximum(t - 1, 0), done)

    def tile_body(t, carry):
        for par in range(2):
            @pl.when((t & 1) == par)
            def _():
                step(t, xgs[par], xgs[1 - par], ygs[par], ygs[1 - par])
        return carry
    lax.fori_loop(t_lo, t_hi, tile_body, 0)

    @pl.when(e == N_EXPERTS - 1)
    def _():
        for par in range(2):
            @pl.when((last & 1) == par)
            def _():
                add_tile(last, ygs[par])
        y_copy().start()
        y_copy().wait()


def _moe(layer, pos, wpair, plan, h2g, w_gate, w_up, w_down):
    wspec = lambda shape: pl.BlockSpec((1,) + shape, lambda s, e, pos, wpair, plan: (layer, e, 0, 0))
    return pl.pallas_call(
        _moe_kernel,
        grid_spec=pltpu.PrefetchScalarGridSpec(
            num_scalar_prefetch=3,
            grid=(N_STREAMS, N_EXPERTS),
            in_specs=[pl.BlockSpec(memory_space=pl.ANY),
                      wspec((1, D_MODEL, D_EXPERT)), wspec((1, D_MODEL, D_EXPERT)), wspec((1, D_EXPERT, D_MODEL))],
            out_specs=pl.BlockSpec(memory_space=pl.ANY),
            scratch_shapes=[pltpu.VMEM(((T_STREAM + 1) * GROUP_ROWS, LANES), F32),
                            pltpu.VMEM(((T_STREAM + 1) * GROUP_ROWS, LANES), F32),
                            pltpu.SMEM((P_MAX,), jnp.int32),
                            pltpu.VMEM((GROUP_ROWS * XSTRIDE, LANES), F32),
                            pltpu.VMEM((GROUP_ROWS * XSTRIDE, LANES), F32),
                            pltpu.VMEM((GROUP_ROWS * XSTRIDE, LANES), F32),
                            pltpu.VMEM((GROUP_ROWS * XSTRIDE, LANES), F32),
                            pltpu.VMEM((D_MODEL, D_EXPERT), BF16), pltpu.VMEM((D_MODEL, D_EXPERT), BF16),
                            pltpu.VMEM((D_EXPERT, D_MODEL), BF16),
                            pltpu.SemaphoreType.DMA((2,))]),
        out_shape=jax.ShapeDtypeStruct((N_STREAMS, T_STREAM * GROUP_ROWS, LANES), F32),
        compiler_params=pltpu.CompilerParams(
            dimension_semantics=("arbitrary", "arbitrary"), vmem_limit_bytes=VMEM_LIMIT),
        name="moe",
    )(pos, wpair, plan, h2g, w_gate, w_up, w_down)


def _groups_to_rows(src_ref, stage_ref, n_rows):
    stride = n_rows + 8
    for j in range(n_rows):
        stage_ref[pl.ds(j, GROUP_ROWS, stride=stride), :] = src_ref[GROUP_ROWS * j:GROUP_ROWS * (j + 1), :]
    return jnp.concatenate([stage_ref[c * stride:c * stride + n_rows, :] for c in range(GROUP_ROWS)], axis=1)


def _final_kernel(x1_ref, y_ref, g2_ref, lg_ref, lb_ref, o_ref, stage_ref):
    moe = _groups_to_rows(y_ref, stage_ref, TT)
    o_ref[...] = _layer_norm(ALPHA * x1_ref[...] + g2_ref[0] * moe, lg_ref[0], lb_ref[0])


def _final(layer, stream, x1, y, mod, ln_g, ln_b):
    first = stream * (T_STREAM // TT)
    g2 = pl.BlockSpec(
        (1, 1, D_MODEL), lambda i: ((layer * MOD_ROWS + _mod_row_of_tile(first + i, TT)) * 6 + 5, 0, 0))
    return pl.pallas_call(
        _final_kernel,
        grid=(T_STREAM // TT,),
        in_specs=[pl.BlockSpec((TT, D_MODEL), lambda i: (first + i, 0)),
                  pl.BlockSpec((TT * GROUP_ROWS, LANES), lambda i: (first + i, 0)),
                  g2, _layer_spec(layer, (1, D_MODEL)), _layer_spec(layer, (1, D_MODEL))],
        out_specs=pl.BlockSpec((TT, D_MODEL), lambda i: (i, 0)),
        out_shape=jax.ShapeDtypeStruct((T_STREAM, D_MODEL), F32),
        scratch_shapes=[pltpu.VMEM((GROUP_ROWS * (TT + 8), LANES), F32)],
        compiler_params=pltpu.CompilerParams(dimension_semantics=("arbitrary",), vmem_limit_bytes=VMEM_LIMIT),
        name="final",
    )(x1, y, mod, ln_g, ln_b)


def _rope_tables():
    t = np.arange(DEC_SEQ)
    pos = np.stack([t // GRID_W, t % GRID_W], axis=1).astype(np.float32)
    nf = HEAD_DIM // 4
    inv = jnp.asarray(ROPE_THETA, F32) ** (-jnp.arange(nf, dtype=F32) / nf)
    d = np.arange(LANES) % HEAD_DIM
    which = d // (HEAD_DIM // 2)
    ang = jnp.asarray(pos)[:, which] * inv[d % nf][None, :]
    sign = np.where((d % 32) < 16, -1.0, 1.0).astype(np.float32)
    cos = jnp.concatenate([jnp.cos(ang), jnp.ones((TT, LANES), F32)], axis=0)
    sin = jnp.concatenate([jnp.sin(ang) * sign[None, :], jnp.zeros((TT, LANES), F32)], axis=0)
    return cos, sin


def _block_ones():
    h = np.arange(QB_W) // HEAD_DIM
    return jnp.asarray((h[:, None] == h[None, :]).astype(np.float32), dtype=BF16)


def _pool_block_diag(pool_w):
    out = jnp.zeros((POOL_WIDTH, POOL_WIDTH), F32)
    for g in range(4):
        out = out.at[64 * g:64 * g + 64, 64 * g:64 * g + 64].set(pool_w[g])
    return out


def kernel(x_prompt, x_sample, cache_b_k, cache_b_v, cache_c_k, cache_c_v, c, c_ctx, w_ada, b_ada, w_in, w_out,
           pool_w, pool_scale, q_norm, k_norm, rpb, ln1_g, ln1_b, ln2_g, ln2_b, router_g, router_g_b, router_e,
           router_e_b, w_gate, w_up, w_down):
    x_ctx = x_prompt.reshape(T_CTX, D_MODEL)
    x_lat = x_sample.reshape(T_LAT, D_MODEL)
    c8 = jnp.concatenate([c, c_ctx[None], jnp.zeros((MOD_ROWS - DEC_BATCH - 1, D_MODEL), F32)], axis=0)
    mod = _adaln(c8, w_ada, b_ada).reshape(DEPTH * MOD_ROWS * 6, 1, D_MODEL)

    cos_t, sin_t = _rope_tables()
    ones_bd = _block_ones()
    cbk = cache_b_k.reshape(DEC_BATCH, DEPTH, PAST_LEN, KB_W)
    cbv = cache_b_v.reshape(DEC_BATCH, DEPTH, PAST_LEN, KB_W)
    cck = cache_c_k.reshape(DEC_BATCH, DEPTH, PAST_LEN, C_W)
    ccv = cache_c_v.reshape(DEC_BATCH, DEPTH, PAST_LEN, C_W)
    pad = jnp.zeros((DEPTH, D_MODEL, ROUTER_LANES - N_GROUPS - N_EXPERTS), F32)
    wr = jnp.concatenate([router_g, router_e, pad], axis=2)
    br = jnp.concatenate([router_g_b, router_e_b, pad[:, 0]], axis=1)[:, None]
    per_layer = lambda p: p[:, None]

    caches = [jnp.zeros((BATCH, DEPTH, w, SEQ), F32) for w in CACHE_WIDTHS]
    for l in range(DEPTH):
        qn = jnp.tile(q_norm[l], H_B)[None]
        kn = jnp.tile(k_norm[l], KV_B)[None]
        a, qb, kb, vb, qc, kc, vc = _proj(l, x_ctx, x_lat, mod, w_in, ones_bd, qn, kn, cos_t, sin_t)
        ya = _pool(a, _pool_block_diag(pool_w[l]), pool_scale[l][None])
        yb_ctx, yc_ctx, *caches = _attn_ctx(l, qb, kb, vb, qc, kc, vc, caches)
        yb_lat = _attn_latb(l, qb, kb, vb, cbk, cbv)
        yc_lat = _natten(l, qc, kc, vc, cck, ccv, _natten_blocks(rpb[l]))
        x1, h2g, ids, wts = _tail(l, x_ctx, x_lat, ya, yb_ctx, yb_lat, yc_ctx, yc_lat, w_out, mod,
                                  per_layer(ln1_g), per_layer(ln1_b), wr, br)
        pairs = lambda a: a.reshape(N_STREAMS, T_STREAM // TT, 8, TT)[:, :, :2, :].transpose(0, 2, 1, 3)
        pos, tiles = _plan(pairs(ids).reshape(N_STREAMS, 2 * T_STREAM // LANES, LANES))
        gates = jnp.pad(pairs(wts).reshape(N_STREAMS, 2, T_STREAM), ((0, 0), (0, 0), (0, SLOT_PITCH - T_STREAM)))
        y = _moe(l, pos.reshape(-1), gates.reshape(-1), tiles.reshape(-1), h2g, w_gate, w_up, w_down)
        y = y.reshape(T_ALL * GROUP_ROWS, LANES)
        x_ctx = _final(l, 0, x1, y, mod, per_layer(ln2_g), per_layer(ln2_b))
        x_lat = _final(l, 1, x1, y, mod, per_layer(ln2_g), per_layer(ln2_b))

    y_prompt = x_ctx.reshape(BATCH, SEQ, D_MODEL)
    y_sample = x_lat.reshape(DEC_BATCH, DEC_SEQ, D_MODEL)
    heads_last = lambda t, h: t.reshape(BATCH, DEPTH, h, HEAD_DIM, SEQ).transpose(0, 1, 4, 2, 3)
    new_bk, new_bv, new_ck, new_cv = caches
    return (y_prompt, y_sample, heads_last(new_bk, KV_B), heads_last(new_bv, KV_B),
            heads_last(new_ck, H_C), heads_last(new_cv, H_C))
```

```python
import functools

import numpy as np
import jax
import jax.numpy as jnp
from jax import lax
from jax.experimental import pallas as pl
from jax.experimental.pallas import tpu as pltpu

F32 = jnp.float32
BF16 = jnp.bfloat16

D_MODEL = 1024
BATCH = 16
SEQ = 256
DEPTH = 2
DEC_BATCH = 4
DEC_SEQ = 1024
PAST_LEN = 256
GRID_W = 64
GRID_ROWS = DEC_SEQ // GRID_W
HEAD_DIM = 64
POOL_WIDTH = 256
POOL_WINDOWS = (2, 4, 8, 16)
H_B = 6
KV_B = 2
H_C = 6
WIN_R = 8
WIN_C = 16
ROPE_THETA = 10000.0
QB_W = H_B * HEAD_DIM
KB_W = KV_B * HEAD_DIM
C_W = H_C * HEAD_DIM
PROJ_WIDTH = 2048
N_GROUPS = 4
EXPERTS_PER_GROUP = 8
N_EXPERTS = 32
D_EXPERT = 256
ALPHA = (2 * DEPTH) ** 0.25
LN_EPS = 1e-6
RMS_EPS = 1e-6
NEG = -1e30
ATTN_SCALE = HEAD_DIM ** -0.5

T_CTX = BATCH * SEQ
T_LAT = DEC_BATCH * DEC_SEQ
T_ALL = T_CTX + T_LAT

LANES = 128
ROUTER_LANES = 128
EXPERT_LANE0 = N_GROUPS
MOD_ROWS = 8
CTX_MOD_ROW = DEC_BATCH

TT = 512
TP = 1024
POOL_GAP = 16
N_STREAMS = 2
T_STREAM = T_ALL // N_STREAMS
TMX = 128
PLACE_UNROLL = 8
PAD_CHUNK = 8
SLOT_PITCH = 2 * T_STREAM
P_MAX = 2 * T_STREAM + N_EXPERTS * TMX
NT_MAX = P_MAX // TMX
GROUP_ROWS = D_MODEL // LANES
XSTRIDE = TMX + 8
RMW_BATCH = 16
PLAN_TILE_EXPERT, PLAN_N_TILES, PLAN_VALID_END, PLAN_PAD_END = 0, 1, 2, 3
HALF = DEC_SEQ // 2
NAT_KEYS = 12 * GRID_W
VMEM_LIMIT = 56 * 1024 * 1024


def _dot(a, b):
    return jnp.dot(a, b, preferred_element_type=F32)


def _dot_nt(a, b):
    return lax.dot_general(a, b, (((1,), (1,)), ((), ())), preferred_element_type=F32)


def _split_bf16(x):
    hi = x.astype(BF16)
    lo = (x - hi.astype(F32)).astype(BF16)
    return hi, lo


def _layer_norm(y, g, b):
    mu = jnp.mean(y, axis=-1, keepdims=True)
    var = jnp.mean(jnp.square(y - mu), axis=-1, keepdims=True)
    return (y - mu) * lax.rsqrt(var + LN_EPS) * g + b


def _adaln_kernel(c_ref, w_ref, b_ref, o_ref):
    c = c_ref[...]
    s = (c * jax.nn.sigmoid(c)).astype(BF16)
    o_ref[0] = _dot(s, w_ref[0].astype(BF16)) + b_ref[0]


def _adaln(c8, w_ada, b_ada):
    tn = 1536
    n = w_ada.shape[-1]
    return pl.pallas_call(
        _adaln_kernel,
        grid=(DEPTH, n // tn),
        in_specs=[
            pl.BlockSpec((MOD_ROWS, D_MODEL), lambda l, j: (0, 0)),
            pl.BlockSpec((1, D_MODEL, tn), lambda l, j: (l, 0, j)),
            pl.BlockSpec((1, 1, tn), lambda l, j: (l, 0, j)),
        ],
        out_specs=pl.BlockSpec((1, MOD_ROWS, tn), lambda l, j: (l, 0, j)),
        out_shape=jax.ShapeDtypeStruct((DEPTH, MOD_ROWS, n), F32),
        compiler_params=pltpu.CompilerParams(
            dimension_semantics=("arbitrary", "arbitrary"), vmem_limit_bytes=VMEM_LIMIT),
        name="adaln",
    )(c8, w_ada, b_ada.reshape(DEPTH, 1, n))


def _mod_row_of_tile(i, tile):
    n_ctx = T_CTX // tile
    per_req = DEC_SEQ // tile
    return jnp.where(i < n_ctx, CTX_MOD_ROW, (i - n_ctx) // per_req)


def _mod_spec(layer, chunk, tile):
    return pl.BlockSpec(
        (1, 1, D_MODEL), lambda i: ((layer * MOD_ROWS + _mod_row_of_tile(i, tile)) * 6 + chunk, 0, 0))


def _layer_spec(layer, shape):
    return pl.BlockSpec((1,) + shape, lambda *_: (layer,) + (0,) * len(shape))


N_CTX_TILES = T_CTX // TT


def _split_row_specs(width):
    return (pl.BlockSpec((TT, width), lambda i: (jnp.minimum(i, N_CTX_TILES - 1), 0)),
            pl.BlockSpec((TT, width), lambda i: (jnp.maximum(i - N_CTX_TILES, 0), 0)))


def _split_rows(ctx_ref, lat_ref):
    return jnp.where(pl.program_id(0) < N_CTX_TILES, ctx_ref[...], lat_ref[...])


def _rms_norm_heads(x, ones_bd, w):
    hi, lo = _split_bf16(x * x)
    ssq = _dot(hi, ones_bd) + _dot(lo, ones_bd)
    return x * lax.rsqrt(ssq * (1.0 / HEAD_DIM) + RMS_EPS) * w


def _rope_cols(x, cos, sin, first16):
    cols = []
    for j in range(x.shape[1] // LANES):
        xc = x[:, j * LANES:(j + 1) * LANES]
        partner = jnp.where(first16, pltpu.roll(xc, LANES - 16, axis=1), pltpu.roll(xc, 16, axis=1))
        cols.append(xc * cos + partner * sin)
    return jnp.concatenate(cols, axis=1) if len(cols) > 1 else cols[0]


def _proj_kernel(xc_ref, xl_ref, sh_ref, sc_ref, w_ref, ones_ref, qn_ref, kn_ref, cos_ref, sin_ref,
                 a_ref, qb_ref, kb_ref, vb_ref, qc_ref, kc_ref, vc_ref, wbf_ref):
    @pl.when(pl.program_id(0) == 0)
    def _():
        wbf_ref[...] = w_ref[0].astype(BF16)

    h = _split_rows(xc_ref, xl_ref) * (1.0 + sc_ref[0]) + sh_ref[0]
    p = _dot(h.astype(BF16), wbf_ref[...])
    o = 0
    a_ref[...] = p[:, o:o + POOL_WIDTH]; o += POOL_WIDTH
    qb = p[:, o:o + QB_W]; o += QB_W
    kb = p[:, o:o + KB_W]; o += KB_W
    vb_ref[...] = p[:, o:o + KB_W]; o += KB_W
    qc_ref[...] = (p[:, o:o + C_W] * ATTN_SCALE).astype(BF16); o += C_W
    kc_ref[...] = p[:, o:o + C_W]; o += C_W
    vc_ref[...] = p[:, o:o + C_W]

    ones_bd = ones_ref[...]
    cos = cos_ref[...]
    sin = sin_ref[...]
    lane = lax.broadcasted_iota(jnp.int32, (1, LANES), 1)
    first16 = (lane & 31) < 16
    qb = _rope_cols(_rms_norm_heads(qb, ones_bd, qn_ref[...]), cos, sin, first16)
    kb = _rope_cols(_rms_norm_heads(kb, ones_bd[:KB_W, :KB_W], kn_ref[...]), cos, sin, first16)
    qb_ref[...] = (qb * ATTN_SCALE).astype(BF16)
    kb_ref[...] = kb


def _proj(layer, x_ctx, x_lat, mod, w_in, ones_bd, qn, kn, cos_t, sin_t):
    n_tiles = T_ALL // TT
    n_ctx = N_CTX_TILES
    per_req = DEC_SEQ // TT

    def rope_idx(i):
        return (jnp.where(i < n_ctx, per_req, (i - n_ctx) % per_req), 0)

    row = lambda w: pl.BlockSpec((TT, w), lambda i: (i, 0))
    const = lambda s: pl.BlockSpec(s, lambda i: (0,) * len(s))
    widths = (POOL_WIDTH, QB_W, KB_W, KB_W, C_W, C_W, C_W)
    dtypes = (F32, BF16, F32, F32, BF16, F32, F32)
    return pl.pallas_call(
        _proj_kernel,
        grid=(n_tiles,),
        in_specs=[*_split_row_specs(D_MODEL), _mod_spec(layer, 0, TT), _mod_spec(layer, 1, TT),
                  _layer_spec(layer, (D_MODEL, PROJ_WIDTH)),
                  const((QB_W, QB_W)), const((1, QB_W)), const((1, KB_W)),
                  pl.BlockSpec((TT, LANES), rope_idx), pl.BlockSpec((TT, LANES), rope_idx)],
        out_specs=[row(w) for w in widths],
        out_shape=[jax.ShapeDtypeStruct((T_ALL, w), d) for w, d in zip(widths, dtypes)],
        scratch_shapes=[pltpu.VMEM((D_MODEL, PROJ_WIDTH), BF16)],
        compiler_params=pltpu.CompilerParams(dimension_semantics=("arbitrary",), vmem_limit_bytes=VMEM_LIMIT),
        name="proj",
    )(x_ctx, x_lat, mod, mod, w_in, ones_bd, qn, kn, cos_t, sin_t)


def _pool_windows(p):
    n = p.shape[0]
    sh = lambda x, k: pltpu.roll(x, (n - k) % n, axis=0)
    w2 = p + sh(p, -1)
    w4 = sh(w2, -1) + sh(w2, 1)
    w8 = sh(w4, -2) + sh(w4, 2)
    w16 = sh(w8, -4) + sh(w8, 4)
    g = lax.broadcasted_iota(jnp.int32, p.shape, 1) >> 6
    return jnp.where(g == 0, w2, jnp.where(g == 1, w4, jnp.where(g == 2, w8, w16)))


def _pool_kernel(a_ref, w_ref, scale_ref, o_ref, pad_ref):
    a = a_ref[...]

    def mix(csum, seq_len):
        row = lax.broadcasted_iota(jnp.int32, a.shape, 0)
        g = lax.broadcasted_iota(jnp.int32, a.shape, 1) >> 6
        t = row & (seq_len - 1)
        half = jnp.where(g == 0, 1, jnp.where(g == 1, 2, jnp.where(g == 2, 4, 8)))
        cnt = jnp.minimum(t + half, seq_len) - jnp.maximum(t - half, 0)
        pooled = csum / cnt.astype(F32) - a
        y = _dot(pooled.astype(BF16), w_ref[...].astype(BF16)) * scale_ref[...]
        o_ref[...] = y.astype(BF16)

    def padded_sums(seq_len):
        pitch = seq_len + POOL_GAP
        n_req = TP // seq_len
        n = n_req * pitch
        pad_ref[...] = jnp.zeros_like(pad_ref)
        for r in range(n_req):
            pad_ref[r * pitch + POOL_GAP // 2:r * pitch + POOL_GAP // 2 + seq_len, :] = a[r * seq_len:(r + 1) * seq_len]
        sums = _pool_windows(pad_ref[0:n, :])
        return jnp.concatenate(
            [sums[r * pitch + POOL_GAP // 2:r * pitch + POOL_GAP // 2 + seq_len] for r in range(n_req)], axis=0)

    is_ctx = pl.program_id(0) < T_CTX // TP

    @pl.when(is_ctx)
    def _():
        mix(padded_sums(SEQ), SEQ)

    @pl.when(jnp.logical_not(is_ctx))
    def _():
        mix(padded_sums(DEC_SEQ), DEC_SEQ)


def _pool(a, w_bd, scale):
    return pl.pallas_call(
        _pool_kernel,
        grid=(T_ALL // TP,),
        in_specs=[pl.BlockSpec((TP, POOL_WIDTH), lambda i: (i, 0)),
                  pl.BlockSpec((POOL_WIDTH, POOL_WIDTH), lambda i: (0, 0)),
                  pl.BlockSpec((1, POOL_WIDTH), lambda i: (0, 0))],
        out_specs=pl.BlockSpec((TP, POOL_WIDTH), lambda i: (i, 0)),
        out_shape=jax.ShapeDtypeStruct((T_ALL, POOL_WIDTH), BF16),
        scratch_shapes=[pltpu.VMEM(((TP // SEQ) * (SEQ + POOL_GAP), POOL_WIDTH), F32)],
        compiler_params=pltpu.CompilerParams(dimension_semantics=("arbitrary",), vmem_limit_bytes=VMEM_LIMIT),
        name="pool",
    )(a, w_bd, scale)


def _softmax_pv(scores, values):
    m = scores[0].max(axis=-1, keepdims=True)
    for s in scores[1:]:
        m = jnp.maximum(m, s.max(axis=-1, keepdims=True))
    ps = [jnp.exp(s - m) for s in scores]
    l = ps[0].sum(axis=-1, keepdims=True)
    for p in ps[1:]:
        l = l + p.sum(axis=-1, keepdims=True)
    o = _dot(ps[0].astype(BF16), values[0])
    for p, v in zip(ps[1:], values[1:]):
        o = o + _dot(p.astype(BF16), v)
    return o * (1.0 / l)


def _lane_halves():
    lane = lax.broadcasted_iota(jnp.int32, (1, LANES), 1)
    return lane < HEAD_DIM, lane >= HEAD_DIM


def _keep(x, mask):
    return jnp.where(mask, x, 0.0).astype(BF16)


def _gqa_variants(x):
    lo, hi = _lane_halves()
    xs = pltpu.roll(x, HEAD_DIM, axis=1)
    nat_lo, nat_hi = _keep(x, lo), _keep(x, hi)
    sw_lo, sw_hi = _keep(xs, lo), _keep(xs, hi)
    return ((nat_lo, sw_hi), (nat_lo, nat_hi), (sw_lo, nat_hi))


def _mha_variants(x):
    lo, hi = _lane_halves()
    out = []
    for j in range(x.shape[1] // LANES):
        xc = x[:, j * LANES:(j + 1) * LANES]
        out.append((_keep(xc, lo), _keep(xc, hi)))
    return tuple(out)


def _attend_cols(q, k_vars, v_vars, extra_k=None, extra_v=None, bias=None):
    cols = []
    for j in range(q.shape[1] // LANES):
        qc = q[:, j * LANES:(j + 1) * LANES]
        o = None
        for hh in range(2):
            s = _dot_nt(qc, k_vars[j][hh])
            if bias is not None:
                s = s + bias[j][hh]
            scores, values = [s], [v_vars[j][hh]]
            if extra_k is not None:
                scores.append(_dot_nt(qc, extra_k[j][hh]))
                values.append(extra_v[j][hh])
            oh = _softmax_pv(scores, values)
            o = oh if o is None else o + oh
        cols.append(o)
    return jnp.concatenate(cols, axis=1)


def _attn_ctx_kernel(qb_ref, kb_ref, vb_ref, qc_ref, kc_ref, vc_ref, *refs):
    yb_ref, yc_ref, kbt_ref, vbt_ref, kct_ref, vct_ref = refs[-6:]
    kb, vb, kc, vc = kb_ref[...], vb_ref[...], kc_ref[...], vc_ref[...]
    yb = _attend_cols(qb_ref[...], _gqa_variants(kb), _gqa_variants(vb))
    yb_ref[...] = yb.astype(BF16)
    yc = _attend_cols(qc_ref[...], _mha_variants(kc), _mha_variants(vc))
    yc_ref[...] = yc.astype(BF16)
    kbt_ref[0, 0] = kb.T
    vbt_ref[0, 0] = vb.T
    kct_ref[0, 0] = kc.T
    vct_ref[0, 0] = vc.T


CACHE_WIDTHS = (KB_W, KB_W, C_W, C_W)


def _attn_ctx(layer, qb, kb, vb, qc, kc, vc, caches):
    row = lambda w: pl.BlockSpec((SEQ, w), lambda i: (i, 0))
    cache = lambda w: pl.BlockSpec((1, 1, w, SEQ), lambda i: (i, layer, 0, 0))
    return pl.pallas_call(
        _attn_ctx_kernel,
        grid=(BATCH,),
        in_specs=[row(QB_W), row(KB_W), row(KB_W), row(C_W), row(C_W), row(C_W)]
                 + [pl.BlockSpec(memory_space=pl.ANY)] * len(caches),
        out_specs=[row(QB_W), row(C_W)] + [cache(w) for w in CACHE_WIDTHS],
        out_shape=[jax.ShapeDtypeStruct((T_CTX, QB_W), BF16), jax.ShapeDtypeStruct((T_CTX, C_W), BF16)]
                  + [jax.ShapeDtypeStruct((BATCH, DEPTH, w, SEQ), F32) for w in CACHE_WIDTHS],
        input_output_aliases={6 + k: 2 + k for k in range(len(caches))},
        compiler_params=pltpu.CompilerParams(dimension_semantics=("arbitrary",), vmem_limit_bytes=VMEM_LIMIT),
        name="attn_ctx",
    )(qb, kb, vb, qc, kc, vc, *caches)


def _attn_latb_kernel(q_ref, k_ref, v_ref, ck_ref, cv_ref, y_ref):
    y = _attend_cols(q_ref[...], _gqa_variants(ck_ref[0, 0]), _gqa_variants(cv_ref[0, 0]),
                     extra_k=_gqa_variants(k_ref[...]), extra_v=_gqa_variants(v_ref[...]))
    y_ref[...] = y.astype(BF16)


def _attn_latb(layer, qb, kb, vb, cache_k, cache_v):
    ctx_h = T_CTX // HALF
    ctx_r = T_CTX // DEC_SEQ
    cache = pl.BlockSpec((1, 1, PAST_LEN, KB_W), lambda b, s: (b, layer, 0, 0))
    own = pl.BlockSpec((DEC_SEQ, KB_W), lambda b, s: (ctx_r + b, 0))
    return pl.pallas_call(
        _attn_latb_kernel,
        grid=(DEC_BATCH, DEC_SEQ // HALF),
        in_specs=[pl.BlockSpec((HALF, QB_W), lambda b, s: (ctx_h + 2 * b + s, 0)), own, own, cache, cache],
        out_specs=pl.BlockSpec((HALF, QB_W), lambda b, s: (2 * b + s, 0)),
        out_shape=jax.ShapeDtypeStruct((T_LAT, QB_W), BF16),
        compiler_params=pltpu.CompilerParams(
            dimension_semantics=("arbitrary", "arbitrary"), vmem_limit_bytes=VMEM_LIMIT),
        name="attn_latb",
    )(qb, kb, vb, cache_k, cache_v)


def _natten_window(s, i):
    r = (HALF // GRID_W) * s + i
    rs = min(max(r - WIN_R // 2, 0), GRID_ROWS - WIN_R)
    return r, rs


def _natten_kernel(q_ref, k_ref, v_ref, ck_ref, cv_ref, blk_ref, y_ref, bias_ref):
    s = pl.program_id(1)

    for sv in range(2):
        @pl.when((pl.program_id(2) == 0) & (s == sv))
        def _():
            lo_half, _ = _lane_halves()
            masked = jnp.full((GRID_W, LANES), NEG, F32)
            for hh in range(2):
                for i in range(HALF // GRID_W):
                    r, rs = _natten_window(sv, i)
                    for jp in range(NAT_KEYS // LANES):
                        pair = []
                        for j in (2 * jp, 2 * jp + 1):
                            rk = 4 * sv + j
                            pair.append(blk_ref[0, hh, rk - r + WIN_R - 1] if rs <= rk < rs + WIN_R else masked)
                        bias_ref[hh, i * GRID_W:(i + 1) * GRID_W, jp * LANES:(jp + 1) * LANES] = (
                            jnp.where(lo_half, pair[0], pair[1]))

    start = pl.multiple_of(s * (4 * GRID_W), 4 * GRID_W)
    k = _mha_variants(k_ref[pl.ds(start, NAT_KEYS), :])
    v = _mha_variants(v_ref[pl.ds(start, NAT_KEYS), :])
    ck = _mha_variants(ck_ref[0, 0])
    cv = _mha_variants(cv_ref[0, 0])
    bias = ((bias_ref[0], bias_ref[1]),)
    y = _attend_cols(q_ref[...], k, v, extra_k=ck, extra_v=cv, bias=bias)
    y_ref[...] = y.astype(BF16)


def _natten(layer, qc, kc, vc, cache_k, cache_v, blocks):
    ctx_h = T_CTX // HALF
    ctx_r = T_CTX // DEC_SEQ
    cache = pl.BlockSpec((1, 1, PAST_LEN, LANES), lambda j, s, b: (b, layer, 0, j))
    own = pl.BlockSpec((DEC_SEQ, LANES), lambda j, s, b: (ctx_r + b, j))
    return pl.pallas_call(
        _natten_kernel,
        grid=(C_W // LANES, DEC_SEQ // HALF, DEC_BATCH),
        in_specs=[pl.BlockSpec((HALF, LANES), lambda j, s, b: (ctx_h + 2 * b + s, j)), own, own, cache, cache,
                  pl.BlockSpec((1, 2, 2 * WIN_R - 1, GRID_W, LANES), lambda j, s, b: (j, 0, 0, 0, 0))],
        out_specs=pl.BlockSpec((HALF, LANES), lambda j, s, b: (2 * b + s, j)),
        out_shape=jax.ShapeDtypeStruct((T_LAT, C_W), BF16),
        scratch_shapes=[pltpu.VMEM((2, HALF, NAT_KEYS), F32)],
        compiler_params=pltpu.CompilerParams(
            dimension_semantics=("arbitrary", "arbitrary", "arbitrary"), vmem_limit_bytes=VMEM_LIMIT),
        name="natten",
    )(qc, kc, vc, cache_k, cache_v, blocks)


def _natten_blocks(rpb):
    qcol = np.arange(GRID_W)[:, None]
    kcol = np.arange(GRID_W)[None, :]
    cs = np.clip(qcol - WIN_C // 2, 0, GRID_W - WIN_C)
    col_ok = (kcol >= cs) & (kcol < cs + WIN_C)
    dc = np.clip(kcol - qcol, -(WIN_C - 1), WIN_C - 1) + (WIN_C - 1)
    sel_c = (dc[:, :, None] == np.arange(2 * WIN_C - 1)).astype(np.float32)
    blk = jnp.einsum("hdm,qkm->hdqk", rpb, jnp.asarray(sel_c), precision=lax.Precision.HIGHEST)
    blk = jnp.where(jnp.asarray(col_ok), blk, NEG)
    blk = jnp.concatenate([blk, blk], axis=-1)
    return blk.reshape(H_C // 2, 2, 2 * WIN_R - 1, GRID_W, LANES)


def _route(lg):
    lane = lax.broadcasted_iota(jnp.int32, lg.shape, 1).astype(F32)
    low = jnp.float32(-3.0e38)
    far = jnp.float32(ROUTER_LANES)
    is_g = lane < N_GROUPS
    gmax = jnp.where(is_g, lg, low).max(axis=-1, keepdims=True)
    gsel = jnp.where(is_g & (lg == gmax), lane, far).min(axis=-1, keepdims=True)
    pg_sel = 1.0 / jnp.where(is_g, jnp.exp(lg - gmax), 0.0).sum(axis=-1, keepdims=True)
    e0 = EXPERT_LANE0 + EXPERTS_PER_GROUP * gsel
    in_g = (lane >= e0) & (lane < e0 + EXPERTS_PER_GROUP)
    m1 = jnp.where(in_g, lg, low).max(axis=-1, keepdims=True)
    i1 = jnp.where(in_g & (lg == m1), lane, far).min(axis=-1, keepdims=True)
    rest = in_g & (lane != i1)
    m2 = jnp.where(rest, lg, low).max(axis=-1, keepdims=True)
    i2 = jnp.where(rest & (lg == m2), lane, far).min(axis=-1, keepdims=True)
    t = jnp.exp(m2 - m1)
    ssum = 1.0 + t
    w1 = (1.0 / ssum) * pg_sel
    w2 = (t / ssum) * pg_sel
    return i1 - EXPERT_LANE0, i2 - EXPERT_LANE0, w1, w2


def _rows_to_groups(x, stage_ref, out_ref, n_rows):
    stride = n_rows + 8
    for c in range(GROUP_ROWS):
        stage_ref[c * stride:c * stride + n_rows, :] = x[:, c * LANES:(c + 1) * LANES]
    for j in range(n_rows):
        out_ref[GROUP_ROWS * j:GROUP_ROWS * (j + 1), :] = stage_ref[pl.ds(j, GROUP_ROWS, stride=stride), :]


def _tail_kernel(xc_ref, xl_ref, ya_ref, ybc_ref, ybl_ref, ycc_ref, ycl_ref, w_ref, g1_ref, sh2_ref, sc2_ref,
                 lg_ref, lb_ref, wr_ref, br_ref, x1_ref, h2g_ref, ids_ref, wts_ref, wbf_ref, stage_ref):
    @pl.when(pl.program_id(0) == 0)
    def _():
        wbf_ref[...] = w_ref[0].astype(BF16)

    o = (_dot(ya_ref[...], wbf_ref[0:POOL_WIDTH, :])
         + _dot(_split_rows(ybc_ref, ybl_ref), wbf_ref[POOL_WIDTH:POOL_WIDTH + QB_W, :])
         + _dot(_split_rows(ycc_ref, ycl_ref), wbf_ref[POOL_WIDTH + QB_W:, :]))
    x1 = _layer_norm(ALPHA * _split_rows(xc_ref, xl_ref) + g1_ref[0] * o, lg_ref[0], lb_ref[0])
    x1_ref[...] = x1
    h2 = x1 * (1.0 + sc2_ref[0]) + sh2_ref[0]
    hh, hl = _split_bf16(h2)
    wh, wl = _split_bf16(wr_ref[0])
    lg = _dot(hh, wh) + _dot(hl, wh) + _dot(hh, wl) + br_ref[0]
    i1, i2, w1, w2 = _route(lg)
    lane = lax.broadcasted_iota(jnp.int32, lg.shape, 1)
    ids = jnp.where(lane == 0, i1, jnp.where(lane == 1, i2, 0.0))
    ids_ref[0] = ids.T[0:8, :].astype(jnp.int32)
    wts = jnp.where(lane == 0, w1, jnp.where(lane == 1, w2, 0.0))
    wts_ref[0] = wts.T[0:8, :]
    _rows_to_groups(h2, stage_ref, h2g_ref, TT)


def _tail(layer, x_ctx, x_lat, ya, yb_ctx, yb_lat, yc_ctx, yc_lat, w_out, mod, ln_g, ln_b, wr, br):
    row = lambda w: pl.BlockSpec((TT, w), lambda i: (i, 0))
    return pl.pallas_call(
        _tail_kernel,
        grid=(T_ALL // TT,),
        in_specs=[*_split_row_specs(D_MODEL), row(POOL_WIDTH), *_split_row_specs(QB_W), *_split_row_specs(C_W),
                  _layer_spec(layer, (D_MODEL, D_MODEL)),
                  _mod_spec(layer, 2, TT), _mod_spec(layer, 3, TT), _mod_spec(layer, 4, TT),
                  _layer_spec(layer, (1, D_MODEL)), _layer_spec(layer, (1, D_MODEL)),
                  _layer_spec(layer, (D_MODEL, ROUTER_LANES)), _layer_spec(layer, (1, ROUTER_LANES))],
        out_specs=[row(D_MODEL), pl.BlockSpec((TT * GROUP_ROWS, LANES), lambda i: (i, 0)),
                   pl.BlockSpec((1, 8, TT), lambda i: (i, 0, 0)), pl.BlockSpec((1, 8, TT), lambda i: (i, 0, 0))],
        out_shape=[jax.ShapeDtypeStruct((T_ALL, D_MODEL), F32),
                   jax.ShapeDtypeStruct((T_ALL * GROUP_ROWS, LANES), F32),
                   jax.ShapeDtypeStruct((T_ALL // TT, 8, TT), jnp.int32),
                   jax.ShapeDtypeStruct((T_ALL // TT, 8, TT), F32)],
        scratch_shapes=[pltpu.VMEM((D_MODEL, D_MODEL), BF16),
                        pltpu.VMEM((GROUP_ROWS * (TT + 8), LANES), F32)],
        compiler_params=pltpu.CompilerParams(dimension_semantics=("arbitrary",), vmem_limit_bytes=VMEM_LIMIT),
        name="tail",
    )(x_ctx, x_lat, ya, yb_ctx, yb_lat, yc_ctx, yc_lat, w_out, mod, mod, mod, ln_g, ln_b, wr, br)


def _plan_kernel(eid_ref, pos_ref, tile_ref):
    e = eid_ref[0]
    r = lax.broadcasted_iota(jnp.int32, (LANES, LANES), 0)
    c = lax.broadcasted_iota(jnp.int32, (LANES, LANES), 1)
    upper = (r <= c).astype(BF16)
    rows = e.shape[0]
    lower = (lax.broadcasted_iota(jnp.int32, (rows, rows), 1)
             < lax.broadcasted_iota(jnp.int32, (rows, rows), 0)).astype(BF16)
    tile_start = (lax.broadcasted_iota(jnp.int32, (1, LANES), 1) * TMX).astype(F32)
    lane = lax.broadcasted_iota(jnp.int32, (1, LANES), 1)
    pos = jnp.zeros(e.shape, F32)
    base = jnp.zeros((1, LANES), F32)
    n_before = jnp.zeros((1, LANES), F32)
    valid_end = jnp.zeros((1, LANES), F32)
    pad_end = jnp.zeros((1, LANES), F32)
    for ex in range(N_EXPERTS):
        m = e == ex
        incl = _dot(m.astype(BF16), upper)
        row_tot = jnp.broadcast_to(incl[:, LANES - 1:LANES], incl.shape)
        row_off = _dot(lower, row_tot.astype(BF16))
        cnt = jnp.sum(row_tot, axis=0, keepdims=True)
        pos = jnp.where(m, base + row_off + incl - 1.0, pos)
        valid_end = jnp.where(lane == ex, base + cnt, valid_end)
        base = base + jnp.ceil(cnt * (1.0 / TMX)) * TMX
        pad_end = jnp.where(lane == ex, base, pad_end)
        n_before = n_before + (tile_start >= base).astype(F32)
    pos_ref[0] = pos.astype(jnp.int32)
    sub = lax.broadcasted_iota(jnp.int32, (8, LANES), 0)
    rows8 = lambda v: jnp.broadcast_to(v, (8, LANES))
    table = jnp.where(sub == PLAN_TILE_EXPERT, rows8(jnp.minimum(n_before, N_EXPERTS - 1.0)),
                      jnp.where(sub == PLAN_N_TILES, rows8(base * (1.0 / TMX)),
                                jnp.where(sub == PLAN_VALID_END, rows8(valid_end),
                                          jnp.where(sub == PLAN_PAD_END, rows8(pad_end), 0.0))))
    tile_ref[0] = table.astype(jnp.int32)


def _plan(eid):
    rows = 2 * T_STREAM // LANES
    return pl.pallas_call(
        _plan_kernel,
        grid=(N_STREAMS,),
        in_specs=[pl.BlockSpec((1, rows, LANES), lambda s: (s, 0, 0))],
        out_specs=[pl.BlockSpec((1, rows, LANES), lambda s: (s, 0, 0)), pl.BlockSpec((1, 8, LANES), lambda s: (s, 0, 0))],
        out_shape=[jax.ShapeDtypeStruct((N_STREAMS, rows, LANES), jnp.int32),
                   jax.ShapeDtypeStruct((N_STREAMS, 8, LANES), jnp.int32)],
        compiler_params=pltpu.CompilerParams(dimension_semantics=("arbitrary",), vmem_limit_bytes=VMEM_LIMIT),
        name="plan",
    )(eid)


def _moe_kernel(pos_ref, wpair_ref, plan_ref, h2g_hbm, wg_ref, wu_ref, wd_ref, y_hbm,
                h2v, yv, off, xg0, xg1, yg0, yg1, wgb, wub, wdb, sem):
    s = pl.program_id(0)
    e = pl.program_id(1)
    n_pairs = 2 * T_STREAM
    spare_off = T_STREAM * GROUP_ROWS
    plan = lambda row, idx: plan_ref[(s * 8 + row) * LANES + idx]
    n_tiles = plan(PLAN_N_TILES, 0)
    last = n_tiles - 1

    xgs = (xg0, xg1)
    ygs = (yg0, yg1)

    def row_offset(pair):
        return pl.multiple_of((pair & (SLOT_PITCH - 1)) << 3, GROUP_ROWS)

    def gather(tile, dst):
        base = tile * TMX
        for j in range(TMX):
            dst[pl.ds(j, GROUP_ROWS, stride=XSTRIDE), :] = h2v[pl.ds(row_offset(off[base + j]), GROUP_ROWS), :]

    def h2_copy():
        return pltpu.make_async_copy(
            h2g_hbm.at[pl.ds(pl.multiple_of(s * (T_STREAM * GROUP_ROWS), 8), T_STREAM * GROUP_ROWS)],
            h2v.at[pl.ds(0, T_STREAM * GROUP_ROWS)], sem.at[0])

    def y_copy():
        return pltpu.make_async_copy(yv.at[pl.ds(0, T_STREAM * GROUP_ROWS)], y_hbm.at[s], sem.at[1])

    @pl.when(e == 0)
    def _():
        h2_copy().start()
        yv[...] = jnp.zeros_like(yv)
        h2v[spare_off:spare_off + GROUP_ROWS, :] = jnp.zeros((GROUP_ROWS, LANES), F32)

        def pad_expert(ex, carry):
            lo = plan(PLAN_VALID_END, ex)

            def pad_chunk(k, c):
                for u in range(PAD_CHUNK):
                    off[lo + k * PAD_CHUNK + u] = T_STREAM
                return c
            n_chunks = (plan(PLAN_PAD_END, ex) - lo + (PAD_CHUNK - 1)) // PAD_CHUNK
            return lax.fori_loop(0, n_chunks, pad_chunk, carry)
        lax.fori_loop(0, N_EXPERTS, pad_expert, 0)

        def place(i, carry):
            for u in range(PLACE_UNROLL):
                tok = i * PLACE_UNROLL + u
                for slot in range(2):
                    off[pos_ref[s * n_pairs + slot * T_STREAM + tok]] = slot * SLOT_PITCH + tok
            return carry
        lax.fori_loop(0, T_STREAM // PLACE_UNROLL, place, 0)
        yg1[...] = jnp.zeros_like(yg1)
        h2_copy().wait()
        gather(0, xg0)

    t_lo = jnp.where(e == 0, 0, plan(PLAN_PAD_END, jnp.maximum(e - 1, 0))) // TMX
    t_hi = plan(PLAN_PAD_END, e) // TMX

    @pl.when(t_hi > t_lo)
    def _():
        wgb[...] = wg_ref[0, 0].astype(BF16)
        wub[...] = wu_ref[0, 0].astype(BF16)
        wdb[...] = wd_ref[0, 0].astype(BF16)

    def add_tile(tile, done):
        base = tile * TMX
        for j0 in range(0, TMX, RMW_BATCH):
            updates = []
            for j in range(j0, j0 + RMW_BATCH):
                pair = off[base + j]
                dst = row_offset(pair)
                gate = wpair_ref[s * (2 * SLOT_PITCH) + pair]
                row = done[pl.ds(j, GROUP_ROWS, stride=XSTRIDE), :]
                updates.append((dst, yv[pl.ds(dst, GROUP_ROWS), :] + gate * row))
            for dst, val in updates:
                yv[pl.ds(dst, GROUP_ROWS), :] = val

    def step(t, cur, nxt, out, done):
        gather(jnp.minimum(t + 1, last), nxt)
        x = jnp.concatenate([cur[c * XSTRIDE:c * XSTRIDE + TMX, :] for c in range(GROUP_ROWS)], axis=1).astype(BF16)
        hg = _dot(x, wgb[...])
        hu = _dot(x, wub[...])
        act = hg * jax.nn.sigmoid(hg) * hu
        y = _dot(act.astype(BF16), wdb[...])
        for c in range(GROUP_ROWS):
            out[c * XSTRIDE:c * XSTRIDE + TMX, :] = y[:, c * LANES:(c + 1) * LANES]
        add_tile(jnp.maximum(t - 1, 0), done)

    def tile_body(t, carry):
        for par in range(2):
            @pl.when((t & 1) == par)
            def _():
                step(t, xgs[par], xgs[1 - par], ygs[par], ygs[1 - par])
        return carry
    lax.fori_loop(t_lo, t_hi, tile_body, 0)

    @pl.when(e == N_EXPERTS - 1)
    def _():
        for par in range(2):
            @pl.when((last & 1) == par)
            def _():
                add_tile(last, ygs[par])
        y_copy().start()
        y_copy().wait()


def _moe(layer, pos, wpair, plan, h2g, w_gate, w_up, w_down):
    wspec = lambda shape: pl.BlockSpec((1,) + shape, lambda s, e, pos, wpair, plan: (layer, e, 0, 0))
    return pl.pallas_call(
        _moe_kernel,
        grid_spec=pltpu.PrefetchScalarGridSpec(
            num_scalar_prefetch=3,
            grid=(N_STREAMS, N_EXPERTS),
            in_specs=[pl.BlockSpec(memory_space=pl.ANY),
                      wspec((1, D_MODEL, D_EXPERT)), wspec((1, D_MODEL, D_EXPERT)), wspec((1, D_EXPERT, D_MODEL))],
            out_specs=pl.BlockSpec(memory_space=pl.ANY),
            scratch_shapes=[pltpu.VMEM(((T_STREAM + 1) * GROUP_ROWS, LANES), F32),
                            pltpu.VMEM(((T_STREAM + 1) * GROUP_ROWS, LANES), F32),
                            pltpu.SMEM((P_MAX + PAD_CHUNK,), jnp.int32),
                            pltpu.VMEM((GROUP_ROWS * XSTRIDE, LANES), F32),
                            pltpu.VMEM((GROUP_ROWS * XSTRIDE, LANES), F32),
                            pltpu.VMEM((GROUP_ROWS * XSTRIDE, LANES), F32),
                            pltpu.VMEM((GROUP_ROWS * XSTRIDE, LANES), F32),
                            pltpu.VMEM((D_MODEL, D_EXPERT), BF16), pltpu.VMEM((D_MODEL, D_EXPERT), BF16),
                            pltpu.VMEM((D_EXPERT, D_MODEL), BF16),
                            pltpu.SemaphoreType.DMA((2,))]),
        out_shape=jax.ShapeDtypeStruct((N_STREAMS, T_STREAM * GROUP_ROWS, LANES), F32),
        compiler_params=pltpu.CompilerParams(
            dimension_semantics=("arbitrary", "arbitrary"), vmem_limit_bytes=VMEM_LIMIT),
        name="moe",
    )(pos, wpair, plan, h2g, w_gate, w_up, w_down)


def _groups_to_rows(src_ref, stage_ref, n_rows):
    stride = n_rows + 8
    for j in range(n_rows):
        stage_ref[pl.ds(j, GROUP_ROWS, stride=stride), :] = src_ref[GROUP_ROWS * j:GROUP_ROWS * (j + 1), :]
    return jnp.concatenate([stage_ref[c * stride:c * stride + n_rows, :] for c in range(GROUP_ROWS)], axis=1)


def _final_kernel(x1_ref, y_ref, g2_ref, lg_ref, lb_ref, o_ref, stage_ref):
    moe = _groups_to_rows(y_ref, stage_ref, TT)
    o_ref[...] = _layer_norm(ALPHA * x1_ref[...] + g2_ref[0] * moe, lg_ref[0], lb_ref[0])


def _final(layer, stream, x1, y, mod, ln_g, ln_b):
    first = stream * (T_STREAM // TT)
    g2 = pl.BlockSpec(
        (1, 1, D_MODEL), lambda i: ((layer * MOD_ROWS + _mod_row_of_tile(first + i, TT)) * 6 + 5, 0, 0))
    return pl.pallas_call(
        _final_kernel,
        grid=(T_STREAM // TT,),
        in_specs=[pl.BlockSpec((TT, D_MODEL), lambda i: (first + i, 0)),
                  pl.BlockSpec((TT * GROUP_ROWS, LANES), lambda i: (first + i, 0)),
                  g2, _layer_spec(layer, (1, D_MODEL)), _layer_spec(layer, (1, D_MODEL))],
        out_specs=pl.BlockSpec((TT, D_MODEL), lambda i: (i, 0)),
        out_shape=jax.ShapeDtypeStruct((T_STREAM, D_MODEL), F32),
        scratch_shapes=[pltpu.VMEM((GROUP_ROWS * (TT + 8), LANES), F32)],
        compiler_params=pltpu.CompilerParams(dimension_semantics=("arbitrary",), vmem_limit_bytes=VMEM_LIMIT),
        name="final",
    )(x1, y, mod, ln_g, ln_b)


def _rope_tables():
    t = np.arange(DEC_SEQ)
    pos = np.stack([t // GRID_W, t % GRID_W], axis=1).astype(np.float32)
    nf = HEAD_DIM // 4
    inv = jnp.asarray(ROPE_THETA, F32) ** (-jnp.arange(nf, dtype=F32) / nf)
    d = np.arange(LANES) % HEAD_DIM
    which = d // (HEAD_DIM // 2)
    ang = jnp.asarray(pos)[:, which] * inv[d % nf][None, :]
    sign = np.where((d % 32) < 16, -1.0, 1.0).astype(np.float32)
    cos = jnp.concatenate([jnp.cos(ang), jnp.ones((TT, LANES), F32)], axis=0)
    sin = jnp.concatenate([jnp.sin(ang) * sign[None, :], jnp.zeros((TT, LANES), F32)], axis=0)
    return cos, sin


def _block_ones():
    h = np.arange(QB_W) // HEAD_DIM
    return jnp.asarray((h[:, None] == h[None, :]).astype(np.float32), dtype=BF16)


def _pool_block_diag(pool_w):
    out = jnp.zeros((POOL_WIDTH, POOL_WIDTH), F32)
    for g in range(4):
        out = out.at[64 * g:64 * g + 64, 64 * g:64 * g + 64].set(pool_w[g])
    return out


def kernel(x_prompt, x_sample, cache_b_k, cache_b_v, cache_c_k, cache_c_v, c, c_ctx, w_ada, b_ada, w_in, w_out,
           pool_w, pool_scale, q_norm, k_norm, rpb, ln1_g, ln1_b, ln2_g, ln2_b, router_g, router_g_b, router_e,
           router_e_b, w_gate, w_up, w_down):
    x_ctx = x_prompt.reshape(T_CTX, D_MODEL)
    x_lat = x_sample.reshape(T_LAT, D_MODEL)
    c8 = jnp.concatenate([c, c_ctx[None], jnp.zeros((MOD_ROWS - DEC_BATCH - 1, D_MODEL), F32)], axis=0)
    mod = _adaln(c8, w_ada, b_ada).reshape(DEPTH * MOD_ROWS * 6, 1, D_MODEL)

    cos_t, sin_t = _rope_tables()
    ones_bd = _block_ones()
    cbk = cache_b_k.reshape(DEC_BATCH, DEPTH, PAST_LEN, KB_W)
    cbv = cache_b_v.reshape(DEC_BATCH, DEPTH, PAST_LEN, KB_W)
    cck = cache_c_k.reshape(DEC_BATCH, DEPTH, PAST_LEN, C_W)
    ccv = cache_c_v.reshape(DEC_BATCH, DEPTH, PAST_LEN, C_W)
    pad = jnp.zeros((DEPTH, D_MODEL, ROUTER_LANES - N_GROUPS - N_EXPERTS), F32)
    wr = jnp.concatenate([router_g, router_e, pad], axis=2)
    br = jnp.concatenate([router_g_b, router_e_b, pad[:, 0]], axis=1)[:, None]
    per_layer = lambda p: p[:, None]

    caches = [jnp.zeros((BATCH, DEPTH, w, SEQ), F32) for w in CACHE_WIDTHS]
    for l in range(DEPTH):
        qn = jnp.tile(q_norm[l], H_B)[None]
        kn = jnp.tile(k_norm[l], KV_B)[None]
        a, qb, kb, vb, qc, kc, vc = _proj(l, x_ctx, x_lat, mod, w_in, ones_bd, qn, kn, cos_t, sin_t)
        ya = _pool(a, _pool_block_diag(pool_w[l]), pool_scale[l][None])
        yb_ctx, yc_ctx, *caches = _attn_ctx(l, qb, kb, vb, qc, kc, vc, caches)
        yb_lat = _attn_latb(l, qb, kb, vb, cbk, cbv)
        yc_lat = _natten(l, qc, kc, vc, cck, ccv, _natten_blocks(rpb[l]))
        x1, h2g, ids, wts = _tail(l, x_ctx, x_lat, ya, yb_ctx, yb_lat, yc_ctx, yc_lat, w_out, mod,
                                  per_layer(ln1_g), per_layer(ln1_b), wr, br)
        pairs = lambda a: a.reshape(N_STREAMS, T_STREAM // TT, 8, TT)[:, :, :2, :].transpose(0, 2, 1, 3)
        pos, tiles = _plan(pairs(ids).reshape(N_STREAMS, 2 * T_STREAM // LANES, LANES))
        gates = jnp.pad(pairs(wts).reshape(N_STREAMS, 2, T_STREAM), ((0, 0), (0, 0), (0, SLOT_PITCH - T_STREAM)))
        y = _moe(l, pos.reshape(-1), gates.reshape(-1), tiles.reshape(-1), h2g, w_gate, w_up, w_down)
        y = y.reshape(T_ALL * GROUP_ROWS, LANES)
        x_ctx = _final(l, 0, x1, y, mod, per_layer(ln2_g), per_layer(ln2_b))
        x_lat = _final(l, 1, x1, y, mod, per_layer(ln2_g), per_layer(ln2_b))

    y_prompt = x_ctx.reshape(BATCH, SEQ, D_MODEL)
    y_sample = x_lat.reshape(DEC_BATCH, DEC_SEQ, D_MODEL)
    heads_last = lambda t, h: t.reshape(BATCH, DEPTH, h, HEAD_DIM, SEQ).transpose(0, 1, 4, 2, 3)
    new_bk, new_bv, new_ck, new_cv = caches
    return (y_prompt, y_sample, heads_last(new_bk, KV_B), heads_last(new_bv, KV_B),
            heads_last(new_ck, H_C), heads_last(new_cv, H_C))
```

```python
import functools

import numpy as np
import jax
import jax.numpy as jnp
from jax import lax
from jax.experimental import pallas as pl
from jax.experimental.pallas import tpu as pltpu

F32 = jnp.float32
BF16 = jnp.bfloat16

D_MODEL = 1024
BATCH = 16
SEQ = 256
DEPTH = 2
DEC_BATCH = 4
DEC_SEQ = 1024
PAST_LEN = 256
GRID_W = 64
GRID_ROWS = DEC_SEQ // GRID_W
HEAD_DIM = 64
POOL_WIDTH = 256
POOL_WINDOWS = (2, 4, 8, 16)
H_B = 6
KV_B = 2
H_C = 6
WIN_R = 8
WIN_C = 16
ROPE_THETA = 10000.0
QB_W = H_B * HEAD_DIM
KB_W = KV_B * HEAD_DIM
C_W = H_C * HEAD_DIM
PROJ_WIDTH = 2048
N_GROUPS = 4
EXPERTS_PER_GROUP = 8
N_EXPERTS = 32
D_EXPERT = 256
ALPHA = (2 * DEPTH) ** 0.25
LN_EPS = 1e-6
RMS_EPS = 1e-6
NEG = -1e30
ATTN_SCALE = HEAD_DIM ** -0.5

T_CTX = BATCH * SEQ
T_LAT = DEC_BATCH * DEC_SEQ
T_ALL = T_CTX + T_LAT

LANES = 128
ROUTER_LANES = 128
EXPERT_LANE0 = N_GROUPS
MOD_ROWS = 8
CTX_MOD_ROW = DEC_BATCH

TT = 512
TP = 1024
POOL_GAP = 16
N_STREAMS = 2
T_STREAM = T_ALL // N_STREAMS
TMX = 128
PLACE_UNROLL = 8
PAD_CHUNK = 8
SLOT_PITCH = 2 * T_STREAM
P_MAX = 2 * T_STREAM + N_EXPERTS * TMX
NT_MAX = P_MAX // TMX
GROUP_ROWS = D_MODEL // LANES
XSTRIDE = TMX + 8
RMW_BATCH = 16
PLAN_TILE_EXPERT, PLAN_N_TILES, PLAN_VALID_END, PLAN_PAD_END = 0, 1, 2, 3
HALF = DEC_SEQ // 2
NAT_KEYS = 12 * GRID_W
VMEM_LIMIT = 56 * 1024 * 1024


def _dot(a, b):
    return jnp.dot(a, b, preferred_element_type=F32)


def _dot_nt(a, b):
    return lax.dot_general(a, b, (((1,), (1,)), ((), ())), preferred_element_type=F32)


def _split_bf16(x):
    hi = x.astype(BF16)
    lo = (x - hi.astype(F32)).astype(BF16)
    return hi, lo


def _layer_norm(y, g, b):
    mu = jnp.mean(y, axis=-1, keepdims=True)
    var = jnp.mean(jnp.square(y - mu), axis=-1, keepdims=True)
    return (y - mu) * lax.rsqrt(var + LN_EPS) * g + b


def _adaln_kernel(c_ref, w_ref, b_ref, o_ref):
    c = c_ref[...]
    s = (c * jax.nn.sigmoid(c)).astype(BF16)
    o_ref[0] = _dot(s, w_ref[0].astype(BF16)) + b_ref[0]


def _adaln(c8, w_ada, b_ada):
    tn = 1536
    n = w_ada.shape[-1]
    return pl.pallas_call(
        _adaln_kernel,
        grid=(DEPTH, n // tn),
        in_specs=[
            pl.BlockSpec((MOD_ROWS, D_MODEL), lambda l, j: (0, 0)),
            pl.BlockSpec((1, D_MODEL, tn), lambda l, j: (l, 0, j)),
            pl.BlockSpec((1, 1, tn), lambda l, j: (l, 0, j)),
        ],
        out_specs=pl.BlockSpec((1, MOD_ROWS, tn), lambda l, j: (l, 0, j)),
        out_shape=jax.ShapeDtypeStruct((DEPTH, MOD_ROWS, n), F32),
        compiler_params=pltpu.CompilerParams(
            dimension_semantics=("arbitrary", "arbitrary"), vmem_limit_bytes=VMEM_LIMIT),
        name="adaln",
    )(c8, w_ada, b_ada.reshape(DEPTH, 1, n))


def _mod_row_of_tile(i, tile):
    n_ctx = T_CTX // tile
    per_req = DEC_SEQ // tile
    return jnp.where(i < n_ctx, CTX_MOD_ROW, (i - n_ctx) // per_req)


def _mod_spec(layer, chunk, tile):
    return pl.BlockSpec(
        (1, 1, D_MODEL), lambda i: ((layer * MOD_ROWS + _mod_row_of_tile(i, tile)) * 6 + chunk, 0, 0))


def _layer_spec(layer, shape):
    return pl.BlockSpec((1,) + shape, lambda *_: (layer,) + (0,) * len(shape))


N_CTX_TILES = T_CTX // TT


def _split_row_specs(width):
    return (pl.BlockSpec((TT, width), lambda i: (jnp.minimum(i, N_CTX_TILES - 1), 0)),
            pl.BlockSpec((TT, width), lambda i: (jnp.maximum(i - N_CTX_TILES, 0), 0)))


def _split_rows(ctx_ref, lat_ref):
    return jnp.where(pl.program_id(0) < N_CTX_TILES, ctx_ref[...], lat_ref[...])


def _rms_norm_heads(x, ones_bd, w):
    hi, lo = _split_bf16(x * x)
    ssq = _dot(hi, ones_bd) + _dot(lo, ones_bd)
    return x * lax.rsqrt(ssq * (1.0 / HEAD_DIM) + RMS_EPS) * w


def _rope_cols(x, cos, sin, first16):
    cols = []
    for j in range(x.shape[1] // LANES):
        xc = x[:, j * LANES:(j + 1) * LANES]
        partner = jnp.where(first16, pltpu.roll(xc, LANES - 16, axis=1), pltpu.roll(xc, 16, axis=1))
        cols.append(xc * cos + partner * sin)
    return jnp.concatenate(cols, axis=1) if len(cols) > 1 else cols[0]


def _proj_kernel(xc_ref, xl_ref, sh_ref, sc_ref, w_ref, ones_ref, qn_ref, kn_ref, cos_ref, sin_ref,
                 a_ref, qb_ref, kb_ref, vb_ref, qc_ref, kc_ref, vc_ref, wbf_ref):
    @pl.when(pl.program_id(0) == 0)
    def _():
        wbf_ref[...] = w_ref[0].astype(BF16)

    h = _split_rows(xc_ref, xl_ref) * (1.0 + sc_ref[0]) + sh_ref[0]
    p = _dot(h.astype(BF16), wbf_ref[...])
    o = 0
    a_ref[...] = p[:, o:o + POOL_WIDTH]; o += POOL_WIDTH
    qb = p[:, o:o + QB_W]; o += QB_W
    kb = p[:, o:o + KB_W]; o += KB_W
    vb_ref[...] = p[:, o:o + KB_W]; o += KB_W
    qc_ref[...] = (p[:, o:o + C_W] * ATTN_SCALE).astype(BF16); o += C_W
    kc_ref[...] = p[:, o:o + C_W]; o += C_W
    vc_ref[...] = p[:, o:o + C_W]

    ones_bd = ones_ref[...]
    cos = cos_ref[...]
    sin = sin_ref[...]
    lane = lax.broadcasted_iota(jnp.int32, (1, LANES), 1)
    first16 = (lane & 31) < 16
    qb = _rope_cols(_rms_norm_heads(qb, ones_bd, qn_ref[...]), cos, sin, first16)
    kb = _rope_cols(_rms_norm_heads(kb, ones_bd[:KB_W, :KB_W], kn_ref[...]), cos, sin, first16)
    qb_ref[...] = (qb * ATTN_SCALE).astype(BF16)
    kb_ref[...] = kb


def _proj(layer, x_ctx, x_lat, mod, w_in, ones_bd, qn, kn, cos_t, sin_t):
    n_tiles = T_ALL // TT
    n_ctx = N_CTX_TILES
    per_req = DEC_SEQ // TT

    def rope_idx(i):
        return (jnp.where(i < n_ctx, per_req, (i - n_ctx) % per_req), 0)

    row = lambda w: pl.BlockSpec((TT, w), lambda i: (i, 0))
    const = lambda s: pl.BlockSpec(s, lambda i: (0,) * len(s))
    widths = (POOL_WIDTH, QB_W, KB_W, KB_W, C_W, C_W, C_W)
    dtypes = (F32, BF16, F32, F32, BF16, F32, F32)
    return pl.pallas_call(
        _proj_kernel,
        grid=(n_tiles,),
        in_specs=[*_split_row_specs(D_MODEL), _mod_spec(layer, 0, TT), _mod_spec(layer, 1, TT),
                  _layer_spec(layer, (D_MODEL, PROJ_WIDTH)),
                  const((QB_W, QB_W)), const((1, QB_W)), const((1, KB_W)),
                  pl.BlockSpec((TT, LANES), rope_idx), pl.BlockSpec((TT, LANES), rope_idx)],
        out_specs=[row(w) for w in widths],
        out_shape=[jax.ShapeDtypeStruct((T_ALL, w), d) for w, d in zip(widths, dtypes)],
        scratch_shapes=[pltpu.VMEM((D_MODEL, PROJ_WIDTH), BF16)],
        compiler_params=pltpu.CompilerParams(dimension_semantics=("arbitrary",), vmem_limit_bytes=VMEM_LIMIT),
        name="proj",
    )(x_ctx, x_lat, mod, mod, w_in, ones_bd, qn, kn, cos_t, sin_t)


def _pool_windows(p):
    n = p.shape[0]
    sh = lambda x, k: pltpu.roll(x, (n - k) % n, axis=0)
    w2 = p + sh(p, -1)
    w4 = sh(w2, -1) + sh(w2, 1)
    w8 = sh(w4, -2) + sh(w4, 2)
    w16 = sh(w8, -4) + sh(w8, 4)
    g = lax.broadcasted_iota(jnp.int32, p.shape, 1) >> 6
    return jnp.where(g == 0, w2, jnp.where(g == 1, w4, jnp.where(g == 2, w8, w16)))


def _pool_kernel(a_ref, w_ref, scale_ref, o_ref, pad_ref):
    a = a_ref[...]

    def mix(csum, seq_len):
        row = lax.broadcasted_iota(jnp.int32, a.shape, 0)
        g = lax.broadcasted_iota(jnp.int32, a.shape, 1) >> 6
        t = row & (seq_len - 1)
        half = jnp.where(g == 0, 1, jnp.where(g == 1, 2, jnp.where(g == 2, 4, 8)))
        cnt = jnp.minimum(t + half, seq_len) - jnp.maximum(t - half, 0)
        pooled = csum / cnt.astype(F32) - a
        y = _dot(pooled.astype(BF16), w_ref[...].astype(BF16)) * scale_ref[...]
        o_ref[...] = y.astype(BF16)

    def padded_sums(seq_len):
        pitch = seq_len + POOL_GAP
        n_req = TP // seq_len
        n = n_req * pitch
        pad_ref[...] = jnp.zeros_like(pad_ref)
        for r in range(n_req):
            pad_ref[r * pitch + POOL_GAP // 2:r * pitch + POOL_GAP // 2 + seq_len, :] = a[r * seq_len:(r + 1) * seq_len]
        sums = _pool_windows(pad_ref[0:n, :])
        return jnp.concatenate(
            [sums[r * pitch + POOL_GAP // 2:r * pitch + POOL_GAP // 2 + seq_len] for r in range(n_req)], axis=0)

    is_ctx = pl.program_id(0) < T_CTX // TP

    @pl.when(is_ctx)
    def _():
        mix(padded_sums(SEQ), SEQ)

    @pl.when(jnp.logical_not(is_ctx))
    def _():
        mix(padded_sums(DEC_SEQ), DEC_SEQ)


def _pool(a, w_bd, scale):
    return pl.pallas_call(
        _pool_kernel,
        grid=(T_ALL // TP,),
        in_specs=[pl.BlockSpec((TP, POOL_WIDTH), lambda i: (i, 0)),
                  pl.BlockSpec((POOL_WIDTH, POOL_WIDTH), lambda i: (0, 0)),
                  pl.BlockSpec((1, POOL_WIDTH), lambda i: (0, 0))],
        out_specs=pl.BlockSpec((TP, POOL_WIDTH), lambda i: (i, 0)),
        out_shape=jax.ShapeDtypeStruct((T_ALL, POOL_WIDTH), BF16),
        scratch_shapes=[pltpu.VMEM(((TP // SEQ) * (SEQ + POOL_GAP), POOL_WIDTH), F32)],
        compiler_params=pltpu.CompilerParams(dimension_semantics=("arbitrary",), vmem_limit_bytes=VMEM_LIMIT),
        name="pool",
    )(a, w_bd, scale)


def _softmax_pv(scores, values):
    m = scores[0].max(axis=-1, keepdims=True)
    for s in scores[1:]:
        m = jnp.maximum(m, s.max(axis=-1, keepdims=True))
    ps = [jnp.exp(s - m) for s in scores]
    l = ps[0].sum(axis=-1, keepdims=True)
    for p in ps[1:]:
        l = l + p.sum(axis=-1, keepdims=True)
    o = _dot(ps[0].astype(BF16), values[0])
    for p, v in zip(ps[1:], values[1:]):
        o = o + _dot(p.astype(BF16), v)
    return o * (1.0 / l)


def _lane_halves():
    lane = lax.broadcasted_iota(jnp.int32, (1, LANES), 1)
    return lane < HEAD_DIM, lane >= HEAD_DIM


def _keep(x, mask):
    return jnp.where(mask, x, 0.0).astype(BF16)


def _gqa_variants(x):
    lo, hi = _lane_halves()
    xs = pltpu.roll(x, HEAD_DIM, axis=1)
    nat_lo, nat_hi = _keep(x, lo), _keep(x, hi)
    sw_lo, sw_hi = _keep(xs, lo), _keep(xs, hi)
    return ((nat_lo, sw_hi), (nat_lo, nat_hi), (sw_lo, nat_hi))


def _mha_variants(x):
    lo, hi = _lane_halves()
    out = []
    for j in range(x.shape[1] // LANES):
        xc = x[:, j * LANES:(j + 1) * LANES]
        out.append((_keep(xc, lo), _keep(xc, hi)))
    return tuple(out)


def _attend_cols(q, k_vars, v_vars, extra_k=None, extra_v=None, bias=None):
    cols = []
    for j in range(q.shape[1] // LANES):
        qc = q[:, j * LANES:(j + 1) * LANES]
        o = None
        for hh in range(2):
            s = _dot_nt(qc, k_vars[j][hh])
            if bias is not None:
                s = s + bias[j][hh]
            scores, values = [s], [v_vars[j][hh]]
            if extra_k is not None:
                scores.append(_dot_nt(qc, extra_k[j][hh]))
                values.append(extra_v[j][hh])
            oh = _softmax_pv(scores, values)
            o = oh if o is None else o + oh
        cols.append(o)
    return jnp.concatenate(cols, axis=1)


def _attn_ctx_kernel(qb_ref, kb_ref, vb_ref, qc_ref, kc_ref, vc_ref, *refs):
    yb_ref, yc_ref, kbt_ref, vbt_ref, kct_ref, vct_ref = refs[-6:]
    kb, vb, kc, vc = kb_ref[...], vb_ref[...], kc_ref[...], vc_ref[...]
    yb = _attend_cols(qb_ref[...], _gqa_variants(kb), _gqa_variants(vb))
    yb_ref[...] = yb.astype(BF16)
    yc = _attend_cols(qc_ref[...], _mha_variants(kc), _mha_variants(vc))
    yc_ref[...] = yc.astype(BF16)
    kbt_ref[0, 0] = kb.T
    vbt_ref[0, 0] = vb.T
    kct_ref[0, 0] = kc.T
    vct_ref[0, 0] = vc.T


CACHE_WIDTHS = (KB_W, KB_W, C_W, C_W)


def _attn_ctx(layer, qb, kb, vb, qc, kc, vc, caches):
    row = lambda w: pl.BlockSpec((SEQ, w), lambda i: (i, 0))
    cache = lambda w: pl.BlockSpec((1, 1, w, SEQ), lambda i: (i, layer, 0, 0))
    return pl.pallas_call(
        _attn_ctx_kernel,
        grid=(BATCH,),
        in_specs=[row(QB_W), row(KB_W), row(KB_W), row(C_W), row(C_W), row(C_W)]
                 + [pl.BlockSpec(memory_space=pl.ANY)] * len(caches),
        out_specs=[row(QB_W), row(C_W)] + [cache(w) for w in CACHE_WIDTHS],
        out_shape=[jax.ShapeDtypeStruct((T_CTX, QB_W), BF16), jax.ShapeDtypeStruct((T_CTX, C_W), BF16)]
                  + [jax.ShapeDtypeStruct((BATCH, DEPTH, w, SEQ), F32) for w in CACHE_WIDTHS],
        input_output_aliases={6 + k: 2 + k for k in range(len(caches))},
        compiler_params=pltpu.CompilerParams(dimension_semantics=("arbitrary",), vmem_limit_bytes=VMEM_LIMIT),
        name="attn_ctx",
    )(qb, kb, vb, qc, kc, vc, *caches)


def _attn_latb_kernel(q_ref, k_ref, v_ref, ck_ref, cv_ref, y_ref):
    y = _attend_cols(q_ref[...], _gqa_variants(ck_ref[0, 0]), _gqa_variants(cv_ref[0, 0]),
                     extra_k=_gqa_variants(k_ref[...]), extra_v=_gqa_variants(v_ref[...]))
    y_ref[...] = y.astype(BF16)


def _attn_latb(layer, qb, kb, vb, cache_k, cache_v):
    ctx_h = T_CTX // HALF
    ctx_r = T_CTX // DEC_SEQ
    cache = pl.BlockSpec((1, 1, PAST_LEN, KB_W), lambda b, s: (b, layer, 0, 0))
    own = pl.BlockSpec((DEC_SEQ, KB_W), lambda b, s: (ctx_r + b, 0))
    return pl.pallas_call(
        _attn_latb_kernel,
        grid=(DEC_BATCH, DEC_SEQ // HALF),
        in_specs=[pl.BlockSpec((HALF, QB_W), lambda b, s: (ctx_h + 2 * b + s, 0)), own, own, cache, cache],
        out_specs=pl.BlockSpec((HALF, QB_W), lambda b, s: (2 * b + s, 0)),
        out_shape=jax.ShapeDtypeStruct((T_LAT, QB_W), BF16),
        compiler_params=pltpu.CompilerParams(
            dimension_semantics=("arbitrary", "arbitrary"), vmem_limit_bytes=VMEM_LIMIT),
        name="attn_latb",
    )(qb, kb, vb, cache_k, cache_v)


def _natten_window(s, i):
    r = (HALF // GRID_W) * s + i
    rs = min(max(r - WIN_R // 2, 0), GRID_ROWS - WIN_R)
    return r, rs


def _natten_kernel(q_ref, k_ref, v_ref, ck_ref, cv_ref, blk_ref, y_ref, bias_ref):
    s = pl.program_id(1)

    for sv in range(2):
        @pl.when((pl.program_id(2) == 0) & (s == sv))
        def _():
            lo_half, _ = _lane_halves()
            masked = jnp.full((GRID_W, LANES), NEG, F32)
            for hh in range(2):
                for i in range(HALF // GRID_W):
                    r, rs = _natten_window(sv, i)
                    for jp in range(NAT_KEYS // LANES):
                        pair = []
                        for j in (2 * jp, 2 * jp + 1):
                            rk = 4 * sv + j
                            pair.append(blk_ref[0, hh, rk - r + WIN_R - 1] if rs <= rk < rs + WIN_R else masked)
                        bias_ref[hh, i * GRID_W:(i + 1) * GRID_W, jp * LANES:(jp + 1) * LANES] = (
                            jnp.where(lo_half, pair[0], pair[1]))

    start = pl.multiple_of(s * (4 * GRID_W), 4 * GRID_W)
    k = _mha_variants(k_ref[pl.ds(start, NAT_KEYS), :])
    v = _mha_variants(v_ref[pl.ds(start, NAT_KEYS), :])
    ck = _mha_variants(ck_ref[0, 0])
    cv = _mha_variants(cv_ref[0, 0])
    bias = ((bias_ref[0], bias_ref[1]),)
    y = _attend_cols(q_ref[...], k, v, extra_k=ck, extra_v=cv, bias=bias)
    y_ref[...] = y.astype(BF16)


def _natten(layer, qc, kc, vc, cache_k, cache_v, blocks):
    ctx_h = T_CTX // HALF
    ctx_r = T_CTX // DEC_SEQ
    cache = pl.BlockSpec((1, 1, PAST_LEN, LANES), lambda j, s, b: (b, layer, 0, j))
    own = pl.BlockSpec((DEC_SEQ, LANES), lambda j, s, b: (ctx_r + b, j))
    return pl.pallas_call(
        _natten_kernel,
        grid=(C_W // LANES, DEC_SEQ // HALF, DEC_BATCH),
        in_specs=[pl.BlockSpec((HALF, LANES), lambda j, s, b: (ctx_h + 2 * b + s, j)), own, own, cache, cache,
                  pl.BlockSpec((1, 2, 2 * WIN_R - 1, GRID_W, LANES), lambda j, s, b: (j, 0, 0, 0, 0))],
        out_specs=pl.BlockSpec((HALF, LANES), lambda j, s, b: (2 * b + s, j)),
        out_shape=jax.ShapeDtypeStruct((T_LAT, C_W), BF16),
        scratch_shapes=[pltpu.VMEM((2, HALF, NAT_KEYS), F32)],
        compiler_params=pltpu.CompilerParams(
            dimension_semantics=("arbitrary", "arbitrary", "arbitrary"), vmem_limit_bytes=VMEM_LIMIT),
        name="natten",
    )(qc, kc, vc, cache_k, cache_v, blocks)


def _natten_blocks(rpb):
    qcol = np.arange(GRID_W)[:, None]
    kcol = np.arange(GRID_W)[None, :]
    cs = np.clip(qcol - WIN_C // 2, 0, GRID_W - WIN_C)
    col_ok = (kcol >= cs) & (kcol < cs + WIN_C)
    dc = np.clip(kcol - qcol, -(WIN_C - 1), WIN_C - 1) + (WIN_C - 1)
    sel_c = (dc[:, :, None] == np.arange(2 * WIN_C - 1)).astype(np.float32)
    blk = jnp.einsum("hdm,qkm->hdqk", rpb, jnp.asarray(sel_c), precision=lax.Precision.HIGHEST)
    blk = jnp.where(jnp.asarray(col_ok), blk, NEG)
    blk = jnp.concatenate([blk, blk], axis=-1)
    return blk.reshape(H_C // 2, 2, 2 * WIN_R - 1, GRID_W, LANES)


def _route(lg):
    lane = lax.broadcasted_iota(jnp.int32, lg.shape, 1).astype(F32)
    low = jnp.float32(-3.0e38)
    far = jnp.float32(ROUTER_LANES)
    is_g = lane < N_GROUPS
    gmax = jnp.where(is_g, lg, low).max(axis=-1, keepdims=True)
    gsel = jnp.where(is_g & (lg == gmax), lane, far).min(axis=-1, keepdims=True)
    pg_sel = 1.0 / jnp.where(is_g, jnp.exp(lg - gmax), 0.0).sum(axis=-1, keepdims=True)
    e0 = EXPERT_LANE0 + EXPERTS_PER_GROUP * gsel
    in_g = (lane >= e0) & (lane < e0 + EXPERTS_PER_GROUP)
    m1 = jnp.where(in_g, lg, low).max(axis=-1, keepdims=True)
    i1 = jnp.where(in_g & (lg == m1), lane, far).min(axis=-1, keepdims=True)
    rest = in_g & (lane != i1)
    m2 = jnp.where(rest, lg, low).max(axis=-1, keepdims=True)
    i2 = jnp.where(rest & (lg == m2), lane, far).min(axis=-1, keepdims=True)
    t = jnp.exp(m2 - m1)
    ssum = 1.0 + t
    w1 = (1.0 / ssum) * pg_sel
    w2 = (t / ssum) * pg_sel
    return i1 - EXPERT_LANE0, i2 - EXPERT_LANE0, w1, w2


def _rows_to_groups(x, stage_ref, out_ref, n_rows):
    stride = n_rows + 8
    for c in range(GROUP_ROWS):
        stage_ref[c * stride:c * stride + n_rows, :] = x[:, c * LANES:(c + 1) * LANES]
    for j in range(n_rows):
        out_ref[GROUP_ROWS * j:GROUP_ROWS * (j + 1), :] = stage_ref[pl.ds(j, GROUP_ROWS, stride=stride), :]


def _tail_kernel(xc_ref, xl_ref, ya_ref, ybc_ref, ybl_ref, ycc_ref, ycl_ref, w_ref, g1_ref, sh2_ref, sc2_ref,
                 lg_ref, lb_ref, wr_ref, br_ref, x1_ref, h2g_ref, ids_ref, wts_ref, wbf_ref, stage_ref):
    @pl.when(pl.program_id(0) == 0)
    def _():
        wbf_ref[...] = w_ref[0].astype(BF16)

    o = (_dot(ya_ref[...], wbf_ref[0:POOL_WIDTH, :])
         + _dot(_split_rows(ybc_ref, ybl_ref), wbf_ref[POOL_WIDTH:POOL_WIDTH + QB_W, :])
         + _dot(_split_rows(ycc_ref, ycl_ref), wbf_ref[POOL_WIDTH + QB_W:, :]))
    x1 = _layer_norm(ALPHA * _split_rows(xc_ref, xl_ref) + g1_ref[0] * o, lg_ref[0], lb_ref[0])
    x1_ref[...] = x1
    h2 = x1 * (1.0 + sc2_ref[0]) + sh2_ref[0]
    hh, hl = _split_bf16(h2)
    wh, wl = _split_bf16(wr_ref[0])
    lg = _dot(hh, wh) + _dot(hl, wh) + _dot(hh, wl) + br_ref[0]
    i1, i2, w1, w2 = _route(lg)
    lane = lax.broadcasted_iota(jnp.int32, lg.shape, 1)
    ids = jnp.where(lane == 0, i1, jnp.where(lane == 1, i2, 0.0))
    ids_ref[0] = ids.T[0:8, :].astype(jnp.int32)
    wts = jnp.where(lane == 0, w1, jnp.where(lane == 1, w2, 0.0))
    wts_ref[0] = wts.T[0:8, :]
    _rows_to_groups(h2, stage_ref, h2g_ref, TT)


def _tail(layer, x_ctx, x_lat, ya, yb_ctx, yb_lat, yc_ctx, yc_lat, w_out, mod, ln_g, ln_b, wr, br):
    row = lambda w: pl.BlockSpec((TT, w), lambda i: (i, 0))
    return pl.pallas_call(
        _tail_kernel,
        grid=(T_ALL // TT,),
        in_specs=[*_split_row_specs(D_MODEL), row(POOL_WIDTH), *_split_row_specs(QB_W), *_split_row_specs(C_W),
                  _layer_spec(layer, (D_MODEL, D_MODEL)),
                  _mod_spec(layer, 2, TT), _mod_spec(layer, 3, TT), _mod_spec(layer, 4, TT),
                  _layer_spec(layer, (1, D_MODEL)), _layer_spec(layer, (1, D_MODEL)),
                  _layer_spec(layer, (D_MODEL, ROUTER_LANES)), _layer_spec(layer, (1, ROUTER_LANES))],
        out_specs=[row(D_MODEL), pl.BlockSpec((TT * GROUP_ROWS, LANES), lambda i: (i, 0)),
                   pl.BlockSpec((1, 8, TT), lambda i: (i, 0, 0)), pl.BlockSpec((1, 8, TT), lambda i: (i, 0, 0))],
        out_shape=[jax.ShapeDtypeStruct((T_ALL, D_MODEL), F32),
                   jax.ShapeDtypeStruct((T_ALL * GROUP_ROWS, LANES), F32),
                   jax.ShapeDtypeStruct((T_ALL // TT, 8, TT), jnp.int32),
                   jax.ShapeDtypeStruct((T_ALL // TT, 8, TT), F32)],
        scratch_shapes=[pltpu.VMEM((D_MODEL, D_MODEL), BF16),
                        pltpu.VMEM((GROUP_ROWS * (TT + 8), LANES), F32)],
        compiler_params=pltpu.CompilerParams(dimension_semantics=("arbitrary",), vmem_limit_bytes=VMEM_LIMIT),
        name="tail",
    )(x_ctx, x_lat, ya, yb_ctx, yb_lat, yc_ctx, yc_lat, w_out, mod, mod, mod, ln_g, ln_b, wr, br)


def _plan_kernel(eid_ref, pos_ref, tile_ref):
    e = eid_ref[0]
    r = lax.broadcasted_iota(jnp.int32, (LANES, LANES), 0)
    c = lax.broadcasted_iota(jnp.int32, (LANES, LANES), 1)
    upper = (r <= c).astype(BF16)
    rows = e.shape[0]
    lower = (lax.broadcasted_iota(jnp.int32, (rows, rows), 1)
             < lax.broadcasted_iota(jnp.int32, (rows, rows), 0)).astype(BF16)
    tile_start = (lax.broadcasted_iota(jnp.int32, (1, LANES), 1) * TMX).astype(F32)
    lane = lax.broadcasted_iota(jnp.int32, (1, LANES), 1)
    pos = jnp.zeros(e.shape, F32)
    base = jnp.zeros((1, LANES), F32)
    n_before = jnp.zeros((1, LANES), F32)
    valid_end = jnp.zeros((1, LANES), F32)
    pad_end = jnp.zeros((1, LANES), F32)
    for ex in range(N_EXPERTS):
        m = e == ex
        incl = _dot(m.astype(BF16), upper)
        row_tot = jnp.broadcast_to(incl[:, LANES - 1:LANES], incl.shape)
        row_off = _dot(lower, row_tot.astype(BF16))
        cnt = jnp.sum(row_tot, axis=0, keepdims=True)
        pos = jnp.where(m, base + row_off + incl - 1.0, pos)
        valid_end = jnp.where(lane == ex, base + cnt, valid_end)
        base = base + jnp.ceil(cnt * (1.0 / TMX)) * TMX
        pad_end = jnp.where(lane == ex, base, pad_end)
        n_before = n_before + (tile_start >= base).astype(F32)
    pos_ref[0] = pos.astype(jnp.int32)
    sub = lax.broadcasted_iota(jnp.int32, (8, LANES), 0)
    rows8 = lambda v: jnp.broadcast_to(v, (8, LANES))
    table = jnp.where(sub == PLAN_TILE_EXPERT, rows8(jnp.minimum(n_before, N_EXPERTS - 1.0)),
                      jnp.where(sub == PLAN_N_TILES, rows8(base * (1.0 / TMX)),
                                jnp.where(sub == PLAN_VALID_END, rows8(valid_end),
                                          jnp.where(sub == PLAN_PAD_END, rows8(pad_end), 0.0))))
    tile_ref[0] = table.astype(jnp.int32)


def _plan(eid):
    rows = 2 * T_STREAM // LANES
    return pl.pallas_call(
        _plan_kernel,
        grid=(N_STREAMS,),
        in_specs=[pl.BlockSpec((1, rows, LANES), lambda s: (s, 0, 0))],
        out_specs=[pl.BlockSpec((1, rows, LANES), lambda s: (s, 0, 0)), pl.BlockSpec((1, 8, LANES), lambda s: (s, 0, 0))],
        out_shape=[jax.ShapeDtypeStruct((N_STREAMS, rows, LANES), jnp.int32),
                   jax.ShapeDtypeStruct((N_STREAMS, 8, LANES), jnp.int32)],
        compiler_params=pltpu.CompilerParams(dimension_semantics=("arbitrary",), vmem_limit_bytes=VMEM_LIMIT),
        name="plan",
    )(eid)


def _moe_kernel(pos_ref, wpair_ref, plan_ref, h2g_hbm, wg_ref, wu_ref, wd_ref, y_hbm,
                h2v, yv, off, xg0, xg1, yg0, yg1, act0, act1, wgb, wub, wdb, wcnt, sem):
    s = pl.program_id(0)
    e = pl.program_id(1)
    n_pairs = 2 * T_STREAM
    spare_off = T_STREAM * GROUP_ROWS
    plan = lambda row, idx: plan_ref[(s * 8 + row) * LANES + idx]
    n_tiles = plan(PLAN_N_TILES, 0)
    last = n_tiles - 1

    xgs = (xg0, xg1)
    ygs = (yg0, yg1)
    acts = (act0, act1)

    def row_offset(pair):
        return pl.multiple_of((pair & (SLOT_PITCH - 1)) << 3, GROUP_ROWS)

    def gather(tile, dst):
        base = tile * TMX
        for j in range(TMX):
            dst[pl.ds(j, GROUP_ROWS, stride=XSTRIDE), :] = h2v[pl.ds(row_offset(off[base + j]), GROUP_ROWS), :]

    def h2_copy():
        return pltpu.make_async_copy(
            h2g_hbm.at[pl.ds(pl.multiple_of(s * (T_STREAM * GROUP_ROWS), 8), T_STREAM * GROUP_ROWS)],
            h2v.at[pl.ds(0, T_STREAM * GROUP_ROWS)], sem.at[0])

    def y_copy():
        return pltpu.make_async_copy(yv.at[pl.ds(0, T_STREAM * GROUP_ROWS)], y_hbm.at[s], sem.at[1])

    @pl.when(e == 0)
    def _():
        h2_copy().start()
        yv[...] = jnp.zeros_like(yv)
        h2v[spare_off:spare_off + GROUP_ROWS, :] = jnp.zeros((GROUP_ROWS, LANES), F32)

        def pad_expert(ex, carry):
            lo = plan(PLAN_VALID_END, ex)

            def pad_chunk(k, c):
                for u in range(PAD_CHUNK):
                    off[lo + k * PAD_CHUNK + u] = T_STREAM
                return c
            n_chunks = (plan(PLAN_PAD_END, ex) - lo + (PAD_CHUNK - 1)) // PAD_CHUNK
            return lax.fori_loop(0, n_chunks, pad_chunk, carry)
        lax.fori_loop(0, N_EXPERTS, pad_expert, 0)

        def place(i, carry):
            for u in range(PLACE_UNROLL):
                tok = i * PLACE_UNROLL + u
                for slot in range(2):
                    off[pos_ref[s * n_pairs + slot * T_STREAM + tok]] = slot * SLOT_PITCH + tok
            return carry
        lax.fori_loop(0, T_STREAM // PLACE_UNROLL, place, 0)
        yg0[...] = jnp.zeros_like(yg0)
        yg1[...] = jnp.zeros_like(yg1)
        act1[...] = jnp.zeros_like(act1)
        wdb[...] = jnp.zeros_like(wdb)
        wcnt[0] = 0
        h2_copy().wait()
        gather(0, xg0)

    t_lo = jnp.where(e == 0, 0, plan(PLAN_PAD_END, jnp.maximum(e - 1, 0))) // TMX
    t_hi = plan(PLAN_PAD_END, e) // TMX

    @pl.when(t_hi > t_lo)
    def _():
        n = wcnt[0] + 1
        wcnt[0] = n
        wgb[...] = wg_ref[0, 0].astype(BF16)
        wub[...] = wu_ref[0, 0].astype(BF16)
        wdb[n & 1] = wd_ref[0, 0].astype(BF16)

    w_slot = wcnt[0] & 1

    def add_tile(tile, done):
        base = tile * TMX
        for j0 in range(0, TMX, RMW_BATCH):
            updates = []
            for j in range(j0, j0 + RMW_BATCH):
                pair = off[base + j]
                dst = row_offset(pair)
                gate = wpair_ref[s * (2 * SLOT_PITCH) + pair]
                row = done[pl.ds(j, GROUP_ROWS, stride=XSTRIDE), :]
                updates.append((dst, yv[pl.ds(dst, GROUP_ROWS), :] + gate * row))
            for dst, val in updates:
                yv[pl.ds(dst, GROUP_ROWS), :] = val

    def down(act_ref, slot, out):
        y = _dot(act_ref[...], wdb[slot])
        for c in range(GROUP_ROWS):
            out[c * XSTRIDE:c * XSTRIDE + TMX, :] = y[:, c * LANES:(c + 1) * LANES]

    def step(t, par):
        gather(jnp.minimum(t + 1, last), xgs[1 - par])
        cur = xgs[par]
        x = jnp.concatenate([cur[c * XSTRIDE:c * XSTRIDE + TMX, :] for c in range(GROUP_ROWS)], axis=1).astype(BF16)
        hg = _dot(x, wgb[...])
        hu = _dot(x, wub[...])
        acts[par][...] = (hg * jax.nn.sigmoid(hg) * hu).astype(BF16)
        down(acts[1 - par], jnp.where(t - 1 >= t_lo, w_slot, 1 - w_slot), ygs[1 - par])
        add_tile(jnp.maximum(t - 2, 0), ygs[par])

    def tile_body(t, carry):
        for par in range(2):
            @pl.when((t & 1) == par)
            def _():
                step(t, par)
        return carry
    lax.fori_loop(t_lo, t_hi, tile_body, 0)

    @pl.when(e == N_EXPERTS - 1)
    def _():
        for par in range(2):
            @pl.when((last & 1) == par)
            def _():
                add_tile(jnp.maximum(last - 1, 0), ygs[1 - par])
                down(acts[par], w_slot, ygs[par])
                add_tile(last, ygs[par])
        y_copy().start()
        y_copy().wait()


def _moe(layer, pos, wpair, plan, h2g, w_gate, w_up, w_down):
    wspec = lambda shape: pl.BlockSpec((1,) + shape, lambda s, e, pos, wpair, plan: (layer, e, 0, 0))
    return pl.pallas_call(
        _moe_kernel,
        grid_spec=pltpu.PrefetchScalarGridSpec(
            num_scalar_prefetch=3,
            grid=(N_STREAMS, N_EXPERTS),
            in_specs=[pl.BlockSpec(memory_space=pl.ANY),
                      wspec((1, D_MODEL, D_EXPERT)), wspec((1, D_MODEL, D_EXPERT)), wspec((1, D_EXPERT, D_MODEL))],
            out_specs=pl.BlockSpec(memory_space=pl.ANY),
            scratch_shapes=[pltpu.VMEM(((T_STREAM + 1) * GROUP_ROWS, LANES), F32),
                            pltpu.VMEM(((T_STREAM + 1) * GROUP_ROWS, LANES), F32),
                            pltpu.SMEM((P_MAX + PAD_CHUNK,), jnp.int32),
                            pltpu.VMEM((GROUP_ROWS * XSTRIDE, LANES), F32),
                            pltpu.VMEM((GROUP_ROWS * XSTRIDE, LANES), F32),
                            pltpu.VMEM((GROUP_ROWS * XSTRIDE, LANES), F32),
                            pltpu.VMEM((GROUP_ROWS * XSTRIDE, LANES), F32),
                            pltpu.VMEM((TMX, D_EXPERT), BF16), pltpu.VMEM((TMX, D_EXPERT), BF16),
                            pltpu.VMEM((D_MODEL, D_EXPERT), BF16), pltpu.VMEM((D_MODEL, D_EXPERT), BF16),
                            pltpu.VMEM((2, D_EXPERT, D_MODEL), BF16),
                            pltpu.SMEM((1,), jnp.int32),
                            pltpu.SemaphoreType.DMA((2,))]),
        out_shape=jax.ShapeDtypeStruct((N_STREAMS, T_STREAM * GROUP_ROWS, LANES), F32),
        compiler_params=pltpu.CompilerParams(
            dimension_semantics=("arbitrary", "arbitrary"), vmem_limit_bytes=VMEM_LIMIT),
        name="moe",
    )(pos, wpair, plan, h2g, w_gate, w_up, w_down)


def _groups_to_rows(src_ref, stage_ref, n_rows):
    stride = n_rows + 8
    for j in range(n_rows):
        stage_ref[pl.ds(j, GROUP_ROWS, stride=stride), :] = src_ref[GROUP_ROWS * j:GROUP_ROWS * (j + 1), :]
    return jnp.concatenate([stage_ref[c * stride:c * stride + n_rows, :] for c in range(GROUP_ROWS)], axis=1)


def _final_kernel(x1_ref, y_ref, g2_ref, lg_ref, lb_ref, o_ref, stage_ref):
    moe = _groups_to_rows(y_ref, stage_ref, TT)
    o_ref[...] = _layer_norm(ALPHA * x1_ref[...] + g2_ref[0] * moe, lg_ref[0], lb_ref[0])


def _final(layer, stream, x1, y, mod, ln_g, ln_b):
    first = stream * (T_STREAM // TT)
    g2 = pl.BlockSpec(
        (1, 1, D_MODEL), lambda i: ((layer * MOD_ROWS + _mod_row_of_tile(first + i, TT)) * 6 + 5, 0, 0))
    return pl.pallas_call(
        _final_kernel,
        grid=(T_STREAM // TT,),
        in_specs=[pl.BlockSpec((TT, D_MODEL), lambda i: (first + i, 0)),
                  pl.BlockSpec((TT * GROUP_ROWS, LANES), lambda i: (first + i, 0)),
                  g2, _layer_spec(layer, (1, D_MODEL)), _layer_spec(layer, (1, D_MODEL))],
        out_specs=pl.BlockSpec((TT, D_MODEL), lambda i: (i, 0)),
        out_shape=jax.ShapeDtypeStruct((T_STREAM, D_MODEL), F32),
        scratch_shapes=[pltpu.VMEM((GROUP_ROWS * (TT + 8), LANES), F32)],
        compiler_params=pltpu.CompilerParams(dimension_semantics=("arbitrary",), vmem_limit_bytes=VMEM_LIMIT),
        name="final",
    )(x1, y, mod, ln_g, ln_b)


def _rope_tables():
    t = np.arange(DEC_SEQ)
    pos = np.stack([t // GRID_W, t % GRID_W], axis=1).astype(np.float32)
    nf = HEAD_DIM // 4
    inv = jnp.asarray(ROPE_THETA, F32) ** (-jnp.arange(nf, dtype=F32) / nf)
    d = np.arange(LANES) % HEAD_DIM
    which = d // (HEAD_DIM // 2)
    ang = jnp.asarray(pos)[:, which] * inv[d % nf][None, :]
    sign = np.where((d % 32) < 16, -1.0, 1.0).astype(np.float32)
    cos = jnp.concatenate([jnp.cos(ang), jnp.ones((TT, LANES), F32)], axis=0)
    sin = jnp.concatenate([jnp.sin(ang) * sign[None, :], jnp.zeros((TT, LANES), F32)], axis=0)
    return cos, sin


def _block_ones():
    h = np.arange(QB_W) // HEAD_DIM
    return jnp.asarray((h[:, None] == h[None, :]).astype(np.float32), dtype=BF16)


def _pool_block_diag(pool_w):
    out = jnp.zeros((POOL_WIDTH, POOL_WIDTH), F32)
    for g in range(4):
        out = out.at[64 * g:64 * g + 64, 64 * g:64 * g + 64].set(pool_w[g])
    return out


def kernel(x_prompt, x_sample, cache_b_k, cache_b_v, cache_c_k, cache_c_v, c, c_ctx, w_ada, b_ada, w_in, w_out,
           pool_w, pool_scale, q_norm, k_norm, rpb, ln1_g, ln1_b, ln2_g, ln2_b, router_g, router_g_b, router_e,
           router_e_b, w_gate, w_up, w_down):
    x_ctx = x_prompt.reshape(T_CTX, D_MODEL)
    x_lat = x_sample.reshape(T_LAT, D_MODEL)
    c8 = jnp.concatenate([c, c_ctx[None], jnp.zeros((MOD_ROWS - DEC_BATCH - 1, D_MODEL), F32)], axis=0)
    mod = _adaln(c8, w_ada, b_ada).reshape(DEPTH * MOD_ROWS * 6, 1, D_MODEL)

    cos_t, sin_t = _rope_tables()
    ones_bd = _block_ones()
    cbk = cache_b_k.reshape(DEC_BATCH, DEPTH, PAST_LEN, KB_W)
    cbv = cache_b_v.reshape(DEC_BATCH, DEPTH, PAST_LEN, KB_W)
    cck = cache_c_k.reshape(DEC_BATCH, DEPTH, PAST_LEN, C_W)
    ccv = cache_c_v.reshape(DEC_BATCH, DEPTH, PAST_LEN, C_W)
    pad = jnp.zeros((DEPTH, D_MODEL, ROUTER_LANES - N_GROUPS - N_EXPERTS), F32)
    wr = jnp.concatenate([router_g, router_e, pad], axis=2)
    br = jnp.concatenate([router_g_b, router_e_b, pad[:, 0]], axis=1)[:, None]
    per_layer = lambda p: p[:, None]

    caches = [jnp.zeros((BATCH, DEPTH, w, SEQ), F32) for w in CACHE_WIDTHS]
    for l in range(DEPTH):
        qn = jnp.tile(q_norm[l], H_B)[None]
        kn = jnp.tile(k_norm[l], KV_B)[None]
        a, qb, kb, vb, qc, kc, vc = _proj(l, x_ctx, x_lat, mod, w_in, ones_bd, qn, kn, cos_t, sin_t)
        ya = _pool(a, _pool_block_diag(pool_w[l]), pool_scale[l][None])
        yb_ctx, yc_ctx, *caches = _attn_ctx(l, qb, kb, vb, qc, kc, vc, caches)
        yb_lat = _attn_latb(l, qb, kb, vb, cbk, cbv)
        yc_lat = _natten(l, qc, kc, vc, cck, ccv, _natten_blocks(rpb[l]))
        x1, h2g, ids, wts = _tail(l, x_ctx, x_lat, ya, yb_ctx, yb_lat, yc_ctx, yc_lat, w_out, mod,
                                  per_layer(ln1_g), per_layer(ln1_b), wr, br)
        pairs = lambda a: a.reshape(N_STREAMS, T_STREAM // TT, 8, TT)[:, :, :2, :].transpose(0, 2, 1, 3)
        pos, tiles = _plan(pairs(ids).reshape(N_STREAMS, 2 * T_STREAM // LANES, LANES))
        gates = jnp.pad(pairs(wts).reshape(N_STREAMS, 2, T_STREAM), ((0, 0), (0, 0), (0, SLOT_PITCH - T_STREAM)))
        y = _moe(l, pos.reshape(-1), gates.reshape(-1), tiles.reshape(-1), h2g, w_gate, w_up, w_down)
        y = y.reshape(T_ALL * GROUP_ROWS, LANES)
        x_ctx = _final(l, 0, x1, y, mod, per_layer(ln2_g), per_layer(ln2_b))
        x_lat = _final(l, 1, x1, y, mod, per_layer(ln2_g), per_layer(ln2_b))

    y_prompt = x_ctx.reshape(BATCH, SEQ, D_MODEL)
    y_sample = x_lat.reshape(DEC_BATCH, DEC_SEQ, D_MODEL)
    heads_last = lambda t, h: t.reshape(BATCH, DEPTH, h, HEAD_DIM, SEQ).transpose(0, 1, 4, 2, 3)
    new_bk, new_bv, new_ck, new_cv = caches
    return (y_prompt, y_sample, heads_last(new_bk, KV_B), heads_last(new_bv, KV_B),
            heads_last(new_ck, H_C), heads_last(new_cv, H_C))
```

```python
import functools

import numpy as np
import jax
import jax.numpy as jnp
from jax import lax
from jax.experimental import pallas as pl
from jax.experimental.pallas import tpu as pltpu

F32 = jnp.float32
BF16 = jnp.bfloat16

D_MODEL = 1024
BATCH = 16
SEQ = 256
DEPTH = 2
DEC_BATCH = 4
DEC_SEQ = 1024
PAST_LEN = 256
GRID_W = 64
GRID_ROWS = DEC_SEQ // GRID_W
HEAD_DIM = 64
POOL_WIDTH = 256
POOL_WINDOWS = (2, 4, 8, 16)
H_B = 6
KV_B = 2
H_C = 6
WIN_R = 8
WIN_C = 16
ROPE_THETA = 10000.0
QB_W = H_B * HEAD_DIM
KB_W = KV_B * HEAD_DIM
C_W = H_C * HEAD_DIM
PROJ_WIDTH = 2048
N_GROUPS = 4
EXPERTS_PER_GROUP = 8
N_EXPERTS = 32
D_EXPERT = 256
ALPHA = (2 * DEPTH) ** 0.25
LN_EPS = 1e-6
RMS_EPS = 1e-6
NEG = -1e30
ATTN_SCALE = HEAD_DIM ** -0.5

T_CTX = BATCH * SEQ
T_LAT = DEC_BATCH * DEC_SEQ
T_ALL = T_CTX + T_LAT

LANES = 128
ROUTER_LANES = 128
EXPERT_LANE0 = N_GROUPS
MOD_ROWS = 8
CTX_MOD_ROW = DEC_BATCH

TT = 512
TP = 1024
POOL_GAP = 16
N_STREAMS = 2
T_STREAM = T_ALL // N_STREAMS
TMX = 128
PLACE_UNROLL = 8
PAD_CHUNK = 8
SLOT_PITCH = 2 * T_STREAM
P_MAX = 2 * T_STREAM + N_EXPERTS * TMX
NT_MAX = P_MAX // TMX
GROUP_ROWS = D_MODEL // LANES
XSTRIDE = TMX + 8
RMW_BATCH = 16
PLAN_N_TILES, PLAN_VALID_END, PLAN_PAD_END = 0, 1, 2
HALF = DEC_SEQ // 2
NAT_KEYS = 12 * GRID_W
VMEM_LIMIT = 56 * 1024 * 1024


def _dot(a, b):
    return jnp.dot(a, b, preferred_element_type=F32)


def _dot_nt(a, b):
    return lax.dot_general(a, b, (((1,), (1,)), ((), ())), preferred_element_type=F32)


def _split_bf16(x):
    hi = x.astype(BF16)
    lo = (x - hi.astype(F32)).astype(BF16)
    return hi, lo


def _layer_norm(y, g, b):
    mu = jnp.mean(y, axis=-1, keepdims=True)
    var = jnp.mean(jnp.square(y - mu), axis=-1, keepdims=True)
    return (y - mu) * lax.rsqrt(var + LN_EPS) * g + b


def _adaln_kernel(c_ref, w_ref, b_ref, o_ref):
    c = c_ref[...]
    s = (c * jax.nn.sigmoid(c)).astype(BF16)
    o_ref[0] = _dot(s, w_ref[0].astype(BF16)) + b_ref[0]


def _adaln(c8, w_ada, b_ada):
    tn = 1536
    n = w_ada.shape[-1]
    return pl.pallas_call(
        _adaln_kernel,
        grid=(DEPTH, n // tn),
        in_specs=[
            pl.BlockSpec((MOD_ROWS, D_MODEL), lambda l, j: (0, 0)),
            pl.BlockSpec((1, D_MODEL, tn), lambda l, j: (l, 0, j)),
            pl.BlockSpec((1, 1, tn), lambda l, j: (l, 0, j)),
        ],
        out_specs=pl.BlockSpec((1, MOD_ROWS, tn), lambda l, j: (l, 0, j)),
        out_shape=jax.ShapeDtypeStruct((DEPTH, MOD_ROWS, n), F32),
        compiler_params=pltpu.CompilerParams(
            dimension_semantics=("arbitrary", "arbitrary"), vmem_limit_bytes=VMEM_LIMIT),
        name="adaln",
    )(c8, w_ada, b_ada.reshape(DEPTH, 1, n))


def _mod_row_of_tile(i, tile):
    n_ctx = T_CTX // tile
    per_req = DEC_SEQ // tile
    return jnp.where(i < n_ctx, CTX_MOD_ROW, (i - n_ctx) // per_req)


def _mod_spec(layer, chunk, tile):
    return pl.BlockSpec(
        (1, 1, D_MODEL), lambda i: ((layer * MOD_ROWS + _mod_row_of_tile(i, tile)) * 6 + chunk, 0, 0))


def _layer_spec(layer, shape):
    return pl.BlockSpec((1,) + shape, lambda *_: (layer,) + (0,) * len(shape))


N_CTX_TILES = T_CTX // TT


def _split_row_specs(width):
    return (pl.BlockSpec((TT, width), lambda i: (jnp.minimum(i, N_CTX_TILES - 1), 0)),
            pl.BlockSpec((TT, width), lambda i: (jnp.maximum(i - N_CTX_TILES, 0), 0)))


def _split_rows(ctx_ref, lat_ref):
    return jnp.where(pl.program_id(0) < N_CTX_TILES, ctx_ref[...], lat_ref[...])


def _rms_norm_heads(x, ones_bd, w):
    hi, lo = _split_bf16(x * x)
    ssq = _dot(hi, ones_bd) + _dot(lo, ones_bd)
    return x * lax.rsqrt(ssq * (1.0 / HEAD_DIM) + RMS_EPS) * w


def _rope_cols(x, cos, sin, first16):
    cols = []
    for j in range(x.shape[1] // LANES):
        xc = x[:, j * LANES:(j + 1) * LANES]
        partner = jnp.where(first16, pltpu.roll(xc, LANES - 16, axis=1), pltpu.roll(xc, 16, axis=1))
        cols.append(xc * cos + partner * sin)
    return jnp.concatenate(cols, axis=1) if len(cols) > 1 else cols[0]


def _proj_kernel(xc_ref, xl_ref, sh_ref, sc_ref, w_ref, ones_ref, qn_ref, kn_ref, cos_ref, sin_ref,
                 a_ref, qb_ref, kb_ref, vb_ref, qc_ref, kc_ref, vc_ref, wbf_ref):
    @pl.when(pl.program_id(0) == 0)
    def _():
        wbf_ref[...] = w_ref[0].astype(BF16)

    h = _split_rows(xc_ref, xl_ref) * (1.0 + sc_ref[0]) + sh_ref[0]
    p = _dot(h.astype(BF16), wbf_ref[...])
    o = 0
    a_ref[...] = p[:, o:o + POOL_WIDTH]; o += POOL_WIDTH
    qb = p[:, o:o + QB_W]; o += QB_W
    kb = p[:, o:o + KB_W]; o += KB_W
    vb_ref[...] = p[:, o:o + KB_W]; o += KB_W
    qc_ref[...] = (p[:, o:o + C_W] * ATTN_SCALE).astype(BF16); o += C_W
    kc_ref[...] = p[:, o:o + C_W]; o += C_W
    vc_ref[...] = p[:, o:o + C_W]

    ones_bd = ones_ref[...]
    cos = cos_ref[...]
    sin = sin_ref[...]
    lane = lax.broadcasted_iota(jnp.int32, (1, LANES), 1)
    first16 = (lane & 31) < 16
    qb = _rope_cols(_rms_norm_heads(qb, ones_bd, qn_ref[...]), cos, sin, first16)
    kb = _rope_cols(_rms_norm_heads(kb, ones_bd[:KB_W, :KB_W], kn_ref[...]), cos, sin, first16)
    qb_ref[...] = (qb * ATTN_SCALE).astype(BF16)
    kb_ref[...] = kb


def _proj(layer, x_ctx, x_lat, mod, w_in, ones_bd, qn, kn, cos_t, sin_t):
    n_tiles = T_ALL // TT
    n_ctx = N_CTX_TILES
    per_req = DEC_SEQ // TT

    def rope_idx(i):
        return (jnp.where(i < n_ctx, per_req, (i - n_ctx) % per_req), 0)

    row = lambda w: pl.BlockSpec((TT, w), lambda i: (i, 0))
    const = lambda s: pl.BlockSpec(s, lambda i: (0,) * len(s))
    widths = (POOL_WIDTH, QB_W, KB_W, KB_W, C_W, C_W, C_W)
    dtypes = (F32, BF16, F32, F32, BF16, F32, F32)
    return pl.pallas_call(
        _proj_kernel,
        grid=(n_tiles,),
        in_specs=[*_split_row_specs(D_MODEL), _mod_spec(layer, 0, TT), _mod_spec(layer, 1, TT),
                  _layer_spec(layer, (D_MODEL, PROJ_WIDTH)),
                  const((QB_W, QB_W)), const((1, QB_W)), const((1, KB_W)),
                  pl.BlockSpec((TT, LANES), rope_idx), pl.BlockSpec((TT, LANES), rope_idx)],
        out_specs=[row(w) for w in widths],
        out_shape=[jax.ShapeDtypeStruct((T_ALL, w), d) for w, d in zip(widths, dtypes)],
        scratch_shapes=[pltpu.VMEM((D_MODEL, PROJ_WIDTH), BF16)],
        compiler_params=pltpu.CompilerParams(dimension_semantics=("arbitrary",), vmem_limit_bytes=VMEM_LIMIT),
        name="proj",
    )(x_ctx, x_lat, mod, mod, w_in, ones_bd, qn, kn, cos_t, sin_t)


def _pool_windows(p):
    n = p.shape[0]
    sh = lambda x, k: pltpu.roll(x, (n - k) % n, axis=0)
    w2 = p + sh(p, -1)
    w4 = sh(w2, -1) + sh(w2, 1)
    w8 = sh(w4, -2) + sh(w4, 2)
    w16 = sh(w8, -4) + sh(w8, 4)
    g = lax.broadcasted_iota(jnp.int32, p.shape, 1) >> 6
    return jnp.where(g == 0, w2, jnp.where(g == 1, w4, jnp.where(g == 2, w8, w16)))


def _pool_kernel(a_ref, w_ref, scale_ref, o_ref, pad_ref):
    a = a_ref[...]

    def mix(csum, seq_len):
        row = lax.broadcasted_iota(jnp.int32, a.shape, 0)
        g = lax.broadcasted_iota(jnp.int32, a.shape, 1) >> 6
        t = row & (seq_len - 1)
        half = jnp.where(g == 0, 1, jnp.where(g == 1, 2, jnp.where(g == 2, 4, 8)))
        cnt = jnp.minimum(t + half, seq_len) - jnp.maximum(t - half, 0)
        pooled = csum / cnt.astype(F32) - a
        y = _dot(pooled.astype(BF16), w_ref[...].astype(BF16)) * scale_ref[...]
        o_ref[...] = y.astype(BF16)

    def padded_sums(seq_len):
        pitch = seq_len + POOL_GAP
        n_req = TP // seq_len
        n = n_req * pitch
        pad_ref[...] = jnp.zeros_like(pad_ref)
        for r in range(n_req):
            pad_ref[r * pitch + POOL_GAP // 2:r * pitch + POOL_GAP // 2 + seq_len, :] = a[r * seq_len:(r + 1) * seq_len]
        sums = _pool_windows(pad_ref[0:n, :])
        return jnp.concatenate(
            [sums[r * pitch + POOL_GAP // 2:r * pitch + POOL_GAP // 2 + seq_len] for r in range(n_req)], axis=0)

    is_ctx = pl.program_id(0) < T_CTX // TP

    @pl.when(is_ctx)
    def _():
        mix(padded_sums(SEQ), SEQ)

    @pl.when(jnp.logical_not(is_ctx))
    def _():
        mix(padded_sums(DEC_SEQ), DEC_SEQ)


def _pool(a, w_bd, scale):
    return pl.pallas_call(
        _pool_kernel,
        grid=(T_ALL // TP,),
        in_specs=[pl.BlockSpec((TP, POOL_WIDTH), lambda i: (i, 0)),
                  pl.BlockSpec((POOL_WIDTH, POOL_WIDTH), lambda i: (0, 0)),
                  pl.BlockSpec((1, POOL_WIDTH), lambda i: (0, 0))],
        out_specs=pl.BlockSpec((TP, POOL_WIDTH), lambda i: (i, 0)),
        out_shape=jax.ShapeDtypeStruct((T_ALL, POOL_WIDTH), BF16),
        scratch_shapes=[pltpu.VMEM(((TP // SEQ) * (SEQ + POOL_GAP), POOL_WIDTH), F32)],
        compiler_params=pltpu.CompilerParams(dimension_semantics=("arbitrary",), vmem_limit_bytes=VMEM_LIMIT),
        name="pool",
    )(a, w_bd, scale)


def _softmax_pv(scores, values):
    m = scores[0].max(axis=-1, keepdims=True)
    for s in scores[1:]:
        m = jnp.maximum(m, s.max(axis=-1, keepdims=True))
    ps = [jnp.exp(s - m) for s in scores]
    l = ps[0].sum(axis=-1, keepdims=True)
    for p in ps[1:]:
        l = l + p.sum(axis=-1, keepdims=True)
    o = _dot(ps[0].astype(BF16), values[0])
    for p, v in zip(ps[1:], values[1:]):
        o = o + _dot(p.astype(BF16), v)
    return o * (1.0 / l)


def _lane_halves():
    lane = lax.broadcasted_iota(jnp.int32, (1, LANES), 1)
    return lane < HEAD_DIM, lane >= HEAD_DIM


def _keep(x, mask):
    return jnp.where(mask, x, 0.0).astype(BF16)


def _gqa_variants(x):
    lo, hi = _lane_halves()
    xs = pltpu.roll(x, HEAD_DIM, axis=1)
    nat_lo, nat_hi = _keep(x, lo), _keep(x, hi)
    sw_lo, sw_hi = _keep(xs, lo), _keep(xs, hi)
    return ((nat_lo, sw_hi), (nat_lo, nat_hi), (sw_lo, nat_hi))


def _mha_variants(x):
    lo, hi = _lane_halves()
    out = []
    for j in range(x.shape[1] // LANES):
        xc = x[:, j * LANES:(j + 1) * LANES]
        out.append((_keep(xc, lo), _keep(xc, hi)))
    return tuple(out)


def _attend_cols(q, k_vars, v_vars, extra_k=None, extra_v=None, bias=None):
    cols = []
    for j in range(q.shape[1] // LANES):
        qc = q[:, j * LANES:(j + 1) * LANES]
        o = None
        for hh in range(2):
            s = _dot_nt(qc, k_vars[j][hh])
            if bias is not None:
                s = s + bias[j][hh]
            scores, values = [s], [v_vars[j][hh]]
            if extra_k is not None:
                scores.append(_dot_nt(qc, extra_k[j][hh]))
                values.append(extra_v[j][hh])
            oh = _softmax_pv(scores, values)
            o = oh if o is None else o + oh
        cols.append(o)
    return jnp.concatenate(cols, axis=1)


def _attn_ctx_kernel(layer_slot, qb_ref, kb_ref, vb_ref, qc_ref, kc_ref, vc_ref, *refs):
    yb_ref, yc_ref, kbt_ref, vbt_ref, kct_ref, vct_ref = refs[-6:]
    kb, vb, kc, vc = kb_ref[...], vb_ref[...], kc_ref[...], vc_ref[...]
    yb = _attend_cols(qb_ref[...], _gqa_variants(kb), _gqa_variants(vb))
    yb_ref[...] = yb.astype(BF16)
    yc = _attend_cols(qc_ref[...], _mha_variants(kc), _mha_variants(vc))
    yc_ref[...] = yc.astype(BF16)
    for ref, val in ((kbt_ref, kb), (vbt_ref, vb), (kct_ref, kc), (vct_ref, vc)):
        for slot in range(ref.shape[1]):
            ref[0, slot] = val.T if slot == layer_slot else jnp.zeros(ref.shape[2:], F32)


CACHE_WIDTHS = (KB_W, KB_W, C_W, C_W)


def _attn_ctx(layer, qb, kb, vb, qc, kc, vc, caches):
    row = lambda w: pl.BlockSpec((SEQ, w), lambda i: (i, 0))
    if caches:
        cache = lambda w: pl.BlockSpec((1, 1, w, SEQ), lambda i: (i, layer, 0, 0))
        layer_slot = 0
    else:
        cache = lambda w: pl.BlockSpec((1, DEPTH, w, SEQ), lambda i: (i, 0, 0, 0))
        layer_slot = layer
    return pl.pallas_call(
        functools.partial(_attn_ctx_kernel, layer_slot),
        grid=(BATCH,),
        in_specs=[row(QB_W), row(KB_W), row(KB_W), row(C_W), row(C_W), row(C_W)]
                 + [pl.BlockSpec(memory_space=pl.ANY)] * len(caches),
        out_specs=[row(QB_W), row(C_W)] + [cache(w) for w in CACHE_WIDTHS],
        out_shape=[jax.ShapeDtypeStruct((T_CTX, QB_W), BF16), jax.ShapeDtypeStruct((T_CTX, C_W), BF16)]
                  + [jax.ShapeDtypeStruct((BATCH, DEPTH, w, SEQ), F32) for w in CACHE_WIDTHS],
        input_output_aliases={6 + k: 2 + k for k in range(len(caches))},
        compiler_params=pltpu.CompilerParams(dimension_semantics=("arbitrary",), vmem_limit_bytes=VMEM_LIMIT),
        name="attn_ctx",
    )(qb, kb, vb, qc, kc, vc, *caches)


def _attn_latb_kernel(q_ref, k_ref, v_ref, ck_ref, cv_ref, y_ref):
    y = _attend_cols(q_ref[...], _gqa_variants(ck_ref[0, 0]), _gqa_variants(cv_ref[0, 0]),
                     extra_k=_gqa_variants(k_ref[...]), extra_v=_gqa_variants(v_ref[...]))
    y_ref[...] = y.astype(BF16)


def _attn_latb(layer, qb, kb, vb, cache_k, cache_v):
    ctx_h = T_CTX // HALF
    ctx_r = T_CTX // DEC_SEQ
    cache = pl.BlockSpec((1, 1, PAST_LEN, KB_W), lambda b, s: (b, layer, 0, 0))
    own = pl.BlockSpec((DEC_SEQ, KB_W), lambda b, s: (ctx_r + b, 0))
    return pl.pallas_call(
        _attn_latb_kernel,
        grid=(DEC_BATCH, DEC_SEQ // HALF),
        in_specs=[pl.BlockSpec((HALF, QB_W), lambda b, s: (ctx_h + 2 * b + s, 0)), own, own, cache, cache],
        out_specs=pl.BlockSpec((HALF, QB_W), lambda b, s: (2 * b + s, 0)),
        out_shape=jax.ShapeDtypeStruct((T_LAT, QB_W), BF16),
        compiler_params=pltpu.CompilerParams(
            dimension_semantics=("arbitrary", "arbitrary"), vmem_limit_bytes=VMEM_LIMIT),
        name="attn_latb",
    )(qb, kb, vb, cache_k, cache_v)


def _natten_window(s, i):
    r = (HALF // GRID_W) * s + i
    rs = min(max(r - WIN_R // 2, 0), GRID_ROWS - WIN_R)
    return r, rs


def _natten_kernel(q_ref, k_ref, v_ref, ck_ref, cv_ref, blk_ref, y_ref, bias_ref):
    s = pl.program_id(1)

    for sv in range(2):
        @pl.when((pl.program_id(2) == 0) & (s == sv))
        def _():
            lo_half, _ = _lane_halves()
            masked = jnp.full((GRID_W, LANES), NEG, F32)
            for hh in range(2):
                for i in range(HALF // GRID_W):
                    r, rs = _natten_window(sv, i)
                    for jp in range(NAT_KEYS // LANES):
                        pair = []
                        for j in (2 * jp, 2 * jp + 1):
                            rk = 4 * sv + j
                            pair.append(blk_ref[0, hh, rk - r + WIN_R - 1] if rs <= rk < rs + WIN_R else masked)
                        bias_ref[hh, i * GRID_W:(i + 1) * GRID_W, jp * LANES:(jp + 1) * LANES] = (
                            jnp.where(lo_half, pair[0], pair[1]))

    start = pl.multiple_of(s * (4 * GRID_W), 4 * GRID_W)
    k = _mha_variants(k_ref[pl.ds(start, NAT_KEYS), :])
    v = _mha_variants(v_ref[pl.ds(start, NAT_KEYS), :])
    ck = _mha_variants(ck_ref[0, 0])
    cv = _mha_variants(cv_ref[0, 0])
    bias = ((bias_ref[0], bias_ref[1]),)
    y = _attend_cols(q_ref[...], k, v, extra_k=ck, extra_v=cv, bias=bias)
    y_ref[...] = y.astype(BF16)


def _natten(layer, qc, kc, vc, cache_k, cache_v, blocks):
    ctx_h = T_CTX // HALF
    ctx_r = T_CTX // DEC_SEQ
    cache = pl.BlockSpec((1, 1, PAST_LEN, LANES), lambda j, s, b: (b, layer, 0, j))
    own = pl.BlockSpec((DEC_SEQ, LANES), lambda j, s, b: (ctx_r + b, j))
    return pl.pallas_call(
        _natten_kernel,
        grid=(C_W // LANES, DEC_SEQ // HALF, DEC_BATCH),
        in_specs=[pl.BlockSpec((HALF, LANES), lambda j, s, b: (ctx_h + 2 * b + s, j)), own, own, cache, cache,
                  pl.BlockSpec((1, 2, 2 * WIN_R - 1, GRID_W, LANES), lambda j, s, b: (j, 0, 0, 0, 0))],
        out_specs=pl.BlockSpec((HALF, LANES), lambda j, s, b: (2 * b + s, j)),
        out_shape=jax.ShapeDtypeStruct((T_LAT, C_W), BF16),
        scratch_shapes=[pltpu.VMEM((2, HALF, NAT_KEYS), F32)],
        compiler_params=pltpu.CompilerParams(
            dimension_semantics=("arbitrary", "arbitrary", "arbitrary"), vmem_limit_bytes=VMEM_LIMIT),
        name="natten",
    )(qc, kc, vc, cache_k, cache_v, blocks)


def _natten_blocks(rpb):
    qcol = np.arange(GRID_W)[:, None]
    kcol = np.arange(GRID_W)[None, :]
    cs = np.clip(qcol - WIN_C // 2, 0, GRID_W - WIN_C)
    col_ok = (kcol >= cs) & (kcol < cs + WIN_C)
    dc = np.clip(kcol - qcol, -(WIN_C - 1), WIN_C - 1) + (WIN_C - 1)
    sel_c = (dc[:, :, None] == np.arange(2 * WIN_C - 1)).astype(np.float32)
    blk = jnp.einsum("hdm,qkm->hdqk", rpb, jnp.asarray(sel_c), precision=lax.Precision.HIGHEST)
    blk = jnp.where(jnp.asarray(col_ok), blk, NEG)
    blk = jnp.concatenate([blk, blk], axis=-1)
    return blk.reshape(H_C // 2, 2, 2 * WIN_R - 1, GRID_W, LANES)


def _route(lg):
    lane = lax.broadcasted_iota(jnp.int32, lg.shape, 1).astype(F32)
    low = jnp.float32(-3.0e38)
    far = jnp.float32(ROUTER_LANES)
    is_g = lane < N_GROUPS
    gmax = jnp.where(is_g, lg, low).max(axis=-1, keepdims=True)
    gsel = jnp.where(is_g & (lg == gmax), lane, far).min(axis=-1, keepdims=True)
    pg_sel = 1.0 / jnp.where(is_g, jnp.exp(lg - gmax), 0.0).sum(axis=-1, keepdims=True)
    e0 = EXPERT_LANE0 + EXPERTS_PER_GROUP * gsel
    in_g = (lane >= e0) & (lane < e0 + EXPERTS_PER_GROUP)
    m1 = jnp.where(in_g, lg, low).max(axis=-1, keepdims=True)
    i1 = jnp.where(in_g & (lg == m1), lane, far).min(axis=-1, keepdims=True)
    rest = in_g & (lane != i1)
    m2 = jnp.where(rest, lg, low).max(axis=-1, keepdims=True)
    i2 = jnp.where(rest & (lg == m2), lane, far).min(axis=-1, keepdims=True)
    t = jnp.exp(m2 - m1)
    ssum = 1.0 + t
    w1 = (1.0 / ssum) * pg_sel
    w2 = (t / ssum) * pg_sel
    return i1 - EXPERT_LANE0, i2 - EXPERT_LANE0, w1, w2


def _rows_to_groups(x, stage_ref, out_ref, n_rows):
    stride = n_rows + 8
    for c in range(GROUP_ROWS):
        stage_ref[c * stride:c * stride + n_rows, :] = x[:, c * LANES:(c + 1) * LANES]
    for j in range(n_rows):
        out_ref[GROUP_ROWS * j:GROUP_ROWS * (j + 1), :] = stage_ref[pl.ds(j, GROUP_ROWS, stride=stride), :]


def _tail_kernel(xc_ref, xl_ref, ya_ref, ybc_ref, ybl_ref, ycc_ref, ycl_ref, w_ref, g1_ref, sh2_ref, sc2_ref,
                 lg_ref, lb_ref, wr_ref, br_ref, x1_ref, h2g_ref, ids_ref, wts_ref, wbf_ref, stage_ref):
    @pl.when(pl.program_id(0) == 0)
    def _():
        wbf_ref[...] = w_ref[0].astype(BF16)

    o = (_dot(ya_ref[...], wbf_ref[0:POOL_WIDTH, :])
         + _dot(_split_rows(ybc_ref, ybl_ref), wbf_ref[POOL_WIDTH:POOL_WIDTH + QB_W, :])
         + _dot(_split_rows(ycc_ref, ycl_ref), wbf_ref[POOL_WIDTH + QB_W:, :]))
    x1 = _layer_norm(ALPHA * _split_rows(xc_ref, xl_ref) + g1_ref[0] * o, lg_ref[0], lb_ref[0])
    x1_ref[...] = x1
    h2 = x1 * (1.0 + sc2_ref[0]) + sh2_ref[0]
    hh, hl = _split_bf16(h2)
    wh, wl = _split_bf16(wr_ref[0])
    lg = _dot(hh, wh) + _dot(hl, wh) + _dot(hh, wl) + br_ref[0]
    i1, i2, w1, w2 = _route(lg)
    lane = lax.broadcasted_iota(jnp.int32, lg.shape, 1)
    ids = jnp.where(lane == 0, i1, jnp.where(lane == 1, i2, 0.0))
    ids_ref[0] = ids.T[0:8, :].astype(jnp.int32)
    wts = jnp.where(lane == 0, w1, jnp.where(lane == 1, w2, 0.0))
    wts_ref[0] = wts.T[0:8, :]
    _rows_to_groups(h2, stage_ref, h2g_ref, TT)


def _tail(layer, x_ctx, x_lat, ya, yb_ctx, yb_lat, yc_ctx, yc_lat, w_out, mod, ln_g, ln_b, wr, br):
    row = lambda w: pl.BlockSpec((TT, w), lambda i: (i, 0))
    return pl.pallas_call(
        _tail_kernel,
        grid=(T_ALL // TT,),
        in_specs=[*_split_row_specs(D_MODEL), row(POOL_WIDTH), *_split_row_specs(QB_W), *_split_row_specs(C_W),
                  _layer_spec(layer, (D_MODEL, D_MODEL)),
                  _mod_spec(layer, 2, TT), _mod_spec(layer, 3, TT), _mod_spec(layer, 4, TT),
                  _layer_spec(layer, (1, D_MODEL)), _layer_spec(layer, (1, D_MODEL)),
                  _layer_spec(layer, (D_MODEL, ROUTER_LANES)), _layer_spec(layer, (1, ROUTER_LANES))],
        out_specs=[row(D_MODEL), pl.BlockSpec((TT * GROUP_ROWS, LANES), lambda i: (i, 0)),
                   pl.BlockSpec((1, 8, TT), lambda i: (i, 0, 0)), pl.BlockSpec((1, 8, TT), lambda i: (i, 0, 0))],
        out_shape=[jax.ShapeDtypeStruct((T_ALL, D_MODEL), F32),
                   jax.ShapeDtypeStruct((T_ALL * GROUP_ROWS, LANES), F32),
                   jax.ShapeDtypeStruct((T_ALL // TT, 8, TT), jnp.int32),
                   jax.ShapeDtypeStruct((T_ALL // TT, 8, TT), F32)],
        scratch_shapes=[pltpu.VMEM((D_MODEL, D_MODEL), BF16),
                        pltpu.VMEM((GROUP_ROWS * (TT + 8), LANES), F32)],
        compiler_params=pltpu.CompilerParams(dimension_semantics=("arbitrary",), vmem_limit_bytes=VMEM_LIMIT),
        name="tail",
    )(x_ctx, x_lat, ya, yb_ctx, yb_lat, yc_ctx, yc_lat, w_out, mod, mod, mod, ln_g, ln_b, wr, br)


def _plan_kernel(eid_ref, pos_ref, tile_ref):
    e = eid_ref[0]
    rows = e.shape[0]
    r = lax.broadcasted_iota(jnp.int32, (LANES, LANES), 0)
    c = lax.broadcasted_iota(jnp.int32, (LANES, LANES), 1)
    upper = (r <= c).astype(BF16)
    ones = jnp.ones((LANES, LANES), BF16)
    lower = (lax.broadcasted_iota(jnp.int32, (rows, rows), 1)
             < lax.broadcasted_iota(jnp.int32, (rows, rows), 0)).astype(BF16)
    lane = lax.broadcasted_iota(jnp.int32, (1, LANES), 1)
    masks = [e == ex for ex in range(N_EXPERTS)]
    stack = jnp.concatenate([m.astype(BF16) for m in masks], axis=0)
    incl_all = _dot(stack, upper)
    tot_all = _dot(stack, ones)
    pos = jnp.zeros(e.shape, F32)
    base = jnp.zeros((1, LANES), F32)
    valid_end = jnp.zeros((1, LANES), F32)
    pad_end = jnp.zeros((1, LANES), F32)
    for ex, m in enumerate(masks):
        incl = incl_all[ex * rows:(ex + 1) * rows]
        row_tot = tot_all[ex * rows:(ex + 1) * rows]
        row_off = _dot(lower, row_tot.astype(BF16))
        cnt = jnp.sum(row_tot, axis=0, keepdims=True)
        pos = jnp.where(m, base + row_off + incl - 1.0, pos)
        valid_end = jnp.where(lane == ex, base + cnt, valid_end)
        base = base + jnp.ceil(cnt * (1.0 / TMX)) * TMX
        pad_end = jnp.where(lane == ex, base, pad_end)
    pos_ref[0] = pos.astype(jnp.int32)
    sub = lax.broadcasted_iota(jnp.int32, (8, LANES), 0)
    rows8 = lambda v: jnp.broadcast_to(v, (8, LANES))
    table = jnp.where(sub == PLAN_N_TILES, rows8(base * (1.0 / TMX)),
                      jnp.where(sub == PLAN_VALID_END, rows8(valid_end),
                                jnp.where(sub == PLAN_PAD_END, rows8(pad_end), 0.0)))
    tile_ref[0] = table.astype(jnp.int32)


def _plan(eid):
    rows = 2 * T_STREAM // LANES
    return pl.pallas_call(
        _plan_kernel,
        grid=(N_STREAMS,),
        in_specs=[pl.BlockSpec((1, rows, LANES), lambda s: (s, 0, 0))],
        out_specs=[pl.BlockSpec((1, rows, LANES), lambda s: (s, 0, 0)), pl.BlockSpec((1, 8, LANES), lambda s: (s, 0, 0))],
        out_shape=[jax.ShapeDtypeStruct((N_STREAMS, rows, LANES), jnp.int32),
                   jax.ShapeDtypeStruct((N_STREAMS, 8, LANES), jnp.int32)],
        compiler_params=pltpu.CompilerParams(dimension_semantics=("arbitrary",), vmem_limit_bytes=VMEM_LIMIT),
        name="plan",
    )(eid)


def _moe_kernel(pos_ref, wpair_ref, plan_ref, h2g_hbm, wg_ref, wu_ref, wd_ref, y_hbm,
                h2v, yv, off, xg0, xg1, yg0, yg1, wgb, wub, wdb, sem):
    s = pl.program_id(0)
    e = pl.program_id(1)
    n_pairs = 2 * T_STREAM
    spare_off = T_STREAM * GROUP_ROWS
    plan = lambda row, idx: plan_ref[(s * 8 + row) * LANES + idx]
    n_tiles = plan(PLAN_N_TILES, 0)
    last = n_tiles - 1

    xgs = (xg0, xg1)
    ygs = (yg0, yg1)

    def row_offset(pair):
        return pl.multiple_of((pair & (SLOT_PITCH - 1)) << 3, GROUP_ROWS)

    def gather(tile, dst):
        base = tile * TMX
        for j in range(TMX):
            dst[pl.ds(j, GROUP_ROWS, stride=XSTRIDE), :] = h2v[pl.ds(row_offset(off[base + j]), GROUP_ROWS), :]

    def h2_copy():
        return pltpu.make_async_copy(
            h2g_hbm.at[pl.ds(pl.multiple_of(s * (T_STREAM * GROUP_ROWS), 8), T_STREAM * GROUP_ROWS)],
            h2v.at[pl.ds(0, T_STREAM * GROUP_ROWS)], sem.at[0])

    def y_copy():
        return pltpu.make_async_copy(yv.at[pl.ds(0, T_STREAM * GROUP_ROWS)], y_hbm.at[s], sem.at[1])

    @pl.when(e == 0)
    def _():
        h2_copy().start()
        yv[...] = jnp.zeros_like(yv)
        h2v[spare_off:spare_off + GROUP_ROWS, :] = jnp.zeros((GROUP_ROWS, LANES), F32)

        def pad_expert(ex, carry):
            lo = plan(PLAN_VALID_END, ex)

            def pad_chunk(k, c):
                for u in range(PAD_CHUNK):
                    off[lo + k * PAD_CHUNK + u] = T_STREAM
                return c
            n_chunks = (plan(PLAN_PAD_END, ex) - lo + (PAD_CHUNK - 1)) // PAD_CHUNK
            return lax.fori_loop(0, n_chunks, pad_chunk, carry)
        lax.fori_loop(0, N_EXPERTS, pad_expert, 0)

        def place(i, carry):
            for u in range(PLACE_UNROLL):
                tok = i * PLACE_UNROLL + u
                for slot in range(2):
                    off[pos_ref[s * n_pairs + slot * T_STREAM + tok]] = slot * SLOT_PITCH + tok
            return carry
        lax.fori_loop(0, T_STREAM // PLACE_UNROLL, place, 0)
        yg1[...] = jnp.zeros_like(yg1)
        h2_copy().wait()
        gather(0, xg0)

    t_lo = jnp.where(e == 0, 0, plan(PLAN_PAD_END, jnp.maximum(e - 1, 0))) // TMX
    t_hi = plan(PLAN_PAD_END, e) // TMX

    @pl.when(t_hi > t_lo)
    def _():
        wgb[...] = wg_ref[0, 0].astype(BF16)
        wub[...] = wu_ref[0, 0].astype(BF16)
        wdb[...] = wd_ref[0, 0].astype(BF16)

    def add_tile(tile, done):
        base = tile * TMX
        for j0 in range(0, TMX, RMW_BATCH):
            updates = []
            for j in range(j0, j0 + RMW_BATCH):
                pair = off[base + j]
                dst = row_offset(pair)
                gate = wpair_ref[s * (2 * SLOT_PITCH) + pair]
                row = done[pl.ds(j, GROUP_ROWS, stride=XSTRIDE), :]
                updates.append((dst, yv[pl.ds(dst, GROUP_ROWS), :] + gate * row))
            for dst, val in updates:
                yv[pl.ds(dst, GROUP_ROWS), :] = val

    def step(t, cur, nxt, out, done):
        gather(jnp.minimum(t + 1, last), nxt)
        x = jnp.concatenate([cur[c * XSTRIDE:c * XSTRIDE + TMX, :] for c in range(GROUP_ROWS)], axis=1).astype(BF16)
        hg = _dot(x, wgb[...])
        hu = _dot(x, wub[...])
        act = hg * jax.nn.sigmoid(hg) * hu
        y = _dot(act.astype(BF16), wdb[...])
        for c in range(GROUP_ROWS):
            out[c * XSTRIDE:c * XSTRIDE + TMX, :] = y[:, c * LANES:(c + 1) * LANES]
        add_tile(jnp.maximum(t - 1, 0), done)

    def tile_body(t, carry):
        for par in range(2):
            @pl.when((t & 1) == par)
            def _():
                step(t, xgs[par], xgs[1 - par], ygs[par], ygs[1 - par])
        return carry
    lax.fori_loop(t_lo, t_hi, tile_body, 0)

    @pl.when(e == N_EXPERTS - 1)
    def _():
        for par in range(2):
            @pl.when((last & 1) == par)
            def _():
                add_tile(last, ygs[par])
        y_copy().start()
        y_copy().wait()


def _moe(layer, pos, wpair, plan, h2g, w_gate, w_up, w_down):
    wspec = lambda shape: pl.BlockSpec((1,) + shape, lambda s, e, pos, wpair, plan: (layer, e, 0, 0))
    return pl.pallas_call(
        _moe_kernel,
        grid_spec=pltpu.PrefetchScalarGridSpec(
            num_scalar_prefetch=3,
            grid=(N_STREAMS, N_EXPERTS),
            in_specs=[pl.BlockSpec(memory_space=pl.ANY),
                      wspec((1, D_MODEL, D_EXPERT)), wspec((1, D_MODEL, D_EXPERT)), wspec((1, D_EXPERT, D_MODEL))],
            out_specs=pl.BlockSpec(memory_space=pl.ANY),
            scratch_shapes=[pltpu.VMEM(((T_STREAM + 1) * GROUP_ROWS, LANES), F32),
                            pltpu.VMEM(((T_STREAM + 1) * GROUP_ROWS, LANES), F32),
                            pltpu.SMEM((P_MAX + PAD_CHUNK,), jnp.int32),
                            pltpu.VMEM((GROUP_ROWS * XSTRIDE, LANES), F32),
                            pltpu.VMEM((GROUP_ROWS * XSTRIDE, LANES), F32),
                            pltpu.VMEM((GROUP_ROWS * XSTRIDE, LANES), F32),
                            pltpu.VMEM((GROUP_ROWS * XSTRIDE, LANES), F32),
                            pltpu.VMEM((D_MODEL, D_EXPERT), BF16), pltpu.VMEM((D_MODEL, D_EXPERT), BF16),
                            pltpu.VMEM((D_EXPERT, D_MODEL), BF16),
                            pltpu.SemaphoreType.DMA((2,))]),
        out_shape=jax.ShapeDtypeStruct((N_STREAMS, T_STREAM * GROUP_ROWS, LANES), F32),
        compiler_params=pltpu.CompilerParams(
            dimension_semantics=("arbitrary", "arbitrary"), vmem_limit_bytes=VMEM_LIMIT),
        name="moe",
    )(pos, wpair, plan, h2g, w_gate, w_up, w_down)


def _groups_to_rows(src_ref, stage_ref, n_rows):
    stride = n_rows + 8
    for j in range(n_rows):
        stage_ref[pl.ds(j, GROUP_ROWS, stride=stride), :] = src_ref[GROUP_ROWS * j:GROUP_ROWS * (j + 1), :]
    return jnp.concatenate([stage_ref[c * stride:c * stride + n_rows, :] for c in range(GROUP_ROWS)], axis=1)


def _final_kernel(x1_ref, y_ref, g2_ref, lg_ref, lb_ref, o_ref, stage_ref):
    moe = _groups_to_rows(y_ref, stage_ref, TT)
    o_ref[...] = _layer_norm(ALPHA * x1_ref[...] + g2_ref[0] * moe, lg_ref[0], lb_ref[0])


def _final(layer, stream, x1, y, mod, ln_g, ln_b):
    first = stream * (T_STREAM // TT)
    g2 = pl.BlockSpec(
        (1, 1, D_MODEL), lambda i: ((layer * MOD_ROWS + _mod_row_of_tile(first + i, TT)) * 6 + 5, 0, 0))
    return pl.pallas_call(
        _final_kernel,
        grid=(T_STREAM // TT,),
        in_specs=[pl.BlockSpec((TT, D_MODEL), lambda i: (first + i, 0)),
                  pl.BlockSpec((TT * GROUP_ROWS, LANES), lambda i: (first + i, 0)),
                  g2, _layer_spec(layer, (1, D_MODEL)), _layer_spec(layer, (1, D_MODEL))],
        out_specs=pl.BlockSpec((TT, D_MODEL), lambda i: (i, 0)),
        out_shape=jax.ShapeDtypeStruct((T_STREAM, D_MODEL), F32),
        scratch_shapes=[pltpu.VMEM((GROUP_ROWS * (TT + 8), LANES), F32)],
        compiler_params=pltpu.CompilerParams(dimension_semantics=("arbitrary",), vmem_limit_bytes=VMEM_LIMIT),
        name="final",
    )(x1, y, mod, ln_g, ln_b)


def _rope_tables():
    t = np.arange(DEC_SEQ)
    pos = np.stack([t // GRID_W, t % GRID_W], axis=1).astype(np.float32)
    nf = HEAD_DIM // 4
    inv = jnp.asarray(ROPE_THETA, F32) ** (-jnp.arange(nf, dtype=F32) / nf)
    d = np.arange(LANES) % HEAD_DIM
    which = d // (HEAD_DIM // 2)
    ang = jnp.asarray(pos)[:, which] * inv[d % nf][None, :]
    sign = np.where((d % 32) < 16, -1.0, 1.0).astype(np.float32)
    cos = jnp.concatenate([jnp.cos(ang), jnp.ones((TT, LANES), F32)], axis=0)
    sin = jnp.concatenate([jnp.sin(ang) * sign[None, :], jnp.zeros((TT, LANES), F32)], axis=0)
    return cos, sin


def _block_ones():
    h = np.arange(QB_W) // HEAD_DIM
    return jnp.asarray((h[:, None] == h[None, :]).astype(np.float32), dtype=BF16)


def _pool_block_diag(pool_w):
    out = jnp.zeros((POOL_WIDTH, POOL_WIDTH), F32)
    for g in range(4):
        out = out.at[64 * g:64 * g + 64, 64 * g:64 * g + 64].set(pool_w[g])
    return out


def kernel(x_prompt, x_sample, cache_b_k, cache_b_v, cache_c_k, cache_c_v, c, c_ctx, w_ada, b_ada, w_in, w_out,
           pool_w, pool_scale, q_norm, k_norm, rpb, ln1_g, ln1_b, ln2_g, ln2_b, router_g, router_g_b, router_e,
           router_e_b, w_gate, w_up, w_down):
    x_ctx = x_prompt.reshape(T_CTX, D_MODEL)
    x_lat = x_sample.reshape(T_LAT, D_MODEL)
    c8 = jnp.concatenate([c, c_ctx[None], jnp.zeros((MOD_ROWS - DEC_BATCH - 1, D_MODEL), F32)], axis=0)
    mod = _adaln(c8, w_ada, b_ada).reshape(DEPTH * MOD_ROWS * 6, 1, D_MODEL)

    cos_t, sin_t = _rope_tables()
    ones_bd = _block_ones()
    cbk = cache_b_k.reshape(DEC_BATCH, DEPTH, PAST_LEN, KB_W)
    cbv = cache_b_v.reshape(DEC_BATCH, DEPTH, PAST_LEN, KB_W)
    cck = cache_c_k.reshape(DEC_BATCH, DEPTH, PAST_LEN, C_W)
    ccv = cache_c_v.reshape(DEC_BATCH, DEPTH, PAST_LEN, C_W)
    pad = jnp.zeros((DEPTH, D_MODEL, ROUTER_LANES - N_GROUPS - N_EXPERTS), F32)
    wr = jnp.concatenate([router_g, router_e, pad], axis=2)
    br = jnp.concatenate([router_g_b, router_e_b, pad[:, 0]], axis=1)[:, None]
    per_layer = lambda p: p[:, None]

    caches = ()
    for l in range(DEPTH):
        qn = jnp.tile(q_norm[l], H_B)[None]
        kn = jnp.tile(k_norm[l], KV_B)[None]
        a, qb, kb, vb, qc, kc, vc = _proj(l, x_ctx, x_lat, mod, w_in, ones_bd, qn, kn, cos_t, sin_t)
        ya = _pool(a, _pool_block_diag(pool_w[l]), pool_scale[l][None])
        yb_ctx, yc_ctx, *caches = _attn_ctx(l, qb, kb, vb, qc, kc, vc, caches)
        yb_lat = _attn_latb(l, qb, kb, vb, cbk, cbv)
        yc_lat = _natten(l, qc, kc, vc, cck, ccv, _natten_blocks(rpb[l]))
        x1, h2g, ids, wts = _tail(l, x_ctx, x_lat, ya, yb_ctx, yb_lat, yc_ctx, yc_lat, w_out, mod,
                                  per_layer(ln1_g), per_layer(ln1_b), wr, br)
        pairs = lambda a: a.reshape(N_STREAMS, T_STREAM // TT, 8, TT)[:, :, :2, :].transpose(0, 2, 1, 3)
        pos, tiles = _plan(pairs(ids).reshape(N_STREAMS, 2 * T_STREAM // LANES, LANES))
        gates = jnp.pad(pairs(wts).reshape(N_STREAMS, 2, T_STREAM), ((0, 0), (0, 0), (0, SLOT_PITCH - T_STREAM)))
        y = _moe(l, pos.reshape(-1), gates.reshape(-1), tiles.reshape(-1), h2g, w_gate, w_up, w_down)
        y = y.reshape(T_ALL * GROUP_ROWS, LANES)
        x_ctx = _final(l, 0, x1, y, mod, per_layer(ln2_g), per_layer(ln2_b))
        x_lat = _final(l, 1, x1, y, mod, per_layer(ln2_g), per_layer(ln2_b))

    y_prompt = x_ctx.reshape(BATCH, SEQ, D_MODEL)
    y_sample = x_lat.reshape(DEC_BATCH, DEC_SEQ, D_MODEL)
    heads_last = lambda t, h: t.reshape(BATCH, DEPTH, h, HEAD_DIM, SEQ).transpose(0, 1, 4, 2, 3)
    new_bk, new_bv, new_ck, new_cv = caches
    return (y_prompt, y_sample, heads_last(new_bk, KV_B), heads_last(new_bv, KV_B),
            heads_last(new_ck, H_C), heads_last(new_cv, H_C))
```

```python
import functools

import numpy as np
import jax
import jax.numpy as jnp
from jax import lax
from jax.experimental import pallas as pl
from jax.experimental.pallas import tpu as pltpu

F32 = jnp.float32
BF16 = jnp.bfloat16

D_MODEL = 1024
BATCH = 16
SEQ = 256
DEPTH = 2
DEC_BATCH = 4
DEC_SEQ = 1024
PAST_LEN = 256
GRID_W = 64
GRID_ROWS = DEC_SEQ // GRID_W
HEAD_DIM = 64
POOL_WIDTH = 256
POOL_WINDOWS = (2, 4, 8, 16)
H_B = 6
KV_B = 2
H_C = 6
WIN_R = 8
WIN_C = 16
ROPE_THETA = 10000.0
QB_W = H_B * HEAD_DIM
KB_W = KV_B * HEAD_DIM
C_W = H_C * HEAD_DIM
PROJ_WIDTH = 2048
N_GROUPS = 4
EXPERTS_PER_GROUP = 8
N_EXPERTS = 32
D_EXPERT = 256
ALPHA = (2 * DEPTH) ** 0.25
LN_EPS = 1e-6
RMS_EPS = 1e-6
NEG = -1e30
ATTN_SCALE = HEAD_DIM ** -0.5

T_CTX = BATCH * SEQ
T_LAT = DEC_BATCH * DEC_SEQ
T_ALL = T_CTX + T_LAT

LANES = 128
ROUTER_LANES = 128
EXPERT_LANE0 = N_GROUPS
MOD_ROWS = 8
CTX_MOD_ROW = DEC_BATCH

TT = 512
TP = 1024
POOL_GAP = 16
N_STREAMS = 2
T_STREAM = T_ALL // N_STREAMS
TMX = 128
PLACE_UNROLL = 8
PAD_CHUNK = 8
SLOT_PITCH = 2 * T_STREAM
P_MAX = 2 * T_STREAM + N_EXPERTS * TMX
NT_MAX = P_MAX // TMX
GROUP_ROWS = D_MODEL // LANES
XSTRIDE = TMX + 8
RMW_BATCH = 16
PLAN_N_TILES, PLAN_VALID_END, PLAN_PAD_END = 0, 1, 2
HALF = DEC_SEQ // 2
NAT_KEYS = 12 * GRID_W
VMEM_LIMIT = 56 * 1024 * 1024


def _dot(a, b):
    return jnp.dot(a, b, preferred_element_type=F32)


def _dot_nt(a, b):
    return lax.dot_general(a, b, (((1,), (1,)), ((), ())), preferred_element_type=F32)


def _split_bf16(x):
    hi = x.astype(BF16)
    lo = (x - hi.astype(F32)).astype(BF16)
    return hi, lo


def _layer_norm(y, g, b):
    mu = jnp.mean(y, axis=-1, keepdims=True)
    var = jnp.mean(jnp.square(y - mu), axis=-1, keepdims=True)
    return (y - mu) * lax.rsqrt(var + LN_EPS) * g + b


def _adaln_kernel(c_ref, w_ref, b_ref, o_ref):
    c = c_ref[...]
    s = (c * jax.nn.sigmoid(c)).astype(BF16)
    o_ref[0] = _dot(s, w_ref[0].astype(BF16)) + b_ref[0]


def _adaln(c8, w_ada, b_ada):
    tn = 1536
    n = w_ada.shape[-1]
    return pl.pallas_call(
        _adaln_kernel,
        grid=(DEPTH, n // tn),
        in_specs=[
            pl.BlockSpec((MOD_ROWS, D_MODEL), lambda l, j: (0, 0)),
            pl.BlockSpec((1, D_MODEL, tn), lambda l, j: (l, 0, j)),
            pl.BlockSpec((1, 1, tn), lambda l, j: (l, 0, j)),
        ],
        out_specs=pl.BlockSpec((1, MOD_ROWS, tn), lambda l, j: (l, 0, j)),
        out_shape=jax.ShapeDtypeStruct((DEPTH, MOD_ROWS, n), F32),
        compiler_params=pltpu.CompilerParams(
            dimension_semantics=("arbitrary", "arbitrary"), vmem_limit_bytes=VMEM_LIMIT),
        name="adaln",
    )(c8, w_ada, b_ada.reshape(DEPTH, 1, n))


def _mod_row_of_tile(i, tile):
    n_ctx = T_CTX // tile
    per_req = DEC_SEQ // tile
    return jnp.where(i < n_ctx, CTX_MOD_ROW, (i - n_ctx) // per_req)


def _mod_spec(layer, chunk, tile):
    return pl.BlockSpec(
        (1, 1, D_MODEL), lambda i: ((layer * MOD_ROWS + _mod_row_of_tile(i, tile)) * 6 + chunk, 0, 0))


def _layer_spec(layer, shape):
    return pl.BlockSpec((1,) + shape, lambda *_: (layer,) + (0,) * len(shape))


N_CTX_TILES = T_CTX // TT


def _split_row_specs(width):
    return (pl.BlockSpec((TT, width), lambda i: (jnp.minimum(i, N_CTX_TILES - 1), 0)),
            pl.BlockSpec((TT, width), lambda i: (jnp.maximum(i - N_CTX_TILES, 0), 0)))


def _split_rows(ctx_ref, lat_ref):
    return jnp.where(pl.program_id(0) < N_CTX_TILES, ctx_ref[...], lat_ref[...])


def _rms_norm_heads(x, ones_bd, w):
    hi, lo = _split_bf16(x * x)
    ssq = _dot(hi, ones_bd) + _dot(lo, ones_bd)
    return x * lax.rsqrt(ssq * (1.0 / HEAD_DIM) + RMS_EPS) * w


def _rope_cols(x, cos, sin, first16):
    cols = []
    for j in range(x.shape[1] // LANES):
        xc = x[:, j * LANES:(j + 1) * LANES]
        partner = jnp.where(first16, pltpu.roll(xc, LANES - 16, axis=1), pltpu.roll(xc, 16, axis=1))
        cols.append(xc * cos + partner * sin)
    return jnp.concatenate(cols, axis=1) if len(cols) > 1 else cols[0]


def _proj_kernel(xc_ref, xl_ref, sh_ref, sc_ref, w_ref, ones_ref, qn_ref, kn_ref, cos_ref, sin_ref,
                 a_ref, qb_ref, kb_ref, vb_ref, qc_ref, kc_ref, vc_ref, wbf_ref):
    @pl.when(pl.program_id(0) == 0)
    def _():
        wbf_ref[...] = w_ref[0].astype(BF16)

    h = _split_rows(xc_ref, xl_ref) * (1.0 + sc_ref[0]) + sh_ref[0]
    p = _dot(h.astype(BF16), wbf_ref[...])
    o = 0
    a_ref[...] = p[:, o:o + POOL_WIDTH]; o += POOL_WIDTH
    qb = p[:, o:o + QB_W]; o += QB_W
    kb = p[:, o:o + KB_W]; o += KB_W
    vb_ref[...] = p[:, o:o + KB_W]; o += KB_W
    qc_ref[...] = (p[:, o:o + C_W] * ATTN_SCALE).astype(BF16); o += C_W
    kc_ref[...] = p[:, o:o + C_W]; o += C_W
    vc_ref[...] = p[:, o:o + C_W]

    ones_bd = ones_ref[...]
    cos = cos_ref[...]
    sin = sin_ref[...]
    lane = lax.broadcasted_iota(jnp.int32, (1, LANES), 1)
    first16 = (lane & 31) < 16
    qb = _rope_cols(_rms_norm_heads(qb, ones_bd, qn_ref[...]), cos, sin, first16)
    kb = _rope_cols(_rms_norm_heads(kb, ones_bd[:KB_W, :KB_W], kn_ref[...]), cos, sin, first16)
    qb_ref[...] = (qb * ATTN_SCALE).astype(BF16)
    kb_ref[...] = kb


def _proj(layer, x_ctx, x_lat, mod, w_in, ones_bd, qn, kn, cos_t, sin_t):
    n_tiles = T_ALL // TT
    n_ctx = N_CTX_TILES
    per_req = DEC_SEQ // TT

    def rope_idx(i):
        return (jnp.where(i < n_ctx, per_req, (i - n_ctx) % per_req), 0)

    row = lambda w: pl.BlockSpec((TT, w), lambda i: (i, 0))
    const = lambda s: pl.BlockSpec(s, lambda i: (0,) * len(s))
    widths = (POOL_WIDTH, QB_W, KB_W, KB_W, C_W, C_W, C_W)
    dtypes = (F32, BF16, F32, F32, BF16, F32, F32)
    return pl.pallas_call(
        _proj_kernel,
        grid=(n_tiles,),
        in_specs=[*_split_row_specs(D_MODEL), _mod_spec(layer, 0, TT), _mod_spec(layer, 1, TT),
                  _layer_spec(layer, (D_MODEL, PROJ_WIDTH)),
                  const((QB_W, QB_W)), const((1, QB_W)), const((1, KB_W)),
                  pl.BlockSpec((TT, LANES), rope_idx), pl.BlockSpec((TT, LANES), rope_idx)],
        out_specs=[row(w) for w in widths],
        out_shape=[jax.ShapeDtypeStruct((T_ALL, w), d) for w, d in zip(widths, dtypes)],
        scratch_shapes=[pltpu.VMEM((D_MODEL, PROJ_WIDTH), BF16)],
        compiler_params=pltpu.CompilerParams(dimension_semantics=("arbitrary",), vmem_limit_bytes=VMEM_LIMIT),
        name="proj",
    )(x_ctx, x_lat, mod, mod, w_in, ones_bd, qn, kn, cos_t, sin_t)


def _pool_windows(p):
    n = p.shape[0]
    sh = lambda x, k: pltpu.roll(x, (n - k) % n, axis=0)
    w2 = p + sh(p, -1)
    w4 = sh(w2, -1) + sh(w2, 1)
    w8 = sh(w4, -2) + sh(w4, 2)
    w16 = sh(w8, -4) + sh(w8, 4)
    g = lax.broadcasted_iota(jnp.int32, p.shape, 1) >> 6
    return jnp.where(g == 0, w2, jnp.where(g == 1, w4, jnp.where(g == 2, w8, w16)))


def _pool_kernel(a_ref, w_ref, scale_ref, o_ref, pad_ref):
    a = a_ref[...]

    def mix(csum, seq_len):
        row = lax.broadcasted_iota(jnp.int32, a.shape, 0)
        g = lax.broadcasted_iota(jnp.int32, a.shape, 1) >> 6
        t = row & (seq_len - 1)
        half = jnp.where(g == 0, 1, jnp.where(g == 1, 2, jnp.where(g == 2, 4, 8)))
        cnt = jnp.minimum(t + half, seq_len) - jnp.maximum(t - half, 0)
        pooled = csum / cnt.astype(F32) - a
        y = _dot(pooled.astype(BF16), w_ref[...].astype(BF16)) * scale_ref[...]
        o_ref[...] = y.astype(BF16)

    def padded_sums(seq_len):
        pitch = seq_len + POOL_GAP
        n_req = TP // seq_len
        n = n_req * pitch
        pad_ref[...] = jnp.zeros_like(pad_ref)
        for r in range(n_req):
            pad_ref[r * pitch + POOL_GAP // 2:r * pitch + POOL_GAP // 2 + seq_len, :] = a[r * seq_len:(r + 1) * seq_len]
        sums = _pool_windows(pad_ref[0:n, :])
        return jnp.concatenate(
            [sums[r * pitch + POOL_GAP // 2:r * pitch + POOL_GAP // 2 + seq_len] for r in range(n_req)], axis=0)

    is_ctx = pl.program_id(0) < T_CTX // TP

    @pl.when(is_ctx)
    def _():
        mix(padded_sums(SEQ), SEQ)

    @pl.when(jnp.logical_not(is_ctx))
    def _():
        mix(padded_sums(DEC_SEQ), DEC_SEQ)


def _pool(a, w_bd, scale):
    return pl.pallas_call(
        _pool_kernel,
        grid=(T_ALL // TP,),
        in_specs=[pl.BlockSpec((TP, POOL_WIDTH), lambda i: (i, 0)),
                  pl.BlockSpec((POOL_WIDTH, POOL_WIDTH), lambda i: (0, 0)),
                  pl.BlockSpec((1, POOL_WIDTH), lambda i: (0, 0))],
        out_specs=pl.BlockSpec((TP, POOL_WIDTH), lambda i: (i, 0)),
        out_shape=jax.ShapeDtypeStruct((T_ALL, POOL_WIDTH), BF16),
        scratch_shapes=[pltpu.VMEM(((TP // SEQ) * (SEQ + POOL_GAP), POOL_WIDTH), F32)],
        compiler_params=pltpu.CompilerParams(dimension_semantics=("arbitrary",), vmem_limit_bytes=VMEM_LIMIT),
        name="pool",
    )(a, w_bd, scale)


def _softmax_pv(scores, values):
    m = scores[0].max(axis=-1, keepdims=True)
    for s in scores[1:]:
        m = jnp.maximum(m, s.max(axis=-1, keepdims=True))
    ps = [jnp.exp(s - m) for s in scores]
    l = ps[0].sum(axis=-1, keepdims=True)
    for p in ps[1:]:
        l = l + p.sum(axis=-1, keepdims=True)
    o = _dot(ps[0].astype(BF16), values[0])
    for p, v in zip(ps[1:], values[1:]):
        o = o + _dot(p.astype(BF16), v)
    return o * (1.0 / l)


def _lane_halves():
    lane = lax.broadcasted_iota(jnp.int32, (1, LANES), 1)
    return lane < HEAD_DIM, lane >= HEAD_DIM


def _keep(x, mask):
    return jnp.where(mask, x, 0.0).astype(BF16)


def _gqa_variants(x):
    lo, hi = _lane_halves()
    xs = pltpu.roll(x, HEAD_DIM, axis=1)
    nat_lo, nat_hi = _keep(x, lo), _keep(x, hi)
    sw_lo, sw_hi = _keep(xs, lo), _keep(xs, hi)
    return ((nat_lo, sw_hi), (nat_lo, nat_hi), (sw_lo, nat_hi))


def _mha_variants(x):
    lo, hi = _lane_halves()
    out = []
    for j in range(x.shape[1] // LANES):
        xc = x[:, j * LANES:(j + 1) * LANES]
        out.append((_keep(xc, lo), _keep(xc, hi)))
    return tuple(out)


def _attend_cols(q, k_vars, v_vars, extra_k=None, extra_v=None, bias=None):
    cols = []
    for j in range(q.shape[1] // LANES):
        qc = q[:, j * LANES:(j + 1) * LANES]
        o = None
        for hh in range(2):
            s = _dot_nt(qc, k_vars[j][hh])
            if bias is not None:
                s = s + bias[j][hh]
            scores, values = [s], [v_vars[j][hh]]
            if extra_k is not None:
                scores.append(_dot_nt(qc, extra_k[j][hh]))
                values.append(extra_v[j][hh])
            oh = _softmax_pv(scores, values)
            o = oh if o is None else o + oh
        cols.append(o)
    return jnp.concatenate(cols, axis=1)


def _attn_ctx_kernel(layer_slot, qb_ref, kb_ref, vb_ref, qc_ref, kc_ref, vc_ref, *refs):
    yb_ref, yc_ref, kbt_ref, vbt_ref, kct_ref, vct_ref = refs[-6:]
    kb, vb, kc, vc = kb_ref[...], vb_ref[...], kc_ref[...], vc_ref[...]
    yb = _attend_cols(qb_ref[...], _gqa_variants(kb), _gqa_variants(vb))
    yb_ref[...] = yb.astype(BF16)
    yc = _attend_cols(qc_ref[...], _mha_variants(kc), _mha_variants(vc))
    yc_ref[...] = yc.astype(BF16)
    for ref, val in ((kbt_ref, kb), (vbt_ref, vb), (kct_ref, kc), (vct_ref, vc)):
        for slot in range(ref.shape[1]):
            ref[0, slot] = val.T if slot == layer_slot else jnp.zeros(ref.shape[2:], F32)


CACHE_WIDTHS = (KB_W, KB_W, C_W, C_W)


def _attn_ctx(layer, qb, kb, vb, qc, kc, vc, caches):
    row = lambda w: pl.BlockSpec((SEQ, w), lambda i: (i, 0))
    if caches:
        cache = lambda w: pl.BlockSpec((1, 1, w, SEQ), lambda i: (i, layer, 0, 0))
        layer_slot = 0
    else:
        cache = lambda w: pl.BlockSpec((1, DEPTH, w, SEQ), lambda i: (i, 0, 0, 0))
        layer_slot = layer
    return pl.pallas_call(
        functools.partial(_attn_ctx_kernel, layer_slot),
        grid=(BATCH,),
        in_specs=[row(QB_W), row(KB_W), row(KB_W), row(C_W), row(C_W), row(C_W)]
                 + [pl.BlockSpec(memory_space=pl.ANY)] * len(caches),
        out_specs=[row(QB_W), row(C_W)] + [cache(w) for w in CACHE_WIDTHS],
        out_shape=[jax.ShapeDtypeStruct((T_CTX, QB_W), BF16), jax.ShapeDtypeStruct((T_CTX, C_W), BF16)]
                  + [jax.ShapeDtypeStruct((BATCH, DEPTH, w, SEQ), F32) for w in CACHE_WIDTHS],
        input_output_aliases={6 + k: 2 + k for k in range(len(caches))},
        compiler_params=pltpu.CompilerParams(dimension_semantics=("arbitrary",), vmem_limit_bytes=VMEM_LIMIT),
        name="attn_ctx",
    )(qb, kb, vb, qc, kc, vc, *caches)


def _attn_latb_kernel(q_ref, k_ref, v_ref, ck_ref, cv_ref, y_ref):
    y = _attend_cols(q_ref[...], _gqa_variants(ck_ref[0, 0]), _gqa_variants(cv_ref[0, 0]),
                     extra_k=_gqa_variants(k_ref[...]), extra_v=_gqa_variants(v_ref[...]))
    y_ref[...] = y.astype(BF16)


def _attn_latb(layer, qb, kb, vb, cache_k, cache_v):
    ctx_h = T_CTX // HALF
    ctx_r = T_CTX // DEC_SEQ
    cache = pl.BlockSpec((1, 1, PAST_LEN, KB_W), lambda b, s: (b, layer, 0, 0))
    own = pl.BlockSpec((DEC_SEQ, KB_W), lambda b, s: (ctx_r + b, 0))
    return pl.pallas_call(
        _attn_latb_kernel,
        grid=(DEC_BATCH, DEC_SEQ // HALF),
        in_specs=[pl.BlockSpec((HALF, QB_W), lambda b, s: (ctx_h + 2 * b + s, 0)), own, own, cache, cache],
        out_specs=pl.BlockSpec((HALF, QB_W), lambda b, s: (2 * b + s, 0)),
        out_shape=jax.ShapeDtypeStruct((T_LAT, QB_W), BF16),
        compiler_params=pltpu.CompilerParams(
            dimension_semantics=("arbitrary", "arbitrary"), vmem_limit_bytes=VMEM_LIMIT),
        name="attn_latb",
    )(qb, kb, vb, cache_k, cache_v)


def _natten_window(s, i):
    r = (HALF // GRID_W) * s + i
    rs = min(max(r - WIN_R // 2, 0), GRID_ROWS - WIN_R)
    return r, rs


def _natten_kernel(q_ref, k_ref, v_ref, ck_ref, cv_ref, blk_ref, y_ref, bias_ref):
    s = pl.program_id(1)

    for sv in range(2):
        @pl.when((pl.program_id(2) == 0) & (s == sv))
        def _():
            lo_half, _ = _lane_halves()
            masked = jnp.full((GRID_W, LANES), NEG, F32)
            for hh in range(2):
                for i in range(HALF // GRID_W):
                    r, rs = _natten_window(sv, i)
                    for jp in range(NAT_KEYS // LANES):
                        pair = []
                        for j in (2 * jp, 2 * jp + 1):
                            rk = 4 * sv + j
                            pair.append(blk_ref[0, hh, rk - r + WIN_R - 1] if rs <= rk < rs + WIN_R else masked)
                        bias_ref[hh, i * GRID_W:(i + 1) * GRID_W, jp * LANES:(jp + 1) * LANES] = (
                            jnp.where(lo_half, pair[0], pair[1]))

    start = pl.multiple_of(s * (4 * GRID_W), 4 * GRID_W)
    k = _mha_variants(k_ref[pl.ds(start, NAT_KEYS), :])
    v = _mha_variants(v_ref[pl.ds(start, NAT_KEYS), :])
    ck = _mha_variants(ck_ref[0, 0])
    cv = _mha_variants(cv_ref[0, 0])
    bias = ((bias_ref[0], bias_ref[1]),)
    y = _attend_cols(q_ref[...], k, v, extra_k=ck, extra_v=cv, bias=bias)
    y_ref[...] = y.astype(BF16)


def _natten(layer, qc, kc, vc, cache_k, cache_v, blocks):
    ctx_h = T_CTX // HALF
    ctx_r = T_CTX // DEC_SEQ
    cache = pl.BlockSpec((1, 1, PAST_LEN, LANES), lambda j, s, b: (b, layer, 0, j))
    own = pl.BlockSpec((DEC_SEQ, LANES), lambda j, s, b: (ctx_r + b, j))
    return pl.pallas_call(
        _natten_kernel,
        grid=(C_W // LANES, DEC_SEQ // HALF, DEC_BATCH),
        in_specs=[pl.BlockSpec((HALF, LANES), lambda j, s, b: (ctx_h + 2 * b + s, j)), own, own, cache, cache,
                  pl.BlockSpec((1, 2, 2 * WIN_R - 1, GRID_W, LANES), lambda j, s, b: (j, 0, 0, 0, 0))],
        out_specs=pl.BlockSpec((HALF, LANES), lambda j, s, b: (2 * b + s, j)),
        out_shape=jax.ShapeDtypeStruct((T_LAT, C_W), BF16),
        scratch_shapes=[pltpu.VMEM((2, HALF, NAT_KEYS), F32)],
        compiler_params=pltpu.CompilerParams(
            dimension_semantics=("arbitrary", "arbitrary", "arbitrary"), vmem_limit_bytes=VMEM_LIMIT),
        name="natten",
    )(qc, kc, vc, cache_k, cache_v, blocks)


def _natten_blocks(rpb):
    qcol = np.arange(GRID_W)[:, None]
    kcol = np.arange(GRID_W)[None, :]
    cs = np.clip(qcol - WIN_C // 2, 0, GRID_W - WIN_C)
    col_ok = (kcol >= cs) & (kcol < cs + WIN_C)
    dc = np.clip(kcol - qcol, -(WIN_C - 1), WIN_C - 1) + (WIN_C - 1)
    sel_c = (dc[:, :, None] == np.arange(2 * WIN_C - 1)).astype(np.float32)
    blk = jnp.einsum("hdm,qkm->hdqk", rpb, jnp.asarray(sel_c), precision=lax.Precision.HIGHEST)
    blk = jnp.where(jnp.asarray(col_ok), blk, NEG)
    blk = jnp.concatenate([blk, blk], axis=-1)
    return blk.reshape(H_C // 2, 2, 2 * WIN_R - 1, GRID_W, LANES)


def _route(lg):
    lane = lax.broadcasted_iota(jnp.int32, lg.shape, 1).astype(F32)
    low = jnp.float32(-3.0e38)
    far = jnp.float32(ROUTER_LANES)
    is_g = lane < N_GROUPS
    gmax = jnp.where(is_g, lg, low).max(axis=-1, keepdims=True)
    gsel = jnp.where(is_g & (lg == gmax), lane, far).min(axis=-1, keepdims=True)
    pg_sel = 1.0 / jnp.where(is_g, jnp.exp(lg - gmax), 0.0).sum(axis=-1, keepdims=True)
    e0 = EXPERT_LANE0 + EXPERTS_PER_GROUP * gsel
    in_g = (lane >= e0) & (lane < e0 + EXPERTS_PER_GROUP)
    m1 = jnp.where(in_g, lg, low).max(axis=-1, keepdims=True)
    i1 = jnp.where(in_g & (lg == m1), lane, far).min(axis=-1, keepdims=True)
    rest = in_g & (lane != i1)
    m2 = jnp.where(rest, lg, low).max(axis=-1, keepdims=True)
    i2 = jnp.where(rest & (lg == m2), lane, far).min(axis=-1, keepdims=True)
    t = jnp.exp(m2 - m1)
    ssum = 1.0 + t
    w1 = (1.0 / ssum) * pg_sel
    w2 = (t / ssum) * pg_sel
    return i1 - EXPERT_LANE0, i2 - EXPERT_LANE0, w1, w2


def _rows_to_groups(x, stage_ref, out_ref, n_rows):
    stride = n_rows + 8
    for c in range(GROUP_ROWS):
        stage_ref[c * stride:c * stride + n_rows, :] = x[:, c * LANES:(c + 1) * LANES]
    for j in range(n_rows):
        out_ref[GROUP_ROWS * j:GROUP_ROWS * (j + 1), :] = stage_ref[pl.ds(j, GROUP_ROWS, stride=stride), :]


def _tail_kernel(xc_ref, xl_ref, ya_ref, ybc_ref, ybl_ref, ycc_ref, ycl_ref, w_ref, g1_ref, sh2_ref, sc2_ref,
                 lg_ref, lb_ref, wr_ref, br_ref, x1_ref, h2g_ref, ids_ref, wts_ref, wbf_ref, stage_ref):
    @pl.when(pl.program_id(0) == 0)
    def _():
        wbf_ref[...] = w_ref[0].astype(BF16)

    o = (_dot(ya_ref[...], wbf_ref[0:POOL_WIDTH, :])
         + _dot(_split_rows(ybc_ref, ybl_ref), wbf_ref[POOL_WIDTH:POOL_WIDTH + QB_W, :])
         + _dot(_split_rows(ycc_ref, ycl_ref), wbf_ref[POOL_WIDTH + QB_W:, :]))
    x1 = _layer_norm(ALPHA * _split_rows(xc_ref, xl_ref) + g1_ref[0] * o, lg_ref[0], lb_ref[0])
    x1_ref[...] = x1
    h2 = x1 * (1.0 + sc2_ref[0]) + sh2_ref[0]
    hh, hl = _split_bf16(h2)
    wh, wl = _split_bf16(wr_ref[0])
    lg = _dot(hh, wh) + _dot(hl, wh) + _dot(hh, wl) + br_ref[0]
    i1, i2, w1, w2 = _route(lg)
    lane = lax.broadcasted_iota(jnp.int32, lg.shape, 1)
    ids = jnp.where(lane == 0, i1, jnp.where(lane == 1, i2, 0.0))
    ids_ref[0] = ids.T[0:8, :].astype(jnp.int32)
    wts = jnp.where(lane == 0, w1, jnp.where(lane == 1, w2, 0.0))
    wts_ref[0] = wts.T[0:8, :]
    _rows_to_groups(h2, stage_ref, h2g_ref, TT)


def _tail(layer, x_ctx, x_lat, ya, yb_ctx, yb_lat, yc_ctx, yc_lat, w_out, mod, ln_g, ln_b, wr, br):
    row = lambda w: pl.BlockSpec((TT, w), lambda i: (i, 0))
    return pl.pallas_call(
        _tail_kernel,
        grid=(T_ALL // TT,),
        in_specs=[*_split_row_specs(D_MODEL), row(POOL_WIDTH), *_split_row_specs(QB_W), *_split_row_specs(C_W),
                  _layer_spec(layer, (D_MODEL, D_MODEL)),
                  _mod_spec(layer, 2, TT), _mod_spec(layer, 3, TT), _mod_spec(layer, 4, TT),
                  _layer_spec(layer, (1, D_MODEL)), _layer_spec(layer, (1, D_MODEL)),
                  _layer_spec(layer, (D_MODEL, ROUTER_LANES)), _layer_spec(layer, (1, ROUTER_LANES))],
        out_specs=[row(D_MODEL), pl.BlockSpec((TT * GROUP_ROWS, LANES), lambda i: (i, 0)),
                   pl.BlockSpec((1, 8, TT), lambda i: (i, 0, 0)), pl.BlockSpec((1, 8, TT), lambda i: (i, 0, 0))],
        out_shape=[jax.ShapeDtypeStruct((T_ALL, D_MODEL), F32),
                   jax.ShapeDtypeStruct((T_ALL * GROUP_ROWS, LANES), F32),
                   jax.ShapeDtypeStruct((T_ALL // TT, 8, TT), jnp.int32),
                   jax.ShapeDtypeStruct((T_ALL // TT, 8, TT), F32)],
        scratch_shapes=[pltpu.VMEM((D_MODEL, D_MODEL), BF16),
                        pltpu.VMEM((GROUP_ROWS * (TT + 8), LANES), F32)],
        compiler_params=pltpu.CompilerParams(dimension_semantics=("arbitrary",), vmem_limit_bytes=VMEM_LIMIT),
        name="tail",
    )(x_ctx, x_lat, ya, yb_ctx, yb_lat, yc_ctx, yc_lat, w_out, mod, mod, mod, ln_g, ln_b, wr, br)


def _plan_kernel(eid_ref, pos_ref, tile_ref):
    e = eid_ref[0]
    rows = e.shape[0]
    r = lax.broadcasted_iota(jnp.int32, (LANES, LANES), 0)
    c = lax.broadcasted_iota(jnp.int32, (LANES, LANES), 1)
    upper = (r <= c).astype(BF16)
    ones = jnp.ones((LANES, LANES), BF16)
    lower = (lax.broadcasted_iota(jnp.int32, (rows, rows), 1)
             < lax.broadcasted_iota(jnp.int32, (rows, rows), 0)).astype(BF16)
    lane = lax.broadcasted_iota(jnp.int32, (1, LANES), 1)
    masks = [e == ex for ex in range(N_EXPERTS)]
    stack = jnp.concatenate([m.astype(BF16) for m in masks], axis=0)
    incl_all = _dot(stack, upper)
    tot_all = _dot(stack, ones)
    pos = jnp.zeros(e.shape, F32)
    base = jnp.zeros((1, LANES), F32)
    valid_end = jnp.zeros((1, LANES), F32)
    pad_end = jnp.zeros((1, LANES), F32)
    for ex, m in enumerate(masks):
        incl = incl_all[ex * rows:(ex + 1) * rows]
        row_tot = tot_all[ex * rows:(ex + 1) * rows]
        row_off = _dot(lower, row_tot.astype(BF16))
        cnt = jnp.sum(row_tot, axis=0, keepdims=True)
        pos = jnp.where(m, base + row_off + incl - 1.0, pos)
        valid_end = jnp.where(lane == ex, base + cnt, valid_end)
        base = base + jnp.ceil(cnt * (1.0 / TMX)) * TMX
        pad_end = jnp.where(lane == ex, base, pad_end)
    pos_ref[0] = pos.astype(jnp.int32)
    sub = lax.broadcasted_iota(jnp.int32, (8, LANES), 0)
    rows8 = lambda v: jnp.broadcast_to(v, (8, LANES))
    table = jnp.where(sub == PLAN_N_TILES, rows8(base * (1.0 / TMX)),
                      jnp.where(sub == PLAN_VALID_END, rows8(valid_end),
                                jnp.where(sub == PLAN_PAD_END, rows8(pad_end), 0.0)))
    tile_ref[0] = table.astype(jnp.int32)


def _plan(eid):
    rows = 2 * T_STREAM // LANES
    return pl.pallas_call(
        _plan_kernel,
        grid=(N_STREAMS,),
        in_specs=[pl.BlockSpec((1, rows, LANES), lambda s: (s, 0, 0))],
        out_specs=[pl.BlockSpec((1, rows, LANES), lambda s: (s, 0, 0)), pl.BlockSpec((1, 8, LANES), lambda s: (s, 0, 0))],
        out_shape=[jax.ShapeDtypeStruct((N_STREAMS, rows, LANES), jnp.int32),
                   jax.ShapeDtypeStruct((N_STREAMS, 8, LANES), jnp.int32)],
        compiler_params=pltpu.CompilerParams(dimension_semantics=("arbitrary",), vmem_limit_bytes=VMEM_LIMIT),
        name="plan",
    )(eid)


def _moe_kernel(pos_ref, wpair_ref, plan_ref, h2g_hbm, wg_ref, wu_ref, wd_ref, y_hbm,
                h2v, yv, off, xg0, xg1, yg0, yg1, wgb, wub, wdb, sem):
    s = pl.program_id(0)
    e = pl.program_id(1)
    n_pairs = 2 * T_STREAM
    spare_off = T_STREAM * GROUP_ROWS
    plan = lambda row, idx: plan_ref[(s * 8 + row) * LANES + idx]
    n_tiles = plan(PLAN_N_TILES, 0)
    last = n_tiles - 1

    xgs = (xg0, xg1)
    ygs = (yg0, yg1)

    def row_offset(pair):
        return pl.multiple_of((pair & (SLOT_PITCH - 1)) << 3, GROUP_ROWS)

    def gather(tile, dst):
        base = tile * TMX
        for j in range(TMX):
            dst[pl.ds(j, GROUP_ROWS, stride=XSTRIDE), :] = h2v[pl.ds(row_offset(off[base + j]), GROUP_ROWS), :]

    def h2_copy():
        return pltpu.make_async_copy(
            h2g_hbm.at[pl.ds(pl.multiple_of(s * (T_STREAM * GROUP_ROWS), 8), T_STREAM * GROUP_ROWS)],
            h2v.at[pl.ds(0, T_STREAM * GROUP_ROWS)], sem.at[0])

    def y_copy(stream):
        return pltpu.make_async_copy(yv.at[pl.ds(0, T_STREAM * GROUP_ROWS)], y_hbm.at[stream], sem.at[1])

    @pl.when(e == 0)
    def _():
        h2_copy().start()
        h2v[spare_off:spare_off + GROUP_ROWS, :] = jnp.zeros((GROUP_ROWS, LANES), F32)

        def pad_expert(ex, carry):
            lo = plan(PLAN_VALID_END, ex)

            def pad_chunk(k, c):
                for u in range(PAD_CHUNK):
                    off[lo + k * PAD_CHUNK + u] = T_STREAM
                return c
            n_chunks = (plan(PLAN_PAD_END, ex) - lo + (PAD_CHUNK - 1)) // PAD_CHUNK
            return lax.fori_loop(0, n_chunks, pad_chunk, carry)
        lax.fori_loop(0, N_EXPERTS, pad_expert, 0)

        def place(i, carry):
            for u in range(PLACE_UNROLL):
                tok = i * PLACE_UNROLL + u
                for slot in range(2):
                    off[pos_ref[s * n_pairs + slot * T_STREAM + tok]] = slot * SLOT_PITCH + tok
            return carry
        lax.fori_loop(0, T_STREAM // PLACE_UNROLL, place, 0)

        @pl.when(s > 0)
        def _():
            y_copy(s - 1).wait()
        yv[...] = jnp.zeros_like(yv)
        yg1[...] = jnp.zeros_like(yg1)
        h2_copy().wait()
        gather(0, xg0)

    t_lo = jnp.where(e == 0, 0, plan(PLAN_PAD_END, jnp.maximum(e - 1, 0))) // TMX
    t_hi = plan(PLAN_PAD_END, e) // TMX

    @pl.when(t_hi > t_lo)
    def _():
        wgb[...] = wg_ref[0, 0].astype(BF16)
        wub[...] = wu_ref[0, 0].astype(BF16)
        wdb[...] = wd_ref[0, 0].astype(BF16)

    def add_tile(tile, done):
        base = tile * TMX
        for j0 in range(0, TMX, RMW_BATCH):
            updates = []
            for j in range(j0, j0 + RMW_BATCH):
                pair = off[base + j]
                dst = row_offset(pair)
                gate = wpair_ref[s * (2 * SLOT_PITCH) + pair]
                row = done[pl.ds(j, GROUP_ROWS, stride=XSTRIDE), :]
                updates.append((dst, yv[pl.ds(dst, GROUP_ROWS), :] + gate * row))
            for dst, val in updates:
                yv[pl.ds(dst, GROUP_ROWS), :] = val

    def step(t, cur, nxt, out, done):
        gather(jnp.minimum(t + 1, last), nxt)
        x = jnp.concatenate([cur[c * XSTRIDE:c * XSTRIDE + TMX, :] for c in range(GROUP_ROWS)], axis=1).astype(BF16)
        hg = _dot(x, wgb[...])
        hu = _dot(x, wub[...])
        act = hg * jax.nn.sigmoid(hg) * hu
        y = _dot(act.astype(BF16), wdb[...])
        for c in range(GROUP_ROWS):
            out[c * XSTRIDE:c * XSTRIDE + TMX, :] = y[:, c * LANES:(c + 1) * LANES]
        add_tile(jnp.maximum(t - 1, 0), done)

    def tile_body(t, carry):
        for par in range(2):
            @pl.when((t & 1) == par)
            def _():
                step(t, xgs[par], xgs[1 - par], ygs[par], ygs[1 - par])
        return carry
    lax.fori_loop(t_lo, t_hi, tile_body, 0)

    @pl.when(e == N_EXPERTS - 1)
    def _():
        for par in range(2):
            @pl.when((last & 1) == par)
            def _():
                add_tile(last, ygs[par])
        y_copy(s).start()

        @pl.when(s == N_STREAMS - 1)
        def _():
            y_copy(s).wait()


def _moe(layer, pos, wpair, plan, h2g, w_gate, w_up, w_down):
    wspec = lambda shape: pl.BlockSpec((1,) + shape, lambda s, e, pos, wpair, plan: (layer, e, 0, 0))
    return pl.pallas_call(
        _moe_kernel,
        grid_spec=pltpu.PrefetchScalarGridSpec(
            num_scalar_prefetch=3,
            grid=(N_STREAMS, N_EXPERTS),
            in_specs=[pl.BlockSpec(memory_space=pl.ANY),
                      wspec((1, D_MODEL, D_EXPERT)), wspec((1, D_MODEL, D_EXPERT)), wspec((1, D_EXPERT, D_MODEL))],
            out_specs=pl.BlockSpec(memory_space=pl.ANY),
            scratch_shapes=[pltpu.VMEM(((T_STREAM + 1) * GROUP_ROWS, LANES), F32),
                            pltpu.VMEM(((T_STREAM + 1) * GROUP_ROWS, LANES), F32),
                            pltpu.SMEM((P_MAX + PAD_CHUNK,), jnp.int32),
                            pltpu.VMEM((GROUP_ROWS * XSTRIDE, LANES), F32),
                            pltpu.VMEM((GROUP_ROWS * XSTRIDE, LANES), F32),
                            pltpu.VMEM((GROUP_ROWS * XSTRIDE, LANES), F32),
                            pltpu.VMEM((GROUP_ROWS * XSTRIDE, LANES), F32),
                            pltpu.VMEM((D_MODEL, D_EXPERT), BF16), pltpu.VMEM((D_MODEL, D_EXPERT), BF16),
                            pltpu.VMEM((D_EXPERT, D_MODEL), BF16),
                            pltpu.SemaphoreType.DMA((2,))]),
        out_shape=jax.ShapeDtypeStruct((N_STREAMS, T_STREAM * GROUP_ROWS, LANES), F32),
        compiler_params=pltpu.CompilerParams(
            dimension_semantics=("arbitrary", "arbitrary"), vmem_limit_bytes=VMEM_LIMIT),
        name="moe",
    )(pos, wpair, plan, h2g, w_gate, w_up, w_down)


def _groups_to_rows(src_ref, stage_ref, n_rows):
    stride = n_rows + 8
    for j in range(n_rows):
        stage_ref[pl.ds(j, GROUP_ROWS, stride=stride), :] = src_ref[GROUP_ROWS * j:GROUP_ROWS * (j + 1), :]
    return jnp.concatenate([stage_ref[c * stride:c * stride + n_rows, :] for c in range(GROUP_ROWS)], axis=1)


def _final_kernel(x1_ref, y_ref, g2_ref, lg_ref, lb_ref, o_ref, stage_ref):
    moe = _groups_to_rows(y_ref, stage_ref, TT)
    o_ref[...] = _layer_norm(ALPHA * x1_ref[...] + g2_ref[0] * moe, lg_ref[0], lb_ref[0])


def _final(layer, stream, x1, y, mod, ln_g, ln_b):
    first = stream * (T_STREAM // TT)
    g2 = pl.BlockSpec(
        (1, 1, D_MODEL), lambda i: ((layer * MOD_ROWS + _mod_row_of_tile(first + i, TT)) * 6 + 5, 0, 0))
    return pl.pallas_call(
        _final_kernel,
        grid=(T_STREAM // TT,),
        in_specs=[pl.BlockSpec((TT, D_MODEL), lambda i: (first + i, 0)),
                  pl.BlockSpec((TT * GROUP_ROWS, LANES), lambda i: (first + i, 0)),
                  g2, _layer_spec(layer, (1, D_MODEL)), _layer_spec(layer, (1, D_MODEL))],
        out_specs=pl.BlockSpec((TT, D_MODEL), lambda i: (i, 0)),
        out_shape=jax.ShapeDtypeStruct((T_STREAM, D_MODEL), F32),
        scratch_shapes=[pltpu.VMEM((GROUP_ROWS * (TT + 8), LANES), F32)],
        compiler_params=pltpu.CompilerParams(dimension_semantics=("arbitrary",), vmem_limit_bytes=VMEM_LIMIT),
        name="final",
    )(x1, y, mod, ln_g, ln_b)


def _rope_tables():
    t = np.arange(DEC_SEQ)
    pos = np.stack([t // GRID_W, t % GRID_W], axis=1).astype(np.float32)
    nf = HEAD_DIM // 4
    inv = jnp.asarray(ROPE_THETA, F32) ** (-jnp.arange(nf, dtype=F32) / nf)
    d = np.arange(LANES) % HEAD_DIM
    which = d // (HEAD_DIM // 2)
    ang = jnp.asarray(pos)[:, which] * inv[d % nf][None, :]
    sign = np.where((d % 32) < 16, -1.0, 1.0).astype(np.float32)
    cos = jnp.concatenate([jnp.cos(ang), jnp.ones((TT, LANES), F32)], axis=0)
    sin = jnp.concatenate([jnp.sin(ang) * sign[None, :], jnp.zeros((TT, LANES), F32)], axis=0)
    return cos, sin


def _block_ones():
    h = np.arange(QB_W) // HEAD_DIM
    return jnp.asarray((h[:, None] == h[None, :]).astype(np.float32), dtype=BF16)


def _pool_block_diag(pool_w):
    out = jnp.zeros((POOL_WIDTH, POOL_WIDTH), F32)
    for g in range(4):
        out = out.at[64 * g:64 * g + 64, 64 * g:64 * g + 64].set(pool_w[g])
    return out


def kernel(x_prompt, x_sample, cache_b_k, cache_b_v, cache_c_k, cache_c_v, c, c_ctx, w_ada, b_ada, w_in, w_out,
           pool_w, pool_scale, q_norm, k_norm, rpb, ln1_g, ln1_b, ln2_g, ln2_b, router_g, router_g_b, router_e,
           router_e_b, w_gate, w_up, w_down):
    x_ctx = x_prompt.reshape(T_CTX, D_MODEL)
    x_lat = x_sample.reshape(T_LAT, D_MODEL)
    c8 = jnp.concatenate([c, c_ctx[None], jnp.zeros((MOD_ROWS - DEC_BATCH - 1, D_MODEL), F32)], axis=0)
    mod = _adaln(c8, w_ada, b_ada).reshape(DEPTH * MOD_ROWS * 6, 1, D_MODEL)

    cos_t, sin_t = _rope_tables()
    ones_bd = _block_ones()
    cbk = cache_b_k.reshape(DEC_BATCH, DEPTH, PAST_LEN, KB_W)
    cbv = cache_b_v.reshape(DEC_BATCH, DEPTH, PAST_LEN, KB_W)
    cck = cache_c_k.reshape(DEC_BATCH, DEPTH, PAST_LEN, C_W)
    ccv = cache_c_v.reshape(DEC_BATCH, DEPTH, PAST_LEN, C_W)
    pad = jnp.zeros((DEPTH, D_MODEL, ROUTER_LANES - N_GROUPS - N_EXPERTS), F32)
    wr = jnp.concatenate([router_g, router_e, pad], axis=2)
    br = jnp.concatenate([router_g_b, router_e_b, pad[:, 0]], axis=1)[:, None]
    per_layer = lambda p: p[:, None]

    caches = ()
    for l in range(DEPTH):
        qn = jnp.tile(q_norm[l], H_B)[None]
        kn = jnp.tile(k_norm[l], KV_B)[None]
        a, qb, kb, vb, qc, kc, vc = _proj(l, x_ctx, x_lat, mod, w_in, ones_bd, qn, kn, cos_t, sin_t)
        ya = _pool(a, _pool_block_diag(pool_w[l]), pool_scale[l][None])
        yb_ctx, yc_ctx, *caches = _attn_ctx(l, qb, kb, vb, qc, kc, vc, caches)
        yb_lat = _attn_latb(l, qb, kb, vb, cbk, cbv)
        yc_lat = _natten(l, qc, kc, vc, cck, ccv, _natten_blocks(rpb[l]))
        x1, h2g, ids, wts = _tail(l, x_ctx, x_lat, ya, yb_ctx, yb_lat, yc_ctx, yc_lat, w_out, mod,
                                  per_layer(ln1_g), per_layer(ln1_b), wr, br)
        pairs = lambda a: a.reshape(N_STREAMS, T_STREAM // TT, 8, TT)[:, :, :2, :].transpose(0, 2, 1, 3)
        pos, tiles = _plan(pairs(ids).reshape(N_STREAMS, 2 * T_STREAM // LANES, LANES))
        gates = jnp.pad(pairs(wts).reshape(N_STREAMS, 2, T_STREAM), ((0, 0), (0, 0), (0, SLOT_PITCH - T_STREAM)))
        y = _moe(l, pos.reshape(-1), gates.reshape(-1), tiles.reshape(-1), h2g, w_gate, w_up, w_down)
        y = y.reshape(T_ALL * GROUP_ROWS, LANES)
        x_ctx = _final(l, 0, x1, y, mod, per_layer(ln2_g), per_layer(ln2_b))
        x_lat = _final(l, 1, x1, y, mod, per_layer(ln2_g), per_layer(ln2_b))

    y_prompt = x_ctx.reshape(BATCH, SEQ, D_MODEL)
    y_sample = x_lat.reshape(DEC_BATCH, DEC_SEQ, D_MODEL)
    heads_last = lambda t, h: t.reshape(BATCH, DEPTH, h, HEAD_DIM, SEQ).transpose(0, 1, 4, 2, 3)
    new_bk, new_bv, new_ck, new_cv = caches
    return (y_prompt, y_sample, heads_last(new_bk, KV_B), heads_last(new_bv, KV_B),
            heads_last(new_ck, H_C), heads_last(new_cv, H_C))
```

```python
import functools

import numpy as np
import jax
import jax.numpy as jnp
from jax import lax
from jax.experimental import pallas as pl
from jax.experimental.pallas import tpu as pltpu

F32 = jnp.float32
BF16 = jnp.bfloat16

D_MODEL = 1024
BATCH = 16
SEQ = 256
DEPTH = 2
DEC_BATCH = 4
DEC_SEQ = 1024
PAST_LEN = 256
GRID_W = 64
GRID_ROWS = DEC_SEQ // GRID_W
HEAD_DIM = 64
POOL_WIDTH = 256
POOL_WINDOWS = (2, 4, 8, 16)
H_B = 6
KV_B = 2
H_C = 6
WIN_R = 8
WIN_C = 16
ROPE_THETA = 10000.0
QB_W = H_B * HEAD_DIM
KB_W = KV_B * HEAD_DIM
C_W = H_C * HEAD_DIM
PROJ_WIDTH = 2048
N_GROUPS = 4
EXPERTS_PER_GROUP = 8
N_EXPERTS = 32
D_EXPERT = 256
ALPHA = (2 * DEPTH) ** 0.25
LN_EPS = 1e-6
RMS_EPS = 1e-6
NEG = -1e30
ATTN_SCALE = HEAD_DIM ** -0.5

T_CTX = BATCH * SEQ
T_LAT = DEC_BATCH * DEC_SEQ
T_ALL = T_CTX + T_LAT

LANES = 128
ROUTER_LANES = 128
EXPERT_LANE0 = N_GROUPS
MOD_ROWS = 8
CTX_MOD_ROW = DEC_BATCH

TT = 512
TP = 1024
POOL_GAP = 16
N_STREAMS = 2
T_STREAM = T_ALL // N_STREAMS
TMX = 128
PLACE_UNROLL = 8
PAD_CHUNK = 8
SLOT_PITCH = 2 * T_STREAM
P_MAX = 2 * T_STREAM + N_EXPERTS * TMX
NT_MAX = P_MAX // TMX
GROUP_ROWS = D_MODEL // LANES
XSTRIDE = TMX + 8
RMW_BATCH = 16
PLAN_N_TILES, PLAN_VALID_END, PLAN_PAD_END = 0, 1, 2
HALF = DEC_SEQ // 2
NAT_KEYS = 12 * GRID_W
NAT_QROWS = 4
VMEM_LIMIT = 56 * 1024 * 1024


def _dot(a, b):
    return jnp.dot(a, b, preferred_element_type=F32)


def _dot_nt(a, b):
    return lax.dot_general(a, b, (((1,), (1,)), ((), ())), preferred_element_type=F32)


def _split_bf16(x):
    hi = x.astype(BF16)
    lo = (x - hi.astype(F32)).astype(BF16)
    return hi, lo


def _layer_norm(y, g, b):
    mu = jnp.mean(y, axis=-1, keepdims=True)
    var = jnp.mean(jnp.square(y - mu), axis=-1, keepdims=True)
    return (y - mu) * lax.rsqrt(var + LN_EPS) * g + b


def _adaln_kernel(c_ref, w_ref, b_ref, o_ref):
    c = c_ref[...]
    s = (c * jax.nn.sigmoid(c)).astype(BF16)
    o_ref[0] = _dot(s, w_ref[0].astype(BF16)) + b_ref[0]


def _adaln(c8, w_ada, b_ada):
    tn = 1536
    n = w_ada.shape[-1]
    return pl.pallas_call(
        _adaln_kernel,
        grid=(DEPTH, n // tn),
        in_specs=[
            pl.BlockSpec((MOD_ROWS, D_MODEL), lambda l, j: (0, 0)),
            pl.BlockSpec((1, D_MODEL, tn), lambda l, j: (l, 0, j)),
            pl.BlockSpec((1, 1, tn), lambda l, j: (l, 0, j)),
        ],
        out_specs=pl.BlockSpec((1, MOD_ROWS, tn), lambda l, j: (l, 0, j)),
        out_shape=jax.ShapeDtypeStruct((DEPTH, MOD_ROWS, n), F32),
        compiler_params=pltpu.CompilerParams(
            dimension_semantics=("arbitrary", "arbitrary"), vmem_limit_bytes=VMEM_LIMIT),
        name="adaln",
    )(c8, w_ada, b_ada.reshape(DEPTH, 1, n))


def _mod_row_of_tile(i, tile):
    n_ctx = T_CTX // tile
    per_req = DEC_SEQ // tile
    return jnp.where(i < n_ctx, CTX_MOD_ROW, (i - n_ctx) // per_req)


def _mod_spec(layer, chunk, tile):
    return pl.BlockSpec(
        (1, 1, D_MODEL), lambda i: ((layer * MOD_ROWS + _mod_row_of_tile(i, tile)) * 6 + chunk, 0, 0))


def _layer_spec(layer, shape):
    return pl.BlockSpec((1,) + shape, lambda *_: (layer,) + (0,) * len(shape))


N_CTX_TILES = T_CTX // TT


def _split_row_specs(width):
    return (pl.BlockSpec((TT, width), lambda i: (jnp.minimum(i, N_CTX_TILES - 1), 0)),
            pl.BlockSpec((TT, width), lambda i: (jnp.maximum(i - N_CTX_TILES, 0), 0)))


def _split_rows(ctx_ref, lat_ref):
    return jnp.where(pl.program_id(0) < N_CTX_TILES, ctx_ref[...], lat_ref[...])


def _rms_norm_heads(x, ones_bd, w):
    hi, lo = _split_bf16(x * x)
    ssq = _dot(hi, ones_bd) + _dot(lo, ones_bd)
    return x * lax.rsqrt(ssq * (1.0 / HEAD_DIM) + RMS_EPS) * w


def _rope_cols(x, cos, sin, first16):
    cols = []
    for j in range(x.shape[1] // LANES):
        xc = x[:, j * LANES:(j + 1) * LANES]
        partner = jnp.where(first16, pltpu.roll(xc, LANES - 16, axis=1), pltpu.roll(xc, 16, axis=1))
        cols.append(xc * cos + partner * sin)
    return jnp.concatenate(cols, axis=1) if len(cols) > 1 else cols[0]


def _proj_kernel(xc_ref, xl_ref, sh_ref, sc_ref, w_ref, ones_ref, qn_ref, kn_ref, cos_ref, sin_ref,
                 a_ref, qb_ref, kb_ref, vb_ref, qc_ref, kc_ref, vc_ref, wbf_ref):
    @pl.when(pl.program_id(0) == 0)
    def _():
        wbf_ref[...] = w_ref[0].astype(BF16)

    h = _split_rows(xc_ref, xl_ref) * (1.0 + sc_ref[0]) + sh_ref[0]
    p = _dot(h.astype(BF16), wbf_ref[...])
    o = 0
    a_ref[...] = p[:, o:o + POOL_WIDTH]; o += POOL_WIDTH
    qb = p[:, o:o + QB_W]; o += QB_W
    kb = p[:, o:o + KB_W]; o += KB_W
    vb_ref[...] = p[:, o:o + KB_W]; o += KB_W
    qc_ref[...] = (p[:, o:o + C_W] * ATTN_SCALE).astype(BF16); o += C_W
    kc_ref[...] = p[:, o:o + C_W]; o += C_W
    vc_ref[...] = p[:, o:o + C_W]

    ones_bd = ones_ref[...]
    cos = cos_ref[...]
    sin = sin_ref[...]
    lane = lax.broadcasted_iota(jnp.int32, (1, LANES), 1)
    first16 = (lane & 31) < 16
    qb = _rope_cols(_rms_norm_heads(qb, ones_bd, qn_ref[...]), cos, sin, first16)
    kb = _rope_cols(_rms_norm_heads(kb, ones_bd[:KB_W, :KB_W], kn_ref[...]), cos, sin, first16)
    qb_ref[...] = (qb * ATTN_SCALE).astype(BF16)
    kb_ref[...] = kb


def _proj(layer, x_ctx, x_lat, mod, w_in, ones_bd, qn, kn, cos_t, sin_t):
    n_tiles = T_ALL // TT
    n_ctx = N_CTX_TILES
    per_req = DEC_SEQ // TT

    def rope_idx(i):
        return (jnp.where(i < n_ctx, per_req, (i - n_ctx) % per_req), 0)

    row = lambda w: pl.BlockSpec((TT, w), lambda i: (i, 0))
    const = lambda s: pl.BlockSpec(s, lambda i: (0,) * len(s))
    widths = (POOL_WIDTH, QB_W, KB_W, KB_W, C_W, C_W, C_W)
    dtypes = (F32, BF16, F32, F32, BF16, F32, F32)
    return pl.pallas_call(
        _proj_kernel,
        grid=(n_tiles,),
        in_specs=[*_split_row_specs(D_MODEL), _mod_spec(layer, 0, TT), _mod_spec(layer, 1, TT),
                  _layer_spec(layer, (D_MODEL, PROJ_WIDTH)),
                  const((QB_W, QB_W)), const((1, QB_W)), const((1, KB_W)),
                  pl.BlockSpec((TT, LANES), rope_idx), pl.BlockSpec((TT, LANES), rope_idx)],
        out_specs=[row(w) for w in widths],
        out_shape=[jax.ShapeDtypeStruct((T_ALL, w), d) for w, d in zip(widths, dtypes)],
        scratch_shapes=[pltpu.VMEM((D_MODEL, PROJ_WIDTH), BF16)],
        compiler_params=pltpu.CompilerParams(dimension_semantics=("arbitrary",), vmem_limit_bytes=VMEM_LIMIT),
        name="proj",
    )(x_ctx, x_lat, mod, mod, w_in, ones_bd, qn, kn, cos_t, sin_t)


def _pool_windows(p):
    n = p.shape[0]
    sh = lambda x, k: pltpu.roll(x, (n - k) % n, axis=0)
    w2 = p + sh(p, -1)
    w4 = sh(w2, -1) + sh(w2, 1)
    w8 = sh(w4, -2) + sh(w4, 2)
    w16 = sh(w8, -4) + sh(w8, 4)
    g = lax.broadcasted_iota(jnp.int32, p.shape, 1) >> 6
    return jnp.where(g == 0, w2, jnp.where(g == 1, w4, jnp.where(g == 2, w8, w16)))


def _pool_kernel(a_ref, w_ref, scale_ref, o_ref, pad_ref):
    a = a_ref[...]

    def mix(csum, seq_len):
        row = lax.broadcasted_iota(jnp.int32, a.shape, 0)
        g = lax.broadcasted_iota(jnp.int32, a.shape, 1) >> 6
        t = row & (seq_len - 1)
        half = jnp.where(g == 0, 1, jnp.where(g == 1, 2, jnp.where(g == 2, 4, 8)))
        cnt = jnp.minimum(t + half, seq_len) - jnp.maximum(t - half, 0)
        pooled = csum / cnt.astype(F32) - a
        y = _dot(pooled.astype(BF16), w_ref[...].astype(BF16)) * scale_ref[...]
        o_ref[...] = y.astype(BF16)

    def padded_sums(seq_len):
        pitch = seq_len + POOL_GAP
        n_req = TP // seq_len
        n = n_req * pitch
        pad_ref[...] = jnp.zeros_like(pad_ref)
        for r in range(n_req):
            pad_ref[r * pitch + POOL_GAP // 2:r * pitch + POOL_GAP // 2 + seq_len, :] = a[r * seq_len:(r + 1) * seq_len]
        sums = _pool_windows(pad_ref[0:n, :])
        return jnp.concatenate(
            [sums[r * pitch + POOL_GAP // 2:r * pitch + POOL_GAP // 2 + seq_len] for r in range(n_req)], axis=0)

    is_ctx = pl.program_id(0) < T_CTX // TP

    @pl.when(is_ctx)
    def _():
        mix(padded_sums(SEQ), SEQ)

    @pl.when(jnp.logical_not(is_ctx))
    def _():
        mix(padded_sums(DEC_SEQ), DEC_SEQ)


def _pool(a, w_bd, scale):
    return pl.pallas_call(
        _pool_kernel,
        grid=(T_ALL // TP,),
        in_specs=[pl.BlockSpec((TP, POOL_WIDTH), lambda i: (i, 0)),
                  pl.BlockSpec((POOL_WIDTH, POOL_WIDTH), lambda i: (0, 0)),
                  pl.BlockSpec((1, POOL_WIDTH), lambda i: (0, 0))],
        out_specs=pl.BlockSpec((TP, POOL_WIDTH), lambda i: (i, 0)),
        out_shape=jax.ShapeDtypeStruct((T_ALL, POOL_WIDTH), BF16),
        scratch_shapes=[pltpu.VMEM(((TP // SEQ) * (SEQ + POOL_GAP), POOL_WIDTH), F32)],
        compiler_params=pltpu.CompilerParams(dimension_semantics=("arbitrary",), vmem_limit_bytes=VMEM_LIMIT),
        name="pool",
    )(a, w_bd, scale)


def _softmax_pv(scores, values):
    m = scores[0].max(axis=-1, keepdims=True)
    for s in scores[1:]:
        m = jnp.maximum(m, s.max(axis=-1, keepdims=True))
    ps = [jnp.exp(s - m) for s in scores]
    l = ps[0].sum(axis=-1, keepdims=True)
    for p in ps[1:]:
        l = l + p.sum(axis=-1, keepdims=True)
    o = _dot(ps[0].astype(BF16), values[0])
    for p, v in zip(ps[1:], values[1:]):
        o = o + _dot(p.astype(BF16), v)
    return o * (1.0 / l)


def _lane_halves():
    lane = lax.broadcasted_iota(jnp.int32, (1, LANES), 1)
    return lane < HEAD_DIM, lane >= HEAD_DIM


def _keep(x, mask):
    return jnp.where(mask, x, 0.0).astype(BF16)


def _gqa_variants(x):
    lo, hi = _lane_halves()
    xs = pltpu.roll(x, HEAD_DIM, axis=1)
    nat_lo, nat_hi = _keep(x, lo), _keep(x, hi)
    sw_lo, sw_hi = _keep(xs, lo), _keep(xs, hi)
    return ((nat_lo, sw_hi), (nat_lo, nat_hi), (sw_lo, nat_hi))


def _mha_variants(x):
    lo, hi = _lane_halves()
    out = []
    for j in range(x.shape[1] // LANES):
        xc = x[:, j * LANES:(j + 1) * LANES]
        out.append((_keep(xc, lo), _keep(xc, hi)))
    return tuple(out)


def _attend_cols(q, k_vars, v_vars, extra_k=None, extra_v=None, bias=None):
    cols = []
    for j in range(q.shape[1] // LANES):
        qc = q[:, j * LANES:(j + 1) * LANES]
        o = None
        for hh in range(2):
            s = _dot_nt(qc, k_vars[j][hh])
            if bias is not None:
                s = s + bias[j][hh]
            scores, values = [s], [v_vars[j][hh]]
            if extra_k is not None:
                scores.append(_dot_nt(qc, extra_k[j][hh]))
                values.append(extra_v[j][hh])
            oh = _softmax_pv(scores, values)
            o = oh if o is None else o + oh
        cols.append(o)
    return jnp.concatenate(cols, axis=1)


def _attn_ctx_kernel(layer_slot, qb_ref, kb_ref, vb_ref, qc_ref, kc_ref, vc_ref, *refs):
    yb_ref, yc_ref, kbt_ref, vbt_ref, kct_ref, vct_ref = refs[-6:]
    kb, vb, kc, vc = kb_ref[...], vb_ref[...], kc_ref[...], vc_ref[...]
    yb = _attend_cols(qb_ref[...], _gqa_variants(kb), _gqa_variants(vb))
    yb_ref[...] = yb.astype(BF16)
    yc = _attend_cols(qc_ref[...], _mha_variants(kc), _mha_variants(vc))
    yc_ref[...] = yc.astype(BF16)
    for ref, val in ((kbt_ref, kb), (vbt_ref, vb), (kct_ref, kc), (vct_ref, vc)):
        for slot in range(ref.shape[1]):
            ref[0, slot] = val.T if slot == layer_slot else jnp.zeros(ref.shape[2:], F32)


CACHE_WIDTHS = (KB_W, KB_W, C_W, C_W)


def _attn_ctx(layer, qb, kb, vb, qc, kc, vc, caches):
    row = lambda w: pl.BlockSpec((SEQ, w), lambda i: (i, 0))
    if caches:
        cache = lambda w: pl.BlockSpec((1, 1, w, SEQ), lambda i: (i, layer, 0, 0))
        layer_slot = 0
    else:
        cache = lambda w: pl.BlockSpec((1, DEPTH, w, SEQ), lambda i: (i, 0, 0, 0))
        layer_slot = layer
    return pl.pallas_call(
        functools.partial(_attn_ctx_kernel, layer_slot),
        grid=(BATCH,),
        in_specs=[row(QB_W), row(KB_W), row(KB_W), row(C_W), row(C_W), row(C_W)]
                 + [pl.BlockSpec(memory_space=pl.ANY)] * len(caches),
        out_specs=[row(QB_W), row(C_W)] + [cache(w) for w in CACHE_WIDTHS],
        out_shape=[jax.ShapeDtypeStruct((T_CTX, QB_W), BF16), jax.ShapeDtypeStruct((T_CTX, C_W), BF16)]
                  + [jax.ShapeDtypeStruct((BATCH, DEPTH, w, SEQ), F32) for w in CACHE_WIDTHS],
        input_output_aliases={6 + k: 2 + k for k in range(len(caches))},
        compiler_params=pltpu.CompilerParams(dimension_semantics=("arbitrary",), vmem_limit_bytes=VMEM_LIMIT),
        name="attn_ctx",
    )(qb, kb, vb, qc, kc, vc, *caches)


def _attn_latb_kernel(q_ref, k_ref, v_ref, ck_ref, cv_ref, y_ref):
    y = _attend_cols(q_ref[...], _gqa_variants(ck_ref[0, 0]), _gqa_variants(cv_ref[0, 0]),
                     extra_k=_gqa_variants(k_ref[...]), extra_v=_gqa_variants(v_ref[...]))
    y_ref[...] = y.astype(BF16)


def _attn_latb(layer, qb, kb, vb, cache_k, cache_v):
    ctx_h = T_CTX // HALF
    ctx_r = T_CTX // DEC_SEQ
    cache = pl.BlockSpec((1, 1, PAST_LEN, KB_W), lambda b, s: (b, layer, 0, 0))
    own = pl.BlockSpec((DEC_SEQ, KB_W), lambda b, s: (ctx_r + b, 0))
    return pl.pallas_call(
        _attn_latb_kernel,
        grid=(DEC_BATCH, DEC_SEQ // HALF),
        in_specs=[pl.BlockSpec((HALF, QB_W), lambda b, s: (ctx_h + 2 * b + s, 0)), own, own, cache, cache],
        out_specs=pl.BlockSpec((HALF, QB_W), lambda b, s: (2 * b + s, 0)),
        out_shape=jax.ShapeDtypeStruct((T_LAT, QB_W), BF16),
        compiler_params=pltpu.CompilerParams(
            dimension_semantics=("arbitrary", "arbitrary"), vmem_limit_bytes=VMEM_LIMIT),
        name="attn_latb",
    )(qb, kb, vb, cache_k, cache_v)


def _natten_window(s, i):
    r = (HALF // GRID_W) * s + i
    rs = min(max(r - WIN_R // 2, 0), GRID_ROWS - WIN_R)
    return r, rs


def _natten_kernel(q_ref, k_ref, v_ref, ck_ref, cv_ref, blk_ref, y_ref, bias_ref):
    s = pl.program_id(1)

    for sv in range(2):
        @pl.when((pl.program_id(2) == 0) & (s == sv))
        def _():
            lo_half, _ = _lane_halves()
            masked = jnp.full((GRID_W, LANES), NEG, F32)
            for hh in range(2):
                for i in range(HALF // GRID_W):
                    r, rs = _natten_window(sv, i)
                    for jp in range(NAT_KEYS // LANES):
                        pair = []
                        for j in (2 * jp, 2 * jp + 1):
                            rk = 4 * sv + j
                            pair.append(blk_ref[0, hh, rk - r + WIN_R - 1] if rs <= rk < rs + WIN_R else masked)
                        bias_ref[hh, i * GRID_W:(i + 1) * GRID_W, jp * LANES:(jp + 1) * LANES] = (
                            jnp.where(lo_half, pair[0], pair[1]))

    start = pl.multiple_of(s * (4 * GRID_W), 4 * GRID_W)
    k = _mha_variants(k_ref[pl.ds(start, NAT_KEYS), :])
    v = _mha_variants(v_ref[pl.ds(start, NAT_KEYS), :])
    ck = _mha_variants(ck_ref[0, 0])
    cv = _mha_variants(cv_ref[0, 0])

    for sv in range(2):
        @pl.when(s == sv)
        def _():
            for i_lo in range(0, HALF // GRID_W, NAT_QROWS):
                firsts = [_natten_window(sv, i)[1] - 4 * sv for i in range(i_lo, i_lo + NAT_QROWS)]
                k_lo = (min(firsts) // 2) * 2 * GRID_W
                k_hi = -((-(max(firsts) + WIN_R)) // 2) * 2 * GRID_W
                qs = slice(i_lo * GRID_W, (i_lo + NAT_QROWS) * GRID_W)
                cut = lambda pairs: tuple(tuple(x[k_lo:k_hi] for x in p) for p in pairs)
                bias = ((bias_ref[0, qs, k_lo:k_hi], bias_ref[1, qs, k_lo:k_hi]),)
                y = _attend_cols(q_ref[qs, :], cut(k), cut(v), extra_k=ck, extra_v=cv, bias=bias)
                y_ref[qs, :] = y.astype(BF16)


def _natten(layer, qc, kc, vc, cache_k, cache_v, blocks):
    ctx_h = T_CTX // HALF
    ctx_r = T_CTX // DEC_SEQ
    cache = pl.BlockSpec((1, 1, PAST_LEN, LANES), lambda j, s, b: (b, layer, 0, j))
    own = pl.BlockSpec((DEC_SEQ, LANES), lambda j, s, b: (ctx_r + b, j))
    return pl.pallas_call(
        _natten_kernel,
        grid=(C_W // LANES, DEC_SEQ // HALF, DEC_BATCH),
        in_specs=[pl.BlockSpec((HALF, LANES), lambda j, s, b: (ctx_h + 2 * b + s, j)), own, own, cache, cache,
                  pl.BlockSpec((1, 2, 2 * WIN_R - 1, GRID_W, LANES), lambda j, s, b: (j, 0, 0, 0, 0))],
        out_specs=pl.BlockSpec((HALF, LANES), lambda j, s, b: (2 * b + s, j)),
        out_shape=jax.ShapeDtypeStruct((T_LAT, C_W), BF16),
        scratch_shapes=[pltpu.VMEM((2, HALF, NAT_KEYS), F32)],
        compiler_params=pltpu.CompilerParams(
            dimension_semantics=("arbitrary", "arbitrary", "arbitrary"), vmem_limit_bytes=VMEM_LIMIT),
        name="natten",
    )(qc, kc, vc, cache_k, cache_v, blocks)


def _natten_blocks(rpb):
    qcol = np.arange(GRID_W)[:, None]
    kcol = np.arange(GRID_W)[None, :]
    cs = np.clip(qcol - WIN_C // 2, 0, GRID_W - WIN_C)
    col_ok = (kcol >= cs) & (kcol < cs + WIN_C)
    dc = np.clip(kcol - qcol, -(WIN_C - 1), WIN_C - 1) + (WIN_C - 1)
    sel_c = (dc[:, :, None] == np.arange(2 * WIN_C - 1)).astype(np.float32)
    blk = jnp.einsum("hdm,qkm->hdqk", rpb, jnp.asarray(sel_c), precision=lax.Precision.HIGHEST)
    blk = jnp.where(jnp.asarray(col_ok), blk, NEG)
    blk = jnp.concatenate([blk, blk], axis=-1)
    return blk.reshape(H_C // 2, 2, 2 * WIN_R - 1, GRID_W, LANES)


def _route(lg):
    lane = lax.broadcasted_iota(jnp.int32, lg.shape, 1).astype(F32)
    low = jnp.float32(-3.0e38)
    far = jnp.float32(ROUTER_LANES)
    is_g = lane < N_GROUPS
    gmax = jnp.where(is_g, lg, low).max(axis=-1, keepdims=True)
    gsel = jnp.where(is_g & (lg == gmax), lane, far).min(axis=-1, keepdims=True)
    pg_sel = 1.0 / jnp.where(is_g, jnp.exp(lg - gmax), 0.0).sum(axis=-1, keepdims=True)
    e0 = EXPERT_LANE0 + EXPERTS_PER_GROUP * gsel
    in_g = (lane >= e0) & (lane < e0 + EXPERTS_PER_GROUP)
    m1 = jnp.where(in_g, lg, low).max(axis=-1, keepdims=True)
    i1 = jnp.where(in_g & (lg == m1), lane, far).min(axis=-1, keepdims=True)
    rest = in_g & (lane != i1)
    m2 = jnp.where(rest, lg, low).max(axis=-1, keepdims=True)
    i2 = jnp.where(rest & (lg == m2), lane, far).min(axis=-1, keepdims=True)
    t = jnp.exp(m2 - m1)
    ssum = 1.0 + t
    w1 = (1.0 / ssum) * pg_sel
    w2 = (t / ssum) * pg_sel
    return i1 - EXPERT_LANE0, i2 - EXPERT_LANE0, w1, w2


def _rows_to_groups(x, stage_ref, out_ref, n_rows):
    stride = n_rows + 8
    for c in range(GROUP_ROWS):
        stage_ref[c * stride:c * stride + n_rows, :] = x[:, c * LANES:(c + 1) * LANES]
    for j in range(n_rows):
        out_ref[GROUP_ROWS * j:GROUP_ROWS * (j + 1), :] = stage_ref[pl.ds(j, GROUP_ROWS, stride=stride), :]


def _tail_kernel(xc_ref, xl_ref, ya_ref, ybc_ref, ybl_ref, ycc_ref, ycl_ref, w_ref, g1_ref, sh2_ref, sc2_ref,
                 lg_ref, lb_ref, wr_ref, br_ref, x1_ref, h2g_ref, ids_ref, wts_ref, wbf_ref, stage_ref):
    @pl.when(pl.program_id(0) == 0)
    def _():
        wbf_ref[...] = w_ref[0].astype(BF16)

    o = (_dot(ya_ref[...], wbf_ref[0:POOL_WIDTH, :])
         + _dot(_split_rows(ybc_ref, ybl_ref), wbf_ref[POOL_WIDTH:POOL_WIDTH + QB_W, :])
         + _dot(_split_rows(ycc_ref, ycl_ref), wbf_ref[POOL_WIDTH + QB_W:, :]))
    x1 = _layer_norm(ALPHA * _split_rows(xc_ref, xl_ref) + g1_ref[0] * o, lg_ref[0], lb_ref[0])
    x1_ref[...] = x1
    h2 = x1 * (1.0 + sc2_ref[0]) + sh2_ref[0]
    hh, hl = _split_bf16(h2)
    wh, wl = _split_bf16(wr_ref[0])
    lg = _dot(hh, wh) + _dot(hl, wh) + _dot(hh, wl) + br_ref[0]
    i1, i2, w1, w2 = _route(lg)
    lane = lax.broadcasted_iota(jnp.int32, lg.shape, 1)
    ids = jnp.where(lane == 0, i1, jnp.where(lane == 1, i2, 0.0))
    ids_ref[0] = ids.T[0:8, :].astype(jnp.int32)
    wts = jnp.where(lane == 0, w1, jnp.where(lane == 1, w2, 0.0))
    wts_ref[0] = wts.T[0:8, :]
    _rows_to_groups(h2, stage_ref, h2g_ref, TT)


def _tail(layer, x_ctx, x_lat, ya, yb_ctx, yb_lat, yc_ctx, yc_lat, w_out, mod, ln_g, ln_b, wr, br):
    row = lambda w: pl.BlockSpec((TT, w), lambda i: (i, 0))
    return pl.pallas_call(
        _tail_kernel,
        grid=(T_ALL // TT,),
        in_specs=[*_split_row_specs(D_MODEL), row(POOL_WIDTH), *_split_row_specs(QB_W), *_split_row_specs(C_W),
                  _layer_spec(layer, (D_MODEL, D_MODEL)),
                  _mod_spec(layer, 2, TT), _mod_spec(layer, 3, TT), _mod_spec(layer, 4, TT),
                  _layer_spec(layer, (1, D_MODEL)), _layer_spec(layer, (1, D_MODEL)),
                  _layer_spec(layer, (D_MODEL, ROUTER_LANES)), _layer_spec(layer, (1, ROUTER_LANES))],
        out_specs=[row(D_MODEL), pl.BlockSpec((TT * GROUP_ROWS, LANES), lambda i: (i, 0)),
                   pl.BlockSpec((1, 8, TT), lambda i: (i, 0, 0)), pl.BlockSpec((1, 8, TT), lambda i: (i, 0, 0))],
        out_shape=[jax.ShapeDtypeStruct((T_ALL, D_MODEL), F32),
                   jax.ShapeDtypeStruct((T_ALL * GROUP_ROWS, LANES), F32),
                   jax.ShapeDtypeStruct((T_ALL // TT, 8, TT), jnp.int32),
                   jax.ShapeDtypeStruct((T_ALL // TT, 8, TT), F32)],
        scratch_shapes=[pltpu.VMEM((D_MODEL, D_MODEL), BF16),
                        pltpu.VMEM((GROUP_ROWS * (TT + 8), LANES), F32)],
        compiler_params=pltpu.CompilerParams(dimension_semantics=("arbitrary",), vmem_limit_bytes=VMEM_LIMIT),
        name="tail",
    )(x_ctx, x_lat, ya, yb_ctx, yb_lat, yc_ctx, yc_lat, w_out, mod, mod, mod, ln_g, ln_b, wr, br)


def _plan_kernel(eid_ref, pos_ref, tile_ref):
    e = eid_ref[0]
    rows = e.shape[0]
    r = lax.broadcasted_iota(jnp.int32, (LANES, LANES), 0)
    c = lax.broadcasted_iota(jnp.int32, (LANES, LANES), 1)
    upper = (r <= c).astype(BF16)
    ones = jnp.ones((LANES, LANES), BF16)
    lower = (lax.broadcasted_iota(jnp.int32, (rows, rows), 1)
             < lax.broadcasted_iota(jnp.int32, (rows, rows), 0)).astype(BF16)
    lane = lax.broadcasted_iota(jnp.int32, (1, LANES), 1)
    masks = [e == ex for ex in range(N_EXPERTS)]
    stack = jnp.concatenate([m.astype(BF16) for m in masks], axis=0)
    incl_all = _dot(stack, upper)
    tot_all = _dot(stack, ones)
    pos = jnp.zeros(e.shape, F32)
    base = jnp.zeros((1, LANES), F32)
    valid_end = jnp.zeros((1, LANES), F32)
    pad_end = jnp.zeros((1, LANES), F32)
    for ex, m in enumerate(masks):
        incl = incl_all[ex * rows:(ex + 1) * rows]
        row_tot = tot_all[ex * rows:(ex + 1) * rows]
        row_off = _dot(lower, row_tot.astype(BF16))
        cnt = jnp.sum(row_tot, axis=0, keepdims=True)
        pos = jnp.where(m, base + row_off + incl - 1.0, pos)
        valid_end = jnp.where(lane == ex, base + cnt, valid_end)
        base = base + jnp.ceil(cnt * (1.0 / TMX)) * TMX
        pad_end = jnp.where(lane == ex, base, pad_end)
    pos_ref[0] = pos.astype(jnp.int32)
    sub = lax.broadcasted_iota(jnp.int32, (8, LANES), 0)
    rows8 = lambda v: jnp.broadcast_to(v, (8, LANES))
    table = jnp.where(sub == PLAN_N_TILES, rows8(base * (1.0 / TMX)),
                      jnp.where(sub == PLAN_VALID_END, rows8(valid_end),
                                jnp.where(sub == PLAN_PAD_END, rows8(pad_end), 0.0)))
    tile_ref[0] = table.astype(jnp.int32)


def _plan(eid):
    rows = 2 * T_STREAM // LANES
    return pl.pallas_call(
        _plan_kernel,
        grid=(N_STREAMS,),
        in_specs=[pl.BlockSpec((1, rows, LANES), lambda s: (s, 0, 0))],
        out_specs=[pl.BlockSpec((1, rows, LANES), lambda s: (s, 0, 0)), pl.BlockSpec((1, 8, LANES), lambda s: (s, 0, 0))],
        out_shape=[jax.ShapeDtypeStruct((N_STREAMS, rows, LANES), jnp.int32),
                   jax.ShapeDtypeStruct((N_STREAMS, 8, LANES), jnp.int32)],
        compiler_params=pltpu.CompilerParams(dimension_semantics=("arbitrary",), vmem_limit_bytes=VMEM_LIMIT),
        name="plan",
    )(eid)


def _moe_kernel(pos_ref, wpair_ref, plan_ref, h2g_hbm, wg_ref, wu_ref, wd_ref, y_hbm,
                h2v, yv, off, xg0, xg1, yg0, yg1, wgb, wub, wdb, sem):
    s = pl.program_id(0)
    e = pl.program_id(1)
    n_pairs = 2 * T_STREAM
    spare_off = T_STREAM * GROUP_ROWS
    plan = lambda row, idx: plan_ref[(s * 8 + row) * LANES + idx]
    n_tiles = plan(PLAN_N_TILES, 0)
    last = n_tiles - 1

    xgs = (xg0, xg1)
    ygs = (yg0, yg1)

    def row_offset(pair):
        return pl.multiple_of((pair & (SLOT_PITCH - 1)) << 3, GROUP_ROWS)

    def gather(tile, dst):
        base = tile * TMX
        for j in range(TMX):
            dst[pl.ds(j, GROUP_ROWS, stride=XSTRIDE), :] = h2v[pl.ds(row_offset(off[base + j]), GROUP_ROWS), :]

    def h2_copy():
        return pltpu.make_async_copy(
            h2g_hbm.at[pl.ds(pl.multiple_of(s * (T_STREAM * GROUP_ROWS), 8), T_STREAM * GROUP_ROWS)],
            h2v.at[pl.ds(0, T_STREAM * GROUP_ROWS)], sem.at[0])

    def y_copy(stream):
        return pltpu.make_async_copy(yv.at[pl.ds(0, T_STREAM * GROUP_ROWS)], y_hbm.at[stream], sem.at[1])

    @pl.when(e == 0)
    def _():
        h2_copy().start()
        h2v[spare_off:spare_off + GROUP_ROWS, :] = jnp.zeros((GROUP_ROWS, LANES), F32)

        def pad_expert(ex, carry):
            lo = plan(PLAN_VALID_END, ex)

            def pad_chunk(k, c):
                for u in range(PAD_CHUNK):
                    off[lo + k * PAD_CHUNK + u] = T_STREAM
                return c
            n_chunks = (plan(PLAN_PAD_END, ex) - lo + (PAD_CHUNK - 1)) // PAD_CHUNK
            return lax.fori_loop(0, n_chunks, pad_chunk, carry)
        lax.fori_loop(0, N_EXPERTS, pad_expert, 0)

        def place(i, carry):
            for u in range(PLACE_UNROLL):
                tok = i * PLACE_UNROLL + u
                for slot in range(2):
                    off[pos_ref[s * n_pairs + slot * T_STREAM + tok]] = slot * SLOT_PITCH + tok
            return carry
        lax.fori_loop(0, T_STREAM // PLACE_UNROLL, place, 0)

        @pl.when(s > 0)
        def _():
            y_copy(s - 1).wait()
        yv[...] = jnp.zeros_like(yv)
        yg1[...] = jnp.zeros_like(yg1)
        h2_copy().wait()
        gather(0, xg0)

    t_lo = jnp.where(e == 0, 0, plan(PLAN_PAD_END, jnp.maximum(e - 1, 0))) // TMX
    t_hi = plan(PLAN_PAD_END, e) // TMX

    @pl.when(t_hi > t_lo)
    def _():
        wgb[...] = wg_ref[0, 0].astype(BF16)
        wub[...] = wu_ref[0, 0].astype(BF16)
        wdb[...] = wd_ref[0, 0].astype(BF16)

    def add_tile(tile, done):
        base = tile * TMX
        for j0 in range(0, TMX, RMW_BATCH):
            updates = []
            for j in range(j0, j0 + RMW_BATCH):
                pair = off[base + j]
                dst = row_offset(pair)
                gate = wpair_ref[s * (2 * SLOT_PITCH) + pair]
                row = done[pl.ds(j, GROUP_ROWS, stride=XSTRIDE), :]
                updates.append((dst, yv[pl.ds(dst, GROUP_ROWS), :] + gate * row))
            for dst, val in updates:
                yv[pl.ds(dst, GROUP_ROWS), :] = val

    def step(t, cur, nxt, out, done):
        gather(jnp.minimum(t + 1, last), nxt)
        x = jnp.concatenate([cur[c * XSTRIDE:c * XSTRIDE + TMX, :] for c in range(GROUP_ROWS)], axis=1).astype(BF16)
        hg = _dot(x, wgb[...])
        hu = _dot(x, wub[...])
        act = hg * jax.nn.sigmoid(hg) * hu
        y = _dot(act.astype(BF16), wdb[...])
        for c in range(GROUP_ROWS):
            out[c * XSTRIDE:c * XSTRIDE + TMX, :] = y[:, c * LANES:(c + 1) * LANES]
        add_tile(jnp.maximum(t - 1, 0), done)

    def tile_body(t, carry):
        for par in range(2):
            @pl.when((t & 1) == par)
            def _():
                step(t, xgs[par], xgs[1 - par], ygs[par], ygs[1 - par])
        return carry
    lax.fori_loop(t_lo, t_hi, tile_body, 0)

    @pl.when(e == N_EXPERTS - 1)
    def _():
        for par in range(2):
            @pl.when((last & 1) == par)
            def _():
                add_tile(last, ygs[par])
        y_copy(s).start()

        @pl.when(s == N_STREAMS - 1)
        def _():
            y_copy(s).wait()


def _moe(layer, pos, wpair, plan, h2g, w_gate, w_up, w_down):
    wspec = lambda shape: pl.BlockSpec((1,) + shape, lambda s, e, pos, wpair, plan: (layer, e, 0, 0))
    return pl.pallas_call(
        _moe_kernel,
        grid_spec=pltpu.PrefetchScalarGridSpec(
            num_scalar_prefetch=3,
            grid=(N_STREAMS, N_EXPERTS),
            in_specs=[pl.BlockSpec(memory_space=pl.ANY),
                      wspec((1, D_MODEL, D_EXPERT)), wspec((1, D_MODEL, D_EXPERT)), wspec((1, D_EXPERT, D_MODEL))],
            out_specs=pl.BlockSpec(memory_space=pl.ANY),
            scratch_shapes=[pltpu.VMEM(((T_STREAM + 1) * GROUP_ROWS, LANES), F32),
                            pltpu.VMEM(((T_STREAM + 1) * GROUP_ROWS, LANES), F32),
                            pltpu.SMEM((P_MAX + PAD_CHUNK,), jnp.int32),
                            pltpu.VMEM((GROUP_ROWS * XSTRIDE, LANES), F32),
                            pltpu.VMEM((GROUP_ROWS * XSTRIDE, LANES), F32),
                            pltpu.VMEM((GROUP_ROWS * XSTRIDE, LANES), F32),
                            pltpu.VMEM((GROUP_ROWS * XSTRIDE, LANES), F32),
                            pltpu.VMEM((D_MODEL, D_EXPERT), BF16), pltpu.VMEM((D_MODEL, D_EXPERT), BF16),
                            pltpu.VMEM((D_EXPERT, D_MODEL), BF16),
                            pltpu.SemaphoreType.DMA((2,))]),
        out_shape=jax.ShapeDtypeStruct((N_STREAMS, T_STREAM * GROUP_ROWS, LANES), F32),
        compiler_params=pltpu.CompilerParams(
            dimension_semantics=("arbitrary", "arbitrary"), vmem_limit_bytes=VMEM_LIMIT),
        name="moe",
    )(pos, wpair, plan, h2g, w_gate, w_up, w_down)


def _groups_to_rows(src_ref, stage_ref, n_rows):
    stride = n_rows + 8
    for j in range(n_rows):
        stage_ref[pl.ds(j, GROUP_ROWS, stride=stride), :] = src_ref[GROUP_ROWS * j:GROUP_ROWS * (j + 1), :]
    return jnp.concatenate([stage_ref[c * stride:c * stride + n_rows, :] for c in range(GROUP_ROWS)], axis=1)


def _final_kernel(x1_ref, y_ref, g2_ref, lg_ref, lb_ref, o_ref, stage_ref):
    moe = _groups_to_rows(y_ref, stage_ref, TT)
    o_ref[...] = _layer_norm(ALPHA * x1_ref[...] + g2_ref[0] * moe, lg_ref[0], lb_ref[0])


def _final(layer, stream, x1, y, mod, ln_g, ln_b):
    first = stream * (T_STREAM // TT)
    g2 = pl.BlockSpec(
        (1, 1, D_MODEL), lambda i: ((layer * MOD_ROWS + _mod_row_of_tile(first + i, TT)) * 6 + 5, 0, 0))
    return pl.pallas_call(
        _final_kernel,
        grid=(T_STREAM // TT,),
        in_specs=[pl.BlockSpec((TT, D_MODEL), lambda i: (first + i, 0)),
                  pl.BlockSpec((TT * GROUP_ROWS, LANES), lambda i: (first + i, 0)),
                  g2, _layer_spec(layer, (1, D_MODEL)), _layer_spec(layer, (1, D_MODEL))],
        out_specs=pl.BlockSpec((TT, D_MODEL), lambda i: (i, 0)),
        out_shape=jax.ShapeDtypeStruct((T_STREAM, D_MODEL), F32),
        scratch_shapes=[pltpu.VMEM((GROUP_ROWS * (TT + 8), LANES), F32)],
        compiler_params=pltpu.CompilerParams(dimension_semantics=("arbitrary",), vmem_limit_bytes=VMEM_LIMIT),
        name="final",
    )(x1, y, mod, ln_g, ln_b)


def _rope_tables():
    t = np.arange(DEC_SEQ)
    pos = np.stack([t // GRID_W, t % GRID_W], axis=1).astype(np.float32)
    nf = HEAD_DIM // 4
    inv = jnp.asarray(ROPE_THETA, F32) ** (-jnp.arange(nf, dtype=F32) / nf)
    d = np.arange(LANES) % HEAD_DIM
    which = d // (HEAD_DIM // 2)
    ang = jnp.asarray(pos)[:, which] * inv[d % nf][None, :]
    sign = np.where((d % 32) < 16, -1.0, 1.0).astype(np.float32)
    cos = jnp.concatenate([jnp.cos(ang), jnp.ones((TT, LANES), F32)], axis=0)
    sin = jnp.concatenate([jnp.sin(ang) * sign[None, :], jnp.zeros((TT, LANES), F32)], axis=0)
    return cos, sin


def _block_ones():
    h = np.arange(QB_W) // HEAD_DIM
    return jnp.asarray((h[:, None] == h[None, :]).astype(np.float32), dtype=BF16)


def _pool_block_diag(pool_w):
    out = jnp.zeros((POOL_WIDTH, POOL_WIDTH), F32)
    for g in range(4):
        out = out.at[64 * g:64 * g + 64, 64 * g:64 * g + 64].set(pool_w[g])
    return out


def kernel(x_prompt, x_sample, cache_b_k, cache_b_v, cache_c_k, cache_c_v, c, c_ctx, w_ada, b_ada, w_in, w_out,
           pool_w, pool_scale, q_norm, k_norm, rpb, ln1_g, ln1_b, ln2_g, ln2_b, router_g, router_g_b, router_e,
           router_e_b, w_gate, w_up, w_down):
    x_ctx = x_prompt.reshape(T_CTX, D_MODEL)
    x_lat = x_sample.reshape(T_LAT, D_MODEL)
    c8 = jnp.concatenate([c, c_ctx[None], jnp.zeros((MOD_ROWS - DEC_BATCH - 1, D_MODEL), F32)], axis=0)
    mod = _adaln(c8, w_ada, b_ada).reshape(DEPTH * MOD_ROWS * 6, 1, D_MODEL)

    cos_t, sin_t = _rope_tables()
    ones_bd = _block_ones()
    cbk = cache_b_k.reshape(DEC_BATCH, DEPTH, PAST_LEN, KB_W)
    cbv = cache_b_v.reshape(DEC_BATCH, DEPTH, PAST_LEN, KB_W)
    cck = cache_c_k.reshape(DEC_BATCH, DEPTH, PAST_LEN, C_W)
    ccv = cache_c_v.reshape(DEC_BATCH, DEPTH, PAST_LEN, C_W)
    pad = jnp.zeros((DEPTH, D_MODEL, ROUTER_LANES - N_GROUPS - N_EXPERTS), F32)
    wr = jnp.concatenate([router_g, router_e, pad], axis=2)
    br = jnp.concatenate([router_g_b, router_e_b, pad[:, 0]], axis=1)[:, None]
    per_layer = lambda p: p[:, None]

    caches = ()
    for l in range(DEPTH):
        qn = jnp.tile(q_norm[l], H_B)[None]
        kn = jnp.tile(k_norm[l], KV_B)[None]
        a, qb, kb, vb, qc, kc, vc = _proj(l, x_ctx, x_lat, mod, w_in, ones_bd, qn, kn, cos_t, sin_t)
        ya = _pool(a, _pool_block_diag(pool_w[l]), pool_scale[l][None])
        yb_ctx, yc_ctx, *caches = _attn_ctx(l, qb, kb, vb, qc, kc, vc, caches)
        yb_lat = _attn_latb(l, qb, kb, vb, cbk, cbv)
        yc_lat = _natten(l, qc, kc, vc, cck, ccv, _natten_blocks(rpb[l]))
        x1, h2g, ids, wts = _tail(l, x_ctx, x_lat, ya, yb_ctx, yb_lat, yc_ctx, yc_lat, w_out, mod,
                                  per_layer(ln1_g), per_layer(ln1_b), wr, br)
        pairs = lambda a: a.reshape(N_STREAMS, T_STREAM // TT, 8, TT)[:, :, :2, :].transpose(0, 2, 1, 3)
        pos, tiles = _plan(pairs(ids).reshape(N_STREAMS, 2 * T_STREAM // LANES, LANES))
        gates = jnp.pad(pairs(wts).reshape(N_STREAMS, 2, T_STREAM), ((0, 0), (0, 0), (0, SLOT_PITCH - T_STREAM)))
        y = _moe(l, pos.reshape(-1), gates.reshape(-1), tiles.reshape(-1), h2g, w_gate, w_up, w_down)
        y = y.reshape(T_ALL * GROUP_ROWS, LANES)
        x_ctx = _final(l, 0, x1, y, mod, per_layer(ln2_g), per_layer(ln2_b))
        x_lat = _final(l, 1, x1, y, mod, per_layer(ln2_g), per_layer(ln2_b))

    y_prompt = x_ctx.reshape(BATCH, SEQ, D_MODEL)
    y_sample = x_lat.reshape(DEC_BATCH, DEC_SEQ, D_MODEL)
    heads_last = lambda t, h: t.reshape(BATCH, DEPTH, h, HEAD_DIM, SEQ).transpose(0, 1, 4, 2, 3)
    new_bk, new_bv, new_ck, new_cv = caches
    return (y_prompt, y_sample, heads_last(new_bk, KV_B), heads_last(new_bv, KV_B),
            heads_last(new_ck, H_C), heads_last(new_cv, H_C))
```

```python
import functools

import numpy as np
import jax
import jax.numpy as jnp
from jax import lax
from jax.experimental import pallas as pl
from jax.experimental.pallas import tpu as pltpu

F32 = jnp.float32
BF16 = jnp.bfloat16

D_MODEL = 1024
BATCH = 16
SEQ = 256
DEPTH = 2
DEC_BATCH = 4
DEC_SEQ = 1024
PAST_LEN = 256
GRID_W = 64
GRID_ROWS = DEC_SEQ // GRID_W
HEAD_DIM = 64
POOL_WIDTH = 256
H_B = 6
KV_B = 2
H_C = 6
WIN_R = 8
WIN_C = 16
ROPE_THETA = 10000.0
QB_W = H_B * HEAD_DIM
KB_W = KV_B * HEAD_DIM
C_W = H_C * HEAD_DIM
PROJ_WIDTH = 2048
N_GROUPS = 4
EXPERTS_PER_GROUP = 8
N_EXPERTS = 32
D_EXPERT = 256
ALPHA = (2 * DEPTH) ** 0.25
LN_EPS = 1e-6
RMS_EPS = 1e-6
NEG = -1e30
ATTN_SCALE = HEAD_DIM ** -0.5

T_CTX = BATCH * SEQ
T_LAT = DEC_BATCH * DEC_SEQ
T_ALL = T_CTX + T_LAT

LANES = 128
ROUTER_LANES = 128
EXPERT_LANE0 = N_GROUPS
MOD_ROWS = 8
CTX_MOD_ROW = DEC_BATCH

TT = 512
TP = 1024
TF = 1024
POOL_GAP = 16
N_STREAMS = 2
T_STREAM = T_ALL // N_STREAMS
TMX = 128
PLACE_UNROLL = 8
PAD_CHUNK = 8
SLOT_PITCH = 2 * T_STREAM
P_MAX = 2 * T_STREAM + N_EXPERTS * TMX
GROUP_ROWS = D_MODEL // LANES
XSTRIDE = TMX + 8
RMW_BATCH = 16
PLAN_N_TILES, PLAN_VALID_END, PLAN_PAD_END = 0, 1, 2
HALF = DEC_SEQ // 2
NAT_KEYS = 12 * GRID_W
VMEM_LIMIT = 56 * 1024 * 1024


def _dot(a, b):
    return jnp.dot(a, b, preferred_element_type=F32)


def _dot_nt(a, b):
    return lax.dot_general(a, b, (((1,), (1,)), ((), ())), preferred_element_type=F32)


def _split_bf16(x):
    hi = x.astype(BF16)
    lo = (x - hi.astype(F32)).astype(BF16)
    return hi, lo


def _layer_norm(y, g, b):
    mu = jnp.mean(y, axis=-1, keepdims=True)
    var = jnp.mean(jnp.square(y - mu), axis=-1, keepdims=True)
    return (y - mu) * lax.rsqrt(var + LN_EPS) * g + b


def _adaln_kernel(c_ref, w_ref, b_ref, o_ref):
    c = c_ref[...]
    s = (c * jax.nn.sigmoid(c)).astype(BF16)
    o_ref[0] = _dot(s, w_ref[0].astype(BF16)) + b_ref[0]


def _adaln(c8, w_ada, b_ada):
    tn = 1536
    n = w_ada.shape[-1]
    return pl.pallas_call(
        _adaln_kernel,
        grid=(DEPTH, n // tn),
        in_specs=[
            pl.BlockSpec((MOD_ROWS, D_MODEL), lambda l, j: (0, 0)),
            pl.BlockSpec((1, D_MODEL, tn), lambda l, j: (l, 0, j)),
            pl.BlockSpec((1, 1, tn), lambda l, j: (l, 0, j)),
        ],
        out_specs=pl.BlockSpec((1, MOD_ROWS, tn), lambda l, j: (l, 0, j)),
        out_shape=jax.ShapeDtypeStruct((DEPTH, MOD_ROWS, n), F32),
        compiler_params=pltpu.CompilerParams(
            dimension_semantics=("arbitrary", "arbitrary"), vmem_limit_bytes=VMEM_LIMIT),
        name="adaln",
    )(c8, w_ada, b_ada.reshape(DEPTH, 1, n))


def _mod_row_of_tile(i, tile):
    n_ctx = T_CTX // tile
    per_req = DEC_SEQ // tile
    return jnp.where(i < n_ctx, CTX_MOD_ROW, (i - n_ctx) // per_req)


def _mod_spec(layer, chunk, tile):
    return pl.BlockSpec(
        (1, 1, D_MODEL), lambda i: ((layer * MOD_ROWS + _mod_row_of_tile(i, tile)) * 6 + chunk, 0, 0))


def _layer_spec(layer, shape):
    return pl.BlockSpec((1,) + shape, lambda *_: (layer,) + (0,) * len(shape))


N_CTX_TILES = T_CTX // TT


def _split_row_specs(width):
    return (pl.BlockSpec((TT, width), lambda i: (jnp.minimum(i, N_CTX_TILES - 1), 0)),
            pl.BlockSpec((TT, width), lambda i: (jnp.maximum(i - N_CTX_TILES, 0), 0)))


def _split_rows(ctx_ref, lat_ref):
    return jnp.where(pl.program_id(0) < N_CTX_TILES, ctx_ref[...], lat_ref[...])


def _rms_norm_heads(x, ones_bd, w):
    hi, lo = _split_bf16(x * x)
    ssq = _dot(hi, ones_bd) + _dot(lo, ones_bd)
    return x * lax.rsqrt(ssq * (1.0 / HEAD_DIM) + RMS_EPS) * w


def _rope_cols(x, cos, sin, first16):
    cols = []
    for j in range(x.shape[1] // LANES):
        xc = x[:, j * LANES:(j + 1) * LANES]
        partner = jnp.where(first16, pltpu.roll(xc, LANES - 16, axis=1), pltpu.roll(xc, 16, axis=1))
        cols.append(xc * cos + partner * sin)
    return jnp.concatenate(cols, axis=1) if len(cols) > 1 else cols[0]


def _proj_kernel(xc_ref, xl_ref, sh_ref, sc_ref, w_ref, ones_ref, qn_ref, kn_ref, cos_ref, sin_ref,
                 a_ref, qb_ref, kb_ref, vb_ref, qc_ref, kc_ref, vc_ref, wbf_ref):
    @pl.when(pl.program_id(0) == 0)
    def _():
        wbf_ref[...] = w_ref[0].astype(BF16)

    h = _split_rows(xc_ref, xl_ref) * (1.0 + sc_ref[0]) + sh_ref[0]
    p = _dot(h.astype(BF16), wbf_ref[...])
    o = 0
    a_ref[...] = p[:, o:o + POOL_WIDTH]; o += POOL_WIDTH
    qb = p[:, o:o + QB_W]; o += QB_W
    kb = p[:, o:o + KB_W]; o += KB_W
    vb_ref[...] = p[:, o:o + KB_W]; o += KB_W
    qc_ref[...] = (p[:, o:o + C_W] * ATTN_SCALE).astype(BF16); o += C_W
    kc_ref[...] = p[:, o:o + C_W]; o += C_W
    vc_ref[...] = p[:, o:o + C_W]

    ones_bd = ones_ref[...]
    cos = cos_ref[...]
    sin = sin_ref[...]
    lane = lax.broadcasted_iota(jnp.int32, (1, LANES), 1)
    first16 = (lane & 31) < 16
    qb = _rope_cols(_rms_norm_heads(qb, ones_bd, qn_ref[...]), cos, sin, first16)
    kb = _rope_cols(_rms_norm_heads(kb, ones_bd[:KB_W, :KB_W], kn_ref[...]), cos, sin, first16)
    qb_ref[...] = (qb * ATTN_SCALE).astype(BF16)
    kb_ref[...] = kb


def _proj(layer, x_ctx, x_lat, mod, w_in, ones_bd, qn, kn, cos_t, sin_t):
    n_tiles = T_ALL // TT
    n_ctx = N_CTX_TILES
    per_req = DEC_SEQ // TT

    def rope_idx(i):
        return (jnp.where(i < n_ctx, per_req, (i - n_ctx) % per_req), 0)

    row = lambda w: pl.BlockSpec((TT, w), lambda i: (i, 0))
    const = lambda s: pl.BlockSpec(s, lambda i: (0,) * len(s))
    widths = (POOL_WIDTH, QB_W, KB_W, KB_W, C_W, C_W, C_W)
    dtypes = (F32, BF16, F32, F32, BF16, F32, F32)
    return pl.pallas_call(
        _proj_kernel,
        grid=(n_tiles,),
        in_specs=[*_split_row_specs(D_MODEL), _mod_spec(layer, 0, TT), _mod_spec(layer, 1, TT),
                  _layer_spec(layer, (D_MODEL, PROJ_WIDTH)),
                  const((QB_W, QB_W)), const((1, QB_W)), const((1, KB_W)),
                  pl.BlockSpec((TT, LANES), rope_idx), pl.BlockSpec((TT, LANES), rope_idx)],
        out_specs=[row(w) for w in widths],
        out_shape=[jax.ShapeDtypeStruct((T_ALL, w), d) for w, d in zip(widths, dtypes)],
        scratch_shapes=[pltpu.VMEM((D_MODEL, PROJ_WIDTH), BF16)],
        compiler_params=pltpu.CompilerParams(dimension_semantics=("arbitrary",), vmem_limit_bytes=VMEM_LIMIT),
        name="proj",
    )(x_ctx, x_lat, mod, mod, w_in, ones_bd, qn, kn, cos_t, sin_t)


def _pool_windows(p):
    n = p.shape[0]
    sh = lambda x, k: pltpu.roll(x, (n - k) % n, axis=0)
    w2 = p + sh(p, -1)
    w4 = sh(w2, -1) + sh(w2, 1)
    w8 = sh(w4, -2) + sh(w4, 2)
    w16 = sh(w8, -4) + sh(w8, 4)
    g = lax.broadcasted_iota(jnp.int32, p.shape, 1) >> 6
    return jnp.where(g == 0, w2, jnp.where(g == 1, w4, jnp.where(g == 2, w8, w16)))


def _pool_kernel(a_ref, w_ref, scale_ref, o_ref, pad_ref):
    a = a_ref[...]

    def mix(csum, seq_len):
        row = lax.broadcasted_iota(jnp.int32, a.shape, 0)
        g = lax.broadcasted_iota(jnp.int32, a.shape, 1) >> 6
        t = row & (seq_len - 1)
        half = jnp.where(g == 0, 1, jnp.where(g == 1, 2, jnp.where(g == 2, 4, 8)))
        cnt = jnp.minimum(t + half, seq_len) - jnp.maximum(t - half, 0)
        pooled = csum / cnt.astype(F32) - a
        y = _dot(pooled.astype(BF16), w_ref[...].astype(BF16)) * scale_ref[...]
        o_ref[...] = y.astype(BF16)

    def padded_sums(seq_len):
        pitch = seq_len + POOL_GAP
        n_req = TP // seq_len
        n = n_req * pitch
        pad_ref[...] = jnp.zeros_like(pad_ref)
        for r in range(n_req):
            pad_ref[r * pitch + POOL_GAP // 2:r * pitch + POOL_GAP // 2 + seq_len, :] = a[r * seq_len:(r + 1) * seq_len]
        sums = _pool_windows(pad_ref[0:n, :])
        return jnp.concatenate(
            [sums[r * pitch + POOL_GAP // 2:r * pitch + POOL_GAP // 2 + seq_len] for r in range(n_req)], axis=0)

    is_ctx = pl.program_id(0) < T_CTX // TP

    @pl.when(is_ctx)
    def _():
        mix(padded_sums(SEQ), SEQ)

    @pl.when(jnp.logical_not(is_ctx))
    def _():
        mix(padded_sums(DEC_SEQ), DEC_SEQ)


def _pool(a, w_bd, scale):
    return pl.pallas_call(
        _pool_kernel,
        grid=(T_ALL // TP,),
        in_specs=[pl.BlockSpec((TP, POOL_WIDTH), lambda i: (i, 0)),
                  pl.BlockSpec((POOL_WIDTH, POOL_WIDTH), lambda i: (0, 0)),
                  pl.BlockSpec((1, POOL_WIDTH), lambda i: (0, 0))],
        out_specs=pl.BlockSpec((TP, POOL_WIDTH), lambda i: (i, 0)),
        out_shape=jax.ShapeDtypeStruct((T_ALL, POOL_WIDTH), BF16),
        scratch_shapes=[pltpu.VMEM(((TP // SEQ) * (SEQ + POOL_GAP), POOL_WIDTH), F32)],
        compiler_params=pltpu.CompilerParams(dimension_semantics=("arbitrary",), vmem_limit_bytes=VMEM_LIMIT),
        name="pool",
    )(a, w_bd, scale)


def _softmax_pv(scores, values):
    m = scores[0].max(axis=-1, keepdims=True)
    for s in scores[1:]:
        m = jnp.maximum(m, s.max(axis=-1, keepdims=True))
    ps = [jnp.exp(s - m) for s in scores]
    l = ps[0].sum(axis=-1, keepdims=True)
    for p in ps[1:]:
        l = l + p.sum(axis=-1, keepdims=True)
    o = _dot(ps[0].astype(BF16), values[0])
    for p, v in zip(ps[1:], values[1:]):
        o = o + _dot(p.astype(BF16), v)
    return o * (1.0 / l)


def _lane_halves():
    lane = lax.broadcasted_iota(jnp.int32, (1, LANES), 1)
    return lane < HEAD_DIM, lane >= HEAD_DIM


def _keep(x, mask):
    return jnp.where(mask, x, 0.0).astype(BF16)


def _gqa_variants(x):
    lo, hi = _lane_halves()
    xs = pltpu.roll(x, HEAD_DIM, axis=1)
    nat_lo, nat_hi = _keep(x, lo), _keep(x, hi)
    sw_lo, sw_hi = _keep(xs, lo), _keep(xs, hi)
    return ((nat_lo, sw_hi), (nat_lo, nat_hi), (sw_lo, nat_hi))


def _mha_variants(x):
    lo, hi = _lane_halves()
    out = []
    for j in range(x.shape[1] // LANES):
        xc = x[:, j * LANES:(j + 1) * LANES]
        out.append((_keep(xc, lo), _keep(xc, hi)))
    return tuple(out)


def _attend_cols(q, k_vars, v_vars, extra_k=None, extra_v=None, bias=None):
    cols = []
    for j in range(q.shape[1] // LANES):
        qc = q[:, j * LANES:(j + 1) * LANES]
        o = None
        for hh in range(2):
            s = _dot_nt(qc, k_vars[j][hh])
            if bias is not None:
                s = s + bias[j][hh]
            scores, values = [s], [v_vars[j][hh]]
            if extra_k is not None:
                scores.append(_dot_nt(qc, extra_k[j][hh]))
                values.append(extra_v[j][hh])
            oh = _softmax_pv(scores, values)
            o = oh if o is None else o + oh
        cols.append(o)
    return jnp.concatenate(cols, axis=1)


def _attn_ctx_kernel(layer_slot, qb_ref, kb_ref, vb_ref, qc_ref, kc_ref, vc_ref, *refs):
    yb_ref, yc_ref, kbt_ref, vbt_ref, kct_ref, vct_ref = refs[-6:]
    kb, vb, kc, vc = kb_ref[...], vb_ref[...], kc_ref[...], vc_ref[...]
    yb = _attend_cols(qb_ref[...], _gqa_variants(kb), _gqa_variants(vb))
    yb_ref[...] = yb.astype(BF16)
    yc = _attend_cols(qc_ref[...], _mha_variants(kc), _mha_variants(vc))
    yc_ref[...] = yc.astype(BF16)
    for ref, val in ((kbt_ref, kb), (vbt_ref, vb), (kct_ref, kc), (vct_ref, vc)):
        for slot in range(ref.shape[1]):
            ref[0, slot] = val.T if slot == layer_slot else jnp.zeros(ref.shape[2:], F32)


CACHE_WIDTHS = (KB_W, KB_W, C_W, C_W)


def _attn_ctx(layer, qb, kb, vb, qc, kc, vc, caches):
    row = lambda w: pl.BlockSpec((SEQ, w), lambda i: (i, 0))
    if caches:
        cache = lambda w: pl.BlockSpec((1, 1, w, SEQ), lambda i: (i, layer, 0, 0))
        layer_slot = 0
    else:
        cache = lambda w: pl.BlockSpec((1, DEPTH, w, SEQ), lambda i: (i, 0, 0, 0))
        layer_slot = layer
    return pl.pallas_call(
        functools.partial(_attn_ctx_kernel, layer_slot),
        grid=(BATCH,),
        in_specs=[row(QB_W), row(KB_W), row(KB_W), row(C_W), row(C_W), row(C_W)]
                 + [pl.BlockSpec(memory_space=pl.ANY)] * len(caches),
        out_specs=[row(QB_W), row(C_W)] + [cache(w) for w in CACHE_WIDTHS],
        out_shape=[jax.ShapeDtypeStruct((T_CTX, QB_W), BF16), jax.ShapeDtypeStruct((T_CTX, C_W), BF16)]
                  + [jax.ShapeDtypeStruct((BATCH, DEPTH, w, SEQ), F32) for w in CACHE_WIDTHS],
        input_output_aliases={6 + k: 2 + k for k in range(len(caches))},
        compiler_params=pltpu.CompilerParams(dimension_semantics=("arbitrary",), vmem_limit_bytes=VMEM_LIMIT),
        name="attn_ctx",
    )(qb, kb, vb, qc, kc, vc, *caches)


def _attn_latb_kernel(q_ref, k_ref, v_ref, ck_ref, cv_ref, y_ref):
    y = _attend_cols(q_ref[...], _gqa_variants(ck_ref[0, 0]), _gqa_variants(cv_ref[0, 0]),
                     extra_k=_gqa_variants(k_ref[...]), extra_v=_gqa_variants(v_ref[...]))
    y_ref[...] = y.astype(BF16)


def _attn_latb(layer, qb, kb, vb, cache_k, cache_v):
    ctx_h = T_CTX // HALF
    ctx_r = T_CTX // DEC_SEQ
    cache = pl.BlockSpec((1, 1, PAST_LEN, KB_W), lambda b, s: (b, layer, 0, 0))
    own = pl.BlockSpec((DEC_SEQ, KB_W), lambda b, s: (ctx_r + b, 0))
    return pl.pallas_call(
        _attn_latb_kernel,
        grid=(DEC_BATCH, DEC_SEQ // HALF),
        in_specs=[pl.BlockSpec((HALF, QB_W), lambda b, s: (ctx_h + 2 * b + s, 0)), own, own, cache, cache],
        out_specs=pl.BlockSpec((HALF, QB_W), lambda b, s: (2 * b + s, 0)),
        out_shape=jax.ShapeDtypeStruct((T_LAT, QB_W), BF16),
        compiler_params=pltpu.CompilerParams(
            dimension_semantics=("arbitrary", "arbitrary"), vmem_limit_bytes=VMEM_LIMIT),
        name="attn_latb",
    )(qb, kb, vb, cache_k, cache_v)


def _natten_window(s, i):
    r = (HALF // GRID_W) * s + i
    rs = min(max(r - WIN_R // 2, 0), GRID_ROWS - WIN_R)
    return r, rs


def _natten_kernel(q_ref, k_ref, v_ref, ck_ref, cv_ref, blk_ref, y_ref, bias_ref):
    s = pl.program_id(1)

    for sv in range(2):
        @pl.when((pl.program_id(2) == 0) & (s == sv))
        def _():
            lo_half, _ = _lane_halves()
            masked = jnp.full((GRID_W, LANES), NEG, F32)
            for hh in range(2):
                for i in range(HALF // GRID_W):
                    r, rs = _natten_window(sv, i)
                    for jp in range(NAT_KEYS // LANES):
                        pair = []
                        for j in (2 * jp, 2 * jp + 1):
                            rk = 4 * sv + j
                            pair.append(blk_ref[0, hh, rk - r + WIN_R - 1] if rs <= rk < rs + WIN_R else masked)
                        bias_ref[hh, i * GRID_W:(i + 1) * GRID_W, jp * LANES:(jp + 1) * LANES] = (
                            jnp.where(lo_half, pair[0], pair[1]))

    start = pl.multiple_of(s * (4 * GRID_W), 4 * GRID_W)
    k = _mha_variants(k_ref[pl.ds(start, NAT_KEYS), :])
    v = _mha_variants(v_ref[pl.ds(start, NAT_KEYS), :])
    ck = _mha_variants(ck_ref[0, 0])
    cv = _mha_variants(cv_ref[0, 0])
    bias = ((bias_ref[0], bias_ref[1]),)
    y = _attend_cols(q_ref[...], k, v, extra_k=ck, extra_v=cv, bias=bias)
    y_ref[...] = y.astype(BF16)


def _natten(layer, qc, kc, vc, cache_k, cache_v, blocks):
    ctx_h = T_CTX // HALF
    ctx_r = T_CTX // DEC_SEQ
    cache = pl.BlockSpec((1, 1, PAST_LEN, LANES), lambda j, s, b: (b, layer, 0, j))
    own = pl.BlockSpec((DEC_SEQ, LANES), lambda j, s, b: (ctx_r + b, j))
    return pl.pallas_call(
        _natten_kernel,
        grid=(C_W // LANES, DEC_SEQ // HALF, DEC_BATCH),
        in_specs=[pl.BlockSpec((HALF, LANES), lambda j, s, b: (ctx_h + 2 * b + s, j)), own, own, cache, cache,
                  pl.BlockSpec((1, 2, 2 * WIN_R - 1, GRID_W, LANES), lambda j, s, b: (j, 0, 0, 0, 0))],
        out_specs=pl.BlockSpec((HALF, LANES), lambda j, s, b: (2 * b + s, j)),
        out_shape=jax.ShapeDtypeStruct((T_LAT, C_W), BF16),
        scratch_shapes=[pltpu.VMEM((2, HALF, NAT_KEYS), F32)],
        compiler_params=pltpu.CompilerParams(
            dimension_semantics=("arbitrary", "arbitrary", "arbitrary"), vmem_limit_bytes=VMEM_LIMIT),
        name="natten",
    )(qc, kc, vc, cache_k, cache_v, blocks)


def _natten_blocks(rpb):
    qcol = np.arange(GRID_W)[:, None]
    kcol = np.arange(GRID_W)[None, :]
    cs = np.clip(qcol - WIN_C // 2, 0, GRID_W - WIN_C)
    col_ok = (kcol >= cs) & (kcol < cs + WIN_C)
    dc = np.clip(kcol - qcol, -(WIN_C - 1), WIN_C - 1) + (WIN_C - 1)
    sel_c = (dc[:, :, None] == np.arange(2 * WIN_C - 1)).astype(np.float32)
    blk = jnp.einsum("hdm,qkm->hdqk", rpb, jnp.asarray(sel_c), precision=lax.Precision.HIGHEST)
    blk = jnp.where(jnp.asarray(col_ok), blk, NEG)
    blk = jnp.concatenate([blk, blk], axis=-1)
    return blk.reshape(H_C // 2, 2, 2 * WIN_R - 1, GRID_W, LANES)


def _route(lg):
    lane = lax.broadcasted_iota(jnp.int32, lg.shape, 1).astype(F32)
    low = jnp.float32(-3.0e38)
    far = jnp.float32(ROUTER_LANES)
    is_g = lane < N_GROUPS
    gmax = jnp.where(is_g, lg, low).max(axis=-1, keepdims=True)
    gsel = jnp.where(is_g & (lg == gmax), lane, far).min(axis=-1, keepdims=True)
    pg_sel = 1.0 / jnp.where(is_g, jnp.exp(lg - gmax), 0.0).sum(axis=-1, keepdims=True)
    e0 = EXPERT_LANE0 + EXPERTS_PER_GROUP * gsel
    in_g = (lane >= e0) & (lane < e0 + EXPERTS_PER_GROUP)
    m1 = jnp.where(in_g, lg, low).max(axis=-1, keepdims=True)
    i1 = jnp.where(in_g & (lg == m1), lane, far).min(axis=-1, keepdims=True)
    rest = in_g & (lane != i1)
    m2 = jnp.where(rest, lg, low).max(axis=-1, keepdims=True)
    i2 = jnp.where(rest & (lg == m2), lane, far).min(axis=-1, keepdims=True)
    t = jnp.exp(m2 - m1)
    ssum = 1.0 + t
    w1 = (1.0 / ssum) * pg_sel
    w2 = (t / ssum) * pg_sel
    return i1 - EXPERT_LANE0, i2 - EXPERT_LANE0, w1, w2


def _rows_to_groups(x, stage_ref, out_ref, n_rows):
    stride = n_rows + 8
    for c in range(GROUP_ROWS):
        stage_ref[c * stride:c * stride + n_rows, :] = x[:, c * LANES:(c + 1) * LANES]
    for j in range(n_rows):
        out_ref[GROUP_ROWS * j:GROUP_ROWS * (j + 1), :] = stage_ref[pl.ds(j, GROUP_ROWS, stride=stride), :]


def _tail_kernel(xc_ref, xl_ref, ya_ref, ybc_ref, ybl_ref, ycc_ref, ycl_ref, w_ref, g1_ref, sh2_ref, sc2_ref,
                 lg_ref, lb_ref, wr_ref, br_ref, x1_ref, h2g_ref, ids_ref, wts_ref, wbf_ref, stage_ref):
    @pl.when(pl.program_id(0) == 0)
    def _():
        wbf_ref[...] = w_ref[0].astype(BF16)

    o = (_dot(ya_ref[...], wbf_ref[0:POOL_WIDTH, :])
         + _dot(_split_rows(ybc_ref, ybl_ref), wbf_ref[POOL_WIDTH:POOL_WIDTH + QB_W, :])
         + _dot(_split_rows(ycc_ref, ycl_ref), wbf_ref[POOL_WIDTH + QB_W:, :]))
    x1 = _layer_norm(ALPHA * _split_rows(xc_ref, xl_ref) + g1_ref[0] * o, lg_ref[0], lb_ref[0])
    x1_ref[...] = x1
    h2 = x1 * (1.0 + sc2_ref[0]) + sh2_ref[0]
    hh, hl = _split_bf16(h2)
    wh, wl = _split_bf16(wr_ref[0])
    lg = _dot(hh, wh) + _dot(hl, wh) + _dot(hh, wl) + br_ref[0]
    i1, i2, w1, w2 = _route(lg)
    lane = lax.broadcasted_iota(jnp.int32, lg.shape, 1)
    ids = jnp.where(lane == 0, i1, jnp.where(lane == 1, i2, 0.0))
    ids_ref[0] = ids.T[0:8, :].astype(jnp.int32)
    wts = jnp.where(lane == 0, w1, jnp.where(lane == 1, w2, 0.0))
    wts_ref[0] = wts.T[0:8, :]
    _rows_to_groups(h2, stage_ref, h2g_ref, TT)


def _tail(layer, x_ctx, x_lat, ya, yb_ctx, yb_lat, yc_ctx, yc_lat, w_out, mod, ln_g, ln_b, wr, br):
    row = lambda w: pl.BlockSpec((TT, w), lambda i: (i, 0))
    return pl.pallas_call(
        _tail_kernel,
        grid=(T_ALL // TT,),
        in_specs=[*_split_row_specs(D_MODEL), row(POOL_WIDTH), *_split_row_specs(QB_W), *_split_row_specs(C_W),
                  _layer_spec(layer, (D_MODEL, D_MODEL)),
                  _mod_spec(layer, 2, TT), _mod_spec(layer, 3, TT), _mod_spec(layer, 4, TT),
                  _layer_spec(layer, (1, D_MODEL)), _layer_spec(layer, (1, D_MODEL)),
                  _layer_spec(layer, (D_MODEL, ROUTER_LANES)), _layer_spec(layer, (1, ROUTER_LANES))],
        out_specs=[row(D_MODEL), pl.BlockSpec((TT * GROUP_ROWS, LANES), lambda i: (i, 0)),
                   pl.BlockSpec((1, 8, TT), lambda i: (i, 0, 0)), pl.BlockSpec((1, 8, TT), lambda i: (i, 0, 0))],
        out_shape=[jax.ShapeDtypeStruct((T_ALL, D_MODEL), F32),
                   jax.ShapeDtypeStruct((T_ALL * GROUP_ROWS, LANES), F32),
                   jax.ShapeDtypeStruct((T_ALL // TT, 8, TT), jnp.int32),
                   jax.ShapeDtypeStruct((T_ALL // TT, 8, TT), F32)],
        scratch_shapes=[pltpu.VMEM((D_MODEL, D_MODEL), BF16),
                        pltpu.VMEM((GROUP_ROWS * (TT + 8), LANES), F32)],
        compiler_params=pltpu.CompilerParams(dimension_semantics=("arbitrary",), vmem_limit_bytes=VMEM_LIMIT),
        name="tail",
    )(x_ctx, x_lat, ya, yb_ctx, yb_lat, yc_ctx, yc_lat, w_out, mod, mod, mod, ln_g, ln_b, wr, br)


def _plan_kernel(eid_ref, pos_ref, tile_ref):
    e = eid_ref[0]
    rows = e.shape[0]
    r = lax.broadcasted_iota(jnp.int32, (LANES, LANES), 0)
    c = lax.broadcasted_iota(jnp.int32, (LANES, LANES), 1)
    upper = (r <= c).astype(BF16)
    ones = jnp.ones((LANES, LANES), BF16)
    lower = (lax.broadcasted_iota(jnp.int32, (rows, rows), 1)
             < lax.broadcasted_iota(jnp.int32, (rows, rows), 0)).astype(BF16)
    lane = lax.broadcasted_iota(jnp.int32, (1, LANES), 1)
    masks = [e == ex for ex in range(N_EXPERTS)]
    stack = jnp.concatenate([m.astype(BF16) for m in masks], axis=0)
    incl_all = _dot(stack, upper)
    tot_all = _dot(stack, ones)
    pos = jnp.zeros(e.shape, F32)
    base = jnp.zeros((1, LANES), F32)
    valid_end = jnp.zeros((1, LANES), F32)
    pad_end = jnp.zeros((1, LANES), F32)
    for ex, m in enumerate(masks):
        incl = incl_all[ex * rows:(ex + 1) * rows]
        row_tot = tot_all[ex * rows:(ex + 1) * rows]
        row_off = _dot(lower, row_tot.astype(BF16))
        cnt = jnp.sum(row_tot, axis=0, keepdims=True)
        pos = jnp.where(m, base + row_off + incl - 1.0, pos)
        valid_end = jnp.where(lane == ex, base + cnt, valid_end)
        base = base + jnp.ceil(cnt * (1.0 / TMX)) * TMX
        pad_end = jnp.where(lane == ex, base, pad_end)
    pos_ref[0] = pos.astype(jnp.int32)
    sub = lax.broadcasted_iota(jnp.int32, (8, LANES), 0)
    rows8 = lambda v: jnp.broadcast_to(v, (8, LANES))
    table = jnp.where(sub == PLAN_N_TILES, rows8(base * (1.0 / TMX)),
                      jnp.where(sub == PLAN_VALID_END, rows8(valid_end),
                                jnp.where(sub == PLAN_PAD_END, rows8(pad_end), 0.0)))
    tile_ref[0] = table.astype(jnp.int32)


def _plan(eid):
    rows = 2 * T_STREAM // LANES
    return pl.pallas_call(
        _plan_kernel,
        grid=(N_STREAMS,),
        in_specs=[pl.BlockSpec((1, rows, LANES), lambda s: (s, 0, 0))],
        out_specs=[pl.BlockSpec((1, rows, LANES), lambda s: (s, 0, 0)), pl.BlockSpec((1, 8, LANES), lambda s: (s, 0, 0))],
        out_shape=[jax.ShapeDtypeStruct((N_STREAMS, rows, LANES), jnp.int32),
                   jax.ShapeDtypeStruct((N_STREAMS, 8, LANES), jnp.int32)],
        compiler_params=pltpu.CompilerParams(dimension_semantics=("arbitrary",), vmem_limit_bytes=VMEM_LIMIT),
        name="plan",
    )(eid)


def _moe_kernel(pos_ref, wpair_ref, plan_ref, h2g_hbm, wg_ref, wu_ref, wd_ref, y_hbm,
                h2v, yv, off, xg0, xg1, yg0, yg1, wgb, wub, wdb, sem):
    s = pl.program_id(0)
    e = pl.program_id(1)
    n_pairs = 2 * T_STREAM
    spare_off = T_STREAM * GROUP_ROWS
    plan = lambda row, idx: plan_ref[(s * 8 + row) * LANES + idx]
    n_tiles = plan(PLAN_N_TILES, 0)
    last = n_tiles - 1

    xgs = (xg0, xg1)
    ygs = (yg0, yg1)

    def row_offset(pair):
        return pl.multiple_of((pair & (SLOT_PITCH - 1)) << 3, GROUP_ROWS)

    def gather(tile, dst):
        base = tile * TMX
        for j in range(TMX):
            dst[pl.ds(j, GROUP_ROWS, stride=XSTRIDE), :] = h2v[pl.ds(row_offset(off[base + j]), GROUP_ROWS), :]

    def h2_copy():
        return pltpu.make_async_copy(
            h2g_hbm.at[pl.ds(pl.multiple_of(s * (T_STREAM * GROUP_ROWS), 8), T_STREAM * GROUP_ROWS)],
            h2v.at[pl.ds(0, T_STREAM * GROUP_ROWS)], sem.at[0])

    def y_copy(stream):
        return pltpu.make_async_copy(yv.at[pl.ds(0, T_STREAM * GROUP_ROWS)], y_hbm.at[stream], sem.at[1])

    @pl.when(e == 0)
    def _():
        h2_copy().start()
        h2v[spare_off:spare_off + GROUP_ROWS, :] = jnp.zeros((GROUP_ROWS, LANES), F32)

        def pad_expert(ex, carry):
            lo = plan(PLAN_VALID_END, ex)

            def pad_chunk(k, c):
                for u in range(PAD_CHUNK):
                    off[lo + k * PAD_CHUNK + u] = T_STREAM
                return c
            n_chunks = (plan(PLAN_PAD_END, ex) - lo + (PAD_CHUNK - 1)) // PAD_CHUNK
            return lax.fori_loop(0, n_chunks, pad_chunk, carry)
        lax.fori_loop(0, N_EXPERTS, pad_expert, 0)

        def place(i, carry):
            for u in range(PLACE_UNROLL):
                tok = i * PLACE_UNROLL + u
                for slot in range(2):
                    off[pos_ref[s * n_pairs + slot * T_STREAM + tok]] = slot * SLOT_PITCH + tok
            return carry
        lax.fori_loop(0, T_STREAM // PLACE_UNROLL, place, 0)

        @pl.when(s > 0)
        def _():
            y_copy(s - 1).wait()
        yv[...] = jnp.zeros_like(yv)
        yg1[...] = jnp.zeros_like(yg1)
        h2_copy().wait()
        gather(0, xg0)

    t_lo = jnp.where(e == 0, 0, plan(PLAN_PAD_END, jnp.maximum(e - 1, 0))) // TMX
    t_hi = plan(PLAN_PAD_END, e) // TMX

    @pl.when(t_hi > t_lo)
    def _():
        wgb[...] = wg_ref[0, 0].astype(BF16)
        wub[...] = wu_ref[0, 0].astype(BF16)
        wdb[...] = wd_ref[0, 0].astype(BF16)

    def add_tile(tile, done):
        base = tile * TMX
        for j0 in range(0, TMX, RMW_BATCH):
            updates = []
            for j in range(j0, j0 + RMW_BATCH):
                pair = off[base + j]
                dst = row_offset(pair)
                gate = wpair_ref[s * (2 * SLOT_PITCH) + pair]
                row = done[pl.ds(j, GROUP_ROWS, stride=XSTRIDE), :]
                updates.append((dst, yv[pl.ds(dst, GROUP_ROWS), :] + gate * row))
            for dst, val in updates:
                yv[pl.ds(dst, GROUP_ROWS), :] = val

    def step(t, cur, nxt, out, done):
        gather(jnp.minimum(t + 1, last), nxt)
        x = jnp.concatenate([cur[c * XSTRIDE:c * XSTRIDE + TMX, :] for c in range(GROUP_ROWS)], axis=1).astype(BF16)
        hg = _dot(x, wgb[...])
        hu = _dot(x, wub[...])
        act = hg * jax.nn.sigmoid(hg) * hu
        y = _dot(act.astype(BF16), wdb[...])
        for c in range(GROUP_ROWS):
            out[c * XSTRIDE:c * XSTRIDE + TMX, :] = y[:, c * LANES:(c + 1) * LANES]
        add_tile(jnp.maximum(t - 1, 0), done)

    def tile_body(t, carry):
        for par in range(2):
            @pl.when((t & 1) == par)
            def _():
                step(t, xgs[par], xgs[1 - par], ygs[par], ygs[1 - par])
        return carry
    lax.fori_loop(t_lo, t_hi, tile_body, 0)

    @pl.when(e == N_EXPERTS - 1)
    def _():
        for par in range(2):
            @pl.when((last & 1) == par)
            def _():
                add_tile(last, ygs[par])
        y_copy(s).start()

        @pl.when(s == N_STREAMS - 1)
        def _():
            y_copy(s).wait()


def _moe(layer, pos, wpair, plan, h2g, w_gate, w_up, w_down):
    wspec = lambda shape: pl.BlockSpec((1,) + shape, lambda s, e, pos, wpair, plan: (layer, e, 0, 0))
    return pl.pallas_call(
        _moe_kernel,
        grid_spec=pltpu.PrefetchScalarGridSpec(
            num_scalar_prefetch=3,
            grid=(N_STREAMS, N_EXPERTS),
            in_specs=[pl.BlockSpec(memory_space=pl.ANY),
                      wspec((1, D_MODEL, D_EXPERT)), wspec((1, D_MODEL, D_EXPERT)), wspec((1, D_EXPERT, D_MODEL))],
            out_specs=pl.BlockSpec(memory_space=pl.ANY),
            scratch_shapes=[pltpu.VMEM(((T_STREAM + 1) * GROUP_ROWS, LANES), F32),
                            pltpu.VMEM(((T_STREAM + 1) * GROUP_ROWS, LANES), F32),
                            pltpu.SMEM((P_MAX + PAD_CHUNK,), jnp.int32),
                            pltpu.VMEM((GROUP_ROWS * XSTRIDE, LANES), F32),
                            pltpu.VMEM((GROUP_ROWS * XSTRIDE, LANES), F32),
                            pltpu.VMEM((GROUP_ROWS * XSTRIDE, LANES), F32),
                            pltpu.VMEM((GROUP_ROWS * XSTRIDE, LANES), F32),
                            pltpu.VMEM((D_MODEL, D_EXPERT), BF16), pltpu.VMEM((D_MODEL, D_EXPERT), BF16),
                            pltpu.VMEM((D_EXPERT, D_MODEL), BF16),
                            pltpu.SemaphoreType.DMA((2,))]),
        out_shape=jax.ShapeDtypeStruct((N_STREAMS, T_STREAM * GROUP_ROWS, LANES), F32),
        compiler_params=pltpu.CompilerParams(
            dimension_semantics=("arbitrary", "arbitrary"), vmem_limit_bytes=VMEM_LIMIT),
        name="moe",
    )(pos, wpair, plan, h2g, w_gate, w_up, w_down)


def _groups_to_rows(src_ref, stage_ref, n_rows):
    stride = n_rows + 8
    for j in range(n_rows):
        stage_ref[pl.ds(j, GROUP_ROWS, stride=stride), :] = src_ref[GROUP_ROWS * j:GROUP_ROWS * (j + 1), :]
    return jnp.concatenate([stage_ref[c * stride:c * stride + n_rows, :] for c in range(GROUP_ROWS)], axis=1)


def _final_kernel(x1_ref, y_ref, g2_ref, lg_ref, lb_ref, o_ref, stage_ref):
    moe = _groups_to_rows(y_ref, stage_ref, TF)
    o_ref[...] = _layer_norm(ALPHA * x1_ref[...] + g2_ref[0] * moe, lg_ref[0], lb_ref[0])


def _final(layer, stream, x1, y, mod, ln_g, ln_b):
    first = stream * (T_STREAM // TF)
    g2 = pl.BlockSpec(
        (1, 1, D_MODEL), lambda i: ((layer * MOD_ROWS + _mod_row_of_tile(first + i, TF)) * 6 + 5, 0, 0))
    return pl.pallas_call(
        _final_kernel,
        grid=(T_STREAM // TF,),
        in_specs=[pl.BlockSpec((TF, D_MODEL), lambda i: (first + i, 0)),
                  pl.BlockSpec((TF * GROUP_ROWS, LANES), lambda i: (first + i, 0)),
                  g2, _layer_spec(layer, (1, D_MODEL)), _layer_spec(layer, (1, D_MODEL))],
        out_specs=pl.BlockSpec((TF, D_MODEL), lambda i: (i, 0)),
        out_shape=jax.ShapeDtypeStruct((T_STREAM, D_MODEL), F32),
        scratch_shapes=[pltpu.VMEM((GROUP_ROWS * (TF + 8), LANES), F32)],
        compiler_params=pltpu.CompilerParams(dimension_semantics=("arbitrary",), vmem_limit_bytes=VMEM_LIMIT),
        name="final",
    )(x1, y, mod, ln_g, ln_b)


def _rope_tables():
    t = np.arange(DEC_SEQ)
    pos = np.stack([t // GRID_W, t % GRID_W], axis=1).astype(np.float32)
    nf = HEAD_DIM // 4
    inv = jnp.asarray(ROPE_THETA, F32) ** (-jnp.arange(nf, dtype=F32) / nf)
    d = np.arange(LANES) % HEAD_DIM
    which = d // (HEAD_DIM // 2)
    ang = jnp.asarray(pos)[:, which] * inv[d % nf][None, :]
    sign = np.where((d % 32) < 16, -1.0, 1.0).astype(np.float32)
    cos = jnp.concatenate([jnp.cos(ang), jnp.ones((TT, LANES), F32)], axis=0)
    sin = jnp.concatenate([jnp.sin(ang) * sign[None, :], jnp.zeros((TT, LANES), F32)], axis=0)
    return cos, sin


def _block_ones():
    h = np.arange(QB_W) // HEAD_DIM
    return jnp.asarray((h[:, None] == h[None, :]).astype(np.float32), dtype=BF16)


def _pool_block_diag(pool_w):
    out = jnp.zeros((POOL_WIDTH, POOL_WIDTH), F32)
    for g in range(4):
        out = out.at[64 * g:64 * g + 64, 64 * g:64 * g + 64].set(pool_w[g])
    return out


def kernel(x_prompt, x_sample, cache_b_k, cache_b_v, cache_c_k, cache_c_v, c, c_ctx, w_ada, b_ada, w_in, w_out,
           pool_w, pool_scale, q_norm, k_norm, rpb, ln1_g, ln1_b, ln2_g, ln2_b, router_g, router_g_b, router_e,
           router_e_b, w_gate, w_up, w_down):
    x_ctx = x_prompt.reshape(T_CTX, D_MODEL)
    x_lat = x_sample.reshape(T_LAT, D_MODEL)
    c8 = jnp.concatenate([c, c_ctx[None], jnp.zeros((MOD_ROWS - DEC_BATCH - 1, D_MODEL), F32)], axis=0)
    mod = _adaln(c8, w_ada, b_ada).reshape(DEPTH * MOD_ROWS * 6, 1, D_MODEL)

    cos_t, sin_t = _rope_tables()
    ones_bd = _block_ones()
    cbk = cache_b_k.reshape(DEC_BATCH, DEPTH, PAST_LEN, KB_W)
    cbv = cache_b_v.reshape(DEC_BATCH, DEPTH, PAST_LEN, KB_W)
    cck = cache_c_k.reshape(DEC_BATCH, DEPTH, PAST_LEN, C_W)
    ccv = cache_c_v.reshape(DEC_BATCH, DEPTH, PAST_LEN, C_W)
    pad = jnp.zeros((DEPTH, D_MODEL, ROUTER_LANES - N_GROUPS - N_EXPERTS), F32)
    wr = jnp.concatenate([router_g, router_e, pad], axis=2)
    br = jnp.concatenate([router_g_b, router_e_b, pad[:, 0]], axis=1)[:, None]
    per_layer = lambda p: p[:, None]

    caches = ()
    for l in range(DEPTH):
        qn = jnp.tile(q_norm[l], H_B)[None]
        kn = jnp.tile(k_norm[l], KV_B)[None]
        a, qb, kb, vb, qc, kc, vc = _proj(l, x_ctx, x_lat, mod, w_in, ones_bd, qn, kn, cos_t, sin_t)
        ya = _pool(a, _pool_block_diag(pool_w[l]), pool_scale[l][None])
        yb_ctx, yc_ctx, *caches = _attn_ctx(l, qb, kb, vb, qc, kc, vc, caches)
        yb_lat = _attn_latb(l, qb, kb, vb, cbk, cbv)
        yc_lat = _natten(l, qc, kc, vc, cck, ccv, _natten_blocks(rpb[l]))
        x1, h2g, ids, wts = _tail(l, x_ctx, x_lat, ya, yb_ctx, yb_lat, yc_ctx, yc_lat, w_out, mod,
                                  per_layer(ln1_g), per_layer(ln1_b), wr, br)
        pairs = lambda a: a.reshape(N_STREAMS, T_STREAM // TT, 8, TT)[:, :, :2, :].transpose(0, 2, 1, 3)
        pos, tiles = _plan(pairs(ids).reshape(N_STREAMS, 2 * T_STREAM // LANES, LANES))
        gates = jnp.pad(pairs(wts).reshape(N_STREAMS, 2, T_STREAM), ((0, 0), (0, 0), (0, SLOT_PITCH - T_STREAM)))
        y = _moe(l, pos.reshape(-1), gates.reshape(-1), tiles.reshape(-1), h2g, w_gate, w_up, w_down)
        y = y.reshape(T_ALL * GROUP_ROWS, LANES)
        x_ctx = _final(l, 0, x1, y, mod, per_layer(ln2_g), per_layer(ln2_b))
        x_lat = _final(l, 1, x1, y, mod, per_layer(ln2_g), per_layer(ln2_b))

    y_prompt = x_ctx.reshape(BATCH, SEQ, D_MODEL)
    y_sample = x_lat.reshape(DEC_BATCH, DEC_SEQ, D_MODEL)
    heads_last = lambda t, h: t.reshape(BATCH, DEPTH, h, HEAD_DIM, SEQ).transpose(0, 1, 4, 2, 3)
    new_bk, new_bv, new_ck, new_cv = caches
    return (y_prompt, y_sample, heads_last(new_bk, KV_B), heads_last(new_bv, KV_B),
            heads_last(new_ck, H_C), heads_last(new_cv, H_C))
```

```python
import functools

import numpy as np
import jax
import jax.numpy as jnp
from jax import lax
from jax.experimental import pallas as pl
from jax.experimental.pallas import tpu as pltpu

F32 = jnp.float32
BF16 = jnp.bfloat16

D_MODEL = 1024
BATCH = 16
SEQ = 256
DEPTH = 2
DEC_BATCH = 4
DEC_SEQ = 1024
PAST_LEN = 256
GRID_W = 64
GRID_ROWS = DEC_SEQ // GRID_W
HEAD_DIM = 64
POOL_WIDTH = 256
H_B = 6
KV_B = 2
H_C = 6
WIN_R = 8
WIN_C = 16
ROPE_THETA = 10000.0
QB_W = H_B * HEAD_DIM
KB_W = KV_B * HEAD_DIM
C_W = H_C * HEAD_DIM
PROJ_WIDTH = 2048
N_GROUPS = 4
EXPERTS_PER_GROUP = 8
N_EXPERTS = 32
D_EXPERT = 256
ALPHA = (2 * DEPTH) ** 0.25
LN_EPS = 1e-6
RMS_EPS = 1e-6
NEG = -1e30
ATTN_SCALE = HEAD_DIM ** -0.5

T_CTX = BATCH * SEQ
T_LAT = DEC_BATCH * DEC_SEQ
T_ALL = T_CTX + T_LAT

LANES = 128
ROUTER_LANES = 128
EXPERT_LANE0 = N_GROUPS
MOD_ROWS = 8
CTX_MOD_ROW = DEC_BATCH

TT = 512
TP = 1024
TF = 1024
POOL_GAP = 16
N_STREAMS = 2
T_STREAM = T_ALL // N_STREAMS
TMX = 128
PLACE_UNROLL = 8
PAD_CHUNK = 8
SLOT_PITCH = 2 * T_STREAM
P_MAX = 2 * T_STREAM + N_EXPERTS * TMX
GROUP_ROWS = D_MODEL // LANES
XSTRIDE = TMX + 8
RMW_BATCH = 16
PLAN_N_TILES, PLAN_VALID_END, PLAN_PAD_END = 0, 1, 2
HALF = DEC_SEQ // 2
NAT_KEYS = 12 * GRID_W
VMEM_LIMIT = 56 * 1024 * 1024


def _dot(a, b):
    return jnp.dot(a, b, preferred_element_type=F32)


def _dot_nt(a, b):
    return lax.dot_general(a, b, (((1,), (1,)), ((), ())), preferred_element_type=F32)


def _split_bf16(x):
    hi = x.astype(BF16)
    lo = (x - hi.astype(F32)).astype(BF16)
    return hi, lo


def _layer_norm(y, g, b):
    mu = jnp.mean(y, axis=-1, keepdims=True)
    var = jnp.mean(jnp.square(y - mu), axis=-1, keepdims=True)
    return (y - mu) * lax.rsqrt(var + LN_EPS) * g + b


def _adaln_kernel(c_ref, w_ref, b_ref, o_ref):
    c = c_ref[...]
    s = (c * jax.nn.sigmoid(c)).astype(BF16)
    o_ref[0] = _dot(s, w_ref[0].astype(BF16)) + b_ref[0]


def _adaln(c8, w_ada, b_ada):
    tn = 1536
    n = w_ada.shape[-1]
    return pl.pallas_call(
        _adaln_kernel,
        grid=(DEPTH, n // tn),
        in_specs=[
            pl.BlockSpec((MOD_ROWS, D_MODEL), lambda l, j: (0, 0)),
            pl.BlockSpec((1, D_MODEL, tn), lambda l, j: (l, 0, j)),
            pl.BlockSpec((1, 1, tn), lambda l, j: (l, 0, j)),
        ],
        out_specs=pl.BlockSpec((1, MOD_ROWS, tn), lambda l, j: (l, 0, j)),
        out_shape=jax.ShapeDtypeStruct((DEPTH, MOD_ROWS, n), F32),
        compiler_params=pltpu.CompilerParams(
            dimension_semantics=("arbitrary", "arbitrary"), vmem_limit_bytes=VMEM_LIMIT),
        name="adaln",
    )(c8, w_ada, b_ada.reshape(DEPTH, 1, n))


def _mod_row_of_tile(i, tile):
    n_ctx = T_CTX // tile
    per_req = DEC_SEQ // tile
    return jnp.where(i < n_ctx, CTX_MOD_ROW, (i - n_ctx) // per_req)


def _mod_spec(layer, chunk, tile):
    return pl.BlockSpec(
        (1, 1, D_MODEL), lambda i: ((layer * MOD_ROWS + _mod_row_of_tile(i, tile)) * 6 + chunk, 0, 0))


def _layer_spec(layer, shape):
    return pl.BlockSpec((1,) + shape, lambda *_: (layer,) + (0,) * len(shape))


N_CTX_TILES = T_CTX // TT


def _split_row_specs(width):
    return (pl.BlockSpec((TT, width), lambda i: (jnp.minimum(i, N_CTX_TILES - 1), 0)),
            pl.BlockSpec((TT, width), lambda i: (jnp.maximum(i - N_CTX_TILES, 0), 0)))


def _split_rows(ctx_ref, lat_ref):
    return jnp.where(pl.program_id(0) < N_CTX_TILES, ctx_ref[...], lat_ref[...])


def _rms_norm_heads(x, ones_bd, w):
    hi, lo = _split_bf16(x * x)
    ssq = _dot(hi, ones_bd) + _dot(lo, ones_bd)
    return x * lax.rsqrt(ssq * (1.0 / HEAD_DIM) + RMS_EPS) * w


def _rope_cols(x, cos, sin, first16):
    cols = []
    for j in range(x.shape[1] // LANES):
        xc = x[:, j * LANES:(j + 1) * LANES]
        partner = jnp.where(first16, pltpu.roll(xc, LANES - 16, axis=1), pltpu.roll(xc, 16, axis=1))
        cols.append(xc * cos + partner * sin)
    return jnp.concatenate(cols, axis=1) if len(cols) > 1 else cols[0]


def _proj_kernel(xc_ref, xl_ref, sh_ref, sc_ref, w_ref, ones_ref, qn_ref, kn_ref, cos_ref, sin_ref,
                 a_ref, qb_ref, kb_ref, vb_ref, qc_ref, kc_ref, vc_ref, wbf_ref):
    @pl.when(pl.program_id(0) == 0)
    def _():
        wbf_ref[...] = w_ref[0].astype(BF16)

    h = _split_rows(xc_ref, xl_ref) * (1.0 + sc_ref[0]) + sh_ref[0]
    p = _dot(h.astype(BF16), wbf_ref[...])
    o = 0
    a_ref[...] = p[:, o:o + POOL_WIDTH]; o += POOL_WIDTH
    qb = p[:, o:o + QB_W]; o += QB_W
    kb = p[:, o:o + KB_W]; o += KB_W
    vb_ref[...] = p[:, o:o + KB_W]; o += KB_W
    qc_ref[...] = (p[:, o:o + C_W] * ATTN_SCALE).astype(BF16); o += C_W
    kc_ref[...] = p[:, o:o + C_W]; o += C_W
    vc_ref[...] = p[:, o:o + C_W]

    ones_bd = ones_ref[...]
    cos = cos_ref[...]
    sin = sin_ref[...]
    lane = lax.broadcasted_iota(jnp.int32, (1, LANES), 1)
    first16 = (lane & 31) < 16
    qb = _rope_cols(_rms_norm_heads(qb, ones_bd, qn_ref[...]), cos, sin, first16)
    kb = _rope_cols(_rms_norm_heads(kb, ones_bd[:KB_W, :KB_W], kn_ref[...]), cos, sin, first16)
    qb_ref[...] = (qb * ATTN_SCALE).astype(BF16)
    kb_ref[...] = kb


def _proj(layer, x_ctx, x_lat, mod, w_in, ones_bd, qn, kn, cos_t, sin_t):
    n_tiles = T_ALL // TT
    n_ctx = N_CTX_TILES
    per_req = DEC_SEQ // TT

    def rope_idx(i):
        return (jnp.where(i < n_ctx, per_req, (i - n_ctx) % per_req), 0)

    row = lambda w: pl.BlockSpec((TT, w), lambda i: (i, 0))
    const = lambda s: pl.BlockSpec(s, lambda i: (0,) * len(s))
    widths = (POOL_WIDTH, QB_W, KB_W, KB_W, C_W, C_W, C_W)
    dtypes = (F32, BF16, F32, F32, BF16, F32, F32)
    return pl.pallas_call(
        _proj_kernel,
        grid=(n_tiles,),
        in_specs=[*_split_row_specs(D_MODEL), _mod_spec(layer, 0, TT), _mod_spec(layer, 1, TT),
                  _layer_spec(layer, (D_MODEL, PROJ_WIDTH)),
                  const((QB_W, QB_W)), const((1, QB_W)), const((1, KB_W)),
                  pl.BlockSpec((TT, LANES), rope_idx), pl.BlockSpec((TT, LANES), rope_idx)],
        out_specs=[row(w) for w in widths],
        out_shape=[jax.ShapeDtypeStruct((T_ALL, w), d) for w, d in zip(widths, dtypes)],
        scratch_shapes=[pltpu.VMEM((D_MODEL, PROJ_WIDTH), BF16)],
        compiler_params=pltpu.CompilerParams(dimension_semantics=("arbitrary",), vmem_limit_bytes=VMEM_LIMIT),
        name="proj",
    )(x_ctx, x_lat, mod, mod, w_in, ones_bd, qn, kn, cos_t, sin_t)


def _pool_windows(p):
    n = p.shape[0]
    sh = lambda x, k: pltpu.roll(x, (n - k) % n, axis=0)
    w2 = p + sh(p, -1)
    w4 = sh(w2, -1) + sh(w2, 1)
    w8 = sh(w4, -2) + sh(w4, 2)
    w16 = sh(w8, -4) + sh(w8, 4)
    g = lax.broadcasted_iota(jnp.int32, p.shape, 1) >> 6
    return jnp.where(g == 0, w2, jnp.where(g == 1, w4, jnp.where(g == 2, w8, w16)))


def _pool_kernel(a_ref, w_ref, scale_ref, o_ref, pad_ref):
    a = a_ref[...]

    def mix(csum, seq_len):
        row = lax.broadcasted_iota(jnp.int32, a.shape, 0)
        g = lax.broadcasted_iota(jnp.int32, a.shape, 1) >> 6
        t = row & (seq_len - 1)
        half = jnp.where(g == 0, 1, jnp.where(g == 1, 2, jnp.where(g == 2, 4, 8)))
        cnt = jnp.minimum(t + half, seq_len) - jnp.maximum(t - half, 0)
        pooled = csum / cnt.astype(F32) - a
        y = _dot(pooled.astype(BF16), w_ref[...].astype(BF16)) * scale_ref[...]
        o_ref[...] = y.astype(BF16)

    def padded_sums(seq_len):
        pitch = seq_len + POOL_GAP
        n_req = TP // seq_len
        n = n_req * pitch
        pad_ref[...] = jnp.zeros_like(pad_ref)
        for r in range(n_req):
            pad_ref[r * pitch + POOL_GAP // 2:r * pitch + POOL_GAP // 2 + seq_len, :] = a[r * seq_len:(r + 1) * seq_len]
        sums = _pool_windows(pad_ref[0:n, :])
        return jnp.concatenate(
            [sums[r * pitch + POOL_GAP // 2:r * pitch + POOL_GAP // 2 + seq_len] for r in range(n_req)], axis=0)

    is_ctx = pl.program_id(0) < T_CTX // TP

    @pl.when(is_ctx)
    def _():
        mix(padded_sums(SEQ), SEQ)

    @pl.when(jnp.logical_not(is_ctx))
    def _():
        mix(padded_sums(DEC_SEQ), DEC_SEQ)


def _pool(a, w_bd, scale):
    return pl.pallas_call(
        _pool_kernel,
        grid=(T_ALL // TP,),
        in_specs=[pl.BlockSpec((TP, POOL_WIDTH), lambda i: (i, 0)),
                  pl.BlockSpec((POOL_WIDTH, POOL_WIDTH), lambda i: (0, 0)),
                  pl.BlockSpec((1, POOL_WIDTH), lambda i: (0, 0))],
        out_specs=pl.BlockSpec((TP, POOL_WIDTH), lambda i: (i, 0)),
        out_shape=jax.ShapeDtypeStruct((T_ALL, POOL_WIDTH), BF16),
        scratch_shapes=[pltpu.VMEM(((TP // SEQ) * (SEQ + POOL_GAP), POOL_WIDTH), F32)],
        compiler_params=pltpu.CompilerParams(dimension_semantics=("arbitrary",), vmem_limit_bytes=VMEM_LIMIT),
        name="pool",
    )(a, w_bd, scale)


def _softmax_pv(scores, values):
    m = scores[0].max(axis=-1, keepdims=True)
    for s in scores[1:]:
        m = jnp.maximum(m, s.max(axis=-1, keepdims=True))
    ps = [jnp.exp(s - m) for s in scores]
    l = ps[0].sum(axis=-1, keepdims=True)
    for p in ps[1:]:
        l = l + p.sum(axis=-1, keepdims=True)
    o = _dot(ps[0].astype(BF16), values[0])
    for p, v in zip(ps[1:], values[1:]):
        o = o + _dot(p.astype(BF16), v)
    return o * (1.0 / l)


def _lane_halves():
    lane = lax.broadcasted_iota(jnp.int32, (1, LANES), 1)
    return lane < HEAD_DIM, lane >= HEAD_DIM


def _keep(x, mask):
    return jnp.where(mask, x, 0.0).astype(BF16)


def _gqa_variants(x):
    lo, hi = _lane_halves()
    xs = pltpu.roll(x, HEAD_DIM, axis=1)
    nat_lo, nat_hi = _keep(x, lo), _keep(x, hi)
    sw_lo, sw_hi = _keep(xs, lo), _keep(xs, hi)
    return ((nat_lo, sw_hi), (nat_lo, nat_hi), (sw_lo, nat_hi))


def _mha_variants(x):
    lo, hi = _lane_halves()
    out = []
    for j in range(x.shape[1] // LANES):
        xc = x[:, j * LANES:(j + 1) * LANES]
        out.append((_keep(xc, lo), _keep(xc, hi)))
    return tuple(out)


def _attend_cols(q, k_vars, v_vars, extra_k=None, extra_v=None, bias=None):
    cols = []
    for j in range(q.shape[1] // LANES):
        qc = q[:, j * LANES:(j + 1) * LANES]
        o = None
        for hh in range(2):
            s = _dot_nt(qc, k_vars[j][hh])
            if bias is not None:
                s = s + bias[j][hh]
            scores, values = [s], [v_vars[j][hh]]
            if extra_k is not None:
                scores.append(_dot_nt(qc, extra_k[j][hh]))
                values.append(extra_v[j][hh])
            oh = _softmax_pv(scores, values)
            o = oh if o is None else o + oh
        cols.append(o)
    return jnp.concatenate(cols, axis=1)


def _attn_ctx_kernel(layer_slot, qb_ref, kb_ref, vb_ref, qc_ref, kc_ref, vc_ref, *refs):
    yb_ref, yc_ref, kbt_ref, vbt_ref, kct_ref, vct_ref = refs[-6:]
    for r in range(CTX_PER_STEP):
        rows = slice(r * SEQ, (r + 1) * SEQ)
        kb, vb, kc, vc = kb_ref[rows, :], vb_ref[rows, :], kc_ref[rows, :], vc_ref[rows, :]
        yb = _attend_cols(qb_ref[rows, :], _gqa_variants(kb), _gqa_variants(vb))
        yb_ref[rows, :] = yb.astype(BF16)
        yc = _attend_cols(qc_ref[rows, :], _mha_variants(kc), _mha_variants(vc))
        yc_ref[rows, :] = yc.astype(BF16)
        for ref, val in ((kbt_ref, kb), (vbt_ref, vb), (kct_ref, kc), (vct_ref, vc)):
            for slot in range(ref.shape[1]):
                ref[r, slot] = val.T if slot == layer_slot else jnp.zeros(ref.shape[2:], F32)


CACHE_WIDTHS = (KB_W, KB_W, C_W, C_W)
CTX_PER_STEP = 2


def _attn_ctx(layer, qb, kb, vb, qc, kc, vc, caches):
    row = lambda w: pl.BlockSpec((CTX_PER_STEP * SEQ, w), lambda i: (i, 0))
    if caches:
        cache = lambda w: pl.BlockSpec((CTX_PER_STEP, 1, w, SEQ), lambda i: (i, layer, 0, 0))
        layer_slot = 0
    else:
        cache = lambda w: pl.BlockSpec((CTX_PER_STEP, DEPTH, w, SEQ), lambda i: (i, 0, 0, 0))
        layer_slot = layer
    return pl.pallas_call(
        functools.partial(_attn_ctx_kernel, layer_slot),
        grid=(BATCH // CTX_PER_STEP,),
        in_specs=[row(QB_W), row(KB_W), row(KB_W), row(C_W), row(C_W), row(C_W)]
                 + [pl.BlockSpec(memory_space=pl.ANY)] * len(caches),
        out_specs=[row(QB_W), row(C_W)] + [cache(w) for w in CACHE_WIDTHS],
        out_shape=[jax.ShapeDtypeStruct((T_CTX, QB_W), BF16), jax.ShapeDtypeStruct((T_CTX, C_W), BF16)]
                  + [jax.ShapeDtypeStruct((BATCH, DEPTH, w, SEQ), F32) for w in CACHE_WIDTHS],
        input_output_aliases={6 + k: 2 + k for k in range(len(caches))},
        compiler_params=pltpu.CompilerParams(dimension_semantics=("arbitrary",), vmem_limit_bytes=VMEM_LIMIT),
        name="attn_ctx",
    )(qb, kb, vb, qc, kc, vc, *caches)


def _attn_latb_kernel(q_ref, k_ref, v_ref, ck_ref, cv_ref, y_ref):
    y = _attend_cols(q_ref[...], _gqa_variants(ck_ref[0, 0]), _gqa_variants(cv_ref[0, 0]),
                     extra_k=_gqa_variants(k_ref[...]), extra_v=_gqa_variants(v_ref[...]))
    y_ref[...] = y.astype(BF16)


def _attn_latb(layer, qb, kb, vb, cache_k, cache_v):
    ctx_h = T_CTX // HALF
    ctx_r = T_CTX // DEC_SEQ
    cache = pl.BlockSpec((1, 1, PAST_LEN, KB_W), lambda b, s: (b, layer, 0, 0))
    own = pl.BlockSpec((DEC_SEQ, KB_W), lambda b, s: (ctx_r + b, 0))
    return pl.pallas_call(
        _attn_latb_kernel,
        grid=(DEC_BATCH, DEC_SEQ // HALF),
        in_specs=[pl.BlockSpec((HALF, QB_W), lambda b, s: (ctx_h + 2 * b + s, 0)), own, own, cache, cache],
        out_specs=pl.BlockSpec((HALF, QB_W), lambda b, s: (2 * b + s, 0)),
        out_shape=jax.ShapeDtypeStruct((T_LAT, QB_W), BF16),
        compiler_params=pltpu.CompilerParams(
            dimension_semantics=("arbitrary", "arbitrary"), vmem_limit_bytes=VMEM_LIMIT),
        name="attn_latb",
    )(qb, kb, vb, cache_k, cache_v)


def _natten_window(s, i):
    r = (HALF // GRID_W) * s + i
    rs = min(max(r - WIN_R // 2, 0), GRID_ROWS - WIN_R)
    return r, rs


def _natten_kernel(q_ref, k_ref, v_ref, ck_ref, cv_ref, blk_ref, y_ref, bias_ref):
    s = pl.program_id(1)

    for sv in range(2):
        @pl.when((pl.program_id(2) == 0) & (s == sv))
        def _():
            lo_half, _ = _lane_halves()
            masked = jnp.full((GRID_W, LANES), NEG, F32)
            for hh in range(2):
                for i in range(HALF // GRID_W):
                    r, rs = _natten_window(sv, i)
                    for jp in range(NAT_KEYS // LANES):
                        pair = []
                        for j in (2 * jp, 2 * jp + 1):
                            rk = 4 * sv + j
                            pair.append(blk_ref[0, hh, rk - r + WIN_R - 1] if rs <= rk < rs + WIN_R else masked)
                        bias_ref[hh, i * GRID_W:(i + 1) * GRID_W, jp * LANES:(jp + 1) * LANES] = (
                            jnp.where(lo_half, pair[0], pair[1]))

    start = pl.multiple_of(s * (4 * GRID_W), 4 * GRID_W)
    k = _mha_variants(k_ref[pl.ds(start, NAT_KEYS), :])
    v = _mha_variants(v_ref[pl.ds(start, NAT_KEYS), :])
    ck = _mha_variants(ck_ref[0, 0])
    cv = _mha_variants(cv_ref[0, 0])
    bias = ((bias_ref[0], bias_ref[1]),)
    y = _attend_cols(q_ref[...], k, v, extra_k=ck, extra_v=cv, bias=bias)
    y_ref[...] = y.astype(BF16)


def _natten(layer, qc, kc, vc, cache_k, cache_v, blocks):
    ctx_h = T_CTX // HALF
    ctx_r = T_CTX // DEC_SEQ
    cache = pl.BlockSpec((1, 1, PAST_LEN, LANES), lambda j, s, b: (b, layer, 0, j))
    own = pl.BlockSpec((DEC_SEQ, LANES), lambda j, s, b: (ctx_r + b, j))
    return pl.pallas_call(
        _natten_kernel,
        grid=(C_W // LANES, DEC_SEQ // HALF, DEC_BATCH),
        in_specs=[pl.BlockSpec((HALF, LANES), lambda j, s, b: (ctx_h + 2 * b + s, j)), own, own, cache, cache,
                  pl.BlockSpec((1, 2, 2 * WIN_R - 1, GRID_W, LANES), lambda j, s, b: (j, 0, 0, 0, 0))],
        out_specs=pl.BlockSpec((HALF, LANES), lambda j, s, b: (2 * b + s, j)),
        out_shape=jax.ShapeDtypeStruct((T_LAT, C_W), BF16),
        scratch_shapes=[pltpu.VMEM((2, HALF, NAT_KEYS), F32)],
        compiler_params=pltpu.CompilerParams(
            dimension_semantics=("arbitrary", "arbitrary", "arbitrary"), vmem_limit_bytes=VMEM_LIMIT),
        name="natten",
    )(qc, kc, vc, cache_k, cache_v, blocks)


def _natten_blocks(rpb):
    qcol = np.arange(GRID_W)[:, None]
    kcol = np.arange(GRID_W)[None, :]
    cs = np.clip(qcol - WIN_C // 2, 0, GRID_W - WIN_C)
    col_ok = (kcol >= cs) & (kcol < cs + WIN_C)
    dc = np.clip(kcol - qcol, -(WIN_C - 1), WIN_C - 1) + (WIN_C - 1)
    sel_c = (dc[:, :, None] == np.arange(2 * WIN_C - 1)).astype(np.float32)
    blk = jnp.einsum("hdm,qkm->hdqk", rpb, jnp.asarray(sel_c), precision=lax.Precision.HIGHEST)
    blk = jnp.where(jnp.asarray(col_ok), blk, NEG)
    blk = jnp.concatenate([blk, blk], axis=-1)
    return blk.reshape(H_C // 2, 2, 2 * WIN_R - 1, GRID_W, LANES)


def _route(lg):
    lane = lax.broadcasted_iota(jnp.int32, lg.shape, 1).astype(F32)
    low = jnp.float32(-3.0e38)
    far = jnp.float32(ROUTER_LANES)
    is_g = lane < N_GROUPS
    gmax = jnp.where(is_g, lg, low).max(axis=-1, keepdims=True)
    gsel = jnp.where(is_g & (lg == gmax), lane, far).min(axis=-1, keepdims=True)
    pg_sel = 1.0 / jnp.where(is_g, jnp.exp(lg - gmax), 0.0).sum(axis=-1, keepdims=True)
    e0 = EXPERT_LANE0 + EXPERTS_PER_GROUP * gsel
    in_g = (lane >= e0) & (lane < e0 + EXPERTS_PER_GROUP)
    m1 = jnp.where(in_g, lg, low).max(axis=-1, keepdims=True)
    i1 = jnp.where(in_g & (lg == m1), lane, far).min(axis=-1, keepdims=True)
    rest = in_g & (lane != i1)
    m2 = jnp.where(rest, lg, low).max(axis=-1, keepdims=True)
    i2 = jnp.where(rest & (lg == m2), lane, far).min(axis=-1, keepdims=True)
    t = jnp.exp(m2 - m1)
    ssum = 1.0 + t
    w1 = (1.0 / ssum) * pg_sel
    w2 = (t / ssum) * pg_sel
    return i1 - EXPERT_LANE0, i2 - EXPERT_LANE0, w1, w2


def _rows_to_groups(x, stage_ref, out_ref, n_rows):
    stride = n_rows + 8
    for c in range(GROUP_ROWS):
        stage_ref[c * stride:c * stride + n_rows, :] = x[:, c * LANES:(c + 1) * LANES]
    for j in range(n_rows):
        out_ref[GROUP_ROWS * j:GROUP_ROWS * (j + 1), :] = stage_ref[pl.ds(j, GROUP_ROWS, stride=stride), :]


def _tail_kernel(xc_ref, xl_ref, ya_ref, ybc_ref, ybl_ref, ycc_ref, ycl_ref, w_ref, g1_ref, sh2_ref, sc2_ref,
                 lg_ref, lb_ref, wr_ref, br_ref, x1_ref, h2g_ref, ids_ref, wts_ref, wbf_ref, stage_ref):
    @pl.when(pl.program_id(0) == 0)
    def _():
        wbf_ref[...] = w_ref[0].astype(BF16)

    o = (_dot(ya_ref[...], wbf_ref[0:POOL_WIDTH, :])
         + _dot(_split_rows(ybc_ref, ybl_ref), wbf_ref[POOL_WIDTH:POOL_WIDTH + QB_W, :])
         + _dot(_split_rows(ycc_ref, ycl_ref), wbf_ref[POOL_WIDTH + QB_W:, :]))
    x1 = _layer_norm(ALPHA * _split_rows(xc_ref, xl_ref) + g1_ref[0] * o, lg_ref[0], lb_ref[0])
    x1_ref[...] = x1
    h2 = x1 * (1.0 + sc2_ref[0]) + sh2_ref[0]
    hh, hl = _split_bf16(h2)
    wh, wl = _split_bf16(wr_ref[0])
    lg = _dot(hh, wh) + _dot(hl, wh) + _dot(hh, wl) + br_ref[0]
    i1, i2, w1, w2 = _route(lg)
    lane = lax.broadcasted_iota(jnp.int32, lg.shape, 1)
    ids = jnp.where(lane == 0, i1, jnp.where(lane == 1, i2, 0.0))
    ids_ref[0] = ids.T[0:8, :].astype(jnp.int32)
    wts = jnp.where(lane == 0, w1, jnp.where(lane == 1, w2, 0.0))
    wts_ref[0] = wts.T[0:8, :]
    _rows_to_groups(h2, stage_ref, h2g_ref, TT)


def _tail(layer, x_ctx, x_lat, ya, yb_ctx, yb_lat, yc_ctx, yc_lat, w_out, mod, ln_g, ln_b, wr, br):
    row = lambda w: pl.BlockSpec((TT, w), lambda i: (i, 0))
    return pl.pallas_call(
        _tail_kernel,
        grid=(T_ALL // TT,),
        in_specs=[*_split_row_specs(D_MODEL), row(POOL_WIDTH), *_split_row_specs(QB_W), *_split_row_specs(C_W),
                  _layer_spec(layer, (D_MODEL, D_MODEL)),
                  _mod_spec(layer, 2, TT), _mod_spec(layer, 3, TT), _mod_spec(layer, 4, TT),
                  _layer_spec(layer, (1, D_MODEL)), _layer_spec(layer, (1, D_MODEL)),
                  _layer_spec(layer, (D_MODEL, ROUTER_LANES)), _layer_spec(layer, (1, ROUTER_LANES))],
        out_specs=[row(D_MODEL), pl.BlockSpec((TT * GROUP_ROWS, LANES), lambda i: (i, 0)),
                   pl.BlockSpec((1, 8, TT), lambda i: (i, 0, 0)), pl.BlockSpec((1, 8, TT), lambda i: (i, 0, 0))],
        out_shape=[jax.ShapeDtypeStruct((T_ALL, D_MODEL), F32),
                   jax.ShapeDtypeStruct((T_ALL * GROUP_ROWS, LANES), F32),
                   jax.ShapeDtypeStruct((T_ALL // TT, 8, TT), jnp.int32),
                   jax.ShapeDtypeStruct((T_ALL // TT, 8, TT), F32)],
        scratch_shapes=[pltpu.VMEM((D_MODEL, D_MODEL), BF16),
                        pltpu.VMEM((GROUP_ROWS * (TT + 8), LANES), F32)],
        compiler_params=pltpu.CompilerParams(dimension_semantics=("arbitrary",), vmem_limit_bytes=VMEM_LIMIT),
        name="tail",
    )(x_ctx, x_lat, ya, yb_ctx, yb_lat, yc_ctx, yc_lat, w_out, mod, mod, mod, ln_g, ln_b, wr, br)


def _plan_kernel(eid_ref, pos_ref, tile_ref):
    e = eid_ref[0]
    rows = e.shape[0]
    r = lax.broadcasted_iota(jnp.int32, (LANES, LANES), 0)
    c = lax.broadcasted_iota(jnp.int32, (LANES, LANES), 1)
    upper = (r <= c).astype(BF16)
    ones = jnp.ones((LANES, LANES), BF16)
    lower = (lax.broadcasted_iota(jnp.int32, (rows, rows), 1)
             < lax.broadcasted_iota(jnp.int32, (rows, rows), 0)).astype(BF16)
    lane = lax.broadcasted_iota(jnp.int32, (1, LANES), 1)
    masks = [e == ex for ex in range(N_EXPERTS)]
    stack = jnp.concatenate([m.astype(BF16) for m in masks], axis=0)
    incl_all = _dot(stack, upper)
    tot_all = _dot(stack, ones)
    pos = jnp.zeros(e.shape, F32)
    base = jnp.zeros((1, LANES), F32)
    valid_end = jnp.zeros((1, LANES), F32)
    pad_end = jnp.zeros((1, LANES), F32)
    for ex, m in enumerate(masks):
        incl = incl_all[ex * rows:(ex + 1) * rows]
        row_tot = tot_all[ex * rows:(ex + 1) * rows]
        row_off = _dot(lower, row_tot.astype(BF16))
        cnt = jnp.sum(row_tot, axis=0, keepdims=True)
        pos = jnp.where(m, base + row_off + incl - 1.0, pos)
        valid_end = jnp.where(lane == ex, base + cnt, valid_end)
        base = base + jnp.ceil(cnt * (1.0 / TMX)) * TMX
        pad_end = jnp.where(lane == ex, base, pad_end)
    pos_ref[0] = pos.astype(jnp.int32)
    sub = lax.broadcasted_iota(jnp.int32, (8, LANES), 0)
    rows8 = lambda v: jnp.broadcast_to(v, (8, LANES))
    table = jnp.where(sub == PLAN_N_TILES, rows8(base * (1.0 / TMX)),
                      jnp.where(sub == PLAN_VALID_END, rows8(valid_end),
                                jnp.where(sub == PLAN_PAD_END, rows8(pad_end), 0.0)))
    tile_ref[0] = table.astype(jnp.int32)


def _plan(eid):
    rows = 2 * T_STREAM // LANES
    return pl.pallas_call(
        _plan_kernel,
        grid=(N_STREAMS,),
        in_specs=[pl.BlockSpec((1, rows, LANES), lambda s: (s, 0, 0))],
        out_specs=[pl.BlockSpec((1, rows, LANES), lambda s: (s, 0, 0)), pl.BlockSpec((1, 8, LANES), lambda s: (s, 0, 0))],
        out_shape=[jax.ShapeDtypeStruct((N_STREAMS, rows, LANES), jnp.int32),
                   jax.ShapeDtypeStruct((N_STREAMS, 8, LANES), jnp.int32)],
        compiler_params=pltpu.CompilerParams(dimension_semantics=("arbitrary",), vmem_limit_bytes=VMEM_LIMIT),
        name="plan",
    )(eid)


def _moe_kernel(pos_ref, wpair_ref, plan_ref, h2g_hbm, wg_ref, wu_ref, wd_ref, y_hbm,
                h2v, yv, off, xg0, xg1, yg0, yg1, wgb, wub, wdb, sem):
    s = pl.program_id(0)
    e = pl.program_id(1)
    n_pairs = 2 * T_STREAM
    spare_off = T_STREAM * GROUP_ROWS
    plan = lambda row, idx: plan_ref[(s * 8 + row) * LANES + idx]
    n_tiles = plan(PLAN_N_TILES, 0)
    last = n_tiles - 1

    xgs = (xg0, xg1)
    ygs = (yg0, yg1)

    def row_offset(pair):
        return pl.multiple_of((pair & (SLOT_PITCH - 1)) << 3, GROUP_ROWS)

    def gather(tile, dst):
        base = tile * TMX
        for j in range(TMX):
            dst[pl.ds(j, GROUP_ROWS, stride=XSTRIDE), :] = h2v[pl.ds(row_offset(off[base + j]), GROUP_ROWS), :]

    def h2_copy():
        return pltpu.make_async_copy(
            h2g_hbm.at[pl.ds(pl.multiple_of(s * (T_STREAM * GROUP_ROWS), 8), T_STREAM * GROUP_ROWS)],
            h2v.at[pl.ds(0, T_STREAM * GROUP_ROWS)], sem.at[0])

    def y_copy(stream):
        return pltpu.make_async_copy(yv.at[pl.ds(0, T_STREAM * GROUP_ROWS)], y_hbm.at[stream], sem.at[1])

    @pl.when(e == 0)
    def _():
        h2_copy().start()
        h2v[spare_off:spare_off + GROUP_ROWS, :] = jnp.zeros((GROUP_ROWS, LANES), F32)

        def pad_expert(ex, carry):
            lo = plan(PLAN_VALID_END, ex)

            def pad_chunk(k, c):
                for u in range(PAD_CHUNK):
                    off[lo + k * PAD_CHUNK + u] = T_STREAM
                return c
            n_chunks = (plan(PLAN_PAD_END, ex) - lo + (PAD_CHUNK - 1)) // PAD_CHUNK
            return lax.fori_loop(0, n_chunks, pad_chunk, carry)
        lax.fori_loop(0, N_EXPERTS, pad_expert, 0)

        def place(i, carry):
            for u in range(PLACE_UNROLL):
                tok = i * PLACE_UNROLL + u
                for slot in range(2):
                    off[pos_ref[s * n_pairs + slot * T_STREAM + tok]] = slot * SLOT_PITCH + tok
            return carry
        lax.fori_loop(0, T_STREAM // PLACE_UNROLL, place, 0)

        @pl.when(s > 0)
        def _():
            y_copy(s - 1).wait()
        yv[...] = jnp.zeros_like(yv)
        yg1[...] = jnp.zeros_like(yg1)
        h2_copy().wait()
        gather(0, xg0)

    t_lo = jnp.where(e == 0, 0, plan(PLAN_PAD_END, jnp.maximum(e - 1, 0))) // TMX
    t_hi = plan(PLAN_PAD_END, e) // TMX

    @pl.when(t_hi > t_lo)
    def _():
        wgb[...] = wg_ref[0, 0].astype(BF16)
        wub[...] = wu_ref[0, 0].astype(BF16)
        wdb[...] = wd_ref[0, 0].astype(BF16)

    def add_tile(tile, done):
        base = tile * TMX
        for j0 in range(0, TMX, RMW_BATCH):
            updates = []
            for j in range(j0, j0 + RMW_BATCH):
                pair = off[base + j]
                dst = row_offset(pair)
                gate = wpair_ref[s * (2 * SLOT_PITCH) + pair]
                row = done[pl.ds(j, GROUP_ROWS, stride=XSTRIDE), :]
                updates.append((dst, yv[pl.ds(dst, GROUP_ROWS), :] + gate * row))
            for dst, val in updates:
                yv[pl.ds(dst, GROUP_ROWS), :] = val

    def step(t, cur, nxt, out, done):
        gather(jnp.minimum(t + 1, last), nxt)
        x = jnp.concatenate([cur[c * XSTRIDE:c * XSTRIDE + TMX, :] for c in range(GROUP_ROWS)], axis=1).astype(BF16)
        hg = _dot(x, wgb[...])
        hu = _dot(x, wub[...])
        act = hg * jax.nn.sigmoid(hg) * hu
        y = _dot(act.astype(BF16), wdb[...])
        for c in range(GROUP_ROWS):
            out[c * XSTRIDE:c * XSTRIDE + TMX, :] = y[:, c * LANES:(c + 1) * LANES]
        add_tile(jnp.maximum(t - 1, 0), done)

    def tile_body(t, carry):
        for par in range(2):
            @pl.when((t & 1) == par)
            def _():
                step(t, xgs[par], xgs[1 - par], ygs[par], ygs[1 - par])
        return carry
    lax.fori_loop(t_lo, t_hi, tile_body, 0)

    @pl.when(e == N_EXPERTS - 1)
    def _():
        for par in range(2):
            @pl.when((last & 1) == par)
            def _():
                add_tile(last, ygs[par])
        y_copy(s).start()

        @pl.when(s == N_STREAMS - 1)
        def _():
            y_copy(s).wait()


def _moe(layer, pos, wpair, plan, h2g, w_gate, w_up, w_down):
    wspec = lambda shape: pl.BlockSpec((1,) + shape, lambda s, e, pos, wpair, plan: (layer, e, 0, 0))
    return pl.pallas_call(
        _moe_kernel,
        grid_spec=pltpu.PrefetchScalarGridSpec(
            num_scalar_prefetch=3,
            grid=(N_STREAMS, N_EXPERTS),
            in_specs=[pl.BlockSpec(memory_space=pl.ANY),
                      wspec((1, D_MODEL, D_EXPERT)), wspec((1, D_MODEL, D_EXPERT)), wspec((1, D_EXPERT, D_MODEL))],
            out_specs=pl.BlockSpec(memory_space=pl.ANY),
            scratch_shapes=[pltpu.VMEM(((T_STREAM + 1) * GROUP_ROWS, LANES), F32),
                            pltpu.VMEM(((T_STREAM + 1) * GROUP_ROWS, LANES), F32),
                            pltpu.SMEM((P_MAX + PAD_CHUNK,), jnp.int32),
                            pltpu.VMEM((GROUP_ROWS * XSTRIDE, LANES), F32),
                            pltpu.VMEM((GROUP_ROWS * XSTRIDE, LANES), F32),
                            pltpu.VMEM((GROUP_ROWS * XSTRIDE, LANES), F32),
                            pltpu.VMEM((GROUP_ROWS * XSTRIDE, LANES), F32),
                            pltpu.VMEM((D_MODEL, D_EXPERT), BF16), pltpu.VMEM((D_MODEL, D_EXPERT), BF16),
                            pltpu.VMEM((D_EXPERT, D_MODEL), BF16),
                            pltpu.SemaphoreType.DMA((2,))]),
        out_shape=jax.ShapeDtypeStruct((N_STREAMS, T_STREAM * GROUP_ROWS, LANES), F32),
        compiler_params=pltpu.CompilerParams(
            dimension_semantics=("arbitrary", "arbitrary"), vmem_limit_bytes=VMEM_LIMIT),
        name="moe",
    )(pos, wpair, plan, h2g, w_gate, w_up, w_down)


def _groups_to_rows(src_ref, stage_ref, n_rows):
    stride = n_rows + 8
    for j in range(n_rows):
        stage_ref[pl.ds(j, GROUP_ROWS, stride=stride), :] = src_ref[GROUP_ROWS * j:GROUP_ROWS * (j + 1), :]
    return jnp.concatenate([stage_ref[c * stride:c * stride + n_rows, :] for c in range(GROUP_ROWS)], axis=1)


def _final_kernel(x1_ref, y_ref, g2_ref, lg_ref, lb_ref, o_ref, stage_ref):
    moe = _groups_to_rows(y_ref, stage_ref, TF)
    o_ref[...] = _layer_norm(ALPHA * x1_ref[...] + g2_ref[0] * moe, lg_ref[0], lb_ref[0])


def _final(layer, stream, x1, y, mod, ln_g, ln_b):
    first = stream * (T_STREAM // TF)
    g2 = pl.BlockSpec(
        (1, 1, D_MODEL), lambda i: ((layer * MOD_ROWS + _mod_row_of_tile(first + i, TF)) * 6 + 5, 0, 0))
    return pl.pallas_call(
        _final_kernel,
        grid=(T_STREAM // TF,),
        in_specs=[pl.BlockSpec((TF, D_MODEL), lambda i: (first + i, 0)),
                  pl.BlockSpec((TF * GROUP_ROWS, LANES), lambda i: (first + i, 0)),
                  g2, _layer_spec(layer, (1, D_MODEL)), _layer_spec(layer, (1, D_MODEL))],
        out_specs=pl.BlockSpec((TF, D_MODEL), lambda i: (i, 0)),
        out_shape=jax.ShapeDtypeStruct((T_STREAM, D_MODEL), F32),
        scratch_shapes=[pltpu.VMEM((GROUP_ROWS * (TF + 8), LANES), F32)],
        compiler_params=pltpu.CompilerParams(dimension_semantics=("arbitrary",), vmem_limit_bytes=VMEM_LIMIT),
        name="final",
    )(x1, y, mod, ln_g, ln_b)


def _rope_tables():
    t = np.arange(DEC_SEQ)
    pos = np.stack([t // GRID_W, t % GRID_W], axis=1).astype(np.float32)
    nf = HEAD_DIM // 4
    inv = jnp.asarray(ROPE_THETA, F32) ** (-jnp.arange(nf, dtype=F32) / nf)
    d = np.arange(LANES) % HEAD_DIM
    which = d // (HEAD_DIM // 2)
    ang = jnp.asarray(pos)[:, which] * inv[d % nf][None, :]
    sign = np.where((d % 32) < 16, -1.0, 1.0).astype(np.float32)
    cos = jnp.concatenate([jnp.cos(ang), jnp.ones((TT, LANES), F32)], axis=0)
    sin = jnp.concatenate([jnp.sin(ang) * sign[None, :], jnp.zeros((TT, LANES), F32)], axis=0)
    return cos, sin


def _block_ones():
    h = np.arange(QB_W) // HEAD_DIM
    return jnp.asarray((h[:, None] == h[None, :]).astype(np.float32), dtype=BF16)


def _pool_block_diag(pool_w):
    out = jnp.zeros((POOL_WIDTH, POOL_WIDTH), F32)
    for g in range(4):
        out = out.at[64 * g:64 * g + 64, 64 * g:64 * g + 64].set(pool_w[g])
    return out


def kernel(x_prompt, x_sample, cache_b_k, cache_b_v, cache_c_k, cache_c_v, c, c_ctx, w_ada, b_ada, w_in, w_out,
           pool_w, pool_scale, q_norm, k_norm, rpb, ln1_g, ln1_b, ln2_g, ln2_b, router_g, router_g_b, router_e,
           router_e_b, w_gate, w_up, w_down):
    x_ctx = x_prompt.reshape(T_CTX, D_MODEL)
    x_lat = x_sample.reshape(T_LAT, D_MODEL)
    c8 = jnp.concatenate([c, c_ctx[None], jnp.zeros((MOD_ROWS - DEC_BATCH - 1, D_MODEL), F32)], axis=0)
    mod = _adaln(c8, w_ada, b_ada).reshape(DEPTH * MOD_ROWS * 6, 1, D_MODEL)

    cos_t, sin_t = _rope_tables()
    ones_bd = _block_ones()
    cbk = cache_b_k.reshape(DEC_BATCH, DEPTH, PAST_LEN, KB_W)
    cbv = cache_b_v.reshape(DEC_BATCH, DEPTH, PAST_LEN, KB_W)
    cck = cache_c_k.reshape(DEC_BATCH, DEPTH, PAST_LEN, C_W)
    ccv = cache_c_v.reshape(DEC_BATCH, DEPTH, PAST_LEN, C_W)
    pad = jnp.zeros((DEPTH, D_MODEL, ROUTER_LANES - N_GROUPS - N_EXPERTS), F32)
    wr = jnp.concatenate([router_g, router_e, pad], axis=2)
    br = jnp.concatenate([router_g_b, router_e_b, pad[:, 0]], axis=1)[:, None]
    per_layer = lambda p: p[:, None]

    caches = ()
    for l in range(DEPTH):
        qn = jnp.tile(q_norm[l], H_B)[None]
        kn = jnp.tile(k_norm[l], KV_B)[None]
        a, qb, kb, vb, qc, kc, vc = _proj(l, x_ctx, x_lat, mod, w_in, ones_bd, qn, kn, cos_t, sin_t)
        ya = _pool(a, _pool_block_diag(pool_w[l]), pool_scale[l][None])
        yb_ctx, yc_ctx, *caches = _attn_ctx(l, qb, kb, vb, qc, kc, vc, caches)
        yb_lat = _attn_latb(l, qb, kb, vb, cbk, cbv)
        yc_lat = _natten(l, qc, kc, vc, cck, ccv, _natten_blocks(rpb[l]))
        x1, h2g, ids, wts = _tail(l, x_ctx, x_lat, ya, yb_ctx, yb_lat, yc_ctx, yc_lat, w_out, mod,
                                  per_layer(ln1_g), per_layer(ln1_b), wr, br)
        pairs = lambda a: a.reshape(N_STREAMS, T_STREAM // TT, 8, TT)[:, :, :2, :].transpose(0, 2, 1, 3)
        pos, tiles = _plan(pairs(ids).reshape(N_STREAMS, 2 * T_STREAM // LANES, LANES))
        gates = jnp.pad(pairs(wts).reshape(N_STREAMS, 2, T_STREAM), ((0, 0), (0, 0), (0, SLOT_PITCH - T_STREAM)))
        y = _moe(l, pos.reshape(-1), gates.reshape(-1), tiles.reshape(-1), h2g, w_gate, w_up, w_down)
        y = y.reshape(T_ALL * GROUP_ROWS, LANES)
        x_ctx = _final(l, 0, x1, y, mod, per_layer(ln2_g), per_layer(ln2_b))
        x_lat = _final(l, 1, x1, y, mod, per_layer(ln2_g), per_layer(ln2_b))

    y_prompt = x_ctx.reshape(BATCH, SEQ, D_MODEL)
    y_sample = x_lat.reshape(DEC_BATCH, DEC_SEQ, D_MODEL)
    heads_last = lambda t, h: t.reshape(BATCH, DEPTH, h, HEAD_DIM, SEQ).transpose(0, 1, 4, 2, 3)
    new_bk, new_bv, new_ck, new_cv = caches
    return (y_prompt, y_sample, heads_last(new_bk, KV_B), heads_last(new_bv, KV_B),
            heads_last(new_ck, H_C), heads_last(new_cv, H_C))
```

```python
import functools

import numpy as np
import jax
import jax.numpy as jnp
from jax import lax
from jax.experimental import pallas as pl
from jax.experimental.pallas import tpu as pltpu

F32 = jnp.float32
BF16 = jnp.bfloat16

D_MODEL = 1024
BATCH = 16
SEQ = 256
DEPTH = 2
DEC_BATCH = 4
DEC_SEQ = 1024
PAST_LEN = 256
GRID_W = 64
GRID_ROWS = DEC_SEQ // GRID_W
HEAD_DIM = 64
POOL_WIDTH = 256
H_B = 6
KV_B = 2
H_C = 6
WIN_R = 8
WIN_C = 16
ROPE_THETA = 10000.0
QB_W = H_B * HEAD_DIM
KB_W = KV_B * HEAD_DIM
C_W = H_C * HEAD_DIM
PROJ_WIDTH = 2048
N_GROUPS = 4
EXPERTS_PER_GROUP = 8
N_EXPERTS = 32
D_EXPERT = 256
ALPHA = (2 * DEPTH) ** 0.25
LN_EPS = 1e-6
RMS_EPS = 1e-6
NEG = -1e30
ATTN_SCALE = HEAD_DIM ** -0.5

T_CTX = BATCH * SEQ
T_LAT = DEC_BATCH * DEC_SEQ
T_ALL = T_CTX + T_LAT

LANES = 128
ROUTER_LANES = 128
EXPERT_LANE0 = N_GROUPS
MOD_ROWS = 8
CTX_MOD_ROW = DEC_BATCH

TT = 512
TP = 1024
TF = 1024
POOL_GAP = 16
N_STREAMS = 2
T_STREAM = T_ALL // N_STREAMS
TMX = 128
PLACE_UNROLL = 8
PAD_CHUNK = 8
SLOT_PITCH = 2 * T_STREAM
P_MAX = 2 * T_STREAM + N_EXPERTS * TMX
GROUP_ROWS = D_MODEL // LANES
XSTRIDE = TMX + 8
RMW_BATCH = 16
PLAN_N_TILES, PLAN_VALID_END, PLAN_PAD_END = 0, 1, 2
HALF = DEC_SEQ // 2
NAT_KEYS = 12 * GRID_W
VMEM_LIMIT = 56 * 1024 * 1024


def _dot(a, b):
    return jnp.dot(a, b, preferred_element_type=F32)


def _dot_nt(a, b):
    return lax.dot_general(a, b, (((1,), (1,)), ((), ())), preferred_element_type=F32)


def _split_bf16(x):
    hi = x.astype(BF16)
    lo = (x - hi.astype(F32)).astype(BF16)
    return hi, lo


def _layer_norm(y, g, b):
    mu = jnp.mean(y, axis=-1, keepdims=True)
    var = jnp.mean(jnp.square(y - mu), axis=-1, keepdims=True)
    return (y - mu) * lax.rsqrt(var + LN_EPS) * g + b


def _adaln_kernel(c_ref, w_ref, b_ref, o_ref):
    c = c_ref[...]
    s = (c * jax.nn.sigmoid(c)).astype(BF16)
    o_ref[0] = _dot(s, w_ref[0].astype(BF16)) + b_ref[0]


def _adaln(c8, w_ada, b_ada):
    tn = 1536
    n = w_ada.shape[-1]
    return pl.pallas_call(
        _adaln_kernel,
        grid=(DEPTH, n // tn),
        in_specs=[
            pl.BlockSpec((MOD_ROWS, D_MODEL), lambda l, j: (0, 0)),
            pl.BlockSpec((1, D_MODEL, tn), lambda l, j: (l, 0, j)),
            pl.BlockSpec((1, 1, tn), lambda l, j: (l, 0, j)),
        ],
        out_specs=pl.BlockSpec((1, MOD_ROWS, tn), lambda l, j: (l, 0, j)),
        out_shape=jax.ShapeDtypeStruct((DEPTH, MOD_ROWS, n), F32),
        compiler_params=pltpu.CompilerParams(
            dimension_semantics=("arbitrary", "arbitrary"), vmem_limit_bytes=VMEM_LIMIT),
        name="adaln",
    )(c8, w_ada, b_ada.reshape(DEPTH, 1, n))


def _mod_row_of_tile(i, tile):
    n_ctx = T_CTX // tile
    per_req = DEC_SEQ // tile
    return jnp.where(i < n_ctx, CTX_MOD_ROW, (i - n_ctx) // per_req)


def _mod_spec(layer, chunk, tile):
    return pl.BlockSpec(
        (1, 1, D_MODEL), lambda i: ((layer * MOD_ROWS + _mod_row_of_tile(i, tile)) * 6 + chunk, 0, 0))


def _layer_spec(layer, shape):
    return pl.BlockSpec((1,) + shape, lambda *_: (layer,) + (0,) * len(shape))


N_CTX_TILES = T_CTX // TT


def _split_row_specs(width):
    return (pl.BlockSpec((TT, width), lambda i: (jnp.minimum(i, N_CTX_TILES - 1), 0)),
            pl.BlockSpec((TT, width), lambda i: (jnp.maximum(i - N_CTX_TILES, 0), 0)))


def _split_rows(ctx_ref, lat_ref):
    return jnp.where(pl.program_id(0) < N_CTX_TILES, ctx_ref[...], lat_ref[...])


def _rms_norm_heads(x, ones_bd, w):
    hi, lo = _split_bf16(x * x)
    ssq = _dot(hi, ones_bd) + _dot(lo, ones_bd)
    return x * lax.rsqrt(ssq * (1.0 / HEAD_DIM) + RMS_EPS) * w


def _rope_cols(x, cos, sin, first16):
    cols = []
    for j in range(x.shape[1] // LANES):
        xc = x[:, j * LANES:(j + 1) * LANES]
        partner = jnp.where(first16, pltpu.roll(xc, LANES - 16, axis=1), pltpu.roll(xc, 16, axis=1))
        cols.append(xc * cos + partner * sin)
    return jnp.concatenate(cols, axis=1) if len(cols) > 1 else cols[0]


def _proj_kernel(xc_ref, xl_ref, sh_ref, sc_ref, w_ref, ones_ref, qn_ref, kn_ref, cos_ref, sin_ref,
                 a_ref, qb_ref, kb_ref, vb_ref, qc_ref, kc_ref, vc_ref, wbf_ref):
    @pl.when(pl.program_id(0) == 0)
    def _():
        wbf_ref[...] = w_ref[0].astype(BF16)

    h = _split_rows(xc_ref, xl_ref) * (1.0 + sc_ref[0]) + sh_ref[0]
    p = _dot(h.astype(BF16), wbf_ref[...])
    o = 0
    a_ref[...] = p[:, o:o + POOL_WIDTH]; o += POOL_WIDTH
    qb = p[:, o:o + QB_W]; o += QB_W
    kb = p[:, o:o + KB_W]; o += KB_W
    vb_ref[...] = p[:, o:o + KB_W]; o += KB_W
    qc_ref[...] = (p[:, o:o + C_W] * ATTN_SCALE).astype(BF16); o += C_W
    kc_ref[...] = p[:, o:o + C_W]; o += C_W
    vc_ref[...] = p[:, o:o + C_W]

    ones_bd = ones_ref[...]
    cos = cos_ref[...]
    sin = sin_ref[...]
    lane = lax.broadcasted_iota(jnp.int32, (1, LANES), 1)
    first16 = (lane & 31) < 16
    qb = _rope_cols(_rms_norm_heads(qb, ones_bd, qn_ref[...]), cos, sin, first16)
    kb = _rope_cols(_rms_norm_heads(kb, ones_bd[:KB_W, :KB_W], kn_ref[...]), cos, sin, first16)
    qb_ref[...] = (qb * ATTN_SCALE).astype(BF16)
    kb_ref[...] = kb


def _proj(layer, x_ctx, x_lat, mod, w_in, ones_bd, qn, kn, cos_t, sin_t):
    n_tiles = T_ALL // TT
    n_ctx = N_CTX_TILES
    per_req = DEC_SEQ // TT

    def rope_idx(i):
        return (jnp.where(i < n_ctx, per_req, (i - n_ctx) % per_req), 0)

    row = lambda w: pl.BlockSpec((TT, w), lambda i: (i, 0))
    const = lambda s: pl.BlockSpec(s, lambda i: (0,) * len(s))
    widths = (POOL_WIDTH, QB_W, KB_W, KB_W, C_W, C_W, C_W)
    dtypes = (F32, BF16, F32, F32, BF16, F32, F32)
    return pl.pallas_call(
        _proj_kernel,
        grid=(n_tiles,),
        in_specs=[*_split_row_specs(D_MODEL), _mod_spec(layer, 0, TT), _mod_spec(layer, 1, TT),
                  _layer_spec(layer, (D_MODEL, PROJ_WIDTH)),
                  const((QB_W, QB_W)), const((1, QB_W)), const((1, KB_W)),
                  pl.BlockSpec((TT, LANES), rope_idx), pl.BlockSpec((TT, LANES), rope_idx)],
        out_specs=[row(w) for w in widths],
        out_shape=[jax.ShapeDtypeStruct((T_ALL, w), d) for w, d in zip(widths, dtypes)],
        scratch_shapes=[pltpu.VMEM((D_MODEL, PROJ_WIDTH), BF16)],
        compiler_params=pltpu.CompilerParams(dimension_semantics=("arbitrary",), vmem_limit_bytes=VMEM_LIMIT),
        name="proj",
    )(x_ctx, x_lat, mod, mod, w_in, ones_bd, qn, kn, cos_t, sin_t)


def _pool_windows(p):
    n = p.shape[0]
    sh = lambda x, k: pltpu.roll(x, (n - k) % n, axis=0)
    w2 = p + sh(p, -1)
    w4 = sh(w2, -1) + sh(w2, 1)
    w8 = sh(w4, -2) + sh(w4, 2)
    w16 = sh(w8, -4) + sh(w8, 4)
    g = lax.broadcasted_iota(jnp.int32, p.shape, 1) >> 6
    return jnp.where(g == 0, w2, jnp.where(g == 1, w4, jnp.where(g == 2, w8, w16)))


def _pool_kernel(a_ref, w_ref, scale_ref, o_ref, pad_ref):
    a = a_ref[...]

    def mix(csum, seq_len):
        row = lax.broadcasted_iota(jnp.int32, a.shape, 0)
        g = lax.broadcasted_iota(jnp.int32, a.shape, 1) >> 6
        t = row & (seq_len - 1)
        half = jnp.where(g == 0, 1, jnp.where(g == 1, 2, jnp.where(g == 2, 4, 8)))
        cnt = jnp.minimum(t + half, seq_len) - jnp.maximum(t - half, 0)
        pooled = csum / cnt.astype(F32) - a
        y = _dot(pooled.astype(BF16), w_ref[...].astype(BF16)) * scale_ref[...]
        o_ref[...] = y.astype(BF16)

    def padded_sums(seq_len):
        pitch = seq_len + POOL_GAP
        n_req = TP // seq_len
        n = n_req * pitch
        pad_ref[...] = jnp.zeros_like(pad_ref)
        for r in range(n_req):
            pad_ref[r * pitch + POOL_GAP // 2:r * pitch + POOL_GAP // 2 + seq_len, :] = a[r * seq_len:(r + 1) * seq_len]
        sums = _pool_windows(pad_ref[0:n, :])
        return jnp.concatenate(
            [sums[r * pitch + POOL_GAP // 2:r * pitch + POOL_GAP // 2 + seq_len] for r in range(n_req)], axis=0)

    is_ctx = pl.program_id(0) < T_CTX // TP

    @pl.when(is_ctx)
    def _():
        mix(padded_sums(SEQ), SEQ)

    @pl.when(jnp.logical_not(is_ctx))
    def _():
        mix(padded_sums(DEC_SEQ), DEC_SEQ)


def _pool(a, w_bd, scale):
    return pl.pallas_call(
        _pool_kernel,
        grid=(T_ALL // TP,),
        in_specs=[pl.BlockSpec((TP, POOL_WIDTH), lambda i: (i, 0)),
                  pl.BlockSpec((POOL_WIDTH, POOL_WIDTH), lambda i: (0, 0)),
                  pl.BlockSpec((1, POOL_WIDTH), lambda i: (0, 0))],
        out_specs=pl.BlockSpec((TP, POOL_WIDTH), lambda i: (i, 0)),
        out_shape=jax.ShapeDtypeStruct((T_ALL, POOL_WIDTH), BF16),
        scratch_shapes=[pltpu.VMEM(((TP // SEQ) * (SEQ + POOL_GAP), POOL_WIDTH), F32)],
        compiler_params=pltpu.CompilerParams(dimension_semantics=("arbitrary",), vmem_limit_bytes=VMEM_LIMIT),
        name="pool",
    )(a, w_bd, scale)


def _softmax_pv(scores, values):
    m = scores[0].max(axis=-1, keepdims=True)
    for s in scores[1:]:
        m = jnp.maximum(m, s.max(axis=-1, keepdims=True))
    ps = [jnp.exp(s - m) for s in scores]
    l = ps[0].sum(axis=-1, keepdims=True)
    for p in ps[1:]:
        l = l + p.sum(axis=-1, keepdims=True)
    o = _dot(ps[0].astype(BF16), values[0])
    for p, v in zip(ps[1:], values[1:]):
        o = o + _dot(p.astype(BF16), v)
    return o * (1.0 / l)


def _lane_halves():
    lane = lax.broadcasted_iota(jnp.int32, (1, LANES), 1)
    return lane < HEAD_DIM, lane >= HEAD_DIM


def _keep(x, mask):
    return jnp.where(mask, x, 0.0).astype(BF16)


def _gqa_variants(x):
    lo, hi = _lane_halves()
    xs = pltpu.roll(x, HEAD_DIM, axis=1)
    nat_lo, nat_hi = _keep(x, lo), _keep(x, hi)
    sw_lo, sw_hi = _keep(xs, lo), _keep(xs, hi)
    return ((nat_lo, sw_hi), (nat_lo, nat_hi), (sw_lo, nat_hi))


def _mha_variants(x):
    lo, hi = _lane_halves()
    out = []
    for j in range(x.shape[1] // LANES):
        xc = x[:, j * LANES:(j + 1) * LANES]
        out.append((_keep(xc, lo), _keep(xc, hi)))
    return tuple(out)


def _attend_cols(q, k_vars, v_vars, extra_k=None, extra_v=None, bias=None):
    cols = []
    for j in range(q.shape[1] // LANES):
        qc = q[:, j * LANES:(j + 1) * LANES]
        o = None
        for hh in range(2):
            s = _dot_nt(qc, k_vars[j][hh])
            if bias is not None:
                s = s + bias[j][hh]
            scores, values = [s], [v_vars[j][hh]]
            if extra_k is not None:
                scores.append(_dot_nt(qc, extra_k[j][hh]))
                values.append(extra_v[j][hh])
            oh = _softmax_pv(scores, values)
            o = oh if o is None else o + oh
        cols.append(o)
    return jnp.concatenate(cols, axis=1)


def _attn_ctx_kernel(layer_slot, qb_ref, kb_ref, vb_ref, qc_ref, kc_ref, vc_ref, *refs):
    yb_ref, yc_ref, kbt_ref, vbt_ref, kct_ref, vct_ref = refs[-6:]
    kb, vb, kc, vc = kb_ref[...], vb_ref[...], kc_ref[...], vc_ref[...]
    yb = _attend_cols(qb_ref[...], _gqa_variants(kb), _gqa_variants(vb))
    yb_ref[...] = yb.astype(BF16)
    yc = _attend_cols(qc_ref[...], _mha_variants(kc), _mha_variants(vc))
    yc_ref[...] = yc.astype(BF16)
    for ref, val in ((kbt_ref, kb), (vbt_ref, vb), (kct_ref, kc), (vct_ref, vc)):
        for slot in range(ref.shape[1]):
            ref[0, slot] = val.T if slot == layer_slot else jnp.zeros(ref.shape[2:], F32)


CACHE_WIDTHS = (KB_W, KB_W, C_W, C_W)


def _attn_ctx(layer, qb, kb, vb, qc, kc, vc, caches):
    row = lambda w: pl.BlockSpec((SEQ, w), lambda i: (i, 0))
    if caches:
        cache = lambda w: pl.BlockSpec((1, 1, w, SEQ), lambda i: (i, layer, 0, 0))
        layer_slot = 0
    else:
        cache = lambda w: pl.BlockSpec((1, DEPTH, w, SEQ), lambda i: (i, 0, 0, 0))
        layer_slot = layer
    return pl.pallas_call(
        functools.partial(_attn_ctx_kernel, layer_slot),
        grid=(BATCH,),
        in_specs=[row(QB_W), row(KB_W), row(KB_W), row(C_W), row(C_W), row(C_W)]
                 + [pl.BlockSpec(memory_space=pl.ANY)] * len(caches),
        out_specs=[row(QB_W), row(C_W)] + [cache(w) for w in CACHE_WIDTHS],
        out_shape=[jax.ShapeDtypeStruct((T_CTX, QB_W), BF16), jax.ShapeDtypeStruct((T_CTX, C_W), BF16)]
                  + [jax.ShapeDtypeStruct((BATCH, DEPTH, w, SEQ), F32) for w in CACHE_WIDTHS],
        input_output_aliases={6 + k: 2 + k for k in range(len(caches))},
        compiler_params=pltpu.CompilerParams(dimension_semantics=("arbitrary",), vmem_limit_bytes=VMEM_LIMIT),
        name="attn_ctx",
    )(qb, kb, vb, qc, kc, vc, *caches)


def _attn_latb_kernel(q_ref, k_ref, v_ref, ck_ref, cv_ref, y_ref):
    y = _attend_cols(q_ref[...], _gqa_variants(ck_ref[0, 0]), _gqa_variants(cv_ref[0, 0]),
                     extra_k=_gqa_variants(k_ref[...]), extra_v=_gqa_variants(v_ref[...]))
    y_ref[...] = y.astype(BF16)


def _attn_latb(layer, qb, kb, vb, cache_k, cache_v):
    ctx_h = T_CTX // HALF
    ctx_r = T_CTX // DEC_SEQ
    cache = pl.BlockSpec((1, 1, PAST_LEN, KB_W), lambda b, s: (b, layer, 0, 0))
    own = pl.BlockSpec((DEC_SEQ, KB_W), lambda b, s: (ctx_r + b, 0))
    return pl.pallas_call(
        _attn_latb_kernel,
        grid=(DEC_BATCH, DEC_SEQ // HALF),
        in_specs=[pl.BlockSpec((HALF, QB_W), lambda b, s: (ctx_h + 2 * b + s, 0)), own, own, cache, cache],
        out_specs=pl.BlockSpec((HALF, QB_W), lambda b, s: (2 * b + s, 0)),
        out_shape=jax.ShapeDtypeStruct((T_LAT, QB_W), BF16),
        compiler_params=pltpu.CompilerParams(
            dimension_semantics=("arbitrary", "arbitrary"), vmem_limit_bytes=VMEM_LIMIT),
        name="attn_latb",
    )(qb, kb, vb, cache_k, cache_v)


def _natten_window(s, i):
    r = (HALF // GRID_W) * s + i
    rs = min(max(r - WIN_R // 2, 0), GRID_ROWS - WIN_R)
    return r, rs


def _natten_kernel(q_ref, k_ref, v_ref, ck_ref, cv_ref, blk_ref, y_ref, bias_ref):
    s = pl.program_id(1)

    for sv in range(2):
        @pl.when((pl.program_id(2) == 0) & (s == sv))
        def _():
            lo_half, _ = _lane_halves()
            masked = jnp.full((GRID_W, LANES), NEG, F32)
            for hh in range(2):
                for i in range(HALF // GRID_W):
                    r, rs = _natten_window(sv, i)
                    for jp in range(NAT_KEYS // LANES):
                        pair = []
                        for j in (2 * jp, 2 * jp + 1):
                            rk = 4 * sv + j
                            pair.append(blk_ref[0, hh, rk - r + WIN_R - 1] if rs <= rk < rs + WIN_R else masked)
                        bias_ref[hh, i * GRID_W:(i + 1) * GRID_W, jp * LANES:(jp + 1) * LANES] = (
                            jnp.where(lo_half, pair[0], pair[1]))

    start = pl.multiple_of(s * (4 * GRID_W), 4 * GRID_W)
    k = _mha_variants(k_ref[pl.ds(start, NAT_KEYS), :])
    v = _mha_variants(v_ref[pl.ds(start, NAT_KEYS), :])
    ck = _mha_variants(ck_ref[0, 0])
    cv = _mha_variants(cv_ref[0, 0])
    bias = ((bias_ref[0], bias_ref[1]),)
    y = _attend_cols(q_ref[...], k, v, extra_k=ck, extra_v=cv, bias=bias)
    y_ref[...] = y.astype(BF16)


def _natten(layer, qc, kc, vc, cache_k, cache_v, blocks):
    ctx_h = T_CTX // HALF
    ctx_r = T_CTX // DEC_SEQ
    cache = pl.BlockSpec((1, 1, PAST_LEN, LANES), lambda j, s, b: (b, layer, 0, j))
    own = pl.BlockSpec((DEC_SEQ, LANES), lambda j, s, b: (ctx_r + b, j))
    return pl.pallas_call(
        _natten_kernel,
        grid=(C_W // LANES, DEC_SEQ // HALF, DEC_BATCH),
        in_specs=[pl.BlockSpec((HALF, LANES), lambda j, s, b: (ctx_h + 2 * b + s, j)), own, own, cache, cache,
                  pl.BlockSpec((1, 2, 2 * WIN_R - 1, GRID_W, LANES),
                               lambda j, s, b: (layer * (C_W // LANES) + j, 0, 0, 0, 0))],
        out_specs=pl.BlockSpec((HALF, LANES), lambda j, s, b: (2 * b + s, j)),
        out_shape=jax.ShapeDtypeStruct((T_LAT, C_W), BF16),
        scratch_shapes=[pltpu.VMEM((2, HALF, NAT_KEYS), F32)],
        compiler_params=pltpu.CompilerParams(
            dimension_semantics=("arbitrary", "arbitrary", "arbitrary"), vmem_limit_bytes=VMEM_LIMIT),
        name="natten",
    )(qc, kc, vc, cache_k, cache_v, blocks)


def _natten_blocks(rpb):
    qcol = np.arange(GRID_W)[:, None]
    kcol = (np.arange(LANES) % GRID_W)[None, :]
    cs = np.clip(qcol - WIN_C // 2, 0, GRID_W - WIN_C)
    col_ok = (kcol >= cs) & (kcol < cs + WIN_C)
    dc = np.clip(kcol - qcol, -(WIN_C - 1), WIN_C - 1) + (WIN_C - 1)
    sel_c = (dc[:, :, None] == np.arange(2 * WIN_C - 1)).astype(np.float32)
    blk = jnp.einsum("lhdm,qkm->lhdqk", rpb, jnp.asarray(sel_c), precision=lax.Precision.HIGHEST)
    blk = jnp.where(jnp.asarray(col_ok), blk, NEG)
    return blk.reshape(DEPTH * (H_C // 2), 2, 2 * WIN_R - 1, GRID_W, LANES)


def _route(lg):
    lane = lax.broadcasted_iota(jnp.int32, lg.shape, 1).astype(F32)
    low = jnp.float32(-3.0e38)
    far = jnp.float32(ROUTER_LANES)
    is_g = lane < N_GROUPS
    gmax = jnp.where(is_g, lg, low).max(axis=-1, keepdims=True)
    gsel = jnp.where(is_g & (lg == gmax), lane, far).min(axis=-1, keepdims=True)
    pg_sel = 1.0 / jnp.where(is_g, jnp.exp(lg - gmax), 0.0).sum(axis=-1, keepdims=True)
    e0 = EXPERT_LANE0 + EXPERTS_PER_GROUP * gsel
    in_g = (lane >= e0) & (lane < e0 + EXPERTS_PER_GROUP)
    m1 = jnp.where(in_g, lg, low).max(axis=-1, keepdims=True)
    i1 = jnp.where(in_g & (lg == m1), lane, far).min(axis=-1, keepdims=True)
    rest = in_g & (lane != i1)
    m2 = jnp.where(rest, lg, low).max(axis=-1, keepdims=True)
    i2 = jnp.where(rest & (lg == m2), lane, far).min(axis=-1, keepdims=True)
    t = jnp.exp(m2 - m1)
    ssum = 1.0 + t
    w1 = (1.0 / ssum) * pg_sel
    w2 = (t / ssum) * pg_sel
    return i1 - EXPERT_LANE0, i2 - EXPERT_LANE0, w1, w2


def _rows_to_groups(x, stage_ref, out_ref, n_rows):
    stride = n_rows + 8
    for c in range(GROUP_ROWS):
        stage_ref[c * stride:c * stride + n_rows, :] = x[:, c * LANES:(c + 1) * LANES]
    for j in range(n_rows):
        out_ref[GROUP_ROWS * j:GROUP_ROWS * (j + 1), :] = stage_ref[pl.ds(j, GROUP_ROWS, stride=stride), :]


def _tail_kernel(xc_ref, xl_ref, ya_ref, ybc_ref, ybl_ref, ycc_ref, ycl_ref, w_ref, g1_ref, sh2_ref, sc2_ref,
                 lg_ref, lb_ref, wr_ref, br_ref, x1_ref, h2g_ref, ids_ref, wts_ref, wbf_ref, stage_ref):
    @pl.when(pl.program_id(0) == 0)
    def _():
        wbf_ref[...] = w_ref[0].astype(BF16)

    o = (_dot(ya_ref[...], wbf_ref[0:POOL_WIDTH, :])
         + _dot(_split_rows(ybc_ref, ybl_ref), wbf_ref[POOL_WIDTH:POOL_WIDTH + QB_W, :])
         + _dot(_split_rows(ycc_ref, ycl_ref), wbf_ref[POOL_WIDTH + QB_W:, :]))
    x1 = _layer_norm(ALPHA * _split_rows(xc_ref, xl_ref) + g1_ref[0] * o, lg_ref[0], lb_ref[0])
    x1_ref[...] = x1
    h2 = x1 * (1.0 + sc2_ref[0]) + sh2_ref[0]
    hh, hl = _split_bf16(h2)
    wh, wl = _split_bf16(wr_ref[0])
    lg = _dot(hh, wh) + _dot(hl, wh) + _dot(hh, wl) + br_ref[0]
    i1, i2, w1, w2 = _route(lg)
    lane = lax.broadcasted_iota(jnp.int32, lg.shape, 1)
    ids = jnp.where(lane == 0, i1, jnp.where(lane == 1, i2, 0.0))
    ids_ref[0] = ids.T[0:8, :].astype(jnp.int32)
    wts = jnp.where(lane == 0, w1, jnp.where(lane == 1, w2, 0.0))
    wts_ref[0] = wts.T[0:8, :]
    _rows_to_groups(h2, stage_ref, h2g_ref, TT)


def _tail(layer, x_ctx, x_lat, ya, yb_ctx, yb_lat, yc_ctx, yc_lat, w_out, mod, ln_g, ln_b, wr, br):
    row = lambda w: pl.BlockSpec((TT, w), lambda i: (i, 0))
    return pl.pallas_call(
        _tail_kernel,
        grid=(T_ALL // TT,),
        in_specs=[*_split_row_specs(D_MODEL), row(POOL_WIDTH), *_split_row_specs(QB_W), *_split_row_specs(C_W),
                  _layer_spec(layer, (D_MODEL, D_MODEL)),
                  _mod_spec(layer, 2, TT), _mod_spec(layer, 3, TT), _mod_spec(layer, 4, TT),
                  _layer_spec(layer, (1, D_MODEL)), _layer_spec(layer, (1, D_MODEL)),
                  _layer_spec(layer, (D_MODEL, ROUTER_LANES)), _layer_spec(layer, (1, ROUTER_LANES))],
        out_specs=[row(D_MODEL), pl.BlockSpec((TT * GROUP_ROWS, LANES), lambda i: (i, 0)),
                   pl.BlockSpec((1, 8, TT), lambda i: (i, 0, 0)), pl.BlockSpec((1, 8, TT), lambda i: (i, 0, 0))],
        out_shape=[jax.ShapeDtypeStruct((T_ALL, D_MODEL), F32),
                   jax.ShapeDtypeStruct((T_ALL * GROUP_ROWS, LANES), F32),
                   jax.ShapeDtypeStruct((T_ALL // TT, 8, TT), jnp.int32),
                   jax.ShapeDtypeStruct((T_ALL // TT, 8, TT), F32)],
        scratch_shapes=[pltpu.VMEM((D_MODEL, D_MODEL), BF16),
                        pltpu.VMEM((GROUP_ROWS * (TT + 8), LANES), F32)],
        compiler_params=pltpu.CompilerParams(dimension_semantics=("arbitrary",), vmem_limit_bytes=VMEM_LIMIT),
        name="tail",
    )(x_ctx, x_lat, ya, yb_ctx, yb_lat, yc_ctx, yc_lat, w_out, mod, mod, mod, ln_g, ln_b, wr, br)


def _plan_kernel(eid_ref, pos_ref, tile_ref):
    e = eid_ref[0]
    rows = e.shape[0]
    r = lax.broadcasted_iota(jnp.int32, (LANES, LANES), 0)
    c = lax.broadcasted_iota(jnp.int32, (LANES, LANES), 1)
    upper = (r <= c).astype(BF16)
    ones = jnp.ones((LANES, LANES), BF16)
    lower = (lax.broadcasted_iota(jnp.int32, (rows, rows), 1)
             < lax.broadcasted_iota(jnp.int32, (rows, rows), 0)).astype(BF16)
    lane = lax.broadcasted_iota(jnp.int32, (1, LANES), 1)
    masks = [e == ex for ex in range(N_EXPERTS)]
    stack = jnp.concatenate([m.astype(BF16) for m in masks], axis=0)
    incl_all = _dot(stack, upper)
    tot_all = _dot(stack, ones)
    pos = jnp.zeros(e.shape, F32)
    base = jnp.zeros((1, LANES), F32)
    valid_end = jnp.zeros((1, LANES), F32)
    pad_end = jnp.zeros((1, LANES), F32)
    for ex, m in enumerate(masks):
        incl = incl_all[ex * rows:(ex + 1) * rows]
        row_tot = tot_all[ex * rows:(ex + 1) * rows]
        row_off = _dot(lower, row_tot.astype(BF16))
        cnt = jnp.sum(row_tot, axis=0, keepdims=True)
        pos = jnp.where(m, base + row_off + incl - 1.0, pos)
        valid_end = jnp.where(lane == ex, base + cnt, valid_end)
        base = base + jnp.ceil(cnt * (1.0 / TMX)) * TMX
        pad_end = jnp.where(lane == ex, base, pad_end)
    pos_ref[0] = pos.astype(jnp.int32)
    sub = lax.broadcasted_iota(jnp.int32, (8, LANES), 0)
    rows8 = lambda v: jnp.broadcast_to(v, (8, LANES))
    table = jnp.where(sub == PLAN_N_TILES, rows8(base * (1.0 / TMX)),
                      jnp.where(sub == PLAN_VALID_END, rows8(valid_end),
                                jnp.where(sub == PLAN_PAD_END, rows8(pad_end), 0.0)))
    tile_ref[0] = table.astype(jnp.int32)


def _plan(eid):
    rows = 2 * T_STREAM // LANES
    return pl.pallas_call(
        _plan_kernel,
        grid=(N_STREAMS,),
        in_specs=[pl.BlockSpec((1, rows, LANES), lambda s: (s, 0, 0))],
        out_specs=[pl.BlockSpec((1, rows, LANES), lambda s: (s, 0, 0)), pl.BlockSpec((1, 8, LANES), lambda s: (s, 0, 0))],
        out_shape=[jax.ShapeDtypeStruct((N_STREAMS, rows, LANES), jnp.int32),
                   jax.ShapeDtypeStruct((N_STREAMS, 8, LANES), jnp.int32)],
        compiler_params=pltpu.CompilerParams(dimension_semantics=("arbitrary",), vmem_limit_bytes=VMEM_LIMIT),
        name="plan",
    )(eid)


def _moe_kernel(pos_ref, wpair_ref, plan_ref, h2g_hbm, wg_ref, wu_ref, wd_ref, y_hbm,
                h2v, yv, off, xg0, xg1, yg0, yg1, wgb, wub, wdb, sem):
    s = pl.program_id(0)
    e = pl.program_id(1)
    n_pairs = 2 * T_STREAM
    spare_off = T_STREAM * GROUP_ROWS
    plan = lambda row, idx: plan_ref[(s * 8 + row) * LANES + idx]
    n_tiles = plan(PLAN_N_TILES, 0)
    last = n_tiles - 1

    xgs = (xg0, xg1)
    ygs = (yg0, yg1)

    def row_offset(pair):
        return pl.multiple_of((pair & (SLOT_PITCH - 1)) << 3, GROUP_ROWS)

    def gather(tile, dst):
        base = tile * TMX
        for j in range(TMX):
            dst[pl.ds(j, GROUP_ROWS, stride=XSTRIDE), :] = h2v[pl.ds(row_offset(off[base + j]), GROUP_ROWS), :]

    def h2_copy():
        return pltpu.make_async_copy(
            h2g_hbm.at[pl.ds(pl.multiple_of(s * (T_STREAM * GROUP_ROWS), 8), T_STREAM * GROUP_ROWS)],
            h2v.at[pl.ds(0, T_STREAM * GROUP_ROWS)], sem.at[0])

    def y_copy(stream):
        return pltpu.make_async_copy(yv.at[pl.ds(0, T_STREAM * GROUP_ROWS)], y_hbm.at[stream], sem.at[1])

    @pl.when(e == 0)
    def _():
        h2_copy().start()
        h2v[spare_off:spare_off + GROUP_ROWS, :] = jnp.zeros((GROUP_ROWS, LANES), F32)

        def pad_expert(ex, carry):
            lo = plan(PLAN_VALID_END, ex)

            def pad_chunk(k, c):
                for u in range(PAD_CHUNK):
                    off[lo + k * PAD_CHUNK + u] = T_STREAM
                return c
            n_chunks = (plan(PLAN_PAD_END, ex) - lo + (PAD_CHUNK - 1)) // PAD_CHUNK
            return lax.fori_loop(0, n_chunks, pad_chunk, carry)
        lax.fori_loop(0, N_EXPERTS, pad_expert, 0)

        def place(i, carry):
            for u in range(PLACE_UNROLL):
                tok = i * PLACE_UNROLL + u
                for slot in range(2):
                    off[pos_ref[s * n_pairs + slot * T_STREAM + tok]] = slot * SLOT_PITCH + tok
            return carry
        lax.fori_loop(0, T_STREAM // PLACE_UNROLL, place, 0)

        @pl.when(s > 0)
        def _():
            y_copy(s - 1).wait()
        yv[...] = jnp.zeros_like(yv)
        yg1[...] = jnp.zeros_like(yg1)
        h2_copy().wait()
        gather(0, xg0)

    t_lo = jnp.where(e == 0, 0, plan(PLAN_PAD_END, jnp.maximum(e - 1, 0))) // TMX
    t_hi = plan(PLAN_PAD_END, e) // TMX

    @pl.when(t_hi > t_lo)
    def _():
        wgb[...] = wg_ref[0, 0].astype(BF16)
        wub[...] = wu_ref[0, 0].astype(BF16)
        wdb[...] = wd_ref[0, 0].astype(BF16)

    def add_tile(tile, done):
        base = tile * TMX
        for j0 in range(0, TMX, RMW_BATCH):
            updates = []
            for j in range(j0, j0 + RMW_BATCH):
                pair = off[base + j]
                dst = row_offset(pair)
                gate = wpair_ref[s * (2 * SLOT_PITCH) + pair]
                row = done[pl.ds(j, GROUP_ROWS, stride=XSTRIDE), :]
                updates.append((dst, yv[pl.ds(dst, GROUP_ROWS), :] + gate * row))
            for dst, val in updates:
                yv[pl.ds(dst, GROUP_ROWS), :] = val

    def step(t, cur, nxt, out, done):
        gather(jnp.minimum(t + 1, last), nxt)
        x = jnp.concatenate([cur[c * XSTRIDE:c * XSTRIDE + TMX, :] for c in range(GROUP_ROWS)], axis=1).astype(BF16)
        hg = _dot(x, wgb[...])
        hu = _dot(x, wub[...])
        act = hg * jax.nn.sigmoid(hg) * hu
        y = _dot(act.astype(BF16), wdb[...])
        for c in range(GROUP_ROWS):
            out[c * XSTRIDE:c * XSTRIDE + TMX, :] = y[:, c * LANES:(c + 1) * LANES]
        add_tile(jnp.maximum(t - 1, 0), done)

    def tile_body(t, carry):
        for par in range(2):
            @pl.when((t & 1) == par)
            def _():
                step(t, xgs[par], xgs[1 - par], ygs[par], ygs[1 - par])
        return carry
    lax.fori_loop(t_lo, t_hi, tile_body, 0)

    @pl.when(e == N_EXPERTS - 1)
    def _():
        for par in range(2):
            @pl.when((last & 1) == par)
            def _():
                add_tile(last, ygs[par])
        y_copy(s).start()

        @pl.when(s == N_STREAMS - 1)
        def _():
            y_copy(s).wait()


def _moe(layer, pos, wpair, plan, h2g, w_gate, w_up, w_down):
    wspec = lambda shape: pl.BlockSpec((1,) + shape, lambda s, e, pos, wpair, plan: (layer, e, 0, 0))
    return pl.pallas_call(
        _moe_kernel,
        grid_spec=pltpu.PrefetchScalarGridSpec(
            num_scalar_prefetch=3,
            grid=(N_STREAMS, N_EXPERTS),
            in_specs=[pl.BlockSpec(memory_space=pl.ANY),
                      wspec((1, D_MODEL, D_EXPERT)), wspec((1, D_MODEL, D_EXPERT)), wspec((1, D_EXPERT, D_MODEL))],
            out_specs=pl.BlockSpec(memory_space=pl.ANY),
            scratch_shapes=[pltpu.VMEM(((T_STREAM + 1) * GROUP_ROWS, LANES), F32),
                            pltpu.VMEM(((T_STREAM + 1) * GROUP_ROWS, LANES), F32),
                            pltpu.SMEM((P_MAX + PAD_CHUNK,), jnp.int32),
                            pltpu.VMEM((GROUP_ROWS * XSTRIDE, LANES), F32),
                            pltpu.VMEM((GROUP_ROWS * XSTRIDE, LANES), F32),
                            pltpu.VMEM((GROUP_ROWS * XSTRIDE, LANES), F32),
                            pltpu.VMEM((GROUP_ROWS * XSTRIDE, LANES), F32),
                            pltpu.VMEM((D_MODEL, D_EXPERT), BF16), pltpu.VMEM((D_MODEL, D_EXPERT), BF16),
                            pltpu.VMEM((D_EXPERT, D_MODEL), BF16),
                            pltpu.SemaphoreType.DMA((2,))]),
        out_shape=jax.ShapeDtypeStruct((N_STREAMS, T_STREAM * GROUP_ROWS, LANES), F32),
        compiler_params=pltpu.CompilerParams(
            dimension_semantics=("arbitrary", "arbitrary"), vmem_limit_bytes=VMEM_LIMIT),
        name="moe",
    )(pos, wpair, plan, h2g, w_gate, w_up, w_down)


def _groups_to_rows(src_ref, stage_ref, n_rows):
    stride = n_rows + 8
    for j in range(n_rows):
        stage_ref[pl.ds(j, GROUP_ROWS, stride=stride), :] = src_ref[GROUP_ROWS * j:GROUP_ROWS * (j + 1), :]
    return jnp.concatenate([stage_ref[c * stride:c * stride + n_rows, :] for c in range(GROUP_ROWS)], axis=1)


def _final_kernel(x1_ref, y_ref, g2_ref, lg_ref, lb_ref, o_ref, stage_ref):
    moe = _groups_to_rows(y_ref, stage_ref, TF)
    o_ref[...] = _layer_norm(ALPHA * x1_ref[...] + g2_ref[0] * moe, lg_ref[0], lb_ref[0])


def _final(layer, stream, x1, y, mod, ln_g, ln_b):
    first = stream * (T_STREAM // TF)
    g2 = pl.BlockSpec(
        (1, 1, D_MODEL), lambda i: ((layer * MOD_ROWS + _mod_row_of_tile(first + i, TF)) * 6 + 5, 0, 0))
    return pl.pallas_call(
        _final_kernel,
        grid=(T_STREAM // TF,),
        in_specs=[pl.BlockSpec((TF, D_MODEL), lambda i: (first + i, 0)),
                  pl.BlockSpec((TF * GROUP_ROWS, LANES), lambda i: (first + i, 0)),
                  g2, _layer_spec(layer, (1, D_MODEL)), _layer_spec(layer, (1, D_MODEL))],
        out_specs=pl.BlockSpec((TF, D_MODEL), lambda i: (i, 0)),
        out_shape=jax.ShapeDtypeStruct((T_STREAM, D_MODEL), F32),
        scratch_shapes=[pltpu.VMEM((GROUP_ROWS * (TF + 8), LANES), F32)],
        compiler_params=pltpu.CompilerParams(dimension_semantics=("arbitrary",), vmem_limit_bytes=VMEM_LIMIT),
        name="final",
    )(x1, y, mod, ln_g, ln_b)


def _rope_tables():
    t = np.arange(DEC_SEQ)
    pos = np.stack([t // GRID_W, t % GRID_W], axis=1).astype(np.float32)
    nf = HEAD_DIM // 4
    inv = jnp.asarray(ROPE_THETA, F32) ** (-jnp.arange(nf, dtype=F32) / nf)
    d = np.arange(LANES) % HEAD_DIM
    which = d // (HEAD_DIM // 2)
    ang = jnp.asarray(pos)[:, which] * inv[d % nf][None, :]
    sign = np.where((d % 32) < 16, -1.0, 1.0).astype(np.float32)
    cos = jnp.concatenate([jnp.cos(ang), jnp.ones((TT, LANES), F32)], axis=0)
    sin = jnp.concatenate([jnp.sin(ang) * sign[None, :], jnp.zeros((TT, LANES), F32)], axis=0)
    return cos, sin


def _block_ones():
    h = np.arange(QB_W) // HEAD_DIM
    return jnp.asarray((h[:, None] == h[None, :]).astype(np.float32), dtype=BF16)


def _pool_block_diag(pool_w):
    out = jnp.zeros((POOL_WIDTH, POOL_WIDTH), F32)
    for g in range(4):
        out = out.at[64 * g:64 * g + 64, 64 * g:64 * g + 64].set(pool_w[g])
    return out


def kernel(x_prompt, x_sample, cache_b_k, cache_b_v, cache_c_k, cache_c_v, c, c_ctx, w_ada, b_ada, w_in, w_out,
           pool_w, pool_scale, q_norm, k_norm, rpb, ln1_g, ln1_b, ln2_g, ln2_b, router_g, router_g_b, router_e,
           router_e_b, w_gate, w_up, w_down):
    x_ctx = x_prompt.reshape(T_CTX, D_MODEL)
    x_lat = x_sample.reshape(T_LAT, D_MODEL)
    c8 = jnp.concatenate([c, c_ctx[None], jnp.zeros((MOD_ROWS - DEC_BATCH - 1, D_MODEL), F32)], axis=0)
    mod = _adaln(c8, w_ada, b_ada).reshape(DEPTH * MOD_ROWS * 6, 1, D_MODEL)

    cos_t, sin_t = _rope_tables()
    ones_bd = _block_ones()
    cbk = cache_b_k.reshape(DEC_BATCH, DEPTH, PAST_LEN, KB_W)
    cbv = cache_b_v.reshape(DEC_BATCH, DEPTH, PAST_LEN, KB_W)
    cck = cache_c_k.reshape(DEC_BATCH, DEPTH, PAST_LEN, C_W)
    ccv = cache_c_v.reshape(DEC_BATCH, DEPTH, PAST_LEN, C_W)
    pad = jnp.zeros((DEPTH, D_MODEL, ROUTER_LANES - N_GROUPS - N_EXPERTS), F32)
    wr = jnp.concatenate([router_g, router_e, pad], axis=2)
    br = jnp.concatenate([router_g_b, router_e_b, pad[:, 0]], axis=1)[:, None]
    per_layer = lambda p: p[:, None]
    bias_blocks = _natten_blocks(rpb)

    caches = ()
    for l in range(DEPTH):
        qn = jnp.tile(q_norm[l], H_B)[None]
        kn = jnp.tile(k_norm[l], KV_B)[None]
        a, qb, kb, vb, qc, kc, vc = _proj(l, x_ctx, x_lat, mod, w_in, ones_bd, qn, kn, cos_t, sin_t)
        ya = _pool(a, _pool_block_diag(pool_w[l]), pool_scale[l][None])
        yb_ctx, yc_ctx, *caches = _attn_ctx(l, qb, kb, vb, qc, kc, vc, caches)
        yb_lat = _attn_latb(l, qb, kb, vb, cbk, cbv)
        yc_lat = _natten(l, qc, kc, vc, cck, ccv, bias_blocks)
        x1, h2g, ids, wts = _tail(l, x_ctx, x_lat, ya, yb_ctx, yb_lat, yc_ctx, yc_lat, w_out, mod,
                                  per_layer(ln1_g), per_layer(ln1_b), wr, br)
        pairs = lambda a: a.reshape(N_STREAMS, T_STREAM // TT, 8, TT)[:, :, :2, :].transpose(0, 2, 1, 3)
        pos, tiles = _plan(pairs(ids).reshape(N_STREAMS, 2 * T_STREAM // LANES, LANES))
        gates = jnp.pad(pairs(wts).reshape(N_STREAMS, 2, T_STREAM), ((0, 0), (0, 0), (0, SLOT_PITCH - T_STREAM)))
        y = _moe(l, pos.reshape(-1), gates.reshape(-1), tiles.reshape(-1), h2g, w_gate, w_up, w_down)
        y = y.reshape(T_ALL * GROUP_ROWS, LANES)
        x_ctx = _final(l, 0, x1, y, mod, per_layer(ln2_g), per_layer(ln2_b))
        x_lat = _final(l, 1, x1, y, mod, per_layer(ln2_g), per_layer(ln2_b))

    y_prompt = x_ctx.reshape(BATCH, SEQ, D_MODEL)
    y_sample = x_lat.reshape(DEC_BATCH, DEC_SEQ, D_MODEL)
    heads_last = lambda t, h: t.reshape(BATCH, DEPTH, h, HEAD_DIM, SEQ).transpose(0, 1, 4, 2, 3)
    new_bk, new_bv, new_ck, new_cv = caches
    return (y_prompt, y_sample, heads_last(new_bk, KV_B), heads_last(new_bv, KV_B),
            heads_last(new_ck, H_C), heads_last(new_cv, H_C))
```

```python
import functools

import numpy as np
import jax
import jax.numpy as jnp
from jax import lax
from jax.experimental import pallas as pl
from jax.experimental.pallas import tpu as pltpu

F32 = jnp.float32
BF16 = jnp.bfloat16

D_MODEL = 1024
BATCH = 16
SEQ = 256
DEPTH = 2
DEC_BATCH = 4
DEC_SEQ = 1024
PAST_LEN = 256
GRID_W = 64
GRID_ROWS = DEC_SEQ // GRID_W
HEAD_DIM = 64
POOL_WIDTH = 256
H_B = 6
KV_B = 2
H_C = 6
WIN_R = 8
WIN_C = 16
ROPE_THETA = 10000.0
QB_W = H_B * HEAD_DIM
KB_W = KV_B * HEAD_DIM
C_W = H_C * HEAD_DIM
PROJ_WIDTH = 2048
N_GROUPS = 4
EXPERTS_PER_GROUP = 8
N_EXPERTS = 32
D_EXPERT = 256
ALPHA = (2 * DEPTH) ** 0.25
LN_EPS = 1e-6
RMS_EPS = 1e-6
NEG = -1e30
ATTN_SCALE = HEAD_DIM ** -0.5

T_CTX = BATCH * SEQ
T_LAT = DEC_BATCH * DEC_SEQ
T_ALL = T_CTX + T_LAT

LANES = 128
ROUTER_LANES = 128
EXPERT_LANE0 = N_GROUPS
MOD_ROWS = 8
CTX_MOD_ROW = DEC_BATCH

TT = 512
TP = 1024
TF = 1024
POOL_GAP = 16
N_STREAMS = 2
T_STREAM = T_ALL // N_STREAMS
TMX = 128
PLACE_UNROLL = 8
PAD_CHUNK = 8
SLOT_PITCH = 2 * T_STREAM
P_MAX = 2 * T_STREAM + N_EXPERTS * TMX
GROUP_ROWS = D_MODEL // LANES
XSTRIDE = TMX + 8
RMW_BATCH = 16
PLAN_N_TILES, PLAN_VALID_END, PLAN_PAD_END = 0, 1, 2
HALF = DEC_SEQ // 2
NAT_KEYS = 12 * GRID_W
VMEM_LIMIT = 56 * 1024 * 1024


def _dot(a, b):
    return jnp.dot(a, b, preferred_element_type=F32)


def _dot_nt(a, b):
    return lax.dot_general(a, b, (((1,), (1,)), ((), ())), preferred_element_type=F32)


def _split_bf16(x):
    hi = x.astype(BF16)
    lo = (x - hi.astype(F32)).astype(BF16)
    return hi, lo


def _layer_norm(y, g, b):
    mu = jnp.mean(y, axis=-1, keepdims=True)
    var = jnp.mean(jnp.square(y - mu), axis=-1, keepdims=True)
    return (y - mu) * lax.rsqrt(var + LN_EPS) * g + b


def _adaln_kernel(c_ref, w_ref, b_ref, o_ref):
    c = c_ref[...]
    s = (c * jax.nn.sigmoid(c)).astype(BF16)
    o_ref[0] = _dot(s, w_ref[0].astype(BF16)) + b_ref[0]


def _adaln(c8, w_ada, b_ada):
    tn = 3072
    n = w_ada.shape[-1]
    return pl.pallas_call(
        _adaln_kernel,
        grid=(DEPTH, n // tn),
        in_specs=[
            pl.BlockSpec((MOD_ROWS, D_MODEL), lambda l, j: (0, 0)),
            pl.BlockSpec((1, D_MODEL, tn), lambda l, j: (l, 0, j)),
            pl.BlockSpec((1, 1, tn), lambda l, j: (l, 0, j)),
        ],
        out_specs=pl.BlockSpec((1, MOD_ROWS, tn), lambda l, j: (l, 0, j)),
        out_shape=jax.ShapeDtypeStruct((DEPTH, MOD_ROWS, n), F32),
        compiler_params=pltpu.CompilerParams(
            dimension_semantics=("arbitrary", "arbitrary"), vmem_limit_bytes=VMEM_LIMIT),
        name="adaln",
    )(c8, w_ada, b_ada.reshape(DEPTH, 1, n))


def _mod_row_of_tile(i, tile):
    n_ctx = T_CTX // tile
    per_req = DEC_SEQ // tile
    return jnp.where(i < n_ctx, CTX_MOD_ROW, (i - n_ctx) // per_req)


def _mod_spec(layer, chunk, tile):
    return pl.BlockSpec(
        (1, 1, D_MODEL), lambda i: ((layer * MOD_ROWS + _mod_row_of_tile(i, tile)) * 6 + chunk, 0, 0))


def _layer_spec(layer, shape):
    return pl.BlockSpec((1,) + shape, lambda *_: (layer,) + (0,) * len(shape))


N_CTX_TILES = T_CTX // TT


def _split_row_specs(width):
    return (pl.BlockSpec((TT, width), lambda i: (jnp.minimum(i, N_CTX_TILES - 1), 0)),
            pl.BlockSpec((TT, width), lambda i: (jnp.maximum(i - N_CTX_TILES, 0), 0)))


def _split_rows(ctx_ref, lat_ref):
    return jnp.where(pl.program_id(0) < N_CTX_TILES, ctx_ref[...], lat_ref[...])


def _rms_norm_heads(x, ones_bd, w):
    hi, lo = _split_bf16(x * x)
    ssq = _dot(hi, ones_bd) + _dot(lo, ones_bd)
    return x * lax.rsqrt(ssq * (1.0 / HEAD_DIM) + RMS_EPS) * w


def _rope_cols(x, cos, sin, first16):
    cols = []
    for j in range(x.shape[1] // LANES):
        xc = x[:, j * LANES:(j + 1) * LANES]
        partner = jnp.where(first16, pltpu.roll(xc, LANES - 16, axis=1), pltpu.roll(xc, 16, axis=1))
        cols.append(xc * cos + partner * sin)
    return jnp.concatenate(cols, axis=1) if len(cols) > 1 else cols[0]


def _proj_kernel(xc_ref, xl_ref, sh_ref, sc_ref, w_ref, ones_ref, qn_ref, kn_ref, cos_ref, sin_ref,
                 a_ref, qb_ref, kb_ref, vb_ref, qc_ref, kc_ref, vc_ref, wbf_ref):
    @pl.when(pl.program_id(0) == 0)
    def _():
        wbf_ref[...] = w_ref[0].astype(BF16)

    h = _split_rows(xc_ref, xl_ref) * (1.0 + sc_ref[0]) + sh_ref[0]
    p = _dot(h.astype(BF16), wbf_ref[...])
    o = 0
    a_ref[...] = p[:, o:o + POOL_WIDTH]; o += POOL_WIDTH
    qb = p[:, o:o + QB_W]; o += QB_W
    kb = p[:, o:o + KB_W]; o += KB_W
    vb_ref[...] = p[:, o:o + KB_W]; o += KB_W
    qc_ref[...] = (p[:, o:o + C_W] * ATTN_SCALE).astype(BF16); o += C_W
    kc_ref[...] = p[:, o:o + C_W]; o += C_W
    vc_ref[...] = p[:, o:o + C_W]

    ones_bd = ones_ref[...]
    cos = cos_ref[...]
    sin = sin_ref[...]
    lane = lax.broadcasted_iota(jnp.int32, (1, LANES), 1)
    first16 = (lane & 31) < 16
    qb = _rope_cols(_rms_norm_heads(qb, ones_bd, qn_ref[...]), cos, sin, first16)
    kb = _rope_cols(_rms_norm_heads(kb, ones_bd[:KB_W, :KB_W], kn_ref[...]), cos, sin, first16)
    qb_ref[...] = (qb * ATTN_SCALE).astype(BF16)
    kb_ref[...] = kb


def _proj(layer, x_ctx, x_lat, mod, w_in, ones_bd, qn, kn, cos_t, sin_t):
    n_tiles = T_ALL // TT
    n_ctx = N_CTX_TILES
    per_req = DEC_SEQ // TT

    def rope_idx(i):
        return (jnp.where(i < n_ctx, per_req, (i - n_ctx) % per_req), 0)

    row = lambda w: pl.BlockSpec((TT, w), lambda i: (i, 0))
    const = lambda s: pl.BlockSpec(s, lambda i: (0,) * len(s))
    widths = (POOL_WIDTH, QB_W, KB_W, KB_W, C_W, C_W, C_W)
    dtypes = (F32, BF16, F32, F32, BF16, F32, F32)
    return pl.pallas_call(
        _proj_kernel,
        grid=(n_tiles,),
        in_specs=[*_split_row_specs(D_MODEL), _mod_spec(layer, 0, TT), _mod_spec(layer, 1, TT),
                  _layer_spec(layer, (D_MODEL, PROJ_WIDTH)),
                  const((QB_W, QB_W)), const((1, QB_W)), const((1, KB_W)),
                  pl.BlockSpec((TT, LANES), rope_idx), pl.BlockSpec((TT, LANES), rope_idx)],
        out_specs=[row(w) for w in widths],
        out_shape=[jax.ShapeDtypeStruct((T_ALL, w), d) for w, d in zip(widths, dtypes)],
        scratch_shapes=[pltpu.VMEM((D_MODEL, PROJ_WIDTH), BF16)],
        compiler_params=pltpu.CompilerParams(dimension_semantics=("arbitrary",), vmem_limit_bytes=VMEM_LIMIT),
        name="proj",
    )(x_ctx, x_lat, mod, mod, w_in, ones_bd, qn, kn, cos_t, sin_t)


def _pool_windows(p):
    n = p.shape[0]
    sh = lambda x, k: pltpu.roll(x, (n - k) % n, axis=0)
    w2 = p + sh(p, -1)
    w4 = sh(w2, -1) + sh(w2, 1)
    w8 = sh(w4, -2) + sh(w4, 2)
    w16 = sh(w8, -4) + sh(w8, 4)
    g = lax.broadcasted_iota(jnp.int32, p.shape, 1) >> 6
    return jnp.where(g == 0, w2, jnp.where(g == 1, w4, jnp.where(g == 2, w8, w16)))


def _pool_kernel(a_ref, w_ref, scale_ref, o_ref, pad_ref):
    a = a_ref[...]

    def mix(csum, seq_len):
        row = lax.broadcasted_iota(jnp.int32, a.shape, 0)
        g = lax.broadcasted_iota(jnp.int32, a.shape, 1) >> 6
        t = row & (seq_len - 1)
        half = jnp.where(g == 0, 1, jnp.where(g == 1, 2, jnp.where(g == 2, 4, 8)))
        cnt = jnp.minimum(t + half, seq_len) - jnp.maximum(t - half, 0)
        pooled = csum / cnt.astype(F32) - a
        y = _dot(pooled.astype(BF16), w_ref[...].astype(BF16)) * scale_ref[...]
        o_ref[...] = y.astype(BF16)

    def padded_sums(seq_len):
        pitch = seq_len + POOL_GAP
        n_req = TP // seq_len
        n = n_req * pitch
        pad_ref[...] = jnp.zeros_like(pad_ref)
        for r in range(n_req):
            pad_ref[r * pitch + POOL_GAP // 2:r * pitch + POOL_GAP // 2 + seq_len, :] = a[r * seq_len:(r + 1) * seq_len]
        sums = _pool_windows(pad_ref[0:n, :])
        return jnp.concatenate(
            [sums[r * pitch + POOL_GAP // 2:r * pitch + POOL_GAP // 2 + seq_len] for r in range(n_req)], axis=0)

    is_ctx = pl.program_id(0) < T_CTX // TP

    @pl.when(is_ctx)
    def _():
        mix(padded_sums(SEQ), SEQ)

    @pl.when(jnp.logical_not(is_ctx))
    def _():
        mix(padded_sums(DEC_SEQ), DEC_SEQ)


def _pool(a, w_bd, scale):
    return pl.pallas_call(
        _pool_kernel,
        grid=(T_ALL // TP,),
        in_specs=[pl.BlockSpec((TP, POOL_WIDTH), lambda i: (i, 0)),
                  pl.BlockSpec((POOL_WIDTH, POOL_WIDTH), lambda i: (0, 0)),
                  pl.BlockSpec((1, POOL_WIDTH), lambda i: (0, 0))],
        out_specs=pl.BlockSpec((TP, POOL_WIDTH), lambda i: (i, 0)),
        out_shape=jax.ShapeDtypeStruct((T_ALL, POOL_WIDTH), BF16),
        scratch_shapes=[pltpu.VMEM(((TP // SEQ) * (SEQ + POOL_GAP), POOL_WIDTH), F32)],
        compiler_params=pltpu.CompilerParams(dimension_semantics=("arbitrary",), vmem_limit_bytes=VMEM_LIMIT),
        name="pool",
    )(a, w_bd, scale)


def _softmax_pv(scores, values):
    m = scores[0].max(axis=-1, keepdims=True)
    for s in scores[1:]:
        m = jnp.maximum(m, s.max(axis=-1, keepdims=True))
    ps = [jnp.exp(s - m) for s in scores]
    l = ps[0].sum(axis=-1, keepdims=True)
    for p in ps[1:]:
        l = l + p.sum(axis=-1, keepdims=True)
    o = _dot(ps[0].astype(BF16), values[0])
    for p, v in zip(ps[1:], values[1:]):
        o = o + _dot(p.astype(BF16), v)
    return o * (1.0 / l)


def _lane_halves():
    lane = lax.broadcasted_iota(jnp.int32, (1, LANES), 1)
    return lane < HEAD_DIM, lane >= HEAD_DIM


def _keep(x, mask):
    return jnp.where(mask, x, 0.0).astype(BF16)


def _gqa_variants(x):
    lo, hi = _lane_halves()
    xs = pltpu.roll(x, HEAD_DIM, axis=1)
    nat_lo, nat_hi = _keep(x, lo), _keep(x, hi)
    sw_lo, sw_hi = _keep(xs, lo), _keep(xs, hi)
    return ((nat_lo, sw_hi), (nat_lo, nat_hi), (sw_lo, nat_hi))


def _mha_variants(x):
    lo, hi = _lane_halves()
    out = []
    for j in range(x.shape[1] // LANES):
        xc = x[:, j * LANES:(j + 1) * LANES]
        out.append((_keep(xc, lo), _keep(xc, hi)))
    return tuple(out)


def _attend_cols(q, k_vars, v_vars, extra_k=None, extra_v=None, bias=None):
    cols = []
    for j in range(q.shape[1] // LANES):
        qc = q[:, j * LANES:(j + 1) * LANES]
        o = None
        for hh in range(2):
            s = _dot_nt(qc, k_vars[j][hh])
            if bias is not None:
                s = s + bias[j][hh]
            scores, values = [s], [v_vars[j][hh]]
            if extra_k is not None:
                scores.append(_dot_nt(qc, extra_k[j][hh]))
                values.append(extra_v[j][hh])
            oh = _softmax_pv(scores, values)
            o = oh if o is None else o + oh
        cols.append(o)
    return jnp.concatenate(cols, axis=1)


def _attn_ctx_kernel(layer_slot, qb_ref, kb_ref, vb_ref, qc_ref, kc_ref, vc_ref, *refs):
    yb_ref, yc_ref, kbt_ref, vbt_ref, kct_ref, vct_ref = refs[-6:]
    kb, vb, kc, vc = kb_ref[...], vb_ref[...], kc_ref[...], vc_ref[...]
    yb = _attend_cols(qb_ref[...], _gqa_variants(kb), _gqa_variants(vb))
    yb_ref[...] = yb.astype(BF16)
    yc = _attend_cols(qc_ref[...], _mha_variants(kc), _mha_variants(vc))
    yc_ref[...] = yc.astype(BF16)
    for ref, val in ((kbt_ref, kb), (vbt_ref, vb), (kct_ref, kc), (vct_ref, vc)):
        for slot in range(ref.shape[1]):
            ref[0, slot] = val.T if slot == layer_slot else jnp.zeros(ref.shape[2:], F32)


CACHE_WIDTHS = (KB_W, KB_W, C_W, C_W)


def _attn_ctx(layer, qb, kb, vb, qc, kc, vc, caches):
    row = lambda w: pl.BlockSpec((SEQ, w), lambda i: (i, 0))
    if caches:
        cache = lambda w: pl.BlockSpec((1, 1, w, SEQ), lambda i: (i, layer, 0, 0))
        layer_slot = 0
    else:
        cache = lambda w: pl.BlockSpec((1, DEPTH, w, SEQ), lambda i: (i, 0, 0, 0))
        layer_slot = layer
    return pl.pallas_call(
        functools.partial(_attn_ctx_kernel, layer_slot),
        grid=(BATCH,),
        in_specs=[row(QB_W), row(KB_W), row(KB_W), row(C_W), row(C_W), row(C_W)]
                 + [pl.BlockSpec(memory_space=pl.ANY)] * len(caches),
        out_specs=[row(QB_W), row(C_W)] + [cache(w) for w in CACHE_WIDTHS],
        out_shape=[jax.ShapeDtypeStruct((T_CTX, QB_W), BF16), jax.ShapeDtypeStruct((T_CTX, C_W), BF16)]
                  + [jax.ShapeDtypeStruct((BATCH, DEPTH, w, SEQ), F32) for w in CACHE_WIDTHS],
        input_output_aliases={6 + k: 2 + k for k in range(len(caches))},
        compiler_params=pltpu.CompilerParams(dimension_semantics=("arbitrary",), vmem_limit_bytes=VMEM_LIMIT),
        name="attn_ctx",
    )(qb, kb, vb, qc, kc, vc, *caches)


def _attn_latb_kernel(q_ref, k_ref, v_ref, ck_ref, cv_ref, y_ref):
    y = _attend_cols(q_ref[...], _gqa_variants(ck_ref[0, 0]), _gqa_variants(cv_ref[0, 0]),
                     extra_k=_gqa_variants(k_ref[...]), extra_v=_gqa_variants(v_ref[...]))
    y_ref[...] = y.astype(BF16)


def _attn_latb(layer, qb, kb, vb, cache_k, cache_v):
    ctx_h = T_CTX // HALF
    ctx_r = T_CTX // DEC_SEQ
    cache = pl.BlockSpec((1, 1, PAST_LEN, KB_W), lambda b, s: (b, layer, 0, 0))
    own = pl.BlockSpec((DEC_SEQ, KB_W), lambda b, s: (ctx_r + b, 0))
    return pl.pallas_call(
        _attn_latb_kernel,
        grid=(DEC_BATCH, DEC_SEQ // HALF),
        in_specs=[pl.BlockSpec((HALF, QB_W), lambda b, s: (ctx_h + 2 * b + s, 0)), own, own, cache, cache],
        out_specs=pl.BlockSpec((HALF, QB_W), lambda b, s: (2 * b + s, 0)),
        out_shape=jax.ShapeDtypeStruct((T_LAT, QB_W), BF16),
        compiler_params=pltpu.CompilerParams(
            dimension_semantics=("arbitrary", "arbitrary"), vmem_limit_bytes=VMEM_LIMIT),
        name="attn_latb",
    )(qb, kb, vb, cache_k, cache_v)


def _natten_window(s, i):
    r = (HALF // GRID_W) * s + i
    rs = min(max(r - WIN_R // 2, 0), GRID_ROWS - WIN_R)
    return r, rs


def _natten_kernel(q_ref, k_ref, v_ref, ck_ref, cv_ref, blk_ref, y_ref, bias_ref):
    s = pl.program_id(1)

    for sv in range(2):
        @pl.when((pl.program_id(2) == 0) & (s == sv))
        def _():
            lo_half, _ = _lane_halves()
            masked = jnp.full((GRID_W, LANES), NEG, F32)
            for hh in range(2):
                for i in range(HALF // GRID_W):
                    r, rs = _natten_window(sv, i)
                    for jp in range(NAT_KEYS // LANES):
                        pair = []
                        for j in (2 * jp, 2 * jp + 1):
                            rk = 4 * sv + j
                            pair.append(blk_ref[0, hh, rk - r + WIN_R - 1] if rs <= rk < rs + WIN_R else masked)
                        bias_ref[hh, i * GRID_W:(i + 1) * GRID_W, jp * LANES:(jp + 1) * LANES] = (
                            jnp.where(lo_half, pair[0], pair[1]))

    start = pl.multiple_of(s * (4 * GRID_W), 4 * GRID_W)
    k = _mha_variants(k_ref[pl.ds(start, NAT_KEYS), :])
    v = _mha_variants(v_ref[pl.ds(start, NAT_KEYS), :])
    ck = _mha_variants(ck_ref[0, 0])
    cv = _mha_variants(cv_ref[0, 0])
    bias = ((bias_ref[0], bias_ref[1]),)
    y = _attend_cols(q_ref[...], k, v, extra_k=ck, extra_v=cv, bias=bias)
    y_ref[...] = y.astype(BF16)


def _natten(layer, qc, kc, vc, cache_k, cache_v, blocks):
    ctx_h = T_CTX // HALF
    ctx_r = T_CTX // DEC_SEQ
    cache = pl.BlockSpec((1, 1, PAST_LEN, LANES), lambda j, s, b: (b, layer, 0, j))
    own = pl.BlockSpec((DEC_SEQ, LANES), lambda j, s, b: (ctx_r + b, j))
    return pl.pallas_call(
        _natten_kernel,
        grid=(C_W // LANES, DEC_SEQ // HALF, DEC_BATCH),
        in_specs=[pl.BlockSpec((HALF, LANES), lambda j, s, b: (ctx_h + 2 * b + s, j)), own, own, cache, cache,
                  pl.BlockSpec((1, 2, 2 * WIN_R - 1, GRID_W, LANES), lambda j, s, b: (j, 0, 0, 0, 0))],
        out_specs=pl.BlockSpec((HALF, LANES), lambda j, s, b: (2 * b + s, j)),
        out_shape=jax.ShapeDtypeStruct((T_LAT, C_W), BF16),
        scratch_shapes=[pltpu.VMEM((2, HALF, NAT_KEYS), F32)],
        compiler_params=pltpu.CompilerParams(
            dimension_semantics=("arbitrary", "arbitrary", "arbitrary"), vmem_limit_bytes=VMEM_LIMIT),
        name="natten",
    )(qc, kc, vc, cache_k, cache_v, blocks)


def _natten_blocks(rpb):
    qcol = np.arange(GRID_W)[:, None]
    kcol = np.arange(GRID_W)[None, :]
    cs = np.clip(qcol - WIN_C // 2, 0, GRID_W - WIN_C)
    col_ok = (kcol >= cs) & (kcol < cs + WIN_C)
    dc = np.clip(kcol - qcol, -(WIN_C - 1), WIN_C - 1) + (WIN_C - 1)
    sel_c = (dc[:, :, None] == np.arange(2 * WIN_C - 1)).astype(np.float32)
    blk = jnp.einsum("hdm,qkm->hdqk", rpb, jnp.asarray(sel_c), precision=lax.Precision.HIGHEST)
    blk = jnp.where(jnp.asarray(col_ok), blk, NEG)
    blk = jnp.concatenate([blk, blk], axis=-1)
    return blk.reshape(H_C // 2, 2, 2 * WIN_R - 1, GRID_W, LANES)


def _route(lg):
    lane = lax.broadcasted_iota(jnp.int32, lg.shape, 1).astype(F32)
    low = jnp.float32(-3.0e38)
    far = jnp.float32(ROUTER_LANES)
    is_g = lane < N_GROUPS
    gmax = jnp.where(is_g, lg, low).max(axis=-1, keepdims=True)
    gsel = jnp.where(is_g & (lg == gmax), lane, far).min(axis=-1, keepdims=True)
    pg_sel = 1.0 / jnp.where(is_g, jnp.exp(lg - gmax), 0.0).sum(axis=-1, keepdims=True)
    e0 = EXPERT_LANE0 + EXPERTS_PER_GROUP * gsel
    in_g = (lane >= e0) & (lane < e0 + EXPERTS_PER_GROUP)
    m1 = jnp.where(in_g, lg, low).max(axis=-1, keepdims=True)
    i1 = jnp.where(in_g & (lg == m1), lane, far).min(axis=-1, keepdims=True)
    rest = in_g & (lane != i1)
    m2 = jnp.where(rest, lg, low).max(axis=-1, keepdims=True)
    i2 = jnp.where(rest & (lg == m2), lane, far).min(axis=-1, keepdims=True)
    t = jnp.exp(m2 - m1)
    ssum = 1.0 + t
    w1 = (1.0 / ssum) * pg_sel
    w2 = (t / ssum) * pg_sel
    return i1 - EXPERT_LANE0, i2 - EXPERT_LANE0, w1, w2


def _rows_to_groups(x, stage_ref, out_ref, n_rows):
    stride = n_rows + 8
    for c in range(GROUP_ROWS):
        stage_ref[c * stride:c * stride + n_rows, :] = x[:, c * LANES:(c + 1) * LANES]
    for j in range(n_rows):
        out_ref[GROUP_ROWS * j:GROUP_ROWS * (j + 1), :] = stage_ref[pl.ds(j, GROUP_ROWS, stride=stride), :]


def _tail_kernel(xc_ref, xl_ref, ya_ref, ybc_ref, ybl_ref, ycc_ref, ycl_ref, w_ref, g1_ref, sh2_ref, sc2_ref,
                 lg_ref, lb_ref, wr_ref, br_ref, x1_ref, h2g_ref, ids_ref, wts_ref, wbf_ref, stage_ref):
    @pl.when(pl.program_id(0) == 0)
    def _():
        wbf_ref[...] = w_ref[0].astype(BF16)

    o = (_dot(ya_ref[...], wbf_ref[0:POOL_WIDTH, :])
         + _dot(_split_rows(ybc_ref, ybl_ref), wbf_ref[POOL_WIDTH:POOL_WIDTH + QB_W, :])
         + _dot(_split_rows(ycc_ref, ycl_ref), wbf_ref[POOL_WIDTH + QB_W:, :]))
    x1 = _layer_norm(ALPHA * _split_rows(xc_ref, xl_ref) + g1_ref[0] * o, lg_ref[0], lb_ref[0])
    x1_ref[...] = x1
    h2 = x1 * (1.0 + sc2_ref[0]) + sh2_ref[0]
    hh, hl = _split_bf16(h2)
    wh, wl = _split_bf16(wr_ref[0])
    lg = _dot(hh, wh) + _dot(hl, wh) + _dot(hh, wl) + br_ref[0]
    i1, i2, w1, w2 = _route(lg)
    lane = lax.broadcasted_iota(jnp.int32, lg.shape, 1)
    ids = jnp.where(lane == 0, i1, jnp.where(lane == 1, i2, 0.0))
    ids_ref[0] = ids.T[0:8, :].astype(jnp.int32)
    wts = jnp.where(lane == 0, w1, jnp.where(lane == 1, w2, 0.0))
    wts_ref[0] = wts.T[0:8, :]
    _rows_to_groups(h2, stage_ref, h2g_ref, TT)


def _tail(layer, x_ctx, x_lat, ya, yb_ctx, yb_lat, yc_ctx, yc_lat, w_out, mod, ln_g, ln_b, wr, br):
    row = lambda w: pl.BlockSpec((TT, w), lambda i: (i, 0))
    return pl.pallas_call(
        _tail_kernel,
        grid=(T_ALL // TT,),
        in_specs=[*_split_row_specs(D_MODEL), row(POOL_WIDTH), *_split_row_specs(QB_W), *_split_row_specs(C_W),
                  _layer_spec(layer, (D_MODEL, D_MODEL)),
                  _mod_spec(layer, 2, TT), _mod_spec(layer, 3, TT), _mod_spec(layer, 4, TT),
                  _layer_spec(layer, (1, D_MODEL)), _layer_spec(layer, (1, D_MODEL)),
                  _layer_spec(layer, (D_MODEL, ROUTER_LANES)), _layer_spec(layer, (1, ROUTER_LANES))],
        out_specs=[row(D_MODEL), pl.BlockSpec((TT * GROUP_ROWS, LANES), lambda i: (i, 0)),
                   pl.BlockSpec((1, 8, TT), lambda i: (i, 0, 0)), pl.BlockSpec((1, 8, TT), lambda i: (i, 0, 0))],
        out_shape=[jax.ShapeDtypeStruct((T_ALL, D_MODEL), F32),
                   jax.ShapeDtypeStruct((T_ALL * GROUP_ROWS, LANES), F32),
                   jax.ShapeDtypeStruct((T_ALL // TT, 8, TT), jnp.int32),
                   jax.ShapeDtypeStruct((T_ALL // TT, 8, TT), F32)],
        scratch_shapes=[pltpu.VMEM((D_MODEL, D_MODEL), BF16),
                        pltpu.VMEM((GROUP_ROWS * (TT + 8), LANES), F32)],
        compiler_params=pltpu.CompilerParams(dimension_semantics=("arbitrary",), vmem_limit_bytes=VMEM_LIMIT),
        name="tail",
    )(x_ctx, x_lat, ya, yb_ctx, yb_lat, yc_ctx, yc_lat, w_out, mod, mod, mod, ln_g, ln_b, wr, br)


def _plan_kernel(eid_ref, pos_ref, tile_ref):
    e = eid_ref[0]
    rows = e.shape[0]
    r = lax.broadcasted_iota(jnp.int32, (LANES, LANES), 0)
    c = lax.broadcasted_iota(jnp.int32, (LANES, LANES), 1)
    upper = (r <= c).astype(BF16)
    ones = jnp.ones((LANES, LANES), BF16)
    lower = (lax.broadcasted_iota(jnp.int32, (rows, rows), 1)
             < lax.broadcasted_iota(jnp.int32, (rows, rows), 0)).astype(BF16)
    lane = lax.broadcasted_iota(jnp.int32, (1, LANES), 1)
    masks = [e == ex for ex in range(N_EXPERTS)]
    stack = jnp.concatenate([m.astype(BF16) for m in masks], axis=0)
    incl_all = _dot(stack, upper)
    tot_all = _dot(stack, ones)
    pos = jnp.zeros(e.shape, F32)
    base = jnp.zeros((1, LANES), F32)
    valid_end = jnp.zeros((1, LANES), F32)
    pad_end = jnp.zeros((1, LANES), F32)
    for ex, m in enumerate(masks):
        incl = incl_all[ex * rows:(ex + 1) * rows]
        row_tot = tot_all[ex * rows:(ex + 1) * rows]
        row_off = _dot(lower, row_tot.astype(BF16))
        cnt = jnp.sum(row_tot, axis=0, keepdims=True)
        pos = jnp.where(m, base + row_off + incl - 1.0, pos)
        valid_end = jnp.where(lane == ex, base + cnt, valid_end)
        base = base + jnp.ceil(cnt * (1.0 / TMX)) * TMX
        pad_end = jnp.where(lane == ex, base, pad_end)
    pos_ref[0] = pos.astype(jnp.int32)
    sub = lax.broadcasted_iota(jnp.int32, (8, LANES), 0)
    rows8 = lambda v: jnp.broadcast_to(v, (8, LANES))
    table = jnp.where(sub == PLAN_N_TILES, rows8(base * (1.0 / TMX)),
                      jnp.where(sub == PLAN_VALID_END, rows8(valid_end),
                                jnp.where(sub == PLAN_PAD_END, rows8(pad_end), 0.0)))
    tile_ref[0] = table.astype(jnp.int32)


def _plan(eid):
    rows = 2 * T_STREAM // LANES
    return pl.pallas_call(
        _plan_kernel,
        grid=(N_STREAMS,),
        in_specs=[pl.BlockSpec((1, rows, LANES), lambda s: (s, 0, 0))],
        out_specs=[pl.BlockSpec((1, rows, LANES), lambda s: (s, 0, 0)), pl.BlockSpec((1, 8, LANES), lambda s: (s, 0, 0))],
        out_shape=[jax.ShapeDtypeStruct((N_STREAMS, rows, LANES), jnp.int32),
                   jax.ShapeDtypeStruct((N_STREAMS, 8, LANES), jnp.int32)],
        compiler_params=pltpu.CompilerParams(dimension_semantics=("arbitrary",), vmem_limit_bytes=VMEM_LIMIT),
        name="plan",
    )(eid)


def _moe_kernel(pos_ref, wpair_ref, plan_ref, h2g_hbm, wg_ref, wu_ref, wd_ref, y_hbm,
                h2v, yv, off, xg0, xg1, yg0, yg1, wgb, wub, wdb, sem):
    s = pl.program_id(0)
    e = pl.program_id(1)
    n_pairs = 2 * T_STREAM
    spare_off = T_STREAM * GROUP_ROWS
    plan = lambda row, idx: plan_ref[(s * 8 + row) * LANES + idx]
    n_tiles = plan(PLAN_N_TILES, 0)
    last = n_tiles - 1

    xgs = (xg0, xg1)
    ygs = (yg0, yg1)

    def row_offset(pair):
        return pl.multiple_of((pair & (SLOT_PITCH - 1)) << 3, GROUP_ROWS)

    def gather(tile, dst):
        base = tile * TMX
        for j in range(TMX):
            dst[pl.ds(j, GROUP_ROWS, stride=XSTRIDE), :] = h2v[pl.ds(row_offset(off[base + j]), GROUP_ROWS), :]

    def h2_copy():
        return pltpu.make_async_copy(
            h2g_hbm.at[pl.ds(pl.multiple_of(s * (T_STREAM * GROUP_ROWS), 8), T_STREAM * GROUP_ROWS)],
            h2v.at[pl.ds(0, T_STREAM * GROUP_ROWS)], sem.at[0])

    def y_copy(stream):
        return pltpu.make_async_copy(yv.at[pl.ds(0, T_STREAM * GROUP_ROWS)], y_hbm.at[stream], sem.at[1])

    @pl.when(e == 0)
    def _():
        h2_copy().start()
        h2v[spare_off:spare_off + GROUP_ROWS, :] = jnp.zeros((GROUP_ROWS, LANES), F32)

        def pad_expert(ex, carry):
            lo = plan(PLAN_VALID_END, ex)

            def pad_chunk(k, c):
                for u in range(PAD_CHUNK):
                    off[lo + k * PAD_CHUNK + u] = T_STREAM
                return c
            n_chunks = (plan(PLAN_PAD_END, ex) - lo + (PAD_CHUNK - 1)) // PAD_CHUNK
            return lax.fori_loop(0, n_chunks, pad_chunk, carry)
        lax.fori_loop(0, N_EXPERTS, pad_expert, 0)

        def place(i, carry):
            for u in range(PLACE_UNROLL):
                tok = i * PLACE_UNROLL + u
                for slot in range(2):
                    off[pos_ref[s * n_pairs + slot * T_STREAM + tok]] = slot * SLOT_PITCH + tok
            return carry
        lax.fori_loop(0, T_STREAM // PLACE_UNROLL, place, 0)

        @pl.when(s > 0)
        def _():
            y_copy(s - 1).wait()
        yv[...] = jnp.zeros_like(yv)
        yg1[...] = jnp.zeros_like(yg1)
        h2_copy().wait()
        gather(0, xg0)

    t_lo = jnp.where(e == 0, 0, plan(PLAN_PAD_END, jnp.maximum(e - 1, 0))) // TMX
    t_hi = plan(PLAN_PAD_END, e) // TMX

    @pl.when(t_hi > t_lo)
    def _():
        wgb[...] = wg_ref[0, 0].astype(BF16)
        wub[...] = wu_ref[0, 0].astype(BF16)
        wdb[...] = wd_ref[0, 0].astype(BF16)

    def add_tile(tile, done):
        base = tile * TMX
        for j0 in range(0, TMX, RMW_BATCH):
            updates = []
            for j in range(j0, j0 + RMW_BATCH):
                pair = off[base + j]
                dst = row_offset(pair)
                gate = wpair_ref[s * (2 * SLOT_PITCH) + pair]
                row = done[pl.ds(j, GROUP_ROWS, stride=XSTRIDE), :]
                updates.append((dst, yv[pl.ds(dst, GROUP_ROWS), :] + gate * row))
            for dst, val in updates:
                yv[pl.ds(dst, GROUP_ROWS), :] = val

    def step(t, cur, nxt, out, done):
        gather(jnp.minimum(t + 1, last), nxt)
        x = jnp.concatenate([cur[c * XSTRIDE:c * XSTRIDE + TMX, :] for c in range(GROUP_ROWS)], axis=1).astype(BF16)
        hg = _dot(x, wgb[...])
        hu = _dot(x, wub[...])
        act = hg * jax.nn.sigmoid(hg) * hu
        y = _dot(act.astype(BF16), wdb[...])
        for c in range(GROUP_ROWS):
            out[c * XSTRIDE:c * XSTRIDE + TMX, :] = y[:, c * LANES:(c + 1) * LANES]
        add_tile(jnp.maximum(t - 1, 0), done)

    def tile_body(t, carry):
        for par in range(2):
            @pl.when((t & 1) == par)
            def _():
                step(t, xgs[par], xgs[1 - par], ygs[par], ygs[1 - par])
        return carry
    lax.fori_loop(t_lo, t_hi, tile_body, 0)

    @pl.when(e == N_EXPERTS - 1)
    def _():
        for par in range(2):
            @pl.when((last & 1) == par)
            def _():
                add_tile(last, ygs[par])
        y_copy(s).start()

        @pl.when(s == N_STREAMS - 1)
        def _():
            y_copy(s).wait()


def _moe(layer, pos, wpair, plan, h2g, w_gate, w_up, w_down):
    wspec = lambda shape: pl.BlockSpec((1,) + shape, lambda s, e, pos, wpair, plan: (layer, e, 0, 0))
    return pl.pallas_call(
        _moe_kernel,
        grid_spec=pltpu.PrefetchScalarGridSpec(
            num_scalar_prefetch=3,
            grid=(N_STREAMS, N_EXPERTS),
            in_specs=[pl.BlockSpec(memory_space=pl.ANY),
                      wspec((1, D_MODEL, D_EXPERT)), wspec((1, D_MODEL, D_EXPERT)), wspec((1, D_EXPERT, D_MODEL))],
            out_specs=pl.BlockSpec(memory_space=pl.ANY),
            scratch_shapes=[pltpu.VMEM(((T_STREAM + 1) * GROUP_ROWS, LANES), F32),
                            pltpu.VMEM(((T_STREAM + 1) * GROUP_ROWS, LANES), F32),
                            pltpu.SMEM((P_MAX + PAD_CHUNK,), jnp.int32),
                            pltpu.VMEM((GROUP_ROWS * XSTRIDE, LANES), F32),
                            pltpu.VMEM((GROUP_ROWS * XSTRIDE, LANES), F32),
                            pltpu.VMEM((GROUP_ROWS * XSTRIDE, LANES), F32),
                            pltpu.VMEM((GROUP_ROWS * XSTRIDE, LANES), F32),
                            pltpu.VMEM((D_MODEL, D_EXPERT), BF16), pltpu.VMEM((D_MODEL, D_EXPERT), BF16),
                            pltpu.VMEM((D_EXPERT, D_MODEL), BF16),
                            pltpu.SemaphoreType.DMA((2,))]),
        out_shape=jax.ShapeDtypeStruct((N_STREAMS, T_STREAM * GROUP_ROWS, LANES), F32),
        compiler_params=pltpu.CompilerParams(
            dimension_semantics=("arbitrary", "arbitrary"), vmem_limit_bytes=VMEM_LIMIT),
        name="moe",
    )(pos, wpair, plan, h2g, w_gate, w_up, w_down)


def _groups_to_rows(src_ref, stage_ref, n_rows):
    stride = n_rows + 8
    for j in range(n_rows):
        stage_ref[pl.ds(j, GROUP_ROWS, stride=stride), :] = src_ref[GROUP_ROWS * j:GROUP_ROWS * (j + 1), :]
    return jnp.concatenate([stage_ref[c * stride:c * stride + n_rows, :] for c in range(GROUP_ROWS)], axis=1)


def _final_kernel(x1_ref, y_ref, g2_ref, lg_ref, lb_ref, o_ref, stage_ref):
    moe = _groups_to_rows(y_ref, stage_ref, TF)
    o_ref[...] = _layer_norm(ALPHA * x1_ref[...] + g2_ref[0] * moe, lg_ref[0], lb_ref[0])


def _final(layer, stream, x1, y, mod, ln_g, ln_b):
    first = stream * (T_STREAM // TF)
    g2 = pl.BlockSpec(
        (1, 1, D_MODEL), lambda i: ((layer * MOD_ROWS + _mod_row_of_tile(first + i, TF)) * 6 + 5, 0, 0))
    return pl.pallas_call(
        _final_kernel,
        grid=(T_STREAM // TF,),
        in_specs=[pl.BlockSpec((TF, D_MODEL), lambda i: (first + i, 0)),
                  pl.BlockSpec((TF * GROUP_ROWS, LANES), lambda i: (first + i, 0)),
                  g2, _layer_spec(layer, (1, D_MODEL)), _layer_spec(layer, (1, D_MODEL))],
        out_specs=pl.BlockSpec((TF, D_MODEL), lambda i: (i, 0)),
        out_shape=jax.ShapeDtypeStruct((T_STREAM, D_MODEL), F32),
        scratch_shapes=[pltpu.VMEM((GROUP_ROWS * (TF + 8), LANES), F32)],
        compiler_params=pltpu.CompilerParams(dimension_semantics=("arbitrary",), vmem_limit_bytes=VMEM_LIMIT),
        name="final",
    )(x1, y, mod, ln_g, ln_b)


def _rope_tables():
    t = np.arange(DEC_SEQ)
    pos = np.stack([t // GRID_W, t % GRID_W], axis=1).astype(np.float32)
    nf = HEAD_DIM // 4
    inv = jnp.asarray(ROPE_THETA, F32) ** (-jnp.arange(nf, dtype=F32) / nf)
    d = np.arange(LANES) % HEAD_DIM
    which = d // (HEAD_DIM // 2)
    ang = jnp.asarray(pos)[:, which] * inv[d % nf][None, :]
    sign = np.where((d % 32) < 16, -1.0, 1.0).astype(np.float32)
    cos = jnp.concatenate([jnp.cos(ang), jnp.ones((TT, LANES), F32)], axis=0)
    sin = jnp.concatenate([jnp.sin(ang) * sign[None, :], jnp.zeros((TT, LANES), F32)], axis=0)
    return cos, sin


def _block_ones():
    h = np.arange(QB_W) // HEAD_DIM
    return jnp.asarray((h[:, None] == h[None, :]).astype(np.float32), dtype=BF16)


def _pool_block_diag(pool_w):
    out = jnp.zeros((POOL_WIDTH, POOL_WIDTH), F32)
    for g in range(4):
        out = out.at[64 * g:64 * g + 64, 64 * g:64 * g + 64].set(pool_w[g])
    return out


def kernel(x_prompt, x_sample, cache_b_k, cache_b_v, cache_c_k, cache_c_v, c, c_ctx, w_ada, b_ada, w_in, w_out,
           pool_w, pool_scale, q_norm, k_norm, rpb, ln1_g, ln1_b, ln2_g, ln2_b, router_g, router_g_b, router_e,
           router_e_b, w_gate, w_up, w_down):
    x_ctx = x_prompt.reshape(T_CTX, D_MODEL)
    x_lat = x_sample.reshape(T_LAT, D_MODEL)
    c8 = jnp.concatenate([c, c_ctx[None], jnp.zeros((MOD_ROWS - DEC_BATCH - 1, D_MODEL), F32)], axis=0)
    mod = _adaln(c8, w_ada, b_ada).reshape(DEPTH * MOD_ROWS * 6, 1, D_MODEL)

    cos_t, sin_t = _rope_tables()
    ones_bd = _block_ones()
    cbk = cache_b_k.reshape(DEC_BATCH, DEPTH, PAST_LEN, KB_W)
    cbv = cache_b_v.reshape(DEC_BATCH, DEPTH, PAST_LEN, KB_W)
    cck = cache_c_k.reshape(DEC_BATCH, DEPTH, PAST_LEN, C_W)
    ccv = cache_c_v.reshape(DEC_BATCH, DEPTH, PAST_LEN, C_W)
    pad = jnp.zeros((DEPTH, D_MODEL, ROUTER_LANES - N_GROUPS - N_EXPERTS), F32)
    wr = jnp.concatenate([router_g, router_e, pad], axis=2)
    br = jnp.concatenate([router_g_b, router_e_b, pad[:, 0]], axis=1)[:, None]
    per_layer = lambda p: p[:, None]

    caches = ()
    for l in range(DEPTH):
        qn = jnp.tile(q_norm[l], H_B)[None]
        kn = jnp.tile(k_norm[l], KV_B)[None]
        a, qb, kb, vb, qc, kc, vc = _proj(l, x_ctx, x_lat, mod, w_in, ones_bd, qn, kn, cos_t, sin_t)
        ya = _pool(a, _pool_block_diag(pool_w[l]), pool_scale[l][None])
        yb_ctx, yc_ctx, *caches = _attn_ctx(l, qb, kb, vb, qc, kc, vc, caches)
        yb_lat = _attn_latb(l, qb, kb, vb, cbk, cbv)
        yc_lat = _natten(l, qc, kc, vc, cck, ccv, _natten_blocks(rpb[l]))
        x1, h2g, ids, wts = _tail(l, x_ctx, x_lat, ya, yb_ctx, yb_lat, yc_ctx, yc_lat, w_out, mod,
                                  per_layer(ln1_g), per_layer(ln1_b), wr, br)
        pairs = lambda a: a.reshape(N_STREAMS, T_STREAM // TT, 8, TT)[:, :, :2, :].transpose(0, 2, 1, 3)
        pos, tiles = _plan(pairs(ids).reshape(N_STREAMS, 2 * T_STREAM // LANES, LANES))
        gates = jnp.pad(pairs(wts).reshape(N_STREAMS, 2, T_STREAM), ((0, 0), (0, 0), (0, SLOT_PITCH - T_STREAM)))
        y = _moe(l, pos.reshape(-1), gates.reshape(-1), tiles.reshape(-1), h2g, w_gate, w_up, w_down)
        y = y.reshape(T_ALL * GROUP_ROWS, LANES)
        x_ctx = _final(l, 0, x1, y, mod, per_layer(ln2_g), per_layer(ln2_b))
        x_lat = _final(l, 1, x1, y, mod, per_layer(ln2_g), per_layer(ln2_b))

    y_prompt = x_ctx.reshape(BATCH, SEQ, D_MODEL)
    y_sample = x_lat.reshape(DEC_BATCH, DEC_SEQ, D_MODEL)
    heads_last = lambda t, h: t.reshape(BATCH, DEPTH, h, HEAD_DIM, SEQ).transpose(0, 1, 4, 2, 3)
    new_bk, new_bv, new_ck, new_cv = caches
    return (y_prompt, y_sample, heads_last(new_bk, KV_B), heads_last(new_bv, KV_B),
            heads_last(new_ck, H_C), heads_last(new_cv, H_C))
```

```python
import functools

import numpy as np
import jax
import jax.numpy as jnp
from jax import lax
from jax.experimental import pallas as pl
from jax.experimental.pallas import tpu as pltpu

F32 = jnp.float32
BF16 = jnp.bfloat16

D_MODEL = 1024
BATCH = 16
SEQ = 256
DEPTH = 2
DEC_BATCH = 4
DEC_SEQ = 1024
PAST_LEN = 256
GRID_W = 64
GRID_ROWS = DEC_SEQ // GRID_W
HEAD_DIM = 64
POOL_WIDTH = 256
H_B = 6
KV_B = 2
H_C = 6
WIN_R = 8
WIN_C = 16
ROPE_THETA = 10000.0
QB_W = H_B * HEAD_DIM
KB_W = KV_B * HEAD_DIM
C_W = H_C * HEAD_DIM
PROJ_WIDTH = 2048
N_GROUPS = 4
EXPERTS_PER_GROUP = 8
N_EXPERTS = 32
D_EXPERT = 256
ALPHA = (2 * DEPTH) ** 0.25
LN_EPS = 1e-6
RMS_EPS = 1e-6
NEG = -1e30
ATTN_SCALE = HEAD_DIM ** -0.5

T_CTX = BATCH * SEQ
T_LAT = DEC_BATCH * DEC_SEQ
T_ALL = T_CTX + T_LAT

LANES = 128
ROUTER_LANES = 128
EXPERT_LANE0 = N_GROUPS
MOD_ROWS = 8
CTX_MOD_ROW = DEC_BATCH

TT = 512
TP = 1024
TF = 1024
LN_CHUNK = 64
POOL_GAP = 16
N_STREAMS = 2
T_STREAM = T_ALL // N_STREAMS
TMX = 128
PLACE_UNROLL = 8
PAD_CHUNK = 8
SLOT_PITCH = 2 * T_STREAM
P_MAX = 2 * T_STREAM + N_EXPERTS * TMX
GROUP_ROWS = D_MODEL // LANES
XSTRIDE = TMX + 8
RMW_BATCH = 16
PLAN_N_TILES, PLAN_VALID_END, PLAN_PAD_END = 0, 1, 2
HALF = DEC_SEQ // 2
NAT_KEYS = 12 * GRID_W
VMEM_LIMIT = 56 * 1024 * 1024


def _dot(a, b):
    return jnp.dot(a, b, preferred_element_type=F32)


def _dot_nt(a, b):
    return lax.dot_general(a, b, (((1,), (1,)), ((), ())), preferred_element_type=F32)


def _split_bf16(x):
    hi = x.astype(BF16)
    lo = (x - hi.astype(F32)).astype(BF16)
    return hi, lo


def _layer_norm(y, g, b):
    mu = jnp.mean(y, axis=-1, keepdims=True)
    var = jnp.mean(jnp.square(y - mu), axis=-1, keepdims=True)
    return (y - mu) * lax.rsqrt(var + LN_EPS) * g + b


def _adaln_kernel(c_ref, w_ref, b_ref, o_ref):
    c = c_ref[...]
    s = (c * jax.nn.sigmoid(c)).astype(BF16)
    o_ref[0] = _dot(s, w_ref[0].astype(BF16)) + b_ref[0]


def _adaln(c8, w_ada, b_ada):
    tn = 1536
    n = w_ada.shape[-1]
    return pl.pallas_call(
        _adaln_kernel,
        grid=(DEPTH, n // tn),
        in_specs=[
            pl.BlockSpec((MOD_ROWS, D_MODEL), lambda l, j: (0, 0)),
            pl.BlockSpec((1, D_MODEL, tn), lambda l, j: (l, 0, j)),
            pl.BlockSpec((1, 1, tn), lambda l, j: (l, 0, j)),
        ],
        out_specs=pl.BlockSpec((1, MOD_ROWS, tn), lambda l, j: (l, 0, j)),
        out_shape=jax.ShapeDtypeStruct((DEPTH, MOD_ROWS, n), F32),
        compiler_params=pltpu.CompilerParams(
            dimension_semantics=("arbitrary", "arbitrary"), vmem_limit_bytes=VMEM_LIMIT),
        name="adaln",
    )(c8, w_ada, b_ada.reshape(DEPTH, 1, n))


def _mod_row_of_tile(i, tile):
    n_ctx = T_CTX // tile
    per_req = DEC_SEQ // tile
    return jnp.where(i < n_ctx, CTX_MOD_ROW, (i - n_ctx) // per_req)


def _mod_spec(layer, chunk, tile):
    return pl.BlockSpec(
        (1, 1, D_MODEL), lambda i: ((layer * MOD_ROWS + _mod_row_of_tile(i, tile)) * 6 + chunk, 0, 0))


def _layer_spec(layer, shape):
    return pl.BlockSpec((1,) + shape, lambda *_: (layer,) + (0,) * len(shape))


N_CTX_TILES = T_CTX // TT


def _split_row_specs(width):
    return (pl.BlockSpec((TT, width), lambda i: (jnp.minimum(i, N_CTX_TILES - 1), 0)),
            pl.BlockSpec((TT, width), lambda i: (jnp.maximum(i - N_CTX_TILES, 0), 0)))


def _split_rows(ctx_ref, lat_ref):
    return jnp.where(pl.program_id(0) < N_CTX_TILES, ctx_ref[...], lat_ref[...])


def _rms_norm_heads(x, ones_bd, w):
    hi, lo = _split_bf16(x * x)
    ssq = _dot(hi, ones_bd) + _dot(lo, ones_bd)
    return x * lax.rsqrt(ssq * (1.0 / HEAD_DIM) + RMS_EPS) * w


def _rope_cols(x, cos, sin, first16):
    cols = []
    for j in range(x.shape[1] // LANES):
        xc = x[:, j * LANES:(j + 1) * LANES]
        partner = jnp.where(first16, pltpu.roll(xc, LANES - 16, axis=1), pltpu.roll(xc, 16, axis=1))
        cols.append(xc * cos + partner * sin)
    return jnp.concatenate(cols, axis=1) if len(cols) > 1 else cols[0]


def _proj_kernel(xc_ref, xl_ref, sh_ref, sc_ref, w_ref, ones_ref, qn_ref, kn_ref, cos_ref, sin_ref,
                 a_ref, qb_ref, kb_ref, vb_ref, qc_ref, kc_ref, vc_ref, wbf_ref):
    @pl.when(pl.program_id(0) == 0)
    def _():
        wbf_ref[...] = w_ref[0].astype(BF16)

    h = _split_rows(xc_ref, xl_ref) * (1.0 + sc_ref[0]) + sh_ref[0]
    p = _dot(h.astype(BF16), wbf_ref[...])
    o = 0
    a_ref[...] = p[:, o:o + POOL_WIDTH]; o += POOL_WIDTH
    qb = p[:, o:o + QB_W]; o += QB_W
    kb = p[:, o:o + KB_W]; o += KB_W
    vb_ref[...] = p[:, o:o + KB_W]; o += KB_W
    qc_ref[...] = (p[:, o:o + C_W] * ATTN_SCALE).astype(BF16); o += C_W
    kc_ref[...] = p[:, o:o + C_W]; o += C_W
    vc_ref[...] = p[:, o:o + C_W]

    ones_bd = ones_ref[...]
    cos = cos_ref[...]
    sin = sin_ref[...]
    lane = lax.broadcasted_iota(jnp.int32, (1, LANES), 1)
    first16 = (lane & 31) < 16
    qb = _rope_cols(_rms_norm_heads(qb, ones_bd, qn_ref[...]), cos, sin, first16)
    kb = _rope_cols(_rms_norm_heads(kb, ones_bd[:KB_W, :KB_W], kn_ref[...]), cos, sin, first16)
    qb_ref[...] = (qb * ATTN_SCALE).astype(BF16)
    kb_ref[...] = kb


def _proj(layer, x_ctx, x_lat, mod, w_in, ones_bd, qn, kn, cos_t, sin_t):
    n_tiles = T_ALL // TT
    n_ctx = N_CTX_TILES
    per_req = DEC_SEQ // TT

    def rope_idx(i):
        return (jnp.where(i < n_ctx, per_req, (i - n_ctx) % per_req), 0)

    row = lambda w: pl.BlockSpec((TT, w), lambda i: (i, 0))
    const = lambda s: pl.BlockSpec(s, lambda i: (0,) * len(s))
    widths = (POOL_WIDTH, QB_W, KB_W, KB_W, C_W, C_W, C_W)
    dtypes = (F32, BF16, F32, F32, BF16, F32, F32)
    return pl.pallas_call(
        _proj_kernel,
        grid=(n_tiles,),
        in_specs=[*_split_row_specs(D_MODEL), _mod_spec(layer, 0, TT), _mod_spec(layer, 1, TT),
                  _layer_spec(layer, (D_MODEL, PROJ_WIDTH)),
                  const((QB_W, QB_W)), const((1, QB_W)), const((1, KB_W)),
                  pl.BlockSpec((TT, LANES), rope_idx), pl.BlockSpec((TT, LANES), rope_idx)],
        out_specs=[row(w) for w in widths],
        out_shape=[jax.ShapeDtypeStruct((T_ALL, w), d) for w, d in zip(widths, dtypes)],
        scratch_shapes=[pltpu.VMEM((D_MODEL, PROJ_WIDTH), BF16)],
        compiler_params=pltpu.CompilerParams(dimension_semantics=("arbitrary",), vmem_limit_bytes=VMEM_LIMIT),
        name="proj",
    )(x_ctx, x_lat, mod, mod, w_in, ones_bd, qn, kn, cos_t, sin_t)


def _pool_windows(p):
    n = p.shape[0]
    sh = lambda x, k: pltpu.roll(x, (n - k) % n, axis=0)
    w2 = p + sh(p, -1)
    w4 = sh(w2, -1) + sh(w2, 1)
    w8 = sh(w4, -2) + sh(w4, 2)
    w16 = sh(w8, -4) + sh(w8, 4)
    g = lax.broadcasted_iota(jnp.int32, p.shape, 1) >> 6
    return jnp.where(g == 0, w2, jnp.where(g == 1, w4, jnp.where(g == 2, w8, w16)))


def _pool_kernel(a_ref, w_ref, scale_ref, o_ref, pad_ref):
    a = a_ref[...]

    def mix(csum, seq_len):
        row = lax.broadcasted_iota(jnp.int32, a.shape, 0)
        g = lax.broadcasted_iota(jnp.int32, a.shape, 1) >> 6
        t = row & (seq_len - 1)
        half = jnp.where(g == 0, 1, jnp.where(g == 1, 2, jnp.where(g == 2, 4, 8)))
        cnt = jnp.minimum(t + half, seq_len) - jnp.maximum(t - half, 0)
        pooled = csum / cnt.astype(F32) - a
        y = _dot(pooled.astype(BF16), w_ref[...].astype(BF16)) * scale_ref[...]
        o_ref[...] = y.astype(BF16)

    def padded_sums(seq_len):
        pitch = seq_len + POOL_GAP
        n_req = TP // seq_len
        n = n_req * pitch
        pad_ref[...] = jnp.zeros_like(pad_ref)
        for r in range(n_req):
            pad_ref[r * pitch + POOL_GAP // 2:r * pitch + POOL_GAP // 2 + seq_len, :] = a[r * seq_len:(r + 1) * seq_len]
        sums = _pool_windows(pad_ref[0:n, :])
        return jnp.concatenate(
            [sums[r * pitch + POOL_GAP // 2:r * pitch + POOL_GAP // 2 + seq_len] for r in range(n_req)], axis=0)

    is_ctx = pl.program_id(0) < T_CTX // TP

    @pl.when(is_ctx)
    def _():
        mix(padded_sums(SEQ), SEQ)

    @pl.when(jnp.logical_not(is_ctx))
    def _():
        mix(padded_sums(DEC_SEQ), DEC_SEQ)


def _pool(a, w_bd, scale):
    return pl.pallas_call(
        _pool_kernel,
        grid=(T_ALL // TP,),
        in_specs=[pl.BlockSpec((TP, POOL_WIDTH), lambda i: (i, 0)),
                  pl.BlockSpec((POOL_WIDTH, POOL_WIDTH), lambda i: (0, 0)),
                  pl.BlockSpec((1, POOL_WIDTH), lambda i: (0, 0))],
        out_specs=pl.BlockSpec((TP, POOL_WIDTH), lambda i: (i, 0)),
        out_shape=jax.ShapeDtypeStruct((T_ALL, POOL_WIDTH), BF16),
        scratch_shapes=[pltpu.VMEM(((TP // SEQ) * (SEQ + POOL_GAP), POOL_WIDTH), F32)],
        compiler_params=pltpu.CompilerParams(dimension_semantics=("arbitrary",), vmem_limit_bytes=VMEM_LIMIT),
        name="pool",
    )(a, w_bd, scale)


def _softmax_pv(scores, values):
    m = scores[0].max(axis=-1, keepdims=True)
    for s in scores[1:]:
        m = jnp.maximum(m, s.max(axis=-1, keepdims=True))
    ps = [jnp.exp(s - m) for s in scores]
    l = ps[0].sum(axis=-1, keepdims=True)
    for p in ps[1:]:
        l = l + p.sum(axis=-1, keepdims=True)
    o = _dot(ps[0].astype(BF16), values[0])
    for p, v in zip(ps[1:], values[1:]):
        o = o + _dot(p.astype(BF16), v)
    return o * (1.0 / l)


def _lane_halves():
    lane = lax.broadcasted_iota(jnp.int32, (1, LANES), 1)
    return lane < HEAD_DIM, lane >= HEAD_DIM


def _keep(x, mask):
    return jnp.where(mask, x, 0.0).astype(BF16)


def _gqa_variants(x):
    lo, hi = _lane_halves()
    xs = pltpu.roll(x, HEAD_DIM, axis=1)
    nat_lo, nat_hi = _keep(x, lo), _keep(x, hi)
    sw_lo, sw_hi = _keep(xs, lo), _keep(xs, hi)
    return ((nat_lo, sw_hi), (nat_lo, nat_hi), (sw_lo, nat_hi))


def _mha_variants(x):
    lo, hi = _lane_halves()
    out = []
    for j in range(x.shape[1] // LANES):
        xc = x[:, j * LANES:(j + 1) * LANES]
        out.append((_keep(xc, lo), _keep(xc, hi)))
    return tuple(out)


def _attend_cols(q, k_vars, v_vars, extra_k=None, extra_v=None, bias=None):
    cols = []
    for j in range(q.shape[1] // LANES):
        qc = q[:, j * LANES:(j + 1) * LANES]
        o = None
        for hh in range(2):
            s = _dot_nt(qc, k_vars[j][hh])
            if bias is not None:
                s = s + bias[j][hh]
            scores, values = [s], [v_vars[j][hh]]
            if extra_k is not None:
                scores.append(_dot_nt(qc, extra_k[j][hh]))
                values.append(extra_v[j][hh])
            oh = _softmax_pv(scores, values)
            o = oh if o is None else o + oh
        cols.append(o)
    return jnp.concatenate(cols, axis=1)


def _attn_ctx_kernel(layer_slot, qb_ref, kb_ref, vb_ref, qc_ref, kc_ref, vc_ref, *refs):
    yb_ref, yc_ref, kbt_ref, vbt_ref, kct_ref, vct_ref = refs[-6:]
    kb, vb, kc, vc = kb_ref[...], vb_ref[...], kc_ref[...], vc_ref[...]
    yb = _attend_cols(qb_ref[...], _gqa_variants(kb), _gqa_variants(vb))
    yb_ref[...] = yb.astype(BF16)
    yc = _attend_cols(qc_ref[...], _mha_variants(kc), _mha_variants(vc))
    yc_ref[...] = yc.astype(BF16)
    for ref, val in ((kbt_ref, kb), (vbt_ref, vb), (kct_ref, kc), (vct_ref, vc)):
        for slot in range(ref.shape[1]):
            ref[0, slot] = val.T if slot == layer_slot else jnp.zeros(ref.shape[2:], F32)


CACHE_WIDTHS = (KB_W, KB_W, C_W, C_W)


def _attn_ctx(layer, qb, kb, vb, qc, kc, vc, caches):
    row = lambda w: pl.BlockSpec((SEQ, w), lambda i: (i, 0))
    if caches:
        cache = lambda w: pl.BlockSpec((1, 1, w, SEQ), lambda i: (i, layer, 0, 0))
        layer_slot = 0
    else:
        cache = lambda w: pl.BlockSpec((1, DEPTH, w, SEQ), lambda i: (i, 0, 0, 0))
        layer_slot = layer
    return pl.pallas_call(
        functools.partial(_attn_ctx_kernel, layer_slot),
        grid=(BATCH,),
        in_specs=[row(QB_W), row(KB_W), row(KB_W), row(C_W), row(C_W), row(C_W)]
                 + [pl.BlockSpec(memory_space=pl.ANY)] * len(caches),
        out_specs=[row(QB_W), row(C_W)] + [cache(w) for w in CACHE_WIDTHS],
        out_shape=[jax.ShapeDtypeStruct((T_CTX, QB_W), BF16), jax.ShapeDtypeStruct((T_CTX, C_W), BF16)]
                  + [jax.ShapeDtypeStruct((BATCH, DEPTH, w, SEQ), F32) for w in CACHE_WIDTHS],
        input_output_aliases={6 + k: 2 + k for k in range(len(caches))},
        compiler_params=pltpu.CompilerParams(dimension_semantics=("arbitrary",), vmem_limit_bytes=VMEM_LIMIT),
        name="attn_ctx",
    )(qb, kb, vb, qc, kc, vc, *caches)


def _attn_latb_kernel(q_ref, k_ref, v_ref, ck_ref, cv_ref, y_ref):
    y = _attend_cols(q_ref[...], _gqa_variants(ck_ref[0, 0]), _gqa_variants(cv_ref[0, 0]),
                     extra_k=_gqa_variants(k_ref[...]), extra_v=_gqa_variants(v_ref[...]))
    y_ref[...] = y.astype(BF16)


def _attn_latb(layer, qb, kb, vb, cache_k, cache_v):
    ctx_h = T_CTX // HALF
    ctx_r = T_CTX // DEC_SEQ
    cache = pl.BlockSpec((1, 1, PAST_LEN, KB_W), lambda b, s: (b, layer, 0, 0))
    own = pl.BlockSpec((DEC_SEQ, KB_W), lambda b, s: (ctx_r + b, 0))
    return pl.pallas_call(
        _attn_latb_kernel,
        grid=(DEC_BATCH, DEC_SEQ // HALF),
        in_specs=[pl.BlockSpec((HALF, QB_W), lambda b, s: (ctx_h + 2 * b + s, 0)), own, own, cache, cache],
        out_specs=pl.BlockSpec((HALF, QB_W), lambda b, s: (2 * b + s, 0)),
        out_shape=jax.ShapeDtypeStruct((T_LAT, QB_W), BF16),
        compiler_params=pltpu.CompilerParams(
            dimension_semantics=("arbitrary", "arbitrary"), vmem_limit_bytes=VMEM_LIMIT),
        name="attn_latb",
    )(qb, kb, vb, cache_k, cache_v)


def _natten_window(s, i):
    r = (HALF // GRID_W) * s + i
    rs = min(max(r - WIN_R // 2, 0), GRID_ROWS - WIN_R)
    return r, rs


def _natten_kernel(q_ref, k_ref, v_ref, ck_ref, cv_ref, blk_ref, y_ref, bias_ref):
    s = pl.program_id(1)

    for sv in range(2):
        @pl.when((pl.program_id(2) == 0) & (s == sv))
        def _():
            lo_half, _ = _lane_halves()
            masked = jnp.full((GRID_W, LANES), NEG, F32)
            for hh in range(2):
                for i in range(HALF // GRID_W):
                    r, rs = _natten_window(sv, i)
                    for jp in range(NAT_KEYS // LANES):
                        pair = []
                        for j in (2 * jp, 2 * jp + 1):
                            rk = 4 * sv + j
                            pair.append(blk_ref[0, hh, rk - r + WIN_R - 1] if rs <= rk < rs + WIN_R else masked)
                        bias_ref[hh, i * GRID_W:(i + 1) * GRID_W, jp * LANES:(jp + 1) * LANES] = (
                            jnp.where(lo_half, pair[0], pair[1]))

    start = pl.multiple_of(s * (4 * GRID_W), 4 * GRID_W)
    k = _mha_variants(k_ref[pl.ds(start, NAT_KEYS), :])
    v = _mha_variants(v_ref[pl.ds(start, NAT_KEYS), :])
    ck = _mha_variants(ck_ref[0, 0])
    cv = _mha_variants(cv_ref[0, 0])
    bias = ((bias_ref[0], bias_ref[1]),)
    y = _attend_cols(q_ref[...], k, v, extra_k=ck, extra_v=cv, bias=bias)
    y_ref[...] = y.astype(BF16)


def _natten(layer, qc, kc, vc, cache_k, cache_v, blocks):
    ctx_h = T_CTX // HALF
    ctx_r = T_CTX // DEC_SEQ
    cache = pl.BlockSpec((1, 1, PAST_LEN, LANES), lambda j, s, b: (b, layer, 0, j))
    own = pl.BlockSpec((DEC_SEQ, LANES), lambda j, s, b: (ctx_r + b, j))
    return pl.pallas_call(
        _natten_kernel,
        grid=(C_W // LANES, DEC_SEQ // HALF, DEC_BATCH),
        in_specs=[pl.BlockSpec((HALF, LANES), lambda j, s, b: (ctx_h + 2 * b + s, j)), own, own, cache, cache,
                  pl.BlockSpec((1, 2, 2 * WIN_R - 1, GRID_W, LANES), lambda j, s, b: (j, 0, 0, 0, 0))],
        out_specs=pl.BlockSpec((HALF, LANES), lambda j, s, b: (2 * b + s, j)),
        out_shape=jax.ShapeDtypeStruct((T_LAT, C_W), BF16),
        scratch_shapes=[pltpu.VMEM((2, HALF, NAT_KEYS), F32)],
        compiler_params=pltpu.CompilerParams(
            dimension_semantics=("arbitrary", "arbitrary", "arbitrary"), vmem_limit_bytes=VMEM_LIMIT),
        name="natten",
    )(qc, kc, vc, cache_k, cache_v, blocks)


def _natten_blocks(rpb):
    qcol = np.arange(GRID_W)[:, None]
    kcol = np.arange(GRID_W)[None, :]
    cs = np.clip(qcol - WIN_C // 2, 0, GRID_W - WIN_C)
    col_ok = (kcol >= cs) & (kcol < cs + WIN_C)
    dc = np.clip(kcol - qcol, -(WIN_C - 1), WIN_C - 1) + (WIN_C - 1)
    sel_c = (dc[:, :, None] == np.arange(2 * WIN_C - 1)).astype(np.float32)
    blk = jnp.einsum("hdm,qkm->hdqk", rpb, jnp.asarray(sel_c), precision=lax.Precision.HIGHEST)
    blk = jnp.where(jnp.asarray(col_ok), blk, NEG)
    blk = jnp.concatenate([blk, blk], axis=-1)
    return blk.reshape(H_C // 2, 2, 2 * WIN_R - 1, GRID_W, LANES)


def _route(lg):
    lane = lax.broadcasted_iota(jnp.int32, lg.shape, 1).astype(F32)
    low = jnp.float32(-3.0e38)
    far = jnp.float32(ROUTER_LANES)
    is_g = lane < N_GROUPS
    gmax = jnp.where(is_g, lg, low).max(axis=-1, keepdims=True)
    gsel = jnp.where(is_g & (lg == gmax), lane, far).min(axis=-1, keepdims=True)
    pg_sel = 1.0 / jnp.where(is_g, jnp.exp(lg - gmax), 0.0).sum(axis=-1, keepdims=True)
    e0 = EXPERT_LANE0 + EXPERTS_PER_GROUP * gsel
    in_g = (lane >= e0) & (lane < e0 + EXPERTS_PER_GROUP)
    m1 = jnp.where(in_g, lg, low).max(axis=-1, keepdims=True)
    i1 = jnp.where(in_g & (lg == m1), lane, far).min(axis=-1, keepdims=True)
    rest = in_g & (lane != i1)
    m2 = jnp.where(rest, lg, low).max(axis=-1, keepdims=True)
    i2 = jnp.where(rest & (lg == m2), lane, far).min(axis=-1, keepdims=True)
    t = jnp.exp(m2 - m1)
    ssum = 1.0 + t
    w1 = (1.0 / ssum) * pg_sel
    w2 = (t / ssum) * pg_sel
    return i1 - EXPERT_LANE0, i2 - EXPERT_LANE0, w1, w2


def _rows_to_groups(x, stage_ref, out_ref, n_rows):
    stride = n_rows + 8
    for c in range(GROUP_ROWS):
        stage_ref[c * stride:c * stride + n_rows, :] = x[:, c * LANES:(c + 1) * LANES]
    for j in range(n_rows):
        out_ref[GROUP_ROWS * j:GROUP_ROWS * (j + 1), :] = stage_ref[pl.ds(j, GROUP_ROWS, stride=stride), :]


def _tail_kernel(xc_ref, xl_ref, ya_ref, ybc_ref, ybl_ref, ycc_ref, ycl_ref, w_ref, g1_ref, sh2_ref, sc2_ref,
                 lg_ref, lb_ref, wr_ref, br_ref, x1_ref, h2g_ref, ids_ref, wts_ref, wbf_ref, stage_ref,
                 o_ref, h2_ref):
    @pl.when(pl.program_id(0) == 0)
    def _():
        wbf_ref[...] = w_ref[0].astype(BF16)

    o_ref[...] = (_dot(ya_ref[...], wbf_ref[0:POOL_WIDTH, :])
                  + _dot(_split_rows(ybc_ref, ybl_ref), wbf_ref[POOL_WIDTH:POOL_WIDTH + QB_W, :])
                  + _dot(_split_rows(ycc_ref, ycl_ref), wbf_ref[POOL_WIDTH + QB_W:, :]))
    is_ctx = pl.program_id(0) < N_CTX_TILES

    def norm_chunk(c, carry):
        rows = pl.ds(pl.multiple_of(c * LN_CHUNK, LN_CHUNK), LN_CHUNK)
        x = jnp.where(is_ctx, xc_ref[rows, :], xl_ref[rows, :])
        x1 = _layer_norm(ALPHA * x + g1_ref[0] * o_ref[rows, :], lg_ref[0], lb_ref[0])
        x1_ref[rows, :] = x1
        h2_ref[rows, :] = x1 * (1.0 + sc2_ref[0]) + sh2_ref[0]
        return carry
    lax.fori_loop(0, TT // LN_CHUNK, norm_chunk, 0)
    h2 = h2_ref[...]
    hh, hl = _split_bf16(h2)
    wh, wl = _split_bf16(wr_ref[0])
    lg = _dot(hh, wh) + _dot(hl, wh) + _dot(hh, wl) + br_ref[0]
    i1, i2, w1, w2 = _route(lg)
    lane = lax.broadcasted_iota(jnp.int32, lg.shape, 1)
    ids = jnp.where(lane == 0, i1, jnp.where(lane == 1, i2, 0.0))
    ids_ref[0] = ids.T[0:8, :].astype(jnp.int32)
    wts = jnp.where(lane == 0, w1, jnp.where(lane == 1, w2, 0.0))
    wts_ref[0] = wts.T[0:8, :]
    _rows_to_groups(h2, stage_ref, h2g_ref, TT)


def _tail(layer, x_ctx, x_lat, ya, yb_ctx, yb_lat, yc_ctx, yc_lat, w_out, mod, ln_g, ln_b, wr, br):
    row = lambda w: pl.BlockSpec((TT, w), lambda i: (i, 0))
    return pl.pallas_call(
        _tail_kernel,
        grid=(T_ALL // TT,),
        in_specs=[*_split_row_specs(D_MODEL), row(POOL_WIDTH), *_split_row_specs(QB_W), *_split_row_specs(C_W),
                  _layer_spec(layer, (D_MODEL, D_MODEL)),
                  _mod_spec(layer, 2, TT), _mod_spec(layer, 3, TT), _mod_spec(layer, 4, TT),
                  _layer_spec(layer, (1, D_MODEL)), _layer_spec(layer, (1, D_MODEL)),
                  _layer_spec(layer, (D_MODEL, ROUTER_LANES)), _layer_spec(layer, (1, ROUTER_LANES))],
        out_specs=[row(D_MODEL), pl.BlockSpec((TT * GROUP_ROWS, LANES), lambda i: (i, 0)),
                   pl.BlockSpec((1, 8, TT), lambda i: (i, 0, 0)), pl.BlockSpec((1, 8, TT), lambda i: (i, 0, 0))],
        out_shape=[jax.ShapeDtypeStruct((T_ALL, D_MODEL), F32),
                   jax.ShapeDtypeStruct((T_ALL * GROUP_ROWS, LANES), F32),
                   jax.ShapeDtypeStruct((T_ALL // TT, 8, TT), jnp.int32),
                   jax.ShapeDtypeStruct((T_ALL // TT, 8, TT), F32)],
        scratch_shapes=[pltpu.VMEM((D_MODEL, D_MODEL), BF16),
                        pltpu.VMEM((GROUP_ROWS * (TT + 8), LANES), F32),
                        pltpu.VMEM((TT, D_MODEL), F32), pltpu.VMEM((TT, D_MODEL), F32)],
        compiler_params=pltpu.CompilerParams(dimension_semantics=("arbitrary",), vmem_limit_bytes=VMEM_LIMIT),
        name="tail",
    )(x_ctx, x_lat, ya, yb_ctx, yb_lat, yc_ctx, yc_lat, w_out, mod, mod, mod, ln_g, ln_b, wr, br)


def _plan_kernel(eid_ref, pos_ref, tile_ref):
    e = eid_ref[0]
    rows = e.shape[0]
    r = lax.broadcasted_iota(jnp.int32, (LANES, LANES), 0)
    c = lax.broadcasted_iota(jnp.int32, (LANES, LANES), 1)
    upper = (r <= c).astype(BF16)
    ones = jnp.ones((LANES, LANES), BF16)
    lower = (lax.broadcasted_iota(jnp.int32, (rows, rows), 1)
             < lax.broadcasted_iota(jnp.int32, (rows, rows), 0)).astype(BF16)
    lane = lax.broadcasted_iota(jnp.int32, (1, LANES), 1)
    masks = [e == ex for ex in range(N_EXPERTS)]
    stack = jnp.concatenate([m.astype(BF16) for m in masks], axis=0)
    incl_all = _dot(stack, upper)
    tot_all = _dot(stack, ones)
    pos = jnp.zeros(e.shape, F32)
    base = jnp.zeros((1, LANES), F32)
    valid_end = jnp.zeros((1, LANES), F32)
    pad_end = jnp.zeros((1, LANES), F32)
    for ex, m in enumerate(masks):
        incl = incl_all[ex * rows:(ex + 1) * rows]
        row_tot = tot_all[ex * rows:(ex + 1) * rows]
        row_off = _dot(lower, row_tot.astype(BF16))
        cnt = jnp.sum(row_tot, axis=0, keepdims=True)
        pos = jnp.where(m, base + row_off + incl - 1.0, pos)
        valid_end = jnp.where(lane == ex, base + cnt, valid_end)
        base = base + jnp.ceil(cnt * (1.0 / TMX)) * TMX
        pad_end = jnp.where(lane == ex, base, pad_end)
    pos_ref[0] = pos.astype(jnp.int32)
    sub = lax.broadcasted_iota(jnp.int32, (8, LANES), 0)
    rows8 = lambda v: jnp.broadcast_to(v, (8, LANES))
    table = jnp.where(sub == PLAN_N_TILES, rows8(base * (1.0 / TMX)),
                      jnp.where(sub == PLAN_VALID_END, rows8(valid_end),
                                jnp.where(sub == PLAN_PAD_END, rows8(pad_end), 0.0)))
    tile_ref[0] = table.astype(jnp.int32)


def _plan(eid):
    rows = 2 * T_STREAM // LANES
    return pl.pallas_call(
        _plan_kernel,
        grid=(N_STREAMS,),
        in_specs=[pl.BlockSpec((1, rows, LANES), lambda s: (s, 0, 0))],
        out_specs=[pl.BlockSpec((1, rows, LANES), lambda s: (s, 0, 0)), pl.BlockSpec((1, 8, LANES), lambda s: (s, 0, 0))],
        out_shape=[jax.ShapeDtypeStruct((N_STREAMS, rows, LANES), jnp.int32),
                   jax.ShapeDtypeStruct((N_STREAMS, 8, LANES), jnp.int32)],
        compiler_params=pltpu.CompilerParams(dimension_semantics=("arbitrary",), vmem_limit_bytes=VMEM_LIMIT),
        name="plan",
    )(eid)


def _moe_kernel(pos_ref, wpair_ref, plan_ref, h2g_hbm, wg_ref, wu_ref, wd_ref, y_hbm,
                h2v, yv, off, xg0, xg1, yg0, yg1, wgb, wub, wdb, sem):
    s = pl.program_id(0)
    e = pl.program_id(1)
    n_pairs = 2 * T_STREAM
    spare_off = T_STREAM * GROUP_ROWS
    plan = lambda row, idx: plan_ref[(s * 8 + row) * LANES + idx]
    n_tiles = plan(PLAN_N_TILES, 0)
    last = n_tiles - 1

    xgs = (xg0, xg1)
    ygs = (yg0, yg1)

    def row_offset(pair):
        return pl.multiple_of((pair & (SLOT_PITCH - 1)) << 3, GROUP_ROWS)

    def gather(tile, dst):
        base = tile * TMX
        for j in range(TMX):
            dst[pl.ds(j, GROUP_ROWS, stride=XSTRIDE), :] = h2v[pl.ds(row_offset(off[base + j]), GROUP_ROWS), :]

    def h2_copy():
        return pltpu.make_async_copy(
            h2g_hbm.at[pl.ds(pl.multiple_of(s * (T_STREAM * GROUP_ROWS), 8), T_STREAM * GROUP_ROWS)],
            h2v.at[pl.ds(0, T_STREAM * GROUP_ROWS)], sem.at[0])

    def y_copy(stream):
        return pltpu.make_async_copy(yv.at[pl.ds(0, T_STREAM * GROUP_ROWS)], y_hbm.at[stream], sem.at[1])

    @pl.when(e == 0)
    def _():
        h2_copy().start()
        h2v[spare_off:spare_off + GROUP_ROWS, :] = jnp.zeros((GROUP_ROWS, LANES), F32)

        def pad_expert(ex, carry):
            lo = plan(PLAN_VALID_END, ex)

            def pad_chunk(k, c):
                for u in range(PAD_CHUNK):
                    off[lo + k * PAD_CHUNK + u] = T_STREAM
                return c
            n_chunks = (plan(PLAN_PAD_END, ex) - lo + (PAD_CHUNK - 1)) // PAD_CHUNK
            return lax.fori_loop(0, n_chunks, pad_chunk, carry)
        lax.fori_loop(0, N_EXPERTS, pad_expert, 0)

        def place(i, carry):
            for u in range(PLACE_UNROLL):
                tok = i * PLACE_UNROLL + u
                for slot in range(2):
                    off[pos_ref[s * n_pairs + slot * T_STREAM + tok]] = slot * SLOT_PITCH + tok
            return carry
        lax.fori_loop(0, T_STREAM // PLACE_UNROLL, place, 0)

        @pl.when(s > 0)
        def _():
            y_copy(s - 1).wait()
        yv[...] = jnp.zeros_like(yv)
        yg1[...] = jnp.zeros_like(yg1)
        h2_copy().wait()
        gather(0, xg0)

    t_lo = jnp.where(e == 0, 0, plan(PLAN_PAD_END, jnp.maximum(e - 1, 0))) // TMX
    t_hi = plan(PLAN_PAD_END, e) // TMX

    @pl.when(t_hi > t_lo)
    def _():
        wgb[...] = wg_ref[0, 0].astype(BF16)
        wub[...] = wu_ref[0, 0].astype(BF16)
        wdb[...] = wd_ref[0, 0].astype(BF16)

    def add_tile(tile, done):
        base = tile * TMX
        for j0 in range(0, TMX, RMW_BATCH):
            updates = []
            for j in range(j0, j0 + RMW_BATCH):
                pair = off[base + j]
                dst = row_offset(pair)
                gate = wpair_ref[s * (2 * SLOT_PITCH) + pair]
                row = done[pl.ds(j, GROUP_ROWS, stride=XSTRIDE), :]
                updates.append((dst, yv[pl.ds(dst, GROUP_ROWS), :] + gate * row))
            for dst, val in updates:
                yv[pl.ds(dst, GROUP_ROWS), :] = val

    def step(t, cur, nxt, out, done):
        gather(jnp.minimum(t + 1, last), nxt)
        x = jnp.concatenate([cur[c * XSTRIDE:c * XSTRIDE + TMX, :] for c in range(GROUP_ROWS)], axis=1).astype(BF16)
        hg = _dot(x, wgb[...])
        hu = _dot(x, wub[...])
        act = hg * jax.nn.sigmoid(hg) * hu
        y = _dot(act.astype(BF16), wdb[...])
        for c in range(GROUP_ROWS):
            out[c * XSTRIDE:c * XSTRIDE + TMX, :] = y[:, c * LANES:(c + 1) * LANES]
        add_tile(jnp.maximum(t - 1, 0), done)

    def tile_body(t, carry):
        for par in range(2):
            @pl.when((t & 1) == par)
            def _():
                step(t, xgs[par], xgs[1 - par], ygs[par], ygs[1 - par])
        return carry
    lax.fori_loop(t_lo, t_hi, tile_body, 0)

    @pl.when(e == N_EXPERTS - 1)
    def _():
        for par in range(2):
            @pl.when((last & 1) == par)
            def _():
                add_tile(last, ygs[par])
        y_copy(s).start()

        @pl.when(s == N_STREAMS - 1)
        def _():
            y_copy(s).wait()


def _moe(layer, pos, wpair, plan, h2g, w_gate, w_up, w_down):
    wspec = lambda shape: pl.BlockSpec((1,) + shape, lambda s, e, pos, wpair, plan: (layer, e, 0, 0))
    return pl.pallas_call(
        _moe_kernel,
        grid_spec=pltpu.PrefetchScalarGridSpec(
            num_scalar_prefetch=3,
            grid=(N_STREAMS, N_EXPERTS),
            in_specs=[pl.BlockSpec(memory_space=pl.ANY),
                      wspec((1, D_MODEL, D_EXPERT)), wspec((1, D_MODEL, D_EXPERT)), wspec((1, D_EXPERT, D_MODEL))],
            out_specs=pl.BlockSpec(memory_space=pl.ANY),
            scratch_shapes=[pltpu.VMEM(((T_STREAM + 1) * GROUP_ROWS, LANES), F32),
                            pltpu.VMEM(((T_STREAM + 1) * GROUP_ROWS, LANES), F32),
                            pltpu.SMEM((P_MAX + PAD_CHUNK,), jnp.int32),
                            pltpu.VMEM((GROUP_ROWS * XSTRIDE, LANES), F32),
                            pltpu.VMEM((GROUP_ROWS * XSTRIDE, LANES), F32),
                            pltpu.VMEM((GROUP_ROWS * XSTRIDE, LANES), F32),
                            pltpu.VMEM((GROUP_ROWS * XSTRIDE, LANES), F32),
                            pltpu.VMEM((D_MODEL, D_EXPERT), BF16), pltpu.VMEM((D_MODEL, D_EXPERT), BF16),
                            pltpu.VMEM((D_EXPERT, D_MODEL), BF16),
                            pltpu.SemaphoreType.DMA((2,))]),
        out_shape=jax.ShapeDtypeStruct((N_STREAMS, T_STREAM * GROUP_ROWS, LANES), F32),
        compiler_params=pltpu.CompilerParams(
            dimension_semantics=("arbitrary", "arbitrary"), vmem_limit_bytes=VMEM_LIMIT),
        name="moe",
    )(pos, wpair, plan, h2g, w_gate, w_up, w_down)


def _groups_to_rows(src_ref, stage_ref, n_rows):
    stride = n_rows + 8
    for j in range(n_rows):
        stage_ref[pl.ds(j, GROUP_ROWS, stride=stride), :] = src_ref[GROUP_ROWS * j:GROUP_ROWS * (j + 1), :]
    return jnp.concatenate([stage_ref[c * stride:c * stride + n_rows, :] for c in range(GROUP_ROWS)], axis=1)


def _final_kernel(x1_ref, y_ref, g2_ref, lg_ref, lb_ref, o_ref, stage_ref):
    moe = _groups_to_rows(y_ref, stage_ref, TF)
    o_ref[...] = _layer_norm(ALPHA * x1_ref[...] + g2_ref[0] * moe, lg_ref[0], lb_ref[0])


def _final(layer, stream, x1, y, mod, ln_g, ln_b):
    first = stream * (T_STREAM // TF)
    g2 = pl.BlockSpec(
        (1, 1, D_MODEL), lambda i: ((layer * MOD_ROWS + _mod_row_of_tile(first + i, TF)) * 6 + 5, 0, 0))
    return pl.pallas_call(
        _final_kernel,
        grid=(T_STREAM // TF,),
        in_specs=[pl.BlockSpec((TF, D_MODEL), lambda i: (first + i, 0)),
                  pl.BlockSpec((TF * GROUP_ROWS, LANES), lambda i: (first + i, 0)),
                  g2, _layer_spec(layer, (1, D_MODEL)), _layer_spec(layer, (1, D_MODEL))],
        out_specs=pl.BlockSpec((TF, D_MODEL), lambda i: (i, 0)),
        out_shape=jax.ShapeDtypeStruct((T_STREAM, D_MODEL), F32),
        scratch_shapes=[pltpu.VMEM((GROUP_ROWS * (TF + 8), LANES), F32)],
        compiler_params=pltpu.CompilerParams(dimension_semantics=("arbitrary",), vmem_limit_bytes=VMEM_LIMIT),
        name="final",
    )(x1, y, mod, ln_g, ln_b)


def _rope_tables():
    t = np.arange(DEC_SEQ)
    pos = np.stack([t // GRID_W, t % GRID_W], axis=1).astype(np.float32)
    nf = HEAD_DIM // 4
    inv = jnp.asarray(ROPE_THETA, F32) ** (-jnp.arange(nf, dtype=F32) / nf)
    d = np.arange(LANES) % HEAD_DIM
    which = d // (HEAD_DIM // 2)
    ang = jnp.asarray(pos)[:, which] * inv[d % nf][None, :]
    sign = np.where((d % 32) < 16, -1.0, 1.0).astype(np.float32)
    cos = jnp.concatenate([jnp.cos(ang), jnp.ones((TT, LANES), F32)], axis=0)
    sin = jnp.concatenate([jnp.sin(ang) * sign[None, :], jnp.zeros((TT, LANES), F32)], axis=0)
    return cos, sin


def _block_ones():
    h = np.arange(QB_W) // HEAD_DIM
    return jnp.asarray((h[:, None] == h[None, :]).astype(np.float32), dtype=BF16)


def _pool_block_diag(pool_w):
    out = jnp.zeros((POOL_WIDTH, POOL_WIDTH), F32)
    for g in range(4):
        out = out.at[64 * g:64 * g + 64, 64 * g:64 * g + 64].set(pool_w[g])
    return out


def kernel(x_prompt, x_sample, cache_b_k, cache_b_v, cache_c_k, cache_c_v, c, c_ctx, w_ada, b_ada, w_in, w_out,
           pool_w, pool_scale, q_norm, k_norm, rpb, ln1_g, ln1_b, ln2_g, ln2_b, router_g, router_g_b, router_e,
           router_e_b, w_gate, w_up, w_down):
    x_ctx = x_prompt.reshape(T_CTX, D_MODEL)
    x_lat = x_sample.reshape(T_LAT, D_MODEL)
    c8 = jnp.concatenate([c, c_ctx[None], jnp.zeros((MOD_ROWS - DEC_BATCH - 1, D_MODEL), F32)], axis=0)
    mod = _adaln(c8, w_ada, b_ada).reshape(DEPTH * MOD_ROWS * 6, 1, D_MODEL)

    cos_t, sin_t = _rope_tables()
    ones_bd = _block_ones()
    cbk = cache_b_k.reshape(DEC_BATCH, DEPTH, PAST_LEN, KB_W)
    cbv = cache_b_v.reshape(DEC_BATCH, DEPTH, PAST_LEN, KB_W)
    cck = cache_c_k.reshape(DEC_BATCH, DEPTH, PAST_LEN, C_W)
    ccv = cache_c_v.reshape(DEC_BATCH, DEPTH, PAST_LEN, C_W)
    pad = jnp.zeros((DEPTH, D_MODEL, ROUTER_LANES - N_GROUPS - N_EXPERTS), F32)
    wr = jnp.concatenate([router_g, router_e, pad], axis=2)
    br = jnp.concatenate([router_g_b, router_e_b, pad[:, 0]], axis=1)[:, None]
    per_layer = lambda p: p[:, None]

    caches = ()
    for l in range(DEPTH):
        qn = jnp.tile(q_norm[l], H_B)[None]
        kn = jnp.tile(k_norm[l], KV_B)[None]
        a, qb, kb, vb, qc, kc, vc = _proj(l, x_ctx, x_lat, mod, w_in, ones_bd, qn, kn, cos_t, sin_t)
        ya = _pool(a, _pool_block_diag(pool_w[l]), pool_scale[l][None])
        yb_ctx, yc_ctx, *caches = _attn_ctx(l, qb, kb, vb, qc, kc, vc, caches)
        yb_lat = _attn_latb(l, qb, kb, vb, cbk, cbv)
        yc_lat = _natten(l, qc, kc, vc, cck, ccv, _natten_blocks(rpb[l]))
        x1, h2g, ids, wts = _tail(l, x_ctx, x_lat, ya, yb_ctx, yb_lat, yc_ctx, yc_lat, w_out, mod,
                                  per_layer(ln1_g), per_layer(ln1_b), wr, br)
        pairs = lambda a: a.reshape(N_STREAMS, T_STREAM // TT, 8, TT)[:, :, :2, :].transpose(0, 2, 1, 3)
        pos, tiles = _plan(pairs(ids).reshape(N_STREAMS, 2 * T_STREAM // LANES, LANES))
        gates = jnp.pad(pairs(wts).reshape(N_STREAMS, 2, T_STREAM), ((0, 0), (0, 0), (0, SLOT_PITCH - T_STREAM)))
        y = _moe(l, pos.reshape(-1), gates.reshape(-1), tiles.reshape(-1), h2g, w_gate, w_up, w_down)
        y = y.reshape(T_ALL * GROUP_ROWS, LANES)
        x_ctx = _final(l, 0, x1, y, mod, per_layer(ln2_g), per_layer(ln2_b))
        x_lat = _final(l, 1, x1, y, mod, per_layer(ln2_g), per_layer(ln2_b))

    y_prompt = x_ctx.reshape(BATCH, SEQ, D_MODEL)
    y_sample = x_lat.reshape(DEC_BATCH, DEC_SEQ, D_MODEL)
    heads_last = lambda t, h: t.reshape(BATCH, DEPTH, h, HEAD_DIM, SEQ).transpose(0, 1, 4, 2, 3)
    new_bk, new_bv, new_ck, new_cv = caches
    return (y_prompt, y_sample, heads_last(new_bk, KV_B), heads_last(new_bv, KV_B),
            heads_last(new_ck, H_C), heads_last(new_cv, H_C))
```
